```python
import jax, jax.numpy as jnp
from jax import lax
import numpy as np

D_MODEL = 1024
BATCH = 8
SEQ = 8192
DEPTH = 2

BLOCK = 128
NORM_EPS = 1e-6
SWA_WINDOW = 128
SWA_HEADS = 4
SWA_KV_HEADS = 2
SWA_HEAD_DIM = 64
CONV_WIDTH = 256
CONV_K = 3
MLA_HEADS = 4
MLA_Q_RANK = 256
MLA_KV_RANK = 128
MLA_NOPE_DIM = 64
MLA_ROPE_DIM = 32
MLA_V_DIM = 64
ROPE_THETA = 10000.0
SB_HEADS = 4
SB_HEAD_DIM = 64
GROUP_WIDTH = 256
N_GROUPS = 4
D_MIX = GROUP_WIDTH * N_GROUPS

A_Q = SWA_HEADS * SWA_HEAD_DIM
A_KV = SWA_KV_HEADS * SWA_HEAD_DIM
SB_W = SB_HEADS * SB_HEAD_DIM
IN_SIZES = (A_Q, A_KV, A_KV,
            CONV_WIDTH, CONV_WIDTH, CONV_WIDTH,
            MLA_Q_RANK, MLA_KV_RANK, MLA_ROPE_DIM,
            SB_W, SB_W, SB_W,
            D_MIX)
D_IN = int(sum(IN_SIZES))
SPLIT_IDX = [int(v) for v in np.cumsum(IN_SIZES)[:-1]]

kernel_name = "hymba_style_four_group_hybrid"


def rmsnorm(x, g):
    x32 = x.astype(jnp.float32)
    y = x32 * lax.rsqrt(jnp.mean(x32 * x32, axis=-1, keepdims=True) + NORM_EPS)
    return (y * g.astype(jnp.float32)).astype(x.dtype)


def to_blocks(t):
    b, s = t.shape[:2]
    t = t.reshape((b, s // BLOCK, BLOCK) + t.shape[2:])
    return jnp.moveaxis(t, 1, 0)


def from_blocks(t):
    t = jnp.moveaxis(t, 0, 1)
    return t.reshape((t.shape[0], t.shape[1] * t.shape[2]) + t.shape[3:])


def rope(x, pos):
    half = x.shape[-1] // 2
    freqs = ROPE_THETA ** (-jnp.arange(half, dtype=jnp.float32) / half)
    ang = pos.astype(jnp.float32)[..., None] * freqs
    ang = ang.reshape(ang.shape[:2] + (1,) * (x.ndim - 3) + (half,))
    cos, sin = jnp.cos(ang), jnp.sin(ang)
    x32 = x.astype(jnp.float32)
    x1, x2 = x32[..., :half], x32[..., half:]
    return jnp.concatenate([x1 * cos - x2 * sin, x1 * sin + x2 * cos], axis=-1).astype(x.dtype)


def swa_sink_attention(q, k, v, sinks):
    b, s, h, d = q.shape
    kvh = k.shape[2]
    g = h // kvh
    nb = s // BLOCK
    qb = q.reshape(b, nb, BLOCK, kvh, g, d).astype(jnp.float32)
    kb = k.reshape(b, nb, BLOCK, kvh, d).astype(jnp.float32)
    vb = v.reshape(b, nb, BLOCK, kvh, d).astype(jnp.float32)
    prev = lambda t: jnp.concatenate([jnp.zeros_like(t[:, :1]), t[:, :-1]], axis=1)
    kk = jnp.concatenate([prev(kb), kb], axis=2)
    vv = jnp.concatenate([prev(vb), vb], axis=2)
    scores = jnp.einsum('bnqhgd,bnkhd->bnhgqk', qb, kk) * (d ** -0.5)
    qi = jnp.arange(BLOCK)[:, None]
    kj = jnp.arange(2 * BLOCK)[None, :]
    diff = qi + BLOCK - kj
    blk = jnp.arange(nb)[:, None, None]
    valid = (diff >= 0) & (diff < SWA_WINDOW) & (blk * BLOCK + kj - BLOCK >= 0)
    scores = jnp.where(valid[None, :, None, None], scores, -jnp.inf)
    sink = jnp.broadcast_to(sinks.astype(jnp.float32).reshape(1, 1, kvh, g, 1, 1), scores.shape[:-1] + (1,))
    probs = jax.nn.softmax(jnp.concatenate([scores, sink], axis=-1), axis=-1)[..., :-1]
    out = jnp.einsum('bnhgqk,bnkhd->bnqhgd', probs, vv)
    return out.reshape(b, s, h * d).astype(q.dtype)


def short_gated_conv(bg, cg, xin, conv_w, conv_b):
    u = cg * xin
    y = lax.conv_general_dilated(u, conv_w[:, None, :], window_strides=(1,),
                                 padding=[(CONV_K - 1, 0)],
                                 dimension_numbers=('NWC', 'WIO', 'NWC'),
                                 feature_group_count=u.shape[-1])
    return bg * (y + conv_b)


def causal_softmax_attention(q, k, v):
    b, s, h, dk = q.shape
    scale = dk ** -0.5
    kf = k.astype(jnp.float32)
    vf = v.astype(jnp.float32)
    kpos = jnp.arange(s)

    def step(args):
        qblk, i = args
        sc = jnp.einsum('bqhd,bkhd->bhqk', qblk.astype(jnp.float32), kf) * scale
        qpos = i * BLOCK + jnp.arange(BLOCK)
        sc = jnp.where(kpos[None, :] <= qpos[:, None], sc, -jnp.inf)
        p = jax.nn.softmax(sc, axis=-1)
        return jnp.einsum('bhqk,bkhd->bqhd', p, vf)

    out = lax.map(step, (to_blocks(q), jnp.arange(s // BLOCK)))
    return from_blocks(out).reshape(b, s, -1).astype(q.dtype)


def mla(cq, ckv, kr, pos, g_q, w_uq, g_kv, w_ukv):
    b, s, _ = cq.shape
    q = (rmsnorm(cq, g_q) @ w_uq).reshape(b, s, MLA_HEADS, MLA_NOPE_DIM + MLA_ROPE_DIM)
    q = jnp.concatenate([q[..., :MLA_NOPE_DIM], rope(q[..., MLA_NOPE_DIM:], pos)], axis=-1)
    kv = (rmsnorm(ckv, g_kv) @ w_ukv).reshape(b, s, MLA_HEADS, MLA_NOPE_DIM + MLA_V_DIM)
    k_nope, v = kv[..., :MLA_NOPE_DIM], kv[..., MLA_NOPE_DIM:]
    k_rope = jnp.broadcast_to(rope(kr, pos)[:, :, None, :], (b, s, MLA_HEADS, MLA_ROPE_DIM))
    k = jnp.concatenate([k_nope, k_rope], axis=-1)
    return causal_softmax_attention(q, k, v)


def stick_breaking_attention(q, k, v):
    b, s, h, d = q.shape
    scale = d ** -0.5
    kf = k.astype(jnp.float32)
    vf = v.astype(jnp.float32)
    kpos = jnp.arange(s)

    def step(args):
        qblk, i = args
        z = jnp.einsum('bqhd,bkhd->bhqk', qblk.astype(jnp.float32), kf) * scale
        qpos = i * BLOCK + jnp.arange(BLOCK)
        mask = kpos[None, :] < qpos[:, None]
        log_keep = jnp.where(mask, jax.nn.log_sigmoid(-z), 0.0)
        after = lax.cumsum(log_keep, axis=3, reverse=True) - log_keep
        a = jnp.where(mask, jnp.exp(jax.nn.log_sigmoid(z) + after), 0.0)
        return jnp.einsum('bhqk,bkhd->bqhd', a, vf)

    out = lax.map(step, (to_blocks(q), jnp.arange(s // BLOCK)))
    return from_blocks(out).reshape(b, s, -1).astype(q.dtype)


def hybrid_layer(x, pos, g_pre, w_in, sinks, conv_w, conv_b, g_cq, w_uq, g_ckv, w_ukv, g_grp, w_out, g_post):
    b, s, _ = x.shape
    h = rmsnorm(x, g_pre) @ w_in
    (a_q, a_k, a_v, b_b, b_c, b_x, c_q, c_kv, c_kr, d_q, d_k, d_v, gate) = jnp.split(h, SPLIT_IDX, axis=-1)
    ya = swa_sink_attention(a_q.reshape(b, s, SWA_HEADS, SWA_HEAD_DIM),
                            a_k.reshape(b, s, SWA_KV_HEADS, SWA_HEAD_DIM),
                            a_v.reshape(b, s, SWA_KV_HEADS, SWA_HEAD_DIM), sinks)
    yb = short_gated_conv(b_b, b_c, b_x, conv_w, conv_b)
    yc = mla(c_q, c_kv, c_kr, pos, g_cq, w_uq, g_ckv, w_ukv)
    yd = stick_breaking_attention(d_q.reshape(b, s, SB_HEADS, SB_HEAD_DIM),
                                  d_k.reshape(b, s, SB_HEADS, SB_HEAD_DIM),
                                  d_v.reshape(b, s, SB_HEADS, SB_HEAD_DIM))
    y = jnp.stack([ya, yb, yc, yd], axis=2)
    y = rmsnorm(y, g_grp.reshape(N_GROUPS, GROUP_WIDTH)).reshape(b, s, D_MIX)
    y = y * jax.nn.silu(gate)
    return x + rmsnorm(y @ w_out, g_post)


def _fwd_setup_inputs(seed: int = 0) -> dict:
    key = jax.random.key(seed)
    ks = jax.random.split(key, 16)
    f32 = jnp.float32
    nrm = lambda k, shape, scale: jax.random.normal(k, shape, f32) * scale
    gain = lambda k, shape: 1.0 + 0.02 * jax.random.normal(k, shape, f32)
    x = jax.random.normal(ks[0], (BATCH, SEQ, D_MODEL), f32)
    positions = jnp.broadcast_to(jnp.arange(SEQ, dtype=jnp.int32)[None, :], (BATCH, SEQ))
    return {
        "x": x,
        "positions": positions,
        "norm_pre": gain(ks[1], (DEPTH, D_MODEL)),
        "w_in": nrm(ks[2], (DEPTH, D_MODEL, D_IN), D_MODEL ** -0.5),
        "attn_sinks": nrm(ks[3], (DEPTH, SWA_HEADS), 0.5),
        "conv_w": nrm(ks[4], (DEPTH, CONV_K, CONV_WIDTH), CONV_K ** -0.5),
        "conv_b": nrm(ks[5], (DEPTH, CONV_WIDTH), 0.01),
        "mla_q_norm": gain(ks[6], (DEPTH, MLA_Q_RANK)),
        "mla_w_uq": nrm(ks[7], (DEPTH, MLA_Q_RANK, MLA_HEADS * (MLA_NOPE_DIM + MLA_ROPE_DIM)), MLA_Q_RANK ** -0.5),
        "mla_kv_norm": gain(ks[8], (DEPTH, MLA_KV_RANK)),
        "mla_w_ukv": nrm(ks[9], (DEPTH, MLA_KV_RANK, MLA_HEADS * (MLA_NOPE_DIM + MLA_V_DIM)), MLA_KV_RANK ** -0.5),
        "group_norm": gain(ks[10], (DEPTH, D_MIX)),
        "w_out": nrm(ks[11], (DEPTH, D_MIX, D_MODEL), D_MIX ** -0.5),
        "norm_post": gain(ks[12], (DEPTH, D_MODEL)),
    }


def _fwd_reference(x, positions, norm_pre, w_in, attn_sinks, conv_w, conv_b, mla_q_norm, mla_w_uq,
              mla_kv_norm, mla_w_ukv, group_norm, w_out, norm_post):
    for l in range(DEPTH):
        x = hybrid_layer(x, positions, norm_pre[l], w_in[l], attn_sinks[l], conv_w[l], conv_b[l],
                         mla_q_norm[l], mla_w_uq[l], mla_kv_norm[l], mla_w_ukv[l],
                         group_norm[l], w_out[l], norm_post[l])
    return x


import jax as _jax
import jax.numpy as _jnp

TWIN_FORMAT = 'train_step'
FWD_PARAMS = ['x', 'positions', 'norm_pre', 'w_in', 'attn_sinks', 'conv_w', 'conv_b', 'mla_q_norm', 'mla_w_uq', 'mla_kv_norm', 'mla_w_ukv', 'group_norm', 'w_out', 'norm_post']
TWIN_WEIGHTS = ['norm_pre', 'w_in', 'attn_sinks', 'conv_w', 'conv_b', 'mla_q_norm', 'mla_w_uq', 'mla_kv_norm', 'mla_w_ukv', 'group_norm', 'w_out', 'norm_post']
TWIN_DIFF_INPUT = 'x'
TWIN_INPUTS = ['x', 'positions', 'norm_pre', 'w_in', 'attn_sinks', 'conv_w', 'conv_b', 'mla_q_norm', 'mla_w_uq', 'mla_kv_norm', 'mla_w_ukv', 'group_norm', 'w_out', 'norm_post', 'loss_target', 'm_norm_pre', 'm_w_in', 'm_attn_sinks', 'm_conv_w', 'm_conv_b', 'm_mla_q_norm', 'm_mla_w_uq', 'm_mla_kv_norm', 'm_mla_w_ukv', 'm_group_norm', 'm_w_out', 'm_norm_post', 'v_norm_pre', 'v_w_in', 'v_attn_sinks', 'v_conv_w', 'v_conv_b', 'v_mla_q_norm', 'v_mla_w_uq', 'v_mla_kv_norm', 'v_mla_w_ukv', 'v_group_norm', 'v_w_out', 'v_norm_post']
TWIN_OUTPUTS = ['loss', 'grad_x', 'grad_norm_pre', 'grad_w_in', 'grad_attn_sinks', 'grad_conv_w', 'grad_conv_b', 'grad_mla_q_norm', 'grad_mla_w_uq', 'grad_mla_kv_norm', 'grad_mla_w_ukv', 'grad_group_norm', 'grad_w_out', 'grad_norm_post', 'delta_norm_pre', 'delta_w_in', 'delta_attn_sinks', 'delta_conv_w', 'delta_conv_b', 'delta_mla_q_norm', 'delta_mla_w_uq', 'delta_mla_kv_norm', 'delta_mla_w_ukv', 'delta_group_norm', 'delta_w_out', 'delta_norm_post', 'new_m_norm_pre', 'new_m_w_in', 'new_m_attn_sinks', 'new_m_conv_w', 'new_m_conv_b', 'new_m_mla_q_norm', 'new_m_mla_w_uq', 'new_m_mla_kv_norm', 'new_m_mla_w_ukv', 'new_m_group_norm', 'new_m_w_out', 'new_m_norm_post', 'new_v_norm_pre', 'new_v_w_in', 'new_v_attn_sinks', 'new_v_conv_w', 'new_v_conv_b', 'new_v_mla_q_norm', 'new_v_mla_w_uq', 'new_v_mla_kv_norm', 'new_v_mla_w_ukv', 'new_v_group_norm', 'new_v_w_out', 'new_v_norm_post']
TWIN_LEAF_KINDS = {'loss': 'loss', 'grad_x': 'grad_x', 'grad_norm_pre': 'grad_w', 'grad_w_in': 'grad_w', 'grad_attn_sinks': 'grad_w', 'grad_conv_w': 'grad_w', 'grad_conv_b': 'grad_w', 'grad_mla_q_norm': 'grad_w', 'grad_mla_w_uq': 'grad_w', 'grad_mla_kv_norm': 'grad_w', 'grad_mla_w_ukv': 'grad_w', 'grad_group_norm': 'grad_w', 'grad_w_out': 'grad_w', 'grad_norm_post': 'grad_w', 'delta_norm_pre': 'delta_w', 'delta_w_in': 'delta_w', 'delta_attn_sinks': 'delta_w', 'delta_conv_w': 'delta_w', 'delta_conv_b': 'delta_w', 'delta_mla_q_norm': 'delta_w', 'delta_mla_w_uq': 'delta_w', 'delta_mla_kv_norm': 'delta_w', 'delta_mla_w_ukv': 'delta_w', 'delta_group_norm': 'delta_w', 'delta_w_out': 'delta_w', 'delta_norm_post': 'delta_w', 'new_m_norm_pre': 'new_m', 'new_m_w_in': 'new_m', 'new_m_attn_sinks': 'new_m', 'new_m_conv_w': 'new_m', 'new_m_conv_b': 'new_m', 'new_m_mla_q_norm': 'new_m', 'new_m_mla_w_uq': 'new_m', 'new_m_mla_kv_norm': 'new_m', 'new_m_mla_w_ukv': 'new_m', 'new_m_group_norm': 'new_m', 'new_m_w_out': 'new_m', 'new_m_norm_post': 'new_m', 'new_v_norm_pre': 'new_v', 'new_v_w_in': 'new_v', 'new_v_attn_sinks': 'new_v', 'new_v_conv_w': 'new_v', 'new_v_conv_b': 'new_v', 'new_v_mla_q_norm': 'new_v', 'new_v_mla_w_uq': 'new_v', 'new_v_mla_kv_norm': 'new_v', 'new_v_mla_w_ukv': 'new_v', 'new_v_group_norm': 'new_v', 'new_v_w_out': 'new_v', 'new_v_norm_post': 'new_v'}


def _forward(args):
    return _fwd_reference(*[args[k] for k in FWD_PARAMS])


def _output_shape():
    def fwd():
        inp = _fwd_setup_inputs(0)
        return _fwd_reference(*[inp[k] for k in FWD_PARAMS])
    out = _jax.eval_shape(fwd)
    return out.shape, out.dtype

N_MICROBATCH = 1
ADAM_LR = 0.001
ADAM_B1 = 0.9
ADAM_B2 = 0.999
ADAM_EPS = 1e-08
ADAM_WD = 0.01
ADAM_STEP = 10
PER_EXAMPLE_BATCH_AXIS = {'x': 0, 'positions': 0, 'loss_target': 0}
SHARED_INPUTS = []
_WEIGHT_DTYPES = {'norm_pre': _jnp.float32, 'w_in': _jnp.float32, 'attn_sinks': _jnp.float32, 'conv_w': _jnp.float32, 'conv_b': _jnp.float32, 'mla_q_norm': _jnp.float32, 'mla_w_uq': _jnp.float32, 'mla_kv_norm': _jnp.float32, 'mla_w_ukv': _jnp.float32, 'group_norm': _jnp.float32, 'w_out': _jnp.float32, 'norm_post': _jnp.float32}
MOMENT_SCALE = {'norm_pre': 1.144623e+00, 'w_in': 5.979124e-01, 'attn_sinks': 1.472330e-01, 'conv_w': 5.764385e-01, 'conv_b': 8.048784e-01, 'mla_q_norm': 6.873908e-01, 'mla_w_uq': 5.497678e-01, 'mla_kv_norm': 1.943679e+00, 'mla_w_ukv': 7.222384e-01, 'group_norm': 6.575381e-01, 'w_out': 6.598440e-01, 'norm_post': 6.408082e+01}


def _to_microbatches(a, axis):
    t = _jnp.moveaxis(a, axis, 0)
    t = t.reshape((N_MICROBATCH, t.shape[0] // N_MICROBATCH) + t.shape[1:])
    return _jnp.moveaxis(t, 1, axis + 1)


def setup_inputs(seed: int = 0) -> dict:
    inp = _fwd_setup_inputs(seed)
    key = _jax.random.fold_in(_jax.random.key(seed), 7919)
    shape, _ = _output_shape()
    out = dict(inp)
    out["loss_target"] = _jax.random.normal(_jax.random.fold_in(key, 0), shape, _jnp.float32)
    for i, name in enumerate(TWIN_WEIGHTS):
        w = inp[name].astype(_jnp.float32)
        if MOMENT_SCALE is None:
            s = _jnp.sqrt(_jnp.mean(_jnp.square(w)) + 1e-30)
        else:
            s = MOMENT_SCALE[name]
        km, kv = _jax.random.split(_jax.random.fold_in(key, i + 1))
        out[name] = w
        out["m_" + name] = s * _jax.random.normal(km, w.shape, _jnp.float32)
        out["v_" + name] = (s * s) * _jax.random.uniform(kv, w.shape, _jnp.float32, 0.5, 1.5)
    if N_MICROBATCH > 1:
        for name, axis in PER_EXAMPLE_BATCH_AXIS.items():
            out[name] = _to_microbatches(out[name], axis)
    return {'x': out['x'], 'positions': out['positions'], 'norm_pre': out['norm_pre'], 'w_in': out['w_in'], 'attn_sinks': out['attn_sinks'], 'conv_w': out['conv_w'], 'conv_b': out['conv_b'], 'mla_q_norm': out['mla_q_norm'], 'mla_w_uq': out['mla_w_uq'], 'mla_kv_norm': out['mla_kv_norm'], 'mla_w_ukv': out['mla_w_ukv'], 'group_norm': out['group_norm'], 'w_out': out['w_out'], 'norm_post': out['norm_post'], 'loss_target': out['loss_target'], 'm_norm_pre': out['m_norm_pre'], 'm_w_in': out['m_w_in'], 'm_attn_sinks': out['m_attn_sinks'], 'm_conv_w': out['m_conv_w'], 'm_conv_b': out['m_conv_b'], 'm_mla_q_norm': out['m_mla_q_norm'], 'm_mla_w_uq': out['m_mla_w_uq'], 'm_mla_kv_norm': out['m_mla_kv_norm'], 'm_mla_w_ukv': out['m_mla_w_ukv'], 'm_group_norm': out['m_group_norm'], 'm_w_out': out['m_w_out'], 'm_norm_post': out['m_norm_post'], 'v_norm_pre': out['v_norm_pre'], 'v_w_in': out['v_w_in'], 'v_attn_sinks': out['v_attn_sinks'], 'v_conv_w': out['v_conv_w'], 'v_conv_b': out['v_conv_b'], 'v_mla_q_norm': out['v_mla_q_norm'], 'v_mla_w_uq': out['v_mla_w_uq'], 'v_mla_kv_norm': out['v_mla_kv_norm'], 'v_mla_w_ukv': out['v_mla_w_ukv'], 'v_group_norm': out['v_group_norm'], 'v_w_out': out['v_w_out'], 'v_norm_post': out['v_norm_post']}


def _loss(weights, diff, rest, loss_target):
    with _jax.named_scope("forward"):
        args = {**rest, TWIN_DIFF_INPUT: diff, **{k: w.astype(_WEIGHT_DTYPES[k]) for k, w in weights.items()}}
        y = _forward(args)
    with _jax.named_scope("loss_head"):
        err = _jnp.square(y.astype(_jnp.float32) - loss_target)
        return 0.5 * _jnp.sum(_jnp.mean(err, axis=-1)) if err.ndim else 0.5 * err


def _adamw(w, g, m, v):
    m = ADAM_B1 * m + (1.0 - ADAM_B1) * g
    v = ADAM_B2 * v + (1.0 - ADAM_B2) * _jnp.square(g)
    m_hat = m / (1.0 - ADAM_B1 ** ADAM_STEP)
    v_hat = v / (1.0 - ADAM_B2 ** ADAM_STEP)
    delta = -ADAM_LR * (m_hat / (_jnp.sqrt(v_hat) + ADAM_EPS) + ADAM_WD * w)
    return delta, m, v


def reference(x, positions, norm_pre, w_in, attn_sinks, conv_w, conv_b, mla_q_norm, mla_w_uq, mla_kv_norm, mla_w_ukv, group_norm, w_out, norm_post, loss_target, m_norm_pre, m_w_in, m_attn_sinks, m_conv_w, m_conv_b, m_mla_q_norm, m_mla_w_uq, m_mla_kv_norm, m_mla_w_ukv, m_group_norm, m_w_out, m_norm_post, v_norm_pre, v_w_in, v_attn_sinks, v_conv_w, v_conv_b, v_mla_q_norm, v_mla_w_uq, v_mla_kv_norm, v_mla_w_ukv, v_group_norm, v_w_out, v_norm_post):
    given = dict(x=x, positions=positions, norm_pre=norm_pre, w_in=w_in, attn_sinks=attn_sinks, conv_w=conv_w, conv_b=conv_b, mla_q_norm=mla_q_norm, mla_w_uq=mla_w_uq, mla_kv_norm=mla_kv_norm, mla_w_ukv=mla_w_ukv, group_norm=group_norm, w_out=w_out, norm_post=norm_post, loss_target=loss_target, m_norm_pre=m_norm_pre, m_w_in=m_w_in, m_attn_sinks=m_attn_sinks, m_conv_w=m_conv_w, m_conv_b=m_conv_b, m_mla_q_norm=m_mla_q_norm, m_mla_w_uq=m_mla_w_uq, m_mla_kv_norm=m_mla_kv_norm, m_mla_w_ukv=m_mla_w_ukv, m_group_norm=m_group_norm, m_w_out=m_w_out, m_norm_post=m_norm_post, v_norm_pre=v_norm_pre, v_w_in=v_w_in, v_attn_sinks=v_attn_sinks, v_conv_w=v_conv_w, v_conv_b=v_conv_b, v_mla_q_norm=v_mla_q_norm, v_mla_w_uq=v_mla_w_uq, v_mla_kv_norm=v_mla_kv_norm, v_mla_w_ukv=v_mla_w_ukv, v_group_norm=v_group_norm, v_w_out=v_w_out, v_norm_post=v_norm_post)
    weights = {n: given[n] for n in TWIN_WEIGHTS}
    shared = {n: given[n] for n in SHARED_INPUTS}
    per_example = {n: given[n] for n in ['x', 'positions']}
    grad_fn = _jax.value_and_grad(_loss, argnums=(0, 1))

    def one_microbatch(ex, loss_target):
        ex = dict(ex)
        diff = ex.pop(TWIN_DIFF_INPUT)
        return grad_fn(weights, diff, {**shared, **ex}, loss_target)

    if N_MICROBATCH == 1:
        loss, (grad_w, grad_x) = one_microbatch(per_example, given["loss_target"])
    else:
        def body(carry, xs):
            loss_sum, grad_sum = carry
            l_k, (gw_k, gx_k) = one_microbatch(xs[0], xs[1])
            with _jax.named_scope("update"):
                return (loss_sum + l_k, _jax.tree.map(_jnp.add, grad_sum, gw_k)), gx_k

        init = (_jnp.zeros((), _jnp.float32), _jax.tree.map(_jnp.zeros_like, weights))
        (loss, grad_w), grad_x = _jax.lax.scan(body, init, (per_example, given["loss_target"]))
    with _jax.named_scope("update"):
        delta_w, new_m, new_v = {}, {}, {}
        for n in TWIN_WEIGHTS:
            delta_w[n], new_m[n], new_v[n] = _adamw(weights[n], grad_w[n], given["m_" + n], given["v_" + n])
    return (loss, grad_x, *[grad_w[n] for n in TWIN_WEIGHTS], *[delta_w[n] for n in TWIN_WEIGHTS],
            *[new_m[n] for n in TWIN_WEIGHTS], *[new_v[n] for n in TWIN_WEIGHTS])
```

```python
import functools

import jax
import jax.numpy as jnp
import numpy as np
from jax import lax
from jax.experimental import pallas as pl
from jax.experimental.pallas import tpu as pltpu

F32 = jnp.float32
BF = jnp.bfloat16
MESH = pl.DeviceIdType.MESH

D_MODEL = 1024
DEPTH = 2
EPS = 1e-6
N_DEV = 8
VMEM_LIMIT = 56 * 1024 * 1024
NEG = -1e30
MLA_SCALE = 96.0 ** -0.5
SB_SCALE = 0.125

NP = 3840
C_GATE = 0
C_AQ = 1024
C_AK = 1280
C_AV = 1408
C_BB = 1536
C_BC = 1792
C_BX = 2048
C_CQ = 2304
C_CKV = 2560
C_CKR = 2688
C_CKRS = 2816
C_DQ = 2944
C_DK = 3200
C_DV = 3456
C_END = 3712

def _swap32(a):
    return jnp.concatenate([a[:, 16:32], a[:, 0:16]], axis=1)

ADAM_LR, ADAM_B1, ADAM_B2, ADAM_EPS, ADAM_WD, ADAM_STEP = 0.001, 0.9, 0.999, 1e-08, 0.01, 10


def _dot(a, b):
    return jnp.dot(a, b, preferred_element_type=F32)


def _dot_nt(a, b):
    return lax.dot_general(a, b, (((1,), (1,)), ((), ())), preferred_element_type=F32)


def _dot_tn(a, b):
    return lax.dot_general(a, b, (((0,), (0,)), ((), ())), preferred_element_type=F32)


def _params(n_grid):
    return pltpu.CompilerParams(dimension_semantics=("arbitrary",) * n_grid, vmem_limit_bytes=VMEM_LIMIT)


def _rms_fwd(x, g):
    r = lax.rsqrt(jnp.mean(x * x, axis=-1, keepdims=True) + EPS)
    return (x * r) * g, r


def _rms_bwd(x, g, r, dy, width=None):
    n = x.shape[-1] if width is None else width
    u = dy * g
    dx = r * u - x * (r * r * r) * (jnp.sum(x * u, axis=-1, keepdims=True) / n)
    return dx, dy * (x * r)


def _iota(shape, axis):
    return lax.broadcasted_iota(jnp.int32, shape, axis)


def _inproj_fwd(x, g, wp):
    T = x.shape[0]
    tm = 256

    def body(x_ref, g_ref, w_ref, h32_ref, h16_ref, xn_ref):
        xn, _ = _rms_fwd(x_ref[...], g_ref[...])
        xn = xn.astype(BF)
        xn_ref[...] = xn
        h = _dot(xn, w_ref[...])
        h32_ref[...] = h
        h16_ref[...] = h.astype(BF)

    return pl.pallas_call(
        body, name="inproj_fwd", grid=(T // tm,),
        in_specs=[pl.BlockSpec((tm, D_MODEL), lambda n: (n, 0)),
                  pl.BlockSpec((1, D_MODEL), lambda n: (0, 0)),
                  pl.BlockSpec((D_MODEL, NP), lambda n: (0, 0))],
        out_specs=[pl.BlockSpec((tm, NP), lambda n: (n, 0)),
                   pl.BlockSpec((tm, NP), lambda n: (n, 0)),
                   pl.BlockSpec((tm, D_MODEL), lambda n: (n, 0))],
        out_shape=[jax.ShapeDtypeStruct((T, NP), F32), jax.ShapeDtypeStruct((T, NP), BF),
                   jax.ShapeDtypeStruct((T, D_MODEL), BF)],
        compiler_params=_params(1))(x, g, wp)


def _inproj_bwd_x(dh, wpt, x, g, dxo):
    T = x.shape[0]
    tm = 256

    def body(dh_ref, w_ref, x_ref, g_ref, dxo_ref, dx_ref, dg_ref):
        n = pl.program_id(0)
        dxn = _dot(dh_ref[...], w_ref[...])
        xv = x_ref[...]
        _, r = _rms_fwd(xv, g_ref[...])
        dx, dgt = _rms_bwd(xv, g_ref[...], r, dxn)
        dx_ref[...] = dxo_ref[...] + dx

        @pl.when(n == 0)
        def _():
            dg_ref[...] = jnp.zeros_like(dg_ref)

        dg_ref[...] += jnp.sum(dgt, axis=0, keepdims=True)

    return pl.pallas_call(
        body, name="inproj_bwd_x", grid=(T // tm,),
        in_specs=[pl.BlockSpec((tm, NP), lambda n: (n, 0)),
                  pl.BlockSpec((NP, D_MODEL), lambda n: (0, 0)),
                  pl.BlockSpec((tm, D_MODEL), lambda n: (n, 0)),
                  pl.BlockSpec((1, D_MODEL), lambda n: (0, 0)),
                  pl.BlockSpec((tm, D_MODEL), lambda n: (n, 0))],
        out_specs=[pl.BlockSpec((tm, D_MODEL), lambda n: (n, 0)),
                   pl.BlockSpec((1, D_MODEL), lambda n: (0, 0))],
        out_shape=[jax.ShapeDtypeStruct((T, D_MODEL), F32), jax.ShapeDtypeStruct((1, D_MODEL), F32)],
        compiler_params=_params(1))(dh, wpt, x, g, dxo)


def _matmul_tn(a, b, tn, name):
    T, M = a.shape
    N = b.shape[1]
    tk = 512 if T % 512 == 0 else T

    def body(a_ref, b_ref, o_ref):
        k = pl.program_id(1)

        @pl.when(k == 0)
        def _():
            o_ref[...] = jnp.zeros_like(o_ref)

        o_ref[...] += _dot_tn(a_ref[...], b_ref[...])

    return pl.pallas_call(
        body, name=name, grid=(N // tn, T // tk),
        in_specs=[pl.BlockSpec((tk, M), lambda j, k: (k, 0)),
                  pl.BlockSpec((tk, tn), lambda j, k: (k, j))],
        out_specs=pl.BlockSpec((M, tn), lambda j, k: (0, j)),
        out_shape=jax.ShapeDtypeStruct((M, N), F32),
        compiler_params=_params(2))(a, b)


def _swa_probs(q, kc, kp, sink, mask_c, mask_p):
    sc = jnp.where(mask_c, _dot_nt(q, kc) * SB_SCALE, NEG)
    sp = jnp.where(mask_p, _dot_nt(q, kp) * SB_SCALE, NEG)
    m = jnp.maximum(jnp.maximum(jnp.max(sc, axis=-1, keepdims=True), jnp.max(sp, axis=-1, keepdims=True)), sink)
    pc = jnp.exp(sc - m)
    pp = jnp.exp(sp - m)
    ps = jnp.exp(sink - m)
    inv = 1.0 / (jnp.sum(pc, axis=-1, keepdims=True) + jnp.sum(pp, axis=-1, keepdims=True) + ps)
    return pc * inv, pp * inv, ps * inv


def _swa_masks(n, tq):
    row = _iota((tq, tq), 0)
    col = _iota((tq, tq), 1)
    return col <= row, jnp.logical_and(col > row, n > 0)


def _swa_specs(tq):
    return [pl.BlockSpec(memory_space=pltpu.SMEM),
            pl.BlockSpec((tq, 256), lambda n: (n, C_AQ // 256)),
            pl.BlockSpec((tq, 128), lambda n: (n, C_AK // 128)),
            pl.BlockSpec((tq, 128), lambda n: (jnp.maximum(n - 1, 0), C_AK // 128)),
            pl.BlockSpec((tq, 128), lambda n: (n, C_AV // 128)),
            pl.BlockSpec((tq, 128), lambda n: (jnp.maximum(n - 1, 0), C_AV // 128))]


def _swa_fwd(h16, sinks):
    T = h16.shape[0]
    tq = 128

    def body(s_ref, q_ref, kc_ref, kp_ref, vc_ref, vp_ref, o_ref):
        n = pl.program_id(0)
        mask_c, mask_p = _swa_masks(n, tq)
        for h in range(4):
            g = h // 2
            q = q_ref[:, h * 64:(h + 1) * 64]
            pc, pp, _ = _swa_probs(q, kc_ref[:, g * 64:(g + 1) * 64], kp_ref[:, g * 64:(g + 1) * 64],
                                   s_ref[h], mask_c, mask_p)
            o_ref[:, h * 64:(h + 1) * 64] = (_dot(pc.astype(BF), vc_ref[:, g * 64:(g + 1) * 64])
                                             + _dot(pp.astype(BF), vp_ref[:, g * 64:(g + 1) * 64]))

    return pl.pallas_call(
        body, name="swa_fwd", grid=(T // tq,), in_specs=_swa_specs(tq),
        out_specs=pl.BlockSpec((tq, 256), lambda n: (n, 0)),
        out_shape=jax.ShapeDtypeStruct((T, 256), F32),
        compiler_params=_params(1))(sinks, h16, h16, h16, h16, h16)


def _swa_bwd(h16, sinks, dya):
    T = h16.shape[0]
    tq = 128

    def body(s_ref, q_ref, kc_ref, kp_ref, vc_ref, vp_ref, do_ref, dq_ref, dk_ref, dv_ref, ds_ref):
        n = pl.program_id(0)

        @pl.when(n == 0)
        def _():
            dk_ref[...] = jnp.zeros_like(dk_ref)
            dv_ref[...] = jnp.zeros_like(dv_ref)
            ds_ref[...] = jnp.zeros_like(ds_ref)

        mask_c, mask_p = _swa_masks(n, tq)
        rc = pl.ds(pl.multiple_of(n * tq, tq), tq)
        rp = pl.ds(pl.multiple_of(jnp.maximum(n - 1, 0) * tq, tq), tq)
        lane = _iota((8, 128), 1)
        row8 = _iota((8, 128), 0)
        for h in range(4):
            g = h // 2
            hs = slice(h * 64, (h + 1) * 64)
            gs = slice(g * 64, (g + 1) * 64)
            q = q_ref[:, hs]
            kc, kp, vc, vp = kc_ref[:, gs], kp_ref[:, gs], vc_ref[:, gs], vp_ref[:, gs]
            pc, pp, ps = _swa_probs(q, kc, kp, s_ref[h], mask_c, mask_p)
            pcb, ppb = pc.astype(BF), pp.astype(BF)
            do = do_ref[:, hs]
            dob = do.astype(BF)
            o = _dot(pcb, vc) + _dot(ppb, vp)
            dd = jnp.sum(do * o, axis=-1, keepdims=True)
            dsc = (pc * (_dot_nt(dob, vc) - dd) * SB_SCALE).astype(BF)
            dsp = (pp * (_dot_nt(dob, vp) - dd) * SB_SCALE).astype(BF)
            dq_ref[:, hs] = (_dot(dsc, kc) + _dot(dsp, kp)).astype(BF)
            dk_ref[rc, gs] += _dot_tn(dsc, q)
            dk_ref[rp, gs] += _dot_tn(dsp, q)
            dv_ref[rc, gs] += _dot_tn(pcb, dob)
            dv_ref[rp, gs] += _dot_tn(ppb, dob)
            dsink = -jnp.sum(ps * dd)
            ds_ref[...] += jnp.where(jnp.logical_and(lane == h, row8 == 0), dsink, 0.0)

    return pl.pallas_call(
        body, name="swa_bwd", grid=(T // tq,),
        in_specs=_swa_specs(tq) + [pl.BlockSpec((tq, 256), lambda n: (n, 0))],
        out_specs=[pl.BlockSpec((tq, 256), lambda n: (n, 0)),
                   pl.BlockSpec((T, 128), lambda n: (0, 0)),
                   pl.BlockSpec((T, 128), lambda n: (0, 0)),
                   pl.BlockSpec((8, 128), lambda n: (0, 0))],
        out_shape=[jax.ShapeDtypeStruct((T, 256), BF), jax.ShapeDtypeStruct((T, 128), F32),
                   jax.ShapeDtypeStruct((T, 128), F32), jax.ShapeDtypeStruct((8, 128), F32)],
        compiler_params=_params(1))(sinks, h16, h16, h16, h16, h16, dya)


def _conv_u(bc_ref, bx_ref, bch_ref, bxh_ref, n, tm):
    u = bc_ref[...] * bx_ref[...]
    uh = bch_ref[...] * bxh_ref[...] * (n > 0).astype(F32)
    rowi = _iota((tm, 256), 0)
    u1 = jnp.where(rowi == 0, uh[7:8, :], pltpu.roll(u, 1, axis=0))
    u2 = jnp.where(rowi == 0, uh[6:7, :], jnp.where(rowi == 1, uh[7:8, :], pltpu.roll(u, 2, axis=0)))
    return u, u1, u2


def _conv_fwd(h32, cw, cb):
    T = h32.shape[0]
    tm = 512 if T % 512 == 0 else T
    hb = tm // 8

    def body(bb_ref, bc_ref, bx_ref, bch_ref, bxh_ref, w_ref, b_ref, o_ref):
        n = pl.program_id(0)
        u, u1, u2 = _conv_u(bc_ref, bx_ref, bch_ref, bxh_ref, n, tm)
        y = w_ref[0:1, :] * u2 + w_ref[1:2, :] * u1 + w_ref[2:3, :] * u + b_ref[...]
        o_ref[...] = bb_ref[...] * y

    halo = lambda c: pl.BlockSpec((8, 256), lambda n: (jnp.maximum(n * hb - 1, 0), c // 256))
    return pl.pallas_call(
        body, name="conv_fwd", grid=(T // tm,),
        in_specs=[pl.BlockSpec((tm, 256), lambda n: (n, C_BB // 256)),
                  pl.BlockSpec((tm, 256), lambda n: (n, C_BC // 256)),
                  pl.BlockSpec((tm, 256), lambda n: (n, C_BX // 256)),
                  halo(C_BC), halo(C_BX),
                  pl.BlockSpec((8, 256), lambda n: (0, 0)),
                  pl.BlockSpec((1, 256), lambda n: (0, 0))],
        out_specs=pl.BlockSpec((tm, 256), lambda n: (n, 0)),
        out_shape=jax.ShapeDtypeStruct((T, 256), F32),
        compiler_params=_params(1))(h32, h32, h32, h32, h32, cw, cb)


def _conv_bwd(h32, cw, cb, dyb):
    T = h32.shape[0]
    tm = 512 if T % 512 == 0 else T
    hb = tm // 8
    nt = T // tm

    def body(bb_ref, bc_ref, bx_ref, bch_ref, bxh_ref, bbn_ref, dy_ref, dyn_ref, w_ref, b_ref,
             dbb_ref, dbc_ref, dbx_ref, dw_ref):
        n = pl.program_id(0)

        @pl.when(n == 0)
        def _():
            dw_ref[...] = jnp.zeros_like(dw_ref)

        u, u1, u2 = _conv_u(bc_ref, bx_ref, bch_ref, bxh_ref, n, tm)
        w0, w1, w2 = w_ref[0:1, :], w_ref[1:2, :], w_ref[2:3, :]
        y = w0 * u2 + w1 * u1 + w2 * u + b_ref[...]
        dyb_ = dy_ref[...]
        dbb_ref[...] = (dyb_ * y).astype(BF)
        dy = dyb_ * bb_ref[...]
        dyn = dyn_ref[...] * bbn_ref[...] * (n < nt - 1).astype(F32)
        rowi = _iota((tm, 256), 0)
        dy1 = jnp.where(rowi == tm - 1, dyn[0:1, :], pltpu.roll(dy, tm - 1, axis=0))
        dy2 = jnp.where(rowi == tm - 2, dyn[0:1, :],
                        jnp.where(rowi == tm - 1, dyn[1:2, :], pltpu.roll(dy, tm - 2, axis=0)))
        du = w2 * dy + w1 * dy1 + w0 * dy2
        dbc_ref[...] = (du * bx_ref[...]).astype(BF)
        dbx_ref[...] = (du * bc_ref[...]).astype(BF)
        dw_ref[0:1, :] += jnp.sum(dy * u2, axis=0, keepdims=True)
        dw_ref[1:2, :] += jnp.sum(dy * u1, axis=0, keepdims=True)
        dw_ref[2:3, :] += jnp.sum(dy * u, axis=0, keepdims=True)
        dw_ref[3:4, :] += jnp.sum(dy, axis=0, keepdims=True)

    halo = lambda c: pl.BlockSpec((8, 256), lambda n: (jnp.maximum(n * hb - 1, 0), c // 256))
    nxt = lambda c: pl.BlockSpec((8, 256), lambda n: (jnp.minimum((n + 1) * hb, T // 8 - 1), c // 256))
    cur = lambda c: pl.BlockSpec((tm, 256), lambda n: (n, c // 256))
    return pl.pallas_call(
        body, name="conv_bwd", grid=(nt,),
        in_specs=[cur(C_BB), cur(C_BC), cur(C_BX), halo(C_BC), halo(C_BX), nxt(C_BB),
                  cur(0), nxt(0),
                  pl.BlockSpec((8, 256), lambda n: (0, 0)),
                  pl.BlockSpec((1, 256), lambda n: (0, 0))],
        out_specs=[cur(0), cur(0), cur(0), pl.BlockSpec((8, 256), lambda n: (0, 0))],
        out_shape=[jax.ShapeDtypeStruct((T, 256), BF)] * 3 + [jax.ShapeDtypeStruct((8, 256), F32)],
        compiler_params=_params(1))(h32, h32, h32, h32, h32, h32, dyb, dyb, cw, cb)


def _cprep_specs(tm):
    return [pl.BlockSpec((tm, 256), lambda n: (n, C_CQ // 256)),
            pl.BlockSpec((tm, 128), lambda n: (n, C_CKV // 128)),
            pl.BlockSpec((tm, 128), lambda n: (n, C_CKR // 128)),
            pl.BlockSpec((tm, 128), lambda n: (n, C_CKRS // 128)),
            pl.BlockSpec((1, 256), lambda n: (0, 0)),
            pl.BlockSpec((1, 128), lambda n: (0, 0)),
            pl.BlockSpec((tm, 128), lambda n: (n, 0)),
            pl.BlockSpec((tm, 128), lambda n: (n, 0))]


def _cprep_fwd(h32, gq, gkv, wuq2, wkv2, cosk, sin):
    T = h32.shape[0]
    tm = 512 if T % 512 == 0 else T

    def body(cq_ref, ckv_ref, ckr_ref, ckrs_ref, gq_ref, gkv_ref, cos_ref, sin_ref, wuq_ref, wkv_ref,
             q_ref, k_ref, v_ref):
        cosk_, sin_ = cos_ref[...], sin_ref[...]
        cosq = cosk_ + (_iota((tm, 128), 1) < 64).astype(F32)
        cqn, _ = _rms_fwd(cq_ref[...], gq_ref[...])
        q2 = _dot(cqn.astype(BF), wuq_ref[...])
        ckvn, _ = _rms_fwd(ckv_ref[...], gkv_ref[...])
        kv2 = _dot(ckvn.astype(BF), wkv_ref[...])
        kr = ckr_ref[...] * cosk_ + ckrs_ref[...] * sin_
        for h in range(4):
            hs = slice(h * 128, (h + 1) * 128)
            q_ref[:, hs] = ((q2[:, hs] * cosq + q2[:, 512 + h * 128:512 + (h + 1) * 128] * sin_) * MLA_SCALE).astype(BF)
            k_ref[:, hs] = (kv2[:, hs] + kr).astype(BF)
        v_ref[...] = kv2[:, 512:].astype(BF)

    return pl.pallas_call(
        body, name="cprep_fwd", grid=(T // tm,),
        in_specs=_cprep_specs(tm) + [pl.BlockSpec((256, 1024), lambda n: (0, 0)),
                                     pl.BlockSpec((128, 1024), lambda n: (0, 0))],
        out_specs=[pl.BlockSpec((tm, 512), lambda n: (n, 0))] * 3,
        out_shape=[jax.ShapeDtypeStruct((T, 512), BF)] * 3,
        compiler_params=_params(1))(h32, h32, h32, h32, gq, gkv, cosk, sin, wuq2, wkv2)


def _cprep_bwd(h32, gq, gkv, wuq2t, wkv2t, cosk, sin, dq, dk, dv):
    T = h32.shape[0]
    tm = 512 if T % 512 == 0 else T

    def body(cq_ref, ckv_ref, ckr_ref, ckrs_ref, gq_ref, gkv_ref, cos_ref, sin_ref, wuq_ref, wkv_ref,
             dq_ref, dk_ref, dv_ref,
             dcq_ref, dckv_ref, dckr_ref, dckrs_ref, dq2_ref, dkv2_ref, cqn_ref, ckvn_ref, dgq_ref, dgkv_ref):
        n = pl.program_id(0)

        @pl.when(n == 0)
        def _():
            dgq_ref[...] = jnp.zeros_like(dgq_ref)
            dgkv_ref[...] = jnp.zeros_like(dgkv_ref)

        cosk_, sin_ = cos_ref[...], sin_ref[...]
        cosq = cosk_ + (_iota((tm, 128), 1) < 64).astype(F32)
        dkr = jnp.zeros((tm, 128), F32)
        for h in range(4):
            hs = slice(h * 128, (h + 1) * 128)
            dqh = dq_ref[:, hs] * MLA_SCALE
            dq2_ref[:, hs] = (dqh * cosq).astype(BF)
            dq2_ref[:, 512 + h * 128:512 + (h + 1) * 128] = (dqh * sin_).astype(BF)
            dkr = dkr + dk_ref[:, hs]
        dkv2_ref[:, :512] = dk_ref[...].astype(BF)
        dkv2_ref[:, 512:] = dv_ref[...].astype(BF)
        dckr_ref[...] = (dkr * cosk_).astype(BF)
        dckrs_ref[...] = (dkr * sin_).astype(BF)

        cq, gq_ = cq_ref[...], gq_ref[...]
        cqn, rq = _rms_fwd(cq, gq_)
        cqn_ref[...] = cqn.astype(BF)
        dcq, dgt = _rms_bwd(cq, gq_, rq, _dot(dq2_ref[...], wuq_ref[...]))
        dcq_ref[...] = dcq.astype(BF)
        dgq_ref[...] += jnp.sum(dgt, axis=0, keepdims=True)

        ckv, gkv_ = ckv_ref[...], gkv_ref[...]
        ckvn, rkv = _rms_fwd(ckv, gkv_)
        ckvn_ref[...] = ckvn.astype(BF)
        dckv, dgt2 = _rms_bwd(ckv, gkv_, rkv, _dot(dkv2_ref[...], wkv_ref[...]))
        dckv_ref[...] = dckv.astype(BF)
        dgkv_ref[...] += jnp.sum(dgt2, axis=0, keepdims=True)

    row = lambda w: pl.BlockSpec((tm, w), lambda n: (n, 0))
    return pl.pallas_call(
        body, name="cprep_bwd", grid=(T // tm,),
        in_specs=_cprep_specs(tm) + [pl.BlockSpec((1024, 256), lambda n: (0, 0)),
                                     pl.BlockSpec((1024, 128), lambda n: (0, 0)),
                                     row(512), row(512), row(512)],
        out_specs=[row(256), row(128), row(128), row(128), row(1024), row(1024), row(256), row(128),
                   pl.BlockSpec((1, 256), lambda n: (0, 0)), pl.BlockSpec((1, 128), lambda n: (0, 0))],
        out_shape=[jax.ShapeDtypeStruct((T, 256), BF), jax.ShapeDtypeStruct((T, 128), BF),
                   jax.ShapeDtypeStruct((T, 128), BF), jax.ShapeDtypeStruct((T, 128), BF),
                   jax.ShapeDtypeStruct((T, 1024), BF), jax.ShapeDtypeStruct((T, 1024), BF),
                   jax.ShapeDtypeStruct((T, 256), BF), jax.ShapeDtypeStruct((T, 128), BF),
                   jax.ShapeDtypeStruct((1, 256), F32), jax.ShapeDtypeStruct((1, 128), F32)],
        compiler_params=_params(1))(h32, h32, h32, h32, gq, gkv, cosk, sin, wuq2t, wkv2t, dq, dk, dv)


def _causal_mask(t):
    return _iota((t, t), 1) <= _iota((t, t), 0)


def _mla_fwd(q, k, v):
    T = q.shape[0]
    tq = 256

    def body(q_ref, k_ref, v_ref, o_ref, lse_ref):
        i = pl.program_id(1)
        qv = q_ref[...]
        mask = _causal_mask(tq)

        def step(j, carry, masked):
            m, l, acc = carry
            rows = pl.ds(pl.multiple_of(j * tq, tq), tq)
            s = _dot_nt(qv, k_ref[rows, :])
            if masked:
                s = jnp.where(mask, s, NEG)
            m_new = jnp.maximum(m, jnp.max(s, axis=-1, keepdims=True))
            p = jnp.exp(s - m_new)
            alpha = jnp.exp(m - m_new)
            l = alpha * l + jnp.sum(p, axis=-1, keepdims=True)
            acc = alpha * acc + _dot(p.astype(BF), v_ref[rows, :])
            return m_new, l, acc

        init = (jnp.full((tq, 1), NEG, F32), jnp.zeros((tq, 1), F32), jnp.zeros((tq, 128), F32))
        carry = lax.fori_loop(0, i, lambda j, c: step(j, c, False), init)
        m, l, acc = step(i, carry, True)
        o_ref[...] = acc * (1.0 / l)
        lse_ref[...] = jnp.broadcast_to(m + jnp.log(l), (tq, 128))

    blk = pl.BlockSpec((tq, 128), lambda h, i: (i, h))
    full = pl.BlockSpec((T, 128), lambda h, i: (0, h))
    return pl.pallas_call(
        body, name="mla_fwd", grid=(4, T // tq), in_specs=[blk, full, full], out_specs=[blk, blk],
        out_shape=[jax.ShapeDtypeStruct((T, 512), F32), jax.ShapeDtypeStruct((T, 512), F32)],
        compiler_params=_params(2))(q, k, v)


def _mla_bwd(q, k, v, o, lse, do):
    T = q.shape[0]
    tq = 256

    def body(q_ref, k_ref, v_ref, o_ref, lse_ref, do_ref, dq_ref, dk_ref, dv_ref):
        i = pl.program_id(1)

        @pl.when(i == 0)
        def _():
            dk_ref[...] = jnp.zeros_like(dk_ref)
            dv_ref[...] = jnp.zeros_like(dv_ref)

        qv = q_ref[...]
        do = do_ref[...]
        dob = do.astype(BF)
        dd = jnp.sum(do * o_ref[...], axis=-1, keepdims=True)
        lse_ = lse_ref[:, 0:1]
        mask = _causal_mask(tq)

        def step(j, dq, masked):
            rows = pl.ds(pl.multiple_of(j * tq, tq), tq)
            kj, vj = k_ref[rows, :], v_ref[rows, :]
            s = _dot_nt(qv, kj)
            if masked:
                s = jnp.where(mask, s, NEG)
            p = jnp.exp(s - lse_)
            ds = (p * (_dot_nt(dob, vj) - dd)).astype(BF)
            dk_ref[rows, :] += _dot_tn(ds, qv)
            dv_ref[rows, :] += _dot_tn(p.astype(BF), dob)
            return dq + _dot(ds, kj)

        dq = lax.fori_loop(0, i, lambda j, c: step(j, c, False), jnp.zeros((tq, 128), F32))
        dq_ref[...] = step(i, dq, True)

    blk = pl.BlockSpec((tq, 128), lambda h, i: (i, h))
    full = pl.BlockSpec((T, 128), lambda h, i: (0, h))
    return pl.pallas_call(
        body, name="mla_bwd", grid=(4, T // tq), in_specs=[blk, full, full, blk, blk, blk],
        out_specs=[blk, full, full],
        out_shape=[jax.ShapeDtypeStruct((T, 512), F32)] * 3,
        compiler_params=_params(2))(q, k, v, o, lse, do)


def _sb_tile(qm, kj, rr, strict, masked, upper):
    z = _dot_nt(qm, kj) * SB_SCALE
    sp = jnp.maximum(z, 0.0) + jnp.log(1.0 + jnp.exp(-jnp.abs(z)))
    lk = -sp
    if masked:
        lk = jnp.where(strict, lk, 0.0)
    hi = lk.astype(BF)
    lo = (lk - hi.astype(F32)).astype(BF)
    after = rr + _dot(hi, upper) + _dot(lo, upper)
    ll = z - sp
    a = jnp.exp(ll + after)
    if masked:
        a = jnp.where(strict, a, 0.0)
    return z, ll, a, jnp.sum(lk, axis=-1, keepdims=True)


def _sb_consts(t):
    row, col = _iota((t, t), 0), _iota((t, t), 1)
    strict = col < row
    upper = (row > col).astype(BF)
    lower = (row >= col).astype(BF)
    return strict, upper, lower


def _sb_fwd(h16):
    T = h16.shape[0]
    tq = 256

    def body(q_ref, k_ref, v_ref, o_ref):
        i = pl.program_id(1)
        strict, upper, _ = _sb_consts(tq)
        lane = _iota((tq, 128), 1)
        q2 = q_ref[...]
        outs = []
        for hh in range(2):
            mine = (lane < 64) if hh == 0 else (lane >= 64)
            qm = jnp.where(mine, q2, jnp.zeros_like(q2))

            def step(j, carry, masked):
                rr, acc = carry
                rows = pl.ds(pl.multiple_of(j * tq, tq), tq)
                _, _, a, rs = _sb_tile(qm, k_ref[rows, :], rr, strict, masked, upper)
                return rr + rs, acc + _dot(a.astype(BF), v_ref[rows, :])

            carry = step(i, (jnp.zeros((tq, 1), F32), jnp.zeros((tq, 128), F32)), True)
            _, acc = lax.fori_loop(0, i, lambda t, c: step(i - 1 - t, c, False), carry)
            outs.append(acc)
        o_ref[...] = jnp.where(lane < 64, outs[0], outs[1])

    return pl.pallas_call(
        body, name="sb_fwd", grid=(2, T // tq),
        in_specs=[pl.BlockSpec((tq, 128), lambda p, i: (i, C_DQ // 128 + p)),
                  pl.BlockSpec((T, 128), lambda p, i: (0, C_DK // 128 + p)),
                  pl.BlockSpec((T, 128), lambda p, i: (0, C_DV // 128 + p))],
        out_specs=pl.BlockSpec((tq, 128), lambda p, i: (i, p)),
        out_shape=jax.ShapeDtypeStruct((T, 256), F32),
        compiler_params=_params(2))(h16, h16, h16)


def _sb_bwd(h16, yd, dyd):
    T = h16.shape[0]
    tq = 256

    def body(q_ref, k_ref, v_ref, o_ref, do_ref, dq_ref, dk_ref, dv_ref):
        i = pl.program_id(1)

        @pl.when(i == 0)
        def _():
            dk_ref[...] = jnp.zeros_like(dk_ref)
            dv_ref[...] = jnp.zeros_like(dv_ref)

        strict, upper, lower = _sb_consts(tq)
        lane = _iota((tq, 128), 1)
        q2 = q_ref[...]
        do2 = do_ref[...]
        dob2 = do2.astype(BF)
        dol2 = (do2 - dob2.astype(F32)).astype(BF)
        doo = do2 * o_ref[...]
        dqs = []
        for hh in range(2):
            mine = (lane < 64) if hh == 0 else (lane >= 64)
            qm = jnp.where(mine, q2, jnp.zeros_like(q2))
            dom = jnp.where(mine, dob2, jnp.zeros_like(dob2))
            dolm = jnp.where(mine, dol2, jnp.zeros_like(dol2))
            dd = jnp.sum(jnp.where(mine, doo, 0.0), axis=-1, keepdims=True)

            def step(j, carry, masked):
                rr, sg, dq = carry
                rows = pl.ds(pl.multiple_of(j * tq, tq), tq)
                kj, vj = k_ref[rows, :], v_ref[rows, :]
                z, ll, a, rs = _sb_tile(qm, kj, rr, strict, masked, upper)
                ab = a.astype(BF)
                g = (_dot_nt(dom, vj) + _dot_nt(dolm, vj)) * ab.astype(F32)
                ghi = g.astype(BF)
                suf = sg + _dot(ghi, lower) + _dot((g - ghi.astype(F32)).astype(BF), lower)
                sig = jnp.exp(ll)
                dz = g * (1.0 - sig) - sig * (dd - suf)
                if masked:
                    dz = jnp.where(strict, dz, 0.0)
                dzb = (dz * SB_SCALE).astype(BF)
                dk_ref[rows, :] += jnp.where(mine, _dot_tn(dzb, q2), 0.0)
                dv_ref[rows, :] += jnp.where(mine, _dot_tn(ab, dob2), 0.0)
                return rr + rs, sg + jnp.sum(g, axis=-1, keepdims=True), dq + _dot(dzb, kj)

            zero = jnp.zeros((tq, 1), F32)
            carry = step(i, (zero, zero, jnp.zeros((tq, 128), F32)), True)
            _, _, dq = lax.fori_loop(0, i, lambda t, c: step(i - 1 - t, c, False), carry)
            dqs.append(dq)
        dq_ref[...] = jnp.where(lane < 64, dqs[0], dqs[1])

    blk = lambda c: pl.BlockSpec((tq, 128), lambda p, i: (i, c // 128 + p))
    full = lambda c: pl.BlockSpec((T, 128), lambda p, i: (0, c // 128 + p))
    return pl.pallas_call(
        body, name="sb_bwd", grid=(2, T // tq),
        in_specs=[blk(C_DQ), full(C_DK), full(C_DV), blk(0), blk(0)],
        out_specs=[blk(0), full(0), full(0)],
        out_shape=[jax.ShapeDtypeStruct((T, 256), F32)] * 3,
        compiler_params=_params(2))(h16, h16, h16, yd, dyd)


def _compact_c(ycp):
    return jnp.concatenate([ycp[:, h * 128:h * 128 + 64] for h in range(4)], axis=1)


def _post_fwd(ya, yb, ycp, yd, h32, ggrp, wout, gpost, x):
    T = x.shape[0]
    tm = 256

    def body(ya_ref, yb_ref, yc_ref, yd_ref, gate_ref, gg_ref, w_ref, gp_ref, x_ref, xn_ref, ym_ref, o_ref):
        ys = [ya_ref[...], yb_ref[...], _compact_c(yc_ref[...]), yd_ref[...]]
        gate = gate_ref[...]
        sil = gate * (1.0 / (1.0 + jnp.exp(-gate)))
        parts = []
        for gi in range(4):
            ng, _ = _rms_fwd(ys[gi], gg_ref[:, gi * 256:(gi + 1) * 256])
            parts.append(ng * sil[:, gi * 256:(gi + 1) * 256])
        ym = jnp.concatenate(parts, axis=1).astype(BF)
        ym_ref[...] = ym
        o = _dot(ym, w_ref[...])
        o_ref[...] = o
        on, _ = _rms_fwd(o, gp_ref[...])
        xn_ref[...] = x_ref[...] + on

    row = lambda w: pl.BlockSpec((tm, w), lambda n: (n, 0))
    vec = pl.BlockSpec((1, 1024), lambda n: (0, 0))
    return pl.pallas_call(
        body, name="post_fwd", grid=(T // tm,),
        in_specs=[row(256), row(256), row(512), row(256), pl.BlockSpec((tm, 1024), lambda n: (n, C_GATE // 1024)),
                  vec, pl.BlockSpec((1024, 1024), lambda n: (0, 0)), vec, row(1024)],
        out_specs=[row(1024), row(1024), row(1024)],
        out_shape=[jax.ShapeDtypeStruct((T, 1024), F32), jax.ShapeDtypeStruct((T, 1024), BF),
                   jax.ShapeDtypeStruct((T, 1024), F32)],
        compiler_params=_params(1))(ya, yb, ycp, yd, h32, ggrp, wout, gpost, x)


def _post_bwd(dx, o, gpost, woutt, ya, yb, ycp, yd, h32, ggrp):
    T = dx.shape[0]
    tm = 256

    def body(dx_ref, o_ref, gp_ref, w_ref, ya_ref, yb_ref, yc_ref, yd_ref, gate_ref, gg_ref,
             do_ref, dya_ref, dyb_ref, dyc_ref, dyd_ref, dgate_ref, dgp_ref, dgg_ref):
        n = pl.program_id(0)

        @pl.when(n == 0)
        def _():
            dgp_ref[...] = jnp.zeros_like(dgp_ref)
            dgg_ref[...] = jnp.zeros_like(dgg_ref)

        ov, gp = o_ref[...], gp_ref[...]
        _, ro = _rms_fwd(ov, gp)
        do, dgt = _rms_bwd(ov, gp, ro, dx_ref[...])
        dgp_ref[...] += jnp.sum(dgt, axis=0, keepdims=True)
        dob = do.astype(BF)
        do_ref[...] = dob
        dym = _dot(dob, w_ref[...])
        gate = gate_ref[...]
        sg = 1.0 / (1.0 + jnp.exp(-gate))
        sil = gate * sg
        dsil = sg * (1.0 + gate * (1.0 - sg))
        ys = [ya_ref[...], yb_ref[...], _compact_c(yc_ref[...]), yd_ref[...]]
        dys = []
        for gi in range(4):
            gs = slice(gi * 256, (gi + 1) * 256)
            gg = gg_ref[:, gs]
            ng, rg = _rms_fwd(ys[gi], gg)
            dgate_ref[:, gs] = (dym[:, gs] * ng * dsil[:, gs]).astype(BF)
            dy, dgt2 = _rms_bwd(ys[gi], gg, rg, dym[:, gs] * sil[:, gs])
            dgg_ref[:, gs] += jnp.sum(dgt2, axis=0, keepdims=True)
            dys.append(dy)
        dya_ref[...] = dys[0]
        dyb_ref[...] = dys[1]
        dyd_ref[...] = dys[3]
        z64 = jnp.zeros((tm, 64), F32)
        dyc_ref[...] = jnp.concatenate(
            [piece for h in range(4) for piece in (dys[2][:, h * 64:(h + 1) * 64], z64)], axis=1)

    row = lambda w: pl.BlockSpec((tm, w), lambda n: (n, 0))
    vec = pl.BlockSpec((1, 1024), lambda n: (0, 0))
    return pl.pallas_call(
        body, name="post_bwd", grid=(T // tm,),
        in_specs=[row(1024), row(1024), vec, pl.BlockSpec((1024, 1024), lambda n: (0, 0)),
                  row(256), row(256), row(512), row(256),
                  pl.BlockSpec((tm, 1024), lambda n: (n, C_GATE // 1024)), vec],
        out_specs=[row(1024), row(256), row(256), row(512), row(256), row(1024), vec, vec],
        out_shape=[jax.ShapeDtypeStruct((T, 1024), BF), jax.ShapeDtypeStruct((T, 256), F32),
                   jax.ShapeDtypeStruct((T, 256), F32), jax.ShapeDtypeStruct((T, 512), F32),
                   jax.ShapeDtypeStruct((T, 256), F32), jax.ShapeDtypeStruct((T, 1024), BF),
                   jax.ShapeDtypeStruct((1, 1024), F32), jax.ShapeDtypeStruct((1, 1024), F32)],
        compiler_params=_params(1))(dx, o, gpost, woutt, ya, yb, ycp, yd, h32, ggrp)


def _loss_head(y, tgt):
    T = y.shape[0]
    tm = 512 if T % 512 == 0 else T

    def body(y_ref, t_ref, s_ref, dy_ref):
        n = pl.program_id(0)

        @pl.when(n == 0)
        def _():
            s_ref[...] = jnp.zeros_like(s_ref)

        d = y_ref[...] - t_ref[...]
        s_ref[...] += jnp.sum(d * d, axis=0, keepdims=True)
        dy_ref[...] = d * (1.0 / D_MODEL)

    row = pl.BlockSpec((tm, 1024), lambda n: (n, 0))
    return pl.pallas_call(
        body, name="loss_head", grid=(T // tm,), in_specs=[row, row],
        out_specs=[pl.BlockSpec((1, 1024), lambda n: (0, 0)), row],
        out_shape=[jax.ShapeDtypeStruct((1, 1024), F32), jax.ShapeDtypeStruct((T, 1024), F32)],
        compiler_params=_params(1))(y, tgt)


def _pad_w_in(w):
    z = lambda n: jnp.zeros((w.shape[0], n), w.dtype)
    kr = w[:, 1664:1696]
    return jnp.concatenate([w[:, 2464:3488], w[:, 0:1664], z(64), kr, z(32), z(64), _swap32(kr), z(32),
                            w[:, 1696:2464], z(NP - C_END)], axis=1)


def _unpad_w_in(d):
    kr = d[:, C_CKR + 64:C_CKR + 96] + _swap32(d[:, C_CKRS + 64:C_CKRS + 96])
    return jnp.concatenate([d[:, C_AQ:C_CKR], kr, d[:, C_DQ:C_END], d[:, 0:1024]], axis=1)


def _pad_w_uq(w):
    z = lambda n: jnp.zeros((w.shape[0], n), w.dtype)
    a = [p for h in range(4) for p in (w[:, 96 * h:96 * h + 96], z(32))]
    b = [p for h in range(4) for p in (z(64), _swap32(w[:, 96 * h + 64:96 * h + 96]), z(32))]
    return jnp.concatenate(a + b, axis=1)


def _unpad_w_uq(d):
    out = []
    for h in range(4):
        out.append(d[:, 128 * h:128 * h + 64])
        out.append(d[:, 128 * h + 64:128 * h + 96] + _swap32(d[:, 512 + 128 * h + 64:512 + 128 * h + 96]))
    return jnp.concatenate(out, axis=1)


def _pad_w_ukv(w):
    z = jnp.zeros((w.shape[0], 64), w.dtype)
    a = [p for h in range(4) for p in (w[:, 128 * h:128 * h + 64], z)]
    b = [p for h in range(4) for p in (w[:, 128 * h + 64:128 * h + 128], z)]
    return jnp.concatenate(a + b, axis=1)


def _unpad_w_ukv(d):
    return jnp.concatenate([p for h in range(4) for p in (d[:, 128 * h:128 * h + 64],
                                                          d[:, 512 + 128 * h:512 + 128 * h + 64])], axis=1)


def _rope_tables(pos):
    freqs = 10000.0 ** (-jnp.arange(16, dtype=F32) / 16)
    ang = pos.astype(F32)[:, None] * freqs
    c, s = jnp.cos(ang), jnp.sin(ang)
    z = lambda n: jnp.zeros((pos.shape[0], n), F32)
    return (jnp.concatenate([z(64), c, c, z(32)], axis=1), jnp.concatenate([z(64), -s, s, z(32)], axis=1))


def _layer_weights(W, l):
    wp = _pad_w_in(W["w_in"][l])
    wuq2 = _pad_w_uq(W["mla_w_uq"][l])
    wkv2 = _pad_w_ukv(W["mla_w_ukv"][l])
    wout = W["w_out"][l]
    cw = jnp.concatenate([W["conv_w"][l].astype(F32), jnp.zeros((5, 256), F32)], axis=0)
    return dict(
        wp=wp.astype(BF), wpt=wp.T.astype(BF), wuq2=wuq2.astype(BF), wuq2t=wuq2.T.astype(BF),
        wkv2=wkv2.astype(BF), wkv2t=wkv2.T.astype(BF), wout=wout.astype(BF), woutt=wout.T.astype(BF),
        cw=cw, cb=W["conv_b"][l][None, :], sinks=W["attn_sinks"][l],
        gpre=W["norm_pre"][l][None, :], gq=W["mla_q_norm"][l][None, :], gkv=W["mla_kv_norm"][l][None, :],
        ggrp=W["group_norm"][l][None, :], gpost=W["norm_post"][l][None, :])


def _local_step(x, pos, W, tgt):
    cosk, sin = _rope_tables(pos)
    saved = []
    for l in range(DEPTH):
        lw = _layer_weights(W, l)
        h32, h16, xn = _inproj_fwd(x, lw["gpre"], lw["wp"])
        ya = _swa_fwd(h16, lw["sinks"])
        yb = _conv_fwd(h32, lw["cw"], lw["cb"])
        qc, kc, vc = _cprep_fwd(h32, lw["gq"], lw["gkv"], lw["wuq2"], lw["wkv2"], cosk, sin)
        ycp, lse = _mla_fwd(qc, kc, vc)
        yd = _sb_fwd(h16)
        x_new, ym, o = _post_fwd(ya, yb, ycp, yd, h32, lw["ggrp"], lw["wout"], lw["gpost"], x)
        saved.append(dict(lw=lw, x=x, h32=h32, h16=h16, xn=xn, ya=ya, yb=yb, qc=qc, kc=kc, vc=vc, ycp=ycp,
                          lse=lse, yd=yd, ym=ym, o=o))
        x = x_new
    sq, dx = _loss_head(x, tgt)

    grads = {k: [None] * DEPTH for k in ("norm_pre", "w_in", "attn_sinks", "conv_w", "conv_b", "mla_q_norm",
                                         "mla_w_uq", "mla_kv_norm", "mla_w_ukv", "group_norm", "w_out",
                                         "norm_post")}
    for l in reversed(range(DEPTH)):
        s = saved[l]
        lw = s["lw"]
        do, dya, dyb, dycp, dyd, dgate, dgpost, dggrp = _post_bwd(
            dx, s["o"], lw["gpost"], lw["woutt"], s["ya"], s["yb"], s["ycp"], s["yd"], s["h32"], lw["ggrp"])
        grads["norm_post"][l] = dgpost[0]
        grads["group_norm"][l] = dggrp[0]
        grads["w_out"][l] = _matmul_tn(s["ym"], do, 512, "dw_out")
        sdq, sdk, sdv = _sb_bwd(s["h16"], s["yd"], dyd)
        mdq, mdk, mdv = _mla_bwd(s["qc"], s["kc"], s["vc"], s["ycp"], s["lse"], dycp)
        (dcq, dckv, dckr, dckrs, dq2, dkv2, cqn, ckvn, dgq, dgkv) = _cprep_bwd(
            s["h32"], lw["gq"], lw["gkv"], lw["wuq2t"], lw["wkv2t"], cosk, sin, mdq, mdk, mdv)
        grads["mla_q_norm"][l] = dgq[0]
        grads["mla_kv_norm"][l] = dgkv[0]
        grads["mla_w_uq"][l] = _unpad_w_uq(_matmul_tn(cqn, dq2, 512, "dw_uq"))
        grads["mla_w_ukv"][l] = _unpad_w_ukv(_matmul_tn(ckvn, dkv2, 512, "dw_ukv"))
        dbb, dbc, dbx, dcw = _conv_bwd(s["h32"], lw["cw"], lw["cb"], dyb)
        grads["conv_w"][l] = dcw[0:3]
        grads["conv_b"][l] = dcw[3]
        adq, adk, adv, dsk = _swa_bwd(s["h16"], lw["sinks"], dya)
        grads["attn_sinks"][l] = dsk[0, 0:4]
        T = x.shape[0]
        dh = jnp.concatenate([dgate, adq, adk.astype(BF), adv.astype(BF), dbb, dbc, dbx, dcq, dckv, dckr, dckrs,
                              sdq.astype(BF), sdk.astype(BF), sdv.astype(BF), jnp.zeros((T, NP - C_END), BF)],
                             axis=1)
        grads["w_in"][l] = _unpad_w_in(_matmul_tn(s["xn"], dh, 1280, "dw_in"))
        dx, dgpre = _inproj_bwd_x(dh, lw["wpt"], s["x"], lw["gpre"], dx)
        grads["norm_pre"][l] = dgpre[0]
    return sq, dx, {k: jnp.stack(v) for k, v in grads.items()}


SHARDED = ("w_in", "conv_w", "mla_w_uq", "mla_w_ukv", "w_out")
SHARD_AXIS = {"w_in": 2, "conv_w": 2, "mla_w_uq": 2, "mla_w_ukv": 2, "w_out": 1}
REPLICATED = ("norm_pre", "attn_sinks", "conv_b", "mla_q_norm", "mla_kv_norm", "group_norm", "norm_post")
ORDER = ("norm_pre", "w_in", "attn_sinks", "conv_w", "conv_b", "mla_q_norm", "mla_w_uq", "mla_kv_norm",
         "mla_w_ukv", "group_norm", "w_out", "norm_post")


def _pack(arrs, dtype):
    flat = jnp.concatenate([a.reshape(-1).astype(dtype) for a in arrs])
    rows = -(-flat.shape[0] // 2048) * 16
    flat = jnp.concatenate([flat, jnp.zeros((rows * 128 - flat.shape[0],), dtype)])
    return flat.reshape(rows, 128)


def _unpack(buf, shapes):
    flat = buf.reshape(-1)
    out, off = [], 0
    for s in shapes:
        n = int(np.prod(s))
        out.append(flat[off:off + n].reshape(s))
        off += n
    return out


def _me():
    return lax.axis_index("x"), lax.axis_index("y"), lax.axis_index("c")


def _peer(k):
    x, y, c = _me()
    return (x ^ (k >> 2 & 1), y ^ (k >> 1 & 1), c ^ (k & 1))


def _all_gather(block):
    R = block.shape[0]

    def body(src_ref, out_ref, send_sems, recv_sems, local_sem):
        x, y, c = _me()
        me = 4 * x + 2 * y + c
        mine = pltpu.make_async_copy(src_ref, out_ref.at[me], local_sem)
        mine.start()
        copies = [pltpu.make_async_remote_copy(src_ref=src_ref, dst_ref=out_ref.at[me], send_sem=send_sems.at[k - 1],
                                               recv_sem=recv_sems.at[k - 1], device_id=_peer(k), device_id_type=MESH)
                  for k in range(1, N_DEV)]
        for cp in copies:
            cp.start()
        for cp in copies:
            cp.wait()
        mine.wait()

    return pl.pallas_call(
        body, name="all_gather", out_shape=jax.ShapeDtypeStruct((N_DEV, R, 128), block.dtype),
        in_specs=[pl.BlockSpec(memory_space=pl.ANY)], out_specs=pl.BlockSpec(memory_space=pl.ANY),
        scratch_shapes=[pltpu.SemaphoreType.DMA((N_DEV - 1,)), pltpu.SemaphoreType.DMA((N_DEV - 1,)),
                        pltpu.SemaphoreType.DMA])(block)


def _all_to_all(blocks):
    R = blocks.shape[1]

    def body(src_ref, out_ref, send_sems, recv_sems, local_sem):
        x, y, c = _me()
        me = 4 * x + 2 * y + c
        mine = pltpu.make_async_copy(src_ref.at[me], out_ref.at[me], local_sem)
        mine.start()
        copies = []
        for k in range(1, N_DEV):
            px, py, pc = _peer(k)
            copies.append(pltpu.make_async_remote_copy(
                src_ref=src_ref.at[4 * px + 2 * py + pc], dst_ref=out_ref.at[me], send_sem=send_sems.at[k - 1],
                recv_sem=recv_sems.at[k - 1], device_id=(px, py, pc), device_id_type=MESH))
        for cp in copies:
            cp.start()
        for cp in copies:
            cp.wait()
        mine.wait()

    return pl.pallas_call(
        body, name="all_to_all", out_shape=jax.ShapeDtypeStruct((N_DEV, R, 128), blocks.dtype),
        in_specs=[pl.BlockSpec(memory_space=pl.ANY)], out_specs=pl.BlockSpec(memory_space=pl.ANY),
        scratch_shapes=[pltpu.SemaphoreType.DMA((N_DEV - 1,)), pltpu.SemaphoreType.DMA((N_DEV - 1,)),
                        pltpu.SemaphoreType.DMA])(blocks)


def _adamw(parts, w, m, v):
    R = w.shape[0]
    tr = 16
    for cand in (1024, 512, 256, 128, 64, 32, 16):
        if R % cand == 0:
            tr = cand
            break

    def body(p_ref, w_ref, m_ref, v_ref, g_ref, d_ref, nm_ref, nv_ref):
        g = p_ref[0]
        for k in range(1, N_DEV):
            g = g + p_ref[k]
        g_ref[...] = g
        m_ = ADAM_B1 * m_ref[...] + (1.0 - ADAM_B1) * g
        v_ = ADAM_B2 * v_ref[...] + (1.0 - ADAM_B2) * (g * g)
        nm_ref[...] = m_
        nv_ref[...] = v_
        m_hat = m_ / (1.0 - ADAM_B1 ** ADAM_STEP)
        v_hat = v_ / (1.0 - ADAM_B2 ** ADAM_STEP)
        d_ref[...] = -ADAM_LR * (m_hat / (jnp.sqrt(v_hat) + ADAM_EPS) + ADAM_WD * w_ref[...])

    row = pl.BlockSpec((tr, 128), lambda n: (n, 0))
    return pl.pallas_call(
        body, name="adamw", grid=(R // tr,),
        in_specs=[pl.BlockSpec((N_DEV, tr, 128), lambda n: (0, n, 0)), row, row, row],
        out_specs=[row] * 4, out_shape=[jax.ShapeDtypeStruct((R, 128), F32)] * 4,
        compiler_params=_params(1))(parts, w, m, v)


def kernel(x, positions, norm_pre, w_in, attn_sinks, conv_w, conv_b, mla_q_norm, mla_w_uq, mla_kv_norm, mla_w_ukv, group_norm, w_out, norm_post, loss_target, m_norm_pre, m_w_in, m_attn_sinks, m_conv_w, m_conv_b, m_mla_q_norm, m_mla_w_uq, m_mla_kv_norm, m_mla_w_ukv, m_group_norm, m_w_out, m_norm_post, v_norm_pre, v_w_in, v_attn_sinks, v_conv_w, v_conv_b, v_mla_q_norm, v_mla_w_uq, v_mla_kv_norm, v_mla_w_ukv, v_group_norm, v_w_out, v_norm_post):
    local = dict(norm_pre=norm_pre, w_in=w_in, attn_sinks=attn_sinks, conv_w=conv_w, conv_b=conv_b,
                 mla_q_norm=mla_q_norm, mla_w_uq=mla_w_uq, mla_kv_norm=mla_kv_norm, mla_w_ukv=mla_w_ukv,
                 group_norm=group_norm, w_out=w_out, norm_post=norm_post)
    mom = dict(norm_pre=m_norm_pre, w_in=m_w_in, attn_sinks=m_attn_sinks, conv_w=m_conv_w, conv_b=m_conv_b,
               mla_q_norm=m_mla_q_norm, mla_w_uq=m_mla_w_uq, mla_kv_norm=m_mla_kv_norm, mla_w_ukv=m_mla_w_ukv,
               group_norm=m_group_norm, w_out=m_w_out, norm_post=m_norm_post)
    vel = dict(norm_pre=v_norm_pre, w_in=v_w_in, attn_sinks=v_attn_sinks, conv_w=v_conv_w, conv_b=v_conv_b,
               mla_q_norm=v_mla_q_norm, mla_w_uq=v_mla_w_uq, mla_kv_norm=v_mla_kv_norm, mla_w_ukv=v_mla_w_ukv,
               group_norm=v_group_norm, w_out=v_w_out, norm_post=v_norm_post)

    shard_shapes = [local[n].shape for n in SHARDED]
    gathered = _all_gather(_pack([local[n] for n in SHARDED], BF))
    per_dev = [_unpack(gathered[d], shard_shapes) for d in range(N_DEV)]
    W = {n: local[n] for n in REPLICATED}
    for i, n in enumerate(SHARDED):
        W[n] = jnp.concatenate([per_dev[d][i] for d in range(N_DEV)], axis=SHARD_AXIS[n])

    sq, grad_x, g = _local_step(x[0], positions[0], W, loss_target[0])
    loss = lax.psum(0.5 / D_MODEL * jnp.sum(sq), ("x", "y", "c"))

    names = SHARDED + REPLICATED
    blocks = []
    for d in range(N_DEV):
        parts = []
        for n in SHARDED:
            width = local[n].shape[SHARD_AXIS[n]]
            parts.append(lax.slice_in_dim(g[n], d * width, (d + 1) * width, axis=SHARD_AXIS[n]))
        parts += [g[n] for n in REPLICATED]
        blocks.append(_pack(parts, F32))
    received = _all_to_all(jnp.stack(blocks))

    gsum, delta, new_m, new_v = _adamw(received, _pack([local[n] for n in names], F32),
                                       _pack([mom[n] for n in names], F32), _pack([vel[n] for n in names], F32))
    shapes = [local[n].shape for n in names]
    out = {}
    for tag, buf in (("g", gsum), ("d", delta), ("m", new_m), ("v", new_v)):
        for n, a in zip(names, _unpack(buf, shapes)):
            out[tag, n] = a
    return (loss, grad_x[None], *[out["g", n] for n in ORDER], *[out["d", n] for n in ORDER],
            *[out["m", n] for n in ORDER], *[out["v", n] for n in ORDER])
```

```python
import functools

import jax
import jax.numpy as jnp
import numpy as np
from jax import lax
from jax.experimental import pallas as pl
from jax.experimental.pallas import tpu as pltpu

F32 = jnp.float32
BF = jnp.bfloat16
MESH = pl.DeviceIdType.MESH

D_MODEL = 1024
DEPTH = 2
EPS = 1e-6
N_DEV = 8
VMEM_LIMIT = 56 * 1024 * 1024
NEG = -1e30
MLA_SCALE = 96.0 ** -0.5
SB_SCALE = 0.125

NP = 3840
C_GATE = 0
C_AQ = 1024
C_AK = 1280
C_AV = 1408
C_BB = 1536
C_BC = 1792
C_BX = 2048
C_CQ = 2304
C_CKV = 2560
C_CKR = 2688
C_CKRS = 2816
C_DQ = 2944
C_DK = 3200
C_DV = 3456
C_END = 3712

def _swap32(a):
    return jnp.concatenate([a[:, 16:32], a[:, 0:16]], axis=1)

ADAM_LR, ADAM_B1, ADAM_B2, ADAM_EPS, ADAM_WD, ADAM_STEP = 0.001, 0.9, 0.999, 1e-08, 0.01, 10


def _dot(a, b):
    return jnp.dot(a, b, preferred_element_type=F32)


def _dot_nt(a, b):
    return lax.dot_general(a, b, (((1,), (1,)), ((), ())), preferred_element_type=F32)


def _dot_tn(a, b):
    return lax.dot_general(a, b, (((0,), (0,)), ((), ())), preferred_element_type=F32)


def _params(n_grid):
    return pltpu.CompilerParams(dimension_semantics=("arbitrary",) * n_grid, vmem_limit_bytes=VMEM_LIMIT)


def _rms_fwd(x, g):
    r = lax.rsqrt(jnp.mean(x * x, axis=-1, keepdims=True) + EPS)
    return (x * r) * g, r


def _rms_bwd(x, g, r, dy, width=None):
    n = x.shape[-1] if width is None else width
    u = dy * g
    dx = r * u - x * (r * r * r) * (jnp.sum(x * u, axis=-1, keepdims=True) / n)
    return dx, dy * (x * r)


def _iota(shape, axis):
    return lax.broadcasted_iota(jnp.int32, shape, axis)


def _inproj_fwd(x, g, wp):
    T = x.shape[0]
    tm = 256

    def body(x_ref, g_ref, w_ref, h32_ref, h16_ref, xn_ref):
        xn, _ = _rms_fwd(x_ref[...], g_ref[...])
        xn = xn.astype(BF)
        xn_ref[...] = xn
        h = _dot(xn, w_ref[...])
        h32_ref[...] = h
        h16_ref[...] = h.astype(BF)

    return pl.pallas_call(
        body, name="inproj_fwd", grid=(T // tm,),
        in_specs=[pl.BlockSpec((tm, D_MODEL), lambda n: (n, 0)),
                  pl.BlockSpec((1, D_MODEL), lambda n: (0, 0)),
                  pl.BlockSpec((D_MODEL, NP), lambda n: (0, 0))],
        out_specs=[pl.BlockSpec((tm, NP), lambda n: (n, 0)),
                   pl.BlockSpec((tm, NP), lambda n: (n, 0)),
                   pl.BlockSpec((tm, D_MODEL), lambda n: (n, 0))],
        out_shape=[jax.ShapeDtypeStruct((T, NP), F32), jax.ShapeDtypeStruct((T, NP), BF),
                   jax.ShapeDtypeStruct((T, D_MODEL), BF)],
        compiler_params=_params(1))(x, g, wp)


def _inproj_bwd_x(dh, wpt, x, g, dxo):
    T = x.shape[0]
    tm = 256

    def body(dh_ref, w_ref, x_ref, g_ref, dxo_ref, dx_ref, dg_ref):
        n = pl.program_id(0)
        dxn = _dot(dh_ref[...], w_ref[...])
        xv = x_ref[...]
        _, r = _rms_fwd(xv, g_ref[...])
        dx, dgt = _rms_bwd(xv, g_ref[...], r, dxn)
        dx_ref[...] = dxo_ref[...] + dx

        @pl.when(n == 0)
        def _():
            dg_ref[...] = jnp.zeros_like(dg_ref)

        dg_ref[...] += jnp.sum(dgt, axis=0, keepdims=True)

    return pl.pallas_call(
        body, name="inproj_bwd_x", grid=(T // tm,),
        in_specs=[pl.BlockSpec((tm, NP), lambda n: (n, 0)),
                  pl.BlockSpec((NP, D_MODEL), lambda n: (0, 0)),
                  pl.BlockSpec((tm, D_MODEL), lambda n: (n, 0)),
                  pl.BlockSpec((1, D_MODEL), lambda n: (0, 0)),
                  pl.BlockSpec((tm, D_MODEL), lambda n: (n, 0))],
        out_specs=[pl.BlockSpec((tm, D_MODEL), lambda n: (n, 0)),
                   pl.BlockSpec((1, D_MODEL), lambda n: (0, 0))],
        out_shape=[jax.ShapeDtypeStruct((T, D_MODEL), F32), jax.ShapeDtypeStruct((1, D_MODEL), F32)],
        compiler_params=_params(1))(dh, wpt, x, g, dxo)


def _matmul_tn(a, b, tn, name):
    T, M = a.shape
    N = b.shape[1]
    tk = 512 if T % 512 == 0 else T

    def body(a_ref, b_ref, o_ref):
        k = pl.program_id(1)

        @pl.when(k == 0)
        def _():
            o_ref[...] = jnp.zeros_like(o_ref)

        o_ref[...] += _dot_tn(a_ref[...], b_ref[...])

    return pl.pallas_call(
        body, name=name, grid=(N // tn, T // tk),
        in_specs=[pl.BlockSpec((tk, M), lambda j, k: (k, 0)),
                  pl.BlockSpec((tk, tn), lambda j, k: (k, j))],
        out_specs=pl.BlockSpec((M, tn), lambda j, k: (0, j)),
        out_shape=jax.ShapeDtypeStruct((M, N), F32),
        compiler_params=_params(2))(a, b)


def _swa_probs(q, kc, kp, sink, mask_c, mask_p):
    sc = jnp.where(mask_c, _dot_nt(q, kc) * SB_SCALE, NEG)
    sp = jnp.where(mask_p, _dot_nt(q, kp) * SB_SCALE, NEG)
    m = jnp.maximum(jnp.maximum(jnp.max(sc, axis=-1, keepdims=True), jnp.max(sp, axis=-1, keepdims=True)), sink)
    pc = jnp.exp(sc - m)
    pp = jnp.exp(sp - m)
    ps = jnp.exp(sink - m)
    inv = 1.0 / (jnp.sum(pc, axis=-1, keepdims=True) + jnp.sum(pp, axis=-1, keepdims=True) + ps)
    return pc * inv, pp * inv, ps * inv


def _swa_masks(n, tq):
    row = _iota((tq, tq), 0)
    col = _iota((tq, tq), 1)
    return col <= row, jnp.logical_and(col > row, n > 0)


def _swa_specs(tq):
    return [pl.BlockSpec(memory_space=pltpu.SMEM),
            pl.BlockSpec((tq, 256), lambda n: (n, C_AQ // 256)),
            pl.BlockSpec((tq, 128), lambda n: (n, C_AK // 128)),
            pl.BlockSpec((tq, 128), lambda n: (jnp.maximum(n - 1, 0), C_AK // 128)),
            pl.BlockSpec((tq, 128), lambda n: (n, C_AV // 128)),
            pl.BlockSpec((tq, 128), lambda n: (jnp.maximum(n - 1, 0), C_AV // 128))]


def _swa_fwd(h16, sinks):
    T = h16.shape[0]
    tq = 128

    def body(s_ref, q_ref, kc_ref, kp_ref, vc_ref, vp_ref, o_ref):
        n = pl.program_id(0)
        mask_c, mask_p = _swa_masks(n, tq)
        for h in range(4):
            g = h // 2
            q = q_ref[:, h * 64:(h + 1) * 64]
            pc, pp, _ = _swa_probs(q, kc_ref[:, g * 64:(g + 1) * 64], kp_ref[:, g * 64:(g + 1) * 64],
                                   s_ref[h], mask_c, mask_p)
            o_ref[:, h * 64:(h + 1) * 64] = (_dot(pc.astype(BF), vc_ref[:, g * 64:(g + 1) * 64])
                                             + _dot(pp.astype(BF), vp_ref[:, g * 64:(g + 1) * 64]))

    return pl.pallas_call(
        body, name="swa_fwd", grid=(T // tq,), in_specs=_swa_specs(tq),
        out_specs=pl.BlockSpec((tq, 256), lambda n: (n, 0)),
        out_shape=jax.ShapeDtypeStruct((T, 256), F32),
        compiler_params=_params(1))(sinks, h16, h16, h16, h16, h16)


def _swa_bwd(h16, sinks, dya):
    T = h16.shape[0]
    tq = 128

    def body(s_ref, q_ref, kc_ref, kp_ref, vc_ref, vp_ref, do_ref, dq_ref, dk_ref, dv_ref, ds_ref):
        n = pl.program_id(0)

        @pl.when(n == 0)
        def _():
            dk_ref[...] = jnp.zeros_like(dk_ref)
            dv_ref[...] = jnp.zeros_like(dv_ref)
            ds_ref[...] = jnp.zeros_like(ds_ref)

        mask_c, mask_p = _swa_masks(n, tq)
        rc = pl.ds(pl.multiple_of(n * tq, tq), tq)
        rp = pl.ds(pl.multiple_of(jnp.maximum(n - 1, 0) * tq, tq), tq)
        lane = _iota((8, 128), 1)
        row8 = _iota((8, 128), 0)
        for h in range(4):
            g = h // 2
            hs = slice(h * 64, (h + 1) * 64)
            gs = slice(g * 64, (g + 1) * 64)
            q = q_ref[:, hs]
            kc, kp, vc, vp = kc_ref[:, gs], kp_ref[:, gs], vc_ref[:, gs], vp_ref[:, gs]
            pc, pp, ps = _swa_probs(q, kc, kp, s_ref[h], mask_c, mask_p)
            pcb, ppb = pc.astype(BF), pp.astype(BF)
            do = do_ref[:, hs]
            dob = do.astype(BF)
            o = _dot(pcb, vc) + _dot(ppb, vp)
            dd = jnp.sum(do * o, axis=-1, keepdims=True)
            dsc = (pc * (_dot_nt(dob, vc) - dd) * SB_SCALE).astype(BF)
            dsp = (pp * (_dot_nt(dob, vp) - dd) * SB_SCALE).astype(BF)
            dq_ref[:, hs] = (_dot(dsc, kc) + _dot(dsp, kp)).astype(BF)
            dk_ref[rc, gs] += _dot_tn(dsc, q)
            dk_ref[rp, gs] += _dot_tn(dsp, q)
            dv_ref[rc, gs] += _dot_tn(pcb, dob)
            dv_ref[rp, gs] += _dot_tn(ppb, dob)
            dsink = -jnp.sum(ps * dd)
            ds_ref[...] += jnp.where(jnp.logical_and(lane == h, row8 == 0), dsink, 0.0)

    return pl.pallas_call(
        body, name="swa_bwd", grid=(T // tq,),
        in_specs=_swa_specs(tq) + [pl.BlockSpec((tq, 256), lambda n: (n, 0))],
        out_specs=[pl.BlockSpec((tq, 256), lambda n: (n, 0)),
                   pl.BlockSpec((T, 128), lambda n: (0, 0)),
                   pl.BlockSpec((T, 128), lambda n: (0, 0)),
                   pl.BlockSpec((8, 128), lambda n: (0, 0))],
        out_shape=[jax.ShapeDtypeStruct((T, 256), BF), jax.ShapeDtypeStruct((T, 128), F32),
                   jax.ShapeDtypeStruct((T, 128), F32), jax.ShapeDtypeStruct((8, 128), F32)],
        compiler_params=_params(1))(sinks, h16, h16, h16, h16, h16, dya)


def _conv_u(bc_ref, bx_ref, bch_ref, bxh_ref, n, tm):
    u = bc_ref[...] * bx_ref[...]
    uh = bch_ref[...] * bxh_ref[...] * (n > 0).astype(F32)
    rowi = _iota((tm, 256), 0)
    u1 = jnp.where(rowi == 0, uh[7:8, :], pltpu.roll(u, 1, axis=0))
    u2 = jnp.where(rowi == 0, uh[6:7, :], jnp.where(rowi == 1, uh[7:8, :], pltpu.roll(u, 2, axis=0)))
    return u, u1, u2


def _conv_fwd(h32, cw, cb):
    T = h32.shape[0]
    tm = 512 if T % 512 == 0 else T
    hb = tm // 8

    def body(bb_ref, bc_ref, bx_ref, bch_ref, bxh_ref, w_ref, b_ref, o_ref):
        n = pl.program_id(0)
        u, u1, u2 = _conv_u(bc_ref, bx_ref, bch_ref, bxh_ref, n, tm)
        y = w_ref[0:1, :] * u2 + w_ref[1:2, :] * u1 + w_ref[2:3, :] * u + b_ref[...]
        o_ref[...] = bb_ref[...] * y

    halo = lambda c: pl.BlockSpec((8, 256), lambda n: (jnp.maximum(n * hb - 1, 0), c // 256))
    return pl.pallas_call(
        body, name="conv_fwd", grid=(T // tm,),
        in_specs=[pl.BlockSpec((tm, 256), lambda n: (n, C_BB // 256)),
                  pl.BlockSpec((tm, 256), lambda n: (n, C_BC // 256)),
                  pl.BlockSpec((tm, 256), lambda n: (n, C_BX // 256)),
                  halo(C_BC), halo(C_BX),
                  pl.BlockSpec((8, 256), lambda n: (0, 0)),
                  pl.BlockSpec((1, 256), lambda n: (0, 0))],
        out_specs=pl.BlockSpec((tm, 256), lambda n: (n, 0)),
        out_shape=jax.ShapeDtypeStruct((T, 256), F32),
        compiler_params=_params(1))(h32, h32, h32, h32, h32, cw, cb)


def _conv_bwd(h32, cw, cb, dyb):
    T = h32.shape[0]
    tm = 512 if T % 512 == 0 else T
    hb = tm // 8
    nt = T // tm

    def body(bb_ref, bc_ref, bx_ref, bch_ref, bxh_ref, bbn_ref, dy_ref, dyn_ref, w_ref, b_ref,
             dbb_ref, dbc_ref, dbx_ref, dw_ref):
        n = pl.program_id(0)

        @pl.when(n == 0)
        def _():
            dw_ref[...] = jnp.zeros_like(dw_ref)

        u, u1, u2 = _conv_u(bc_ref, bx_ref, bch_ref, bxh_ref, n, tm)
        w0, w1, w2 = w_ref[0:1, :], w_ref[1:2, :], w_ref[2:3, :]
        y = w0 * u2 + w1 * u1 + w2 * u + b_ref[...]
        dyb_ = dy_ref[...]
        dbb_ref[...] = (dyb_ * y).astype(BF)
        dy = dyb_ * bb_ref[...]
        dyn = dyn_ref[...] * bbn_ref[...] * (n < nt - 1).astype(F32)
        rowi = _iota((tm, 256), 0)
        dy1 = jnp.where(rowi == tm - 1, dyn[0:1, :], pltpu.roll(dy, tm - 1, axis=0))
        dy2 = jnp.where(rowi == tm - 2, dyn[0:1, :],
                        jnp.where(rowi == tm - 1, dyn[1:2, :], pltpu.roll(dy, tm - 2, axis=0)))
        du = w2 * dy + w1 * dy1 + w0 * dy2
        dbc_ref[...] = (du * bx_ref[...]).astype(BF)
        dbx_ref[...] = (du * bc_ref[...]).astype(BF)
        dw_ref[0:1, :] += jnp.sum(dy * u2, axis=0, keepdims=True)
        dw_ref[1:2, :] += jnp.sum(dy * u1, axis=0, keepdims=True)
        dw_ref[2:3, :] += jnp.sum(dy * u, axis=0, keepdims=True)
        dw_ref[3:4, :] += jnp.sum(dy, axis=0, keepdims=True)

    halo = lambda c: pl.BlockSpec((8, 256), lambda n: (jnp.maximum(n * hb - 1, 0), c // 256))
    nxt = lambda c: pl.BlockSpec((8, 256), lambda n: (jnp.minimum((n + 1) * hb, T // 8 - 1), c // 256))
    cur = lambda c: pl.BlockSpec((tm, 256), lambda n: (n, c // 256))
    return pl.pallas_call(
        body, name="conv_bwd", grid=(nt,),
        in_specs=[cur(C_BB), cur(C_BC), cur(C_BX), halo(C_BC), halo(C_BX), nxt(C_BB),
                  cur(0), nxt(0),
                  pl.BlockSpec((8, 256), lambda n: (0, 0)),
                  pl.BlockSpec((1, 256), lambda n: (0, 0))],
        out_specs=[cur(0), cur(0), cur(0), pl.BlockSpec((8, 256), lambda n: (0, 0))],
        out_shape=[jax.ShapeDtypeStruct((T, 256), BF)] * 3 + [jax.ShapeDtypeStruct((8, 256), F32)],
        compiler_params=_params(1))(h32, h32, h32, h32, h32, h32, dyb, dyb, cw, cb)


def _cprep_specs(tm):
    return [pl.BlockSpec((tm, 256), lambda n: (n, C_CQ // 256)),
            pl.BlockSpec((tm, 128), lambda n: (n, C_CKV // 128)),
            pl.BlockSpec((tm, 128), lambda n: (n, C_CKR // 128)),
            pl.BlockSpec((tm, 128), lambda n: (n, C_CKRS // 128)),
            pl.BlockSpec((1, 256), lambda n: (0, 0)),
            pl.BlockSpec((1, 128), lambda n: (0, 0)),
            pl.BlockSpec((tm, 128), lambda n: (n, 0)),
            pl.BlockSpec((tm, 128), lambda n: (n, 0))]


def _cprep_fwd(h32, gq, gkv, wuq2, wkv2, cosk, sin):
    T = h32.shape[0]
    tm = 512 if T % 512 == 0 else T

    def body(cq_ref, ckv_ref, ckr_ref, ckrs_ref, gq_ref, gkv_ref, cos_ref, sin_ref, wuq_ref, wkv_ref,
             q_ref, k_ref, v_ref):
        cosk_, sin_ = cos_ref[...], sin_ref[...]
        cosq = cosk_ + (_iota((tm, 128), 1) < 64).astype(F32)
        cqn, _ = _rms_fwd(cq_ref[...], gq_ref[...])
        q2 = _dot(cqn.astype(BF), wuq_ref[...])
        ckvn, _ = _rms_fwd(ckv_ref[...], gkv_ref[...])
        kv2 = _dot(ckvn.astype(BF), wkv_ref[...])
        kr = ckr_ref[...] * cosk_ + ckrs_ref[...] * sin_
        for h in range(4):
            hs = slice(h * 128, (h + 1) * 128)
            q_ref[:, hs] = ((q2[:, hs] * cosq + q2[:, 512 + h * 128:512 + (h + 1) * 128] * sin_) * MLA_SCALE).astype(BF)
            k_ref[:, hs] = (kv2[:, hs] + kr).astype(BF)
        v_ref[...] = kv2[:, 512:].astype(BF)

    return pl.pallas_call(
        body, name="cprep_fwd", grid=(T // tm,),
        in_specs=_cprep_specs(tm) + [pl.BlockSpec((256, 1024), lambda n: (0, 0)),
                                     pl.BlockSpec((128, 1024), lambda n: (0, 0))],
        out_specs=[pl.BlockSpec((tm, 512), lambda n: (n, 0))] * 3,
        out_shape=[jax.ShapeDtypeStruct((T, 512), BF)] * 3,
        compiler_params=_params(1))(h32, h32, h32, h32, gq, gkv, cosk, sin, wuq2, wkv2)


def _cprep_bwd(h32, gq, gkv, wuq2t, wkv2t, cosk, sin, dq, dk, dv):
    T = h32.shape[0]
    tm = 512 if T % 512 == 0 else T

    def body(cq_ref, ckv_ref, ckr_ref, ckrs_ref, gq_ref, gkv_ref, cos_ref, sin_ref, wuq_ref, wkv_ref,
             dq_ref, dk_ref, dv_ref,
             dcq_ref, dckv_ref, dckr_ref, dckrs_ref, dq2_ref, dkv2_ref, cqn_ref, ckvn_ref, dgq_ref, dgkv_ref):
        n = pl.program_id(0)

        @pl.when(n == 0)
        def _():
            dgq_ref[...] = jnp.zeros_like(dgq_ref)
            dgkv_ref[...] = jnp.zeros_like(dgkv_ref)

        cosk_, sin_ = cos_ref[...], sin_ref[...]
        cosq = cosk_ + (_iota((tm, 128), 1) < 64).astype(F32)
        dkr = jnp.zeros((tm, 128), F32)
        for h in range(4):
            hs = slice(h * 128, (h + 1) * 128)
            dqh = dq_ref[:, hs] * MLA_SCALE
            dq2_ref[:, hs] = (dqh * cosq).astype(BF)
            dq2_ref[:, 512 + h * 128:512 + (h + 1) * 128] = (dqh * sin_).astype(BF)
            dkr = dkr + dk_ref[:, hs]
        dkv2_ref[:, :512] = dk_ref[...].astype(BF)
        dkv2_ref[:, 512:] = dv_ref[...].astype(BF)
        dckr_ref[...] = (dkr * cosk_).astype(BF)
        dckrs_ref[...] = (dkr * sin_).astype(BF)

        cq, gq_ = cq_ref[...], gq_ref[...]
        cqn, rq = _rms_fwd(cq, gq_)
        cqn_ref[...] = cqn.astype(BF)
        dcq, dgt = _rms_bwd(cq, gq_, rq, _dot(dq2_ref[...], wuq_ref[...]))
        dcq_ref[...] = dcq.astype(BF)
        dgq_ref[...] += jnp.sum(dgt, axis=0, keepdims=True)

        ckv, gkv_ = ckv_ref[...], gkv_ref[...]
        ckvn, rkv = _rms_fwd(ckv, gkv_)
        ckvn_ref[...] = ckvn.astype(BF)
        dckv, dgt2 = _rms_bwd(ckv, gkv_, rkv, _dot(dkv2_ref[...], wkv_ref[...]))
        dckv_ref[...] = dckv.astype(BF)
        dgkv_ref[...] += jnp.sum(dgt2, axis=0, keepdims=True)

    row = lambda w: pl.BlockSpec((tm, w), lambda n: (n, 0))
    return pl.pallas_call(
        body, name="cprep_bwd", grid=(T // tm,),
        in_specs=_cprep_specs(tm) + [pl.BlockSpec((1024, 256), lambda n: (0, 0)),
                                     pl.BlockSpec((1024, 128), lambda n: (0, 0)),
                                     row(512), row(512), row(512)],
        out_specs=[row(256), row(128), row(128), row(128), row(1024), row(1024), row(256), row(128),
                   pl.BlockSpec((1, 256), lambda n: (0, 0)), pl.BlockSpec((1, 128), lambda n: (0, 0))],
        out_shape=[jax.ShapeDtypeStruct((T, 256), BF), jax.ShapeDtypeStruct((T, 128), BF),
                   jax.ShapeDtypeStruct((T, 128), BF), jax.ShapeDtypeStruct((T, 128), BF),
                   jax.ShapeDtypeStruct((T, 1024), BF), jax.ShapeDtypeStruct((T, 1024), BF),
                   jax.ShapeDtypeStruct((T, 256), BF), jax.ShapeDtypeStruct((T, 128), BF),
                   jax.ShapeDtypeStruct((1, 256), F32), jax.ShapeDtypeStruct((1, 128), F32)],
        compiler_params=_params(1))(h32, h32, h32, h32, gq, gkv, cosk, sin, wuq2t, wkv2t, dq, dk, dv)


MLA_TILE = 512


def _causal_mask(t):
    return _iota((t, t), 1) <= _iota((t, t), 0)


def _mla_fwd(q, k, v):
    T = q.shape[0]
    tq = MLA_TILE

    def body(q_ref, k_ref, v_ref, o_ref, lse_ref):
        i = pl.program_id(1)
        qv = q_ref[...]
        mask = _causal_mask(tq)

        def step(j, carry, masked):
            m, l, acc = carry
            rows = pl.ds(pl.multiple_of(j * tq, tq), tq)
            s = _dot_nt(qv, k_ref[rows, :])
            if masked:
                s = jnp.where(mask, s, NEG)
            m_new = jnp.maximum(m, jnp.max(s, axis=-1, keepdims=True))
            p = jnp.exp(s - m_new)
            alpha = jnp.exp(m - m_new)
            l = alpha * l + jnp.sum(p, axis=-1, keepdims=True)
            acc = alpha * acc + _dot(p.astype(BF), v_ref[rows, :])
            return m_new, l, acc

        init = (jnp.full((tq, 1), NEG, F32), jnp.zeros((tq, 1), F32), jnp.zeros((tq, 128), F32))
        carry = lax.fori_loop(0, i, lambda j, c: step(j, c, False), init)
        m, l, acc = step(i, carry, True)
        o_ref[...] = acc * (1.0 / l)
        lse_ref[...] = jnp.broadcast_to(m + jnp.log(l), (tq, 128))

    blk = pl.BlockSpec((tq, 128), lambda h, i: (i, h))
    full = pl.BlockSpec((T, 128), lambda h, i: (0, h))
    return pl.pallas_call(
        body, name="mla_fwd", grid=(4, T // tq), in_specs=[blk, full, full], out_specs=[blk, blk],
        out_shape=[jax.ShapeDtypeStruct((T, 512), F32), jax.ShapeDtypeStruct((T, 512), F32)],
        compiler_params=_params(2))(q, k, v)


def _mla_bwd(q, k, v, o, lse, do):
    T = q.shape[0]
    tq = MLA_TILE

    def body(q_ref, k_ref, v_ref, o_ref, lse_ref, do_ref, dq_ref, dk_ref, dv_ref):
        i = pl.program_id(1)

        @pl.when(i == 0)
        def _():
            dk_ref[...] = jnp.zeros_like(dk_ref)
            dv_ref[...] = jnp.zeros_like(dv_ref)

        qv = q_ref[...]
        do = do_ref[...]
        dob = do.astype(BF)
        dd = jnp.sum(do * o_ref[...], axis=-1, keepdims=True)
        lse_ = lse_ref[:, 0:1]
        mask = _causal_mask(tq)

        def step(j, dq, masked):
            rows = pl.ds(pl.multiple_of(j * tq, tq), tq)
            kj, vj = k_ref[rows, :], v_ref[rows, :]
            s = _dot_nt(qv, kj)
            if masked:
                s = jnp.where(mask, s, NEG)
            p = jnp.exp(s - lse_)
            ds = (p * (_dot_nt(dob, vj) - dd)).astype(BF)
            dk_ref[rows, :] += _dot_tn(ds, qv)
            dv_ref[rows, :] += _dot_tn(p.astype(BF), dob)
            return dq + _dot(ds, kj)

        dq = lax.fori_loop(0, i, lambda j, c: step(j, c, False), jnp.zeros((tq, 128), F32))
        dq_ref[...] = step(i, dq, True)

    blk = pl.BlockSpec((tq, 128), lambda h, i: (i, h))
    full = pl.BlockSpec((T, 128), lambda h, i: (0, h))
    return pl.pallas_call(
        body, name="mla_bwd", grid=(4, T // tq), in_specs=[blk, full, full, blk, blk, blk],
        out_specs=[blk, full, full],
        out_shape=[jax.ShapeDtypeStruct((T, 512), F32)] * 3,
        compiler_params=_params(2))(q, k, v, o, lse, do)


def _sb_tile(qm, kj, rr, strict, masked, upper):
    z = _dot_nt(qm, kj) * SB_SCALE
    sp = jnp.maximum(z, 0.0) + jnp.log(1.0 + jnp.exp(-jnp.abs(z)))
    lk = -sp
    if masked:
        lk = jnp.where(strict, lk, 0.0)
    after = rr + _dot(lk.astype(BF), upper)
    ll = z - sp
    a = jnp.exp(ll + after)
    if masked:
        a = jnp.where(strict, a, 0.0)
    return z, ll, a, jnp.sum(lk, axis=-1, keepdims=True)


SB_TQ, SB_TK = 512, 256


def _sb_consts(tq, tk):
    row, col = _iota((tq, tk), 0), _iota((tq, tk), 1)
    strict = [col + d * tk < row for d in range(tq // tk)]
    r2, c2 = _iota((tk, tk), 0), _iota((tk, tk), 1)
    return strict, (r2 > c2).astype(BF), (r2 < c2).astype(BF)


def _sb_fwd(h16):
    T = h16.shape[0]
    tq, tk = SB_TQ, SB_TK
    nd = tq // tk

    def body(q_ref, k_ref, v_ref, o_ref):
        i = pl.program_id(1)
        strict, upper, _ = _sb_consts(tq, tk)
        lane = _iota((tq, 128), 1)
        q2 = q_ref[...]
        qms = [jnp.where(lane < 64, q2, jnp.zeros_like(q2)), jnp.where(lane >= 64, q2, jnp.zeros_like(q2))]

        def step(j, carry, d):
            rows = pl.ds(pl.multiple_of(j * tk, tk), tk)
            kj, vj = k_ref[rows, :], v_ref[rows, :]
            out = []
            for hh in range(2):
                rr, acc = carry[hh]
                _, _, a, rs = _sb_tile(qms[hh], kj, rr, None if d is None else strict[d], d is not None, upper)
                out.append((rr + rs, acc + _dot(a.astype(BF), vj)))
            return tuple(out)

        carry = ((jnp.zeros((tq, 1), F32), jnp.zeros((tq, 128), F32)),) * 2
        for d in reversed(range(nd)):
            carry = step(nd * i + d, carry, d)
        carry = lax.fori_loop(0, nd * i, lambda t, c: step(nd * i - 1 - t, c, None), carry)
        o_ref[...] = jnp.where(lane < 64, carry[0][1], carry[1][1])

    return pl.pallas_call(
        body, name="sb_fwd", grid=(2, T // tq),
        in_specs=[pl.BlockSpec((tq, 128), lambda p, i: (i, C_DQ // 128 + p)),
                  pl.BlockSpec((T, 128), lambda p, i: (0, C_DK // 128 + p)),
                  pl.BlockSpec((T, 128), lambda p, i: (0, C_DV // 128 + p))],
        out_specs=pl.BlockSpec((tq, 128), lambda p, i: (i, p)),
        out_shape=jax.ShapeDtypeStruct((T, 256), F32),
        compiler_params=_params(2))(h16, h16, h16)


def _sb_bwd(h16, yd, dyd):
    T = h16.shape[0]
    tq, tk = SB_TQ, SB_TK
    nd = tq // tk

    def body(q_ref, k_ref, v_ref, o_ref, do_ref, dq_ref, dk_ref, dv_ref):
        i = pl.program_id(1)

        @pl.when(i == 0)
        def _():
            dk_ref[...] = jnp.zeros_like(dk_ref)
            dv_ref[...] = jnp.zeros_like(dv_ref)

        strict, upper, before = _sb_consts(tq, tk)
        lane = _iota((tq, 128), 1)
        lane_k = _iota((tk, 128), 1)
        q2 = q_ref[...]
        dob2 = do_ref[...].astype(BF)
        doo = dob2.astype(F32) * o_ref[...]
        mines = [lane < 64, lane >= 64]
        qms = [jnp.where(m, q2, jnp.zeros_like(q2)) for m in mines]
        doms = [jnp.where(m, dob2, jnp.zeros_like(dob2)) for m in mines]
        dds = [jnp.sum(jnp.where(m, doo, 0.0), axis=-1, keepdims=True) for m in mines]

        def step(j, carry, d):
            rows = pl.ds(pl.multiple_of(j * tk, tk), tk)
            kj, vj = k_ref[rows, :], v_ref[rows, :]
            out, dks, dvs = [], [], []
            for hh in range(2):
                rr, sg, dq = carry[hh]
                z, ll, a, rs = _sb_tile(qms[hh], kj, rr, None if d is None else strict[d], d is not None, upper)
                ab = a.astype(BF)
                g = _dot_nt(doms[hh], vj) * ab.astype(F32)
                gs = jnp.sum(g, axis=-1, keepdims=True)
                pre = (dds[hh] - sg - gs) + _dot(g.astype(BF), before)
                sig = jnp.exp(ll)
                dz = g * (1.0 - sig) - sig * pre
                if d is not None:
                    dz = jnp.where(strict[d], dz, 0.0)
                dzb = (dz * SB_SCALE).astype(BF)
                dks.append(_dot_tn(dzb, q2))
                dvs.append(_dot_tn(ab, dob2))
                out.append((rr + rs, sg + gs, dq + _dot(dzb, kj)))
            dk_ref[rows, :] += jnp.where(lane_k < 64, dks[0], dks[1])
            dv_ref[rows, :] += jnp.where(lane_k < 64, dvs[0], dvs[1])
            return tuple(out)

        zero = jnp.zeros((tq, 1), F32)
        carry = ((zero, zero, jnp.zeros((tq, 128), F32)),) * 2
        for d in reversed(range(nd)):
            carry = step(nd * i + d, carry, d)
        carry = lax.fori_loop(0, nd * i, lambda t, c: step(nd * i - 1 - t, c, None), carry)
        dq_ref[...] = jnp.where(lane < 64, carry[0][2], carry[1][2])

    blk = lambda c: pl.BlockSpec((tq, 128), lambda p, i: (i, c // 128 + p))
    full = lambda c: pl.BlockSpec((T, 128), lambda p, i: (0, c // 128 + p))
    return pl.pallas_call(
        body, name="sb_bwd", grid=(2, T // tq),
        in_specs=[blk(C_DQ), full(C_DK), full(C_DV), blk(0), blk(0)],
        out_specs=[blk(0), full(0), full(0)],
        out_shape=[jax.ShapeDtypeStruct((T, 256), F32)] * 3,
        compiler_params=_params(2))(h16, h16, h16, yd, dyd)


def _compact_c(ycp):
    return jnp.concatenate([ycp[:, h * 128:h * 128 + 64] for h in range(4)], axis=1)


def _post_fwd(ya, yb, ycp, yd, h32, ggrp, wout, gpost, x):
    T = x.shape[0]
    tm = 256

    def body(ya_ref, yb_ref, yc_ref, yd_ref, gate_ref, gg_ref, w_ref, gp_ref, x_ref, xn_ref, ym_ref, o_ref):
        ys = [ya_ref[...], yb_ref[...], _compact_c(yc_ref[...]), yd_ref[...]]
        gate = gate_ref[...]
        sil = gate * (1.0 / (1.0 + jnp.exp(-gate)))
        parts = []
        for gi in range(4):
            ng, _ = _rms_fwd(ys[gi], gg_ref[:, gi * 256:(gi + 1) * 256])
            parts.append(ng * sil[:, gi * 256:(gi + 1) * 256])
        ym = jnp.concatenate(parts, axis=1).astype(BF)
        ym_ref[...] = ym
        o = _dot(ym, w_ref[...])
        o_ref[...] = o
        on, _ = _rms_fwd(o, gp_ref[...])
        xn_ref[...] = x_ref[...] + on

    row = lambda w: pl.BlockSpec((tm, w), lambda n: (n, 0))
    vec = pl.BlockSpec((1, 1024), lambda n: (0, 0))
    return pl.pallas_call(
        body, name="post_fwd", grid=(T // tm,),
        in_specs=[row(256), row(256), row(512), row(256), pl.BlockSpec((tm, 1024), lambda n: (n, C_GATE // 1024)),
                  vec, pl.BlockSpec((1024, 1024), lambda n: (0, 0)), vec, row(1024)],
        out_specs=[row(1024), row(1024), row(1024)],
        out_shape=[jax.ShapeDtypeStruct((T, 1024), F32), jax.ShapeDtypeStruct((T, 1024), BF),
                   jax.ShapeDtypeStruct((T, 1024), F32)],
        compiler_params=_params(1))(ya, yb, ycp, yd, h32, ggrp, wout, gpost, x)


def _post_bwd(dx, o, gpost, woutt, ya, yb, ycp, yd, h32, ggrp):
    T = dx.shape[0]
    tm = 256

    def body(dx_ref, o_ref, gp_ref, w_ref, ya_ref, yb_ref, yc_ref, yd_ref, gate_ref, gg_ref,
             do_ref, dya_ref, dyb_ref, dyc_ref, dyd_ref, dgate_ref, dgp_ref, dgg_ref):
        n = pl.program_id(0)

        @pl.when(n == 0)
        def _():
            dgp_ref[...] = jnp.zeros_like(dgp_ref)
            dgg_ref[...] = jnp.zeros_like(dgg_ref)

        ov, gp = o_ref[...], gp_ref[...]
        _, ro = _rms_fwd(ov, gp)
        do, dgt = _rms_bwd(ov, gp, ro, dx_ref[...])
        dgp_ref[...] += jnp.sum(dgt, axis=0, keepdims=True)
        dob = do.astype(BF)
        do_ref[...] = dob
        dym = _dot(dob, w_ref[...])
        gate = gate_ref[...]
        sg = 1.0 / (1.0 + jnp.exp(-gate))
        sil = gate * sg
        dsil = sg * (1.0 + gate * (1.0 - sg))
        ys = [ya_ref[...], yb_ref[...], _compact_c(yc_ref[...]), yd_ref[...]]
        dys = []
        for gi in range(4):
            gs = slice(gi * 256, (gi + 1) * 256)
            gg = gg_ref[:, gs]
            ng, rg = _rms_fwd(ys[gi], gg)
            dgate_ref[:, gs] = (dym[:, gs] * ng * dsil[:, gs]).astype(BF)
            dy, dgt2 = _rms_bwd(ys[gi], gg, rg, dym[:, gs] * sil[:, gs])
            dgg_ref[:, gs] += jnp.sum(dgt2, axis=0, keepdims=True)
            dys.append(dy)
        dya_ref[...] = dys[0]
        dyb_ref[...] = dys[1]
        dyd_ref[...] = dys[3]
        z64 = jnp.zeros((tm, 64), F32)
        dyc_ref[...] = jnp.concatenate(
            [piece for h in range(4) for piece in (dys[2][:, h * 64:(h + 1) * 64], z64)], axis=1)

    row = lambda w: pl.BlockSpec((tm, w), lambda n: (n, 0))
    vec = pl.BlockSpec((1, 1024), lambda n: (0, 0))
    return pl.pallas_call(
        body, name="post_bwd", grid=(T // tm,),
        in_specs=[row(1024), row(1024), vec, pl.BlockSpec((1024, 1024), lambda n: (0, 0)),
                  row(256), row(256), row(512), row(256),
                  pl.BlockSpec((tm, 1024), lambda n: (n, C_GATE // 1024)), vec],
        out_specs=[row(1024), row(256), row(256), row(512), row(256), row(1024), vec, vec],
        out_shape=[jax.ShapeDtypeStruct((T, 1024), BF), jax.ShapeDtypeStruct((T, 256), F32),
                   jax.ShapeDtypeStruct((T, 256), F32), jax.ShapeDtypeStruct((T, 512), F32),
                   jax.ShapeDtypeStruct((T, 256), F32), jax.ShapeDtypeStruct((T, 1024), BF),
                   jax.ShapeDtypeStruct((1, 1024), F32), jax.ShapeDtypeStruct((1, 1024), F32)],
        compiler_params=_params(1))(dx, o, gpost, woutt, ya, yb, ycp, yd, h32, ggrp)


def _loss_head(y, tgt):
    T = y.shape[0]
    tm = 512 if T % 512 == 0 else T

    def body(y_ref, t_ref, s_ref, dy_ref):
        n = pl.program_id(0)

        @pl.when(n == 0)
        def _():
            s_ref[...] = jnp.zeros_like(s_ref)

        d = y_ref[...] - t_ref[...]
        s_ref[...] += jnp.sum(d * d, axis=0, keepdims=True)
        dy_ref[...] = d * (1.0 / D_MODEL)

    row = pl.BlockSpec((tm, 1024), lambda n: (n, 0))
    return pl.pallas_call(
        body, name="loss_head", grid=(T // tm,), in_specs=[row, row],
        out_specs=[pl.BlockSpec((1, 1024), lambda n: (0, 0)), row],
        out_shape=[jax.ShapeDtypeStruct((1, 1024), F32), jax.ShapeDtypeStruct((T, 1024), F32)],
        compiler_params=_params(1))(y, tgt)


def _pad_w_in(w):
    z = lambda n: jnp.zeros((w.shape[0], n), w.dtype)
    kr = w[:, 1664:1696]
    return jnp.concatenate([w[:, 2464:3488], w[:, 0:1664], z(64), kr, z(32), z(64), _swap32(kr), z(32),
                            w[:, 1696:2464], z(NP - C_END)], axis=1)


def _unpad_w_in(d):
    kr = d[:, C_CKR + 64:C_CKR + 96] + _swap32(d[:, C_CKRS + 64:C_CKRS + 96])
    return jnp.concatenate([d[:, C_AQ:C_CKR], kr, d[:, C_DQ:C_END], d[:, 0:1024]], axis=1)


def _pad_w_uq(w):
    z = lambda n: jnp.zeros((w.shape[0], n), w.dtype)
    a = [p for h in range(4) for p in (w[:, 96 * h:96 * h + 96], z(32))]
    b = [p for h in range(4) for p in (z(64), _swap32(w[:, 96 * h + 64:96 * h + 96]), z(32))]
    return jnp.concatenate(a + b, axis=1)


def _unpad_w_uq(d):
    out = []
    for h in range(4):
        out.append(d[:, 128 * h:128 * h + 64])
        out.append(d[:, 128 * h + 64:128 * h + 96] + _swap32(d[:, 512 + 128 * h + 64:512 + 128 * h + 96]))
    return jnp.concatenate(out, axis=1)


def _pad_w_ukv(w):
    z = jnp.zeros((w.shape[0], 64), w.dtype)
    a = [p for h in range(4) for p in (w[:, 128 * h:128 * h + 64], z)]
    b = [p for h in range(4) for p in (w[:, 128 * h + 64:128 * h + 128], z)]
    return jnp.concatenate(a + b, axis=1)


def _unpad_w_ukv(d):
    return jnp.concatenate([p for h in range(4) for p in (d[:, 128 * h:128 * h + 64],
                                                          d[:, 512 + 128 * h:512 + 128 * h + 64])], axis=1)


def _rope_tables(pos):
    freqs = 10000.0 ** (-jnp.arange(16, dtype=F32) / 16)
    ang = pos.astype(F32)[:, None] * freqs
    c, s = jnp.cos(ang), jnp.sin(ang)
    z = lambda n: jnp.zeros((pos.shape[0], n), F32)
    return (jnp.concatenate([z(64), c, c, z(32)], axis=1), jnp.concatenate([z(64), -s, s, z(32)], axis=1))


def _layer_weights(W, l):
    wp = _pad_w_in(W["w_in"][l])
    wuq2 = _pad_w_uq(W["mla_w_uq"][l])
    wkv2 = _pad_w_ukv(W["mla_w_ukv"][l])
    wout = W["w_out"][l]
    cw = jnp.concatenate([W["conv_w"][l].astype(F32), jnp.zeros((5, 256), F32)], axis=0)
    return dict(
        wp=wp.astype(BF), wpt=wp.T.astype(BF), wuq2=wuq2.astype(BF), wuq2t=wuq2.T.astype(BF),
        wkv2=wkv2.astype(BF), wkv2t=wkv2.T.astype(BF), wout=wout.astype(BF), woutt=wout.T.astype(BF),
        cw=cw, cb=W["conv_b"][l][None, :], sinks=W["attn_sinks"][l],
        gpre=W["norm_pre"][l][None, :], gq=W["mla_q_norm"][l][None, :], gkv=W["mla_kv_norm"][l][None, :],
        ggrp=W["group_norm"][l][None, :], gpost=W["norm_post"][l][None, :])


def _local_step(x, pos, W, tgt):
    cosk, sin = _rope_tables(pos)
    saved = []
    for l in range(DEPTH):
        lw = _layer_weights(W, l)
        h32, h16, xn = _inproj_fwd(x, lw["gpre"], lw["wp"])
        ya = _swa_fwd(h16, lw["sinks"])
        yb = _conv_fwd(h32, lw["cw"], lw["cb"])
        qc, kc, vc = _cprep_fwd(h32, lw["gq"], lw["gkv"], lw["wuq2"], lw["wkv2"], cosk, sin)
        ycp, lse = _mla_fwd(qc, kc, vc)
        yd = _sb_fwd(h16)
        x_new, ym, o = _post_fwd(ya, yb, ycp, yd, h32, lw["ggrp"], lw["wout"], lw["gpost"], x)
        saved.append(dict(lw=lw, x=x, h32=h32, h16=h16, xn=xn, ya=ya, yb=yb, qc=qc, kc=kc, vc=vc, ycp=ycp,
                          lse=lse, yd=yd, ym=ym, o=o))
        x = x_new
    sq, dx = _loss_head(x, tgt)

    grads = {k: [None] * DEPTH for k in ("norm_pre", "w_in", "attn_sinks", "conv_w", "conv_b", "mla_q_norm",
                                         "mla_w_uq", "mla_kv_norm", "mla_w_ukv", "group_norm", "w_out",
                                         "norm_post")}
    for l in reversed(range(DEPTH)):
        s = saved[l]
        lw = s["lw"]
        do, dya, dyb, dycp, dyd, dgate, dgpost, dggrp = _post_bwd(
            dx, s["o"], lw["gpost"], lw["woutt"], s["ya"], s["yb"], s["ycp"], s["yd"], s["h32"], lw["ggrp"])
        grads["norm_post"][l] = dgpost[0]
        grads["group_norm"][l] = dggrp[0]
        grads["w_out"][l] = _matmul_tn(s["ym"], do, 512, "dw_out")
        sdq, sdk, sdv = _sb_bwd(s["h16"], s["yd"], dyd)
        mdq, mdk, mdv = _mla_bwd(s["qc"], s["kc"], s["vc"], s["ycp"], s["lse"], dycp)
        (dcq, dckv, dckr, dckrs, dq2, dkv2, cqn, ckvn, dgq, dgkv) = _cprep_bwd(
            s["h32"], lw["gq"], lw["gkv"], lw["wuq2t"], lw["wkv2t"], cosk, sin, mdq, mdk, mdv)
        grads["mla_q_norm"][l] = dgq[0]
        grads["mla_kv_norm"][l] = dgkv[0]
        grads["mla_w_uq"][l] = _unpad_w_uq(_matmul_tn(cqn, dq2, 512, "dw_uq"))
        grads["mla_w_ukv"][l] = _unpad_w_ukv(_matmul_tn(ckvn, dkv2, 512, "dw_ukv"))
        dbb, dbc, dbx, dcw = _conv_bwd(s["h32"], lw["cw"], lw["cb"], dyb)
        grads["conv_w"][l] = dcw[0:3]
        grads["conv_b"][l] = dcw[3]
        adq, adk, adv, dsk = _swa_bwd(s["h16"], lw["sinks"], dya)
        grads["attn_sinks"][l] = dsk[0, 0:4]
        T = x.shape[0]
        dh = jnp.concatenate([dgate, adq, adk.astype(BF), adv.astype(BF), dbb, dbc, dbx, dcq, dckv, dckr, dckrs,
                              sdq.astype(BF), sdk.astype(BF), sdv.astype(BF), jnp.zeros((T, NP - C_END), BF)],
                             axis=1)
        grads["w_in"][l] = _unpad_w_in(_matmul_tn(s["xn"], dh, 1280, "dw_in"))
        dx, dgpre = _inproj_bwd_x(dh, lw["wpt"], s["x"], lw["gpre"], dx)
        grads["norm_pre"][l] = dgpre[0]
    return sq, dx, {k: jnp.stack(v) for k, v in grads.items()}


SHARDED = ("w_in", "conv_w", "mla_w_uq", "mla_w_ukv", "w_out")
SHARD_AXIS = {"w_in": 2, "conv_w": 2, "mla_w_uq": 2, "mla_w_ukv": 2, "w_out": 1}
REPLICATED = ("norm_pre", "attn_sinks", "conv_b", "mla_q_norm", "mla_kv_norm", "group_norm", "norm_post")
ORDER = ("norm_pre", "w_in", "attn_sinks", "conv_w", "conv_b", "mla_q_norm", "mla_w_uq", "mla_kv_norm",
         "mla_w_ukv", "group_norm", "w_out", "norm_post")


def _pack(arrs, dtype):
    flat = jnp.concatenate([a.reshape(-1).astype(dtype) for a in arrs])
    rows = -(-flat.shape[0] // 2048) * 16
    flat = jnp.concatenate([flat, jnp.zeros((rows * 128 - flat.shape[0],), dtype)])
    return flat.reshape(rows, 128)


def _unpack(buf, shapes):
    flat = buf.reshape(-1)
    out, off = [], 0
    for s in shapes:
        n = int(np.prod(s))
        out.append(flat[off:off + n].reshape(s))
        off += n
    return out


def _me():
    return lax.axis_index("x"), lax.axis_index("y"), lax.axis_index("c")


def _peer(k):
    x, y, c = _me()
    return (x ^ (k >> 2 & 1), y ^ (k >> 1 & 1), c ^ (k & 1))


def _all_gather(block):
    R = block.shape[0]

    def body(src_ref, out_ref, send_sems, recv_sems, local_sem):
        x, y, c = _me()
        me = 4 * x + 2 * y + c
        mine = pltpu.make_async_copy(src_ref, out_ref.at[me], local_sem)
        mine.start()
        copies = [pltpu.make_async_remote_copy(src_ref=src_ref, dst_ref=out_ref.at[me], send_sem=send_sems.at[k - 1],
                                               recv_sem=recv_sems.at[k - 1], device_id=_peer(k), device_id_type=MESH)
                  for k in range(1, N_DEV)]
        for cp in copies:
            cp.start()
        for cp in copies:
            cp.wait()
        mine.wait()

    return pl.pallas_call(
        body, name="all_gather", out_shape=jax.ShapeDtypeStruct((N_DEV, R, 128), block.dtype),
        in_specs=[pl.BlockSpec(memory_space=pl.ANY)], out_specs=pl.BlockSpec(memory_space=pl.ANY),
        scratch_shapes=[pltpu.SemaphoreType.DMA((N_DEV - 1,)), pltpu.SemaphoreType.DMA((N_DEV - 1,)),
                        pltpu.SemaphoreType.DMA])(block)


def _all_to_all(blocks):
    R = blocks.shape[1]

    def body(src_ref, out_ref, send_sems, recv_sems, local_sem):
        x, y, c = _me()
        me = 4 * x + 2 * y + c
        mine = pltpu.make_async_copy(src_ref.at[me], out_ref.at[me], local_sem)
        mine.start()
        copies = []
        for k in range(1, N_DEV):
            px, py, pc = _peer(k)
            copies.append(pltpu.make_async_remote_copy(
                src_ref=src_ref.at[4 * px + 2 * py + pc], dst_ref=out_ref.at[me], send_sem=send_sems.at[k - 1],
                recv_sem=recv_sems.at[k - 1], device_id=(px, py, pc), device_id_type=MESH))
        for cp in copies:
            cp.start()
        for cp in copies:
            cp.wait()
        mine.wait()

    return pl.pallas_call(
        body, name="all_to_all", out_shape=jax.ShapeDtypeStruct((N_DEV, R, 128), blocks.dtype),
        in_specs=[pl.BlockSpec(memory_space=pl.ANY)], out_specs=pl.BlockSpec(memory_space=pl.ANY),
        scratch_shapes=[pltpu.SemaphoreType.DMA((N_DEV - 1,)), pltpu.SemaphoreType.DMA((N_DEV - 1,)),
                        pltpu.SemaphoreType.DMA])(blocks)


def _adamw(parts, w, m, v):
    R = w.shape[0]
    tr = 16
    for cand in (1024, 512, 256, 128, 64, 32, 16):
        if R % cand == 0:
            tr = cand
            break

    def body(p_ref, w_ref, m_ref, v_ref, g_ref, d_ref, nm_ref, nv_ref):
        g = p_ref[0]
        for k in range(1, N_DEV):
            g = g + p_ref[k]
        g_ref[...] = g
        m_ = ADAM_B1 * m_ref[...] + (1.0 - ADAM_B1) * g
        v_ = ADAM_B2 * v_ref[...] + (1.0 - ADAM_B2) * (g * g)
        nm_ref[...] = m_
        nv_ref[...] = v_
        m_hat = m_ / (1.0 - ADAM_B1 ** ADAM_STEP)
        v_hat = v_ / (1.0 - ADAM_B2 ** ADAM_STEP)
        d_ref[...] = -ADAM_LR * (m_hat / (jnp.sqrt(v_hat) + ADAM_EPS) + ADAM_WD * w_ref[...])

    row = pl.BlockSpec((tr, 128), lambda n: (n, 0))
    return pl.pallas_call(
        body, name="adamw", grid=(R // tr,),
        in_specs=[pl.BlockSpec((N_DEV, tr, 128), lambda n: (0, n, 0)), row, row, row],
        out_specs=[row] * 4, out_shape=[jax.ShapeDtypeStruct((R, 128), F32)] * 4,
        compiler_params=_params(1))(parts, w, m, v)


def kernel(x, positions, norm_pre, w_in, attn_sinks, conv_w, conv_b, mla_q_norm, mla_w_uq, mla_kv_norm, mla_w_ukv, group_norm, w_out, norm_post, loss_target, m_norm_pre, m_w_in, m_attn_sinks, m_conv_w, m_conv_b, m_mla_q_norm, m_mla_w_uq, m_mla_kv_norm, m_mla_w_ukv, m_group_norm, m_w_out, m_norm_post, v_norm_pre, v_w_in, v_attn_sinks, v_conv_w, v_conv_b, v_mla_q_norm, v_mla_w_uq, v_mla_kv_norm, v_mla_w_ukv, v_group_norm, v_w_out, v_norm_post):
    local = dict(norm_pre=norm_pre, w_in=w_in, attn_sinks=attn_sinks, conv_w=conv_w, conv_b=conv_b,
                 mla_q_norm=mla_q_norm, mla_w_uq=mla_w_uq, mla_kv_norm=mla_kv_norm, mla_w_ukv=mla_w_ukv,
                 group_norm=group_norm, w_out=w_out, norm_post=norm_post)
    mom = dict(norm_pre=m_norm_pre, w_in=m_w_in, attn_sinks=m_attn_sinks, conv_w=m_conv_w, conv_b=m_conv_b,
               mla_q_norm=m_mla_q_norm, mla_w_uq=m_mla_w_uq, mla_kv_norm=m_mla_kv_norm, mla_w_ukv=m_mla_w_ukv,
               group_norm=m_group_norm, w_out=m_w_out, norm_post=m_norm_post)
    vel = dict(norm_pre=v_norm_pre, w_in=v_w_in, attn_sinks=v_attn_sinks, conv_w=v_conv_w, conv_b=v_conv_b,
               mla_q_norm=v_mla_q_norm, mla_w_uq=v_mla_w_uq, mla_kv_norm=v_mla_kv_norm, mla_w_ukv=v_mla_w_ukv,
               group_norm=v_group_norm, w_out=v_w_out, norm_post=v_norm_post)

    shard_shapes = [local[n].shape for n in SHARDED]
    gathered = _all_gather(_pack([local[n] for n in SHARDED], BF))
    per_dev = [_unpack(gathered[d], shard_shapes) for d in range(N_DEV)]
    W = {n: local[n] for n in REPLICATED}
    for i, n in enumerate(SHARDED):
        W[n] = jnp.concatenate([per_dev[d][i] for d in range(N_DEV)], axis=SHARD_AXIS[n])

    sq, grad_x, g = _local_step(x[0], positions[0], W, loss_target[0])
    loss = lax.psum(0.5 / D_MODEL * jnp.sum(sq), ("x", "y", "c"))

    names = SHARDED + REPLICATED
    blocks = []
    for d in range(N_DEV):
        parts = []
        for n in SHARDED:
            width = local[n].shape[SHARD_AXIS[n]]
            parts.append(lax.slice_in_dim(g[n], d * width, (d + 1) * width, axis=SHARD_AXIS[n]))
        parts += [g[n] for n in REPLICATED]
        blocks.append(_pack(parts, F32))
    received = _all_to_all(jnp.stack(blocks))

    gsum, delta, new_m, new_v = _adamw(received, _pack([local[n] for n in names], F32),
                                       _pack([mom[n] for n in names], F32), _pack([vel[n] for n in names], F32))
    shapes = [local[n].shape for n in names]
    out = {}
    for tag, buf in (("g", gsum), ("d", delta), ("m", new_m), ("v", new_v)):
        for n, a in zip(names, _unpack(buf, shapes)):
            out[tag, n] = a
    return (loss, grad_x[None], *[out["g", n] for n in ORDER], *[out["d", n] for n in ORDER],
            *[out["m", n] for n in ORDER], *[out["v", n] for n in ORDER])
```

```python
import functools

import jax
import jax.numpy as jnp
import numpy as np
from jax import lax
from jax.experimental import pallas as pl
from jax.experimental.pallas import tpu as pltpu

F32 = jnp.float32
BF = jnp.bfloat16
MESH = pl.DeviceIdType.MESH

D_MODEL = 1024
DEPTH = 2
EPS = 1e-6
N_DEV = 8
VMEM_LIMIT = 56 * 1024 * 1024
NEG = -1e30
MLA_SCALE = 96.0 ** -0.5
SB_SCALE = 0.125

NP = 3840
C_GATE = 0
C_AQ = 1024
C_AK = 1280
C_AV = 1408
C_BB = 1536
C_BC = 1792
C_BX = 2048
C_CQ = 2304
C_CKV = 2560
C_CKR = 2688
C_CKRS = 2816
C_DQ = 2944
C_DK = 3200
C_DV = 3456
C_END = 3712

def _swap32(a):
    return jnp.concatenate([a[:, 16:32], a[:, 0:16]], axis=1)

ADAM_LR, ADAM_B1, ADAM_B2, ADAM_EPS, ADAM_WD, ADAM_STEP = 0.001, 0.9, 0.999, 1e-08, 0.01, 10


def _dot(a, b):
    return jnp.dot(a, b, preferred_element_type=F32)


def _dot_nt(a, b):
    return lax.dot_general(a, b, (((1,), (1,)), ((), ())), preferred_element_type=F32)


def _dot_tn(a, b):
    return lax.dot_general(a, b, (((0,), (0,)), ((), ())), preferred_element_type=F32)


def _params(n_grid):
    return pltpu.CompilerParams(dimension_semantics=("arbitrary",) * n_grid, vmem_limit_bytes=VMEM_LIMIT)


def _rms_fwd(x, g):
    r = lax.rsqrt(jnp.mean(x * x, axis=-1, keepdims=True) + EPS)
    return (x * r) * g, r


def _rms_bwd(x, g, r, dy, width=None):
    n = x.shape[-1] if width is None else width
    u = dy * g
    dx = r * u - x * (r * r * r) * (jnp.sum(x * u, axis=-1, keepdims=True) / n)
    return dx, dy * (x * r)


def _iota(shape, axis):
    return lax.broadcasted_iota(jnp.int32, shape, axis)


def _inproj_fwd(x, g, wpt):
    T = x.shape[0]
    tm = 256

    def body(x_ref, g_ref, w_ref, h32_ref, h16_ref, xn_ref):
        xn, _ = _rms_fwd(x_ref[...], g_ref[...])
        xn = xn.astype(BF)
        xn_ref[...] = xn
        h = _dot_nt(xn, w_ref[...])
        h32_ref[...] = h
        h16_ref[...] = h.astype(BF)

    return pl.pallas_call(
        body, name="inproj_fwd", grid=(T // tm,),
        in_specs=[pl.BlockSpec((tm, D_MODEL), lambda n: (n, 0)),
                  pl.BlockSpec((1, D_MODEL), lambda n: (0, 0)),
                  pl.BlockSpec((NP, D_MODEL), lambda n: (0, 0))],
        out_specs=[pl.BlockSpec((tm, NP), lambda n: (n, 0)),
                   pl.BlockSpec((tm, NP), lambda n: (n, 0)),
                   pl.BlockSpec((tm, D_MODEL), lambda n: (n, 0))],
        out_shape=[jax.ShapeDtypeStruct((T, NP), F32), jax.ShapeDtypeStruct((T, NP), BF),
                   jax.ShapeDtypeStruct((T, D_MODEL), BF)],
        compiler_params=_params(1))(x, g, wpt)


def _inproj_bwd_x(dh, wpt, x, g, dxo):
    T = x.shape[0]
    tm = 256

    def body(dh_ref, w_ref, x_ref, g_ref, dxo_ref, dx_ref, dg_ref):
        n = pl.program_id(0)
        dxn = _dot(dh_ref[...], w_ref[...])
        xv = x_ref[...]
        _, r = _rms_fwd(xv, g_ref[...])
        dx, dgt = _rms_bwd(xv, g_ref[...], r, dxn)
        dx_ref[...] = dxo_ref[...] + dx

        @pl.when(n == 0)
        def _():
            dg_ref[...] = jnp.zeros_like(dg_ref)

        dg_ref[...] += jnp.sum(dgt, axis=0, keepdims=True)

    return pl.pallas_call(
        body, name="inproj_bwd_x", grid=(T // tm,),
        in_specs=[pl.BlockSpec((tm, NP), lambda n: (n, 0)),
                  pl.BlockSpec((NP, D_MODEL), lambda n: (0, 0)),
                  pl.BlockSpec((tm, D_MODEL), lambda n: (n, 0)),
                  pl.BlockSpec((1, D_MODEL), lambda n: (0, 0)),
                  pl.BlockSpec((tm, D_MODEL), lambda n: (n, 0))],
        out_specs=[pl.BlockSpec((tm, D_MODEL), lambda n: (n, 0)),
                   pl.BlockSpec((1, D_MODEL), lambda n: (0, 0))],
        out_shape=[jax.ShapeDtypeStruct((T, D_MODEL), F32), jax.ShapeDtypeStruct((1, D_MODEL), F32)],
        compiler_params=_params(1))(dh, wpt, x, g, dxo)


def _matmul_tn(a, b, tn, name):
    T, M = a.shape
    N = b.shape[1]
    tk = 512 if T % 512 == 0 else T

    def body(a_ref, b_ref, o_ref):
        k = pl.program_id(1)

        @pl.when(k == 0)
        def _():
            o_ref[...] = jnp.zeros_like(o_ref)

        o_ref[...] += _dot_tn(a_ref[...], b_ref[...])

    return pl.pallas_call(
        body, name=name, grid=(N // tn, T // tk),
        in_specs=[pl.BlockSpec((tk, M), lambda j, k: (k, 0)),
                  pl.BlockSpec((tk, tn), lambda j, k: (k, j))],
        out_specs=pl.BlockSpec((M, tn), lambda j, k: (0, j)),
        out_shape=jax.ShapeDtypeStruct((M, N), F32),
        compiler_params=_params(2))(a, b)


def _swa_probs(q, kc, kp, sink, mask_c, mask_p):
    sc = jnp.where(mask_c, _dot_nt(q, kc) * SB_SCALE, NEG)
    sp = jnp.where(mask_p, _dot_nt(q, kp) * SB_SCALE, NEG)
    m = jnp.maximum(jnp.maximum(jnp.max(sc, axis=-1, keepdims=True), jnp.max(sp, axis=-1, keepdims=True)), sink)
    pc = jnp.exp(sc - m)
    pp = jnp.exp(sp - m)
    ps = jnp.exp(sink - m)
    inv = 1.0 / (jnp.sum(pc, axis=-1, keepdims=True) + jnp.sum(pp, axis=-1, keepdims=True) + ps)
    return pc * inv, pp * inv, ps * inv


def _swa_masks(n, tq):
    row = _iota((tq, tq), 0)
    col = _iota((tq, tq), 1)
    return col <= row, jnp.logical_and(col > row, n > 0)


def _swa_specs(tq):
    return [pl.BlockSpec(memory_space=pltpu.SMEM),
            pl.BlockSpec((tq, 256), lambda n: (n, C_AQ // 256)),
            pl.BlockSpec((tq, 128), lambda n: (n, C_AK // 128)),
            pl.BlockSpec((tq, 128), lambda n: (jnp.maximum(n - 1, 0), C_AK // 128)),
            pl.BlockSpec((tq, 128), lambda n: (n, C_AV // 128)),
            pl.BlockSpec((tq, 128), lambda n: (jnp.maximum(n - 1, 0), C_AV // 128))]


def _swa_fwd(h16, sinks):
    T = h16.shape[0]
    tq = 128

    def body(s_ref, q_ref, kc_ref, kp_ref, vc_ref, vp_ref, o_ref):
        n = pl.program_id(0)
        mask_c, mask_p = _swa_masks(n, tq)
        for h in range(4):
            g = h // 2
            q = q_ref[:, h * 64:(h + 1) * 64]
            pc, pp, _ = _swa_probs(q, kc_ref[:, g * 64:(g + 1) * 64], kp_ref[:, g * 64:(g + 1) * 64],
                                   s_ref[h], mask_c, mask_p)
            o_ref[:, h * 64:(h + 1) * 64] = (_dot(pc.astype(BF), vc_ref[:, g * 64:(g + 1) * 64])
                                             + _dot(pp.astype(BF), vp_ref[:, g * 64:(g + 1) * 64]))

    return pl.pallas_call(
        body, name="swa_fwd", grid=(T // tq,), in_specs=_swa_specs(tq),
        out_specs=pl.BlockSpec((tq, 256), lambda n: (n, 0)),
        out_shape=jax.ShapeDtypeStruct((T, 256), F32),
        compiler_params=_params(1))(sinks, h16, h16, h16, h16, h16)


def _swa_bwd(h16, sinks, dya):
    T = h16.shape[0]
    tq = 128

    def body(s_ref, q_ref, kc_ref, kp_ref, vc_ref, vp_ref, do_ref, dq_ref, dk_ref, dv_ref, ds_ref):
        n = pl.program_id(0)

        @pl.when(n == 0)
        def _():
            dk_ref[...] = jnp.zeros_like(dk_ref)
            dv_ref[...] = jnp.zeros_like(dv_ref)
            ds_ref[...] = jnp.zeros_like(ds_ref)

        mask_c, mask_p = _swa_masks(n, tq)
        rc = pl.ds(pl.multiple_of(n * tq, tq), tq)
        rp = pl.ds(pl.multiple_of(jnp.maximum(n - 1, 0) * tq, tq), tq)
        lane = _iota((8, 128), 1)
        row8 = _iota((8, 128), 0)
        for h in range(4):
            g = h // 2
            hs = slice(h * 64, (h + 1) * 64)
            gs = slice(g * 64, (g + 1) * 64)
            q = q_ref[:, hs]
            kc, kp, vc, vp = kc_ref[:, gs], kp_ref[:, gs], vc_ref[:, gs], vp_ref[:, gs]
            pc, pp, ps = _swa_probs(q, kc, kp, s_ref[h], mask_c, mask_p)
            pcb, ppb = pc.astype(BF), pp.astype(BF)
            do = do_ref[:, hs]
            dob = do.astype(BF)
            o = _dot(pcb, vc) + _dot(ppb, vp)
            dd = jnp.sum(do * o, axis=-1, keepdims=True)
            dsc = (pc * (_dot_nt(dob, vc) - dd) * SB_SCALE).astype(BF)
            dsp = (pp * (_dot_nt(dob, vp) - dd) * SB_SCALE).astype(BF)
            dq_ref[:, hs] = (_dot(dsc, kc) + _dot(dsp, kp)).astype(BF)
            dk_ref[rc, gs] += _dot_tn(dsc, q)
            dk_ref[rp, gs] += _dot_tn(dsp, q)
            dv_ref[rc, gs] += _dot_tn(pcb, dob)
            dv_ref[rp, gs] += _dot_tn(ppb, dob)
            dsink = -jnp.sum(ps * dd)
            ds_ref[...] += jnp.where(jnp.logical_and(lane == h, row8 == 0), dsink, 0.0)

    return pl.pallas_call(
        body, name="swa_bwd", grid=(T // tq,),
        in_specs=_swa_specs(tq) + [pl.BlockSpec((tq, 256), lambda n: (n, 0))],
        out_specs=[pl.BlockSpec((tq, 256), lambda n: (n, 0)),
                   pl.BlockSpec((T, 128), lambda n: (0, 0)),
                   pl.BlockSpec((T, 128), lambda n: (0, 0)),
                   pl.BlockSpec((8, 128), lambda n: (0, 0))],
        out_shape=[jax.ShapeDtypeStruct((T, 256), BF), jax.ShapeDtypeStruct((T, 128), F32),
                   jax.ShapeDtypeStruct((T, 128), F32), jax.ShapeDtypeStruct((8, 128), F32)],
        compiler_params=_params(1))(sinks, h16, h16, h16, h16, h16, dya)


def _conv_u(bc_ref, bx_ref, bch_ref, bxh_ref, n, tm):
    u = bc_ref[...] * bx_ref[...]
    uh = bch_ref[...] * bxh_ref[...] * (n > 0).astype(F32)
    rowi = _iota((tm, 256), 0)
    u1 = jnp.where(rowi == 0, uh[7:8, :], pltpu.roll(u, 1, axis=0))
    u2 = jnp.where(rowi == 0, uh[6:7, :], jnp.where(rowi == 1, uh[7:8, :], pltpu.roll(u, 2, axis=0)))
    return u, u1, u2


def _conv_fwd(h32, cw, cb):
    T = h32.shape[0]
    tm = 512 if T % 512 == 0 else T
    hb = tm // 8

    def body(bb_ref, bc_ref, bx_ref, bch_ref, bxh_ref, w_ref, b_ref, o_ref):
        n = pl.program_id(0)
        u, u1, u2 = _conv_u(bc_ref, bx_ref, bch_ref, bxh_ref, n, tm)
        y = w_ref[0:1, :] * u2 + w_ref[1:2, :] * u1 + w_ref[2:3, :] * u + b_ref[...]
        o_ref[...] = bb_ref[...] * y

    halo = lambda c: pl.BlockSpec((8, 256), lambda n: (jnp.maximum(n * hb - 1, 0), c // 256))
    return pl.pallas_call(
        body, name="conv_fwd", grid=(T // tm,),
        in_specs=[pl.BlockSpec((tm, 256), lambda n: (n, C_BB // 256)),
                  pl.BlockSpec((tm, 256), lambda n: (n, C_BC // 256)),
                  pl.BlockSpec((tm, 256), lambda n: (n, C_BX // 256)),
                  halo(C_BC), halo(C_BX),
                  pl.BlockSpec((8, 256), lambda n: (0, 0)),
                  pl.BlockSpec((1, 256), lambda n: (0, 0))],
        out_specs=pl.BlockSpec((tm, 256), lambda n: (n, 0)),
        out_shape=jax.ShapeDtypeStruct((T, 256), F32),
        compiler_params=_params(1))(h32, h32, h32, h32, h32, cw, cb)


def _conv_bwd(h32, cw, cb, dyb):
    T = h32.shape[0]
    tm = 512 if T % 512 == 0 else T
    hb = tm // 8
    nt = T // tm

    def body(bb_ref, bc_ref, bx_ref, bch_ref, bxh_ref, bbn_ref, dy_ref, dyn_ref, w_ref, b_ref,
             dbb_ref, dbc_ref, dbx_ref, dw_ref):
        n = pl.program_id(0)

        @pl.when(n == 0)
        def _():
            dw_ref[...] = jnp.zeros_like(dw_ref)

        u, u1, u2 = _conv_u(bc_ref, bx_ref, bch_ref, bxh_ref, n, tm)
        w0, w1, w2 = w_ref[0:1, :], w_ref[1:2, :], w_ref[2:3, :]
        y = w0 * u2 + w1 * u1 + w2 * u + b_ref[...]
        dyb_ = dy_ref[...]
        dbb_ref[...] = (dyb_ * y).astype(BF)
        dy = dyb_ * bb_ref[...]
        dyn = dyn_ref[...] * bbn_ref[...] * (n < nt - 1).astype(F32)
        rowi = _iota((tm, 256), 0)
        dy1 = jnp.where(rowi == tm - 1, dyn[0:1, :], pltpu.roll(dy, tm - 1, axis=0))
        dy2 = jnp.where(rowi == tm - 2, dyn[0:1, :],
                        jnp.where(rowi == tm - 1, dyn[1:2, :], pltpu.roll(dy, tm - 2, axis=0)))
        du = w2 * dy + w1 * dy1 + w0 * dy2
        dbc_ref[...] = (du * bx_ref[...]).astype(BF)
        dbx_ref[...] = (du * bc_ref[...]).astype(BF)
        dw_ref[0:1, :] += jnp.sum(dy * u2, axis=0, keepdims=True)
        dw_ref[1:2, :] += jnp.sum(dy * u1, axis=0, keepdims=True)
        dw_ref[2:3, :] += jnp.sum(dy * u, axis=0, keepdims=True)
        dw_ref[3:4, :] += jnp.sum(dy, axis=0, keepdims=True)

    halo = lambda c: pl.BlockSpec((8, 256), lambda n: (jnp.maximum(n * hb - 1, 0), c // 256))
    nxt = lambda c: pl.BlockSpec((8, 256), lambda n: (jnp.minimum((n + 1) * hb, T // 8 - 1), c // 256))
    cur = lambda c: pl.BlockSpec((tm, 256), lambda n: (n, c // 256))
    return pl.pallas_call(
        body, name="conv_bwd", grid=(nt,),
        in_specs=[cur(C_BB), cur(C_BC), cur(C_BX), halo(C_BC), halo(C_BX), nxt(C_BB),
                  cur(0), nxt(0),
                  pl.BlockSpec((8, 256), lambda n: (0, 0)),
                  pl.BlockSpec((1, 256), lambda n: (0, 0))],
        out_specs=[cur(0), cur(0), cur(0), pl.BlockSpec((8, 256), lambda n: (0, 0))],
        out_shape=[jax.ShapeDtypeStruct((T, 256), BF)] * 3 + [jax.ShapeDtypeStruct((8, 256), F32)],
        compiler_params=_params(1))(h32, h32, h32, h32, h32, h32, dyb, dyb, cw, cb)


def _cprep_specs(tm):
    return [pl.BlockSpec((tm, 256), lambda n: (n, C_CQ // 256)),
            pl.BlockSpec((tm, 128), lambda n: (n, C_CKV // 128)),
            pl.BlockSpec((tm, 128), lambda n: (n, C_CKR // 128)),
            pl.BlockSpec((tm, 128), lambda n: (n, C_CKRS // 128)),
            pl.BlockSpec((1, 256), lambda n: (0, 0)),
            pl.BlockSpec((1, 128), lambda n: (0, 0)),
            pl.BlockSpec((tm, 128), lambda n: (n, 0)),
            pl.BlockSpec((tm, 128), lambda n: (n, 0))]


def _cprep_fwd(h32, gq, gkv, wuq2, wkv2, cosk, sin):
    T = h32.shape[0]
    tm = 512 if T % 512 == 0 else T

    def body(cq_ref, ckv_ref, ckr_ref, ckrs_ref, gq_ref, gkv_ref, cos_ref, sin_ref, wuq_ref, wkv_ref,
             q_ref, k_ref, v_ref):
        cosk_, sin_ = cos_ref[...], sin_ref[...]
        cosq = cosk_ + (_iota((tm, 128), 1) < 64).astype(F32)
        cqn, _ = _rms_fwd(cq_ref[...], gq_ref[...])
        q2 = _dot(cqn.astype(BF), wuq_ref[...])
        ckvn, _ = _rms_fwd(ckv_ref[...], gkv_ref[...])
        kv2 = _dot(ckvn.astype(BF), wkv_ref[...])
        kr = ckr_ref[...] * cosk_ + ckrs_ref[...] * sin_
        for h in range(4):
            hs = slice(h * 128, (h + 1) * 128)
            q_ref[:, hs] = ((q2[:, hs] * cosq + q2[:, 512 + h * 128:512 + (h + 1) * 128] * sin_) * MLA_SCALE).astype(BF)
            k_ref[:, hs] = (kv2[:, hs] + kr).astype(BF)
        v_ref[...] = kv2[:, 512:].astype(BF)

    return pl.pallas_call(
        body, name="cprep_fwd", grid=(T // tm,),
        in_specs=_cprep_specs(tm) + [pl.BlockSpec((256, 1024), lambda n: (0, 0)),
                                     pl.BlockSpec((128, 1024), lambda n: (0, 0))],
        out_specs=[pl.BlockSpec((tm, 512), lambda n: (n, 0))] * 3,
        out_shape=[jax.ShapeDtypeStruct((T, 512), BF)] * 3,
        compiler_params=_params(1))(h32, h32, h32, h32, gq, gkv, cosk, sin, wuq2, wkv2)


def _cprep_bwd(h32, gq, gkv, wuq2t, wkv2t, cosk, sin, dq, dk, dv):
    T = h32.shape[0]
    tm = 512 if T % 512 == 0 else T

    def body(cq_ref, ckv_ref, ckr_ref, ckrs_ref, gq_ref, gkv_ref, cos_ref, sin_ref, wuq_ref, wkv_ref,
             dq_ref, dk_ref, dv_ref,
             dcq_ref, dckv_ref, dckr_ref, dckrs_ref, dq2_ref, dkv2_ref, cqn_ref, ckvn_ref, dgq_ref, dgkv_ref):
        n = pl.program_id(0)

        @pl.when(n == 0)
        def _():
            dgq_ref[...] = jnp.zeros_like(dgq_ref)
            dgkv_ref[...] = jnp.zeros_like(dgkv_ref)

        cosk_, sin_ = cos_ref[...], sin_ref[...]
        cosq = cosk_ + (_iota((tm, 128), 1) < 64).astype(F32)
        dkr = jnp.zeros((tm, 128), F32)
        for h in range(4):
            hs = slice(h * 128, (h + 1) * 128)
            dqh = dq_ref[:, hs] * MLA_SCALE
            dq2_ref[:, hs] = (dqh * cosq).astype(BF)
            dq2_ref[:, 512 + h * 128:512 + (h + 1) * 128] = (dqh * sin_).astype(BF)
            dkr = dkr + dk_ref[:, hs]
        dkv2_ref[:, :512] = dk_ref[...].astype(BF)
        dkv2_ref[:, 512:] = dv_ref[...].astype(BF)
        dckr_ref[...] = (dkr * cosk_).astype(BF)
        dckrs_ref[...] = (dkr * sin_).astype(BF)

        cq, gq_ = cq_ref[...], gq_ref[...]
        cqn, rq = _rms_fwd(cq, gq_)
        cqn_ref[...] = cqn.astype(BF)
        dcq, dgt = _rms_bwd(cq, gq_, rq, _dot(dq2_ref[...], wuq_ref[...]))
        dcq_ref[...] = dcq.astype(BF)
        dgq_ref[...] += jnp.sum(dgt, axis=0, keepdims=True)

        ckv, gkv_ = ckv_ref[...], gkv_ref[...]
        ckvn, rkv = _rms_fwd(ckv, gkv_)
        ckvn_ref[...] = ckvn.astype(BF)
        dckv, dgt2 = _rms_bwd(ckv, gkv_, rkv, _dot(dkv2_ref[...], wkv_ref[...]))
        dckv_ref[...] = dckv.astype(BF)
        dgkv_ref[...] += jnp.sum(dgt2, axis=0, keepdims=True)

    row = lambda w: pl.BlockSpec((tm, w), lambda n: (n, 0))
    return pl.pallas_call(
        body, name="cprep_bwd", grid=(T // tm,),
        in_specs=_cprep_specs(tm) + [pl.BlockSpec((1024, 256), lambda n: (0, 0)),
                                     pl.BlockSpec((1024, 128), lambda n: (0, 0)),
                                     row(512), row(512), row(512)],
        out_specs=[row(256), row(128), row(128), row(128), row(1024), row(1024), row(256), row(128),
                   pl.BlockSpec((1, 256), lambda n: (0, 0)), pl.BlockSpec((1, 128), lambda n: (0, 0))],
        out_shape=[jax.ShapeDtypeStruct((T, 256), BF), jax.ShapeDtypeStruct((T, 128), BF),
                   jax.ShapeDtypeStruct((T, 128), BF), jax.ShapeDtypeStruct((T, 128), BF),
                   jax.ShapeDtypeStruct((T, 1024), BF), jax.ShapeDtypeStruct((T, 1024), BF),
                   jax.ShapeDtypeStruct((T, 256), BF), jax.ShapeDtypeStruct((T, 128), BF),
                   jax.ShapeDtypeStruct((1, 256), F32), jax.ShapeDtypeStruct((1, 128), F32)],
        compiler_params=_params(1))(h32, h32, h32, h32, gq, gkv, cosk, sin, wuq2t, wkv2t, dq, dk, dv)


MLA_TILE = 512


def _causal_mask(t):
    return _iota((t, t), 1) <= _iota((t, t), 0)


def _mla_fwd(q, k, v):
    T = q.shape[0]
    tq = MLA_TILE

    def body(q_ref, k_ref, v_ref, o_ref, lse_ref):
        i = pl.program_id(1)
        qv = q_ref[...]
        mask = _causal_mask(tq)

        def step(j, carry, masked):
            m, l, acc = carry
            rows = pl.ds(pl.multiple_of(j * tq, tq), tq)
            s = _dot_nt(qv, k_ref[rows, :])
            if masked:
                s = jnp.where(mask, s, NEG)
            m_new = jnp.maximum(m, jnp.max(s, axis=-1, keepdims=True))
            p = jnp.exp(s - m_new)
            alpha = jnp.exp(m - m_new)
            l = alpha * l + jnp.sum(p, axis=-1, keepdims=True)
            acc = alpha * acc + _dot(p.astype(BF), v_ref[rows, :])
            return m_new, l, acc

        init = (jnp.full((tq, 1), NEG, F32), jnp.zeros((tq, 1), F32), jnp.zeros((tq, 128), F32))
        carry = lax.fori_loop(0, i, lambda j, c: step(j, c, False), init)
        m, l, acc = step(i, carry, True)
        o_ref[...] = acc * (1.0 / l)
        lse_ref[...] = jnp.broadcast_to(m + jnp.log(l), (tq, 128))

    blk = pl.BlockSpec((tq, 128), lambda h, i: (i, h))
    full = pl.BlockSpec((T, 128), lambda h, i: (0, h))
    return pl.pallas_call(
        body, name="mla_fwd", grid=(4, T // tq), in_specs=[blk, full, full], out_specs=[blk, blk],
        out_shape=[jax.ShapeDtypeStruct((T, 512), F32), jax.ShapeDtypeStruct((T, 512), F32)],
        compiler_params=_params(2))(q, k, v)


def _mla_bwd(q, k, v, o, lse, do):
    T = q.shape[0]
    tq = MLA_TILE

    def body(q_ref, k_ref, v_ref, o_ref, lse_ref, do_ref, dq_ref, dk_ref, dv_ref):
        i = pl.program_id(1)

        @pl.when(i == 0)
        def _():
            dk_ref[...] = jnp.zeros_like(dk_ref)
            dv_ref[...] = jnp.zeros_like(dv_ref)

        qv = q_ref[...]
        do = do_ref[...]
        dob = do.astype(BF)
        dd = jnp.sum(do * o_ref[...], axis=-1, keepdims=True)
        lse_ = lse_ref[:, 0:1]
        mask = _causal_mask(tq)

        def step(j, dq, masked):
            rows = pl.ds(pl.multiple_of(j * tq, tq), tq)
            kj, vj = k_ref[rows, :], v_ref[rows, :]
            s = _dot_nt(qv, kj)
            if masked:
                s = jnp.where(mask, s, NEG)
            p = jnp.exp(s - lse_)
            ds = (p * (_dot_nt(dob, vj) - dd)).astype(BF)
            dk_ref[rows, :] += _dot_tn(ds, qv)
            dv_ref[rows, :] += _dot_tn(p.astype(BF), dob)
            return dq + _dot(ds, kj)

        dq = lax.fori_loop(0, i, lambda j, c: step(j, c, False), jnp.zeros((tq, 128), F32))
        dq_ref[...] = step(i, dq, True)

    blk = pl.BlockSpec((tq, 128), lambda h, i: (i, h))
    full = pl.BlockSpec((T, 128), lambda h, i: (0, h))
    return pl.pallas_call(
        body, name="mla_bwd", grid=(4, T // tq), in_specs=[blk, full, full, blk, blk, blk],
        out_specs=[blk, full, full],
        out_shape=[jax.ShapeDtypeStruct((T, 512), F32)] * 3,
        compiler_params=_params(2))(q, k, v, o, lse, do)


def _sb_tile(qm, kj, rr, strict, masked, upper):
    z = _dot_nt(qm, kj) * SB_SCALE
    sp = jnp.maximum(z, 0.0) + jnp.log(1.0 + jnp.exp(-jnp.abs(z)))
    lk = -sp
    if masked:
        lk = jnp.where(strict, lk, 0.0)
    after = rr + _dot(lk.astype(BF), upper)
    ll = z - sp
    a = jnp.exp(ll + after)
    if masked:
        a = jnp.where(strict, a, 0.0)
    return z, ll, a, jnp.sum(lk, axis=-1, keepdims=True)


SB_TQ, SB_TK = 512, 256


def _sb_consts(tq, tk):
    row, col = _iota((tq, tk), 0), _iota((tq, tk), 1)
    strict = [col + d * tk < row for d in range(tq // tk)]
    r2, c2 = _iota((tk, tk), 0), _iota((tk, tk), 1)
    return strict, (r2 > c2).astype(BF), (r2 < c2).astype(BF)


def _sb_fwd(h16):
    T = h16.shape[0]
    tq, tk = SB_TQ, SB_TK
    nd = tq // tk

    def body(q_ref, k_ref, v_ref, o_ref):
        i = pl.program_id(1)
        strict, upper, _ = _sb_consts(tq, tk)
        lane = _iota((tq, 128), 1)
        q2 = q_ref[...]
        qms = [jnp.where(lane < 64, q2, jnp.zeros_like(q2)), jnp.where(lane >= 64, q2, jnp.zeros_like(q2))]

        def step(j, carry, d):
            rows = pl.ds(pl.multiple_of(j * tk, tk), tk)
            kj, vj = k_ref[rows, :], v_ref[rows, :]
            out = []
            for hh in range(2):
                rr, acc = carry[hh]
                _, _, a, rs = _sb_tile(qms[hh], kj, rr, None if d is None else strict[d], d is not None, upper)
                out.append((rr + rs, acc + _dot(a.astype(BF), vj)))
            return tuple(out)

        carry = ((jnp.zeros((tq, 1), F32), jnp.zeros((tq, 128), F32)),) * 2
        for d in reversed(range(nd)):
            carry = step(nd * i + d, carry, d)
        carry = lax.fori_loop(0, nd * i, lambda t, c: step(nd * i - 1 - t, c, None), carry)
        o_ref[...] = jnp.where(lane < 64, carry[0][1], carry[1][1])

    return pl.pallas_call(
        body, name="sb_fwd", grid=(2, T // tq),
        in_specs=[pl.BlockSpec((tq, 128), lambda p, i: (i, C_DQ // 128 + p)),
                  pl.BlockSpec((T, 128), lambda p, i: (0, C_DK // 128 + p)),
                  pl.BlockSpec((T, 128), lambda p, i: (0, C_DV // 128 + p))],
        out_specs=pl.BlockSpec((tq, 128), lambda p, i: (i, p)),
        out_shape=jax.ShapeDtypeStruct((T, 256), F32),
        compiler_params=_params(2))(h16, h16, h16)


def _sb_bwd(h16, yd, dyd):
    T = h16.shape[0]
    tq, tk = SB_TQ, SB_TK
    nd = tq // tk

    def body(q_ref, k_ref, v_ref, o_ref, do_ref, dq_ref, dk_ref, dv_ref):
        i = pl.program_id(1)

        @pl.when(i == 0)
        def _():
            dk_ref[...] = jnp.zeros_like(dk_ref)
            dv_ref[...] = jnp.zeros_like(dv_ref)

        strict, upper, before = _sb_consts(tq, tk)
        lane = _iota((tq, 128), 1)
        lane_k = _iota((tk, 128), 1)
        q2 = q_ref[...]
        dob2 = do_ref[...].astype(BF)
        doo = dob2.astype(F32) * o_ref[...]
        mines = [lane < 64, lane >= 64]
        qms = [jnp.where(m, q2, jnp.zeros_like(q2)) for m in mines]
        doms = [jnp.where(m, dob2, jnp.zeros_like(dob2)) for m in mines]
        dds = [jnp.sum(jnp.where(m, doo, 0.0), axis=-1, keepdims=True) for m in mines]

        def step(j, carry, d):
            rows = pl.ds(pl.multiple_of(j * tk, tk), tk)
            kj, vj = k_ref[rows, :], v_ref[rows, :]
            out, dks, dvs = [], [], []
            for hh in range(2):
                rr, sg, dq = carry[hh]
                z, ll, a, rs = _sb_tile(qms[hh], kj, rr, None if d is None else strict[d], d is not None, upper)
                ab = a.astype(BF)
                g = _dot_nt(doms[hh], vj) * ab.astype(F32)
                gs = jnp.sum(g, axis=-1, keepdims=True)
                pre = (dds[hh] - sg - gs) + _dot(g.astype(BF), before)
                sig = jnp.exp(ll)
                dz = g * (1.0 - sig) - sig * pre
                if d is not None:
                    dz = jnp.where(strict[d], dz, 0.0)
                dzb = (dz * SB_SCALE).astype(BF)
                dks.append(_dot_tn(dzb, q2))
                dvs.append(_dot_tn(ab, dob2))
                out.append((rr + rs, sg + gs, dq + _dot(dzb, kj)))
            dk_ref[rows, :] += jnp.where(lane_k < 64, dks[0], dks[1])
            dv_ref[rows, :] += jnp.where(lane_k < 64, dvs[0], dvs[1])
            return tuple(out)

        zero = jnp.zeros((tq, 1), F32)
        carry = ((zero, zero, jnp.zeros((tq, 128), F32)),) * 2
        for d in reversed(range(nd)):
            carry = step(nd * i + d, carry, d)
        carry = lax.fori_loop(0, nd * i, lambda t, c: step(nd * i - 1 - t, c, None), carry)
        dq_ref[...] = jnp.where(lane < 64, carry[0][2], carry[1][2])

    blk = lambda c: pl.BlockSpec((tq, 128), lambda p, i: (i, c // 128 + p))
    full = lambda c: pl.BlockSpec((T, 128), lambda p, i: (0, c // 128 + p))
    return pl.pallas_call(
        body, name="sb_bwd", grid=(2, T // tq),
        in_specs=[blk(C_DQ), full(C_DK), full(C_DV), blk(0), blk(0)],
        out_specs=[blk(0), full(0), full(0)],
        out_shape=[jax.ShapeDtypeStruct((T, 256), F32)] * 3,
        compiler_params=_params(2))(h16, h16, h16, yd, dyd)


def _compact_c(ycp):
    return jnp.concatenate([ycp[:, h * 128:h * 128 + 64] for h in range(4)], axis=1)


def _post_fwd(ya, yb, ycp, yd, h32, ggrp, wout, gpost, x):
    T = x.shape[0]
    tm = 256

    def body(ya_ref, yb_ref, yc_ref, yd_ref, gate_ref, gg_ref, w_ref, gp_ref, x_ref, xn_ref, ym_ref, o_ref):
        ys = [ya_ref[...], yb_ref[...], _compact_c(yc_ref[...]), yd_ref[...]]
        gate = gate_ref[...]
        sil = gate * (1.0 / (1.0 + jnp.exp(-gate)))
        parts = []
        for gi in range(4):
            ng, _ = _rms_fwd(ys[gi], gg_ref[:, gi * 256:(gi + 1) * 256])
            parts.append(ng * sil[:, gi * 256:(gi + 1) * 256])
        ym = jnp.concatenate(parts, axis=1).astype(BF)
        ym_ref[...] = ym
        o = _dot(ym, w_ref[...])
        o_ref[...] = o
        on, _ = _rms_fwd(o, gp_ref[...])
        xn_ref[...] = x_ref[...] + on

    row = lambda w: pl.BlockSpec((tm, w), lambda n: (n, 0))
    vec = pl.BlockSpec((1, 1024), lambda n: (0, 0))
    return pl.pallas_call(
        body, name="post_fwd", grid=(T // tm,),
        in_specs=[row(256), row(256), row(512), row(256), pl.BlockSpec((tm, 1024), lambda n: (n, C_GATE // 1024)),
                  vec, pl.BlockSpec((1024, 1024), lambda n: (0, 0)), vec, row(1024)],
        out_specs=[row(1024), row(1024), row(1024)],
        out_shape=[jax.ShapeDtypeStruct((T, 1024), F32), jax.ShapeDtypeStruct((T, 1024), BF),
                   jax.ShapeDtypeStruct((T, 1024), F32)],
        compiler_params=_params(1))(ya, yb, ycp, yd, h32, ggrp, wout, gpost, x)


def _post_bwd(dx, o, gpost, woutt, ya, yb, ycp, yd, h32, ggrp):
    T = dx.shape[0]
    tm = 256

    def body(dx_ref, o_ref, gp_ref, w_ref, ya_ref, yb_ref, yc_ref, yd_ref, gate_ref, gg_ref,
             do_ref, dya_ref, dyb_ref, dyc_ref, dyd_ref, dgate_ref, dgp_ref, dgg_ref):
        n = pl.program_id(0)

        @pl.when(n == 0)
        def _():
            dgp_ref[...] = jnp.zeros_like(dgp_ref)
            dgg_ref[...] = jnp.zeros_like(dgg_ref)

        ov, gp = o_ref[...], gp_ref[...]
        _, ro = _rms_fwd(ov, gp)
        do, dgt = _rms_bwd(ov, gp, ro, dx_ref[...])
        dgp_ref[...] += jnp.sum(dgt, axis=0, keepdims=True)
        dob = do.astype(BF)
        do_ref[...] = dob
        dym = _dot(dob, w_ref[...])
        gate = gate_ref[...]
        sg = 1.0 / (1.0 + jnp.exp(-gate))
        sil = gate * sg
        dsil = sg * (1.0 + gate * (1.0 - sg))
        ys = [ya_ref[...], yb_ref[...], _compact_c(yc_ref[...]), yd_ref[...]]
        dys = []
        for gi in range(4):
            gs = slice(gi * 256, (gi + 1) * 256)
            gg = gg_ref[:, gs]
            ng, rg = _rms_fwd(ys[gi], gg)
            dgate_ref[:, gs] = (dym[:, gs] * ng * dsil[:, gs]).astype(BF)
            dy, dgt2 = _rms_bwd(ys[gi], gg, rg, dym[:, gs] * sil[:, gs])
            dgg_ref[:, gs] += jnp.sum(dgt2, axis=0, keepdims=True)
            dys.append(dy)
        dya_ref[...] = dys[0]
        dyb_ref[...] = dys[1]
        dyd_ref[...] = dys[3]
        z64 = jnp.zeros((tm, 64), F32)
        dyc_ref[...] = jnp.concatenate(
            [piece for h in range(4) for piece in (dys[2][:, h * 64:(h + 1) * 64], z64)], axis=1)

    row = lambda w: pl.BlockSpec((tm, w), lambda n: (n, 0))
    vec = pl.BlockSpec((1, 1024), lambda n: (0, 0))
    return pl.pallas_call(
        body, name="post_bwd", grid=(T // tm,),
        in_specs=[row(1024), row(1024), vec, pl.BlockSpec((1024, 1024), lambda n: (0, 0)),
                  row(256), row(256), row(512), row(256),
                  pl.BlockSpec((tm, 1024), lambda n: (n, C_GATE // 1024)), vec],
        out_specs=[row(1024), row(256), row(256), row(512), row(256), row(1024), vec, vec],
        out_shape=[jax.ShapeDtypeStruct((T, 1024), BF), jax.ShapeDtypeStruct((T, 256), F32),
                   jax.ShapeDtypeStruct((T, 256), F32), jax.ShapeDtypeStruct((T, 512), F32),
                   jax.ShapeDtypeStruct((T, 256), F32), jax.ShapeDtypeStruct((T, 1024), BF),
                   jax.ShapeDtypeStruct((1, 1024), F32), jax.ShapeDtypeStruct((1, 1024), F32)],
        compiler_params=_params(1))(dx, o, gpost, woutt, ya, yb, ycp, yd, h32, ggrp)


def _loss_head(y, tgt):
    T = y.shape[0]
    tm = 512 if T % 512 == 0 else T

    def body(y_ref, t_ref, s_ref, dy_ref):
        n = pl.program_id(0)

        @pl.when(n == 0)
        def _():
            s_ref[...] = jnp.zeros_like(s_ref)

        d = y_ref[...] - t_ref[...]
        s_ref[...] += jnp.sum(d * d, axis=0, keepdims=True)
        dy_ref[...] = d * (1.0 / D_MODEL)

    row = pl.BlockSpec((tm, 1024), lambda n: (n, 0))
    return pl.pallas_call(
        body, name="loss_head", grid=(T // tm,), in_specs=[row, row],
        out_specs=[pl.BlockSpec((1, 1024), lambda n: (0, 0)), row],
        out_shape=[jax.ShapeDtypeStruct((1, 1024), F32), jax.ShapeDtypeStruct((T, 1024), F32)],
        compiler_params=_params(1))(y, tgt)


def _swap_rows32(a):
    return jnp.concatenate([a[16:32], a[0:16]], axis=0)


def _pad_w_in_t(wt):
    z = lambda n: jnp.zeros((n, wt.shape[1]), wt.dtype)
    kr = wt[1664:1696]
    return jnp.concatenate([wt[2464:3488], wt[0:1664], z(64), kr, z(32), z(64), _swap_rows32(kr), z(32),
                            wt[1696:2464], z(NP - C_END)], axis=0)


def _unpad_w_in_t(d):
    kr = d[C_CKR + 64:C_CKR + 96] + _swap_rows32(d[C_CKRS + 64:C_CKRS + 96])
    return jnp.concatenate([d[C_AQ:C_CKR], kr, d[C_DQ:C_END], d[0:1024]], axis=0)


def _pad_w_uq(w):
    z = lambda n: jnp.zeros((w.shape[0], n), w.dtype)
    a = [p for h in range(4) for p in (w[:, 96 * h:96 * h + 96], z(32))]
    b = [p for h in range(4) for p in (z(64), _swap32(w[:, 96 * h + 64:96 * h + 96]), z(32))]
    return jnp.concatenate(a + b, axis=1)


def _unpad_w_uq(d):
    out = []
    for h in range(4):
        out.append(d[:, 128 * h:128 * h + 64])
        out.append(d[:, 128 * h + 64:128 * h + 96] + _swap32(d[:, 512 + 128 * h + 64:512 + 128 * h + 96]))
    return jnp.concatenate(out, axis=1)


def _pad_w_ukv(w):
    z = jnp.zeros((w.shape[0], 64), w.dtype)
    a = [p for h in range(4) for p in (w[:, 128 * h:128 * h + 64], z)]
    b = [p for h in range(4) for p in (w[:, 128 * h + 64:128 * h + 128], z)]
    return jnp.concatenate(a + b, axis=1)


def _unpad_w_ukv(d):
    return jnp.concatenate([p for h in range(4) for p in (d[:, 128 * h:128 * h + 64],
                                                          d[:, 512 + 128 * h:512 + 128 * h + 64])], axis=1)


def _rope_tables(pos):
    freqs = 10000.0 ** (-jnp.arange(16, dtype=F32) / 16)
    ang = pos.astype(F32)[:, None] * freqs
    c, s = jnp.cos(ang), jnp.sin(ang)
    z = lambda n: jnp.zeros((pos.shape[0], n), F32)
    return (jnp.concatenate([z(64), c, c, z(32)], axis=1), jnp.concatenate([z(64), -s, s, z(32)], axis=1))


def _layer_weights(W, l):
    wuq2 = _pad_w_uq(W["mla_w_uq"][l])
    wkv2 = _pad_w_ukv(W["mla_w_ukv"][l])
    wout = W["w_out"][l]
    cw = jnp.concatenate([W["conv_w"][l].astype(F32), jnp.zeros((5, 256), F32)], axis=0)
    return dict(
        wpt=_pad_w_in_t(W["w_in_t"][l]).astype(BF), wuq2=wuq2.astype(BF), wuq2t=wuq2.T.astype(BF),
        wkv2=wkv2.astype(BF), wkv2t=wkv2.T.astype(BF), wout=wout.astype(BF), woutt=wout.T.astype(BF),
        cw=cw, cb=W["conv_b"][l][None, :], sinks=W["attn_sinks"][l],
        gpre=W["norm_pre"][l][None, :], gq=W["mla_q_norm"][l][None, :], gkv=W["mla_kv_norm"][l][None, :],
        ggrp=W["group_norm"][l][None, :], gpost=W["norm_post"][l][None, :])


def _local_step(x, pos, W, tgt):
    cosk, sin = _rope_tables(pos)
    saved = []
    for l in range(DEPTH):
        lw = _layer_weights(W, l)
        h32, h16, xn = _inproj_fwd(x, lw["gpre"], lw["wpt"])
        ya = _swa_fwd(h16, lw["sinks"])
        yb = _conv_fwd(h32, lw["cw"], lw["cb"])
        qc, kc, vc = _cprep_fwd(h32, lw["gq"], lw["gkv"], lw["wuq2"], lw["wkv2"], cosk, sin)
        ycp, lse = _mla_fwd(qc, kc, vc)
        yd = _sb_fwd(h16)
        x_new, ym, o = _post_fwd(ya, yb, ycp, yd, h32, lw["ggrp"], lw["wout"], lw["gpost"], x)
        saved.append(dict(lw=lw, x=x, h32=h32, h16=h16, xn=xn, ya=ya, yb=yb, qc=qc, kc=kc, vc=vc, ycp=ycp,
                          lse=lse, yd=yd, ym=ym, o=o))
        x = x_new
    sq, dx = _loss_head(x, tgt)

    grads = {k: [None] * DEPTH for k in ("norm_pre", "w_in_t", "attn_sinks", "conv_w", "conv_b", "mla_q_norm",
                                         "mla_w_uq", "mla_kv_norm", "mla_w_ukv", "group_norm", "w_out",
                                         "norm_post")}
    for l in reversed(range(DEPTH)):
        s = saved[l]
        lw = s["lw"]
        do, dya, dyb, dycp, dyd, dgate, dgpost, dggrp = _post_bwd(
            dx, s["o"], lw["gpost"], lw["woutt"], s["ya"], s["yb"], s["ycp"], s["yd"], s["h32"], lw["ggrp"])
        grads["norm_post"][l] = dgpost[0]
        grads["group_norm"][l] = dggrp[0]
        grads["w_out"][l] = _matmul_tn(s["ym"], do, 512, "dw_out")
        sdq, sdk, sdv = _sb_bwd(s["h16"], s["yd"], dyd)
        mdq, mdk, mdv = _mla_bwd(s["qc"], s["kc"], s["vc"], s["ycp"], s["lse"], dycp)
        (dcq, dckv, dckr, dckrs, dq2, dkv2, cqn, ckvn, dgq, dgkv) = _cprep_bwd(
            s["h32"], lw["gq"], lw["gkv"], lw["wuq2t"], lw["wkv2t"], cosk, sin, mdq, mdk, mdv)
        grads["mla_q_norm"][l] = dgq[0]
        grads["mla_kv_norm"][l] = dgkv[0]
        grads["mla_w_uq"][l] = _unpad_w_uq(_matmul_tn(cqn, dq2, 512, "dw_uq"))
        grads["mla_w_ukv"][l] = _unpad_w_ukv(_matmul_tn(ckvn, dkv2, 512, "dw_ukv"))
        dbb, dbc, dbx, dcw = _conv_bwd(s["h32"], lw["cw"], lw["cb"], dyb)
        grads["conv_w"][l] = dcw[0:3]
        grads["conv_b"][l] = dcw[3]
        adq, adk, adv, dsk = _swa_bwd(s["h16"], lw["sinks"], dya)
        grads["attn_sinks"][l] = dsk[0, 0:4]
        T = x.shape[0]
        dh = jnp.concatenate([dgate, adq, adk.astype(BF), adv.astype(BF), dbb, dbc, dbx, dcq, dckv, dckr, dckrs,
                              sdq.astype(BF), sdk.astype(BF), sdv.astype(BF), jnp.zeros((T, NP - C_END), BF)],
                             axis=1)
        grads["w_in_t"][l] = _unpad_w_in_t(_matmul_tn(s["xn"], dh, 1280, "dw_in").T)
        dx, dgpre = _inproj_bwd_x(dh, lw["wpt"], s["x"], lw["gpre"], dx)
        grads["norm_pre"][l] = dgpre[0]
    return sq, dx, {k: jnp.stack(v) for k, v in grads.items()}


SMALL_SHARDED = ("conv_w", "mla_w_uq", "mla_w_ukv")
REPLICATED = ("norm_pre", "attn_sinks", "conv_b", "mla_q_norm", "mla_kv_norm", "group_norm", "norm_post")
ORDER = ("norm_pre", "w_in", "attn_sinks", "conv_w", "conv_b", "mla_q_norm", "mla_w_uq", "mla_kv_norm",
         "mla_w_ukv", "group_norm", "w_out", "norm_post")
W_IN_COLS = 436
W_IN_WIN = 440
SMALL_ROWS = 48


def _pack_small(arrs, dtype):
    flat = jnp.concatenate([a.reshape(-1).astype(dtype) for a in arrs])
    flat = jnp.concatenate([flat, jnp.zeros((SMALL_ROWS * D_MODEL - flat.shape[0],), dtype)])
    return flat.reshape(SMALL_ROWS, D_MODEL)


def _unpack_small(buf, shapes):
    flat = buf.reshape(-1)
    out, off = [], 0
    for s in shapes:
        n = int(np.prod(s))
        out.append(flat[off:off + n].reshape(s))
        off += n
    return out


def _pack_state(p, c):
    wt = jnp.transpose(p["w_in"], (0, 2, 1))
    win = lax.dynamic_update_slice(jnp.zeros((DEPTH, W_IN_WIN, D_MODEL), F32), wt, (0, 4 * c, 0))
    small = _pack_small([p[n] for n in SMALL_SHARDED + REPLICATED], F32)
    return jnp.concatenate([win.reshape(DEPTH * W_IN_WIN, D_MODEL), p["w_out"].reshape(DEPTH * 128, D_MODEL), small],
                           axis=0)


def _unpack_state(buf, p, c):
    nw = DEPTH * W_IN_WIN
    win = lax.dynamic_slice(buf[0:nw].reshape(DEPTH, W_IN_WIN, D_MODEL), (0, 4 * c, 0), (DEPTH, W_IN_COLS, D_MODEL))
    out = {"w_in": jnp.transpose(win, (0, 2, 1)), "w_out": buf[nw:nw + DEPTH * 128].reshape(DEPTH, 128, D_MODEL)}
    names = SMALL_SHARDED + REPLICATED
    for n, a in zip(names, _unpack_small(buf[nw + DEPTH * 128:], [p[n].shape for n in names])):
        out[n] = a
    return out


def _me():
    return lax.axis_index("x"), lax.axis_index("y"), lax.axis_index("c")


def _peer(k):
    x, y, c = _me()
    return (x ^ (k >> 2 & 1), y ^ (k >> 1 & 1), c ^ (k & 1))


def _all_gather(block):
    R, C = block.shape

    def body(src_ref, out_ref, send_sems, recv_sems, local_sem):
        x, y, c = _me()
        me = 4 * x + 2 * y + c
        mine = pltpu.make_async_copy(src_ref, out_ref.at[me], local_sem)
        mine.start()
        copies = [pltpu.make_async_remote_copy(src_ref=src_ref, dst_ref=out_ref.at[me], send_sem=send_sems.at[k - 1],
                                               recv_sem=recv_sems.at[k - 1], device_id=_peer(k), device_id_type=MESH)
                  for k in range(1, N_DEV)]
        for cp in copies:
            cp.start()
        for cp in copies:
            cp.wait()
        mine.wait()

    return pl.pallas_call(
        body, name="all_gather", out_shape=jax.ShapeDtypeStruct((N_DEV, R, C), block.dtype),
        in_specs=[pl.BlockSpec(memory_space=pl.ANY)], out_specs=pl.BlockSpec(memory_space=pl.ANY),
        scratch_shapes=[pltpu.SemaphoreType.DMA((N_DEV - 1,)), pltpu.SemaphoreType.DMA((N_DEV - 1,)),
                        pltpu.SemaphoreType.DMA])(block)


def _all_to_all(blocks):
    _, R, C = blocks.shape

    def body(src_ref, out_ref, send_sems, recv_sems, local_sem):
        x, y, c = _me()
        me = 4 * x + 2 * y + c
        mine = pltpu.make_async_copy(src_ref.at[me], out_ref.at[me], local_sem)
        mine.start()
        copies = []
        for k in range(1, N_DEV):
            px, py, pc = _peer(k)
            copies.append(pltpu.make_async_remote_copy(
                src_ref=src_ref.at[4 * px + 2 * py + pc], dst_ref=out_ref.at[me], send_sem=send_sems.at[k - 1],
                recv_sem=recv_sems.at[k - 1], device_id=(px, py, pc), device_id_type=MESH))
        for cp in copies:
            cp.start()
        for cp in copies:
            cp.wait()
        mine.wait()

    return pl.pallas_call(
        body, name="all_to_all", out_shape=jax.ShapeDtypeStruct((N_DEV, R, C), blocks.dtype),
        in_specs=[pl.BlockSpec(memory_space=pl.ANY)], out_specs=pl.BlockSpec(memory_space=pl.ANY),
        scratch_shapes=[pltpu.SemaphoreType.DMA((N_DEV - 1,)), pltpu.SemaphoreType.DMA((N_DEV - 1,)),
                        pltpu.SemaphoreType.DMA])(blocks)


def _adamw(parts, w, m, v):
    R, C = w.shape
    tr = 32
    assert R % tr == 0

    def body(p_ref, w_ref, m_ref, v_ref, g_ref, d_ref, nm_ref, nv_ref):
        g = p_ref[0]
        for k in range(1, N_DEV):
            g = g + p_ref[k]
        g_ref[...] = g
        m_ = ADAM_B1 * m_ref[...] + (1.0 - ADAM_B1) * g
        v_ = ADAM_B2 * v_ref[...] + (1.0 - ADAM_B2) * (g * g)
        nm_ref[...] = m_
        nv_ref[...] = v_
        m_hat = m_ / (1.0 - ADAM_B1 ** ADAM_STEP)
        v_hat = v_ / (1.0 - ADAM_B2 ** ADAM_STEP)
        d_ref[...] = -ADAM_LR * (m_hat / (jnp.sqrt(v_hat) + ADAM_EPS) + ADAM_WD * w_ref[...])

    row = pl.BlockSpec((tr, C), lambda n: (n, 0))
    return pl.pallas_call(
        body, name="adamw", grid=(R // tr,),
        in_specs=[pl.BlockSpec((N_DEV, tr, C), lambda n: (0, n, 0)), row, row, row],
        out_specs=[row] * 4, out_shape=[jax.ShapeDtypeStruct((R, C), F32)] * 4,
        compiler_params=_params(1))(parts, w, m, v)


def kernel(x, positions, norm_pre, w_in, attn_sinks, conv_w, conv_b, mla_q_norm, mla_w_uq, mla_kv_norm, mla_w_ukv, group_norm, w_out, norm_post, loss_target, m_norm_pre, m_w_in, m_attn_sinks, m_conv_w, m_conv_b, m_mla_q_norm, m_mla_w_uq, m_mla_kv_norm, m_mla_w_ukv, m_group_norm, m_w_out, m_norm_post, v_norm_pre, v_w_in, v_attn_sinks, v_conv_w, v_conv_b, v_mla_q_norm, v_mla_w_uq, v_mla_kv_norm, v_mla_w_ukv, v_group_norm, v_w_out, v_norm_post):
    local = dict(norm_pre=norm_pre, w_in=w_in, attn_sinks=attn_sinks, conv_w=conv_w, conv_b=conv_b,
                 mla_q_norm=mla_q_norm, mla_w_uq=mla_w_uq, mla_kv_norm=mla_kv_norm, mla_w_ukv=mla_w_ukv,
                 group_norm=group_norm, w_out=w_out, norm_post=norm_post)
    mom = dict(norm_pre=m_norm_pre, w_in=m_w_in, attn_sinks=m_attn_sinks, conv_w=m_conv_w, conv_b=m_conv_b,
               mla_q_norm=m_mla_q_norm, mla_w_uq=m_mla_w_uq, mla_kv_norm=m_mla_kv_norm, mla_w_ukv=m_mla_w_ukv,
               group_norm=m_group_norm, w_out=m_w_out, norm_post=m_norm_post)
    vel = dict(norm_pre=v_norm_pre, w_in=v_w_in, attn_sinks=v_attn_sinks, conv_w=v_conv_w, conv_b=v_conv_b,
               mla_q_norm=v_mla_q_norm, mla_w_uq=v_mla_w_uq, mla_kv_norm=v_mla_kv_norm, mla_w_ukv=v_mla_w_ukv,
               group_norm=v_group_norm, w_out=v_w_out, norm_post=v_norm_post)

    c = lax.axis_index("c")

    wt = jnp.transpose(w_in, (0, 2, 1)).astype(BF)
    wt = jnp.concatenate([wt, jnp.zeros((DEPTH, 448 - W_IN_COLS, D_MODEL), BF)], axis=1)
    payload = jnp.concatenate([wt.reshape(DEPTH * 448, D_MODEL), w_out.astype(BF).reshape(DEPTH * 128, D_MODEL),
                               _pack_small([local[n] for n in SMALL_SHARDED], BF)], axis=0)
    gathered = _all_gather(payload)
    W = {n: local[n] for n in REPLICATED}
    W["w_in_t"] = jnp.stack([jnp.concatenate([gathered[d, 448 * l:448 * l + W_IN_COLS] for d in range(N_DEV)], axis=0)
                             for l in range(DEPTH)])
    wo0 = DEPTH * 448
    W["w_out"] = jnp.stack([gathered[:, wo0 + 128 * l:wo0 + 128 * (l + 1)].reshape(D_MODEL, D_MODEL)
                            for l in range(DEPTH)])
    small = [_unpack_small(gathered[d, wo0 + DEPTH * 128:], [local[n].shape for n in SMALL_SHARDED])
             for d in range(N_DEV)]
    for i, n in enumerate(SMALL_SHARDED):
        W[n] = jnp.concatenate([small[d][i] for d in range(N_DEV)], axis=2)

    sq, grad_x, g = _local_step(x[0], positions[0], W, loss_target[0])
    loss = lax.psum(0.5 / D_MODEL * jnp.sum(sq), ("x", "y", "c"))

    blocks = []
    for d in range(N_DEV):
        lo = W_IN_COLS * d // 8 * 8
        parts = [g["w_in_t"][:, lo:lo + W_IN_WIN].reshape(DEPTH * W_IN_WIN, D_MODEL),
                 g["w_out"][:, 128 * d:128 * (d + 1)].reshape(DEPTH * 128, D_MODEL)]
        sm = [lax.slice_in_dim(g[n], d * local[n].shape[2], (d + 1) * local[n].shape[2], axis=2)
              for n in SMALL_SHARDED]
        parts.append(_pack_small(sm + [g[n] for n in REPLICATED], F32))
        blocks.append(jnp.concatenate(parts, axis=0))
    received = _all_to_all(jnp.stack(blocks))

    bufs = _adamw(received, _pack_state(local, c), _pack_state(mom, c), _pack_state(vel, c))
    gs, ds, ms, vs = [_unpack_state(b, local, c) for b in bufs]
    return (loss, grad_x[None], *[gs[n] for n in ORDER], *[ds[n] for n in ORDER],
            *[ms[n] for n in ORDER], *[vs[n] for n in ORDER])
```

```python
import functools

import jax
import jax.numpy as jnp
import numpy as np
from jax import lax
from jax.experimental import pallas as pl
from jax.experimental.pallas import tpu as pltpu

F32 = jnp.float32
BF = jnp.bfloat16
MESH = pl.DeviceIdType.MESH

D_MODEL = 1024
DEPTH = 2
EPS = 1e-6
N_DEV = 8
VMEM_LIMIT = 56 * 1024 * 1024
NEG = -1e30
MLA_SCALE = 96.0 ** -0.5
SB_SCALE = 0.125
LOG2E = 1.4426950408889634

NP = 3840
C_GATE = 0
C_AQ = 1024
C_AK = 1280
C_AV = 1408
C_BB = 1536
C_BC = 1792
C_BX = 2048
C_CQ = 2304
C_CKV = 2560
C_CKR = 2688
C_CKRS = 2816
C_DQ = 2944
C_DK = 3200
C_DV = 3456
C_END = 3712

def _swap32(a):
    return jnp.concatenate([a[:, 16:32], a[:, 0:16]], axis=1)

ADAM_LR, ADAM_B1, ADAM_B2, ADAM_EPS, ADAM_WD, ADAM_STEP = 0.001, 0.9, 0.999, 1e-08, 0.01, 10


def _dot(a, b):
    return jnp.dot(a, b, preferred_element_type=F32)


def _dot_nt(a, b):
    return lax.dot_general(a, b, (((1,), (1,)), ((), ())), preferred_element_type=F32)


def _dot_tn(a, b):
    return lax.dot_general(a, b, (((0,), (0,)), ((), ())), preferred_element_type=F32)


def _params(n_grid):
    return pltpu.CompilerParams(dimension_semantics=("arbitrary",) * n_grid, vmem_limit_bytes=VMEM_LIMIT)


def _rms_fwd(x, g):
    r = lax.rsqrt(jnp.mean(x * x, axis=-1, keepdims=True) + EPS)
    return (x * r) * g, r


def _rms_bwd(x, g, r, dy, width=None):
    n = x.shape[-1] if width is None else width
    u = dy * g
    dx = r * u - x * (r * r * r) * (jnp.sum(x * u, axis=-1, keepdims=True) / n)
    return dx, dy * (x * r)


def _iota(shape, axis):
    return lax.broadcasted_iota(jnp.int32, shape, axis)


def _inproj_fwd(x, g, wpt):
    T = x.shape[0]
    tm = 256

    def body(x_ref, g_ref, w_ref, h32_ref, h16_ref, xn_ref):
        xn, _ = _rms_fwd(x_ref[...], g_ref[...])
        xn = xn.astype(BF)
        xn_ref[...] = xn
        h = _dot_nt(xn, w_ref[...])
        h32_ref[...] = h
        h16_ref[...] = h.astype(BF)

    return pl.pallas_call(
        body, name="inproj_fwd", grid=(T // tm,),
        in_specs=[pl.BlockSpec((tm, D_MODEL), lambda n: (n, 0)),
                  pl.BlockSpec((1, D_MODEL), lambda n: (0, 0)),
                  pl.BlockSpec((NP, D_MODEL), lambda n: (0, 0))],
        out_specs=[pl.BlockSpec((tm, NP), lambda n: (n, 0)),
                   pl.BlockSpec((tm, NP), lambda n: (n, 0)),
                   pl.BlockSpec((tm, D_MODEL), lambda n: (n, 0))],
        out_shape=[jax.ShapeDtypeStruct((T, NP), F32), jax.ShapeDtypeStruct((T, NP), BF),
                   jax.ShapeDtypeStruct((T, D_MODEL), BF)],
        compiler_params=_params(1))(x, g, wpt)


def _inproj_bwd_x(dh, wpt, x, g, dxo):
    T = x.shape[0]
    tm = 256

    def body(dh_ref, w_ref, x_ref, g_ref, dxo_ref, dx_ref, dg_ref):
        n = pl.program_id(0)
        dxn = _dot(dh_ref[...], w_ref[...])
        xv = x_ref[...]
        _, r = _rms_fwd(xv, g_ref[...])
        dx, dgt = _rms_bwd(xv, g_ref[...], r, dxn)
        dx_ref[...] = dxo_ref[...] + dx

        @pl.when(n == 0)
        def _():
            dg_ref[...] = jnp.zeros_like(dg_ref)

        dg_ref[...] += jnp.sum(dgt, axis=0, keepdims=True)

    return pl.pallas_call(
        body, name="inproj_bwd_x", grid=(T // tm,),
        in_specs=[pl.BlockSpec((tm, NP), lambda n: (n, 0)),
                  pl.BlockSpec((NP, D_MODEL), lambda n: (0, 0)),
                  pl.BlockSpec((tm, D_MODEL), lambda n: (n, 0)),
                  pl.BlockSpec((1, D_MODEL), lambda n: (0, 0)),
                  pl.BlockSpec((tm, D_MODEL), lambda n: (n, 0))],
        out_specs=[pl.BlockSpec((tm, D_MODEL), lambda n: (n, 0)),
                   pl.BlockSpec((1, D_MODEL), lambda n: (0, 0))],
        out_shape=[jax.ShapeDtypeStruct((T, D_MODEL), F32), jax.ShapeDtypeStruct((1, D_MODEL), F32)],
        compiler_params=_params(1))(dh, wpt, x, g, dxo)


def _matmul_tn(a, b, tn, name):
    T, M = a.shape
    N = b.shape[1]
    tk = 512 if T % 512 == 0 else T

    def body(a_ref, b_ref, o_ref):
        k = pl.program_id(1)

        @pl.when(k == 0)
        def _():
            o_ref[...] = jnp.zeros_like(o_ref)

        o_ref[...] += _dot_tn(a_ref[...], b_ref[...])

    return pl.pallas_call(
        body, name=name, grid=(N // tn, T // tk),
        in_specs=[pl.BlockSpec((tk, M), lambda j, k: (k, 0)),
                  pl.BlockSpec((tk, tn), lambda j, k: (k, j))],
        out_specs=pl.BlockSpec((M, tn), lambda j, k: (0, j)),
        out_shape=jax.ShapeDtypeStruct((M, N), F32),
        compiler_params=_params(2))(a, b)


def _swa_probs(q, kc, kp, sink, mask_c, mask_p):
    sc = jnp.where(mask_c, _dot_nt(q, kc) * SB_SCALE, NEG)
    sp = jnp.where(mask_p, _dot_nt(q, kp) * SB_SCALE, NEG)
    m = jnp.maximum(jnp.maximum(jnp.max(sc, axis=-1, keepdims=True), jnp.max(sp, axis=-1, keepdims=True)), sink)
    pc = jnp.exp(sc - m)
    pp = jnp.exp(sp - m)
    ps = jnp.exp(sink - m)
    inv = 1.0 / (jnp.sum(pc, axis=-1, keepdims=True) + jnp.sum(pp, axis=-1, keepdims=True) + ps)
    return pc * inv, pp * inv, ps * inv


def _swa_masks(n, tq):
    row = _iota((tq, tq), 0)
    col = _iota((tq, tq), 1)
    return col <= row, jnp.logical_and(col > row, n > 0)


def _swa_specs(tq):
    return [pl.BlockSpec(memory_space=pltpu.SMEM),
            pl.BlockSpec((tq, 256), lambda n: (n, C_AQ // 256)),
            pl.BlockSpec((tq, 128), lambda n: (n, C_AK // 128)),
            pl.BlockSpec((tq, 128), lambda n: (jnp.maximum(n - 1, 0), C_AK // 128)),
            pl.BlockSpec((tq, 128), lambda n: (n, C_AV // 128)),
            pl.BlockSpec((tq, 128), lambda n: (jnp.maximum(n - 1, 0), C_AV // 128))]


def _swa_fwd(h16, sinks):
    T = h16.shape[0]
    tq = 128

    def body(s_ref, q_ref, kc_ref, kp_ref, vc_ref, vp_ref, o_ref):
        n = pl.program_id(0)
        mask_c, mask_p = _swa_masks(n, tq)
        for h in range(4):
            g = h // 2
            q = q_ref[:, h * 64:(h + 1) * 64]
            pc, pp, _ = _swa_probs(q, kc_ref[:, g * 64:(g + 1) * 64], kp_ref[:, g * 64:(g + 1) * 64],
                                   s_ref[h], mask_c, mask_p)
            o_ref[:, h * 64:(h + 1) * 64] = (_dot(pc.astype(BF), vc_ref[:, g * 64:(g + 1) * 64])
                                             + _dot(pp.astype(BF), vp_ref[:, g * 64:(g + 1) * 64]))

    return pl.pallas_call(
        body, name="swa_fwd", grid=(T // tq,), in_specs=_swa_specs(tq),
        out_specs=pl.BlockSpec((tq, 256), lambda n: (n, 0)),
        out_shape=jax.ShapeDtypeStruct((T, 256), F32),
        compiler_params=_params(1))(sinks, h16, h16, h16, h16, h16)


def _swa_bwd(h16, sinks, dya):
    T = h16.shape[0]
    tq = 128

    def body(s_ref, q_ref, kc_ref, kp_ref, vc_ref, vp_ref, do_ref, dq_ref, dk_ref, dv_ref, ds_ref):
        n = pl.program_id(0)

        @pl.when(n == 0)
        def _():
            dk_ref[...] = jnp.zeros_like(dk_ref)
            dv_ref[...] = jnp.zeros_like(dv_ref)
            ds_ref[...] = jnp.zeros_like(ds_ref)

        mask_c, mask_p = _swa_masks(n, tq)
        rc = pl.ds(pl.multiple_of(n * tq, tq), tq)
        rp = pl.ds(pl.multiple_of(jnp.maximum(n - 1, 0) * tq, tq), tq)
        lane = _iota((8, 128), 1)
        row8 = _iota((8, 128), 0)
        for h in range(4):
            g = h // 2
            hs = slice(h * 64, (h + 1) * 64)
            gs = slice(g * 64, (g + 1) * 64)
            q = q_ref[:, hs]
            kc, kp, vc, vp = kc_ref[:, gs], kp_ref[:, gs], vc_ref[:, gs], vp_ref[:, gs]
            pc, pp, ps = _swa_probs(q, kc, kp, s_ref[h], mask_c, mask_p)
            pcb, ppb = pc.astype(BF), pp.astype(BF)
            do = do_ref[:, hs]
            dob = do.astype(BF)
            o = _dot(pcb, vc) + _dot(ppb, vp)
            dd = jnp.sum(do * o, axis=-1, keepdims=True)
            dsc = (pc * (_dot_nt(dob, vc) - dd) * SB_SCALE).astype(BF)
            dsp = (pp * (_dot_nt(dob, vp) - dd) * SB_SCALE).astype(BF)
            dq_ref[:, hs] = (_dot(dsc, kc) + _dot(dsp, kp)).astype(BF)
            dk_ref[rc, gs] += _dot_tn(dsc, q)
            dk_ref[rp, gs] += _dot_tn(dsp, q)
            dv_ref[rc, gs] += _dot_tn(pcb, dob)
            dv_ref[rp, gs] += _dot_tn(ppb, dob)
            dsink = -jnp.sum(ps * dd)
            ds_ref[...] += jnp.where(jnp.logical_and(lane == h, row8 == 0), dsink, 0.0)

    return pl.pallas_call(
        body, name="swa_bwd", grid=(T // tq,),
        in_specs=_swa_specs(tq) + [pl.BlockSpec((tq, 256), lambda n: (n, 0))],
        out_specs=[pl.BlockSpec((tq, 256), lambda n: (n, 0)),
                   pl.BlockSpec((T, 128), lambda n: (0, 0)),
                   pl.BlockSpec((T, 128), lambda n: (0, 0)),
                   pl.BlockSpec((8, 128), lambda n: (0, 0))],
        out_shape=[jax.ShapeDtypeStruct((T, 256), BF), jax.ShapeDtypeStruct((T, 128), F32),
                   jax.ShapeDtypeStruct((T, 128), F32), jax.ShapeDtypeStruct((8, 128), F32)],
        compiler_params=_params(1))(sinks, h16, h16, h16, h16, h16, dya)


def _conv_u(bc_ref, bx_ref, bch_ref, bxh_ref, n, tm):
    u = bc_ref[...] * bx_ref[...]
    uh = bch_ref[...] * bxh_ref[...] * (n > 0).astype(F32)
    rowi = _iota((tm, 256), 0)
    u1 = jnp.where(rowi == 0, uh[7:8, :], pltpu.roll(u, 1, axis=0))
    u2 = jnp.where(rowi == 0, uh[6:7, :], jnp.where(rowi == 1, uh[7:8, :], pltpu.roll(u, 2, axis=0)))
    return u, u1, u2


def _conv_fwd(h32, cw, cb):
    T = h32.shape[0]
    tm = 512 if T % 512 == 0 else T
    hb = tm // 8

    def body(bb_ref, bc_ref, bx_ref, bch_ref, bxh_ref, w_ref, b_ref, o_ref):
        n = pl.program_id(0)
        u, u1, u2 = _conv_u(bc_ref, bx_ref, bch_ref, bxh_ref, n, tm)
        y = w_ref[0:1, :] * u2 + w_ref[1:2, :] * u1 + w_ref[2:3, :] * u + b_ref[...]
        o_ref[...] = bb_ref[...] * y

    halo = lambda c: pl.BlockSpec((8, 256), lambda n: (jnp.maximum(n * hb - 1, 0), c // 256))
    return pl.pallas_call(
        body, name="conv_fwd", grid=(T // tm,),
        in_specs=[pl.BlockSpec((tm, 256), lambda n: (n, C_BB // 256)),
                  pl.BlockSpec((tm, 256), lambda n: (n, C_BC // 256)),
                  pl.BlockSpec((tm, 256), lambda n: (n, C_BX // 256)),
                  halo(C_BC), halo(C_BX),
                  pl.BlockSpec((8, 256), lambda n: (0, 0)),
                  pl.BlockSpec((1, 256), lambda n: (0, 0))],
        out_specs=pl.BlockSpec((tm, 256), lambda n: (n, 0)),
        out_shape=jax.ShapeDtypeStruct((T, 256), F32),
        compiler_params=_params(1))(h32, h32, h32, h32, h32, cw, cb)


def _conv_bwd(h32, cw, cb, dyb):
    T = h32.shape[0]
    tm = 512 if T % 512 == 0 else T
    hb = tm // 8
    nt = T // tm

    def body(bb_ref, bc_ref, bx_ref, bch_ref, bxh_ref, bbn_ref, dy_ref, dyn_ref, w_ref, b_ref,
             dbb_ref, dbc_ref, dbx_ref, dw_ref):
        n = pl.program_id(0)

        @pl.when(n == 0)
        def _():
            dw_ref[...] = jnp.zeros_like(dw_ref)

        u, u1, u2 = _conv_u(bc_ref, bx_ref, bch_ref, bxh_ref, n, tm)
        w0, w1, w2 = w_ref[0:1, :], w_ref[1:2, :], w_ref[2:3, :]
        y = w0 * u2 + w1 * u1 + w2 * u + b_ref[...]
        dyb_ = dy_ref[...]
        dbb_ref[...] = (dyb_ * y).astype(BF)
        dy = dyb_ * bb_ref[...]
        dyn = dyn_ref[...] * bbn_ref[...] * (n < nt - 1).astype(F32)
        rowi = _iota((tm, 256), 0)
        dy1 = jnp.where(rowi == tm - 1, dyn[0:1, :], pltpu.roll(dy, tm - 1, axis=0))
        dy2 = jnp.where(rowi == tm - 2, dyn[0:1, :],
                        jnp.where(rowi == tm - 1, dyn[1:2, :], pltpu.roll(dy, tm - 2, axis=0)))
        du = w2 * dy + w1 * dy1 + w0 * dy2
        dbc_ref[...] = (du * bx_ref[...]).astype(BF)
        dbx_ref[...] = (du * bc_ref[...]).astype(BF)
        dw_ref[0:1, :] += jnp.sum(dy * u2, axis=0, keepdims=True)
        dw_ref[1:2, :] += jnp.sum(dy * u1, axis=0, keepdims=True)
        dw_ref[2:3, :] += jnp.sum(dy * u, axis=0, keepdims=True)
        dw_ref[3:4, :] += jnp.sum(dy, axis=0, keepdims=True)

    halo = lambda c: pl.BlockSpec((8, 256), lambda n: (jnp.maximum(n * hb - 1, 0), c // 256))
    nxt = lambda c: pl.BlockSpec((8, 256), lambda n: (jnp.minimum((n + 1) * hb, T // 8 - 1), c // 256))
    cur = lambda c: pl.BlockSpec((tm, 256), lambda n: (n, c // 256))
    return pl.pallas_call(
        body, name="conv_bwd", grid=(nt,),
        in_specs=[cur(C_BB), cur(C_BC), cur(C_BX), halo(C_BC), halo(C_BX), nxt(C_BB),
                  cur(0), nxt(0),
                  pl.BlockSpec((8, 256), lambda n: (0, 0)),
                  pl.BlockSpec((1, 256), lambda n: (0, 0))],
        out_specs=[cur(0), cur(0), cur(0), pl.BlockSpec((8, 256), lambda n: (0, 0))],
        out_shape=[jax.ShapeDtypeStruct((T, 256), BF)] * 3 + [jax.ShapeDtypeStruct((8, 256), F32)],
        compiler_params=_params(1))(h32, h32, h32, h32, h32, h32, dyb, dyb, cw, cb)


def _cprep_specs(tm):
    return [pl.BlockSpec((tm, 256), lambda n: (n, C_CQ // 256)),
            pl.BlockSpec((tm, 128), lambda n: (n, C_CKV // 128)),
            pl.BlockSpec((tm, 128), lambda n: (n, C_CKR // 128)),
            pl.BlockSpec((tm, 128), lambda n: (n, C_CKRS // 128)),
            pl.BlockSpec((1, 256), lambda n: (0, 0)),
            pl.BlockSpec((1, 128), lambda n: (0, 0)),
            pl.BlockSpec((tm, 128), lambda n: (n, 0)),
            pl.BlockSpec((tm, 128), lambda n: (n, 0))]


def _cprep_fwd(h32, gq, gkv, wuq2, wkv2, cosk, sin):
    T = h32.shape[0]
    tm = 512 if T % 512 == 0 else T

    def body(cq_ref, ckv_ref, ckr_ref, ckrs_ref, gq_ref, gkv_ref, cos_ref, sin_ref, wuq_ref, wkv_ref,
             q_ref, k_ref, v_ref):
        cosk_, sin_ = cos_ref[...], sin_ref[...]
        cosq = cosk_ + (_iota((tm, 128), 1) < 64).astype(F32)
        cqn, _ = _rms_fwd(cq_ref[...], gq_ref[...])
        q2 = _dot(cqn.astype(BF), wuq_ref[...])
        ckvn, _ = _rms_fwd(ckv_ref[...], gkv_ref[...])
        kv2 = _dot(ckvn.astype(BF), wkv_ref[...])
        kr = ckr_ref[...] * cosk_ + ckrs_ref[...] * sin_
        for h in range(4):
            hs = slice(h * 128, (h + 1) * 128)
            q_ref[:, hs] = ((q2[:, hs] * cosq + q2[:, 512 + h * 128:512 + (h + 1) * 128] * sin_) * MLA_SCALE).astype(BF)
            k_ref[:, hs] = (kv2[:, hs] + kr).astype(BF)
        v_ref[...] = kv2[:, 512:].astype(BF)

    return pl.pallas_call(
        body, name="cprep_fwd", grid=(T // tm,),
        in_specs=_cprep_specs(tm) + [pl.BlockSpec((256, 1024), lambda n: (0, 0)),
                                     pl.BlockSpec((128, 1024), lambda n: (0, 0))],
        out_specs=[pl.BlockSpec((tm, 512), lambda n: (n, 0))] * 3,
        out_shape=[jax.ShapeDtypeStruct((T, 512), BF)] * 3,
        compiler_params=_params(1))(h32, h32, h32, h32, gq, gkv, cosk, sin, wuq2, wkv2)


def _cprep_bwd(h32, gq, gkv, wuq2t, wkv2t, cosk, sin, dq, dk, dv):
    T = h32.shape[0]
    tm = 512 if T % 512 == 0 else T

    def body(cq_ref, ckv_ref, ckr_ref, ckrs_ref, gq_ref, gkv_ref, cos_ref, sin_ref, wuq_ref, wkv_ref,
             dq_ref, dk_ref, dv_ref,
             dcq_ref, dckv_ref, dckr_ref, dckrs_ref, dq2_ref, dkv2_ref, cqn_ref, ckvn_ref, dgq_ref, dgkv_ref):
        n = pl.program_id(0)

        @pl.when(n == 0)
        def _():
            dgq_ref[...] = jnp.zeros_like(dgq_ref)
            dgkv_ref[...] = jnp.zeros_like(dgkv_ref)

        cosk_, sin_ = cos_ref[...], sin_ref[...]
        cosq = cosk_ + (_iota((tm, 128), 1) < 64).astype(F32)
        dkr = jnp.zeros((tm, 128), F32)
        for h in range(4):
            hs = slice(h * 128, (h + 1) * 128)
            dqh = dq_ref[:, hs] * MLA_SCALE
            dq2_ref[:, hs] = (dqh * cosq).astype(BF)
            dq2_ref[:, 512 + h * 128:512 + (h + 1) * 128] = (dqh * sin_).astype(BF)
            dkr = dkr + dk_ref[:, hs]
        dkv2_ref[:, :512] = dk_ref[...].astype(BF)
        dkv2_ref[:, 512:] = dv_ref[...].astype(BF)
        dckr_ref[...] = (dkr * cosk_).astype(BF)
        dckrs_ref[...] = (dkr * sin_).astype(BF)

        cq, gq_ = cq_ref[...], gq_ref[...]
        cqn, rq = _rms_fwd(cq, gq_)
        cqn_ref[...] = cqn.astype(BF)
        dcq, dgt = _rms_bwd(cq, gq_, rq, _dot(dq2_ref[...], wuq_ref[...]))
        dcq_ref[...] = dcq.astype(BF)
        dgq_ref[...] += jnp.sum(dgt, axis=0, keepdims=True)

        ckv, gkv_ = ckv_ref[...], gkv_ref[...]
        ckvn, rkv = _rms_fwd(ckv, gkv_)
        ckvn_ref[...] = ckvn.astype(BF)
        dckv, dgt2 = _rms_bwd(ckv, gkv_, rkv, _dot(dkv2_ref[...], wkv_ref[...]))
        dckv_ref[...] = dckv.astype(BF)
        dgkv_ref[...] += jnp.sum(dgt2, axis=0, keepdims=True)

    row = lambda w: pl.BlockSpec((tm, w), lambda n: (n, 0))
    return pl.pallas_call(
        body, name="cprep_bwd", grid=(T // tm,),
        in_specs=_cprep_specs(tm) + [pl.BlockSpec((1024, 256), lambda n: (0, 0)),
                                     pl.BlockSpec((1024, 128), lambda n: (0, 0)),
                                     row(512), row(512), row(512)],
        out_specs=[row(256), row(128), row(128), row(128), row(1024), row(1024), row(256), row(128),
                   pl.BlockSpec((1, 256), lambda n: (0, 0)), pl.BlockSpec((1, 128), lambda n: (0, 0))],
        out_shape=[jax.ShapeDtypeStruct((T, 256), BF), jax.ShapeDtypeStruct((T, 128), BF),
                   jax.ShapeDtypeStruct((T, 128), BF), jax.ShapeDtypeStruct((T, 128), BF),
                   jax.ShapeDtypeStruct((T, 1024), BF), jax.ShapeDtypeStruct((T, 1024), BF),
                   jax.ShapeDtypeStruct((T, 256), BF), jax.ShapeDtypeStruct((T, 128), BF),
                   jax.ShapeDtypeStruct((1, 256), F32), jax.ShapeDtypeStruct((1, 128), F32)],
        compiler_params=_params(1))(h32, h32, h32, h32, gq, gkv, cosk, sin, wuq2t, wkv2t, dq, dk, dv)


MLA_TILE = 512


def _causal_mask(t):
    return _iota((t, t), 1) <= _iota((t, t), 0)


def _mla_fwd(q, k, v):
    T = q.shape[0]
    tq = MLA_TILE

    def body(q_ref, k_ref, v_ref, o_ref, lse_ref):
        i = pl.program_id(1)
        qv = q_ref[...]
        mask = _causal_mask(tq)

        def step(j, carry, masked):
            m, l, acc = carry
            rows = pl.ds(pl.multiple_of(j * tq, tq), tq)
            s = _dot_nt(qv, k_ref[rows, :])
            if masked:
                s = jnp.where(mask, s, NEG)
            m_new = jnp.maximum(m, jnp.max(s, axis=-1, keepdims=True))
            p = jnp.exp(s - m_new)
            alpha = jnp.exp(m - m_new)
            l = alpha * l + jnp.sum(p, axis=-1, keepdims=True)
            acc = alpha * acc + _dot(p.astype(BF), v_ref[rows, :])
            return m_new, l, acc

        init = (jnp.full((tq, 1), NEG, F32), jnp.zeros((tq, 1), F32), jnp.zeros((tq, 128), F32))
        carry = lax.fori_loop(0, i, lambda j, c: step(j, c, False), init)
        m, l, acc = step(i, carry, True)
        o_ref[...] = acc * (1.0 / l)
        lse_ref[...] = jnp.broadcast_to(m + jnp.log(l), (tq, 128))

    blk = pl.BlockSpec((tq, 128), lambda h, i: (i, h))
    full = pl.BlockSpec((T, 128), lambda h, i: (0, h))
    return pl.pallas_call(
        body, name="mla_fwd", grid=(4, T // tq), in_specs=[blk, full, full], out_specs=[blk, blk],
        out_shape=[jax.ShapeDtypeStruct((T, 512), F32), jax.ShapeDtypeStruct((T, 512), F32)],
        compiler_params=_params(2))(q, k, v)


def _mla_bwd(q, k, v, o, lse, do):
    T = q.shape[0]
    tq = MLA_TILE

    def body(q_ref, k_ref, v_ref, o_ref, lse_ref, do_ref, dq_ref, dk_ref, dv_ref):
        i = pl.program_id(1)

        @pl.when(i == 0)
        def _():
            dk_ref[...] = jnp.zeros_like(dk_ref)
            dv_ref[...] = jnp.zeros_like(dv_ref)

        qv = q_ref[...]
        do = do_ref[...]
        dob = do.astype(BF)
        dd = jnp.sum(do * o_ref[...], axis=-1, keepdims=True)
        lse_ = lse_ref[:, 0:1]
        mask = _causal_mask(tq)

        def step(j, dq, masked):
            rows = pl.ds(pl.multiple_of(j * tq, tq), tq)
            kj, vj = k_ref[rows, :], v_ref[rows, :]
            s = _dot_nt(qv, kj)
            if masked:
                s = jnp.where(mask, s, NEG)
            p = jnp.exp(s - lse_)
            ds = (p * (_dot_nt(dob, vj) - dd)).astype(BF)
            dk_ref[rows, :] += _dot_tn(ds, qv)
            dv_ref[rows, :] += _dot_tn(p.astype(BF), dob)
            return dq + _dot(ds, kj)

        dq = lax.fori_loop(0, i, lambda j, c: step(j, c, False), jnp.zeros((tq, 128), F32))
        dq_ref[...] = step(i, dq, True)

    blk = pl.BlockSpec((tq, 128), lambda h, i: (i, h))
    full = pl.BlockSpec((T, 128), lambda h, i: (0, h))
    return pl.pallas_call(
        body, name="mla_bwd", grid=(4, T // tq), in_specs=[blk, full, full, blk, blk, blk],
        out_specs=[blk, full, full],
        out_shape=[jax.ShapeDtypeStruct((T, 512), F32)] * 3,
        compiler_params=_params(2))(q, k, v, o, lse, do)


def _sb_tile(qk, rr, strict, masked, upper):
    z2 = qk * (SB_SCALE * LOG2E)
    l1 = jnp.log2(1.0 + jnp.exp2(-jnp.abs(z2)))
    lk = -jnp.maximum(z2, 0.0) - l1
    if masked:
        lk = jnp.where(strict, lk, 0.0)
    after = rr + _dot(lk.astype(BF), upper)
    ll = jnp.minimum(z2, 0.0) - l1
    a = jnp.exp2(ll + after)
    if masked:
        a = jnp.where(strict, a, 0.0)
    return ll, a, jnp.sum(lk, axis=-1, keepdims=True)


SB_TQ, SB_TK = 512, 256
SB_DEAD = -160.0


def _sb_walk(trips, two_steps, carry):
    def alive(c):
        t, cr = c
        top = jnp.maximum(jnp.max(cr[0][0]), jnp.max(cr[1][0]))
        return jnp.logical_and(t < trips, top > SB_DEAD)

    def body(c):
        t, cr = c
        return t + 1, two_steps(t, cr)

    return lax.while_loop(alive, body, (jnp.int32(0), carry))[1]


def _sb_consts(tq, tk):
    row, col = _iota((tq, tk), 0), _iota((tq, tk), 1)
    strict = [col + d * tk < row for d in range(tq // tk)]
    r2, c2 = _iota((tk, tk), 0), _iota((tk, tk), 1)
    return strict, (r2 > c2).astype(BF), (r2 < c2).astype(BF)


def _sb_fwd(h16):
    T = h16.shape[0]
    tq, tk = SB_TQ, SB_TK
    nd = tq // tk

    def body(q_ref, k_ref, v_ref, o_ref):
        i = pl.program_id(1)
        strict, upper, _ = _sb_consts(tq, tk)
        lane = _iota((tq, 128), 1)
        q2 = q_ref[...]
        qms = [jnp.where(lane < 64, q2, jnp.zeros_like(q2)), jnp.where(lane >= 64, q2, jnp.zeros_like(q2))]

        def step(j, carry, d):
            rows = pl.ds(pl.multiple_of(j * tk, tk), tk)
            kj, vj = k_ref[rows, :], v_ref[rows, :]
            out = []
            for hh in range(2):
                rr, acc = carry[hh]
                _, a, rs = _sb_tile(_dot_nt(qms[hh], kj), rr, None if d is None else strict[d], d is not None, upper)
                out.append((rr + rs, acc + _dot(a.astype(BF), vj)))
            return tuple(out)

        def two_steps(t, c):
            j = nd * i - 1 - 2 * t
            return step(j - 1, step(j, c, None), None)

        carry = ((jnp.zeros((tq, 1), F32), jnp.zeros((tq, 128), F32)),) * 2
        for d in reversed(range(nd)):
            carry = step(nd * i + d, carry, d)
        carry = _sb_walk(nd * i // 2, two_steps, carry)
        o_ref[...] = jnp.where(lane < 64, carry[0][1], carry[1][1])

    return pl.pallas_call(
        body, name="sb_fwd", grid=(2, T // tq),
        in_specs=[pl.BlockSpec((tq, 128), lambda p, i: (i, C_DQ // 128 + p)),
                  pl.BlockSpec((T, 128), lambda p, i: (0, C_DK // 128 + p)),
                  pl.BlockSpec((T, 128), lambda p, i: (0, C_DV // 128 + p))],
        out_specs=pl.BlockSpec((tq, 128), lambda p, i: (i, p)),
        out_shape=jax.ShapeDtypeStruct((T, 256), F32),
        compiler_params=_params(2))(h16, h16, h16)


def _sb_bwd(h16, yd, dyd):
    T = h16.shape[0]
    tq, tk = SB_TQ, SB_TK
    nd = tq // tk

    def body(q_ref, k_ref, v_ref, o_ref, do_ref, dq_ref, dk_ref, dv_ref):
        i = pl.program_id(1)

        @pl.when(i == 0)
        def _():
            dk_ref[...] = jnp.zeros_like(dk_ref)
            dv_ref[...] = jnp.zeros_like(dv_ref)

        strict, upper, before = _sb_consts(tq, tk)
        lane = _iota((tq, 128), 1)
        lane_k = _iota((tk, 128), 1)
        q2 = q_ref[...]
        dob2 = do_ref[...].astype(BF)
        doo = dob2.astype(F32) * o_ref[...]
        mines = [lane < 64, lane >= 64]
        qms = [jnp.where(m, q2, jnp.zeros_like(q2)) for m in mines]
        doms = [jnp.where(m, dob2, jnp.zeros_like(dob2)) for m in mines]
        dds = [jnp.sum(jnp.where(m, doo, 0.0), axis=-1, keepdims=True) for m in mines]

        def step(j, carry, d):
            rows = pl.ds(pl.multiple_of(j * tk, tk), tk)
            kj, vj = k_ref[rows, :], v_ref[rows, :]
            out, dks, dvs = [], [], []
            for hh in range(2):
                rr, sg, dq = carry[hh]
                ll, a, rs = _sb_tile(_dot_nt(qms[hh], kj), rr, None if d is None else strict[d], d is not None,
                                     upper)
                ab = a.astype(BF)
                g = _dot_nt(doms[hh], vj) * ab.astype(F32)
                gs = jnp.sum(g, axis=-1, keepdims=True)
                pre = (dds[hh] - sg - gs) + _dot(g.astype(BF), before)
                dz = g - jnp.exp2(ll) * (g + pre)
                if d is not None:
                    dz = jnp.where(strict[d], dz, 0.0)
                dzb = dz.astype(BF)
                dks.append(_dot_tn(dzb, q2))
                dvs.append(_dot_tn(ab, dob2))
                out.append((rr + rs, sg + gs, dq + _dot(dzb, kj)))
            dk_ref[rows, :] += jnp.where(lane_k < 64, dks[0], dks[1]) * SB_SCALE
            dv_ref[rows, :] += jnp.where(lane_k < 64, dvs[0], dvs[1])
            return tuple(out)

        def two_steps(t, c):
            j = nd * i - 1 - 2 * t
            return step(j - 1, step(j, c, None), None)

        zero = jnp.zeros((tq, 1), F32)
        carry = ((zero, zero, jnp.zeros((tq, 128), F32)),) * 2
        for d in reversed(range(nd)):
            carry = step(nd * i + d, carry, d)
        carry = _sb_walk(nd * i // 2, two_steps, carry)
        dq_ref[...] = jnp.where(lane < 64, carry[0][2], carry[1][2]) * SB_SCALE

    blk = lambda c: pl.BlockSpec((tq, 128), lambda p, i: (i, c // 128 + p))
    full = lambda c: pl.BlockSpec((T, 128), lambda p, i: (0, c // 128 + p))
    return pl.pallas_call(
        body, name="sb_bwd", grid=(2, T // tq),
        in_specs=[blk(C_DQ), full(C_DK), full(C_DV), blk(0), blk(0)],
        out_specs=[blk(0), full(0), full(0)],
        out_shape=[jax.ShapeDtypeStruct((T, 256), F32)] * 3,
        compiler_params=_params(2))(h16, h16, h16, yd, dyd)


def _compact_c(ycp):
    return jnp.concatenate([ycp[:, h * 128:h * 128 + 64] for h in range(4)], axis=1)


def _post_fwd(ya, yb, ycp, yd, h32, ggrp, wout, gpost, x):
    T = x.shape[0]
    tm = 256

    def body(ya_ref, yb_ref, yc_ref, yd_ref, gate_ref, gg_ref, w_ref, gp_ref, x_ref, xn_ref, ym_ref, o_ref):
        ys = [ya_ref[...], yb_ref[...], _compact_c(yc_ref[...]), yd_ref[...]]
        gate = gate_ref[...]
        sil = gate * (1.0 / (1.0 + jnp.exp(-gate)))
        parts = []
        for gi in range(4):
            ng, _ = _rms_fwd(ys[gi], gg_ref[:, gi * 256:(gi + 1) * 256])
            parts.append(ng * sil[:, gi * 256:(gi + 1) * 256])
        ym = jnp.concatenate(parts, axis=1).astype(BF)
        ym_ref[...] = ym
        o = _dot(ym, w_ref[...])
        o_ref[...] = o
        on, _ = _rms_fwd(o, gp_ref[...])
        xn_ref[...] = x_ref[...] + on

    row = lambda w: pl.BlockSpec((tm, w), lambda n: (n, 0))
    vec = pl.BlockSpec((1, 1024), lambda n: (0, 0))
    return pl.pallas_call(
        body, name="post_fwd", grid=(T // tm,),
        in_specs=[row(256), row(256), row(512), row(256), pl.BlockSpec((tm, 1024), lambda n: (n, C_GATE // 1024)),
                  vec, pl.BlockSpec((1024, 1024), lambda n: (0, 0)), vec, row(1024)],
        out_specs=[row(1024), row(1024), row(1024)],
        out_shape=[jax.ShapeDtypeStruct((T, 1024), F32), jax.ShapeDtypeStruct((T, 1024), BF),
                   jax.ShapeDtypeStruct((T, 1024), F32)],
        compiler_params=_params(1))(ya, yb, ycp, yd, h32, ggrp, wout, gpost, x)


def _post_bwd(dx, o, gpost, woutt, ya, yb, ycp, yd, h32, ggrp):
    T = dx.shape[0]
    tm = 256

    def body(dx_ref, o_ref, gp_ref, w_ref, ya_ref, yb_ref, yc_ref, yd_ref, gate_ref, gg_ref,
             do_ref, dya_ref, dyb_ref, dyc_ref, dyd_ref, dgate_ref, dgp_ref, dgg_ref):
        n = pl.program_id(0)

        @pl.when(n == 0)
        def _():
            dgp_ref[...] = jnp.zeros_like(dgp_ref)
            dgg_ref[...] = jnp.zeros_like(dgg_ref)

        ov, gp = o_ref[...], gp_ref[...]
        _, ro = _rms_fwd(ov, gp)
        do, dgt = _rms_bwd(ov, gp, ro, dx_ref[...])
        dgp_ref[...] += jnp.sum(dgt, axis=0, keepdims=True)
        dob = do.astype(BF)
        do_ref[...] = dob
        dym = _dot(dob, w_ref[...])
        gate = gate_ref[...]
        sg = 1.0 / (1.0 + jnp.exp(-gate))
        sil = gate * sg
        dsil = sg * (1.0 + gate * (1.0 - sg))
        ys = [ya_ref[...], yb_ref[...], _compact_c(yc_ref[...]), yd_ref[...]]
        dys = []
        for gi in range(4):
            gs = slice(gi * 256, (gi + 1) * 256)
            gg = gg_ref[:, gs]
            ng, rg = _rms_fwd(ys[gi], gg)
            dgate_ref[:, gs] = (dym[:, gs] * ng * dsil[:, gs]).astype(BF)
            dy, dgt2 = _rms_bwd(ys[gi], gg, rg, dym[:, gs] * sil[:, gs])
            dgg_ref[:, gs] += jnp.sum(dgt2, axis=0, keepdims=True)
            dys.append(dy)
        dya_ref[...] = dys[0]
        dyb_ref[...] = dys[1]
        dyd_ref[...] = dys[3]
        z64 = jnp.zeros((tm, 64), F32)
        dyc_ref[...] = jnp.concatenate(
            [piece for h in range(4) for piece in (dys[2][:, h * 64:(h + 1) * 64], z64)], axis=1)

    row = lambda w: pl.BlockSpec((tm, w), lambda n: (n, 0))
    vec = pl.BlockSpec((1, 1024), lambda n: (0, 0))
    return pl.pallas_call(
        body, name="post_bwd", grid=(T // tm,),
        in_specs=[row(1024), row(1024), vec, pl.BlockSpec((1024, 1024), lambda n: (0, 0)),
                  row(256), row(256), row(512), row(256),
                  pl.BlockSpec((tm, 1024), lambda n: (n, C_GATE // 1024)), vec],
        out_specs=[row(1024), row(256), row(256), row(512), row(256), row(1024), vec, vec],
        out_shape=[jax.ShapeDtypeStruct((T, 1024), BF), jax.ShapeDtypeStruct((T, 256), F32),
                   jax.ShapeDtypeStruct((T, 256), F32), jax.ShapeDtypeStruct((T, 512), F32),
                   jax.ShapeDtypeStruct((T, 256), F32), jax.ShapeDtypeStruct((T, 1024), BF),
                   jax.ShapeDtypeStruct((1, 1024), F32), jax.ShapeDtypeStruct((1, 1024), F32)],
        compiler_params=_params(1))(dx, o, gpost, woutt, ya, yb, ycp, yd, h32, ggrp)


def _loss_head(y, tgt):
    T = y.shape[0]
    tm = 512 if T % 512 == 0 else T

    def body(y_ref, t_ref, s_ref, dy_ref):
        n = pl.program_id(0)

        @pl.when(n == 0)
        def _():
            s_ref[...] = jnp.zeros_like(s_ref)

        d = y_ref[...] - t_ref[...]
        s_ref[...] += jnp.sum(d * d, axis=0, keepdims=True)
        dy_ref[...] = d * (1.0 / D_MODEL)

    row = pl.BlockSpec((tm, 1024), lambda n: (n, 0))
    return pl.pallas_call(
        body, name="loss_head", grid=(T // tm,), in_specs=[row, row],
        out_specs=[pl.BlockSpec((1, 1024), lambda n: (0, 0)), row],
        out_shape=[jax.ShapeDtypeStruct((1, 1024), F32), jax.ShapeDtypeStruct((T, 1024), F32)],
        compiler_params=_params(1))(y, tgt)


def _swap_rows32(a):
    return jnp.concatenate([a[16:32], a[0:16]], axis=0)


def _pad_w_in_t(wt):
    z = lambda n: jnp.zeros((n, wt.shape[1]), wt.dtype)
    kr = wt[1664:1696]
    return jnp.concatenate([wt[2464:3488], wt[0:1664], z(64), kr, z(32), z(64), _swap_rows32(kr), z(32),
                            wt[1696:2464], z(NP - C_END)], axis=0)


def _unpad_w_in_t(d):
    kr = d[C_CKR + 64:C_CKR + 96] + _swap_rows32(d[C_CKRS + 64:C_CKRS + 96])
    return jnp.concatenate([d[C_AQ:C_CKR], kr, d[C_DQ:C_END], d[0:1024]], axis=0)


def _pad_w_uq(w):
    z = lambda n: jnp.zeros((w.shape[0], n), w.dtype)
    a = [p for h in range(4) for p in (w[:, 96 * h:96 * h + 96], z(32))]
    b = [p for h in range(4) for p in (z(64), _swap32(w[:, 96 * h + 64:96 * h + 96]), z(32))]
    return jnp.concatenate(a + b, axis=1)


def _unpad_w_uq(d):
    out = []
    for h in range(4):
        out.append(d[:, 128 * h:128 * h + 64])
        out.append(d[:, 128 * h + 64:128 * h + 96] + _swap32(d[:, 512 + 128 * h + 64:512 + 128 * h + 96]))
    return jnp.concatenate(out, axis=1)


def _pad_w_ukv(w):
    z = jnp.zeros((w.shape[0], 64), w.dtype)
    a = [p for h in range(4) for p in (w[:, 128 * h:128 * h + 64], z)]
    b = [p for h in range(4) for p in (w[:, 128 * h + 64:128 * h + 128], z)]
    return jnp.concatenate(a + b, axis=1)


def _unpad_w_ukv(d):
    return jnp.concatenate([p for h in range(4) for p in (d[:, 128 * h:128 * h + 64],
                                                          d[:, 512 + 128 * h:512 + 128 * h + 64])], axis=1)


def _rope_tables(pos):
    freqs = 10000.0 ** (-jnp.arange(16, dtype=F32) / 16)
    ang = pos.astype(F32)[:, None] * freqs
    c, s = jnp.cos(ang), jnp.sin(ang)
    z = lambda n: jnp.zeros((pos.shape[0], n), F32)
    return (jnp.concatenate([z(64), c, c, z(32)], axis=1), jnp.concatenate([z(64), -s, s, z(32)], axis=1))


def _layer_weights(W, l):
    wuq2 = _pad_w_uq(W["mla_w_uq"][l])
    wkv2 = _pad_w_ukv(W["mla_w_ukv"][l])
    wout = W["w_out"][l]
    cw = jnp.concatenate([W["conv_w"][l].astype(F32), jnp.zeros((5, 256), F32)], axis=0)
    return dict(
        wpt=_pad_w_in_t(W["w_in_t"][l]).astype(BF), wuq2=wuq2.astype(BF), wuq2t=wuq2.T.astype(BF),
        wkv2=wkv2.astype(BF), wkv2t=wkv2.T.astype(BF), wout=wout.astype(BF), woutt=wout.T.astype(BF),
        cw=cw, cb=W["conv_b"][l][None, :], sinks=W["attn_sinks"][l],
        gpre=W["norm_pre"][l][None, :], gq=W["mla_q_norm"][l][None, :], gkv=W["mla_kv_norm"][l][None, :],
        ggrp=W["group_norm"][l][None, :], gpost=W["norm_post"][l][None, :])


def _local_step(x, pos, W, tgt):
    cosk, sin = _rope_tables(pos)
    saved = []
    for l in range(DEPTH):
        lw = _layer_weights(W, l)
        h32, h16, xn = _inproj_fwd(x, lw["gpre"], lw["wpt"])
        ya = _swa_fwd(h16, lw["sinks"])
        yb = _conv_fwd(h32, lw["cw"], lw["cb"])
        qc, kc, vc = _cprep_fwd(h32, lw["gq"], lw["gkv"], lw["wuq2"], lw["wkv2"], cosk, sin)
        ycp, lse = _mla_fwd(qc, kc, vc)
        yd = _sb_fwd(h16)
        x_new, ym, o = _post_fwd(ya, yb, ycp, yd, h32, lw["ggrp"], lw["wout"], lw["gpost"], x)
        saved.append(dict(lw=lw, x=x, h32=h32, h16=h16, xn=xn, ya=ya, yb=yb, qc=qc, kc=kc, vc=vc, ycp=ycp,
                          lse=lse, yd=yd, ym=ym, o=o))
        x = x_new
    sq, dx = _loss_head(x, tgt)

    grads = {k: [None] * DEPTH for k in ("norm_pre", "w_in_t", "attn_sinks", "conv_w", "conv_b", "mla_q_norm",
                                         "mla_w_uq", "mla_kv_norm", "mla_w_ukv", "group_norm", "w_out",
                                         "norm_post")}
    for l in reversed(range(DEPTH)):
        s = saved[l]
        lw = s["lw"]
        do, dya, dyb, dycp, dyd, dgate, dgpost, dggrp = _post_bwd(
            dx, s["o"], lw["gpost"], lw["woutt"], s["ya"], s["yb"], s["ycp"], s["yd"], s["h32"], lw["ggrp"])
        grads["norm_post"][l] = dgpost[0]
        grads["group_norm"][l] = dggrp[0]
        grads["w_out"][l] = _matmul_tn(s["ym"], do, 512, "dw_out")
        sdq, sdk, sdv = _sb_bwd(s["h16"], s["yd"], dyd)
        mdq, mdk, mdv = _mla_bwd(s["qc"], s["kc"], s["vc"], s["ycp"], s["lse"], dycp)
        (dcq, dckv, dckr, dckrs, dq2, dkv2, cqn, ckvn, dgq, dgkv) = _cprep_bwd(
            s["h32"], lw["gq"], lw["gkv"], lw["wuq2t"], lw["wkv2t"], cosk, sin, mdq, mdk, mdv)
        grads["mla_q_norm"][l] = dgq[0]
        grads["mla_kv_norm"][l] = dgkv[0]
        grads["mla_w_uq"][l] = _unpad_w_uq(_matmul_tn(cqn, dq2, 512, "dw_uq"))
        grads["mla_w_ukv"][l] = _unpad_w_ukv(_matmul_tn(ckvn, dkv2, 512, "dw_ukv"))
        dbb, dbc, dbx, dcw = _conv_bwd(s["h32"], lw["cw"], lw["cb"], dyb)
        grads["conv_w"][l] = dcw[0:3]
        grads["conv_b"][l] = dcw[3]
        adq, adk, adv, dsk = _swa_bwd(s["h16"], lw["sinks"], dya)
        grads["attn_sinks"][l] = dsk[0, 0:4]
        T = x.shape[0]
        dh = jnp.concatenate([dgate, adq, adk.astype(BF), adv.astype(BF), dbb, dbc, dbx, dcq, dckv, dckr, dckrs,
                              sdq.astype(BF), sdk.astype(BF), sdv.astype(BF), jnp.zeros((T, NP - C_END), BF)],
                             axis=1)
        grads["w_in_t"][l] = _unpad_w_in_t(_matmul_tn(s["xn"], dh, 1280, "dw_in").T)
        dx, dgpre = _inproj_bwd_x(dh, lw["wpt"], s["x"], lw["gpre"], dx)
        grads["norm_pre"][l] = dgpre[0]
    return sq, dx, {k: jnp.stack(v) for k, v in grads.items()}


SMALL_SHARDED = ("conv_w", "mla_w_uq", "mla_w_ukv")
REPLICATED = ("norm_pre", "attn_sinks", "conv_b", "mla_q_norm", "mla_kv_norm", "group_norm", "norm_post")
ORDER = ("norm_pre", "w_in", "attn_sinks", "conv_w", "conv_b", "mla_q_norm", "mla_w_uq", "mla_kv_norm",
         "mla_w_ukv", "group_norm", "w_out", "norm_post")
W_IN_COLS = 436
W_IN_WIN = 440
SMALL_ROWS = 48


def _pack_small(arrs, dtype):
    flat = jnp.concatenate([a.reshape(-1).astype(dtype) for a in arrs])
    flat = jnp.concatenate([flat, jnp.zeros((SMALL_ROWS * D_MODEL - flat.shape[0],), dtype)])
    return flat.reshape(SMALL_ROWS, D_MODEL)


def _unpack_small(buf, shapes):
    flat = buf.reshape(-1)
    out, off = [], 0
    for s in shapes:
        n = int(np.prod(s))
        out.append(flat[off:off + n].reshape(s))
        off += n
    return out


def _pack_state(p, c):
    wt = jnp.transpose(p["w_in"], (0, 2, 1))
    win = lax.dynamic_update_slice(jnp.zeros((DEPTH, W_IN_WIN, D_MODEL), F32), wt, (0, 4 * c, 0))
    small = _pack_small([p[n] for n in SMALL_SHARDED + REPLICATED], F32)
    return jnp.concatenate([win.reshape(DEPTH * W_IN_WIN, D_MODEL), p["w_out"].reshape(DEPTH * 128, D_MODEL), small],
                           axis=0)


def _unpack_state(buf, p, c):
    nw = DEPTH * W_IN_WIN
    win = lax.dynamic_slice(buf[0:nw].reshape(DEPTH, W_IN_WIN, D_MODEL), (0, 4 * c, 0), (DEPTH, W_IN_COLS, D_MODEL))
    out = {"w_in": jnp.transpose(win, (0, 2, 1)), "w_out": buf[nw:nw + DEPTH * 128].reshape(DEPTH, 128, D_MODEL)}
    names = SMALL_SHARDED + REPLICATED
    for n, a in zip(names, _unpack_small(buf[nw + DEPTH * 128:], [p[n].shape for n in names])):
        out[n] = a
    return out


def _me():
    return lax.axis_index("x"), lax.axis_index("y"), lax.axis_index("c")


def _peer(k):
    x, y, c = _me()
    return (x ^ (k >> 2 & 1), y ^ (k >> 1 & 1), c ^ (k & 1))


def _all_gather(block):
    R, C = block.shape

    def body(src_ref, out_ref, send_sems, recv_sems, local_sem):
        x, y, c = _me()
        me = 4 * x + 2 * y + c
        mine = pltpu.make_async_copy(src_ref, out_ref.at[me], local_sem)
        mine.start()
        copies = [pltpu.make_async_remote_copy(src_ref=src_ref, dst_ref=out_ref.at[me], send_sem=send_sems.at[k - 1],
                                               recv_sem=recv_sems.at[k - 1], device_id=_peer(k), device_id_type=MESH)
                  for k in range(1, N_DEV)]
        for cp in copies:
            cp.start()
        for cp in copies:
            cp.wait()
        mine.wait()

    return pl.pallas_call(
        body, name="all_gather", out_shape=jax.ShapeDtypeStruct((N_DEV, R, C), block.dtype),
        in_specs=[pl.BlockSpec(memory_space=pl.ANY)], out_specs=pl.BlockSpec(memory_space=pl.ANY),
        scratch_shapes=[pltpu.SemaphoreType.DMA((N_DEV - 1,)), pltpu.SemaphoreType.DMA((N_DEV - 1,)),
                        pltpu.SemaphoreType.DMA])(block)


def _all_to_all(blocks):
    _, R, C = blocks.shape

    def body(src_ref, out_ref, send_sems, recv_sems, local_sem):
        x, y, c = _me()
        me = 4 * x + 2 * y + c
        mine = pltpu.make_async_copy(src_ref.at[me], out_ref.at[me], local_sem)
        mine.start()
        copies = []
        for k in range(1, N_DEV):
            px, py, pc = _peer(k)
            copies.append(pltpu.make_async_remote_copy(
                src_ref=src_ref.at[4 * px + 2 * py + pc], dst_ref=out_ref.at[me], send_sem=send_sems.at[k - 1],
                recv_sem=recv_sems.at[k - 1], device_id=(px, py, pc), device_id_type=MESH))
        for cp in copies:
            cp.start()
        for cp in copies:
            cp.wait()
        mine.wait()

    return pl.pallas_call(
        body, name="all_to_all", out_shape=jax.ShapeDtypeStruct((N_DEV, R, C), blocks.dtype),
        in_specs=[pl.BlockSpec(memory_space=pl.ANY)], out_specs=pl.BlockSpec(memory_space=pl.ANY),
        scratch_shapes=[pltpu.SemaphoreType.DMA((N_DEV - 1,)), pltpu.SemaphoreType.DMA((N_DEV - 1,)),
                        pltpu.SemaphoreType.DMA])(blocks)


def _adamw(parts, w, m, v):
    R, C = w.shape
    tr = 32
    assert R % tr == 0

    def body(p_ref, w_ref, m_ref, v_ref, g_ref, d_ref, nm_ref, nv_ref):
        g = p_ref[0].astype(F32)
        for k in range(1, N_DEV):
            g = g + p_ref[k].astype(F32)
        g_ref[...] = g
        m_ = ADAM_B1 * m_ref[...] + (1.0 - ADAM_B1) * g
        v_ = ADAM_B2 * v_ref[...] + (1.0 - ADAM_B2) * (g * g)
        nm_ref[...] = m_
        nv_ref[...] = v_
        m_hat = m_ / (1.0 - ADAM_B1 ** ADAM_STEP)
        v_hat = v_ / (1.0 - ADAM_B2 ** ADAM_STEP)
        d_ref[...] = -ADAM_LR * (m_hat / (jnp.sqrt(v_hat) + ADAM_EPS) + ADAM_WD * w_ref[...])

    row = pl.BlockSpec((tr, C), lambda n: (n, 0))
    return pl.pallas_call(
        body, name="adamw", grid=(R // tr,),
        in_specs=[pl.BlockSpec((N_DEV, tr, C), lambda n: (0, n, 0)), row, row, row],
        out_specs=[row] * 4, out_shape=[jax.ShapeDtypeStruct((R, C), F32)] * 4,
        compiler_params=_params(1))(parts, w, m, v)


def kernel(x, positions, norm_pre, w_in, attn_sinks, conv_w, conv_b, mla_q_norm, mla_w_uq, mla_kv_norm, mla_w_ukv, group_norm, w_out, norm_post, loss_target, m_norm_pre, m_w_in, m_attn_sinks, m_conv_w, m_conv_b, m_mla_q_norm, m_mla_w_uq, m_mla_kv_norm, m_mla_w_ukv, m_group_norm, m_w_out, m_norm_post, v_norm_pre, v_w_in, v_attn_sinks, v_conv_w, v_conv_b, v_mla_q_norm, v_mla_w_uq, v_mla_kv_norm, v_mla_w_ukv, v_group_norm, v_w_out, v_norm_post):
    local = dict(norm_pre=norm_pre, w_in=w_in, attn_sinks=attn_sinks, conv_w=conv_w, conv_b=conv_b,
                 mla_q_norm=mla_q_norm, mla_w_uq=mla_w_uq, mla_kv_norm=mla_kv_norm, mla_w_ukv=mla_w_ukv,
                 group_norm=group_norm, w_out=w_out, norm_post=norm_post)
    mom = dict(norm_pre=m_norm_pre, w_in=m_w_in, attn_sinks=m_attn_sinks, conv_w=m_conv_w, conv_b=m_conv_b,
               mla_q_norm=m_mla_q_norm, mla_w_uq=m_mla_w_uq, mla_kv_norm=m_mla_kv_norm, mla_w_ukv=m_mla_w_ukv,
               group_norm=m_group_norm, w_out=m_w_out, norm_post=m_norm_post)
    vel = dict(norm_pre=v_norm_pre, w_in=v_w_in, attn_sinks=v_attn_sinks, conv_w=v_conv_w, conv_b=v_conv_b,
               mla_q_norm=v_mla_q_norm, mla_w_uq=v_mla_w_uq, mla_kv_norm=v_mla_kv_norm, mla_w_ukv=v_mla_w_ukv,
               group_norm=v_group_norm, w_out=v_w_out, norm_post=v_norm_post)

    c = lax.axis_index("c")

    wt = jnp.transpose(w_in, (0, 2, 1)).astype(BF)
    wt = jnp.concatenate([wt, jnp.zeros((DEPTH, 448 - W_IN_COLS, D_MODEL), BF)], axis=1)
    payload = jnp.concatenate([wt.reshape(DEPTH * 448, D_MODEL), w_out.astype(BF).reshape(DEPTH * 128, D_MODEL),
                               _pack_small([local[n] for n in SMALL_SHARDED], BF)], axis=0)
    gathered = _all_gather(payload)
    W = {n: local[n] for n in REPLICATED}
    W["w_in_t"] = jnp.stack([jnp.concatenate([gathered[d, 448 * l:448 * l + W_IN_COLS] for d in range(N_DEV)], axis=0)
                             for l in range(DEPTH)])
    wo0 = DEPTH * 448
    W["w_out"] = jnp.stack([gathered[:, wo0 + 128 * l:wo0 + 128 * (l + 1)].reshape(D_MODEL, D_MODEL)
                            for l in range(DEPTH)])
    small = [_unpack_small(gathered[d, wo0 + DEPTH * 128:], [local[n].shape for n in SMALL_SHARDED])
             for d in range(N_DEV)]
    for i, n in enumerate(SMALL_SHARDED):
        W[n] = jnp.concatenate([small[d][i] for d in range(N_DEV)], axis=2)

    sq, grad_x, g = _local_step(x[0], positions[0], W, loss_target[0])
    loss = lax.psum(0.5 / D_MODEL * jnp.sum(sq), ("x", "y", "c"))

    blocks = []
    for d in range(N_DEV):
        lo = W_IN_COLS * d // 8 * 8
        parts = [g["w_in_t"][:, lo:lo + W_IN_WIN].reshape(DEPTH * W_IN_WIN, D_MODEL),
                 g["w_out"][:, 128 * d:128 * (d + 1)].reshape(DEPTH * 128, D_MODEL)]
        sm = [lax.slice_in_dim(g[n], d * local[n].shape[2], (d + 1) * local[n].shape[2], axis=2)
              for n in SMALL_SHARDED]
        parts.append(_pack_small(sm + [g[n] for n in REPLICATED], F32))
        blocks.append(jnp.concatenate(parts, axis=0).astype(BF))
    received = _all_to_all(jnp.stack(blocks))

    bufs = _adamw(received, _pack_state(local, c), _pack_state(mom, c), _pack_state(vel, c))
    gs, ds, ms, vs = [_unpack_state(b, local, c) for b in bufs]
    return (loss, grad_x[None], *[gs[n] for n in ORDER], *[ds[n] for n in ORDER],
            *[ms[n] for n in ORDER], *[vs[n] for n in ORDER])
```

```python
import functools

import jax
import jax.numpy as jnp
import numpy as np
from jax import lax
from jax.experimental import pallas as pl
from jax.experimental.pallas import tpu as pltpu

F32 = jnp.float32
BF = jnp.bfloat16
MESH = pl.DeviceIdType.MESH

D_MODEL = 1024
DEPTH = 2
EPS = 1e-6
N_DEV = 8
VMEM_LIMIT = 56 * 1024 * 1024
NEG = -1e30
MLA_SCALE = 96.0 ** -0.5
SB_SCALE = 0.125
LOG2E = 1.4426950408889634

NP = 3840
C_GATE = 0
C_AQ = 1024
C_AK = 1280
C_AV = 1408
C_BB = 1536
C_BC = 1792
C_BX = 2048
C_CQ = 2304
C_CKV = 2560
C_CKR = 2688
C_CKRS = 2816
C_DQ = 2944
C_DK = 3200
C_DV = 3456
C_END = 3712

def _swap32(a):
    return jnp.concatenate([a[:, 16:32], a[:, 0:16]], axis=1)

ADAM_LR, ADAM_B1, ADAM_B2, ADAM_EPS, ADAM_WD, ADAM_STEP = 0.001, 0.9, 0.999, 1e-08, 0.01, 10


def _dot(a, b):
    return jnp.dot(a, b, preferred_element_type=F32)


def _dot_nt(a, b):
    return lax.dot_general(a, b, (((1,), (1,)), ((), ())), preferred_element_type=F32)


def _dot_tn(a, b):
    return lax.dot_general(a, b, (((0,), (0,)), ((), ())), preferred_element_type=F32)


def _params(n_grid):
    return pltpu.CompilerParams(dimension_semantics=("arbitrary",) * n_grid, vmem_limit_bytes=VMEM_LIMIT)


def _rms_fwd(x, g):
    r = lax.rsqrt(jnp.mean(x * x, axis=-1, keepdims=True) + EPS)
    return (x * r) * g, r


def _rms_bwd(x, g, r, dy, width=None):
    n = x.shape[-1] if width is None else width
    u = dy * g
    dx = r * u - x * (r * r * r) * (jnp.sum(x * u, axis=-1, keepdims=True) / n)
    return dx, dy * (x * r)


def _iota(shape, axis):
    return lax.broadcasted_iota(jnp.int32, shape, axis)


def _inproj_fwd(x, g, wpt):
    T = x.shape[0]
    tm = 256

    def body(x_ref, g_ref, w_ref, h32_ref, h16_ref, xn_ref):
        xn, _ = _rms_fwd(x_ref[...], g_ref[...])
        xn = xn.astype(BF)
        xn_ref[...] = xn
        h = _dot_nt(xn, w_ref[...])
        h32_ref[...] = h
        h16_ref[...] = h.astype(BF)

    return pl.pallas_call(
        body, name="inproj_fwd", grid=(T // tm,),
        in_specs=[pl.BlockSpec((tm, D_MODEL), lambda n: (n, 0)),
                  pl.BlockSpec((1, D_MODEL), lambda n: (0, 0)),
                  pl.BlockSpec((NP, D_MODEL), lambda n: (0, 0))],
        out_specs=[pl.BlockSpec((tm, NP), lambda n: (n, 0)),
                   pl.BlockSpec((tm, NP), lambda n: (n, 0)),
                   pl.BlockSpec((tm, D_MODEL), lambda n: (n, 0))],
        out_shape=[jax.ShapeDtypeStruct((T, NP), F32), jax.ShapeDtypeStruct((T, NP), BF),
                   jax.ShapeDtypeStruct((T, D_MODEL), BF)],
        compiler_params=_params(1))(x, g, wpt)


def _inproj_bwd_x(parts, wpt, x, g, dxo):
    T = x.shape[0]
    tm = 256
    np_ = len(parts)
    assert sum(p.shape[1] for p in parts) == C_END

    def body(*refs):
        part_refs = refs[:np_]
        w_ref, x_ref, g_ref, dxo_ref, dh_ref, dx_ref, dg_ref = refs[np_:]
        n = pl.program_id(0)
        dh = jnp.concatenate([r[...].astype(BF) for r in part_refs] + [jnp.zeros((tm, NP - C_END), BF)], axis=1)
        dh_ref[...] = dh
        dxn = _dot(dh, w_ref[...])
        xv = x_ref[...]
        _, r = _rms_fwd(xv, g_ref[...])
        dx, dgt = _rms_bwd(xv, g_ref[...], r, dxn)
        dx_ref[...] = dxo_ref[...] + dx

        @pl.when(n == 0)
        def _():
            dg_ref[...] = jnp.zeros_like(dg_ref)

        dg_ref[...] += jnp.sum(dgt, axis=0, keepdims=True)

    return pl.pallas_call(
        body, name="inproj_bwd_x", grid=(T // tm,),
        in_specs=[pl.BlockSpec((tm, p.shape[1]), lambda n: (n, 0)) for p in parts]
        + [pl.BlockSpec((NP, D_MODEL), lambda n: (0, 0)),
           pl.BlockSpec((tm, D_MODEL), lambda n: (n, 0)),
           pl.BlockSpec((1, D_MODEL), lambda n: (0, 0)),
           pl.BlockSpec((tm, D_MODEL), lambda n: (n, 0))],
        out_specs=[pl.BlockSpec((tm, NP), lambda n: (n, 0)),
                   pl.BlockSpec((tm, D_MODEL), lambda n: (n, 0)),
                   pl.BlockSpec((1, D_MODEL), lambda n: (0, 0))],
        out_shape=[jax.ShapeDtypeStruct((T, NP), BF), jax.ShapeDtypeStruct((T, D_MODEL), F32),
                   jax.ShapeDtypeStruct((1, D_MODEL), F32)],
        compiler_params=_params(1))(*parts, wpt, x, g, dxo)


def _matmul_tn(a, b, tn, name):
    T, M = a.shape
    N = b.shape[1]
    tk = 512 if T % 512 == 0 else T

    def body(a_ref, b_ref, o_ref):
        k = pl.program_id(1)

        @pl.when(k == 0)
        def _():
            o_ref[...] = jnp.zeros_like(o_ref)

        o_ref[...] += _dot_tn(a_ref[...], b_ref[...])

    return pl.pallas_call(
        body, name=name, grid=(N // tn, T // tk),
        in_specs=[pl.BlockSpec((tk, M), lambda j, k: (k, 0)),
                  pl.BlockSpec((tk, tn), lambda j, k: (k, j))],
        out_specs=pl.BlockSpec((M, tn), lambda j, k: (0, j)),
        out_shape=jax.ShapeDtypeStruct((M, N), F32),
        compiler_params=_params(2))(a, b)


def _swa_probs(q, kc, kp, sink, mask_c, mask_p):
    sc = jnp.where(mask_c, _dot_nt(q, kc) * SB_SCALE, NEG)
    sp = jnp.where(mask_p, _dot_nt(q, kp) * SB_SCALE, NEG)
    m = jnp.maximum(jnp.maximum(jnp.max(sc, axis=-1, keepdims=True), jnp.max(sp, axis=-1, keepdims=True)), sink)
    pc = jnp.exp(sc - m)
    pp = jnp.exp(sp - m)
    ps = jnp.exp(sink - m)
    inv = 1.0 / (jnp.sum(pc, axis=-1, keepdims=True) + jnp.sum(pp, axis=-1, keepdims=True) + ps)
    return pc * inv, pp * inv, ps * inv


def _swa_masks(n, tq):
    row = _iota((tq, tq), 0)
    col = _iota((tq, tq), 1)
    return col <= row, jnp.logical_and(col > row, n > 0)


def _swa_specs(tq):
    return [pl.BlockSpec(memory_space=pltpu.SMEM),
            pl.BlockSpec((tq, 256), lambda n: (n, C_AQ // 256)),
            pl.BlockSpec((tq, 128), lambda n: (n, C_AK // 128)),
            pl.BlockSpec((tq, 128), lambda n: (jnp.maximum(n - 1, 0), C_AK // 128)),
            pl.BlockSpec((tq, 128), lambda n: (n, C_AV // 128)),
            pl.BlockSpec((tq, 128), lambda n: (jnp.maximum(n - 1, 0), C_AV // 128))]


def _swa_fwd(h16, sinks):
    T = h16.shape[0]
    tq = 128

    def body(s_ref, q_ref, kc_ref, kp_ref, vc_ref, vp_ref, o_ref):
        n = pl.program_id(0)
        mask_c, mask_p = _swa_masks(n, tq)
        for h in range(4):
            g = h // 2
            q = q_ref[:, h * 64:(h + 1) * 64]
            pc, pp, _ = _swa_probs(q, kc_ref[:, g * 64:(g + 1) * 64], kp_ref[:, g * 64:(g + 1) * 64],
                                   s_ref[h], mask_c, mask_p)
            o_ref[:, h * 64:(h + 1) * 64] = (_dot(pc.astype(BF), vc_ref[:, g * 64:(g + 1) * 64])
                                             + _dot(pp.astype(BF), vp_ref[:, g * 64:(g + 1) * 64]))

    return pl.pallas_call(
        body, name="swa_fwd", grid=(T // tq,), in_specs=_swa_specs(tq),
        out_specs=pl.BlockSpec((tq, 256), lambda n: (n, 0)),
        out_shape=jax.ShapeDtypeStruct((T, 256), F32),
        compiler_params=_params(1))(sinks, h16, h16, h16, h16, h16)


def _swa_bwd(h16, sinks, dya):
    T = h16.shape[0]
    tq = 128

    def body(s_ref, q_ref, kc_ref, kp_ref, vc_ref, vp_ref, do_ref, dq_ref, dk_ref, dv_ref, ds_ref):
        n = pl.program_id(0)

        @pl.when(n == 0)
        def _():
            dk_ref[...] = jnp.zeros_like(dk_ref)
            dv_ref[...] = jnp.zeros_like(dv_ref)
            ds_ref[...] = jnp.zeros_like(ds_ref)

        mask_c, mask_p = _swa_masks(n, tq)
        rc = pl.ds(pl.multiple_of(n * tq, tq), tq)
        rp = pl.ds(pl.multiple_of(jnp.maximum(n - 1, 0) * tq, tq), tq)
        lane = _iota((8, 128), 1)
        row8 = _iota((8, 128), 0)
        for h in range(4):
            g = h // 2
            hs = slice(h * 64, (h + 1) * 64)
            gs = slice(g * 64, (g + 1) * 64)
            q = q_ref[:, hs]
            kc, kp, vc, vp = kc_ref[:, gs], kp_ref[:, gs], vc_ref[:, gs], vp_ref[:, gs]
            pc, pp, ps = _swa_probs(q, kc, kp, s_ref[h], mask_c, mask_p)
            pcb, ppb = pc.astype(BF), pp.astype(BF)
            do = do_ref[:, hs]
            dob = do.astype(BF)
            o = _dot(pcb, vc) + _dot(ppb, vp)
            dd = jnp.sum(do * o, axis=-1, keepdims=True)
            dsc = (pc * (_dot_nt(dob, vc) - dd) * SB_SCALE).astype(BF)
            dsp = (pp * (_dot_nt(dob, vp) - dd) * SB_SCALE).astype(BF)
            dq_ref[:, hs] = (_dot(dsc, kc) + _dot(dsp, kp)).astype(BF)
            dk_ref[rc, gs] += _dot_tn(dsc, q)
            dk_ref[rp, gs] += _dot_tn(dsp, q)
            dv_ref[rc, gs] += _dot_tn(pcb, dob)
            dv_ref[rp, gs] += _dot_tn(ppb, dob)
            dsink = -jnp.sum(ps * dd)
            ds_ref[...] += jnp.where(jnp.logical_and(lane == h, row8 == 0), dsink, 0.0)

    return pl.pallas_call(
        body, name="swa_bwd", grid=(T // tq,),
        in_specs=_swa_specs(tq) + [pl.BlockSpec((tq, 256), lambda n: (n, 0))],
        out_specs=[pl.BlockSpec((tq, 256), lambda n: (n, 0)),
                   pl.BlockSpec((T, 128), lambda n: (0, 0)),
                   pl.BlockSpec((T, 128), lambda n: (0, 0)),
                   pl.BlockSpec((8, 128), lambda n: (0, 0))],
        out_shape=[jax.ShapeDtypeStruct((T, 256), BF), jax.ShapeDtypeStruct((T, 128), F32),
                   jax.ShapeDtypeStruct((T, 128), F32), jax.ShapeDtypeStruct((8, 128), F32)],
        compiler_params=_params(1))(sinks, h16, h16, h16, h16, h16, dya)


def _conv_u(bc_ref, bx_ref, bch_ref, bxh_ref, n, tm):
    u = bc_ref[...] * bx_ref[...]
    uh = bch_ref[...] * bxh_ref[...] * (n > 0).astype(F32)
    rowi = _iota((tm, 256), 0)
    u1 = jnp.where(rowi == 0, uh[7:8, :], pltpu.roll(u, 1, axis=0))
    u2 = jnp.where(rowi == 0, uh[6:7, :], jnp.where(rowi == 1, uh[7:8, :], pltpu.roll(u, 2, axis=0)))
    return u, u1, u2


def _conv_fwd(h32, cw, cb):
    T = h32.shape[0]
    tm = 512 if T % 512 == 0 else T
    hb = tm // 8

    def body(bb_ref, bc_ref, bx_ref, bch_ref, bxh_ref, w_ref, b_ref, o_ref):
        n = pl.program_id(0)
        u, u1, u2 = _conv_u(bc_ref, bx_ref, bch_ref, bxh_ref, n, tm)
        y = w_ref[0:1, :] * u2 + w_ref[1:2, :] * u1 + w_ref[2:3, :] * u + b_ref[...]
        o_ref[...] = bb_ref[...] * y

    halo = lambda c: pl.BlockSpec((8, 256), lambda n: (jnp.maximum(n * hb - 1, 0), c // 256))
    return pl.pallas_call(
        body, name="conv_fwd", grid=(T // tm,),
        in_specs=[pl.BlockSpec((tm, 256), lambda n: (n, C_BB // 256)),
                  pl.BlockSpec((tm, 256), lambda n: (n, C_BC // 256)),
                  pl.BlockSpec((tm, 256), lambda n: (n, C_BX // 256)),
                  halo(C_BC), halo(C_BX),
                  pl.BlockSpec((8, 256), lambda n: (0, 0)),
                  pl.BlockSpec((1, 256), lambda n: (0, 0))],
        out_specs=pl.BlockSpec((tm, 256), lambda n: (n, 0)),
        out_shape=jax.ShapeDtypeStruct((T, 256), F32),
        compiler_params=_params(1))(h32, h32, h32, h32, h32, cw, cb)


def _conv_bwd(h32, cw, cb, dyb):
    T = h32.shape[0]
    tm = 512 if T % 512 == 0 else T
    hb = tm // 8
    nt = T // tm

    def body(bb_ref, bc_ref, bx_ref, bch_ref, bxh_ref, bbn_ref, dy_ref, dyn_ref, w_ref, b_ref,
             dbb_ref, dbc_ref, dbx_ref, dw_ref):
        n = pl.program_id(0)

        @pl.when(n == 0)
        def _():
            dw_ref[...] = jnp.zeros_like(dw_ref)

        u, u1, u2 = _conv_u(bc_ref, bx_ref, bch_ref, bxh_ref, n, tm)
        w0, w1, w2 = w_ref[0:1, :], w_ref[1:2, :], w_ref[2:3, :]
        y = w0 * u2 + w1 * u1 + w2 * u + b_ref[...]
        dyb_ = dy_ref[...]
        dbb_ref[...] = (dyb_ * y).astype(BF)
        dy = dyb_ * bb_ref[...]
        dyn = dyn_ref[...] * bbn_ref[...] * (n < nt - 1).astype(F32)
        rowi = _iota((tm, 256), 0)
        dy1 = jnp.where(rowi == tm - 1, dyn[0:1, :], pltpu.roll(dy, tm - 1, axis=0))
        dy2 = jnp.where(rowi == tm - 2, dyn[0:1, :],
                        jnp.where(rowi == tm - 1, dyn[1:2, :], pltpu.roll(dy, tm - 2, axis=0)))
        du = w2 * dy + w1 * dy1 + w0 * dy2
        dbc_ref[...] = (du * bx_ref[...]).astype(BF)
        dbx_ref[...] = (du * bc_ref[...]).astype(BF)
        dw_ref[0:1, :] += jnp.sum(dy * u2, axis=0, keepdims=True)
        dw_ref[1:2, :] += jnp.sum(dy * u1, axis=0, keepdims=True)
        dw_ref[2:3, :] += jnp.sum(dy * u, axis=0, keepdims=True)
        dw_ref[3:4, :] += jnp.sum(dy, axis=0, keepdims=True)

    halo = lambda c: pl.BlockSpec((8, 256), lambda n: (jnp.maximum(n * hb - 1, 0), c // 256))
    nxt = lambda c: pl.BlockSpec((8, 256), lambda n: (jnp.minimum((n + 1) * hb, T // 8 - 1), c // 256))
    cur = lambda c: pl.BlockSpec((tm, 256), lambda n: (n, c // 256))
    return pl.pallas_call(
        body, name="conv_bwd", grid=(nt,),
        in_specs=[cur(C_BB), cur(C_BC), cur(C_BX), halo(C_BC), halo(C_BX), nxt(C_BB),
                  cur(0), nxt(0),
                  pl.BlockSpec((8, 256), lambda n: (0, 0)),
                  pl.BlockSpec((1, 256), lambda n: (0, 0))],
        out_specs=[cur(0), cur(0), cur(0), pl.BlockSpec((8, 256), lambda n: (0, 0))],
        out_shape=[jax.ShapeDtypeStruct((T, 256), BF)] * 3 + [jax.ShapeDtypeStruct((8, 256), F32)],
        compiler_params=_params(1))(h32, h32, h32, h32, h32, h32, dyb, dyb, cw, cb)


def _cprep_specs(tm):
    return [pl.BlockSpec((tm, 256), lambda n: (n, C_CQ // 256)),
            pl.BlockSpec((tm, 128), lambda n: (n, C_CKV // 128)),
            pl.BlockSpec((tm, 128), lambda n: (n, C_CKR // 128)),
            pl.BlockSpec((tm, 128), lambda n: (n, C_CKRS // 128)),
            pl.BlockSpec((1, 256), lambda n: (0, 0)),
            pl.BlockSpec((1, 128), lambda n: (0, 0)),
            pl.BlockSpec((tm, 128), lambda n: (n, 0)),
            pl.BlockSpec((tm, 128), lambda n: (n, 0))]


def _cprep_fwd(h32, gq, gkv, wuq2, wkv2, cosk, sin):
    T = h32.shape[0]
    tm = 512 if T % 512 == 0 else T

    def body(cq_ref, ckv_ref, ckr_ref, ckrs_ref, gq_ref, gkv_ref, cos_ref, sin_ref, wuq_ref, wkv_ref,
             q_ref, k_ref, v_ref):
        cosk_, sin_ = cos_ref[...], sin_ref[...]
        cosq = cosk_ + (_iota((tm, 128), 1) < 64).astype(F32)
        cqn, _ = _rms_fwd(cq_ref[...], gq_ref[...])
        q2 = _dot(cqn.astype(BF), wuq_ref[...])
        ckvn, _ = _rms_fwd(ckv_ref[...], gkv_ref[...])
        kv2 = _dot(ckvn.astype(BF), wkv_ref[...])
        kr = ckr_ref[...] * cosk_ + ckrs_ref[...] * sin_
        for h in range(4):
            hs = slice(h * 128, (h + 1) * 128)
            q_ref[:, hs] = ((q2[:, hs] * cosq + q2[:, 512 + h * 128:512 + (h + 1) * 128] * sin_) * MLA_SCALE).astype(BF)
            k_ref[:, hs] = (kv2[:, hs] + kr).astype(BF)
        v_ref[...] = kv2[:, 512:].astype(BF)

    return pl.pallas_call(
        body, name="cprep_fwd", grid=(T // tm,),
        in_specs=_cprep_specs(tm) + [pl.BlockSpec((256, 1024), lambda n: (0, 0)),
                                     pl.BlockSpec((128, 1024), lambda n: (0, 0))],
        out_specs=[pl.BlockSpec((tm, 512), lambda n: (n, 0))] * 3,
        out_shape=[jax.ShapeDtypeStruct((T, 512), BF)] * 3,
        compiler_params=_params(1))(h32, h32, h32, h32, gq, gkv, cosk, sin, wuq2, wkv2)


def _cprep_bwd(h32, gq, gkv, wuq2t, wkv2t, cosk, sin, dq, dk, dv):
    T = h32.shape[0]
    tm = 512 if T % 512 == 0 else T

    def body(cq_ref, ckv_ref, ckr_ref, ckrs_ref, gq_ref, gkv_ref, cos_ref, sin_ref, wuq_ref, wkv_ref,
             dq_ref, dk_ref, dv_ref,
             dcq_ref, dckv_ref, dckr_ref, dckrs_ref, dq2_ref, dkv2_ref, cqn_ref, ckvn_ref, dgq_ref, dgkv_ref):
        n = pl.program_id(0)

        @pl.when(n == 0)
        def _():
            dgq_ref[...] = jnp.zeros_like(dgq_ref)
            dgkv_ref[...] = jnp.zeros_like(dgkv_ref)

        cosk_, sin_ = cos_ref[...], sin_ref[...]
        cosq = cosk_ + (_iota((tm, 128), 1) < 64).astype(F32)
        dkr = jnp.zeros((tm, 128), F32)
        for h in range(4):
            hs = slice(h * 128, (h + 1) * 128)
            dqh = dq_ref[:, hs] * MLA_SCALE
            dq2_ref[:, hs] = (dqh * cosq).astype(BF)
            dq2_ref[:, 512 + h * 128:512 + (h + 1) * 128] = (dqh * sin_).astype(BF)
            dkr = dkr + dk_ref[:, hs]
        dkv2_ref[:, :512] = dk_ref[...].astype(BF)
        dkv2_ref[:, 512:] = dv_ref[...].astype(BF)
        dckr_ref[...] = (dkr * cosk_).astype(BF)
        dckrs_ref[...] = (dkr * sin_).astype(BF)

        cq, gq_ = cq_ref[...], gq_ref[...]
        cqn, rq = _rms_fwd(cq, gq_)
        cqn_ref[...] = cqn.astype(BF)
        dcq, dgt = _rms_bwd(cq, gq_, rq, _dot(dq2_ref[...], wuq_ref[...]))
        dcq_ref[...] = dcq.astype(BF)
        dgq_ref[...] += jnp.sum(dgt, axis=0, keepdims=True)

        ckv, gkv_ = ckv_ref[...], gkv_ref[...]
        ckvn, rkv = _rms_fwd(ckv, gkv_)
        ckvn_ref[...] = ckvn.astype(BF)
        dckv, dgt2 = _rms_bwd(ckv, gkv_, rkv, _dot(dkv2_ref[...], wkv_ref[...]))
        dckv_ref[...] = dckv.astype(BF)
        dgkv_ref[...] += jnp.sum(dgt2, axis=0, keepdims=True)

    row = lambda w: pl.BlockSpec((tm, w), lambda n: (n, 0))
    return pl.pallas_call(
        body, name="cprep_bwd", grid=(T // tm,),
        in_specs=_cprep_specs(tm) + [pl.BlockSpec((1024, 256), lambda n: (0, 0)),
                                     pl.BlockSpec((1024, 128), lambda n: (0, 0)),
                                     row(512), row(512), row(512)],
        out_specs=[row(256), row(128), row(128), row(128), row(1024), row(1024), row(256), row(128),
                   pl.BlockSpec((1, 256), lambda n: (0, 0)), pl.BlockSpec((1, 128), lambda n: (0, 0))],
        out_shape=[jax.ShapeDtypeStruct((T, 256), BF), jax.ShapeDtypeStruct((T, 128), BF),
                   jax.ShapeDtypeStruct((T, 128), BF), jax.ShapeDtypeStruct((T, 128), BF),
                   jax.ShapeDtypeStruct((T, 1024), BF), jax.ShapeDtypeStruct((T, 1024), BF),
                   jax.ShapeDtypeStruct((T, 256), BF), jax.ShapeDtypeStruct((T, 128), BF),
                   jax.ShapeDtypeStruct((1, 256), F32), jax.ShapeDtypeStruct((1, 128), F32)],
        compiler_params=_params(1))(h32, h32, h32, h32, gq, gkv, cosk, sin, wuq2t, wkv2t, dq, dk, dv)


MLA_TILE = 512


def _causal_mask(t):
    return _iota((t, t), 1) <= _iota((t, t), 0)


def _mla_fwd(q, k, v):
    T = q.shape[0]
    tq = MLA_TILE

    def body(q_ref, k_ref, v_ref, o_ref, lse_ref):
        i = pl.program_id(1)
        qv = q_ref[...]
        mask = _causal_mask(tq)

        def step(j, carry, masked):
            m, l, acc = carry
            rows = pl.ds(pl.multiple_of(j * tq, tq), tq)
            s = _dot_nt(qv, k_ref[rows, :])
            if masked:
                s = jnp.where(mask, s, NEG)
            m_new = jnp.maximum(m, jnp.max(s, axis=-1, keepdims=True))
            p = jnp.exp(s - m_new)
            alpha = jnp.exp(m - m_new)
            l = alpha * l + jnp.sum(p, axis=-1, keepdims=True)
            acc = alpha * acc + _dot(p.astype(BF), v_ref[rows, :])
            return m_new, l, acc

        init = (jnp.full((tq, 1), NEG, F32), jnp.zeros((tq, 1), F32), jnp.zeros((tq, 128), F32))
        carry = lax.fori_loop(0, i, lambda j, c: step(j, c, False), init)
        m, l, acc = step(i, carry, True)
        o_ref[...] = acc * (1.0 / l)
        lse_ref[...] = jnp.broadcast_to(m + jnp.log(l), (tq, 128))

    blk = pl.BlockSpec((tq, 128), lambda h, i: (i, h))
    full = pl.BlockSpec((T, 128), lambda h, i: (0, h))
    return pl.pallas_call(
        body, name="mla_fwd", grid=(4, T // tq), in_specs=[blk, full, full], out_specs=[blk, blk],
        out_shape=[jax.ShapeDtypeStruct((T, 512), F32), jax.ShapeDtypeStruct((T, 512), F32)],
        compiler_params=_params(2))(q, k, v)


def _mla_bwd(q, k, v, o, lse, do):
    T = q.shape[0]
    tq = MLA_TILE

    def body(q_ref, k_ref, v_ref, o_ref, lse_ref, do_ref, dq_ref, dk_ref, dv_ref):
        i = pl.program_id(1)

        @pl.when(i == 0)
        def _():
            dk_ref[...] = jnp.zeros_like(dk_ref)
            dv_ref[...] = jnp.zeros_like(dv_ref)

        qv = q_ref[...]
        do = do_ref[...]
        dob = do.astype(BF)
        dd = jnp.sum(do * o_ref[...], axis=-1, keepdims=True)
        lse_ = lse_ref[:, 0:1]
        mask = _causal_mask(tq)

        def step(j, dq, masked):
            rows = pl.ds(pl.multiple_of(j * tq, tq), tq)
            kj, vj = k_ref[rows, :], v_ref[rows, :]
            s = _dot_nt(qv, kj)
            if masked:
                s = jnp.where(mask, s, NEG)
            p = jnp.exp(s - lse_)
            ds = (p * (_dot_nt(dob, vj) - dd)).astype(BF)
            dk_ref[rows, :] += _dot_tn(ds, qv)
            dv_ref[rows, :] += _dot_tn(p.astype(BF), dob)
            return dq + _dot(ds, kj)

        dq = lax.fori_loop(0, i, lambda j, c: step(j, c, False), jnp.zeros((tq, 128), F32))
        dq_ref[...] = step(i, dq, True)

    blk = pl.BlockSpec((tq, 128), lambda h, i: (i, h))
    full = pl.BlockSpec((T, 128), lambda h, i: (0, h))
    return pl.pallas_call(
        body, name="mla_bwd", grid=(4, T // tq), in_specs=[blk, full, full, blk, blk, blk],
        out_specs=[blk, full, full],
        out_shape=[jax.ShapeDtypeStruct((T, 512), F32)] * 3,
        compiler_params=_params(2))(q, k, v, o, lse, do)


def _sb_tile(qk, rr, strict, masked, upper):
    z2 = qk * (SB_SCALE * LOG2E)
    l1 = jnp.log2(1.0 + jnp.exp2(-jnp.abs(z2)))
    lk = -jnp.maximum(z2, 0.0) - l1
    if masked:
        lk = jnp.where(strict, lk, 0.0)
    after = rr + _dot(lk.astype(BF), upper)
    ll = jnp.minimum(z2, 0.0) - l1
    a = jnp.exp2(ll + after)
    if masked:
        a = jnp.where(strict, a, 0.0)
    return ll, a, jnp.sum(lk, axis=-1, keepdims=True)


SB_TQ, SB_TK = 512, 256
SB_DEAD = -160.0


def _sb_walk(trips, two_steps, carry):
    def alive(c):
        t, cr = c
        top = jnp.maximum(jnp.max(cr[0][0]), jnp.max(cr[1][0]))
        return jnp.logical_and(t < trips, top > SB_DEAD)

    def body(c):
        t, cr = c
        return t + 1, two_steps(t, cr)

    return lax.while_loop(alive, body, (jnp.int32(0), carry))[1]


def _sb_consts(tq, tk):
    row, col = _iota((tq, tk), 0), _iota((tq, tk), 1)
    strict = [col + d * tk < row for d in range(tq // tk)]
    r2, c2 = _iota((tk, tk), 0), _iota((tk, tk), 1)
    return strict, (r2 > c2).astype(BF), (r2 < c2).astype(BF)


def _sb_fwd(h16):
    T = h16.shape[0]
    tq, tk = SB_TQ, SB_TK
    nd = tq // tk

    def body(q_ref, k_ref, v_ref, o_ref):
        i = pl.program_id(1)
        strict, upper, _ = _sb_consts(tq, tk)
        lane = _iota((tq, 128), 1)
        q2 = q_ref[...]
        qms = [jnp.where(lane < 64, q2, jnp.zeros_like(q2)), jnp.where(lane >= 64, q2, jnp.zeros_like(q2))]

        def step(j, carry, d):
            rows = pl.ds(pl.multiple_of(j * tk, tk), tk)
            kj, vj = k_ref[rows, :], v_ref[rows, :]
            out = []
            for hh in range(2):
                rr, acc = carry[hh]
                _, a, rs = _sb_tile(_dot_nt(qms[hh], kj), rr, None if d is None else strict[d], d is not None, upper)
                out.append((rr + rs, acc + _dot(a.astype(BF), vj)))
            return tuple(out)

        def two_steps(t, c):
            j = nd * i - 1 - 2 * t
            return step(j - 1, step(j, c, None), None)

        carry = ((jnp.zeros((tq, 1), F32), jnp.zeros((tq, 128), F32)),) * 2
        for d in reversed(range(nd)):
            carry = step(nd * i + d, carry, d)
        carry = _sb_walk(nd * i // 2, two_steps, carry)
        o_ref[...] = jnp.where(lane < 64, carry[0][1], carry[1][1])

    return pl.pallas_call(
        body, name="sb_fwd", grid=(2, T // tq),
        in_specs=[pl.BlockSpec((tq, 128), lambda p, i: (i, C_DQ // 128 + p)),
                  pl.BlockSpec((T, 128), lambda p, i: (0, C_DK // 128 + p)),
                  pl.BlockSpec((T, 128), lambda p, i: (0, C_DV // 128 + p))],
        out_specs=pl.BlockSpec((tq, 128), lambda p, i: (i, p)),
        out_shape=jax.ShapeDtypeStruct((T, 256), F32),
        compiler_params=_params(2))(h16, h16, h16)


def _sb_bwd(h16, yd, dyd):
    T = h16.shape[0]
    tq, tk = SB_TQ, SB_TK
    nd = tq // tk

    def body(q_ref, k_ref, v_ref, o_ref, do_ref, dq_ref, dk_ref, dv_ref):
        i = pl.program_id(1)

        @pl.when(i == 0)
        def _():
            dk_ref[...] = jnp.zeros_like(dk_ref)
            dv_ref[...] = jnp.zeros_like(dv_ref)

        strict, upper, before = _sb_consts(tq, tk)
        lane = _iota((tq, 128), 1)
        lane_k = _iota((tk, 128), 1)
        q2 = q_ref[...]
        dob2 = do_ref[...].astype(BF)
        doo = dob2.astype(F32) * o_ref[...]
        mines = [lane < 64, lane >= 64]
        qms = [jnp.where(m, q2, jnp.zeros_like(q2)) for m in mines]
        doms = [jnp.where(m, dob2, jnp.zeros_like(dob2)) for m in mines]
        dds = [jnp.sum(jnp.where(m, doo, 0.0), axis=-1, keepdims=True) for m in mines]

        def step(j, carry, d):
            rows = pl.ds(pl.multiple_of(j * tk, tk), tk)
            kj, vj = k_ref[rows, :], v_ref[rows, :]
            out, dks, dvs = [], [], []
            for hh in range(2):
                rr, sg, dq = carry[hh]
                ll, a, rs = _sb_tile(_dot_nt(qms[hh], kj), rr, None if d is None else strict[d], d is not None,
                                     upper)
                ab = a.astype(BF)
                g = _dot_nt(doms[hh], vj) * ab.astype(F32)
                gs = jnp.sum(g, axis=-1, keepdims=True)
                pre = (dds[hh] - sg - gs) + _dot(g.astype(BF), before)
                dz = g - jnp.exp2(ll) * (g + pre)
                if d is not None:
                    dz = jnp.where(strict[d], dz, 0.0)
                dzb = dz.astype(BF)
                dks.append(_dot_tn(dzb, q2))
                dvs.append(_dot_tn(ab, dob2))
                out.append((rr + rs, sg + gs, dq + _dot(dzb, kj)))
            dk_ref[rows, :] += jnp.where(lane_k < 64, dks[0], dks[1]) * SB_SCALE
            dv_ref[rows, :] += jnp.where(lane_k < 64, dvs[0], dvs[1])
            return tuple(out)

        def two_steps(t, c):
            j = nd * i - 1 - 2 * t
            return step(j - 1, step(j, c, None), None)

        zero = jnp.zeros((tq, 1), F32)
        carry = ((zero, zero, jnp.zeros((tq, 128), F32)),) * 2
        for d in reversed(range(nd)):
            carry = step(nd * i + d, carry, d)
        carry = _sb_walk(nd * i // 2, two_steps, carry)
        dq_ref[...] = jnp.where(lane < 64, carry[0][2], carry[1][2]) * SB_SCALE

    blk = lambda c: pl.BlockSpec((tq, 128), lambda p, i: (i, c // 128 + p))
    full = lambda c: pl.BlockSpec((T, 128), lambda p, i: (0, c // 128 + p))
    return pl.pallas_call(
        body, name="sb_bwd", grid=(2, T // tq),
        in_specs=[blk(C_DQ), full(C_DK), full(C_DV), blk(0), blk(0)],
        out_specs=[blk(0), full(0), full(0)],
        out_shape=[jax.ShapeDtypeStruct((T, 256), F32)] * 3,
        compiler_params=_params(2))(h16, h16, h16, yd, dyd)


def _compact_c(ycp):
    return jnp.concatenate([ycp[:, h * 128:h * 128 + 64] for h in range(4)], axis=1)


def _post_fwd(ya, yb, ycp, yd, h32, ggrp, wout, gpost, x):
    T = x.shape[0]
    tm = 256

    def body(ya_ref, yb_ref, yc_ref, yd_ref, gate_ref, gg_ref, w_ref, gp_ref, x_ref, xn_ref, ym_ref, o_ref):
        ys = [ya_ref[...], yb_ref[...], _compact_c(yc_ref[...]), yd_ref[...]]
        gate = gate_ref[...]
        sil = gate * (1.0 / (1.0 + jnp.exp(-gate)))
        parts = []
        for gi in range(4):
            ng, _ = _rms_fwd(ys[gi], gg_ref[:, gi * 256:(gi + 1) * 256])
            parts.append(ng * sil[:, gi * 256:(gi + 1) * 256])
        ym = jnp.concatenate(parts, axis=1).astype(BF)
        ym_ref[...] = ym
        o = _dot(ym, w_ref[...])
        o_ref[...] = o
        on, _ = _rms_fwd(o, gp_ref[...])
        xn_ref[...] = x_ref[...] + on

    row = lambda w: pl.BlockSpec((tm, w), lambda n: (n, 0))
    vec = pl.BlockSpec((1, 1024), lambda n: (0, 0))
    return pl.pallas_call(
        body, name="post_fwd", grid=(T // tm,),
        in_specs=[row(256), row(256), row(512), row(256), pl.BlockSpec((tm, 1024), lambda n: (n, C_GATE // 1024)),
                  vec, pl.BlockSpec((1024, 1024), lambda n: (0, 0)), vec, row(1024)],
        out_specs=[row(1024), row(1024), row(1024)],
        out_shape=[jax.ShapeDtypeStruct((T, 1024), F32), jax.ShapeDtypeStruct((T, 1024), BF),
                   jax.ShapeDtypeStruct((T, 1024), F32)],
        compiler_params=_params(1))(ya, yb, ycp, yd, h32, ggrp, wout, gpost, x)


def _post_bwd(dx, o, gpost, woutt, ya, yb, ycp, yd, h32, ggrp):
    T = dx.shape[0]
    tm = 256

    def body(dx_ref, o_ref, gp_ref, w_ref, ya_ref, yb_ref, yc_ref, yd_ref, gate_ref, gg_ref,
             do_ref, dya_ref, dyb_ref, dyc_ref, dyd_ref, dgate_ref, dgp_ref, dgg_ref):
        n = pl.program_id(0)

        @pl.when(n == 0)
        def _():
            dgp_ref[...] = jnp.zeros_like(dgp_ref)
            dgg_ref[...] = jnp.zeros_like(dgg_ref)

        ov, gp = o_ref[...], gp_ref[...]
        _, ro = _rms_fwd(ov, gp)
        do, dgt = _rms_bwd(ov, gp, ro, dx_ref[...])
        dgp_ref[...] += jnp.sum(dgt, axis=0, keepdims=True)
        dob = do.astype(BF)
        do_ref[...] = dob
        dym = _dot(dob, w_ref[...])
        gate = gate_ref[...]
        sg = 1.0 / (1.0 + jnp.exp(-gate))
        sil = gate * sg
        dsil = sg * (1.0 + gate * (1.0 - sg))
        ys = [ya_ref[...], yb_ref[...], _compact_c(yc_ref[...]), yd_ref[...]]
        dys = []
        for gi in range(4):
            gs = slice(gi * 256, (gi + 1) * 256)
            gg = gg_ref[:, gs]
            ng, rg = _rms_fwd(ys[gi], gg)
            dgate_ref[:, gs] = (dym[:, gs] * ng * dsil[:, gs]).astype(BF)
            dy, dgt2 = _rms_bwd(ys[gi], gg, rg, dym[:, gs] * sil[:, gs])
            dgg_ref[:, gs] += jnp.sum(dgt2, axis=0, keepdims=True)
            dys.append(dy)
        dya_ref[...] = dys[0]
        dyb_ref[...] = dys[1]
        dyd_ref[...] = dys[3]
        z64 = jnp.zeros((tm, 64), F32)
        dyc_ref[...] = jnp.concatenate(
            [piece for h in range(4) for piece in (dys[2][:, h * 64:(h + 1) * 64], z64)], axis=1)

    row = lambda w: pl.BlockSpec((tm, w), lambda n: (n, 0))
    vec = pl.BlockSpec((1, 1024), lambda n: (0, 0))
    return pl.pallas_call(
        body, name="post_bwd", grid=(T // tm,),
        in_specs=[row(1024), row(1024), vec, pl.BlockSpec((1024, 1024), lambda n: (0, 0)),
                  row(256), row(256), row(512), row(256),
                  pl.BlockSpec((tm, 1024), lambda n: (n, C_GATE // 1024)), vec],
        out_specs=[row(1024), row(256), row(256), row(512), row(256), row(1024), vec, vec],
        out_shape=[jax.ShapeDtypeStruct((T, 1024), BF), jax.ShapeDtypeStruct((T, 256), F32),
                   jax.ShapeDtypeStruct((T, 256), F32), jax.ShapeDtypeStruct((T, 512), F32),
                   jax.ShapeDtypeStruct((T, 256), F32), jax.ShapeDtypeStruct((T, 1024), BF),
                   jax.ShapeDtypeStruct((1, 1024), F32), jax.ShapeDtypeStruct((1, 1024), F32)],
        compiler_params=_params(1))(dx, o, gpost, woutt, ya, yb, ycp, yd, h32, ggrp)


def _loss_head(y, tgt):
    T = y.shape[0]
    tm = 512 if T % 512 == 0 else T

    def body(y_ref, t_ref, s_ref, dy_ref):
        n = pl.program_id(0)

        @pl.when(n == 0)
        def _():
            s_ref[...] = jnp.zeros_like(s_ref)

        d = y_ref[...] - t_ref[...]
        s_ref[...] += jnp.sum(d * d, axis=0, keepdims=True)
        dy_ref[...] = d * (1.0 / D_MODEL)

    row = pl.BlockSpec((tm, 1024), lambda n: (n, 0))
    return pl.pallas_call(
        body, name="loss_head", grid=(T // tm,), in_specs=[row, row],
        out_specs=[pl.BlockSpec((1, 1024), lambda n: (0, 0)), row],
        out_shape=[jax.ShapeDtypeStruct((1, 1024), F32), jax.ShapeDtypeStruct((T, 1024), F32)],
        compiler_params=_params(1))(y, tgt)


def _swap_rows32(a):
    return jnp.concatenate([a[16:32], a[0:16]], axis=0)


def _pad_w_in_t(wt):
    z = lambda n: jnp.zeros((n, wt.shape[1]), wt.dtype)
    kr = wt[1664:1696]
    return jnp.concatenate([wt[2464:3488], wt[0:1664], z(64), kr, z(32), z(64), _swap_rows32(kr), z(32),
                            wt[1696:2464], z(NP - C_END)], axis=0)


def _unpad_w_in_t(d):
    kr = d[C_CKR + 64:C_CKR + 96] + _swap_rows32(d[C_CKRS + 64:C_CKRS + 96])
    return jnp.concatenate([d[C_AQ:C_CKR], kr, d[C_DQ:C_END], d[0:1024]], axis=0)


def _pad_w_uq(w):
    z = lambda n: jnp.zeros((w.shape[0], n), w.dtype)
    a = [p for h in range(4) for p in (w[:, 96 * h:96 * h + 96], z(32))]
    b = [p for h in range(4) for p in (z(64), _swap32(w[:, 96 * h + 64:96 * h + 96]), z(32))]
    return jnp.concatenate(a + b, axis=1)


def _unpad_w_uq(d):
    out = []
    for h in range(4):
        out.append(d[:, 128 * h:128 * h + 64])
        out.append(d[:, 128 * h + 64:128 * h + 96] + _swap32(d[:, 512 + 128 * h + 64:512 + 128 * h + 96]))
    return jnp.concatenate(out, axis=1)


def _pad_w_ukv(w):
    z = jnp.zeros((w.shape[0], 64), w.dtype)
    a = [p for h in range(4) for p in (w[:, 128 * h:128 * h + 64], z)]
    b = [p for h in range(4) for p in (w[:, 128 * h + 64:128 * h + 128], z)]
    return jnp.concatenate(a + b, axis=1)


def _unpad_w_ukv(d):
    return jnp.concatenate([p for h in range(4) for p in (d[:, 128 * h:128 * h + 64],
                                                          d[:, 512 + 128 * h:512 + 128 * h + 64])], axis=1)


def _rope_tables(pos):
    freqs = 10000.0 ** (-jnp.arange(16, dtype=F32) / 16)
    ang = pos.astype(F32)[:, None] * freqs
    c, s = jnp.cos(ang), jnp.sin(ang)
    z = lambda n: jnp.zeros((pos.shape[0], n), F32)
    return (jnp.concatenate([z(64), c, c, z(32)], axis=1), jnp.concatenate([z(64), -s, s, z(32)], axis=1))


def _layer_weights(W, l):
    wuq2 = _pad_w_uq(W["mla_w_uq"][l])
    wkv2 = _pad_w_ukv(W["mla_w_ukv"][l])
    wout = W["w_out"][l]
    cw = jnp.concatenate([W["conv_w"][l].astype(F32), jnp.zeros((5, 256), F32)], axis=0)
    return dict(
        wpt=W["wpt"][l], wuq2=wuq2.astype(BF), wuq2t=wuq2.T.astype(BF),
        wkv2=wkv2.astype(BF), wkv2t=wkv2.T.astype(BF), wout=wout.astype(BF), woutt=wout.T.astype(BF),
        cw=cw, cb=W["conv_b"][l][None, :], sinks=W["attn_sinks"][l],
        gpre=W["norm_pre"][l][None, :], gq=W["mla_q_norm"][l][None, :], gkv=W["mla_kv_norm"][l][None, :],
        ggrp=W["group_norm"][l][None, :], gpost=W["norm_post"][l][None, :])


def _local_step(x, pos, W, tgt):
    cosk, sin = _rope_tables(pos)
    saved = []
    for l in range(DEPTH):
        lw = _layer_weights(W, l)
        h32, h16, xn = _inproj_fwd(x, lw["gpre"], lw["wpt"])
        ya = _swa_fwd(h16, lw["sinks"])
        yb = _conv_fwd(h32, lw["cw"], lw["cb"])
        qc, kc, vc = _cprep_fwd(h32, lw["gq"], lw["gkv"], lw["wuq2"], lw["wkv2"], cosk, sin)
        ycp, lse = _mla_fwd(qc, kc, vc)
        yd = _sb_fwd(h16)
        x_new, ym, o = _post_fwd(ya, yb, ycp, yd, h32, lw["ggrp"], lw["wout"], lw["gpost"], x)
        saved.append(dict(lw=lw, x=x, h32=h32, h16=h16, xn=xn, ya=ya, yb=yb, qc=qc, kc=kc, vc=vc, ycp=ycp,
                          lse=lse, yd=yd, ym=ym, o=o))
        x = x_new
    sq, dx = _loss_head(x, tgt)

    grads = {k: [None] * DEPTH for k in ("norm_pre", "w_in_pt", "attn_sinks", "conv_w", "conv_b", "mla_q_norm",
                                         "mla_w_uq", "mla_kv_norm", "mla_w_ukv", "group_norm", "w_out",
                                         "norm_post")}
    for l in reversed(range(DEPTH)):
        s = saved[l]
        lw = s["lw"]
        do, dya, dyb, dycp, dyd, dgate, dgpost, dggrp = _post_bwd(
            dx, s["o"], lw["gpost"], lw["woutt"], s["ya"], s["yb"], s["ycp"], s["yd"], s["h32"], lw["ggrp"])
        grads["norm_post"][l] = dgpost[0]
        grads["group_norm"][l] = dggrp[0]
        grads["w_out"][l] = _matmul_tn(s["ym"], do, 512, "dw_out")
        sdq, sdk, sdv = _sb_bwd(s["h16"], s["yd"], dyd)
        mdq, mdk, mdv = _mla_bwd(s["qc"], s["kc"], s["vc"], s["ycp"], s["lse"], dycp)
        (dcq, dckv, dckr, dckrs, dq2, dkv2, cqn, ckvn, dgq, dgkv) = _cprep_bwd(
            s["h32"], lw["gq"], lw["gkv"], lw["wuq2t"], lw["wkv2t"], cosk, sin, mdq, mdk, mdv)
        grads["mla_q_norm"][l] = dgq[0]
        grads["mla_kv_norm"][l] = dgkv[0]
        grads["mla_w_uq"][l] = _unpad_w_uq(_matmul_tn(cqn, dq2, 512, "dw_uq"))
        grads["mla_w_ukv"][l] = _unpad_w_ukv(_matmul_tn(ckvn, dkv2, 512, "dw_ukv"))
        dbb, dbc, dbx, dcw = _conv_bwd(s["h32"], lw["cw"], lw["cb"], dyb)
        grads["conv_w"][l] = dcw[0:3]
        grads["conv_b"][l] = dcw[3]
        adq, adk, adv, dsk = _swa_bwd(s["h16"], lw["sinks"], dya)
        grads["attn_sinks"][l] = dsk[0, 0:4]
        parts = [dgate, adq, adk, adv, dbb, dbc, dbx, dcq, dckv, dckr, dckrs, sdq, sdk, sdv]
        dh, dx, dgpre = _inproj_bwd_x(parts, lw["wpt"], s["x"], lw["gpre"], dx)
        grads["w_in_pt"][l] = _matmul_tn(s["xn"], dh, 1280, "dw_in").T
        grads["norm_pre"][l] = dgpre[0]
    return sq, dx, grads


SMALL_SHARDED = ("conv_w", "mla_w_uq", "mla_w_ukv")
REPLICATED = ("norm_pre", "attn_sinks", "conv_b", "mla_q_norm", "mla_kv_norm", "group_norm", "norm_post")
ORDER = ("norm_pre", "w_in", "attn_sinks", "conv_w", "conv_b", "mla_q_norm", "mla_w_uq", "mla_kv_norm",
         "mla_w_ukv", "group_norm", "w_out", "norm_post")
W_IN_COLS = 436
W_IN_WIN = 440
SMALL_ROWS = 48


def _pack_small(arrs, dtype):
    flat = jnp.concatenate([a.reshape(-1).astype(dtype) for a in arrs])
    flat = jnp.concatenate([flat, jnp.zeros((SMALL_ROWS * D_MODEL - flat.shape[0],), dtype)])
    return flat.reshape(SMALL_ROWS, D_MODEL)


def _unpack_small(buf, shapes):
    flat = buf.reshape(-1)
    out, off = [], 0
    for s in shapes:
        n = int(np.prod(s))
        out.append(flat[off:off + n].reshape(s))
        off += n
    return out


def _pack_state(ps, c):
    k = len(ps)
    wt = jnp.transpose(jnp.stack([p["w_in"] for p in ps]), (0, 1, 3, 2))
    win = lax.dynamic_update_slice(jnp.zeros((k, DEPTH, W_IN_WIN, D_MODEL), F32), wt, (0, 0, 4 * c, 0))
    wout = jnp.stack([p["w_out"] for p in ps]).reshape(k, DEPTH * 128, D_MODEL)
    flat = jnp.stack([jnp.concatenate([p[n].reshape(-1) for n in SMALL_SHARDED + REPLICATED]) for p in ps])
    small = jnp.pad(flat, ((0, 0), (0, SMALL_ROWS * D_MODEL - flat.shape[1]))).reshape(k, SMALL_ROWS, D_MODEL)
    return jnp.concatenate([win.reshape(k, DEPTH * W_IN_WIN, D_MODEL), wout, small], axis=1)


def _unpack_state(buf, p, c):
    k = buf.shape[0]
    nw = DEPTH * W_IN_WIN
    win = lax.dynamic_slice(buf[:, 0:nw].reshape(k, DEPTH, W_IN_WIN, D_MODEL), (0, 0, 4 * c, 0),
                            (k, DEPTH, W_IN_COLS, D_MODEL))
    out = {"w_in": jnp.transpose(win, (0, 1, 3, 2)),
           "w_out": buf[:, nw:nw + DEPTH * 128].reshape(k, DEPTH, 128, D_MODEL)}
    flat = buf[:, nw + DEPTH * 128:].reshape(k, SMALL_ROWS * D_MODEL)
    off = 0
    for n in SMALL_SHARDED + REPLICATED:
        size = int(np.prod(p[n].shape))
        out[n] = flat[:, off:off + size].reshape((k,) + p[n].shape)
        off += size
    return out


def _rows_of_w_in_t(lo, hi, padded, kr):
    segs = ((0, 1664, padded, C_AQ), (1664, 1696, kr, 0), (1696, 2464, padded, C_DQ), (2464, 3488, padded, C_GATE))
    out = []
    for s0, s1, src, base in segs:
        a, b = max(lo, s0), min(hi, s1)
        if a < b:
            out.append(src[base + a - s0:base + b - s0])
    return out


def _me():
    return lax.axis_index("x"), lax.axis_index("y"), lax.axis_index("c")


def _peer(k):
    x, y, c = _me()
    return (x ^ (k >> 2 & 1), y ^ (k >> 1 & 1), c ^ (k & 1))


def _all_gather(block):
    R, C = block.shape

    def body(src_ref, out_ref, send_sems, recv_sems, local_sem):
        x, y, c = _me()
        me = 4 * x + 2 * y + c
        mine = pltpu.make_async_copy(src_ref, out_ref.at[me], local_sem)
        mine.start()
        copies = [pltpu.make_async_remote_copy(src_ref=src_ref, dst_ref=out_ref.at[me], send_sem=send_sems.at[k - 1],
                                               recv_sem=recv_sems.at[k - 1], device_id=_peer(k), device_id_type=MESH)
                  for k in range(1, N_DEV)]
        for cp in copies:
            cp.start()
        for cp in copies:
            cp.wait()
        mine.wait()

    return pl.pallas_call(
        body, name="all_gather", out_shape=jax.ShapeDtypeStruct((N_DEV, R, C), block.dtype),
        in_specs=[pl.BlockSpec(memory_space=pl.ANY)], out_specs=pl.BlockSpec(memory_space=pl.ANY),
        scratch_shapes=[pltpu.SemaphoreType.DMA((N_DEV - 1,)), pltpu.SemaphoreType.DMA((N_DEV - 1,)),
                        pltpu.SemaphoreType.DMA])(block)


def _all_to_all(blocks):
    _, R, C = blocks.shape

    def body(src_ref, out_ref, send_sems, recv_sems, local_sem):
        x, y, c = _me()
        me = 4 * x + 2 * y + c
        mine = pltpu.make_async_copy(src_ref.at[me], out_ref.at[me], local_sem)
        mine.start()
        copies = []
        for k in range(1, N_DEV):
            px, py, pc = _peer(k)
            copies.append(pltpu.make_async_remote_copy(
                src_ref=src_ref.at[4 * px + 2 * py + pc], dst_ref=out_ref.at[me], send_sem=send_sems.at[k - 1],
                recv_sem=recv_sems.at[k - 1], device_id=(px, py, pc), device_id_type=MESH))
        for cp in copies:
            cp.start()
        for cp in copies:
            cp.wait()
        mine.wait()

    return pl.pallas_call(
        body, name="all_to_all", out_shape=jax.ShapeDtypeStruct((N_DEV, R, C), blocks.dtype),
        in_specs=[pl.BlockSpec(memory_space=pl.ANY)], out_specs=pl.BlockSpec(memory_space=pl.ANY),
        scratch_shapes=[pltpu.SemaphoreType.DMA((N_DEV - 1,)), pltpu.SemaphoreType.DMA((N_DEV - 1,)),
                        pltpu.SemaphoreType.DMA])(blocks)


def _adamw(parts, state):
    _, R, C = state.shape
    tr = 32
    assert R % tr == 0

    def body(p_ref, s_ref, o_ref):
        g = p_ref[0].astype(F32)
        for k in range(1, N_DEV):
            g = g + p_ref[k].astype(F32)
        o_ref[0] = g
        m_ = ADAM_B1 * s_ref[1] + (1.0 - ADAM_B1) * g
        v_ = ADAM_B2 * s_ref[2] + (1.0 - ADAM_B2) * (g * g)
        o_ref[2] = m_
        o_ref[3] = v_
        m_hat = m_ / (1.0 - ADAM_B1 ** ADAM_STEP)
        v_hat = v_ / (1.0 - ADAM_B2 ** ADAM_STEP)
        o_ref[1] = -ADAM_LR * (m_hat / (jnp.sqrt(v_hat) + ADAM_EPS) + ADAM_WD * s_ref[0])

    return pl.pallas_call(
        body, name="adamw", grid=(R // tr,),
        in_specs=[pl.BlockSpec((N_DEV, tr, C), lambda n: (0, n, 0)), pl.BlockSpec((3, tr, C), lambda n: (0, n, 0))],
        out_specs=pl.BlockSpec((4, tr, C), lambda n: (0, n, 0)), out_shape=jax.ShapeDtypeStruct((4, R, C), F32),
        compiler_params=_params(1))(parts, state)


def kernel(x, positions, norm_pre, w_in, attn_sinks, conv_w, conv_b, mla_q_norm, mla_w_uq, mla_kv_norm, mla_w_ukv, group_norm, w_out, norm_post, loss_target, m_norm_pre, m_w_in, m_attn_sinks, m_conv_w, m_conv_b, m_mla_q_norm, m_mla_w_uq, m_mla_kv_norm, m_mla_w_ukv, m_group_norm, m_w_out, m_norm_post, v_norm_pre, v_w_in, v_attn_sinks, v_conv_w, v_conv_b, v_mla_q_norm, v_mla_w_uq, v_mla_kv_norm, v_mla_w_ukv, v_group_norm, v_w_out, v_norm_post):
    local = dict(norm_pre=norm_pre, w_in=w_in, attn_sinks=attn_sinks, conv_w=conv_w, conv_b=conv_b,
                 mla_q_norm=mla_q_norm, mla_w_uq=mla_w_uq, mla_kv_norm=mla_kv_norm, mla_w_ukv=mla_w_ukv,
                 group_norm=group_norm, w_out=w_out, norm_post=norm_post)
    mom = dict(norm_pre=m_norm_pre, w_in=m_w_in, attn_sinks=m_attn_sinks, conv_w=m_conv_w, conv_b=m_conv_b,
               mla_q_norm=m_mla_q_norm, mla_w_uq=m_mla_w_uq, mla_kv_norm=m_mla_kv_norm, mla_w_ukv=m_mla_w_ukv,
               group_norm=m_group_norm, w_out=m_w_out, norm_post=m_norm_post)
    vel = dict(norm_pre=v_norm_pre, w_in=v_w_in, attn_sinks=v_attn_sinks, conv_w=v_conv_w, conv_b=v_conv_b,
               mla_q_norm=v_mla_q_norm, mla_w_uq=v_mla_w_uq, mla_kv_norm=v_mla_kv_norm, mla_w_ukv=v_mla_w_ukv,
               group_norm=v_group_norm, w_out=v_w_out, norm_post=v_norm_post)

    c = lax.axis_index("c")

    wt = jnp.transpose(w_in, (0, 2, 1)).astype(BF)
    wt = jnp.concatenate([wt, jnp.zeros((DEPTH, 448 - W_IN_COLS, D_MODEL), BF)], axis=1)
    payload = jnp.concatenate([wt.reshape(DEPTH * 448, D_MODEL), w_out.astype(BF).reshape(DEPTH * 128, D_MODEL),
                               _pack_small([local[n] for n in SMALL_SHARDED], BF)], axis=0)
    gathered = _all_gather(payload)
    W = {n: local[n] for n in REPLICATED}

    def nat_rows(l, lo, hi):
        out, r = [], lo
        while r < hi:
            d = r // W_IN_COLS
            e = min(hi, (d + 1) * W_IN_COLS)
            out.append(gathered[d, 448 * l + r - W_IN_COLS * d:448 * l + e - W_IN_COLS * d])
            r = e
        return out

    z = lambda n: [jnp.zeros((n, D_MODEL), BF)]
    W["wpt"] = [jnp.concatenate(nat_rows(l, 2464, 3488) + nat_rows(l, 0, 1664) + z(64) + nat_rows(l, 1664, 1696)
                                + z(96) + nat_rows(l, 1680, 1696) + nat_rows(l, 1664, 1680) + z(32)
                                + nat_rows(l, 1696, 2464) + z(NP - C_END), axis=0) for l in range(DEPTH)]
    wo0 = DEPTH * 448
    W["w_out"] = gathered[:, wo0:wo0 + DEPTH * 128].reshape(N_DEV, DEPTH, 128, D_MODEL).transpose(1, 0, 2, 3).reshape(
        DEPTH, D_MODEL, D_MODEL)
    flat = gathered[:, wo0 + DEPTH * 128:].reshape(N_DEV, SMALL_ROWS * D_MODEL)
    off = 0
    for n in SMALL_SHARDED:
        depth, rows, width = local[n].shape
        size = depth * rows * width
        W[n] = flat[:, off:off + size].reshape(N_DEV, depth, rows, width).transpose(1, 2, 0, 3).reshape(
            depth, rows, N_DEV * width)
        off += size

    sq, grad_x, g = _local_step(x[0], positions[0], W, loss_target[0])
    loss = lax.psum(0.5 / D_MODEL * jnp.sum(sq), ("x", "y", "c"))

    cols = []
    for n in SMALL_SHARDED:
        depth, rows, width = local[n].shape
        cols.append(jnp.stack(g[n]).reshape(depth, rows, N_DEV, width).transpose(2, 0, 1, 3).reshape(N_DEV, -1))
    rep = jnp.concatenate([a.reshape(-1) for n in REPLICATED for a in g[n]])
    cols.append(jnp.broadcast_to(rep[None], (N_DEV, rep.shape[0])))
    small = jnp.concatenate(cols, axis=1)
    small = jnp.pad(small, ((0, 0), (0, SMALL_ROWS * D_MODEL - small.shape[1]))).reshape(N_DEV, SMALL_ROWS, D_MODEL)
    krs = [p[C_CKR + 64:C_CKR + 96] + _swap_rows32(p[C_CKRS + 64:C_CKRS + 96]) for p in g["w_in_pt"]]
    pieces = []
    for d in range(N_DEV):
        lo = W_IN_COLS * d // 8 * 8
        for l in range(DEPTH):
            pieces += _rows_of_w_in_t(lo, lo + W_IN_WIN, g["w_in_pt"][l], krs[l])
        pieces += [g["w_out"][l][128 * d:128 * (d + 1)] for l in range(DEPTH)]
        pieces.append(small[d])
    blocks = jnp.concatenate(pieces, axis=0).astype(BF).reshape(N_DEV, -1, D_MODEL)
    received = _all_to_all(blocks)

    out = _unpack_state(_adamw(received, _pack_state([local, mom, vel], c)), local, c)
    return (loss, grad_x[None], *[out[n][t] for t in range(4) for n in ORDER])
```

```python
import functools

import jax
import jax.numpy as jnp
import numpy as np
from jax import lax
from jax.experimental import pallas as pl
from jax.experimental.pallas import tpu as pltpu

F32 = jnp.float32
BF = jnp.bfloat16
MESH = pl.DeviceIdType.MESH

D_MODEL = 1024
DEPTH = 2
EPS = 1e-6
N_DEV = 8
VMEM_LIMIT = 56 * 1024 * 1024
NEG = -1e30
MLA_SCALE = 96.0 ** -0.5
SB_SCALE = 0.125
LOG2E = 1.4426950408889634

NP = 3840
C_GATE = 0
C_AQ = 1024
C_AK = 1280
C_AV = 1408
C_BB = 1536
C_BC = 1792
C_BX = 2048
C_CQ = 2304
C_CKV = 2560
C_CKR = 2688
C_CKRS = 2816
C_DQ = 2944
C_DK = 3200
C_DV = 3456
C_END = 3712

def _swap32(a):
    return jnp.concatenate([a[:, 16:32], a[:, 0:16]], axis=1)

ADAM_LR, ADAM_B1, ADAM_B2, ADAM_EPS, ADAM_WD, ADAM_STEP = 0.001, 0.9, 0.999, 1e-08, 0.01, 10


def _dot(a, b):
    return jnp.dot(a, b, preferred_element_type=F32)


def _dot_nt(a, b):
    return lax.dot_general(a, b, (((1,), (1,)), ((), ())), preferred_element_type=F32)


def _dot_tn(a, b):
    return lax.dot_general(a, b, (((0,), (0,)), ((), ())), preferred_element_type=F32)


def _params(n_grid):
    return pltpu.CompilerParams(dimension_semantics=("arbitrary",) * n_grid, vmem_limit_bytes=VMEM_LIMIT)


def _rms_fwd(x, g):
    r = lax.rsqrt(jnp.mean(x * x, axis=-1, keepdims=True) + EPS)
    return (x * r) * g, r


def _rms_bwd(x, g, r, dy, width=None):
    n = x.shape[-1] if width is None else width
    u = dy * g
    dx = r * u - x * (r * r * r) * (jnp.sum(x * u, axis=-1, keepdims=True) / n)
    return dx, dy * (x * r)


def _iota(shape, axis):
    return lax.broadcasted_iota(jnp.int32, shape, axis)


def _inproj_fwd(x, g, wpt):
    T = x.shape[0]
    tm = 256

    def body(x_ref, g_ref, w_ref, h32_ref, h16_ref, xn_ref):
        xn, _ = _rms_fwd(x_ref[...], g_ref[...])
        xn = xn.astype(BF)
        xn_ref[...] = xn
        h = _dot_nt(xn, w_ref[...])
        h32_ref[...] = h
        h16_ref[...] = h.astype(BF)

    return pl.pallas_call(
        body, name="inproj_fwd", grid=(T // tm,),
        in_specs=[pl.BlockSpec((tm, D_MODEL), lambda n: (n, 0)),
                  pl.BlockSpec((1, D_MODEL), lambda n: (0, 0)),
                  pl.BlockSpec((NP, D_MODEL), lambda n: (0, 0))],
        out_specs=[pl.BlockSpec((tm, NP), lambda n: (n, 0)),
                   pl.BlockSpec((tm, NP), lambda n: (n, 0)),
                   pl.BlockSpec((tm, D_MODEL), lambda n: (n, 0))],
        out_shape=[jax.ShapeDtypeStruct((T, NP), F32), jax.ShapeDtypeStruct((T, NP), BF),
                   jax.ShapeDtypeStruct((T, D_MODEL), BF)],
        compiler_params=_params(1))(x, g, wpt)


def _inproj_bwd_x(parts, wpt, x, g, dxo):
    T = x.shape[0]
    tm = 256
    np_ = len(parts)
    assert sum(p.shape[1] for p in parts) == C_END

    def body(*refs):
        part_refs = refs[:np_]
        w_ref, x_ref, g_ref, dxo_ref, dh_ref, dx_ref, dg_ref = refs[np_:]
        n = pl.program_id(0)
        dh = jnp.concatenate([r[...].astype(BF) for r in part_refs] + [jnp.zeros((tm, NP - C_END), BF)], axis=1)
        dh_ref[...] = dh
        dxn = _dot(dh, w_ref[...])
        xv = x_ref[...]
        _, r = _rms_fwd(xv, g_ref[...])
        dx, dgt = _rms_bwd(xv, g_ref[...], r, dxn)
        dx_ref[...] = dxo_ref[...] + dx

        @pl.when(n == 0)
        def _():
            dg_ref[...] = jnp.zeros_like(dg_ref)

        dg_ref[...] += jnp.sum(dgt, axis=0, keepdims=True)

    return pl.pallas_call(
        body, name="inproj_bwd_x", grid=(T // tm,),
        in_specs=[pl.BlockSpec((tm, p.shape[1]), lambda n: (n, 0)) for p in parts]
        + [pl.BlockSpec((NP, D_MODEL), lambda n: (0, 0)),
           pl.BlockSpec((tm, D_MODEL), lambda n: (n, 0)),
           pl.BlockSpec((1, D_MODEL), lambda n: (0, 0)),
           pl.BlockSpec((tm, D_MODEL), lambda n: (n, 0))],
        out_specs=[pl.BlockSpec((tm, NP), lambda n: (n, 0)),
                   pl.BlockSpec((tm, D_MODEL), lambda n: (n, 0)),
                   pl.BlockSpec((1, D_MODEL), lambda n: (0, 0))],
        out_shape=[jax.ShapeDtypeStruct((T, NP), BF), jax.ShapeDtypeStruct((T, D_MODEL), F32),
                   jax.ShapeDtypeStruct((1, D_MODEL), F32)],
        compiler_params=_params(1))(*parts, wpt, x, g, dxo)


def _matmul_tn(a, b, tn, name):
    T, M = a.shape
    N = b.shape[1]
    tk = 512 if T % 512 == 0 else T

    def body(a_ref, b_ref, o_ref):
        k = pl.program_id(1)

        @pl.when(k == 0)
        def _():
            o_ref[...] = jnp.zeros_like(o_ref)

        o_ref[...] += _dot_tn(a_ref[...], b_ref[...])

    return pl.pallas_call(
        body, name=name, grid=(N // tn, T // tk),
        in_specs=[pl.BlockSpec((tk, M), lambda j, k: (k, 0)),
                  pl.BlockSpec((tk, tn), lambda j, k: (k, j))],
        out_specs=pl.BlockSpec((M, tn), lambda j, k: (0, j)),
        out_shape=jax.ShapeDtypeStruct((M, N), F32),
        compiler_params=_params(2))(a, b)


SWA_BLK = 128
SWA_TQ = 512


def _bdot_nt(a, b):
    return lax.dot_general(a, b, (((2,), (2,)), ((0,), (0,))), preferred_element_type=F32)


def _bdot(a, b):
    return lax.dot_general(a, b, (((2,), (1,)), ((0,), (0,))), preferred_element_type=F32)


def _bdot_tn(a, b):
    return lax.dot_general(a, b, (((1,), (1,)), ((0,), (0,))), preferred_element_type=F32)


def _swa_probs(q, kc, kp, sink, mask_c, mask_p):
    sc = jnp.where(mask_c, _bdot_nt(q, kc) * SB_SCALE, NEG)
    sp = jnp.where(mask_p, _bdot_nt(q, kp) * SB_SCALE, NEG)
    m = jnp.maximum(jnp.maximum(jnp.max(sc, axis=-1, keepdims=True), jnp.max(sp, axis=-1, keepdims=True)), sink)
    pc = jnp.exp(sc - m)
    pp = jnp.exp(sp - m)
    ps = jnp.exp(sink - m)
    inv = 1.0 / (jnp.sum(pc, axis=-1, keepdims=True) + jnp.sum(pp, axis=-1, keepdims=True) + ps)
    return pc * inv, pp * inv, ps * inv


def _swa_masks(n, nb):
    blk = _iota((nb, SWA_BLK, SWA_BLK), 0)
    row = _iota((nb, SWA_BLK, SWA_BLK), 1)
    col = _iota((nb, SWA_BLK, SWA_BLK), 2)
    return col <= row, jnp.logical_and(col > row, jnp.logical_or(blk > 0, n > 0))


def _swa_specs(tq):
    halo = tq // SWA_BLK
    return [pl.BlockSpec(memory_space=pltpu.SMEM),
            pl.BlockSpec((tq, 256), lambda n: (n, C_AQ // 256)),
            pl.BlockSpec((tq, 128), lambda n: (n, C_AK // 128)),
            pl.BlockSpec((SWA_BLK, 128), lambda n: (jnp.maximum(n * halo - 1, 0), C_AK // 128)),
            pl.BlockSpec((tq, 128), lambda n: (n, C_AV // 128)),
            pl.BlockSpec((SWA_BLK, 128), lambda n: (jnp.maximum(n * halo - 1, 0), C_AV // 128))]


def _swa_blocked(cur_ref, prev_ref, gs, nb):
    cur = cur_ref[:, gs].reshape(nb, SWA_BLK, 64)
    prev = jnp.concatenate([prev_ref[:, gs].reshape(1, SWA_BLK, 64), cur[:nb - 1]], axis=0) if nb > 1 \
        else prev_ref[:, gs].reshape(1, SWA_BLK, 64)
    return cur, prev


def _swa_fwd(h16, sinks):
    T = h16.shape[0]
    tq = SWA_TQ if T % SWA_TQ == 0 else SWA_BLK
    nb = tq // SWA_BLK

    def body(s_ref, q_ref, kc_ref, kp_ref, vc_ref, vp_ref, o_ref):
        n = pl.program_id(0)
        mask_c, mask_p = _swa_masks(n, nb)
        for h in range(4):
            hs = slice(h * 64, (h + 1) * 64)
            gs = slice(h // 2 * 64, (h // 2 + 1) * 64)
            kc, kp = _swa_blocked(kc_ref, kp_ref, gs, nb)
            vc, vp = _swa_blocked(vc_ref, vp_ref, gs, nb)
            pc, pp, _ = _swa_probs(q_ref[:, hs].reshape(nb, SWA_BLK, 64), kc, kp, s_ref[h], mask_c, mask_p)
            o_ref[:, hs] = (_bdot(pc.astype(BF), vc) + _bdot(pp.astype(BF), vp)).reshape(tq, 64)

    return pl.pallas_call(
        body, name="swa_fwd", grid=(T // tq,), in_specs=_swa_specs(tq),
        out_specs=pl.BlockSpec((tq, 256), lambda n: (n, 0)),
        out_shape=jax.ShapeDtypeStruct((T, 256), F32),
        compiler_params=_params(1))(sinks, h16, h16, h16, h16, h16)


def _swa_bwd(h16, sinks, dya):
    T = h16.shape[0]
    tq = SWA_TQ if T % SWA_TQ == 0 else SWA_BLK
    nb = tq // SWA_BLK

    def body(s_ref, q_ref, kc_ref, kp_ref, vc_ref, vp_ref, do_ref, dq_ref, dk_ref, dv_ref, ds_ref):
        n = pl.program_id(0)

        @pl.when(n == 0)
        def _():
            dk_ref[...] = jnp.zeros_like(dk_ref)
            dv_ref[...] = jnp.zeros_like(dv_ref)
            ds_ref[...] = jnp.zeros_like(ds_ref)

        mask_c, mask_p = _swa_masks(n, nb)
        rows = pl.ds(pl.multiple_of(n * tq, tq), tq)
        before = pl.ds(pl.multiple_of(jnp.maximum(n * nb - 1, 0) * SWA_BLK, SWA_BLK), SWA_BLK)
        lane = _iota((8, 128), 1)
        row8 = _iota((8, 128), 0)

        def to_keys(own, prev):
            if nb == 1:
                return own
            return own + jnp.concatenate([prev[1:], jnp.zeros((1, SWA_BLK, 64), F32)], axis=0)

        for h in range(4):
            hs = slice(h * 64, (h + 1) * 64)
            gs = slice(h // 2 * 64, (h // 2 + 1) * 64)
            q = q_ref[:, hs].reshape(nb, SWA_BLK, 64)
            kc, kp = _swa_blocked(kc_ref, kp_ref, gs, nb)
            vc, vp = _swa_blocked(vc_ref, vp_ref, gs, nb)
            pc, pp, ps = _swa_probs(q, kc, kp, s_ref[h], mask_c, mask_p)
            pcb, ppb = pc.astype(BF), pp.astype(BF)
            do = do_ref[:, hs].reshape(nb, SWA_BLK, 64)
            dob = do.astype(BF)
            o = _bdot(pcb, vc) + _bdot(ppb, vp)
            dd = jnp.sum(do * o, axis=-1, keepdims=True)
            dsc = (pc * (_bdot_nt(dob, vc) - dd) * SB_SCALE).astype(BF)
            dsp = (pp * (_bdot_nt(dob, vp) - dd) * SB_SCALE).astype(BF)
            dq_ref[:, hs] = (_bdot(dsc, kc) + _bdot(dsp, kp)).reshape(tq, 64).astype(BF)
            dkp, dvp = _bdot_tn(dsp, q), _bdot_tn(ppb, dob)
            dk_ref[rows, gs] += to_keys(_bdot_tn(dsc, q), dkp).reshape(tq, 64)
            dv_ref[rows, gs] += to_keys(_bdot_tn(pcb, dob), dvp).reshape(tq, 64)
            dk_ref[before, gs] += dkp[0]
            dv_ref[before, gs] += dvp[0]
            ds_ref[...] += jnp.where(jnp.logical_and(lane == h, row8 == 0), -jnp.sum(ps * dd), 0.0)

    return pl.pallas_call(
        body, name="swa_bwd", grid=(T // tq,),
        in_specs=_swa_specs(tq) + [pl.BlockSpec((tq, 256), lambda n: (n, 0))],
        out_specs=[pl.BlockSpec((tq, 256), lambda n: (n, 0)),
                   pl.BlockSpec((T, 128), lambda n: (0, 0)),
                   pl.BlockSpec((T, 128), lambda n: (0, 0)),
                   pl.BlockSpec((8, 128), lambda n: (0, 0))],
        out_shape=[jax.ShapeDtypeStruct((T, 256), BF), jax.ShapeDtypeStruct((T, 128), F32),
                   jax.ShapeDtypeStruct((T, 128), F32), jax.ShapeDtypeStruct((8, 128), F32)],
        compiler_params=_params(1))(sinks, h16, h16, h16, h16, h16, dya)


def _conv_u(bc_ref, bx_ref, bch_ref, bxh_ref, n, tm):
    u = bc_ref[...] * bx_ref[...]
    uh = bch_ref[...] * bxh_ref[...] * (n > 0).astype(F32)
    rowi = _iota((tm, 256), 0)
    u1 = jnp.where(rowi == 0, uh[7:8, :], pltpu.roll(u, 1, axis=0))
    u2 = jnp.where(rowi == 0, uh[6:7, :], jnp.where(rowi == 1, uh[7:8, :], pltpu.roll(u, 2, axis=0)))
    return u, u1, u2


def _conv_fwd(h32, cw, cb):
    T = h32.shape[0]
    tm = 512 if T % 512 == 0 else T
    hb = tm // 8

    def body(bb_ref, bc_ref, bx_ref, bch_ref, bxh_ref, w_ref, b_ref, o_ref):
        n = pl.program_id(0)
        u, u1, u2 = _conv_u(bc_ref, bx_ref, bch_ref, bxh_ref, n, tm)
        y = w_ref[0:1, :] * u2 + w_ref[1:2, :] * u1 + w_ref[2:3, :] * u + b_ref[...]
        o_ref[...] = bb_ref[...] * y

    halo = lambda c: pl.BlockSpec((8, 256), lambda n: (jnp.maximum(n * hb - 1, 0), c // 256))
    return pl.pallas_call(
        body, name="conv_fwd", grid=(T // tm,),
        in_specs=[pl.BlockSpec((tm, 256), lambda n: (n, C_BB // 256)),
                  pl.BlockSpec((tm, 256), lambda n: (n, C_BC // 256)),
                  pl.BlockSpec((tm, 256), lambda n: (n, C_BX // 256)),
                  halo(C_BC), halo(C_BX),
                  pl.BlockSpec((8, 256), lambda n: (0, 0)),
                  pl.BlockSpec((1, 256), lambda n: (0, 0))],
        out_specs=pl.BlockSpec((tm, 256), lambda n: (n, 0)),
        out_shape=jax.ShapeDtypeStruct((T, 256), F32),
        compiler_params=_params(1))(h32, h32, h32, h32, h32, cw, cb)


def _conv_bwd(h32, cw, cb, dyb):
    T = h32.shape[0]
    tm = 512 if T % 512 == 0 else T
    hb = tm // 8
    nt = T // tm

    def body(bb_ref, bc_ref, bx_ref, bch_ref, bxh_ref, bbn_ref, dy_ref, dyn_ref, w_ref, b_ref,
             dbb_ref, dbc_ref, dbx_ref, dw_ref):
        n = pl.program_id(0)

        @pl.when(n == 0)
        def _():
            dw_ref[...] = jnp.zeros_like(dw_ref)

        u, u1, u2 = _conv_u(bc_ref, bx_ref, bch_ref, bxh_ref, n, tm)
        w0, w1, w2 = w_ref[0:1, :], w_ref[1:2, :], w_ref[2:3, :]
        y = w0 * u2 + w1 * u1 + w2 * u + b_ref[...]
        dyb_ = dy_ref[...]
        dbb_ref[...] = (dyb_ * y).astype(BF)
        dy = dyb_ * bb_ref[...]
        dyn = dyn_ref[...] * bbn_ref[...] * (n < nt - 1).astype(F32)
        rowi = _iota((tm, 256), 0)
        dy1 = jnp.where(rowi == tm - 1, dyn[0:1, :], pltpu.roll(dy, tm - 1, axis=0))
        dy2 = jnp.where(rowi == tm - 2, dyn[0:1, :],
                        jnp.where(rowi == tm - 1, dyn[1:2, :], pltpu.roll(dy, tm - 2, axis=0)))
        du = w2 * dy + w1 * dy1 + w0 * dy2
        dbc_ref[...] = (du * bx_ref[...]).astype(BF)
        dbx_ref[...] = (du * bc_ref[...]).astype(BF)
        dw_ref[0:1, :] += jnp.sum(dy * u2, axis=0, keepdims=True)
        dw_ref[1:2, :] += jnp.sum(dy * u1, axis=0, keepdims=True)
        dw_ref[2:3, :] += jnp.sum(dy * u, axis=0, keepdims=True)
        dw_ref[3:4, :] += jnp.sum(dy, axis=0, keepdims=True)

    halo = lambda c: pl.BlockSpec((8, 256), lambda n: (jnp.maximum(n * hb - 1, 0), c // 256))
    nxt = lambda c: pl.BlockSpec((8, 256), lambda n: (jnp.minimum((n + 1) * hb, T // 8 - 1), c // 256))
    cur = lambda c: pl.BlockSpec((tm, 256), lambda n: (n, c // 256))
    return pl.pallas_call(
        body, name="conv_bwd", grid=(nt,),
        in_specs=[cur(C_BB), cur(C_BC), cur(C_BX), halo(C_BC), halo(C_BX), nxt(C_BB),
                  cur(0), nxt(0),
                  pl.BlockSpec((8, 256), lambda n: (0, 0)),
                  pl.BlockSpec((1, 256), lambda n: (0, 0))],
        out_specs=[cur(0), cur(0), cur(0), pl.BlockSpec((8, 256), lambda n: (0, 0))],
        out_shape=[jax.ShapeDtypeStruct((T, 256), BF)] * 3 + [jax.ShapeDtypeStruct((8, 256), F32)],
        compiler_params=_params(1))(h32, h32, h32, h32, h32, h32, dyb, dyb, cw, cb)


def _cprep_specs(tm):
    return [pl.BlockSpec((tm, 256), lambda n: (n, C_CQ // 256)),
            pl.BlockSpec((tm, 128), lambda n: (n, C_CKV // 128)),
            pl.BlockSpec((tm, 128), lambda n: (n, C_CKR // 128)),
            pl.BlockSpec((tm, 128), lambda n: (n, C_CKRS // 128)),
            pl.BlockSpec((1, 256), lambda n: (0, 0)),
            pl.BlockSpec((1, 128), lambda n: (0, 0)),
            pl.BlockSpec((tm, 128), lambda n: (n, 0)),
            pl.BlockSpec((tm, 128), lambda n: (n, 0))]


def _cprep_fwd(h32, gq, gkv, wuq2, wkv2, cosk, sin):
    T = h32.shape[0]
    tm = 512 if T % 512 == 0 else T

    def body(cq_ref, ckv_ref, ckr_ref, ckrs_ref, gq_ref, gkv_ref, cos_ref, sin_ref, wuq_ref, wkv_ref,
             q_ref, k_ref, v_ref):
        cosk_, sin_ = cos_ref[...], sin_ref[...]
        cosq = cosk_ + (_iota((tm, 128), 1) < 64).astype(F32)
        cqn, _ = _rms_fwd(cq_ref[...], gq_ref[...])
        q2 = _dot(cqn.astype(BF), wuq_ref[...])
        ckvn, _ = _rms_fwd(ckv_ref[...], gkv_ref[...])
        kv2 = _dot(ckvn.astype(BF), wkv_ref[...])
        kr = ckr_ref[...] * cosk_ + ckrs_ref[...] * sin_
        for h in range(4):
            hs = slice(h * 128, (h + 1) * 128)
            q_ref[:, hs] = ((q2[:, hs] * cosq + q2[:, 512 + h * 128:512 + (h + 1) * 128] * sin_) * MLA_SCALE).astype(BF)
            k_ref[:, hs] = (kv2[:, hs] + kr).astype(BF)
        v_ref[...] = kv2[:, 512:].astype(BF)

    return pl.pallas_call(
        body, name="cprep_fwd", grid=(T // tm,),
        in_specs=_cprep_specs(tm) + [pl.BlockSpec((256, 1024), lambda n: (0, 0)),
                                     pl.BlockSpec((128, 1024), lambda n: (0, 0))],
        out_specs=[pl.BlockSpec((tm, 512), lambda n: (n, 0))] * 3,
        out_shape=[jax.ShapeDtypeStruct((T, 512), BF)] * 3,
        compiler_params=_params(1))(h32, h32, h32, h32, gq, gkv, cosk, sin, wuq2, wkv2)


def _cprep_bwd(h32, gq, gkv, wuq2t, wkv2t, cosk, sin, dq, dk, dv):
    T = h32.shape[0]
    tm = 512 if T % 512 == 0 else T

    def body(cq_ref, ckv_ref, ckr_ref, ckrs_ref, gq_ref, gkv_ref, cos_ref, sin_ref, wuq_ref, wkv_ref,
             dq_ref, dk_ref, dv_ref,
             dcq_ref, dckv_ref, dckr_ref, dckrs_ref, dq2_ref, dkv2_ref, cqn_ref, ckvn_ref, dgq_ref, dgkv_ref):
        n = pl.program_id(0)

        @pl.when(n == 0)
        def _():
            dgq_ref[...] = jnp.zeros_like(dgq_ref)
            dgkv_ref[...] = jnp.zeros_like(dgkv_ref)

        cosk_, sin_ = cos_ref[...], sin_ref[...]
        cosq = cosk_ + (_iota((tm, 128), 1) < 64).astype(F32)
        dkr = jnp.zeros((tm, 128), F32)
        for h in range(4):
            hs = slice(h * 128, (h + 1) * 128)
            dqh = dq_ref[:, hs] * MLA_SCALE
            dq2_ref[:, hs] = (dqh * cosq).astype(BF)
            dq2_ref[:, 512 + h * 128:512 + (h + 1) * 128] = (dqh * sin_).astype(BF)
            dkr = dkr + dk_ref[:, hs]
        dkv2_ref[:, :512] = dk_ref[...].astype(BF)
        dkv2_ref[:, 512:] = dv_ref[...].astype(BF)
        dckr_ref[...] = (dkr * cosk_).astype(BF)
        dckrs_ref[...] = (dkr * sin_).astype(BF)

        cq, gq_ = cq_ref[...], gq_ref[...]
        cqn, rq = _rms_fwd(cq, gq_)
        cqn_ref[...] = cqn.astype(BF)
        dcq, dgt = _rms_bwd(cq, gq_, rq, _dot(dq2_ref[...], wuq_ref[...]))
        dcq_ref[...] = dcq.astype(BF)
        dgq_ref[...] += jnp.sum(dgt, axis=0, keepdims=True)

        ckv, gkv_ = ckv_ref[...], gkv_ref[...]
        ckvn, rkv = _rms_fwd(ckv, gkv_)
        ckvn_ref[...] = ckvn.astype(BF)
        dckv, dgt2 = _rms_bwd(ckv, gkv_, rkv, _dot(dkv2_ref[...], wkv_ref[...]))
        dckv_ref[...] = dckv.astype(BF)
        dgkv_ref[...] += jnp.sum(dgt2, axis=0, keepdims=True)

    row = lambda w: pl.BlockSpec((tm, w), lambda n: (n, 0))
    return pl.pallas_call(
        body, name="cprep_bwd", grid=(T // tm,),
        in_specs=_cprep_specs(tm) + [pl.BlockSpec((1024, 256), lambda n: (0, 0)),
                                     pl.BlockSpec((1024, 128), lambda n: (0, 0)),
                                     row(512), row(512), row(512)],
        out_specs=[row(256), row(128), row(128), row(128), row(1024), row(1024), row(256), row(128),
                   pl.BlockSpec((1, 256), lambda n: (0, 0)), pl.BlockSpec((1, 128), lambda n: (0, 0))],
        out_shape=[jax.ShapeDtypeStruct((T, 256), BF), jax.ShapeDtypeStruct((T, 128), BF),
                   jax.ShapeDtypeStruct((T, 128), BF), jax.ShapeDtypeStruct((T, 128), BF),
                   jax.ShapeDtypeStruct((T, 1024), BF), jax.ShapeDtypeStruct((T, 1024), BF),
                   jax.ShapeDtypeStruct((T, 256), BF), jax.ShapeDtypeStruct((T, 128), BF),
                   jax.ShapeDtypeStruct((1, 256), F32), jax.ShapeDtypeStruct((1, 128), F32)],
        compiler_params=_params(1))(h32, h32, h32, h32, gq, gkv, cosk, sin, wuq2t, wkv2t, dq, dk, dv)


MLA_TILE = 512


def _causal_mask(t):
    return _iota((t, t), 1) <= _iota((t, t), 0)


def _mla_fwd(q, k, v):
    T = q.shape[0]
    tq = MLA_TILE

    def body(q_ref, k_ref, v_ref, o_ref, lse_ref):
        i = pl.program_id(1)
        mask = _causal_mask(tq)
        heads = [slice(0, 128), slice(128, 256)]
        qs = [q_ref[:, hs] for hs in heads]

        def step(j, carry, masked):
            rows = pl.ds(pl.multiple_of(j * tq, tq), tq)
            out = []
            for hh, hs in enumerate(heads):
                m, l, acc = carry[hh]
                s = _dot_nt(qs[hh], k_ref[rows, hs])
                if masked:
                    s = jnp.where(mask, s, NEG)
                m_new = jnp.maximum(m, jnp.max(s, axis=-1, keepdims=True))
                p = jnp.exp(s - m_new)
                alpha = jnp.exp(m - m_new)
                l = alpha * l + jnp.sum(p, axis=-1, keepdims=True)
                acc = alpha * acc + _dot(p.astype(BF), v_ref[rows, hs])
                out.append((m_new, l, acc))
            return tuple(out)

        init = ((jnp.full((tq, 1), NEG, F32), jnp.zeros((tq, 1), F32), jnp.zeros((tq, 128), F32)),) * 2
        carry = lax.fori_loop(0, i, lambda j, c: step(j, c, False), init)
        carry = step(i, carry, True)
        for hh, hs in enumerate(heads):
            m, l, acc = carry[hh]
            o_ref[:, hs] = acc * (1.0 / l)
            lse_ref[:, hs] = jnp.broadcast_to(m + jnp.log(l), (tq, 128))

    blk = pl.BlockSpec((tq, 256), lambda h, i: (i, h))
    full = pl.BlockSpec((T, 256), lambda h, i: (0, h))
    return pl.pallas_call(
        body, name="mla_fwd", grid=(2, T // tq), in_specs=[blk, full, full], out_specs=[blk, blk],
        out_shape=[jax.ShapeDtypeStruct((T, 512), F32), jax.ShapeDtypeStruct((T, 512), F32)],
        compiler_params=_params(2))(q, k, v)


def _mla_bwd(q, k, v, o, lse, do):
    T = q.shape[0]
    tq = MLA_TILE

    def body(q_ref, k_ref, v_ref, o_ref, lse_ref, do_ref, dq_ref, dk_ref, dv_ref):
        i = pl.program_id(1)

        @pl.when(i == 0)
        def _():
            dk_ref[...] = jnp.zeros_like(dk_ref)
            dv_ref[...] = jnp.zeros_like(dv_ref)

        qv = q_ref[...]
        do = do_ref[...]
        dob = do.astype(BF)
        dd = jnp.sum(do * o_ref[...], axis=-1, keepdims=True)
        lse_ = lse_ref[:, 0:1]
        mask = _causal_mask(tq)

        def step(j, dq, masked):
            rows = pl.ds(pl.multiple_of(j * tq, tq), tq)
            kj, vj = k_ref[rows, :], v_ref[rows, :]
            s = _dot_nt(qv, kj)
            if masked:
                s = jnp.where(mask, s, NEG)
            p = jnp.exp(s - lse_)
            ds = (p * (_dot_nt(dob, vj) - dd)).astype(BF)
            dk_ref[rows, :] += _dot_tn(ds, qv)
            dv_ref[rows, :] += _dot_tn(p.astype(BF), dob)
            return dq + _dot(ds, kj)

        dq = lax.fori_loop(0, i, lambda j, c: step(j, c, False), jnp.zeros((tq, 128), F32))
        dq_ref[...] = step(i, dq, True)

    blk = pl.BlockSpec((tq, 128), lambda h, i: (i, h))
    full = pl.BlockSpec((T, 128), lambda h, i: (0, h))
    return pl.pallas_call(
        body, name="mla_bwd", grid=(4, T // tq), in_specs=[blk, full, full, blk, blk, blk],
        out_specs=[blk, full, full],
        out_shape=[jax.ShapeDtypeStruct((T, 512), F32)] * 3,
        compiler_params=_params(2))(q, k, v, o, lse, do)


def _sb_tile(qk, rr, strict, masked, upper):
    z2 = qk * (SB_SCALE * LOG2E)
    l1 = jnp.log2(1.0 + jnp.exp2(-jnp.abs(z2)))
    lk = -jnp.maximum(z2, 0.0) - l1
    if masked:
        lk = jnp.where(strict, lk, 0.0)
    after = rr + _dot(lk.astype(BF), upper)
    ll = jnp.minimum(z2, 0.0) - l1
    a = jnp.exp2(ll + after)
    if masked:
        a = jnp.where(strict, a, 0.0)
    return ll, a, jnp.sum(lk, axis=-1, keepdims=True)


SB_TQ, SB_TK = 256, 256
SB_DEAD = -160.0


def _sb_walk(trips, one_step, carry):
    def alive(c):
        t, cr = c
        top = jnp.maximum(jnp.max(cr[0][0]), jnp.max(cr[1][0]))
        return jnp.logical_and(t < trips, top > SB_DEAD)

    def body(c):
        t, cr = c
        return t + 1, one_step(t, cr)

    return lax.while_loop(alive, body, (jnp.int32(0), carry))[1]


def _sb_consts(tq, tk):
    row, col = _iota((tq, tk), 0), _iota((tq, tk), 1)
    strict = [col + d * tk < row for d in range(tq // tk)]
    r2, c2 = _iota((tk, tk), 0), _iota((tk, tk), 1)
    return strict, (r2 > c2).astype(BF), (r2 < c2).astype(BF)


def _sb_fwd(h16):
    T = h16.shape[0]
    tq, tk = SB_TQ, SB_TK
    nd = tq // tk

    def body(q_ref, k_ref, v_ref, o_ref):
        i = pl.program_id(1)
        strict, upper, _ = _sb_consts(tq, tk)
        lane = _iota((tq, 128), 1)
        q2 = q_ref[...]
        qms = [jnp.where(lane < 64, q2, jnp.zeros_like(q2)), jnp.where(lane >= 64, q2, jnp.zeros_like(q2))]

        def step(j, carry, d):
            rows = pl.ds(pl.multiple_of(j * tk, tk), tk)
            kj, vj = k_ref[rows, :], v_ref[rows, :]
            out = []
            for hh in range(2):
                rr, acc = carry[hh]
                _, a, rs = _sb_tile(_dot_nt(qms[hh], kj), rr, None if d is None else strict[d], d is not None, upper)
                out.append((rr + rs, acc + _dot(a.astype(BF), vj)))
            return tuple(out)


        carry = ((jnp.zeros((tq, 1), F32), jnp.zeros((tq, 128), F32)),) * 2
        for d in reversed(range(nd)):
            carry = step(nd * i + d, carry, d)
        carry = _sb_walk(nd * i, lambda t, c: step(nd * i - 1 - t, c, None), carry)
        o_ref[...] = jnp.where(lane < 64, carry[0][1], carry[1][1])

    return pl.pallas_call(
        body, name="sb_fwd", grid=(2, T // tq),
        in_specs=[pl.BlockSpec((tq, 128), lambda p, i: (i, C_DQ // 128 + p)),
                  pl.BlockSpec((T, 128), lambda p, i: (0, C_DK // 128 + p)),
                  pl.BlockSpec((T, 128), lambda p, i: (0, C_DV // 128 + p))],
        out_specs=pl.BlockSpec((tq, 128), lambda p, i: (i, p)),
        out_shape=jax.ShapeDtypeStruct((T, 256), F32),
        compiler_params=_params(2))(h16, h16, h16)


def _sb_bwd(h16, yd, dyd):
    T = h16.shape[0]
    tq, tk = SB_TQ, SB_TK
    nd = tq // tk

    def body(q_ref, k_ref, v_ref, o_ref, do_ref, dq_ref, dk_ref, dv_ref):
        i = pl.program_id(1)

        @pl.when(i == 0)
        def _():
            dk_ref[...] = jnp.zeros_like(dk_ref)
            dv_ref[...] = jnp.zeros_like(dv_ref)

        strict, upper, before = _sb_consts(tq, tk)
        lane = _iota((tq, 128), 1)
        lane_k = _iota((tk, 128), 1)
        q2 = q_ref[...]
        dob2 = do_ref[...].astype(BF)
        doo = dob2.astype(F32) * o_ref[...]
        mines = [lane < 64, lane >= 64]
        qms = [jnp.where(m, q2, jnp.zeros_like(q2)) for m in mines]
        doms = [jnp.where(m, dob2, jnp.zeros_like(dob2)) for m in mines]
        dds = [jnp.sum(jnp.where(m, doo, 0.0), axis=-1, keepdims=True) for m in mines]

        def step(j, carry, d):
            rows = pl.ds(pl.multiple_of(j * tk, tk), tk)
            kj, vj = k_ref[rows, :], v_ref[rows, :]
            out, dks, dvs = [], [], []
            for hh in range(2):
                rr, sg, dq = carry[hh]
                ll, a, rs = _sb_tile(_dot_nt(qms[hh], kj), rr, None if d is None else strict[d], d is not None,
                                     upper)
                ab = a.astype(BF)
                g = _dot_nt(doms[hh], vj) * ab.astype(F32)
                gs = jnp.sum(g, axis=-1, keepdims=True)
                pre = (dds[hh] - sg - gs) + _dot(g.astype(BF), before)
                dz = g - jnp.exp2(ll) * (g + pre)
                if d is not None:
                    dz = jnp.where(strict[d], dz, 0.0)
                dzb = dz.astype(BF)
                dks.append(_dot_tn(dzb, q2))
                dvs.append(_dot_tn(ab, dob2))
                out.append((rr + rs, sg + gs, dq + _dot(dzb, kj)))
            dk_ref[rows, :] += jnp.where(lane_k < 64, dks[0], dks[1]) * SB_SCALE
            dv_ref[rows, :] += jnp.where(lane_k < 64, dvs[0], dvs[1])
            return tuple(out)


        zero = jnp.zeros((tq, 1), F32)
        carry = ((zero, zero, jnp.zeros((tq, 128), F32)),) * 2
        for d in reversed(range(nd)):
            carry = step(nd * i + d, carry, d)
        carry = _sb_walk(nd * i, lambda t, c: step(nd * i - 1 - t, c, None), carry)
        dq_ref[...] = jnp.where(lane < 64, carry[0][2], carry[1][2]) * SB_SCALE

    blk = lambda c: pl.BlockSpec((tq, 128), lambda p, i: (i, c // 128 + p))
    full = lambda c: pl.BlockSpec((T, 128), lambda p, i: (0, c // 128 + p))
    return pl.pallas_call(
        body, name="sb_bwd", grid=(2, T // tq),
        in_specs=[blk(C_DQ), full(C_DK), full(C_DV), blk(0), blk(0)],
        out_specs=[blk(0), full(0), full(0)],
        out_shape=[jax.ShapeDtypeStruct((T, 256), F32)] * 3,
        compiler_params=_params(2))(h16, h16, h16, yd, dyd)


def _compact_c(ycp):
    return jnp.concatenate([ycp[:, h * 128:h * 128 + 64] for h in range(4)], axis=1)


def _post_fwd(ya, yb, ycp, yd, h32, ggrp, wout, gpost, x):
    T = x.shape[0]
    tm = 256

    def body(ya_ref, yb_ref, yc_ref, yd_ref, gate_ref, gg_ref, w_ref, gp_ref, x_ref, xn_ref, ym_ref, o_ref):
        ys = [ya_ref[...], yb_ref[...], _compact_c(yc_ref[...]), yd_ref[...]]
        gate = gate_ref[...]
        sil = gate * (1.0 / (1.0 + jnp.exp(-gate)))
        parts = []
        for gi in range(4):
            ng, _ = _rms_fwd(ys[gi], gg_ref[:, gi * 256:(gi + 1) * 256])
            parts.append(ng * sil[:, gi * 256:(gi + 1) * 256])
        ym = jnp.concatenate(parts, axis=1).astype(BF)
        ym_ref[...] = ym
        o = _dot(ym, w_ref[...])
        o_ref[...] = o
        on, _ = _rms_fwd(o, gp_ref[...])
        xn_ref[...] = x_ref[...] + on

    row = lambda w: pl.BlockSpec((tm, w), lambda n: (n, 0))
    vec = pl.BlockSpec((1, 1024), lambda n: (0, 0))
    return pl.pallas_call(
        body, name="post_fwd", grid=(T // tm,),
        in_specs=[row(256), row(256), row(512), row(256), pl.BlockSpec((tm, 1024), lambda n: (n, C_GATE // 1024)),
                  vec, pl.BlockSpec((1024, 1024), lambda n: (0, 0)), vec, row(1024)],
        out_specs=[row(1024), row(1024), row(1024)],
        out_shape=[jax.ShapeDtypeStruct((T, 1024), F32), jax.ShapeDtypeStruct((T, 1024), BF),
                   jax.ShapeDtypeStruct((T, 1024), F32)],
        compiler_params=_params(1))(ya, yb, ycp, yd, h32, ggrp, wout, gpost, x)


def _post_bwd(dx, o, gpost, woutt, ya, yb, ycp, yd, h32, ggrp):
    T = dx.shape[0]
    tm = 256

    def body(dx_ref, o_ref, gp_ref, w_ref, ya_ref, yb_ref, yc_ref, yd_ref, gate_ref, gg_ref,
             do_ref, dya_ref, dyb_ref, dyc_ref, dyd_ref, dgate_ref, dgp_ref, dgg_ref):
        n = pl.program_id(0)

        @pl.when(n == 0)
        def _():
            dgp_ref[...] = jnp.zeros_like(dgp_ref)
            dgg_ref[...] = jnp.zeros_like(dgg_ref)

        ov, gp = o_ref[...], gp_ref[...]
        _, ro = _rms_fwd(ov, gp)
        do, dgt = _rms_bwd(ov, gp, ro, dx_ref[...])
        dgp_ref[...] += jnp.sum(dgt, axis=0, keepdims=True)
        dob = do.astype(BF)
        do_ref[...] = dob
        dym = _dot(dob, w_ref[...])
        gate = gate_ref[...]
        sg = 1.0 / (1.0 + jnp.exp(-gate))
        sil = gate * sg
        dsil = sg * (1.0 + gate * (1.0 - sg))
        ys = [ya_ref[...], yb_ref[...], _compact_c(yc_ref[...]), yd_ref[...]]
        dys = []
        for gi in range(4):
            gs = slice(gi * 256, (gi + 1) * 256)
            gg = gg_ref[:, gs]
            ng, rg = _rms_fwd(ys[gi], gg)
            dgate_ref[:, gs] = (dym[:, gs] * ng * dsil[:, gs]).astype(BF)
            dy, dgt2 = _rms_bwd(ys[gi], gg, rg, dym[:, gs] * sil[:, gs])
            dgg_ref[:, gs] += jnp.sum(dgt2, axis=0, keepdims=True)
            dys.append(dy)
        dya_ref[...] = dys[0]
        dyb_ref[...] = dys[1]
        dyd_ref[...] = dys[3]
        z64 = jnp.zeros((tm, 64), F32)
        dyc_ref[...] = jnp.concatenate(
            [piece for h in range(4) for piece in (dys[2][:, h * 64:(h + 1) * 64], z64)], axis=1)

    row = lambda w: pl.BlockSpec((tm, w), lambda n: (n, 0))
    vec = pl.BlockSpec((1, 1024), lambda n: (0, 0))
    return pl.pallas_call(
        body, name="post_bwd", grid=(T // tm,),
        in_specs=[row(1024), row(1024), vec, pl.BlockSpec((1024, 1024), lambda n: (0, 0)),
                  row(256), row(256), row(512), row(256),
                  pl.BlockSpec((tm, 1024), lambda n: (n, C_GATE // 1024)), vec],
        out_specs=[row(1024), row(256), row(256), row(512), row(256), row(1024), vec, vec],
        out_shape=[jax.ShapeDtypeStruct((T, 1024), BF), jax.ShapeDtypeStruct((T, 256), F32),
                   jax.ShapeDtypeStruct((T, 256), F32), jax.ShapeDtypeStruct((T, 512), F32),
                   jax.ShapeDtypeStruct((T, 256), F32), jax.ShapeDtypeStruct((T, 1024), BF),
                   jax.ShapeDtypeStruct((1, 1024), F32), jax.ShapeDtypeStruct((1, 1024), F32)],
        compiler_params=_params(1))(dx, o, gpost, woutt, ya, yb, ycp, yd, h32, ggrp)


def _loss_head(y, tgt):
    T = y.shape[0]
    tm = 512 if T % 512 == 0 else T

    def body(y_ref, t_ref, s_ref, dy_ref):
        n = pl.program_id(0)

        @pl.when(n == 0)
        def _():
            s_ref[...] = jnp.zeros_like(s_ref)

        d = y_ref[...] - t_ref[...]
        s_ref[...] += jnp.sum(d * d, axis=0, keepdims=True)
        dy_ref[...] = d * (1.0 / D_MODEL)

    row = pl.BlockSpec((tm, 1024), lambda n: (n, 0))
    return pl.pallas_call(
        body, name="loss_head", grid=(T // tm,), in_specs=[row, row],
        out_specs=[pl.BlockSpec((1, 1024), lambda n: (0, 0)), row],
        out_shape=[jax.ShapeDtypeStruct((1, 1024), F32), jax.ShapeDtypeStruct((T, 1024), F32)],
        compiler_params=_params(1))(y, tgt)


def _swap_rows32(a):
    return jnp.concatenate([a[16:32], a[0:16]], axis=0)


def _pad_w_in_t(wt):
    z = lambda n: jnp.zeros((n, wt.shape[1]), wt.dtype)
    kr = wt[1664:1696]
    return jnp.concatenate([wt[2464:3488], wt[0:1664], z(64), kr, z(32), z(64), _swap_rows32(kr), z(32),
                            wt[1696:2464], z(NP - C_END)], axis=0)


def _unpad_w_in_t(d):
    kr = d[C_CKR + 64:C_CKR + 96] + _swap_rows32(d[C_CKRS + 64:C_CKRS + 96])
    return jnp.concatenate([d[C_AQ:C_CKR], kr, d[C_DQ:C_END], d[0:1024]], axis=0)


def _pad_w_uq(w):
    z = lambda n: jnp.zeros((w.shape[0], n), w.dtype)
    a = [p for h in range(4) for p in (w[:, 96 * h:96 * h + 96], z(32))]
    b = [p for h in range(4) for p in (z(64), _swap32(w[:, 96 * h + 64:96 * h + 96]), z(32))]
    return jnp.concatenate(a + b, axis=1)


def _unpad_w_uq(d):
    out = []
    for h in range(4):
        out.append(d[:, 128 * h:128 * h + 64])
        out.append(d[:, 128 * h + 64:128 * h + 96] + _swap32(d[:, 512 + 128 * h + 64:512 + 128 * h + 96]))
    return jnp.concatenate(out, axis=1)


def _pad_w_ukv(w):
    z = jnp.zeros((w.shape[0], 64), w.dtype)
    a = [p for h in range(4) for p in (w[:, 128 * h:128 * h + 64], z)]
    b = [p for h in range(4) for p in (w[:, 128 * h + 64:128 * h + 128], z)]
    return jnp.concatenate(a + b, axis=1)


def _unpad_w_ukv(d):
    return jnp.concatenate([p for h in range(4) for p in (d[:, 128 * h:128 * h + 64],
                                                          d[:, 512 + 128 * h:512 + 128 * h + 64])], axis=1)


def _rope_tables(pos):
    freqs = 10000.0 ** (-jnp.arange(16, dtype=F32) / 16)
    ang = pos.astype(F32)[:, None] * freqs
    c, s = jnp.cos(ang), jnp.sin(ang)
    z = lambda n: jnp.zeros((pos.shape[0], n), F32)
    return (jnp.concatenate([z(64), c, c, z(32)], axis=1), jnp.concatenate([z(64), -s, s, z(32)], axis=1))


def _layer_weights(W, l):
    wuq2 = _pad_w_uq(W["mla_w_uq"][l])
    wkv2 = _pad_w_ukv(W["mla_w_ukv"][l])
    wout = W["w_out"][l]
    cw = jnp.concatenate([W["conv_w"][l].astype(F32), jnp.zeros((5, 256), F32)], axis=0)
    return dict(
        wpt=W["wpt"][l], wuq2=wuq2.astype(BF), wuq2t=wuq2.T.astype(BF),
        wkv2=wkv2.astype(BF), wkv2t=wkv2.T.astype(BF), wout=wout.astype(BF), woutt=wout.T.astype(BF),
        cw=cw, cb=W["conv_b"][l][None, :], sinks=W["attn_sinks"][l],
        gpre=W["norm_pre"][l][None, :], gq=W["mla_q_norm"][l][None, :], gkv=W["mla_kv_norm"][l][None, :],
        ggrp=W["group_norm"][l][None, :], gpost=W["norm_post"][l][None, :])


def _local_step(x, pos, W, tgt):
    cosk, sin = _rope_tables(pos)
    saved = []
    for l in range(DEPTH):
        lw = _layer_weights(W, l)
        h32, h16, xn = _inproj_fwd(x, lw["gpre"], lw["wpt"])
        ya = _swa_fwd(h16, lw["sinks"])
        yb = _conv_fwd(h32, lw["cw"], lw["cb"])
        qc, kc, vc = _cprep_fwd(h32, lw["gq"], lw["gkv"], lw["wuq2"], lw["wkv2"], cosk, sin)
        ycp, lse = _mla_fwd(qc, kc, vc)
        yd = _sb_fwd(h16)
        x_new, ym, o = _post_fwd(ya, yb, ycp, yd, h32, lw["ggrp"], lw["wout"], lw["gpost"], x)
        saved.append(dict(lw=lw, x=x, h32=h32, h16=h16, xn=xn, ya=ya, yb=yb, qc=qc, kc=kc, vc=vc, ycp=ycp,
                          lse=lse, yd=yd, ym=ym, o=o))
        x = x_new
    sq, dx = _loss_head(x, tgt)

    grads = {k: [None] * DEPTH for k in ("norm_pre", "w_in_pt", "attn_sinks", "conv_w", "conv_b", "mla_q_norm",
                                         "mla_w_uq", "mla_kv_norm", "mla_w_ukv", "group_norm", "w_out",
                                         "norm_post")}
    for l in reversed(range(DEPTH)):
        s = saved[l]
        lw = s["lw"]
        do, dya, dyb, dycp, dyd, dgate, dgpost, dggrp = _post_bwd(
            dx, s["o"], lw["gpost"], lw["woutt"], s["ya"], s["yb"], s["ycp"], s["yd"], s["h32"], lw["ggrp"])
        grads["norm_post"][l] = dgpost[0]
        grads["group_norm"][l] = dggrp[0]
        grads["w_out"][l] = _matmul_tn(s["ym"], do, 512, "dw_out")
        sdq, sdk, sdv = _sb_bwd(s["h16"], s["yd"], dyd)
        mdq, mdk, mdv = _mla_bwd(s["qc"], s["kc"], s["vc"], s["ycp"], s["lse"], dycp)
        (dcq, dckv, dckr, dckrs, dq2, dkv2, cqn, ckvn, dgq, dgkv) = _cprep_bwd(
            s["h32"], lw["gq"], lw["gkv"], lw["wuq2t"], lw["wkv2t"], cosk, sin, mdq, mdk, mdv)
        grads["mla_q_norm"][l] = dgq[0]
        grads["mla_kv_norm"][l] = dgkv[0]
        grads["mla_w_uq"][l] = _unpad_w_uq(_matmul_tn(cqn, dq2, 512, "dw_uq"))
        grads["mla_w_ukv"][l] = _unpad_w_ukv(_matmul_tn(ckvn, dkv2, 512, "dw_ukv"))
        dbb, dbc, dbx, dcw = _conv_bwd(s["h32"], lw["cw"], lw["cb"], dyb)
        grads["conv_w"][l] = dcw[0:3]
        grads["conv_b"][l] = dcw[3]
        adq, adk, adv, dsk = _swa_bwd(s["h16"], lw["sinks"], dya)
        grads["attn_sinks"][l] = dsk[0, 0:4]
        parts = [dgate, adq, adk, adv, dbb, dbc, dbx, dcq, dckv, dckr, dckrs, sdq, sdk, sdv]
        dh, dx, dgpre = _inproj_bwd_x(parts, lw["wpt"], s["x"], lw["gpre"], dx)
        grads["w_in_pt"][l] = _matmul_tn(s["xn"], dh, 1280, "dw_in").T
        grads["norm_pre"][l] = dgpre[0]
    return sq, dx, grads


SMALL_SHARDED = ("conv_w", "mla_w_uq", "mla_w_ukv")
REPLICATED = ("norm_pre", "attn_sinks", "conv_b", "mla_q_norm", "mla_kv_norm", "group_norm", "norm_post")
ORDER = ("norm_pre", "w_in", "attn_sinks", "conv_w", "conv_b", "mla_q_norm", "mla_w_uq", "mla_kv_norm",
         "mla_w_ukv", "group_norm", "w_out", "norm_post")
W_IN_COLS = 436
W_IN_WIN = 440
SMALL_ROWS = 48


def _pack_small(arrs, dtype):
    flat = jnp.concatenate([a.reshape(-1).astype(dtype) for a in arrs])
    flat = jnp.concatenate([flat, jnp.zeros((SMALL_ROWS * D_MODEL - flat.shape[0],), dtype)])
    return flat.reshape(SMALL_ROWS, D_MODEL)


def _unpack_small(buf, shapes):
    flat = buf.reshape(-1)
    out, off = [], 0
    for s in shapes:
        n = int(np.prod(s))
        out.append(flat[off:off + n].reshape(s))
        off += n
    return out


def _pack_state(ps, c):
    k = len(ps)
    wt = jnp.transpose(jnp.stack([p["w_in"] for p in ps]), (0, 1, 3, 2))
    win = lax.dynamic_update_slice(jnp.zeros((k, DEPTH, W_IN_WIN, D_MODEL), F32), wt, (0, 0, 4 * c, 0))
    wout = jnp.stack([p["w_out"] for p in ps]).reshape(k, DEPTH * 128, D_MODEL)
    flat = jnp.stack([jnp.concatenate([p[n].reshape(-1) for n in SMALL_SHARDED + REPLICATED]) for p in ps])
    small = jnp.pad(flat, ((0, 0), (0, SMALL_ROWS * D_MODEL - flat.shape[1]))).reshape(k, SMALL_ROWS, D_MODEL)
    return jnp.concatenate([win.reshape(k, DEPTH * W_IN_WIN, D_MODEL), wout, small], axis=1)


def _unpack_state(buf, p, c):
    k = buf.shape[0]
    nw = DEPTH * W_IN_WIN
    win = lax.dynamic_slice(buf[:, 0:nw].reshape(k, DEPTH, W_IN_WIN, D_MODEL), (0, 0, 4 * c, 0),
                            (k, DEPTH, W_IN_COLS, D_MODEL))
    out = {"w_in": jnp.transpose(win, (0, 1, 3, 2)),
           "w_out": buf[:, nw:nw + DEPTH * 128].reshape(k, DEPTH, 128, D_MODEL)}
    flat = buf[:, nw + DEPTH * 128:].reshape(k, SMALL_ROWS * D_MODEL)
    off = 0
    for n in SMALL_SHARDED + REPLICATED:
        size = int(np.prod(p[n].shape))
        out[n] = flat[:, off:off + size].reshape((k,) + p[n].shape)
        off += size
    return out


def _rows_of_w_in_t(lo, hi, padded, kr):
    segs = ((0, 1664, padded, C_AQ), (1664, 1696, kr, 0), (1696, 2464, padded, C_DQ), (2464, 3488, padded, C_GATE))
    out = []
    for s0, s1, src, base in segs:
        a, b = max(lo, s0), min(hi, s1)
        if a < b:
            out.append(src[base + a - s0:base + b - s0])
    return out


def _me():
    return lax.axis_index("x"), lax.axis_index("y"), lax.axis_index("c")


def _peer(k):
    x, y, c = _me()
    return (x ^ (k >> 2 & 1), y ^ (k >> 1 & 1), c ^ (k & 1))


def _all_gather(block):
    R, C = block.shape

    def body(src_ref, out_ref, send_sems, recv_sems, local_sem):
        x, y, c = _me()
        me = 4 * x + 2 * y + c
        mine = pltpu.make_async_copy(src_ref, out_ref.at[me], local_sem)
        mine.start()
        copies = [pltpu.make_async_remote_copy(src_ref=src_ref, dst_ref=out_ref.at[me], send_sem=send_sems.at[k - 1],
                                               recv_sem=recv_sems.at[k - 1], device_id=_peer(k), device_id_type=MESH)
                  for k in range(1, N_DEV)]
        for cp in copies:
            cp.start()
        for cp in copies:
            cp.wait()
        mine.wait()

    return pl.pallas_call(
        body, name="all_gather", out_shape=jax.ShapeDtypeStruct((N_DEV, R, C), block.dtype),
        in_specs=[pl.BlockSpec(memory_space=pl.ANY)], out_specs=pl.BlockSpec(memory_space=pl.ANY),
        scratch_shapes=[pltpu.SemaphoreType.DMA((N_DEV - 1,)), pltpu.SemaphoreType.DMA((N_DEV - 1,)),
                        pltpu.SemaphoreType.DMA])(block)


def _all_to_all(blocks):
    _, R, C = blocks.shape

    def body(src_ref, out_ref, send_sems, recv_sems, local_sem):
        x, y, c = _me()
        me = 4 * x + 2 * y + c
        mine = pltpu.make_async_copy(src_ref.at[me], out_ref.at[me], local_sem)
        mine.start()
        copies = []
        for k in range(1, N_DEV):
            px, py, pc = _peer(k)
            copies.append(pltpu.make_async_remote_copy(
                src_ref=src_ref.at[4 * px + 2 * py + pc], dst_ref=out_ref.at[me], send_sem=send_sems.at[k - 1],
                recv_sem=recv_sems.at[k - 1], device_id=(px, py, pc), device_id_type=MESH))
        for cp in copies:
            cp.start()
        for cp in copies:
            cp.wait()
        mine.wait()

    return pl.pallas_call(
        body, name="all_to_all", out_shape=jax.ShapeDtypeStruct((N_DEV, R, C), blocks.dtype),
        in_specs=[pl.BlockSpec(memory_space=pl.ANY)], out_specs=pl.BlockSpec(memory_space=pl.ANY),
        scratch_shapes=[pltpu.SemaphoreType.DMA((N_DEV - 1,)), pltpu.SemaphoreType.DMA((N_DEV - 1,)),
                        pltpu.SemaphoreType.DMA])(blocks)


def _adamw(parts, state):
    _, R, C = state.shape
    tr = 32
    assert R % tr == 0

    def body(p_ref, s_ref, o_ref):
        g = p_ref[0].astype(F32)
        for k in range(1, N_DEV):
            g = g + p_ref[k].astype(F32)
        o_ref[0] = g
        m_ = ADAM_B1 * s_ref[1] + (1.0 - ADAM_B1) * g
        v_ = ADAM_B2 * s_ref[2] + (1.0 - ADAM_B2) * (g * g)
        o_ref[2] = m_
        o_ref[3] = v_
        m_hat = m_ / (1.0 - ADAM_B1 ** ADAM_STEP)
        v_hat = v_ / (1.0 - ADAM_B2 ** ADAM_STEP)
        o_ref[1] = -ADAM_LR * (m_hat / (jnp.sqrt(v_hat) + ADAM_EPS) + ADAM_WD * s_ref[0])

    return pl.pallas_call(
        body, name="adamw", grid=(R // tr,),
        in_specs=[pl.BlockSpec((N_DEV, tr, C), lambda n: (0, n, 0)), pl.BlockSpec((3, tr, C), lambda n: (0, n, 0))],
        out_specs=pl.BlockSpec((4, tr, C), lambda n: (0, n, 0)), out_shape=jax.ShapeDtypeStruct((4, R, C), F32),
        compiler_params=_params(1))(parts, state)


def kernel(x, positions, norm_pre, w_in, attn_sinks, conv_w, conv_b, mla_q_norm, mla_w_uq, mla_kv_norm, mla_w_ukv, group_norm, w_out, norm_post, loss_target, m_norm_pre, m_w_in, m_attn_sinks, m_conv_w, m_conv_b, m_mla_q_norm, m_mla_w_uq, m_mla_kv_norm, m_mla_w_ukv, m_group_norm, m_w_out, m_norm_post, v_norm_pre, v_w_in, v_attn_sinks, v_conv_w, v_conv_b, v_mla_q_norm, v_mla_w_uq, v_mla_kv_norm, v_mla_w_ukv, v_group_norm, v_w_out, v_norm_post):
    local = dict(norm_pre=norm_pre, w_in=w_in, attn_sinks=attn_sinks, conv_w=conv_w, conv_b=conv_b,
                 mla_q_norm=mla_q_norm, mla_w_uq=mla_w_uq, mla_kv_norm=mla_kv_norm, mla_w_ukv=mla_w_ukv,
                 group_norm=group_norm, w_out=w_out, norm_post=norm_post)
    mom = dict(norm_pre=m_norm_pre, w_in=m_w_in, attn_sinks=m_attn_sinks, conv_w=m_conv_w, conv_b=m_conv_b,
               mla_q_norm=m_mla_q_norm, mla_w_uq=m_mla_w_uq, mla_kv_norm=m_mla_kv_norm, mla_w_ukv=m_mla_w_ukv,
               group_norm=m_group_norm, w_out=m_w_out, norm_post=m_norm_post)
    vel = dict(norm_pre=v_norm_pre, w_in=v_w_in, attn_sinks=v_attn_sinks, conv_w=v_conv_w, conv_b=v_conv_b,
               mla_q_norm=v_mla_q_norm, mla_w_uq=v_mla_w_uq, mla_kv_norm=v_mla_kv_norm, mla_w_ukv=v_mla_w_ukv,
               group_norm=v_group_norm, w_out=v_w_out, norm_post=v_norm_post)

    c = lax.axis_index("c")

    wt = jnp.transpose(w_in, (0, 2, 1)).astype(BF)
    wt = jnp.concatenate([wt, jnp.zeros((DEPTH, 448 - W_IN_COLS, D_MODEL), BF)], axis=1)
    payload = jnp.concatenate([wt.reshape(DEPTH * 448, D_MODEL), w_out.astype(BF).reshape(DEPTH * 128, D_MODEL),
                               _pack_small([local[n] for n in SMALL_SHARDED], BF)], axis=0)
    gathered = _all_gather(payload)
    W = {n: local[n] for n in REPLICATED}

    def nat_rows(l, lo, hi):
        out, r = [], lo
        while r < hi:
            d = r // W_IN_COLS
            e = min(hi, (d + 1) * W_IN_COLS)
            out.append(gathered[d, 448 * l + r - W_IN_COLS * d:448 * l + e - W_IN_COLS * d])
            r = e
        return out

    z = lambda n: [jnp.zeros((n, D_MODEL), BF)]
    W["wpt"] = [jnp.concatenate(nat_rows(l, 2464, 3488) + nat_rows(l, 0, 1664) + z(64) + nat_rows(l, 1664, 1696)
                                + z(96) + nat_rows(l, 1680, 1696) + nat_rows(l, 1664, 1680) + z(32)
                                + nat_rows(l, 1696, 2464) + z(NP - C_END), axis=0) for l in range(DEPTH)]
    wo0 = DEPTH * 448
    W["w_out"] = gathered[:, wo0:wo0 + DEPTH * 128].reshape(N_DEV, DEPTH, 128, D_MODEL).transpose(1, 0, 2, 3).reshape(
        DEPTH, D_MODEL, D_MODEL)
    flat = gathered[:, wo0 + DEPTH * 128:].reshape(N_DEV, SMALL_ROWS * D_MODEL)
    off = 0
    for n in SMALL_SHARDED:
        depth, rows, width = local[n].shape
        size = depth * rows * width
        W[n] = flat[:, off:off + size].reshape(N_DEV, depth, rows, width).transpose(1, 2, 0, 3).reshape(
            depth, rows, N_DEV * width)
        off += size

    sq, grad_x, g = _local_step(x[0], positions[0], W, loss_target[0])
    loss = lax.psum(0.5 / D_MODEL * jnp.sum(sq), ("x", "y", "c"))

    cols = []
    for n in SMALL_SHARDED:
        depth, rows, width = local[n].shape
        cols.append(jnp.stack(g[n]).reshape(depth, rows, N_DEV, width).transpose(2, 0, 1, 3).reshape(N_DEV, -1))
    rep = jnp.concatenate([a.reshape(-1) for n in REPLICATED for a in g[n]])
    cols.append(jnp.broadcast_to(rep[None], (N_DEV, rep.shape[0])))
    small = jnp.concatenate(cols, axis=1)
    small = jnp.pad(small, ((0, 0), (0, SMALL_ROWS * D_MODEL - small.shape[1]))).reshape(N_DEV, SMALL_ROWS, D_MODEL)
    krs = [p[C_CKR + 64:C_CKR + 96] + _swap_rows32(p[C_CKRS + 64:C_CKRS + 96]) for p in g["w_in_pt"]]
    pieces = []
    for d in range(N_DEV):
        lo = W_IN_COLS * d // 8 * 8
        for l in range(DEPTH):
            pieces += _rows_of_w_in_t(lo, lo + W_IN_WIN, g["w_in_pt"][l], krs[l])
        pieces += [g["w_out"][l][128 * d:128 * (d + 1)] for l in range(DEPTH)]
        pieces.append(small[d])
    blocks = jnp.concatenate(pieces, axis=0).astype(BF).reshape(N_DEV, -1, D_MODEL)
    received = _all_to_all(blocks)

    out = _unpack_state(_adamw(received, _pack_state([local, mom, vel], c)), local, c)
    return (loss, grad_x[None], *[out[n][t] for t in range(4) for n in ORDER])
```

```python
import functools

import jax
import jax.numpy as jnp
import numpy as np
from jax import lax
from jax.experimental import pallas as pl
from jax.experimental.pallas import tpu as pltpu

F32 = jnp.float32
BF = jnp.bfloat16
MESH = pl.DeviceIdType.MESH

D_MODEL = 1024
DEPTH = 2
EPS = 1e-6
N_DEV = 8
VMEM_LIMIT = 56 * 1024 * 1024
NEG = -1e30
MLA_SCALE = 96.0 ** -0.5
SB_SCALE = 0.125
LOG2E = 1.4426950408889634

NP = 3840
C_GATE = 0
C_AQ = 1024
C_AK = 1280
C_AV = 1408
C_BB = 1536
C_BC = 1792
C_BX = 2048
C_CQ = 2304
C_CKV = 2560
C_CKR = 2688
C_CKRS = 2816
C_DQ = 2944
C_DK = 3200
C_DV = 3456
C_END = 3712

def _swap32(a):
    return jnp.concatenate([a[:, 16:32], a[:, 0:16]], axis=1)

ADAM_LR, ADAM_B1, ADAM_B2, ADAM_EPS, ADAM_WD, ADAM_STEP = 0.001, 0.9, 0.999, 1e-08, 0.01, 10


def _dot(a, b):
    return jnp.dot(a, b, preferred_element_type=F32)


def _dot_nt(a, b):
    return lax.dot_general(a, b, (((1,), (1,)), ((), ())), preferred_element_type=F32)


def _dot_tn(a, b):
    return lax.dot_general(a, b, (((0,), (0,)), ((), ())), preferred_element_type=F32)


def _params(n_grid):
    return pltpu.CompilerParams(dimension_semantics=("arbitrary",) * n_grid, vmem_limit_bytes=VMEM_LIMIT)


def _rms_fwd(x, g):
    r = lax.rsqrt(jnp.mean(x * x, axis=-1, keepdims=True) + EPS)
    return (x * r) * g, r


def _rms_bwd(x, g, r, dy, width=None):
    n = x.shape[-1] if width is None else width
    u = dy * g
    dx = r * u - x * (r * r * r) * (jnp.sum(x * u, axis=-1, keepdims=True) / n)
    return dx, dy * (x * r)


def _iota(shape, axis):
    return lax.broadcasted_iota(jnp.int32, shape, axis)


def _inproj_fwd(x, g, wpt):
    T = x.shape[0]
    tm = 256

    def body(x_ref, g_ref, w_ref, h32_ref, h16_ref, xn_ref):
        xn, _ = _rms_fwd(x_ref[...], g_ref[...])
        xn = xn.astype(BF)
        xn_ref[...] = xn
        h = _dot_nt(xn, w_ref[...])
        h32_ref[...] = h
        h16_ref[...] = h.astype(BF)

    return pl.pallas_call(
        body, name="inproj_fwd", grid=(T // tm,),
        in_specs=[pl.BlockSpec((tm, D_MODEL), lambda n: (n, 0)),
                  pl.BlockSpec((1, D_MODEL), lambda n: (0, 0)),
                  pl.BlockSpec((NP, D_MODEL), lambda n: (0, 0))],
        out_specs=[pl.BlockSpec((tm, NP), lambda n: (n, 0)),
                   pl.BlockSpec((tm, NP), lambda n: (n, 0)),
                   pl.BlockSpec((tm, D_MODEL), lambda n: (n, 0))],
        out_shape=[jax.ShapeDtypeStruct((T, NP), F32), jax.ShapeDtypeStruct((T, NP), BF),
                   jax.ShapeDtypeStruct((T, D_MODEL), BF)],
        compiler_params=_params(1))(x, g, wpt)


def _inproj_bwd_x(parts, wpt, x, g, dxo):
    T = x.shape[0]
    tm = 256
    np_ = len(parts)
    assert sum(p.shape[1] for p in parts) == C_END

    def body(*refs):
        part_refs = refs[:np_]
        w_ref, x_ref, g_ref, dxo_ref, dh_ref, dx_ref, dg_ref = refs[np_:]
        n = pl.program_id(0)
        dh = jnp.concatenate([r[...].astype(BF) for r in part_refs] + [jnp.zeros((tm, NP - C_END), BF)], axis=1)
        dh_ref[...] = dh
        dxn = _dot(dh, w_ref[...])
        xv = x_ref[...]
        _, r = _rms_fwd(xv, g_ref[...])
        dx, dgt = _rms_bwd(xv, g_ref[...], r, dxn)
        dx_ref[...] = dxo_ref[...] + dx

        @pl.when(n == 0)
        def _():
            dg_ref[...] = jnp.zeros_like(dg_ref)

        dg_ref[...] += jnp.sum(dgt, axis=0, keepdims=True)

    return pl.pallas_call(
        body, name="inproj_bwd_x", grid=(T // tm,),
        in_specs=[pl.BlockSpec((tm, p.shape[1]), lambda n: (n, 0)) for p in parts]
        + [pl.BlockSpec((NP, D_MODEL), lambda n: (0, 0)),
           pl.BlockSpec((tm, D_MODEL), lambda n: (n, 0)),
           pl.BlockSpec((1, D_MODEL), lambda n: (0, 0)),
           pl.BlockSpec((tm, D_MODEL), lambda n: (n, 0))],
        out_specs=[pl.BlockSpec((tm, NP), lambda n: (n, 0)),
                   pl.BlockSpec((tm, D_MODEL), lambda n: (n, 0)),
                   pl.BlockSpec((1, D_MODEL), lambda n: (0, 0))],
        out_shape=[jax.ShapeDtypeStruct((T, NP), BF), jax.ShapeDtypeStruct((T, D_MODEL), F32),
                   jax.ShapeDtypeStruct((1, D_MODEL), F32)],
        compiler_params=_params(1))(*parts, wpt, x, g, dxo)


def _matmul_tn(a, b, tn, name):
    T, M = a.shape
    N = b.shape[1]
    tk = 512 if T % 512 == 0 else T

    def body(a_ref, b_ref, o_ref):
        k = pl.program_id(1)

        @pl.when(k == 0)
        def _():
            o_ref[...] = jnp.zeros_like(o_ref)

        o_ref[...] += _dot_tn(a_ref[...], b_ref[...])

    return pl.pallas_call(
        body, name=name, grid=(N // tn, T // tk),
        in_specs=[pl.BlockSpec((tk, M), lambda j, k: (k, 0)),
                  pl.BlockSpec((tk, tn), lambda j, k: (k, j))],
        out_specs=pl.BlockSpec((M, tn), lambda j, k: (0, j)),
        out_shape=jax.ShapeDtypeStruct((M, N), F32),
        compiler_params=_params(2))(a, b)


SWA_BLK = 128
SWA_TQ = 1024


def _bdot_nt(a, b):
    return lax.dot_general(a, b, (((2,), (2,)), ((0,), (0,))), preferred_element_type=F32)


def _bdot(a, b):
    return lax.dot_general(a, b, (((2,), (1,)), ((0,), (0,))), preferred_element_type=F32)


def _bdot_tn(a, b):
    return lax.dot_general(a, b, (((1,), (1,)), ((0,), (0,))), preferred_element_type=F32)


def _swa_probs(q, kc, kp, sink, mask_c, mask_p):
    sc = jnp.where(mask_c, _bdot_nt(q, kc) * SB_SCALE, NEG)
    sp = jnp.where(mask_p, _bdot_nt(q, kp) * SB_SCALE, NEG)
    m = jnp.maximum(jnp.maximum(jnp.max(sc, axis=-1, keepdims=True), jnp.max(sp, axis=-1, keepdims=True)), sink)
    pc = jnp.exp(sc - m)
    pp = jnp.exp(sp - m)
    ps = jnp.exp(sink - m)
    inv = 1.0 / (jnp.sum(pc, axis=-1, keepdims=True) + jnp.sum(pp, axis=-1, keepdims=True) + ps)
    return pc * inv, pp * inv, ps * inv


def _swa_masks(n, nb):
    blk = _iota((nb, SWA_BLK, SWA_BLK), 0)
    row = _iota((nb, SWA_BLK, SWA_BLK), 1)
    col = _iota((nb, SWA_BLK, SWA_BLK), 2)
    return col <= row, jnp.logical_and(col > row, jnp.logical_or(blk > 0, n > 0))


def _swa_specs(tq):
    halo = tq // SWA_BLK
    return [pl.BlockSpec(memory_space=pltpu.SMEM),
            pl.BlockSpec((tq, 256), lambda n: (n, C_AQ // 256)),
            pl.BlockSpec((tq, 128), lambda n: (n, C_AK // 128)),
            pl.BlockSpec((SWA_BLK, 128), lambda n: (jnp.maximum(n * halo - 1, 0), C_AK // 128)),
            pl.BlockSpec((tq, 128), lambda n: (n, C_AV // 128)),
            pl.BlockSpec((SWA_BLK, 128), lambda n: (jnp.maximum(n * halo - 1, 0), C_AV // 128))]


def _swa_blocked(cur_ref, prev_ref, gs, nb):
    cur = cur_ref[:, gs].reshape(nb, SWA_BLK, 64)
    prev = jnp.concatenate([prev_ref[:, gs].reshape(1, SWA_BLK, 64), cur[:nb - 1]], axis=0) if nb > 1 \
        else prev_ref[:, gs].reshape(1, SWA_BLK, 64)
    return cur, prev


def _swa_fwd(h16, sinks):
    T = h16.shape[0]
    tq = SWA_TQ if T % SWA_TQ == 0 else SWA_BLK
    nb = tq // SWA_BLK

    def body(s_ref, q_ref, kc_ref, kp_ref, vc_ref, vp_ref, o_ref):
        n = pl.program_id(0)
        mask_c, mask_p = _swa_masks(n, nb)
        for h in range(4):
            hs = slice(h * 64, (h + 1) * 64)
            gs = slice(h // 2 * 64, (h // 2 + 1) * 64)
            kc, kp = _swa_blocked(kc_ref, kp_ref, gs, nb)
            vc, vp = _swa_blocked(vc_ref, vp_ref, gs, nb)
            pc, pp, _ = _swa_probs(q_ref[:, hs].reshape(nb, SWA_BLK, 64), kc, kp, s_ref[h], mask_c, mask_p)
            o_ref[:, hs] = (_bdot(pc.astype(BF), vc) + _bdot(pp.astype(BF), vp)).reshape(tq, 64)

    return pl.pallas_call(
        body, name="swa_fwd", grid=(T // tq,), in_specs=_swa_specs(tq),
        out_specs=pl.BlockSpec((tq, 256), lambda n: (n, 0)),
        out_shape=jax.ShapeDtypeStruct((T, 256), F32),
        compiler_params=_params(1))(sinks, h16, h16, h16, h16, h16)


def _swa_bwd(h16, sinks, dya):
    T = h16.shape[0]
    tq = SWA_TQ if T % SWA_TQ == 0 else SWA_BLK
    nb = tq // SWA_BLK

    def body(s_ref, q_ref, kc_ref, kp_ref, vc_ref, vp_ref, do_ref, dq_ref, dk_ref, dv_ref, ds_ref):
        n = pl.program_id(0)

        @pl.when(n == 0)
        def _():
            dk_ref[...] = jnp.zeros_like(dk_ref)
            dv_ref[...] = jnp.zeros_like(dv_ref)
            ds_ref[...] = jnp.zeros_like(ds_ref)

        mask_c, mask_p = _swa_masks(n, nb)
        rows = pl.ds(pl.multiple_of(n * tq, tq), tq)
        before = pl.ds(pl.multiple_of(jnp.maximum(n * nb - 1, 0) * SWA_BLK, SWA_BLK), SWA_BLK)
        lane = _iota((8, 128), 1)
        row8 = _iota((8, 128), 0)

        def to_keys(own, prev):
            if nb == 1:
                return own
            return own + jnp.concatenate([prev[1:], jnp.zeros((1, SWA_BLK, 64), F32)], axis=0)

        for h in range(4):
            hs = slice(h * 64, (h + 1) * 64)
            gs = slice(h // 2 * 64, (h // 2 + 1) * 64)
            q = q_ref[:, hs].reshape(nb, SWA_BLK, 64)
            kc, kp = _swa_blocked(kc_ref, kp_ref, gs, nb)
            vc, vp = _swa_blocked(vc_ref, vp_ref, gs, nb)
            pc, pp, ps = _swa_probs(q, kc, kp, s_ref[h], mask_c, mask_p)
            pcb, ppb = pc.astype(BF), pp.astype(BF)
            do = do_ref[:, hs].reshape(nb, SWA_BLK, 64)
            dob = do.astype(BF)
            o = _bdot(pcb, vc) + _bdot(ppb, vp)
            dd = jnp.sum(do * o, axis=-1, keepdims=True)
            dsc = (pc * (_bdot_nt(dob, vc) - dd) * SB_SCALE).astype(BF)
            dsp = (pp * (_bdot_nt(dob, vp) - dd) * SB_SCALE).astype(BF)
            dq_ref[:, hs] = (_bdot(dsc, kc) + _bdot(dsp, kp)).reshape(tq, 64).astype(BF)
            dkp, dvp = _bdot_tn(dsp, q), _bdot_tn(ppb, dob)
            dk_ref[rows, gs] += to_keys(_bdot_tn(dsc, q), dkp).reshape(tq, 64)
            dv_ref[rows, gs] += to_keys(_bdot_tn(pcb, dob), dvp).reshape(tq, 64)
            dk_ref[before, gs] += dkp[0]
            dv_ref[before, gs] += dvp[0]
            ds_ref[...] += jnp.where(jnp.logical_and(lane == h, row8 == 0), -jnp.sum(ps * dd), 0.0)

    return pl.pallas_call(
        body, name="swa_bwd", grid=(T // tq,),
        in_specs=_swa_specs(tq) + [pl.BlockSpec((tq, 256), lambda n: (n, 0))],
        out_specs=[pl.BlockSpec((tq, 256), lambda n: (n, 0)),
                   pl.BlockSpec((T, 128), lambda n: (0, 0)),
                   pl.BlockSpec((T, 128), lambda n: (0, 0)),
                   pl.BlockSpec((8, 128), lambda n: (0, 0))],
        out_shape=[jax.ShapeDtypeStruct((T, 256), BF), jax.ShapeDtypeStruct((T, 128), F32),
                   jax.ShapeDtypeStruct((T, 128), F32), jax.ShapeDtypeStruct((8, 128), F32)],
        compiler_params=_params(1))(sinks, h16, h16, h16, h16, h16, dya)


def _conv_u(bc_ref, bx_ref, bch_ref, bxh_ref, n, tm):
    u = bc_ref[...] * bx_ref[...]
    uh = bch_ref[...] * bxh_ref[...] * (n > 0).astype(F32)
    rowi = _iota((tm, 256), 0)
    u1 = jnp.where(rowi == 0, uh[7:8, :], pltpu.roll(u, 1, axis=0))
    u2 = jnp.where(rowi == 0, uh[6:7, :], jnp.where(rowi == 1, uh[7:8, :], pltpu.roll(u, 2, axis=0)))
    return u, u1, u2


def _conv_fwd(h32, cw, cb):
    T = h32.shape[0]
    tm = 512 if T % 512 == 0 else T
    hb = tm // 8

    def body(bb_ref, bc_ref, bx_ref, bch_ref, bxh_ref, w_ref, b_ref, o_ref):
        n = pl.program_id(0)
        u, u1, u2 = _conv_u(bc_ref, bx_ref, bch_ref, bxh_ref, n, tm)
        y = w_ref[0:1, :] * u2 + w_ref[1:2, :] * u1 + w_ref[2:3, :] * u + b_ref[...]
        o_ref[...] = bb_ref[...] * y

    halo = lambda c: pl.BlockSpec((8, 256), lambda n: (jnp.maximum(n * hb - 1, 0), c // 256))
    return pl.pallas_call(
        body, name="conv_fwd", grid=(T // tm,),
        in_specs=[pl.BlockSpec((tm, 256), lambda n: (n, C_BB // 256)),
                  pl.BlockSpec((tm, 256), lambda n: (n, C_BC // 256)),
                  pl.BlockSpec((tm, 256), lambda n: (n, C_BX // 256)),
                  halo(C_BC), halo(C_BX),
                  pl.BlockSpec((8, 256), lambda n: (0, 0)),
                  pl.BlockSpec((1, 256), lambda n: (0, 0))],
        out_specs=pl.BlockSpec((tm, 256), lambda n: (n, 0)),
        out_shape=jax.ShapeDtypeStruct((T, 256), F32),
        compiler_params=_params(1))(h32, h32, h32, h32, h32, cw, cb)


def _conv_bwd(h32, cw, cb, dyb):
    T = h32.shape[0]
    tm = 512 if T % 512 == 0 else T
    hb = tm // 8
    nt = T // tm

    def body(bb_ref, bc_ref, bx_ref, bch_ref, bxh_ref, bbn_ref, dy_ref, dyn_ref, w_ref, b_ref,
             dbb_ref, dbc_ref, dbx_ref, dw_ref):
        n = pl.program_id(0)

        @pl.when(n == 0)
        def _():
            dw_ref[...] = jnp.zeros_like(dw_ref)

        u, u1, u2 = _conv_u(bc_ref, bx_ref, bch_ref, bxh_ref, n, tm)
        w0, w1, w2 = w_ref[0:1, :], w_ref[1:2, :], w_ref[2:3, :]
        y = w0 * u2 + w1 * u1 + w2 * u + b_ref[...]
        dyb_ = dy_ref[...]
        dbb_ref[...] = (dyb_ * y).astype(BF)
        dy = dyb_ * bb_ref[...]
        dyn = dyn_ref[...] * bbn_ref[...] * (n < nt - 1).astype(F32)
        rowi = _iota((tm, 256), 0)
        dy1 = jnp.where(rowi == tm - 1, dyn[0:1, :], pltpu.roll(dy, tm - 1, axis=0))
        dy2 = jnp.where(rowi == tm - 2, dyn[0:1, :],
                        jnp.where(rowi == tm - 1, dyn[1:2, :], pltpu.roll(dy, tm - 2, axis=0)))
        du = w2 * dy + w1 * dy1 + w0 * dy2
        dbc_ref[...] = (du * bx_ref[...]).astype(BF)
        dbx_ref[...] = (du * bc_ref[...]).astype(BF)
        dw_ref[0:1, :] += jnp.sum(dy * u2, axis=0, keepdims=True)
        dw_ref[1:2, :] += jnp.sum(dy * u1, axis=0, keepdims=True)
        dw_ref[2:3, :] += jnp.sum(dy * u, axis=0, keepdims=True)
        dw_ref[3:4, :] += jnp.sum(dy, axis=0, keepdims=True)

    halo = lambda c: pl.BlockSpec((8, 256), lambda n: (jnp.maximum(n * hb - 1, 0), c // 256))
    nxt = lambda c: pl.BlockSpec((8, 256), lambda n: (jnp.minimum((n + 1) * hb, T // 8 - 1), c // 256))
    cur = lambda c: pl.BlockSpec((tm, 256), lambda n: (n, c // 256))
    return pl.pallas_call(
        body, name="conv_bwd", grid=(nt,),
        in_specs=[cur(C_BB), cur(C_BC), cur(C_BX), halo(C_BC), halo(C_BX), nxt(C_BB),
                  cur(0), nxt(0),
                  pl.BlockSpec((8, 256), lambda n: (0, 0)),
                  pl.BlockSpec((1, 256), lambda n: (0, 0))],
        out_specs=[cur(0), cur(0), cur(0), pl.BlockSpec((8, 256), lambda n: (0, 0))],
        out_shape=[jax.ShapeDtypeStruct((T, 256), BF)] * 3 + [jax.ShapeDtypeStruct((8, 256), F32)],
        compiler_params=_params(1))(h32, h32, h32, h32, h32, h32, dyb, dyb, cw, cb)


def _cprep_specs(tm):
    return [pl.BlockSpec((tm, 256), lambda n: (n, C_CQ // 256)),
            pl.BlockSpec((tm, 128), lambda n: (n, C_CKV // 128)),
            pl.BlockSpec((tm, 128), lambda n: (n, C_CKR // 128)),
            pl.BlockSpec((tm, 128), lambda n: (n, C_CKRS // 128)),
            pl.BlockSpec((1, 256), lambda n: (0, 0)),
            pl.BlockSpec((1, 128), lambda n: (0, 0)),
            pl.BlockSpec((tm, 128), lambda n: (n, 0)),
            pl.BlockSpec((tm, 128), lambda n: (n, 0))]


def _cprep_fwd(h32, gq, gkv, wuq2, wkv2, cosk, sin):
    T = h32.shape[0]
    tm = 512 if T % 512 == 0 else T

    def body(cq_ref, ckv_ref, ckr_ref, ckrs_ref, gq_ref, gkv_ref, cos_ref, sin_ref, wuq_ref, wkv_ref,
             q_ref, k_ref, v_ref):
        cosk_, sin_ = cos_ref[...], sin_ref[...]
        cosq = cosk_ + (_iota((tm, 128), 1) < 64).astype(F32)
        cqn, _ = _rms_fwd(cq_ref[...], gq_ref[...])
        q2 = _dot(cqn.astype(BF), wuq_ref[...])
        ckvn, _ = _rms_fwd(ckv_ref[...], gkv_ref[...])
        kv2 = _dot(ckvn.astype(BF), wkv_ref[...])
        kr = ckr_ref[...] * cosk_ + ckrs_ref[...] * sin_
        for h in range(4):
            hs = slice(h * 128, (h + 1) * 128)
            q_ref[:, hs] = ((q2[:, hs] * cosq + q2[:, 512 + h * 128:512 + (h + 1) * 128] * sin_) * MLA_SCALE).astype(BF)
            k_ref[:, hs] = (kv2[:, hs] + kr).astype(BF)
        v_ref[...] = kv2[:, 512:].astype(BF)

    return pl.pallas_call(
        body, name="cprep_fwd", grid=(T // tm,),
        in_specs=_cprep_specs(tm) + [pl.BlockSpec((256, 1024), lambda n: (0, 0)),
                                     pl.BlockSpec((128, 1024), lambda n: (0, 0))],
        out_specs=[pl.BlockSpec((tm, 512), lambda n: (n, 0))] * 3,
        out_shape=[jax.ShapeDtypeStruct((T, 512), BF)] * 3,
        compiler_params=_params(1))(h32, h32, h32, h32, gq, gkv, cosk, sin, wuq2, wkv2)


def _cprep_bwd(h32, gq, gkv, wuq2t, wkv2t, cosk, sin, dq, dk, dv):
    T = h32.shape[0]
    tm = 512 if T % 512 == 0 else T

    def body(cq_ref, ckv_ref, ckr_ref, ckrs_ref, gq_ref, gkv_ref, cos_ref, sin_ref, wuq_ref, wkv_ref,
             dq_ref, dk_ref, dv_ref,
             dcq_ref, dckv_ref, dckr_ref, dckrs_ref, dq2_ref, dkv2_ref, cqn_ref, ckvn_ref, dgq_ref, dgkv_ref):
        n = pl.program_id(0)

        @pl.when(n == 0)
        def _():
            dgq_ref[...] = jnp.zeros_like(dgq_ref)
            dgkv_ref[...] = jnp.zeros_like(dgkv_ref)

        cosk_, sin_ = cos_ref[...], sin_ref[...]
        cosq = cosk_ + (_iota((tm, 128), 1) < 64).astype(F32)
        dkr = jnp.zeros((tm, 128), F32)
        for h in range(4):
            hs = slice(h * 128, (h + 1) * 128)
            dqh = dq_ref[:, hs] * MLA_SCALE
            dq2_ref[:, hs] = (dqh * cosq).astype(BF)
            dq2_ref[:, 512 + h * 128:512 + (h + 1) * 128] = (dqh * sin_).astype(BF)
            dkr = dkr + dk_ref[:, hs]
        dkv2_ref[:, :512] = dk_ref[...].astype(BF)
        dkv2_ref[:, 512:] = dv_ref[...].astype(BF)
        dckr_ref[...] = (dkr * cosk_).astype(BF)
        dckrs_ref[...] = (dkr * sin_).astype(BF)

        cq, gq_ = cq_ref[...], gq_ref[...]
        cqn, rq = _rms_fwd(cq, gq_)
        cqn_ref[...] = cqn.astype(BF)
        dcq, dgt = _rms_bwd(cq, gq_, rq, _dot(dq2_ref[...], wuq_ref[...]))
        dcq_ref[...] = dcq.astype(BF)
        dgq_ref[...] += jnp.sum(dgt, axis=0, keepdims=True)

        ckv, gkv_ = ckv_ref[...], gkv_ref[...]
        ckvn, rkv = _rms_fwd(ckv, gkv_)
        ckvn_ref[...] = ckvn.astype(BF)
        dckv, dgt2 = _rms_bwd(ckv, gkv_, rkv, _dot(dkv2_ref[...], wkv_ref[...]))
        dckv_ref[...] = dckv.astype(BF)
        dgkv_ref[...] += jnp.sum(dgt2, axis=0, keepdims=True)

    row = lambda w: pl.BlockSpec((tm, w), lambda n: (n, 0))
    return pl.pallas_call(
        body, name="cprep_bwd", grid=(T // tm,),
        in_specs=_cprep_specs(tm) + [pl.BlockSpec((1024, 256), lambda n: (0, 0)),
                                     pl.BlockSpec((1024, 128), lambda n: (0, 0)),
                                     row(512), row(512), row(512)],
        out_specs=[row(256), row(128), row(128), row(128), row(1024), row(1024), row(256), row(128),
                   pl.BlockSpec((1, 256), lambda n: (0, 0)), pl.BlockSpec((1, 128), lambda n: (0, 0))],
        out_shape=[jax.ShapeDtypeStruct((T, 256), BF), jax.ShapeDtypeStruct((T, 128), BF),
                   jax.ShapeDtypeStruct((T, 128), BF), jax.ShapeDtypeStruct((T, 128), BF),
                   jax.ShapeDtypeStruct((T, 1024), BF), jax.ShapeDtypeStruct((T, 1024), BF),
                   jax.ShapeDtypeStruct((T, 256), BF), jax.ShapeDtypeStruct((T, 128), BF),
                   jax.ShapeDtypeStruct((1, 256), F32), jax.ShapeDtypeStruct((1, 128), F32)],
        compiler_params=_params(1))(h32, h32, h32, h32, gq, gkv, cosk, sin, wuq2t, wkv2t, dq, dk, dv)


MLA_TILE = 512
MLA_HEADS_PER_STEP = 4


def _causal_mask(t):
    return _iota((t, t), 1) <= _iota((t, t), 0)


def _mla_fwd(q, k, v):
    T = q.shape[0]
    tq = MLA_TILE

    def body(q_ref, k_ref, v_ref, o_ref, lse_ref):
        i = pl.program_id(1)
        mask = _causal_mask(tq)
        heads = [slice(128 * h, 128 * h + 128) for h in range(MLA_HEADS_PER_STEP)]
        qs = [q_ref[:, hs] for hs in heads]

        def step(j, carry, masked):
            rows = pl.ds(pl.multiple_of(j * tq, tq), tq)
            out = []
            for hh, hs in enumerate(heads):
                m, l, acc = carry[hh]
                s = _dot_nt(qs[hh], k_ref[rows, hs])
                if masked:
                    s = jnp.where(mask, s, NEG)
                m_new = jnp.maximum(m, jnp.max(s, axis=-1, keepdims=True))
                p = jnp.exp(s - m_new)
                alpha = jnp.exp(m - m_new)
                l = alpha * l + jnp.sum(p, axis=-1, keepdims=True)
                acc = alpha * acc + _dot(p.astype(BF), v_ref[rows, hs])
                out.append((m_new, l, acc))
            return tuple(out)

        init = ((jnp.full((tq, 1), NEG, F32), jnp.zeros((tq, 1), F32), jnp.zeros((tq, 128), F32)),) * len(heads)
        carry = lax.fori_loop(0, i, lambda j, c: step(j, c, False), init)
        carry = step(i, carry, True)
        for hh, hs in enumerate(heads):
            m, l, acc = carry[hh]
            o_ref[:, hs] = acc * (1.0 / l)
            lse_ref[:, hs] = jnp.broadcast_to(m + jnp.log(l), (tq, 128))

    width = 128 * MLA_HEADS_PER_STEP
    blk = pl.BlockSpec((tq, width), lambda h, i: (i, h))
    full = pl.BlockSpec((T, width), lambda h, i: (0, h))
    return pl.pallas_call(
        body, name="mla_fwd", grid=(4 // MLA_HEADS_PER_STEP, T // tq), in_specs=[blk, full, full],
        out_specs=[blk, blk],
        out_shape=[jax.ShapeDtypeStruct((T, 512), F32), jax.ShapeDtypeStruct((T, 512), F32)],
        compiler_params=_params(2))(q, k, v)


def _mla_bwd(q, k, v, o, lse, do):
    T = q.shape[0]
    tq = MLA_TILE

    def body(q_ref, k_ref, v_ref, o_ref, lse_ref, do_ref, dq_ref, dk_ref, dv_ref):
        i = pl.program_id(1)

        @pl.when(i == 0)
        def _():
            dk_ref[...] = jnp.zeros_like(dk_ref)
            dv_ref[...] = jnp.zeros_like(dv_ref)

        qv = q_ref[...]
        do = do_ref[...]
        dob = do.astype(BF)
        dd = jnp.sum(do * o_ref[...], axis=-1, keepdims=True)
        lse_ = lse_ref[:, 0:1]
        mask = _causal_mask(tq)

        def step(j, dq, masked):
            rows = pl.ds(pl.multiple_of(j * tq, tq), tq)
            kj, vj = k_ref[rows, :], v_ref[rows, :]
            s = _dot_nt(qv, kj)
            if masked:
                s = jnp.where(mask, s, NEG)
            p = jnp.exp(s - lse_)
            ds = (p * (_dot_nt(dob, vj) - dd)).astype(BF)
            dk_ref[rows, :] += _dot_tn(ds, qv)
            dv_ref[rows, :] += _dot_tn(p.astype(BF), dob)
            return dq + _dot(ds, kj)

        dq = lax.fori_loop(0, i, lambda j, c: step(j, c, False), jnp.zeros((tq, 128), F32))
        dq_ref[...] = step(i, dq, True)

    blk = pl.BlockSpec((tq, 128), lambda h, i: (i, h))
    full = pl.BlockSpec((T, 128), lambda h, i: (0, h))
    return pl.pallas_call(
        body, name="mla_bwd", grid=(4, T // tq), in_specs=[blk, full, full, blk, blk, blk],
        out_specs=[blk, full, full],
        out_shape=[jax.ShapeDtypeStruct((T, 512), F32)] * 3,
        compiler_params=_params(2))(q, k, v, o, lse, do)


def _sb_tile(qk, rr, strict, masked, upper):
    z2 = qk * (SB_SCALE * LOG2E)
    l1 = jnp.log2(1.0 + jnp.exp2(-jnp.abs(z2)))
    lk = -jnp.maximum(z2, 0.0) - l1
    if masked:
        lk = jnp.where(strict, lk, 0.0)
    after = rr + _dot(lk.astype(BF), upper)
    ll = jnp.minimum(z2, 0.0) - l1
    a = jnp.exp2(ll + after)
    if masked:
        a = jnp.where(strict, a, 0.0)
    return ll, a, jnp.sum(lk, axis=-1, keepdims=True)


SB_TQ, SB_TK = 256, 256
SB_DEAD = -160.0


def _sb_walk(trips, one_step, carry):
    def alive(c):
        t, cr = c
        top = jnp.maximum(jnp.max(cr[0][0]), jnp.max(cr[1][0]))
        return jnp.logical_and(t < trips, top > SB_DEAD)

    def body(c):
        t, cr = c
        return t + 1, one_step(t, cr)

    return lax.while_loop(alive, body, (jnp.int32(0), carry))[1]


def _sb_consts(tq, tk):
    row, col = _iota((tq, tk), 0), _iota((tq, tk), 1)
    strict = [col + d * tk < row for d in range(tq // tk)]
    r2, c2 = _iota((tk, tk), 0), _iota((tk, tk), 1)
    return strict, (r2 > c2).astype(BF), (r2 < c2).astype(BF)


def _sb_fwd(h16):
    T = h16.shape[0]
    tq, tk = SB_TQ, SB_TK
    nd = tq // tk

    def body(q_ref, k_ref, v_ref, o_ref):
        i = pl.program_id(1)
        strict, upper, _ = _sb_consts(tq, tk)
        lane = _iota((tq, 128), 1)
        q2 = q_ref[...]
        qms = [jnp.where(lane < 64, q2, jnp.zeros_like(q2)), jnp.where(lane >= 64, q2, jnp.zeros_like(q2))]

        def step(j, carry, d):
            rows = pl.ds(pl.multiple_of(j * tk, tk), tk)
            kj, vj = k_ref[rows, :], v_ref[rows, :]
            out = []
            for hh in range(2):
                rr, acc = carry[hh]
                _, a, rs = _sb_tile(_dot_nt(qms[hh], kj), rr, None if d is None else strict[d], d is not None, upper)
                out.append((rr + rs, acc + _dot(a.astype(BF), vj)))
            return tuple(out)


        carry = ((jnp.zeros((tq, 1), F32), jnp.zeros((tq, 128), F32)),) * 2
        for d in reversed(range(nd)):
            carry = step(nd * i + d, carry, d)
        carry = _sb_walk(nd * i, lambda t, c: step(nd * i - 1 - t, c, None), carry)
        o_ref[...] = jnp.where(lane < 64, carry[0][1], carry[1][1])

    return pl.pallas_call(
        body, name="sb_fwd", grid=(2, T // tq),
        in_specs=[pl.BlockSpec((tq, 128), lambda p, i: (i, C_DQ // 128 + p)),
                  pl.BlockSpec((T, 128), lambda p, i: (0, C_DK // 128 + p)),
                  pl.BlockSpec((T, 128), lambda p, i: (0, C_DV // 128 + p))],
        out_specs=pl.BlockSpec((tq, 128), lambda p, i: (i, p)),
        out_shape=jax.ShapeDtypeStruct((T, 256), F32),
        compiler_params=_params(2))(h16, h16, h16)


def _sb_bwd(h16, yd, dyd):
    T = h16.shape[0]
    tq, tk = SB_TQ, SB_TK
    nd = tq // tk

    def body(q_ref, k_ref, v_ref, o_ref, do_ref, dq_ref, dk_ref, dv_ref):
        i = pl.program_id(1)

        @pl.when(i == 0)
        def _():
            dk_ref[...] = jnp.zeros_like(dk_ref)
            dv_ref[...] = jnp.zeros_like(dv_ref)

        strict, upper, before = _sb_consts(tq, tk)
        lane = _iota((tq, 128), 1)
        lane_k = _iota((tk, 128), 1)
        q2 = q_ref[...]
        dob2 = do_ref[...].astype(BF)
        doo = dob2.astype(F32) * o_ref[...]
        mines = [lane < 64, lane >= 64]
        qms = [jnp.where(m, q2, jnp.zeros_like(q2)) for m in mines]
        doms = [jnp.where(m, dob2, jnp.zeros_like(dob2)) for m in mines]
        dds = [jnp.sum(jnp.where(m, doo, 0.0), axis=-1, keepdims=True) for m in mines]

        def step(j, carry, d):
            rows = pl.ds(pl.multiple_of(j * tk, tk), tk)
            kj, vj = k_ref[rows, :], v_ref[rows, :]
            out, dks, dvs = [], [], []
            for hh in range(2):
                rr, sg, dq = carry[hh]
                ll, a, rs = _sb_tile(_dot_nt(qms[hh], kj), rr, None if d is None else strict[d], d is not None,
                                     upper)
                ab = a.astype(BF)
                g = _dot_nt(doms[hh], vj) * ab.astype(F32)
                gs = jnp.sum(g, axis=-1, keepdims=True)
                pre = (dds[hh] - sg - gs) + _dot(g.astype(BF), before)
                dz = g - jnp.exp2(ll) * (g + pre)
                if d is not None:
                    dz = jnp.where(strict[d], dz, 0.0)
                dzb = dz.astype(BF)
                dks.append(_dot_tn(dzb, q2))
                dvs.append(_dot_tn(ab, dob2))
                out.append((rr + rs, sg + gs, dq + _dot(dzb, kj)))
            dk_ref[rows, :] += jnp.where(lane_k < 64, dks[0], dks[1]) * SB_SCALE
            dv_ref[rows, :] += jnp.where(lane_k < 64, dvs[0], dvs[1])
            return tuple(out)


        zero = jnp.zeros((tq, 1), F32)
        carry = ((zero, zero, jnp.zeros((tq, 128), F32)),) * 2
        for d in reversed(range(nd)):
            carry = step(nd * i + d, carry, d)
        carry = _sb_walk(nd * i, lambda t, c: step(nd * i - 1 - t, c, None), carry)
        dq_ref[...] = jnp.where(lane < 64, carry[0][2], carry[1][2]) * SB_SCALE

    blk = lambda c: pl.BlockSpec((tq, 128), lambda p, i: (i, c // 128 + p))
    full = lambda c: pl.BlockSpec((T, 128), lambda p, i: (0, c // 128 + p))
    return pl.pallas_call(
        body, name="sb_bwd", grid=(2, T // tq),
        in_specs=[blk(C_DQ), full(C_DK), full(C_DV), blk(0), blk(0)],
        out_specs=[blk(0), full(0), full(0)],
        out_shape=[jax.ShapeDtypeStruct((T, 256), F32)] * 3,
        compiler_params=_params(2))(h16, h16, h16, yd, dyd)


def _compact_c(ycp):
    return jnp.concatenate([ycp[:, h * 128:h * 128 + 64] for h in range(4)], axis=1)


def _post_fwd(ya, yb, ycp, yd, h32, ggrp, wout, gpost, x):
    T = x.shape[0]
    tm = 256

    def body(ya_ref, yb_ref, yc_ref, yd_ref, gate_ref, gg_ref, w_ref, gp_ref, x_ref, xn_ref, ym_ref, o_ref):
        ys = [ya_ref[...], yb_ref[...], _compact_c(yc_ref[...]), yd_ref[...]]
        gate = gate_ref[...]
        sil = gate * (1.0 / (1.0 + jnp.exp(-gate)))
        parts = []
        for gi in range(4):
            ng, _ = _rms_fwd(ys[gi], gg_ref[:, gi * 256:(gi + 1) * 256])
            parts.append(ng * sil[:, gi * 256:(gi + 1) * 256])
        ym = jnp.concatenate(parts, axis=1).astype(BF)
        ym_ref[...] = ym
        o = _dot(ym, w_ref[...])
        o_ref[...] = o
        on, _ = _rms_fwd(o, gp_ref[...])
        xn_ref[...] = x_ref[...] + on

    row = lambda w: pl.BlockSpec((tm, w), lambda n: (n, 0))
    vec = pl.BlockSpec((1, 1024), lambda n: (0, 0))
    return pl.pallas_call(
        body, name="post_fwd", grid=(T // tm,),
        in_specs=[row(256), row(256), row(512), row(256), pl.BlockSpec((tm, 1024), lambda n: (n, C_GATE // 1024)),
                  vec, pl.BlockSpec((1024, 1024), lambda n: (0, 0)), vec, row(1024)],
        out_specs=[row(1024), row(1024), row(1024)],
        out_shape=[jax.ShapeDtypeStruct((T, 1024), F32), jax.ShapeDtypeStruct((T, 1024), BF),
                   jax.ShapeDtypeStruct((T, 1024), F32)],
        compiler_params=_params(1))(ya, yb, ycp, yd, h32, ggrp, wout, gpost, x)


def _post_bwd(dx, o, gpost, woutt, ya, yb, ycp, yd, h32, ggrp):
    T = dx.shape[0]
    tm = 256

    def body(dx_ref, o_ref, gp_ref, w_ref, ya_ref, yb_ref, yc_ref, yd_ref, gate_ref, gg_ref,
             do_ref, dya_ref, dyb_ref, dyc_ref, dyd_ref, dgate_ref, dgp_ref, dgg_ref):
        n = pl.program_id(0)

        @pl.when(n == 0)
        def _():
            dgp_ref[...] = jnp.zeros_like(dgp_ref)
            dgg_ref[...] = jnp.zeros_like(dgg_ref)

        ov, gp = o_ref[...], gp_ref[...]
        _, ro = _rms_fwd(ov, gp)
        do, dgt = _rms_bwd(ov, gp, ro, dx_ref[...])
        dgp_ref[...] += jnp.sum(dgt, axis=0, keepdims=True)
        dob = do.astype(BF)
        do_ref[...] = dob
        dym = _dot(dob, w_ref[...])
        gate = gate_ref[...]
        sg = 1.0 / (1.0 + jnp.exp(-gate))
        sil = gate * sg
        dsil = sg * (1.0 + gate * (1.0 - sg))
        ys = [ya_ref[...], yb_ref[...], _compact_c(yc_ref[...]), yd_ref[...]]
        dys = []
        for gi in range(4):
            gs = slice(gi * 256, (gi + 1) * 256)
            gg = gg_ref[:, gs]
            ng, rg = _rms_fwd(ys[gi], gg)
            dgate_ref[:, gs] = (dym[:, gs] * ng * dsil[:, gs]).astype(BF)
            dy, dgt2 = _rms_bwd(ys[gi], gg, rg, dym[:, gs] * sil[:, gs])
            dgg_ref[:, gs] += jnp.sum(dgt2, axis=0, keepdims=True)
            dys.append(dy)
        dya_ref[...] = dys[0]
        dyb_ref[...] = dys[1]
        dyd_ref[...] = dys[3]
        z64 = jnp.zeros((tm, 64), F32)
        dyc_ref[...] = jnp.concatenate(
            [piece for h in range(4) for piece in (dys[2][:, h * 64:(h + 1) * 64], z64)], axis=1)

    row = lambda w: pl.BlockSpec((tm, w), lambda n: (n, 0))
    vec = pl.BlockSpec((1, 1024), lambda n: (0, 0))
    return pl.pallas_call(
        body, name="post_bwd", grid=(T // tm,),
        in_specs=[row(1024), row(1024), vec, pl.BlockSpec((1024, 1024), lambda n: (0, 0)),
                  row(256), row(256), row(512), row(256),
                  pl.BlockSpec((tm, 1024), lambda n: (n, C_GATE // 1024)), vec],
        out_specs=[row(1024), row(256), row(256), row(512), row(256), row(1024), vec, vec],
        out_shape=[jax.ShapeDtypeStruct((T, 1024), BF), jax.ShapeDtypeStruct((T, 256), F32),
                   jax.ShapeDtypeStruct((T, 256), F32), jax.ShapeDtypeStruct((T, 512), F32),
                   jax.ShapeDtypeStruct((T, 256), F32), jax.ShapeDtypeStruct((T, 1024), BF),
                   jax.ShapeDtypeStruct((1, 1024), F32), jax.ShapeDtypeStruct((1, 1024), F32)],
        compiler_params=_params(1))(dx, o, gpost, woutt, ya, yb, ycp, yd, h32, ggrp)


def _loss_head(y, tgt):
    T = y.shape[0]
    tm = 512 if T % 512 == 0 else T

    def body(y_ref, t_ref, s_ref, dy_ref):
        n = pl.program_id(0)

        @pl.when(n == 0)
        def _():
            s_ref[...] = jnp.zeros_like(s_ref)

        d = y_ref[...] - t_ref[...]
        s_ref[...] += jnp.sum(d * d, axis=0, keepdims=True)
        dy_ref[...] = d * (1.0 / D_MODEL)

    row = pl.BlockSpec((tm, 1024), lambda n: (n, 0))
    return pl.pallas_call(
        body, name="loss_head", grid=(T // tm,), in_specs=[row, row],
        out_specs=[pl.BlockSpec((1, 1024), lambda n: (0, 0)), row],
        out_shape=[jax.ShapeDtypeStruct((1, 1024), F32), jax.ShapeDtypeStruct((T, 1024), F32)],
        compiler_params=_params(1))(y, tgt)


def _swap_rows32(a):
    return jnp.concatenate([a[16:32], a[0:16]], axis=0)


def _pad_w_uq(w):
    z = lambda n: jnp.zeros((w.shape[0], n), w.dtype)
    a = [p for h in range(4) for p in (w[:, 96 * h:96 * h + 96], z(32))]
    b = [p for h in range(4) for p in (z(64), _swap32(w[:, 96 * h + 64:96 * h + 96]), z(32))]
    return jnp.concatenate(a + b, axis=1)


def _unpad_w_uq(d):
    out = []
    for h in range(4):
        out.append(d[:, 128 * h:128 * h + 64])
        out.append(d[:, 128 * h + 64:128 * h + 96] + _swap32(d[:, 512 + 128 * h + 64:512 + 128 * h + 96]))
    return jnp.concatenate(out, axis=1)


def _pad_w_ukv(w):
    z = jnp.zeros((w.shape[0], 64), w.dtype)
    a = [p for h in range(4) for p in (w[:, 128 * h:128 * h + 64], z)]
    b = [p for h in range(4) for p in (w[:, 128 * h + 64:128 * h + 128], z)]
    return jnp.concatenate(a + b, axis=1)


def _unpad_w_ukv(d):
    return jnp.concatenate([p for h in range(4) for p in (d[:, 128 * h:128 * h + 64],
                                                          d[:, 512 + 128 * h:512 + 128 * h + 64])], axis=1)


def _rope_tables(pos):
    freqs = 10000.0 ** (-jnp.arange(16, dtype=F32) / 16)
    ang = pos.astype(F32)[:, None] * freqs
    c, s = jnp.cos(ang), jnp.sin(ang)
    z = lambda n: jnp.zeros((pos.shape[0], n), F32)
    return (jnp.concatenate([z(64), c, c, z(32)], axis=1), jnp.concatenate([z(64), -s, s, z(32)], axis=1))


def _layer_weights(W, l):
    wuq2 = _pad_w_uq(W["mla_w_uq"][l])
    wkv2 = _pad_w_ukv(W["mla_w_ukv"][l])
    wout = W["w_out"][l]
    cw = jnp.concatenate([W["conv_w"][l].astype(F32), jnp.zeros((5, 256), F32)], axis=0)
    return dict(
        wpt=W["wpt"][l], wuq2=wuq2.astype(BF), wuq2t=wuq2.T.astype(BF),
        wkv2=wkv2.astype(BF), wkv2t=wkv2.T.astype(BF), wout=wout.astype(BF), woutt=wout.T.astype(BF),
        cw=cw, cb=W["conv_b"][l][None, :], sinks=W["attn_sinks"][l],
        gpre=W["norm_pre"][l][None, :], gq=W["mla_q_norm"][l][None, :], gkv=W["mla_kv_norm"][l][None, :],
        ggrp=W["group_norm"][l][None, :], gpost=W["norm_post"][l][None, :])


def _local_step(x, pos, W, tgt):
    cosk, sin = _rope_tables(pos)
    saved = []
    for l in range(DEPTH):
        lw = _layer_weights(W, l)
        h32, h16, xn = _inproj_fwd(x, lw["gpre"], lw["wpt"])
        ya = _swa_fwd(h16, lw["sinks"])
        yb = _conv_fwd(h32, lw["cw"], lw["cb"])
        qc, kc, vc = _cprep_fwd(h32, lw["gq"], lw["gkv"], lw["wuq2"], lw["wkv2"], cosk, sin)
        ycp, lse = _mla_fwd(qc, kc, vc)
        yd = _sb_fwd(h16)
        x_new, ym, o = _post_fwd(ya, yb, ycp, yd, h32, lw["ggrp"], lw["wout"], lw["gpost"], x)
        saved.append(dict(lw=lw, x=x, h32=h32, h16=h16, xn=xn, ya=ya, yb=yb, qc=qc, kc=kc, vc=vc, ycp=ycp,
                          lse=lse, yd=yd, ym=ym, o=o))
        x = x_new
    sq, dx = _loss_head(x, tgt)

    grads = {k: [None] * DEPTH for k in ("norm_pre", "w_in_pt", "attn_sinks", "conv_w", "conv_b", "mla_q_norm",
                                         "mla_w_uq", "mla_kv_norm", "mla_w_ukv", "group_norm", "w_out",
                                         "norm_post")}
    for l in reversed(range(DEPTH)):
        s = saved[l]
        lw = s["lw"]
        do, dya, dyb, dycp, dyd, dgate, dgpost, dggrp = _post_bwd(
            dx, s["o"], lw["gpost"], lw["woutt"], s["ya"], s["yb"], s["ycp"], s["yd"], s["h32"], lw["ggrp"])
        grads["norm_post"][l] = dgpost[0]
        grads["group_norm"][l] = dggrp[0]
        grads["w_out"][l] = _matmul_tn(s["ym"], do, 512, "dw_out")
        sdq, sdk, sdv = _sb_bwd(s["h16"], s["yd"], dyd)
        mdq, mdk, mdv = _mla_bwd(s["qc"], s["kc"], s["vc"], s["ycp"], s["lse"], dycp)
        (dcq, dckv, dckr, dckrs, dq2, dkv2, cqn, ckvn, dgq, dgkv) = _cprep_bwd(
            s["h32"], lw["gq"], lw["gkv"], lw["wuq2t"], lw["wkv2t"], cosk, sin, mdq, mdk, mdv)
        grads["mla_q_norm"][l] = dgq[0]
        grads["mla_kv_norm"][l] = dgkv[0]
        grads["mla_w_uq"][l] = _unpad_w_uq(_matmul_tn(cqn, dq2, 512, "dw_uq"))
        grads["mla_w_ukv"][l] = _unpad_w_ukv(_matmul_tn(ckvn, dkv2, 512, "dw_ukv"))
        dbb, dbc, dbx, dcw = _conv_bwd(s["h32"], lw["cw"], lw["cb"], dyb)
        grads["conv_w"][l] = dcw[0:3]
        grads["conv_b"][l] = dcw[3]
        adq, adk, adv, dsk = _swa_bwd(s["h16"], lw["sinks"], dya)
        grads["attn_sinks"][l] = dsk[0, 0:4]
        parts = [dgate, adq, adk, adv, dbb, dbc, dbx, dcq, dckv, dckr, dckrs, sdq, sdk, sdv]
        dh, dx, dgpre = _inproj_bwd_x(parts, lw["wpt"], s["x"], lw["gpre"], dx)
        grads["w_in_pt"][l] = _matmul_tn(s["xn"], dh, 1280, "dw_in").T
        grads["norm_pre"][l] = dgpre[0]
    return sq, dx, grads


SMALL_SHARDED = ("conv_w", "mla_w_uq", "mla_w_ukv")
REPLICATED = ("norm_pre", "attn_sinks", "conv_b", "mla_q_norm", "mla_kv_norm", "group_norm", "norm_post")
ORDER = ("norm_pre", "w_in", "attn_sinks", "conv_w", "conv_b", "mla_q_norm", "mla_w_uq", "mla_kv_norm",
         "mla_w_ukv", "group_norm", "w_out", "norm_post")
W_IN_COLS = 436
W_IN_WIN = 440
SMALL_ROWS = 48


def _pack_small(arrs, dtype):
    flat = jnp.concatenate([a.reshape(-1).astype(dtype) for a in arrs])
    flat = jnp.concatenate([flat, jnp.zeros((SMALL_ROWS * D_MODEL - flat.shape[0],), dtype)])
    return flat.reshape(SMALL_ROWS, D_MODEL)


def _pack_state(ps, c):
    k = len(ps)
    wt = jnp.transpose(jnp.stack([p["w_in"] for p in ps]), (0, 1, 3, 2))
    win = lax.dynamic_update_slice(jnp.zeros((k, DEPTH, W_IN_WIN, D_MODEL), F32), wt, (0, 0, 4 * c, 0))
    wout = jnp.stack([p["w_out"] for p in ps]).reshape(k, DEPTH * 128, D_MODEL)
    flat = jnp.stack([jnp.concatenate([p[n].reshape(-1) for n in SMALL_SHARDED + REPLICATED]) for p in ps])
    small = jnp.pad(flat, ((0, 0), (0, SMALL_ROWS * D_MODEL - flat.shape[1]))).reshape(k, SMALL_ROWS, D_MODEL)
    return jnp.concatenate([win.reshape(k, DEPTH * W_IN_WIN, D_MODEL), wout, small], axis=1)


def _unpack_state(buf, p, c):
    k = buf.shape[0]
    nw = DEPTH * W_IN_WIN
    win = lax.dynamic_slice(buf[:, 0:nw].reshape(k, DEPTH, W_IN_WIN, D_MODEL), (0, 0, 4 * c, 0),
                            (k, DEPTH, W_IN_COLS, D_MODEL))
    out = {"w_in": jnp.transpose(win, (0, 1, 3, 2)),
           "w_out": buf[:, nw:nw + DEPTH * 128].reshape(k, DEPTH, 128, D_MODEL)}
    flat = buf[:, nw + DEPTH * 128:].reshape(k, SMALL_ROWS * D_MODEL)
    off = 0
    for n in SMALL_SHARDED + REPLICATED:
        size = int(np.prod(p[n].shape))
        out[n] = flat[:, off:off + size].reshape((k,) + p[n].shape)
        off += size
    return out


def _rows_of_w_in_t(lo, hi, padded, kr):
    segs = ((0, 1664, padded, C_AQ), (1664, 1696, kr, 0), (1696, 2464, padded, C_DQ), (2464, 3488, padded, C_GATE))
    out = []
    for s0, s1, src, base in segs:
        a, b = max(lo, s0), min(hi, s1)
        if a < b:
            out.append(src[base + a - s0:base + b - s0])
    return out


def _me():
    return lax.axis_index("x"), lax.axis_index("y"), lax.axis_index("c")


def _all_gather(block):
    R, C = block.shape

    def body(src_ref, out_ref, send_sems, recv_sems, local_sem):
        x, y, c = _me()
        me, sibling = (x, y, c), (x, y, 1 - c)
        chips = [(1 - x, y), (x, 1 - y), (1 - x, 1 - y)]

        def slot(px, py, pc):
            return out_ref.at[4 * px + 2 * py + pc]

        def copy(k, block, to, src=None):
            return pltpu.make_async_remote_copy(
                src_ref=slot(*block) if src is None else src, dst_ref=slot(*block), send_sem=send_sems.at[k],
                recv_sem=recv_sems.at[k], device_id=to, device_id_type=MESH)

        mine = pltpu.make_async_copy(src_ref, slot(*me), local_sem)
        mine.start()
        first = [copy(0, me, sibling, src=src_ref)]
        first += [copy(1 + j, me, (*chip, c), src=src_ref) for j, chip in enumerate(chips)]
        for cp in first:
            cp.start()
        passed = [copy(4 + j, (*chip, c), sibling) for j, chip in enumerate(chips)]
        for j, chip in enumerate(chips):
            copy(1 + j, (*chip, c), me).wait_recv()
            passed[j].start()
        copy(0, sibling, me).wait_recv()
        for j, chip in enumerate(chips):
            copy(4 + j, (*chip, 1 - c), me).wait_recv()
        for cp in first + passed:
            cp.wait_send()
        mine.wait()

    return pl.pallas_call(
        body, name="all_gather", out_shape=jax.ShapeDtypeStruct((N_DEV, R, C), block.dtype),
        in_specs=[pl.BlockSpec(memory_space=pl.ANY)], out_specs=pl.BlockSpec(memory_space=pl.ANY),
        scratch_shapes=[pltpu.SemaphoreType.DMA((N_DEV - 1,)), pltpu.SemaphoreType.DMA((N_DEV - 1,)),
                        pltpu.SemaphoreType.DMA])(block)


N_CHIP = 4


def _sibling_swap(blocks):
    _, R, C = blocks.shape

    def body(src_ref, out_ref, send_sems, recv_sems):
        x, y, c = _me()
        copies = [pltpu.make_async_remote_copy(
            src_ref=src_ref.at[2 * j + 1 - c], dst_ref=out_ref.at[j], send_sem=send_sems.at[j],
            recv_sem=recv_sems.at[j], device_id=(x, y, 1 - c), device_id_type=MESH) for j in range(N_CHIP)]
        for cp in copies:
            cp.start()
        for cp in copies:
            cp.wait()

    return pl.pallas_call(
        body, name="sibling_swap", out_shape=jax.ShapeDtypeStruct((N_CHIP, R, C), blocks.dtype),
        in_specs=[pl.BlockSpec(memory_space=pl.ANY)], out_specs=pl.BlockSpec(memory_space=pl.ANY),
        scratch_shapes=[pltpu.SemaphoreType.DMA((N_CHIP,)), pltpu.SemaphoreType.DMA((N_CHIP,))])(blocks)


def _pair_sum(a, b):
    n, R, C = a.shape
    tr = 592 if R % 592 == 0 else R

    def body(a_ref, b_ref, o_ref):
        o_ref[...] = (a_ref[...].astype(F32) + b_ref[...].astype(F32)).astype(BF)

    spec = pl.BlockSpec((1, tr, C), lambda j, r: (j, r, 0))
    return pl.pallas_call(body, name="pair_sum", grid=(n, R // tr), in_specs=[spec, spec], out_specs=spec,
                          out_shape=jax.ShapeDtypeStruct(a.shape, BF), compiler_params=_params(2))(a, b)


def _chip_exchange(sums):
    _, R, C = sums.shape

    def body(src_ref, out_ref, send_sems, recv_sems, local_sem):
        x, y, c = _me()
        here = 2 * x + y
        mine = pltpu.make_async_copy(src_ref.at[here], out_ref.at[here], local_sem)
        mine.start()
        copies = []
        for k in range(1, N_CHIP):
            px, py = x ^ (k >> 1), y ^ (k & 1)
            copies.append(pltpu.make_async_remote_copy(
                src_ref=src_ref.at[2 * px + py], dst_ref=out_ref.at[here], send_sem=send_sems.at[k - 1],
                recv_sem=recv_sems.at[k - 1], device_id=(px, py, c), device_id_type=MESH))
        for cp in copies:
            cp.start()
        for cp in copies:
            cp.wait()
        mine.wait()

    return pl.pallas_call(
        body, name="chip_exchange", out_shape=jax.ShapeDtypeStruct((N_CHIP, R, C), sums.dtype),
        in_specs=[pl.BlockSpec(memory_space=pl.ANY)], out_specs=pl.BlockSpec(memory_space=pl.ANY),
        scratch_shapes=[pltpu.SemaphoreType.DMA((N_CHIP - 1,)), pltpu.SemaphoreType.DMA((N_CHIP - 1,)),
                        pltpu.SemaphoreType.DMA])(sums)


def _adamw(parts, state):
    _, R, C = state.shape
    n_parts = parts.shape[0]
    tr = 32
    assert R % tr == 0

    def body(p_ref, s_ref, o_ref):
        g = p_ref[0].astype(F32)
        for k in range(1, n_parts):
            g = g + p_ref[k].astype(F32)
        o_ref[0] = g
        m_ = ADAM_B1 * s_ref[1] + (1.0 - ADAM_B1) * g
        v_ = ADAM_B2 * s_ref[2] + (1.0 - ADAM_B2) * (g * g)
        o_ref[2] = m_
        o_ref[3] = v_
        m_hat = m_ / (1.0 - ADAM_B1 ** ADAM_STEP)
        v_hat = v_ / (1.0 - ADAM_B2 ** ADAM_STEP)
        o_ref[1] = -ADAM_LR * (m_hat / (jnp.sqrt(v_hat) + ADAM_EPS) + ADAM_WD * s_ref[0])

    return pl.pallas_call(
        body, name="adamw", grid=(R // tr,),
        in_specs=[pl.BlockSpec((n_parts, tr, C), lambda n: (0, n, 0)), pl.BlockSpec((3, tr, C), lambda n: (0, n, 0))],
        out_specs=pl.BlockSpec((4, tr, C), lambda n: (0, n, 0)), out_shape=jax.ShapeDtypeStruct((4, R, C), F32),
        compiler_params=_params(1))(parts, state)


def kernel(x, positions, norm_pre, w_in, attn_sinks, conv_w, conv_b, mla_q_norm, mla_w_uq, mla_kv_norm, mla_w_ukv, group_norm, w_out, norm_post, loss_target, m_norm_pre, m_w_in, m_attn_sinks, m_conv_w, m_conv_b, m_mla_q_norm, m_mla_w_uq, m_mla_kv_norm, m_mla_w_ukv, m_group_norm, m_w_out, m_norm_post, v_norm_pre, v_w_in, v_attn_sinks, v_conv_w, v_conv_b, v_mla_q_norm, v_mla_w_uq, v_mla_kv_norm, v_mla_w_ukv, v_group_norm, v_w_out, v_norm_post):
    local = dict(norm_pre=norm_pre, w_in=w_in, attn_sinks=attn_sinks, conv_w=conv_w, conv_b=conv_b,
                 mla_q_norm=mla_q_norm, mla_w_uq=mla_w_uq, mla_kv_norm=mla_kv_norm, mla_w_ukv=mla_w_ukv,
                 group_norm=group_norm, w_out=w_out, norm_post=norm_post)
    mom = dict(norm_pre=m_norm_pre, w_in=m_w_in, attn_sinks=m_attn_sinks, conv_w=m_conv_w, conv_b=m_conv_b,
               mla_q_norm=m_mla_q_norm, mla_w_uq=m_mla_w_uq, mla_kv_norm=m_mla_kv_norm, mla_w_ukv=m_mla_w_ukv,
               group_norm=m_group_norm, w_out=m_w_out, norm_post=m_norm_post)
    vel = dict(norm_pre=v_norm_pre, w_in=v_w_in, attn_sinks=v_attn_sinks, conv_w=v_conv_w, conv_b=v_conv_b,
               mla_q_norm=v_mla_q_norm, mla_w_uq=v_mla_w_uq, mla_kv_norm=v_mla_kv_norm, mla_w_ukv=v_mla_w_ukv,
               group_norm=v_group_norm, w_out=v_w_out, norm_post=v_norm_post)

    c = lax.axis_index("c")

    wt = jnp.transpose(w_in, (0, 2, 1)).astype(BF)
    wt = jnp.concatenate([wt, jnp.zeros((DEPTH, 448 - W_IN_COLS, D_MODEL), BF)], axis=1)
    payload = jnp.concatenate([wt.reshape(DEPTH * 448, D_MODEL), w_out.astype(BF).reshape(DEPTH * 128, D_MODEL),
                               _pack_small([local[n] for n in SMALL_SHARDED], BF)], axis=0)
    gathered = _all_gather(payload)
    W = {n: local[n] for n in REPLICATED}

    def nat_rows(l, lo, hi):
        out, r = [], lo
        while r < hi:
            d = r // W_IN_COLS
            e = min(hi, (d + 1) * W_IN_COLS)
            out.append(gathered[d, 448 * l + r - W_IN_COLS * d:448 * l + e - W_IN_COLS * d])
            r = e
        return out

    z = lambda n: [jnp.zeros((n, D_MODEL), BF)]
    W["wpt"] = [jnp.concatenate(nat_rows(l, 2464, 3488) + nat_rows(l, 0, 1664) + z(64) + nat_rows(l, 1664, 1696)
                                + z(96) + nat_rows(l, 1680, 1696) + nat_rows(l, 1664, 1680) + z(32)
                                + nat_rows(l, 1696, 2464) + z(NP - C_END), axis=0) for l in range(DEPTH)]
    wo0 = DEPTH * 448
    W["w_out"] = gathered[:, wo0:wo0 + DEPTH * 128].reshape(N_DEV, DEPTH, 128, D_MODEL).transpose(1, 0, 2, 3).reshape(
        DEPTH, D_MODEL, D_MODEL)
    flat = gathered[:, wo0 + DEPTH * 128:].reshape(N_DEV, SMALL_ROWS * D_MODEL)
    off = 0
    for n in SMALL_SHARDED:
        depth, rows, width = local[n].shape
        size = depth * rows * width
        W[n] = flat[:, off:off + size].reshape(N_DEV, depth, rows, width).transpose(1, 2, 0, 3).reshape(
            depth, rows, N_DEV * width)
        off += size

    sq, grad_x, g = _local_step(x[0], positions[0], W, loss_target[0])
    loss = lax.psum(0.5 / D_MODEL * jnp.sum(sq), ("x", "y", "c"))

    cols = []
    for n in SMALL_SHARDED:
        depth, rows, width = local[n].shape
        cols.append(jnp.stack(g[n]).reshape(depth, rows, N_DEV, width).transpose(2, 0, 1, 3).reshape(N_DEV, -1))
    rep = jnp.concatenate([a.reshape(-1) for n in REPLICATED for a in g[n]])
    cols.append(jnp.broadcast_to(rep[None], (N_DEV, rep.shape[0])))
    small = jnp.concatenate(cols, axis=1)
    small = jnp.pad(small, ((0, 0), (0, SMALL_ROWS * D_MODEL - small.shape[1]))).reshape(N_DEV, SMALL_ROWS, D_MODEL)
    krs = [p[C_CKR + 64:C_CKR + 96] + _swap_rows32(p[C_CKRS + 64:C_CKRS + 96]) for p in g["w_in_pt"]]
    pieces = []
    for d in range(N_DEV):
        lo = W_IN_COLS * d // 8 * 8
        for l in range(DEPTH):
            pieces += _rows_of_w_in_t(lo, lo + W_IN_WIN, g["w_in_pt"][l], krs[l])
        pieces += [g["w_out"][l][128 * d:128 * (d + 1)] for l in range(DEPTH)]
        pieces.append(small[d])
    blocks = jnp.concatenate(pieces, axis=0).astype(BF).reshape(N_DEV, -1, D_MODEL)
    mine = lax.dynamic_index_in_dim(blocks.reshape(N_CHIP, 2, -1, D_MODEL), c, axis=1, keepdims=False)
    received = _chip_exchange(_pair_sum(mine, _sibling_swap(blocks)))

    out = _unpack_state(_adamw(received, _pack_state([local, mom, vel], c)), local, c)
    return (loss, grad_x[None], *[out[n][t] for t in range(4) for n in ORDER])
```

```python
import functools

import jax
import jax.numpy as jnp
import numpy as np
from jax import lax
from jax.experimental import pallas as pl
from jax.experimental.pallas import tpu as pltpu

F32 = jnp.float32
BF = jnp.bfloat16
MESH = pl.DeviceIdType.MESH

D_MODEL = 1024
DEPTH = 2
EPS = 1e-6
N_DEV = 8
VMEM_LIMIT = 56 * 1024 * 1024
NEG = -1e30
MLA_SCALE = 96.0 ** -0.5
SB_SCALE = 0.125
LOG2E = 1.4426950408889634

NP = 3840
C_GATE = 0
C_AQ = 1024
C_AK = 1280
C_AV = 1408
C_BB = 1536
C_BC = 1792
C_BX = 2048
C_CQ = 2304
C_CKV = 2560
C_CKR = 2688
C_CKRS = 2816
C_DQ = 2944
C_DK = 3200
C_DV = 3456
C_END = 3712

def _swap32(a):
    return jnp.concatenate([a[:, 16:32], a[:, 0:16]], axis=1)

ADAM_LR, ADAM_B1, ADAM_B2, ADAM_EPS, ADAM_WD, ADAM_STEP = 0.001, 0.9, 0.999, 1e-08, 0.01, 10


def _dot(a, b):
    return jnp.dot(a, b, preferred_element_type=F32)


def _dot_nt(a, b):
    return lax.dot_general(a, b, (((1,), (1,)), ((), ())), preferred_element_type=F32)


def _dot_tn(a, b):
    return lax.dot_general(a, b, (((0,), (0,)), ((), ())), preferred_element_type=F32)


def _params(n_grid):
    return pltpu.CompilerParams(dimension_semantics=("arbitrary",) * n_grid, vmem_limit_bytes=VMEM_LIMIT)


def _rms_fwd(x, g):
    r = lax.rsqrt(jnp.mean(x * x, axis=-1, keepdims=True) + EPS)
    return (x * r) * g, r


def _rms_bwd(x, g, r, dy, width=None):
    n = x.shape[-1] if width is None else width
    u = dy * g
    dx = r * u - x * (r * r * r) * (jnp.sum(x * u, axis=-1, keepdims=True) / n)
    return dx, dy * (x * r)


def _iota(shape, axis):
    return lax.broadcasted_iota(jnp.int32, shape, axis)


def _inproj_fwd(x, g, wpt):
    T = x.shape[0]
    tm = 256

    def body(x_ref, g_ref, w_ref, h32_ref, h16_ref, xn_ref):
        xn, _ = _rms_fwd(x_ref[...], g_ref[...])
        xn = xn.astype(BF)
        xn_ref[...] = xn
        h = _dot_nt(xn, w_ref[...])
        h32_ref[...] = h
        h16_ref[...] = h.astype(BF)

    return pl.pallas_call(
        body, name="inproj_fwd", grid=(T // tm,),
        in_specs=[pl.BlockSpec((tm, D_MODEL), lambda n: (n, 0)),
                  pl.BlockSpec((1, D_MODEL), lambda n: (0, 0)),
                  pl.BlockSpec((NP, D_MODEL), lambda n: (0, 0))],
        out_specs=[pl.BlockSpec((tm, NP), lambda n: (n, 0)),
                   pl.BlockSpec((tm, NP), lambda n: (n, 0)),
                   pl.BlockSpec((tm, D_MODEL), lambda n: (n, 0))],
        out_shape=[jax.ShapeDtypeStruct((T, NP), F32), jax.ShapeDtypeStruct((T, NP), BF),
                   jax.ShapeDtypeStruct((T, D_MODEL), BF)],
        compiler_params=_params(1))(x, g, wpt)


def _inproj_bwd_x(parts, wpt, x, g, dxo):
    T = x.shape[0]
    tm = 256
    np_ = len(parts)
    assert sum(p.shape[1] for p in parts) == C_END

    def body(*refs):
        part_refs = refs[:np_]
        w_ref, x_ref, g_ref, dxo_ref, dh_ref, dx_ref, dg_ref = refs[np_:]
        n = pl.program_id(0)
        dh = jnp.concatenate([r[...].astype(BF) for r in part_refs] + [jnp.zeros((tm, NP - C_END), BF)], axis=1)
        dh_ref[...] = dh
        dxn = _dot(dh, w_ref[...])
        xv = x_ref[...]
        _, r = _rms_fwd(xv, g_ref[...])
        dx, dgt = _rms_bwd(xv, g_ref[...], r, dxn)
        dx_ref[...] = dxo_ref[...] + dx

        @pl.when(n == 0)
        def _():
            dg_ref[...] = jnp.zeros_like(dg_ref)

        dg_ref[...] += jnp.sum(dgt, axis=0, keepdims=True)

    return pl.pallas_call(
        body, name="inproj_bwd_x", grid=(T // tm,),
        in_specs=[pl.BlockSpec((tm, p.shape[1]), lambda n: (n, 0)) for p in parts]
        + [pl.BlockSpec((NP, D_MODEL), lambda n: (0, 0)),
           pl.BlockSpec((tm, D_MODEL), lambda n: (n, 0)),
           pl.BlockSpec((1, D_MODEL), lambda n: (0, 0)),
           pl.BlockSpec((tm, D_MODEL), lambda n: (n, 0))],
        out_specs=[pl.BlockSpec((tm, NP), lambda n: (n, 0)),
                   pl.BlockSpec((tm, D_MODEL), lambda n: (n, 0)),
                   pl.BlockSpec((1, D_MODEL), lambda n: (0, 0))],
        out_shape=[jax.ShapeDtypeStruct((T, NP), BF), jax.ShapeDtypeStruct((T, D_MODEL), F32),
                   jax.ShapeDtypeStruct((1, D_MODEL), F32)],
        compiler_params=_params(1))(*parts, wpt, x, g, dxo)


def _matmul_tn(a, b, tn, name):
    T, M = a.shape
    N = b.shape[1]
    tk = 512 if T % 512 == 0 else T

    def body(a_ref, b_ref, o_ref):
        k = pl.program_id(1)

        @pl.when(k == 0)
        def _():
            o_ref[...] = jnp.zeros_like(o_ref)

        o_ref[...] += _dot_tn(a_ref[...], b_ref[...])

    return pl.pallas_call(
        body, name=name, grid=(N // tn, T // tk),
        in_specs=[pl.BlockSpec((tk, M), lambda j, k: (k, 0)),
                  pl.BlockSpec((tk, tn), lambda j, k: (k, j))],
        out_specs=pl.BlockSpec((M, tn), lambda j, k: (0, j)),
        out_shape=jax.ShapeDtypeStruct((M, N), F32),
        compiler_params=_params(2))(a, b)


SWA_BLK = 128
SWA_TQ = 1024


def _bdot_nt(a, b):
    return lax.dot_general(a, b, (((2,), (2,)), ((0,), (0,))), preferred_element_type=F32)


def _bdot(a, b):
    return lax.dot_general(a, b, (((2,), (1,)), ((0,), (0,))), preferred_element_type=F32)


def _bdot_tn(a, b):
    return lax.dot_general(a, b, (((1,), (1,)), ((0,), (0,))), preferred_element_type=F32)


def _swa_probs(q, kc, kp, sink, mask_c, mask_p):
    sc = jnp.where(mask_c, _bdot_nt(q, kc) * SB_SCALE, NEG)
    sp = jnp.where(mask_p, _bdot_nt(q, kp) * SB_SCALE, NEG)
    m = jnp.maximum(jnp.maximum(jnp.max(sc, axis=-1, keepdims=True), jnp.max(sp, axis=-1, keepdims=True)), sink)
    pc = jnp.exp(sc - m)
    pp = jnp.exp(sp - m)
    ps = jnp.exp(sink - m)
    inv = 1.0 / (jnp.sum(pc, axis=-1, keepdims=True) + jnp.sum(pp, axis=-1, keepdims=True) + ps)
    return pc * inv, pp * inv, ps * inv


def _swa_masks(n, nb):
    blk = _iota((nb, SWA_BLK, SWA_BLK), 0)
    row = _iota((nb, SWA_BLK, SWA_BLK), 1)
    col = _iota((nb, SWA_BLK, SWA_BLK), 2)
    return col <= row, jnp.logical_and(col > row, jnp.logical_or(blk > 0, n > 0))


def _swa_specs(tq):
    halo = tq // SWA_BLK
    return [pl.BlockSpec(memory_space=pltpu.SMEM),
            pl.BlockSpec((tq, 256), lambda n: (n, C_AQ // 256)),
            pl.BlockSpec((tq, 128), lambda n: (n, C_AK // 128)),
            pl.BlockSpec((SWA_BLK, 128), lambda n: (jnp.maximum(n * halo - 1, 0), C_AK // 128)),
            pl.BlockSpec((tq, 128), lambda n: (n, C_AV // 128)),
            pl.BlockSpec((SWA_BLK, 128), lambda n: (jnp.maximum(n * halo - 1, 0), C_AV // 128))]


def _swa_blocked(cur_ref, prev_ref, gs, nb):
    cur = cur_ref[:, gs].reshape(nb, SWA_BLK, 64)
    prev = jnp.concatenate([prev_ref[:, gs].reshape(1, SWA_BLK, 64), cur[:nb - 1]], axis=0) if nb > 1 \
        else prev_ref[:, gs].reshape(1, SWA_BLK, 64)
    return cur, prev


def _swa_fwd(h16, sinks):
    T = h16.shape[0]
    tq = SWA_TQ if T % SWA_TQ == 0 else SWA_BLK
    nb = tq // SWA_BLK

    def body(s_ref, q_ref, kc_ref, kp_ref, vc_ref, vp_ref, o_ref):
        n = pl.program_id(0)
        mask_c, mask_p = _swa_masks(n, nb)
        for h in range(4):
            hs = slice(h * 64, (h + 1) * 64)
            gs = slice(h // 2 * 64, (h // 2 + 1) * 64)
            kc, kp = _swa_blocked(kc_ref, kp_ref, gs, nb)
            vc, vp = _swa_blocked(vc_ref, vp_ref, gs, nb)
            pc, pp, _ = _swa_probs(q_ref[:, hs].reshape(nb, SWA_BLK, 64), kc, kp, s_ref[h], mask_c, mask_p)
            o_ref[:, hs] = (_bdot(pc.astype(BF), vc) + _bdot(pp.astype(BF), vp)).reshape(tq, 64)

    return pl.pallas_call(
        body, name="swa_fwd", grid=(T // tq,), in_specs=_swa_specs(tq),
        out_specs=pl.BlockSpec((tq, 256), lambda n: (n, 0)),
        out_shape=jax.ShapeDtypeStruct((T, 256), F32),
        compiler_params=_params(1))(sinks, h16, h16, h16, h16, h16)


def _swa_bwd(h16, sinks, dya):
    T = h16.shape[0]
    tq = SWA_TQ if T % SWA_TQ == 0 else SWA_BLK
    nb = tq // SWA_BLK

    def body(s_ref, q_ref, kc_ref, kp_ref, vc_ref, vp_ref, do_ref, dq_ref, dk_ref, dv_ref, ds_ref):
        n = pl.program_id(0)

        @pl.when(n == 0)
        def _():
            dk_ref[...] = jnp.zeros_like(dk_ref)
            dv_ref[...] = jnp.zeros_like(dv_ref)
            ds_ref[...] = jnp.zeros_like(ds_ref)

        mask_c, mask_p = _swa_masks(n, nb)
        rows = pl.ds(pl.multiple_of(n * tq, tq), tq)
        before = pl.ds(pl.multiple_of(jnp.maximum(n * nb - 1, 0) * SWA_BLK, SWA_BLK), SWA_BLK)
        lane = _iota((8, 128), 1)
        row8 = _iota((8, 128), 0)

        def to_keys(own, prev):
            if nb == 1:
                return own
            return own + jnp.concatenate([prev[1:], jnp.zeros((1, SWA_BLK, 64), F32)], axis=0)

        for h in range(4):
            hs = slice(h * 64, (h + 1) * 64)
            gs = slice(h // 2 * 64, (h // 2 + 1) * 64)
            q = q_ref[:, hs].reshape(nb, SWA_BLK, 64)
            kc, kp = _swa_blocked(kc_ref, kp_ref, gs, nb)
            vc, vp = _swa_blocked(vc_ref, vp_ref, gs, nb)
            pc, pp, ps = _swa_probs(q, kc, kp, s_ref[h], mask_c, mask_p)
            pcb, ppb = pc.astype(BF), pp.astype(BF)
            do = do_ref[:, hs].reshape(nb, SWA_BLK, 64)
            dob = do.astype(BF)
            o = _bdot(pcb, vc) + _bdot(ppb, vp)
            dd = jnp.sum(do * o, axis=-1, keepdims=True)
            dsc = (pc * (_bdot_nt(dob, vc) - dd) * SB_SCALE).astype(BF)
            dsp = (pp * (_bdot_nt(dob, vp) - dd) * SB_SCALE).astype(BF)
            dq_ref[:, hs] = (_bdot(dsc, kc) + _bdot(dsp, kp)).reshape(tq, 64).astype(BF)
            dkp, dvp = _bdot_tn(dsp, q), _bdot_tn(ppb, dob)
            dk_ref[rows, gs] += to_keys(_bdot_tn(dsc, q), dkp).reshape(tq, 64)
            dv_ref[rows, gs] += to_keys(_bdot_tn(pcb, dob), dvp).reshape(tq, 64)
            dk_ref[before, gs] += dkp[0]
            dv_ref[before, gs] += dvp[0]
            ds_ref[...] += jnp.where(jnp.logical_and(lane == h, row8 == 0), -jnp.sum(ps * dd), 0.0)

    return pl.pallas_call(
        body, name="swa_bwd", grid=(T // tq,),
        in_specs=_swa_specs(tq) + [pl.BlockSpec((tq, 256), lambda n: (n, 0))],
        out_specs=[pl.BlockSpec((tq, 256), lambda n: (n, 0)),
                   pl.BlockSpec((T, 128), lambda n: (0, 0)),
                   pl.BlockSpec((T, 128), lambda n: (0, 0)),
                   pl.BlockSpec((8, 128), lambda n: (0, 0))],
        out_shape=[jax.ShapeDtypeStruct((T, 256), BF), jax.ShapeDtypeStruct((T, 128), F32),
                   jax.ShapeDtypeStruct((T, 128), F32), jax.ShapeDtypeStruct((8, 128), F32)],
        compiler_params=_params(1))(sinks, h16, h16, h16, h16, h16, dya)


def _conv_u(bc_ref, bx_ref, bch_ref, bxh_ref, n, tm):
    u = bc_ref[...] * bx_ref[...]
    uh = bch_ref[...] * bxh_ref[...] * (n > 0).astype(F32)
    rowi = _iota((tm, 256), 0)
    u1 = jnp.where(rowi == 0, uh[7:8, :], pltpu.roll(u, 1, axis=0))
    u2 = jnp.where(rowi == 0, uh[6:7, :], jnp.where(rowi == 1, uh[7:8, :], pltpu.roll(u, 2, axis=0)))
    return u, u1, u2


def _conv_fwd(h32, cw, cb):
    T = h32.shape[0]
    tm = 512 if T % 512 == 0 else T
    hb = tm // 8

    def body(bb_ref, bc_ref, bx_ref, bch_ref, bxh_ref, w_ref, b_ref, o_ref):
        n = pl.program_id(0)
        u, u1, u2 = _conv_u(bc_ref, bx_ref, bch_ref, bxh_ref, n, tm)
        y = w_ref[0:1, :] * u2 + w_ref[1:2, :] * u1 + w_ref[2:3, :] * u + b_ref[...]
        o_ref[...] = bb_ref[...] * y

    halo = lambda c: pl.BlockSpec((8, 256), lambda n: (jnp.maximum(n * hb - 1, 0), c // 256))
    return pl.pallas_call(
        body, name="conv_fwd", grid=(T // tm,),
        in_specs=[pl.BlockSpec((tm, 256), lambda n: (n, C_BB // 256)),
                  pl.BlockSpec((tm, 256), lambda n: (n, C_BC // 256)),
                  pl.BlockSpec((tm, 256), lambda n: (n, C_BX // 256)),
                  halo(C_BC), halo(C_BX),
                  pl.BlockSpec((8, 256), lambda n: (0, 0)),
                  pl.BlockSpec((1, 256), lambda n: (0, 0))],
        out_specs=pl.BlockSpec((tm, 256), lambda n: (n, 0)),
        out_shape=jax.ShapeDtypeStruct((T, 256), F32),
        compiler_params=_params(1))(h32, h32, h32, h32, h32, cw, cb)


def _conv_bwd(h32, cw, cb, dyb):
    T = h32.shape[0]
    tm = 512 if T % 512 == 0 else T
    hb = tm // 8
    nt = T // tm

    def body(bb_ref, bc_ref, bx_ref, bch_ref, bxh_ref, bbn_ref, dy_ref, dyn_ref, w_ref, b_ref,
             dbb_ref, dbc_ref, dbx_ref, dw_ref):
        n = pl.program_id(0)

        @pl.when(n == 0)
        def _():
            dw_ref[...] = jnp.zeros_like(dw_ref)

        u, u1, u2 = _conv_u(bc_ref, bx_ref, bch_ref, bxh_ref, n, tm)
        w0, w1, w2 = w_ref[0:1, :], w_ref[1:2, :], w_ref[2:3, :]
        y = w0 * u2 + w1 * u1 + w2 * u + b_ref[...]
        dyb_ = dy_ref[...]
        dbb_ref[...] = (dyb_ * y).astype(BF)
        dy = dyb_ * bb_ref[...]
        dyn = dyn_ref[...] * bbn_ref[...] * (n < nt - 1).astype(F32)
        rowi = _iota((tm, 256), 0)
        dy1 = jnp.where(rowi == tm - 1, dyn[0:1, :], pltpu.roll(dy, tm - 1, axis=0))
        dy2 = jnp.where(rowi == tm - 2, dyn[0:1, :],
                        jnp.where(rowi == tm - 1, dyn[1:2, :], pltpu.roll(dy, tm - 2, axis=0)))
        du = w2 * dy + w1 * dy1 + w0 * dy2
        dbc_ref[...] = (du * bx_ref[...]).astype(BF)
        dbx_ref[...] = (du * bc_ref[...]).astype(BF)
        dw_ref[0:1, :] += jnp.sum(dy * u2, axis=0, keepdims=True)
        dw_ref[1:2, :] += jnp.sum(dy * u1, axis=0, keepdims=True)
        dw_ref[2:3, :] += jnp.sum(dy * u, axis=0, keepdims=True)
        dw_ref[3:4, :] += jnp.sum(dy, axis=0, keepdims=True)

    halo = lambda c: pl.BlockSpec((8, 256), lambda n: (jnp.maximum(n * hb - 1, 0), c // 256))
    nxt = lambda c: pl.BlockSpec((8, 256), lambda n: (jnp.minimum((n + 1) * hb, T // 8 - 1), c // 256))
    cur = lambda c: pl.BlockSpec((tm, 256), lambda n: (n, c // 256))
    return pl.pallas_call(
        body, name="conv_bwd", grid=(nt,),
        in_specs=[cur(C_BB), cur(C_BC), cur(C_BX), halo(C_BC), halo(C_BX), nxt(C_BB),
                  cur(0), nxt(0),
                  pl.BlockSpec((8, 256), lambda n: (0, 0)),
                  pl.BlockSpec((1, 256), lambda n: (0, 0))],
        out_specs=[cur(0), cur(0), cur(0), pl.BlockSpec((8, 256), lambda n: (0, 0))],
        out_shape=[jax.ShapeDtypeStruct((T, 256), BF)] * 3 + [jax.ShapeDtypeStruct((8, 256), F32)],
        compiler_params=_params(1))(h32, h32, h32, h32, h32, h32, dyb, dyb, cw, cb)


def _cprep_specs(tm):
    return [pl.BlockSpec((tm, 256), lambda n: (n, C_CQ // 256)),
            pl.BlockSpec((tm, 128), lambda n: (n, C_CKV // 128)),
            pl.BlockSpec((tm, 128), lambda n: (n, C_CKR // 128)),
            pl.BlockSpec((tm, 128), lambda n: (n, C_CKRS // 128)),
            pl.BlockSpec((1, 256), lambda n: (0, 0)),
            pl.BlockSpec((1, 128), lambda n: (0, 0)),
            pl.BlockSpec((tm, 128), lambda n: (n, 0)),
            pl.BlockSpec((tm, 128), lambda n: (n, 0))]


def _cprep_fwd(h32, gq, gkv, wuq2, wkv2, cosk, sin):
    T = h32.shape[0]
    tm = 512 if T % 512 == 0 else T

    def body(cq_ref, ckv_ref, ckr_ref, ckrs_ref, gq_ref, gkv_ref, cos_ref, sin_ref, wuq_ref, wkv_ref,
             q_ref, k_ref, v_ref):
        cosk_, sin_ = cos_ref[...], sin_ref[...]
        cosq = cosk_ + (_iota((tm, 128), 1) < 64).astype(F32)
        cqn, _ = _rms_fwd(cq_ref[...], gq_ref[...])
        q2 = _dot(cqn.astype(BF), wuq_ref[...])
        ckvn, _ = _rms_fwd(ckv_ref[...], gkv_ref[...])
        kv2 = _dot(ckvn.astype(BF), wkv_ref[...])
        kr = ckr_ref[...] * cosk_ + ckrs_ref[...] * sin_
        for h in range(4):
            hs = slice(h * 128, (h + 1) * 128)
            q_ref[:, hs] = ((q2[:, hs] * cosq + q2[:, 512 + h * 128:512 + (h + 1) * 128] * sin_) * MLA_SCALE).astype(BF)
            k_ref[:, hs] = (kv2[:, hs] + kr).astype(BF)
        ones = _iota((tm, 512), 1) % 128 == 64
        v_ref[...] = jnp.where(ones, 1.0, kv2[:, 512:]).astype(BF)

    return pl.pallas_call(
        body, name="cprep_fwd", grid=(T // tm,),
        in_specs=_cprep_specs(tm) + [pl.BlockSpec((256, 1024), lambda n: (0, 0)),
                                     pl.BlockSpec((128, 1024), lambda n: (0, 0))],
        out_specs=[pl.BlockSpec((tm, 512), lambda n: (n, 0))] * 3,
        out_shape=[jax.ShapeDtypeStruct((T, 512), BF)] * 3,
        compiler_params=_params(1))(h32, h32, h32, h32, gq, gkv, cosk, sin, wuq2, wkv2)


def _cprep_bwd(h32, gq, gkv, wuq2t, wkv2t, cosk, sin, dq, dk, dv):
    T = h32.shape[0]
    tm = 512 if T % 512 == 0 else T

    def body(cq_ref, ckv_ref, ckr_ref, ckrs_ref, gq_ref, gkv_ref, cos_ref, sin_ref, wuq_ref, wkv_ref,
             dq_ref, dk_ref, dv_ref,
             dcq_ref, dckv_ref, dckr_ref, dckrs_ref, dq2_ref, dkv2_ref, cqn_ref, ckvn_ref, dgq_ref, dgkv_ref):
        n = pl.program_id(0)

        @pl.when(n == 0)
        def _():
            dgq_ref[...] = jnp.zeros_like(dgq_ref)
            dgkv_ref[...] = jnp.zeros_like(dgkv_ref)

        cosk_, sin_ = cos_ref[...], sin_ref[...]
        cosq = cosk_ + (_iota((tm, 128), 1) < 64).astype(F32)
        dkr = jnp.zeros((tm, 128), F32)
        for h in range(4):
            hs = slice(h * 128, (h + 1) * 128)
            dqh = dq_ref[:, hs] * MLA_SCALE
            dq2_ref[:, hs] = (dqh * cosq).astype(BF)
            dq2_ref[:, 512 + h * 128:512 + (h + 1) * 128] = (dqh * sin_).astype(BF)
            dkr = dkr + dk_ref[:, hs]
        dkv2_ref[:, :512] = dk_ref[...].astype(BF)
        dkv2_ref[:, 512:] = dv_ref[...].astype(BF)
        dckr_ref[...] = (dkr * cosk_).astype(BF)
        dckrs_ref[...] = (dkr * sin_).astype(BF)

        cq, gq_ = cq_ref[...], gq_ref[...]
        cqn, rq = _rms_fwd(cq, gq_)
        cqn_ref[...] = cqn.astype(BF)
        dcq, dgt = _rms_bwd(cq, gq_, rq, _dot(dq2_ref[...], wuq_ref[...]))
        dcq_ref[...] = dcq.astype(BF)
        dgq_ref[...] += jnp.sum(dgt, axis=0, keepdims=True)

        ckv, gkv_ = ckv_ref[...], gkv_ref[...]
        ckvn, rkv = _rms_fwd(ckv, gkv_)
        ckvn_ref[...] = ckvn.astype(BF)
        dckv, dgt2 = _rms_bwd(ckv, gkv_, rkv, _dot(dkv2_ref[...], wkv_ref[...]))
        dckv_ref[...] = dckv.astype(BF)
        dgkv_ref[...] += jnp.sum(dgt2, axis=0, keepdims=True)

    row = lambda w: pl.BlockSpec((tm, w), lambda n: (n, 0))
    return pl.pallas_call(
        body, name="cprep_bwd", grid=(T // tm,),
        in_specs=_cprep_specs(tm) + [pl.BlockSpec((1024, 256), lambda n: (0, 0)),
                                     pl.BlockSpec((1024, 128), lambda n: (0, 0)),
                                     row(512), row(512), row(512)],
        out_specs=[row(256), row(128), row(128), row(128), row(1024), row(1024), row(256), row(128),
                   pl.BlockSpec((1, 256), lambda n: (0, 0)), pl.BlockSpec((1, 128), lambda n: (0, 0))],
        out_shape=[jax.ShapeDtypeStruct((T, 256), BF), jax.ShapeDtypeStruct((T, 128), BF),
                   jax.ShapeDtypeStruct((T, 128), BF), jax.ShapeDtypeStruct((T, 128), BF),
                   jax.ShapeDtypeStruct((T, 1024), BF), jax.ShapeDtypeStruct((T, 1024), BF),
                   jax.ShapeDtypeStruct((T, 256), BF), jax.ShapeDtypeStruct((T, 128), BF),
                   jax.ShapeDtypeStruct((1, 256), F32), jax.ShapeDtypeStruct((1, 128), F32)],
        compiler_params=_params(1))(h32, h32, h32, h32, gq, gkv, cosk, sin, wuq2t, wkv2t, dq, dk, dv)


MLA_TILE = 512
MLA_HEADS_PER_STEP = 4


def _causal_mask(t):
    return _iota((t, t), 1) <= _iota((t, t), 0)


def _mla_fwd(q, k, v):
    T = q.shape[0]
    tq = MLA_TILE

    def body(q_ref, k_ref, v_ref, o_ref, lse_ref):
        i = pl.program_id(1)
        mask = _causal_mask(tq)
        heads = [slice(128 * h, 128 * h + 128) for h in range(MLA_HEADS_PER_STEP)]
        qs = [q_ref[:, hs] for hs in heads]

        def step(j, carry, masked):
            rows = pl.ds(pl.multiple_of(j * tq, tq), tq)
            out = []
            for hh, hs in enumerate(heads):
                m, acc = carry[hh]
                s = _dot_nt(qs[hh], k_ref[rows, hs])
                if masked:
                    s = jnp.where(mask, s, NEG)
                m_new = jnp.maximum(m, jnp.max(s, axis=-1, keepdims=True))
                p = jnp.exp((s - m_new).astype(BF))
                acc = jnp.exp(m - m_new) * acc + _dot(p, v_ref[rows, hs])
                out.append((m_new, acc))
            return tuple(out)

        init = ((jnp.full((tq, 1), NEG, F32), jnp.zeros((tq, 128), F32)),) * len(heads)
        carry = lax.fori_loop(0, i, lambda j, c: step(j, c, False), init)
        carry = step(i, carry, True)
        for hh, hs in enumerate(heads):
            m, acc = carry[hh]
            l = acc[:, 64:65]
            o_ref[:, hs] = acc * (1.0 / l)
            lse_ref[:, hs] = jnp.broadcast_to(m + jnp.log(l), (tq, 128))

    width = 128 * MLA_HEADS_PER_STEP
    blk = pl.BlockSpec((tq, width), lambda h, i: (i, h))
    full = pl.BlockSpec((T, width), lambda h, i: (0, h))
    return pl.pallas_call(
        body, name="mla_fwd", grid=(4 // MLA_HEADS_PER_STEP, T // tq), in_specs=[blk, full, full],
        out_specs=[blk, blk],
        out_shape=[jax.ShapeDtypeStruct((T, 512), F32), jax.ShapeDtypeStruct((T, 512), F32)],
        compiler_params=_params(2))(q, k, v)


def _mla_bwd(q, k, v, o, lse, do):
    T = q.shape[0]
    tq = MLA_TILE

    def body(q_ref, k_ref, v_ref, o_ref, lse_ref, do_ref, dq_ref, dk_ref, dv_ref):
        i = pl.program_id(1)

        @pl.when(i == 0)
        def _():
            dk_ref[...] = jnp.zeros_like(dk_ref)
            dv_ref[...] = jnp.zeros_like(dv_ref)

        heads = [slice(0, 128), slice(128, 256)]
        mask = _causal_mask(tq)
        qs, dobs, dds, lses = [], [], [], []
        for hs in heads:
            do = do_ref[:, hs]
            qs.append(q_ref[:, hs])
            dobs.append(do.astype(BF))
            dds.append(jnp.sum(do * o_ref[:, hs], axis=-1, keepdims=True))
            lses.append(lse_ref[:, hs.start:hs.start + 1])

        def step(j, dqs, masked):
            rows = pl.ds(pl.multiple_of(j * tq, tq), tq)
            out = []
            for hh, hs in enumerate(heads):
                kj, vj = k_ref[rows, hs], v_ref[rows, hs]
                s = _dot_nt(qs[hh], kj)
                if masked:
                    s = jnp.where(mask, s, NEG)
                p = jnp.exp(s - lses[hh])
                ds = (p * (_dot_nt(dobs[hh], vj) - dds[hh])).astype(BF)
                dk_ref[rows, hs] += _dot_tn(ds, qs[hh])
                dv_ref[rows, hs] += _dot_tn(p.astype(BF), dobs[hh])
                out.append(dqs[hh] + _dot(ds, kj))
            return tuple(out)

        dqs = lax.fori_loop(0, i, lambda j, c: step(j, c, False), (jnp.zeros((tq, 128), F32),) * 2)
        dqs = step(i, dqs, True)
        for hh, hs in enumerate(heads):
            dq_ref[:, hs] = dqs[hh]

    blk = pl.BlockSpec((tq, 256), lambda h, i: (i, h))
    full = pl.BlockSpec((T, 256), lambda h, i: (0, h), pipeline_mode=pl.Buffered(1))
    return pl.pallas_call(
        body, name="mla_bwd", grid=(2, T // tq), in_specs=[blk, full, full, blk, blk, blk],
        out_specs=[blk, full, full],
        out_shape=[jax.ShapeDtypeStruct((T, 512), F32)] * 3,
        compiler_params=_params(2))(q, k, v, o, lse, do)


def _sb_tile(qk, rr, strict, masked, upper):
    z2 = qk * (SB_SCALE * LOG2E)
    l1 = jnp.log2(1.0 + jnp.exp2(-jnp.abs(z2)))
    lk = -jnp.maximum(z2, 0.0) - l1
    if masked:
        lk = jnp.where(strict, lk, 0.0)
    after = rr + _dot(lk.astype(BF), upper)
    ll = jnp.minimum(z2, 0.0) - l1
    a = jnp.exp2(ll + after)
    if masked:
        a = jnp.where(strict, a, 0.0)
    return ll, a, jnp.sum(lk, axis=-1, keepdims=True)


SB_TQ, SB_TK = 256, 256
SB_DEAD = -160.0


def _sb_walk(trips, one_step, carry):
    def alive(c):
        t, cr = c
        top = jnp.maximum(jnp.max(cr[0][0]), jnp.max(cr[1][0]))
        return jnp.logical_and(t < trips, top > SB_DEAD)

    def body(c):
        t, cr = c
        return t + 1, one_step(t, cr)

    return lax.while_loop(alive, body, (jnp.int32(0), carry))[1]


def _sb_consts(tq, tk):
    row, col = _iota((tq, tk), 0), _iota((tq, tk), 1)
    strict = [col + d * tk < row for d in range(tq // tk)]
    r2, c2 = _iota((tk, tk), 0), _iota((tk, tk), 1)
    return strict, (r2 > c2).astype(BF), (r2 < c2).astype(BF)


def _sb_fwd(h16):
    T = h16.shape[0]
    tq, tk = SB_TQ, SB_TK
    nd = tq // tk

    def body(q_ref, k_ref, v_ref, o_ref):
        i = pl.program_id(1)
        strict, upper, _ = _sb_consts(tq, tk)
        lane = _iota((tq, 128), 1)
        q2 = q_ref[...]
        qms = [jnp.where(lane < 64, q2, jnp.zeros_like(q2)), jnp.where(lane >= 64, q2, jnp.zeros_like(q2))]

        def step(j, carry, d):
            rows = pl.ds(pl.multiple_of(j * tk, tk), tk)
            kj, vj = k_ref[rows, :], v_ref[rows, :]
            out = []
            for hh in range(2):
                rr, acc = carry[hh]
                _, a, rs = _sb_tile(_dot_nt(qms[hh], kj), rr, None if d is None else strict[d], d is not None, upper)
                out.append((rr + rs, acc + _dot(a.astype(BF), vj)))
            return tuple(out)


        carry = ((jnp.zeros((tq, 1), F32), jnp.zeros((tq, 128), F32)),) * 2
        for d in reversed(range(nd)):
            carry = step(nd * i + d, carry, d)
        carry = _sb_walk(nd * i, lambda t, c: step(nd * i - 1 - t, c, None), carry)
        o_ref[...] = jnp.where(lane < 64, carry[0][1], carry[1][1])

    return pl.pallas_call(
        body, name="sb_fwd", grid=(2, T // tq),
        in_specs=[pl.BlockSpec((tq, 128), lambda p, i: (i, C_DQ // 128 + p)),
                  pl.BlockSpec((T, 128), lambda p, i: (0, C_DK // 128 + p)),
                  pl.BlockSpec((T, 128), lambda p, i: (0, C_DV // 128 + p))],
        out_specs=pl.BlockSpec((tq, 128), lambda p, i: (i, p)),
        out_shape=jax.ShapeDtypeStruct((T, 256), F32),
        compiler_params=_params(2))(h16, h16, h16)


def _sb_bwd(h16, yd, dyd):
    T = h16.shape[0]
    tq, tk = SB_TQ, SB_TK
    nd = tq // tk

    def body(q_ref, k_ref, v_ref, o_ref, do_ref, dq_ref, dk_ref, dv_ref):
        i = pl.program_id(1)

        @pl.when(i == 0)
        def _():
            dk_ref[...] = jnp.zeros_like(dk_ref)
            dv_ref[...] = jnp.zeros_like(dv_ref)

        strict, upper, before = _sb_consts(tq, tk)
        lane = _iota((tq, 128), 1)
        lane_k = _iota((tk, 128), 1)
        q2 = q_ref[...]
        dob2 = do_ref[...].astype(BF)
        doo = dob2.astype(F32) * o_ref[...]
        mines = [lane < 64, lane >= 64]
        qms = [jnp.where(m, q2, jnp.zeros_like(q2)) for m in mines]
        doms = [jnp.where(m, dob2, jnp.zeros_like(dob2)) for m in mines]
        dds = [jnp.sum(jnp.where(m, doo, 0.0), axis=-1, keepdims=True) for m in mines]

        def step(j, carry, d):
            rows = pl.ds(pl.multiple_of(j * tk, tk), tk)
            kj, vj = k_ref[rows, :], v_ref[rows, :]
            out, dks, dvs = [], [], []
            for hh in range(2):
                rr, sg, dq = carry[hh]
                ll, a, rs = _sb_tile(_dot_nt(qms[hh], kj), rr, None if d is None else strict[d], d is not None,
                                     upper)
                ab = a.astype(BF)
                g = _dot_nt(doms[hh], vj) * ab.astype(F32)
                gs = jnp.sum(g, axis=-1, keepdims=True)
                pre = (dds[hh] - sg - gs) + _dot(g.astype(BF), before)
                dz = g - jnp.exp2(ll) * (g + pre)
                if d is not None:
                    dz = jnp.where(strict[d], dz, 0.0)
                dzb = dz.astype(BF)
                dks.append(_dot_tn(dzb, q2))
                dvs.append(_dot_tn(ab, dob2))
                out.append((rr + rs, sg + gs, dq + _dot(dzb, kj)))
            dk_ref[rows, :] += jnp.where(lane_k < 64, dks[0], dks[1]) * SB_SCALE
            dv_ref[rows, :] += jnp.where(lane_k < 64, dvs[0], dvs[1])
            return tuple(out)


        zero = jnp.zeros((tq, 1), F32)
        carry = ((zero, zero, jnp.zeros((tq, 128), F32)),) * 2
        for d in reversed(range(nd)):
            carry = step(nd * i + d, carry, d)
        carry = _sb_walk(nd * i, lambda t, c: step(nd * i - 1 - t, c, None), carry)
        dq_ref[...] = jnp.where(lane < 64, carry[0][2], carry[1][2]) * SB_SCALE

    blk = lambda c: pl.BlockSpec((tq, 128), lambda p, i: (i, c // 128 + p))
    full = lambda c: pl.BlockSpec((T, 128), lambda p, i: (0, c // 128 + p))
    return pl.pallas_call(
        body, name="sb_bwd", grid=(2, T // tq),
        in_specs=[blk(C_DQ), full(C_DK), full(C_DV), blk(0), blk(0)],
        out_specs=[blk(0), full(0), full(0)],
        out_shape=[jax.ShapeDtypeStruct((T, 256), F32)] * 3,
        compiler_params=_params(2))(h16, h16, h16, yd, dyd)


def _compact_c(ycp):
    return jnp.concatenate([ycp[:, h * 128:h * 128 + 64] for h in range(4)], axis=1)


def _post_fwd(ya, yb, ycp, yd, h32, ggrp, wout, gpost, x):
    T = x.shape[0]
    tm = 256

    def body(ya_ref, yb_ref, yc_ref, yd_ref, gate_ref, gg_ref, w_ref, gp_ref, x_ref, xn_ref, ym_ref, o_ref):
        ys = [ya_ref[...], yb_ref[...], _compact_c(yc_ref[...]), yd_ref[...]]
        gate = gate_ref[...]
        sil = gate * (1.0 / (1.0 + jnp.exp(-gate)))
        parts = []
        for gi in range(4):
            ng, _ = _rms_fwd(ys[gi], gg_ref[:, gi * 256:(gi + 1) * 256])
            parts.append(ng * sil[:, gi * 256:(gi + 1) * 256])
        ym = jnp.concatenate(parts, axis=1).astype(BF)
        ym_ref[...] = ym
        o = _dot(ym, w_ref[...])
        o_ref[...] = o
        on, _ = _rms_fwd(o, gp_ref[...])
        xn_ref[...] = x_ref[...] + on

    row = lambda w: pl.BlockSpec((tm, w), lambda n: (n, 0))
    vec = pl.BlockSpec((1, 1024), lambda n: (0, 0))
    return pl.pallas_call(
        body, name="post_fwd", grid=(T // tm,),
        in_specs=[row(256), row(256), row(512), row(256), pl.BlockSpec((tm, 1024), lambda n: (n, C_GATE // 1024)),
                  vec, pl.BlockSpec((1024, 1024), lambda n: (0, 0)), vec, row(1024)],
        out_specs=[row(1024), row(1024), row(1024)],
        out_shape=[jax.ShapeDtypeStruct((T, 1024), F32), jax.ShapeDtypeStruct((T, 1024), BF),
                   jax.ShapeDtypeStruct((T, 1024), F32)],
        compiler_params=_params(1))(ya, yb, ycp, yd, h32, ggrp, wout, gpost, x)


def _post_bwd(dx, o, gpost, woutt, ya, yb, ycp, yd, h32, ggrp):
    T = dx.shape[0]
    tm = 256

    def body(dx_ref, o_ref, gp_ref, w_ref, ya_ref, yb_ref, yc_ref, yd_ref, gate_ref, gg_ref,
             do_ref, dya_ref, dyb_ref, dyc_ref, dyd_ref, dgate_ref, dgp_ref, dgg_ref):
        n = pl.program_id(0)

        @pl.when(n == 0)
        def _():
            dgp_ref[...] = jnp.zeros_like(dgp_ref)
            dgg_ref[...] = jnp.zeros_like(dgg_ref)

        ov, gp = o_ref[...], gp_ref[...]
        _, ro = _rms_fwd(ov, gp)
        do, dgt = _rms_bwd(ov, gp, ro, dx_ref[...])
        dgp_ref[...] += jnp.sum(dgt, axis=0, keepdims=True)
        dob = do.astype(BF)
        do_ref[...] = dob
        dym = _dot(dob, w_ref[...])
        gate = gate_ref[...]
        sg = 1.0 / (1.0 + jnp.exp(-gate))
        sil = gate * sg
        dsil = sg * (1.0 + gate * (1.0 - sg))
        ys = [ya_ref[...], yb_ref[...], _compact_c(yc_ref[...]), yd_ref[...]]
        dys = []
        for gi in range(4):
            gs = slice(gi * 256, (gi + 1) * 256)
            gg = gg_ref[:, gs]
            ng, rg = _rms_fwd(ys[gi], gg)
            dgate_ref[:, gs] = (dym[:, gs] * ng * dsil[:, gs]).astype(BF)
            dy, dgt2 = _rms_bwd(ys[gi], gg, rg, dym[:, gs] * sil[:, gs])
            dgg_ref[:, gs] += jnp.sum(dgt2, axis=0, keepdims=True)
            dys.append(dy)
        dya_ref[...] = dys[0]
        dyb_ref[...] = dys[1]
        dyd_ref[...] = dys[3]
        z64 = jnp.zeros((tm, 64), F32)
        dyc_ref[...] = jnp.concatenate(
            [piece for h in range(4) for piece in (dys[2][:, h * 64:(h + 1) * 64], z64)], axis=1)

    row = lambda w: pl.BlockSpec((tm, w), lambda n: (n, 0))
    vec = pl.BlockSpec((1, 1024), lambda n: (0, 0))
    return pl.pallas_call(
        body, name="post_bwd", grid=(T // tm,),
        in_specs=[row(1024), row(1024), vec, pl.BlockSpec((1024, 1024), lambda n: (0, 0)),
                  row(256), row(256), row(512), row(256),
                  pl.BlockSpec((tm, 1024), lambda n: (n, C_GATE // 1024)), vec],
        out_specs=[row(1024), row(256), row(256), row(512), row(256), row(1024), vec, vec],
        out_shape=[jax.ShapeDtypeStruct((T, 1024), BF), jax.ShapeDtypeStruct((T, 256), F32),
                   jax.ShapeDtypeStruct((T, 256), F32), jax.ShapeDtypeStruct((T, 512), F32),
                   jax.ShapeDtypeStruct((T, 256), F32), jax.ShapeDtypeStruct((T, 1024), BF),
                   jax.ShapeDtypeStruct((1, 1024), F32), jax.ShapeDtypeStruct((1, 1024), F32)],
        compiler_params=_params(1))(dx, o, gpost, woutt, ya, yb, ycp, yd, h32, ggrp)


def _loss_head(y, tgt):
    T = y.shape[0]
    tm = 512 if T % 512 == 0 else T

    def body(y_ref, t_ref, s_ref, dy_ref):
        n = pl.program_id(0)

        @pl.when(n == 0)
        def _():
            s_ref[...] = jnp.zeros_like(s_ref)

        d = y_ref[...] - t_ref[...]
        s_ref[...] += jnp.sum(d * d, axis=0, keepdims=True)
        dy_ref[...] = d * (1.0 / D_MODEL)

    row = pl.BlockSpec((tm, 1024), lambda n: (n, 0))
    return pl.pallas_call(
        body, name="loss_head", grid=(T // tm,), in_specs=[row, row],
        out_specs=[pl.BlockSpec((1, 1024), lambda n: (0, 0)), row],
        out_shape=[jax.ShapeDtypeStruct((1, 1024), F32), jax.ShapeDtypeStruct((T, 1024), F32)],
        compiler_params=_params(1))(y, tgt)


def _swap_rows32(a):
    return jnp.concatenate([a[16:32], a[0:16]], axis=0)


def _pad_w_uq(w):
    z = lambda n: jnp.zeros((w.shape[0], n), w.dtype)
    a = [p for h in range(4) for p in (w[:, 96 * h:96 * h + 96], z(32))]
    b = [p for h in range(4) for p in (z(64), _swap32(w[:, 96 * h + 64:96 * h + 96]), z(32))]
    return jnp.concatenate(a + b, axis=1)


def _unpad_w_uq(d):
    out = []
    for h in range(4):
        out.append(d[:, 128 * h:128 * h + 64])
        out.append(d[:, 128 * h + 64:128 * h + 96] + _swap32(d[:, 512 + 128 * h + 64:512 + 128 * h + 96]))
    return jnp.concatenate(out, axis=1)


def _pad_w_ukv(w):
    z = jnp.zeros((w.shape[0], 64), w.dtype)
    a = [p for h in range(4) for p in (w[:, 128 * h:128 * h + 64], z)]
    b = [p for h in range(4) for p in (w[:, 128 * h + 64:128 * h + 128], z)]
    return jnp.concatenate(a + b, axis=1)


def _unpad_w_ukv(d):
    return jnp.concatenate([p for h in range(4) for p in (d[:, 128 * h:128 * h + 64],
                                                          d[:, 512 + 128 * h:512 + 128 * h + 64])], axis=1)


def _rope_tables(pos):
    freqs = 10000.0 ** (-jnp.arange(16, dtype=F32) / 16)
    ang = pos.astype(F32)[:, None] * freqs
    c, s = jnp.cos(ang), jnp.sin(ang)
    z = lambda n: jnp.zeros((pos.shape[0], n), F32)
    return (jnp.concatenate([z(64), c, c, z(32)], axis=1), jnp.concatenate([z(64), -s, s, z(32)], axis=1))


def _layer_weights(W, l):
    wuq2 = _pad_w_uq(W["mla_w_uq"][l])
    wkv2 = _pad_w_ukv(W["mla_w_ukv"][l])
    wout = W["w_out"][l]
    cw = jnp.concatenate([W["conv_w"][l].astype(F32), jnp.zeros((5, 256), F32)], axis=0)
    return dict(
        wpt=W["wpt"][l], wuq2=wuq2.astype(BF), wuq2t=wuq2.T.astype(BF),
        wkv2=wkv2.astype(BF), wkv2t=wkv2.T.astype(BF), wout=wout.astype(BF), woutt=wout.T.astype(BF),
        cw=cw, cb=W["conv_b"][l][None, :], sinks=W["attn_sinks"][l],
        gpre=W["norm_pre"][l][None, :], gq=W["mla_q_norm"][l][None, :], gkv=W["mla_kv_norm"][l][None, :],
        ggrp=W["group_norm"][l][None, :], gpost=W["norm_post"][l][None, :])


def _local_step(x, pos, W, tgt):
    cosk, sin = _rope_tables(pos)
    saved = []
    for l in range(DEPTH):
        lw = _layer_weights(W, l)
        h32, h16, xn = _inproj_fwd(x, lw["gpre"], lw["wpt"])
        ya = _swa_fwd(h16, lw["sinks"])
        yb = _conv_fwd(h32, lw["cw"], lw["cb"])
        qc, kc, vc = _cprep_fwd(h32, lw["gq"], lw["gkv"], lw["wuq2"], lw["wkv2"], cosk, sin)
        ycp, lse = _mla_fwd(qc, kc, vc)
        yd = _sb_fwd(h16)
        x_new, ym, o = _post_fwd(ya, yb, ycp, yd, h32, lw["ggrp"], lw["wout"], lw["gpost"], x)
        saved.append(dict(lw=lw, x=x, h32=h32, h16=h16, xn=xn, ya=ya, yb=yb, qc=qc, kc=kc, vc=vc, ycp=ycp,
                          lse=lse, yd=yd, ym=ym, o=o))
        x = x_new
    sq, dx = _loss_head(x, tgt)

    grads = {k: [None] * DEPTH for k in ("norm_pre", "w_in_pt", "attn_sinks", "conv_w", "conv_b", "mla_q_norm",
                                         "mla_w_uq", "mla_kv_norm", "mla_w_ukv", "group_norm", "w_out",
                                         "norm_post")}
    for l in reversed(range(DEPTH)):
        s = saved[l]
        lw = s["lw"]
        do, dya, dyb, dycp, dyd, dgate, dgpost, dggrp = _post_bwd(
            dx, s["o"], lw["gpost"], lw["woutt"], s["ya"], s["yb"], s["ycp"], s["yd"], s["h32"], lw["ggrp"])
        grads["norm_post"][l] = dgpost[0]
        grads["group_norm"][l] = dggrp[0]
        grads["w_out"][l] = _matmul_tn(s["ym"], do, 512, "dw_out")
        sdq, sdk, sdv = _sb_bwd(s["h16"], s["yd"], dyd)
        mdq, mdk, mdv = _mla_bwd(s["qc"], s["kc"], s["vc"], s["ycp"], s["lse"], dycp)
        (dcq, dckv, dckr, dckrs, dq2, dkv2, cqn, ckvn, dgq, dgkv) = _cprep_bwd(
            s["h32"], lw["gq"], lw["gkv"], lw["wuq2t"], lw["wkv2t"], cosk, sin, mdq, mdk, mdv)
        grads["mla_q_norm"][l] = dgq[0]
        grads["mla_kv_norm"][l] = dgkv[0]
        grads["mla_w_uq"][l] = _unpad_w_uq(_matmul_tn(cqn, dq2, 512, "dw_uq"))
        grads["mla_w_ukv"][l] = _unpad_w_ukv(_matmul_tn(ckvn, dkv2, 512, "dw_ukv"))
        dbb, dbc, dbx, dcw = _conv_bwd(s["h32"], lw["cw"], lw["cb"], dyb)
        grads["conv_w"][l] = dcw[0:3]
        grads["conv_b"][l] = dcw[3]
        adq, adk, adv, dsk = _swa_bwd(s["h16"], lw["sinks"], dya)
        grads["attn_sinks"][l] = dsk[0, 0:4]
        parts = [dgate, adq, adk, adv, dbb, dbc, dbx, dcq, dckv, dckr, dckrs, sdq, sdk, sdv]
        dh, dx, dgpre = _inproj_bwd_x(parts, lw["wpt"], s["x"], lw["gpre"], dx)
        grads["w_in_pt"][l] = _matmul_tn(s["xn"], dh, 1280, "dw_in").T
        grads["norm_pre"][l] = dgpre[0]
    return sq, dx, grads


SMALL_SHARDED = ("conv_w", "mla_w_uq", "mla_w_ukv")
REPLICATED = ("norm_pre", "attn_sinks", "conv_b", "mla_q_norm", "mla_kv_norm", "group_norm", "norm_post")
ORDER = ("norm_pre", "w_in", "attn_sinks", "conv_w", "conv_b", "mla_q_norm", "mla_w_uq", "mla_kv_norm",
         "mla_w_ukv", "group_norm", "w_out", "norm_post")
W_IN_COLS = 436
W_IN_WIN = 440
SMALL_ROWS = 48


def _pack_small(arrs, dtype):
    flat = jnp.concatenate([a.reshape(-1).astype(dtype) for a in arrs])
    flat = jnp.concatenate([flat, jnp.zeros((SMALL_ROWS * D_MODEL - flat.shape[0],), dtype)])
    return flat.reshape(SMALL_ROWS, D_MODEL)


def _pack_state(ps, c):
    k = len(ps)
    wt = jnp.transpose(jnp.stack([p["w_in"] for p in ps]), (0, 1, 3, 2))
    win = lax.dynamic_update_slice(jnp.zeros((k, DEPTH, W_IN_WIN, D_MODEL), F32), wt, (0, 0, 4 * c, 0))
    wout = jnp.stack([p["w_out"] for p in ps]).reshape(k, DEPTH * 128, D_MODEL)
    flat = jnp.stack([jnp.concatenate([p[n].reshape(-1) for n in SMALL_SHARDED + REPLICATED]) for p in ps])
    small = jnp.pad(flat, ((0, 0), (0, SMALL_ROWS * D_MODEL - flat.shape[1]))).reshape(k, SMALL_ROWS, D_MODEL)
    return jnp.concatenate([win.reshape(k, DEPTH * W_IN_WIN, D_MODEL), wout, small], axis=1)


def _unpack_state(buf, p, c):
    k = buf.shape[0]
    nw = DEPTH * W_IN_WIN
    win = lax.dynamic_slice(buf[:, 0:nw].reshape(k, DEPTH, W_IN_WIN, D_MODEL), (0, 0, 4 * c, 0),
                            (k, DEPTH, W_IN_COLS, D_MODEL))
    out = {"w_in": jnp.transpose(win, (0, 1, 3, 2)),
           "w_out": buf[:, nw:nw + DEPTH * 128].reshape(k, DEPTH, 128, D_MODEL)}
    flat = buf[:, nw + DEPTH * 128:].reshape(k, SMALL_ROWS * D_MODEL)
    off = 0
    for n in SMALL_SHARDED + REPLICATED:
        size = int(np.prod(p[n].shape))
        out[n] = flat[:, off:off + size].reshape((k,) + p[n].shape)
        off += size
    return out


def _rows_of_w_in_t(lo, hi, padded, kr):
    segs = ((0, 1664, padded, C_AQ), (1664, 1696, kr, 0), (1696, 2464, padded, C_DQ), (2464, 3488, padded, C_GATE))
    out = []
    for s0, s1, src, base in segs:
        a, b = max(lo, s0), min(hi, s1)
        if a < b:
            out.append(src[base + a - s0:base + b - s0])
    return out


def _me():
    return lax.axis_index("x"), lax.axis_index("y"), lax.axis_index("c")


def _all_gather(block):
    R, C = block.shape

    def body(src_ref, out_ref, send_sems, recv_sems, local_sem):
        x, y, c = _me()
        me, sibling = (x, y, c), (x, y, 1 - c)
        chips = [(1 - x, y), (x, 1 - y), (1 - x, 1 - y)]

        def slot(px, py, pc):
            return out_ref.at[4 * px + 2 * py + pc]

        def copy(k, block, to, src=None):
            return pltpu.make_async_remote_copy(
                src_ref=slot(*block) if src is None else src, dst_ref=slot(*block), send_sem=send_sems.at[k],
                recv_sem=recv_sems.at[k], device_id=to, device_id_type=MESH)

        mine = pltpu.make_async_copy(src_ref, slot(*me), local_sem)
        mine.start()
        first = [copy(0, me, sibling, src=src_ref)]
        first += [copy(1 + j, me, (*chip, c), src=src_ref) for j, chip in enumerate(chips)]
        for cp in first:
            cp.start()
        passed = [copy(4 + j, (*chip, c), sibling) for j, chip in enumerate(chips)]
        for j, chip in enumerate(chips):
            copy(1 + j, (*chip, c), me).wait_recv()
            passed[j].start()
        copy(0, sibling, me).wait_recv()
        for j, chip in enumerate(chips):
            copy(4 + j, (*chip, 1 - c), me).wait_recv()
        for cp in first + passed:
            cp.wait_send()
        mine.wait()

    return pl.pallas_call(
        body, name="all_gather", out_shape=jax.ShapeDtypeStruct((N_DEV, R, C), block.dtype),
        in_specs=[pl.BlockSpec(memory_space=pl.ANY)], out_specs=pl.BlockSpec(memory_space=pl.ANY),
        scratch_shapes=[pltpu.SemaphoreType.DMA((N_DEV - 1,)), pltpu.SemaphoreType.DMA((N_DEV - 1,)),
                        pltpu.SemaphoreType.DMA])(block)


N_CHIP = 4


def _sibling_swap(blocks):
    _, R, C = blocks.shape

    def body(src_ref, out_ref, send_sems, recv_sems):
        x, y, c = _me()
        copies = [pltpu.make_async_remote_copy(
            src_ref=src_ref.at[2 * j + 1 - c], dst_ref=out_ref.at[j], send_sem=send_sems.at[j],
            recv_sem=recv_sems.at[j], device_id=(x, y, 1 - c), device_id_type=MESH) for j in range(N_CHIP)]
        for cp in copies:
            cp.start()
        for cp in copies:
            cp.wait()

    return pl.pallas_call(
        body, name="sibling_swap", out_shape=jax.ShapeDtypeStruct((N_CHIP, R, C), blocks.dtype),
        in_specs=[pl.BlockSpec(memory_space=pl.ANY)], out_specs=pl.BlockSpec(memory_space=pl.ANY),
        scratch_shapes=[pltpu.SemaphoreType.DMA((N_CHIP,)), pltpu.SemaphoreType.DMA((N_CHIP,))])(blocks)


def _pair_sum(a, b):
    n, R, C = a.shape
    tr = 592 if R % 592 == 0 else R

    def body(a_ref, b_ref, o_ref):
        o_ref[...] = (a_ref[...].astype(F32) + b_ref[...].astype(F32)).astype(BF)

    spec = pl.BlockSpec((1, tr, C), lambda j, r: (j, r, 0))
    return pl.pallas_call(body, name="pair_sum", grid=(n, R // tr), in_specs=[spec, spec], out_specs=spec,
                          out_shape=jax.ShapeDtypeStruct(a.shape, BF), compiler_params=_params(2))(a, b)


def _chip_exchange(sums):
    _, R, C = sums.shape

    def body(src_ref, out_ref, send_sems, recv_sems, local_sem):
        x, y, c = _me()
        here = 2 * x + y
        mine = pltpu.make_async_copy(src_ref.at[here], out_ref.at[here], local_sem)
        mine.start()
        copies = []
        for k in range(1, N_CHIP):
            px, py = x ^ (k >> 1), y ^ (k & 1)
            copies.append(pltpu.make_async_remote_copy(
                src_ref=src_ref.at[2 * px + py], dst_ref=out_ref.at[here], send_sem=send_sems.at[k - 1],
                recv_sem=recv_sems.at[k - 1], device_id=(px, py, c), device_id_type=MESH))
        for cp in copies:
            cp.start()
        for cp in copies:
            cp.wait()
        mine.wait()

    return pl.pallas_call(
        body, name="chip_exchange", out_shape=jax.ShapeDtypeStruct((N_CHIP, R, C), sums.dtype),
        in_specs=[pl.BlockSpec(memory_space=pl.ANY)], out_specs=pl.BlockSpec(memory_space=pl.ANY),
        scratch_shapes=[pltpu.SemaphoreType.DMA((N_CHIP - 1,)), pltpu.SemaphoreType.DMA((N_CHIP - 1,)),
                        pltpu.SemaphoreType.DMA])(sums)


def _adamw(parts, state):
    _, R, C = state.shape
    n_parts = parts.shape[0]
    tr = 32
    assert R % tr == 0

    def body(p_ref, s_ref, o_ref):
        g = p_ref[0].astype(F32)
        for k in range(1, n_parts):
            g = g + p_ref[k].astype(F32)
        o_ref[0] = g
        m_ = ADAM_B1 * s_ref[1] + (1.0 - ADAM_B1) * g
        v_ = ADAM_B2 * s_ref[2] + (1.0 - ADAM_B2) * (g * g)
        o_ref[2] = m_
        o_ref[3] = v_
        m_hat = m_ / (1.0 - ADAM_B1 ** ADAM_STEP)
        v_hat = v_ / (1.0 - ADAM_B2 ** ADAM_STEP)
        o_ref[1] = -ADAM_LR * (m_hat / (jnp.sqrt(v_hat) + ADAM_EPS) + ADAM_WD * s_ref[0])

    return pl.pallas_call(
        body, name="adamw", grid=(R // tr,),
        in_specs=[pl.BlockSpec((n_parts, tr, C), lambda n: (0, n, 0)), pl.BlockSpec((3, tr, C), lambda n: (0, n, 0))],
        out_specs=pl.BlockSpec((4, tr, C), lambda n: (0, n, 0)), out_shape=jax.ShapeDtypeStruct((4, R, C), F32),
        compiler_params=_params(1))(parts, state)


def kernel(x, positions, norm_pre, w_in, attn_sinks, conv_w, conv_b, mla_q_norm, mla_w_uq, mla_kv_norm, mla_w_ukv, group_norm, w_out, norm_post, loss_target, m_norm_pre, m_w_in, m_attn_sinks, m_conv_w, m_conv_b, m_mla_q_norm, m_mla_w_uq, m_mla_kv_norm, m_mla_w_ukv, m_group_norm, m_w_out, m_norm_post, v_norm_pre, v_w_in, v_attn_sinks, v_conv_w, v_conv_b, v_mla_q_norm, v_mla_w_uq, v_mla_kv_norm, v_mla_w_ukv, v_group_norm, v_w_out, v_norm_post):
    local = dict(norm_pre=norm_pre, w_in=w_in, attn_sinks=attn_sinks, conv_w=conv_w, conv_b=conv_b,
                 mla_q_norm=mla_q_norm, mla_w_uq=mla_w_uq, mla_kv_norm=mla_kv_norm, mla_w_ukv=mla_w_ukv,
                 group_norm=group_norm, w_out=w_out, norm_post=norm_post)
    mom = dict(norm_pre=m_norm_pre, w_in=m_w_in, attn_sinks=m_attn_sinks, conv_w=m_conv_w, conv_b=m_conv_b,
               mla_q_norm=m_mla_q_norm, mla_w_uq=m_mla_w_uq, mla_kv_norm=m_mla_kv_norm, mla_w_ukv=m_mla_w_ukv,
               group_norm=m_group_norm, w_out=m_w_out, norm_post=m_norm_post)
    vel = dict(norm_pre=v_norm_pre, w_in=v_w_in, attn_sinks=v_attn_sinks, conv_w=v_conv_w, conv_b=v_conv_b,
               mla_q_norm=v_mla_q_norm, mla_w_uq=v_mla_w_uq, mla_kv_norm=v_mla_kv_norm, mla_w_ukv=v_mla_w_ukv,
               group_norm=v_group_norm, w_out=v_w_out, norm_post=v_norm_post)

    c = lax.axis_index("c")

    wt = jnp.transpose(w_in, (0, 2, 1)).astype(BF)
    wt = jnp.concatenate([wt, jnp.zeros((DEPTH, 448 - W_IN_COLS, D_MODEL), BF)], axis=1)
    payload = jnp.concatenate([wt.reshape(DEPTH * 448, D_MODEL), w_out.astype(BF).reshape(DEPTH * 128, D_MODEL),
                               _pack_small([local[n] for n in SMALL_SHARDED], BF)], axis=0)
    gathered = _all_gather(payload)
    W = {n: local[n] for n in REPLICATED}

    def nat_rows(l, lo, hi):
        out, r = [], lo
        while r < hi:
            d = r // W_IN_COLS
            e = min(hi, (d + 1) * W_IN_COLS)
            out.append(gathered[d, 448 * l + r - W_IN_COLS * d:448 * l + e - W_IN_COLS * d])
            r = e
        return out

    z = lambda n: [jnp.zeros((n, D_MODEL), BF)]
    W["wpt"] = [jnp.concatenate(nat_rows(l, 2464, 3488) + nat_rows(l, 0, 1664) + z(64) + nat_rows(l, 1664, 1696)
                                + z(96) + nat_rows(l, 1680, 1696) + nat_rows(l, 1664, 1680) + z(32)
                                + nat_rows(l, 1696, 2464) + z(NP - C_END), axis=0) for l in range(DEPTH)]
    wo0 = DEPTH * 448
    W["w_out"] = gathered[:, wo0:wo0 + DEPTH * 128].reshape(N_DEV, DEPTH, 128, D_MODEL).transpose(1, 0, 2, 3).reshape(
        DEPTH, D_MODEL, D_MODEL)
    flat = gathered[:, wo0 + DEPTH * 128:].reshape(N_DEV, SMALL_ROWS * D_MODEL)
    off = 0
    for n in SMALL_SHARDED:
        depth, rows, width = local[n].shape
        size = depth * rows * width
        W[n] = flat[:, off:off + size].reshape(N_DEV, depth, rows, width).transpose(1, 2, 0, 3).reshape(
            depth, rows, N_DEV * width)
        off += size

    sq, grad_x, g = _local_step(x[0], positions[0], W, loss_target[0])
    loss = lax.psum(0.5 / D_MODEL * jnp.sum(sq), ("x", "y", "c"))

    cols = []
    for n in SMALL_SHARDED:
        depth, rows, width = local[n].shape
        cols.append(jnp.stack(g[n]).reshape(depth, rows, N_DEV, width).transpose(2, 0, 1, 3).reshape(N_DEV, -1))
    rep = jnp.concatenate([a.reshape(-1) for n in REPLICATED for a in g[n]])
    cols.append(jnp.broadcast_to(rep[None], (N_DEV, rep.shape[0])))
    small = jnp.concatenate(cols, axis=1)
    small = jnp.pad(small, ((0, 0), (0, SMALL_ROWS * D_MODEL - small.shape[1]))).reshape(N_DEV, SMALL_ROWS, D_MODEL)
    krs = [p[C_CKR + 64:C_CKR + 96] + _swap_rows32(p[C_CKRS + 64:C_CKRS + 96]) for p in g["w_in_pt"]]
    pieces = []
    for d in range(N_DEV):
        lo = W_IN_COLS * d // 8 * 8
        for l in range(DEPTH):
            pieces += _rows_of_w_in_t(lo, lo + W_IN_WIN, g["w_in_pt"][l], krs[l])
        pieces += [g["w_out"][l][128 * d:128 * (d + 1)] for l in range(DEPTH)]
        pieces.append(small[d])
    blocks = jnp.concatenate(pieces, axis=0).astype(BF).reshape(N_DEV, -1, D_MODEL)
    mine = lax.dynamic_index_in_dim(blocks.reshape(N_CHIP, 2, -1, D_MODEL), c, axis=1, keepdims=False)
    received = _chip_exchange(_pair_sum(mine, _sibling_swap(blocks)))

    out = _unpack_state(_adamw(received, _pack_state([local, mom, vel], c)), local, c)
    return (loss, grad_x[None], *[out[n][t] for t in range(4) for n in ORDER])
```

```python
import functools

import jax
import jax.numpy as jnp
import numpy as np
from jax import lax
from jax.experimental import pallas as pl
from jax.experimental.pallas import tpu as pltpu

F32 = jnp.float32
BF = jnp.bfloat16
MESH = pl.DeviceIdType.MESH

D_MODEL = 1024
DEPTH = 2
EPS = 1e-6
N_DEV = 8
VMEM_LIMIT = 56 * 1024 * 1024
NEG = -1e30
MLA_SCALE = 96.0 ** -0.5
SB_SCALE = 0.125
LOG2E = 1.4426950408889634

NP = 3840
C_GATE = 0
C_AQ = 1024
C_AK = 1280
C_AV = 1408
C_BB = 1536
C_BC = 1792
C_BX = 2048
C_CQ = 2304
C_CKV = 2560
C_CKR = 2688
C_CKRS = 2816
C_DQ = 2944
C_DK = 3200
C_DV = 3456
C_END = 3712

def _swap32(a):
    return jnp.concatenate([a[:, 16:32], a[:, 0:16]], axis=1)

ADAM_LR, ADAM_B1, ADAM_B2, ADAM_EPS, ADAM_WD, ADAM_STEP = 0.001, 0.9, 0.999, 1e-08, 0.01, 10


def _dot(a, b):
    return jnp.dot(a, b, preferred_element_type=F32)


def _dot_nt(a, b):
    return lax.dot_general(a, b, (((1,), (1,)), ((), ())), preferred_element_type=F32)


def _dot_tn(a, b):
    return lax.dot_general(a, b, (((0,), (0,)), ((), ())), preferred_element_type=F32)


def _params(n_grid):
    return pltpu.CompilerParams(dimension_semantics=("arbitrary",) * n_grid, vmem_limit_bytes=VMEM_LIMIT)


def _rms_fwd(x, g):
    r = lax.rsqrt(jnp.mean(x * x, axis=-1, keepdims=True) + EPS)
    return (x * r) * g, r


def _rms_bwd(x, g, r, dy, width=None):
    n = x.shape[-1] if width is None else width
    u = dy * g
    dx = r * u - x * (r * r * r) * (jnp.sum(x * u, axis=-1, keepdims=True) / n)
    return dx, dy * (x * r)


def _iota(shape, axis):
    return lax.broadcasted_iota(jnp.int32, shape, axis)


def _inproj_fwd(x, g, wpt):
    T = x.shape[0]
    tm = 256

    def body(x_ref, g_ref, w_ref, h32_ref, h16_ref, xn_ref):
        xn, _ = _rms_fwd(x_ref[...], g_ref[...])
        xn = xn.astype(BF)
        xn_ref[...] = xn
        h = _dot_nt(xn, w_ref[...])
        h32_ref[...] = h
        h16_ref[...] = h.astype(BF)

    return pl.pallas_call(
        body, name="inproj_fwd", grid=(T // tm,),
        in_specs=[pl.BlockSpec((tm, D_MODEL), lambda n: (n, 0)),
                  pl.BlockSpec((1, D_MODEL), lambda n: (0, 0)),
                  pl.BlockSpec((NP, D_MODEL), lambda n: (0, 0))],
        out_specs=[pl.BlockSpec((tm, NP), lambda n: (n, 0)),
                   pl.BlockSpec((tm, NP), lambda n: (n, 0)),
                   pl.BlockSpec((tm, D_MODEL), lambda n: (n, 0))],
        out_shape=[jax.ShapeDtypeStruct((T, NP), F32), jax.ShapeDtypeStruct((T, NP), BF),
                   jax.ShapeDtypeStruct((T, D_MODEL), BF)],
        compiler_params=_params(1))(x, g, wpt)


def _inproj_bwd(parts, wpt, x, xn, g, dxo):
    T = x.shape[0]
    tm = 256
    np_ = len(parts)
    chunk = NP // 3
    assert sum(p.shape[1] for p in parts) == C_END and chunk % 128 == 0

    def body(*refs):
        part_refs = refs[:np_]
        w_ref, x_ref, xn_ref, g_ref, dxo_ref, dw_ref, dx_ref, dg_ref = refs[np_:]
        n = pl.program_id(0)

        @pl.when(n == 0)
        def _():
            dw_ref[...] = jnp.zeros_like(dw_ref)
            dg_ref[...] = jnp.zeros_like(dg_ref)

        dh = jnp.concatenate([r[...].astype(BF) for r in part_refs] + [jnp.zeros((tm, NP - C_END), BF)], axis=1)
        xnv = xn_ref[...]
        for cb in range(3):
            cs = slice(cb * chunk, (cb + 1) * chunk)
            dw_ref[:, cs] += _dot_tn(xnv, dh[:, cs])
        dxn = _dot(dh, w_ref[...])
        xv = x_ref[...]
        _, r = _rms_fwd(xv, g_ref[...])
        dx, dgt = _rms_bwd(xv, g_ref[...], r, dxn)
        dx_ref[...] = dxo_ref[...] + dx
        dg_ref[...] += jnp.sum(dgt, axis=0, keepdims=True)

    once = pl.Buffered(1)
    return pl.pallas_call(
        body, name="inproj_bwd", grid=(T // tm,),
        in_specs=[pl.BlockSpec((tm, p.shape[1]), lambda n: (n, 0)) for p in parts]
        + [pl.BlockSpec((NP, D_MODEL), lambda n: (0, 0), pipeline_mode=once),
           pl.BlockSpec((tm, D_MODEL), lambda n: (n, 0)),
           pl.BlockSpec((tm, D_MODEL), lambda n: (n, 0)),
           pl.BlockSpec((1, D_MODEL), lambda n: (0, 0)),
           pl.BlockSpec((tm, D_MODEL), lambda n: (n, 0))],
        out_specs=[pl.BlockSpec((D_MODEL, NP), lambda n: (0, 0), pipeline_mode=once),
                   pl.BlockSpec((tm, D_MODEL), lambda n: (n, 0)),
                   pl.BlockSpec((1, D_MODEL), lambda n: (0, 0))],
        out_shape=[jax.ShapeDtypeStruct((D_MODEL, NP), F32), jax.ShapeDtypeStruct((T, D_MODEL), F32),
                   jax.ShapeDtypeStruct((1, D_MODEL), F32)],
        compiler_params=_params(1))(*parts, wpt, x, xn, g, dxo)


SWA_BLK = 128
SWA_TQ = 1024


def _bdot_nt(a, b):
    return lax.dot_general(a, b, (((2,), (2,)), ((0,), (0,))), preferred_element_type=F32)


def _bdot(a, b):
    return lax.dot_general(a, b, (((2,), (1,)), ((0,), (0,))), preferred_element_type=F32)


def _bdot_tn(a, b):
    return lax.dot_general(a, b, (((1,), (1,)), ((0,), (0,))), preferred_element_type=F32)


def _swa_probs(q, kc, kp, sink, mask_c, mask_p):
    sc = jnp.where(mask_c, _bdot_nt(q, kc) * SB_SCALE, NEG)
    sp = jnp.where(mask_p, _bdot_nt(q, kp) * SB_SCALE, NEG)
    m = jnp.maximum(jnp.maximum(jnp.max(sc, axis=-1, keepdims=True), jnp.max(sp, axis=-1, keepdims=True)), sink)
    pc = jnp.exp(sc - m)
    pp = jnp.exp(sp - m)
    ps = jnp.exp(sink - m)
    inv = 1.0 / (jnp.sum(pc, axis=-1, keepdims=True) + jnp.sum(pp, axis=-1, keepdims=True) + ps)
    return pc * inv, pp * inv, ps * inv


def _swa_masks(n, nb):
    blk = _iota((nb, SWA_BLK, SWA_BLK), 0)
    row = _iota((nb, SWA_BLK, SWA_BLK), 1)
    col = _iota((nb, SWA_BLK, SWA_BLK), 2)
    return col <= row, jnp.logical_and(col > row, jnp.logical_or(blk > 0, n > 0))


def _swa_specs(tq):
    halo = tq // SWA_BLK
    return [pl.BlockSpec(memory_space=pltpu.SMEM),
            pl.BlockSpec((tq, 256), lambda n: (n, C_AQ // 256)),
            pl.BlockSpec((tq, 128), lambda n: (n, C_AK // 128)),
            pl.BlockSpec((SWA_BLK, 128), lambda n: (jnp.maximum(n * halo - 1, 0), C_AK // 128)),
            pl.BlockSpec((tq, 128), lambda n: (n, C_AV // 128)),
            pl.BlockSpec((SWA_BLK, 128), lambda n: (jnp.maximum(n * halo - 1, 0), C_AV // 128))]


def _swa_blocked(cur_ref, prev_ref, gs, nb):
    cur = cur_ref[:, gs].reshape(nb, SWA_BLK, 64)
    prev = jnp.concatenate([prev_ref[:, gs].reshape(1, SWA_BLK, 64), cur[:nb - 1]], axis=0) if nb > 1 \
        else prev_ref[:, gs].reshape(1, SWA_BLK, 64)
    return cur, prev


def _swa_fwd(h16, sinks):
    T = h16.shape[0]
    tq = SWA_TQ if T % SWA_TQ == 0 else SWA_BLK
    nb = tq // SWA_BLK

    def body(s_ref, q_ref, kc_ref, kp_ref, vc_ref, vp_ref, o_ref):
        n = pl.program_id(0)
        mask_c, mask_p = _swa_masks(n, nb)
        for h in range(4):
            hs = slice(h * 64, (h + 1) * 64)
            gs = slice(h // 2 * 64, (h // 2 + 1) * 64)
            kc, kp = _swa_blocked(kc_ref, kp_ref, gs, nb)
            vc, vp = _swa_blocked(vc_ref, vp_ref, gs, nb)
            pc, pp, _ = _swa_probs(q_ref[:, hs].reshape(nb, SWA_BLK, 64), kc, kp, s_ref[h], mask_c, mask_p)
            o_ref[:, hs] = (_bdot(pc.astype(BF), vc) + _bdot(pp.astype(BF), vp)).reshape(tq, 64)

    return pl.pallas_call(
        body, name="swa_fwd", grid=(T // tq,), in_specs=_swa_specs(tq),
        out_specs=pl.BlockSpec((tq, 256), lambda n: (n, 0)),
        out_shape=jax.ShapeDtypeStruct((T, 256), F32),
        compiler_params=_params(1))(sinks, h16, h16, h16, h16, h16)


def _swa_bwd(h16, sinks, dya):
    T = h16.shape[0]
    tq = SWA_TQ if T % SWA_TQ == 0 else SWA_BLK
    nb = tq // SWA_BLK

    def body(s_ref, q_ref, kc_ref, kp_ref, vc_ref, vp_ref, do_ref, dq_ref, dk_ref, dv_ref, ds_ref):
        n = pl.program_id(0)

        @pl.when(n == 0)
        def _():
            dk_ref[...] = jnp.zeros_like(dk_ref)
            dv_ref[...] = jnp.zeros_like(dv_ref)
            ds_ref[...] = jnp.zeros_like(ds_ref)

        mask_c, mask_p = _swa_masks(n, nb)
        rows = pl.ds(pl.multiple_of(n * tq, tq), tq)
        before = pl.ds(pl.multiple_of(jnp.maximum(n * nb - 1, 0) * SWA_BLK, SWA_BLK), SWA_BLK)
        lane = _iota((8, 128), 1)
        row8 = _iota((8, 128), 0)

        def to_keys(own, prev):
            if nb == 1:
                return own
            return own + jnp.concatenate([prev[1:], jnp.zeros((1, SWA_BLK, 64), F32)], axis=0)

        for h in range(4):
            hs = slice(h * 64, (h + 1) * 64)
            gs = slice(h // 2 * 64, (h // 2 + 1) * 64)
            q = q_ref[:, hs].reshape(nb, SWA_BLK, 64)
            kc, kp = _swa_blocked(kc_ref, kp_ref, gs, nb)
            vc, vp = _swa_blocked(vc_ref, vp_ref, gs, nb)
            pc, pp, ps = _swa_probs(q, kc, kp, s_ref[h], mask_c, mask_p)
            pcb, ppb = pc.astype(BF), pp.astype(BF)
            do = do_ref[:, hs].reshape(nb, SWA_BLK, 64)
            dob = do.astype(BF)
            o = _bdot(pcb, vc) + _bdot(ppb, vp)
            dd = jnp.sum(do * o, axis=-1, keepdims=True)
            dsc = (pc * (_bdot_nt(dob, vc) - dd) * SB_SCALE).astype(BF)
            dsp = (pp * (_bdot_nt(dob, vp) - dd) * SB_SCALE).astype(BF)
            dq_ref[:, hs] = (_bdot(dsc, kc) + _bdot(dsp, kp)).reshape(tq, 64).astype(BF)
            dkp, dvp = _bdot_tn(dsp, q), _bdot_tn(ppb, dob)
            dk_ref[rows, gs] += to_keys(_bdot_tn(dsc, q), dkp).reshape(tq, 64)
            dv_ref[rows, gs] += to_keys(_bdot_tn(pcb, dob), dvp).reshape(tq, 64)
            dk_ref[before, gs] += dkp[0]
            dv_ref[before, gs] += dvp[0]
            ds_ref[...] += jnp.where(jnp.logical_and(lane == h, row8 == 0), -jnp.sum(ps * dd), 0.0)

    return pl.pallas_call(
        body, name="swa_bwd", grid=(T // tq,),
        in_specs=_swa_specs(tq) + [pl.BlockSpec((tq, 256), lambda n: (n, 0))],
        out_specs=[pl.BlockSpec((tq, 256), lambda n: (n, 0)),
                   pl.BlockSpec((T, 128), lambda n: (0, 0)),
                   pl.BlockSpec((T, 128), lambda n: (0, 0)),
                   pl.BlockSpec((8, 128), lambda n: (0, 0))],
        out_shape=[jax.ShapeDtypeStruct((T, 256), BF), jax.ShapeDtypeStruct((T, 128), F32),
                   jax.ShapeDtypeStruct((T, 128), F32), jax.ShapeDtypeStruct((8, 128), F32)],
        compiler_params=_params(1))(sinks, h16, h16, h16, h16, h16, dya)


def _conv_u(bc_ref, bx_ref, bch_ref, bxh_ref, n, tm):
    u = bc_ref[...] * bx_ref[...]
    uh = bch_ref[...] * bxh_ref[...] * (n > 0).astype(F32)
    rowi = _iota((tm, 256), 0)
    u1 = jnp.where(rowi == 0, uh[7:8, :], pltpu.roll(u, 1, axis=0))
    u2 = jnp.where(rowi == 0, uh[6:7, :], jnp.where(rowi == 1, uh[7:8, :], pltpu.roll(u, 2, axis=0)))
    return u, u1, u2


def _conv_fwd(h32, cw, cb):
    T = h32.shape[0]
    tm = 512 if T % 512 == 0 else T
    hb = tm // 8

    def body(bb_ref, bc_ref, bx_ref, bch_ref, bxh_ref, w_ref, b_ref, o_ref):
        n = pl.program_id(0)
        u, u1, u2 = _conv_u(bc_ref, bx_ref, bch_ref, bxh_ref, n, tm)
        y = w_ref[0:1, :] * u2 + w_ref[1:2, :] * u1 + w_ref[2:3, :] * u + b_ref[...]
        o_ref[...] = bb_ref[...] * y

    halo = lambda c: pl.BlockSpec((8, 256), lambda n: (jnp.maximum(n * hb - 1, 0), c // 256))
    return pl.pallas_call(
        body, name="conv_fwd", grid=(T // tm,),
        in_specs=[pl.BlockSpec((tm, 256), lambda n: (n, C_BB // 256)),
                  pl.BlockSpec((tm, 256), lambda n: (n, C_BC // 256)),
                  pl.BlockSpec((tm, 256), lambda n: (n, C_BX // 256)),
                  halo(C_BC), halo(C_BX),
                  pl.BlockSpec((8, 256), lambda n: (0, 0)),
                  pl.BlockSpec((1, 256), lambda n: (0, 0))],
        out_specs=pl.BlockSpec((tm, 256), lambda n: (n, 0)),
        out_shape=jax.ShapeDtypeStruct((T, 256), F32),
        compiler_params=_params(1))(h32, h32, h32, h32, h32, cw, cb)


def _conv_bwd(h32, cw, cb, dyb):
    T = h32.shape[0]
    tm = 512 if T % 512 == 0 else T
    hb = tm // 8
    nt = T // tm

    def body(bb_ref, bc_ref, bx_ref, bch_ref, bxh_ref, bbn_ref, dy_ref, dyn_ref, w_ref, b_ref,
             dbb_ref, dbc_ref, dbx_ref, dw_ref):
        n = pl.program_id(0)

        @pl.when(n == 0)
        def _():
            dw_ref[...] = jnp.zeros_like(dw_ref)

        u, u1, u2 = _conv_u(bc_ref, bx_ref, bch_ref, bxh_ref, n, tm)
        w0, w1, w2 = w_ref[0:1, :], w_ref[1:2, :], w_ref[2:3, :]
        y = w0 * u2 + w1 * u1 + w2 * u + b_ref[...]
        dyb_ = dy_ref[...]
        dbb_ref[...] = (dyb_ * y).astype(BF)
        dy = dyb_ * bb_ref[...]
        dyn = dyn_ref[...] * bbn_ref[...] * (n < nt - 1).astype(F32)
        rowi = _iota((tm, 256), 0)
        dy1 = jnp.where(rowi == tm - 1, dyn[0:1, :], pltpu.roll(dy, tm - 1, axis=0))
        dy2 = jnp.where(rowi == tm - 2, dyn[0:1, :],
                        jnp.where(rowi == tm - 1, dyn[1:2, :], pltpu.roll(dy, tm - 2, axis=0)))
        du = w2 * dy + w1 * dy1 + w0 * dy2
        dbc_ref[...] = (du * bx_ref[...]).astype(BF)
        dbx_ref[...] = (du * bc_ref[...]).astype(BF)
        dw_ref[0:1, :] += jnp.sum(dy * u2, axis=0, keepdims=True)
        dw_ref[1:2, :] += jnp.sum(dy * u1, axis=0, keepdims=True)
        dw_ref[2:3, :] += jnp.sum(dy * u, axis=0, keepdims=True)
        dw_ref[3:4, :] += jnp.sum(dy, axis=0, keepdims=True)

    halo = lambda c: pl.BlockSpec((8, 256), lambda n: (jnp.maximum(n * hb - 1, 0), c // 256))
    nxt = lambda c: pl.BlockSpec((8, 256), lambda n: (jnp.minimum((n + 1) * hb, T // 8 - 1), c // 256))
    cur = lambda c: pl.BlockSpec((tm, 256), lambda n: (n, c // 256))
    return pl.pallas_call(
        body, name="conv_bwd", grid=(nt,),
        in_specs=[cur(C_BB), cur(C_BC), cur(C_BX), halo(C_BC), halo(C_BX), nxt(C_BB),
                  cur(0), nxt(0),
                  pl.BlockSpec((8, 256), lambda n: (0, 0)),
                  pl.BlockSpec((1, 256), lambda n: (0, 0))],
        out_specs=[cur(0), cur(0), cur(0), pl.BlockSpec((8, 256), lambda n: (0, 0))],
        out_shape=[jax.ShapeDtypeStruct((T, 256), BF)] * 3 + [jax.ShapeDtypeStruct((8, 256), F32)],
        compiler_params=_params(1))(h32, h32, h32, h32, h32, h32, dyb, dyb, cw, cb)


def _cprep_specs(tm):
    return [pl.BlockSpec((tm, 256), lambda n: (n, C_CQ // 256)),
            pl.BlockSpec((tm, 128), lambda n: (n, C_CKV // 128)),
            pl.BlockSpec((tm, 128), lambda n: (n, C_CKR // 128)),
            pl.BlockSpec((tm, 128), lambda n: (n, C_CKRS // 128)),
            pl.BlockSpec((1, 256), lambda n: (0, 0)),
            pl.BlockSpec((1, 128), lambda n: (0, 0)),
            pl.BlockSpec((tm, 128), lambda n: (n, 0)),
            pl.BlockSpec((tm, 128), lambda n: (n, 0))]


def _cprep_fwd(h32, gq, gkv, wuq2, wkv2, cosk, sin):
    T = h32.shape[0]
    tm = 512 if T % 512 == 0 else T

    def body(cq_ref, ckv_ref, ckr_ref, ckrs_ref, gq_ref, gkv_ref, cos_ref, sin_ref, wuq_ref, wkv_ref,
             q_ref, k_ref, v_ref):
        cosk_, sin_ = cos_ref[...], sin_ref[...]
        cosq = cosk_ + (_iota((tm, 128), 1) < 64).astype(F32)
        cqn, _ = _rms_fwd(cq_ref[...], gq_ref[...])
        q2 = _dot(cqn.astype(BF), wuq_ref[...])
        ckvn, _ = _rms_fwd(ckv_ref[...], gkv_ref[...])
        kv2 = _dot(ckvn.astype(BF), wkv_ref[...])
        kr = ckr_ref[...] * cosk_ + ckrs_ref[...] * sin_
        for h in range(4):
            hs = slice(h * 128, (h + 1) * 128)
            q_ref[:, hs] = ((q2[:, hs] * cosq + q2[:, 512 + h * 128:512 + (h + 1) * 128] * sin_) * MLA_SCALE).astype(BF)
            k_ref[:, hs] = (kv2[:, hs] + kr).astype(BF)
        ones = _iota((tm, 512), 1) % 128 == 64
        v_ref[...] = jnp.where(ones, 1.0, kv2[:, 512:]).astype(BF)

    return pl.pallas_call(
        body, name="cprep_fwd", grid=(T // tm,),
        in_specs=_cprep_specs(tm) + [pl.BlockSpec((256, 1024), lambda n: (0, 0)),
                                     pl.BlockSpec((128, 1024), lambda n: (0, 0))],
        out_specs=[pl.BlockSpec((tm, 512), lambda n: (n, 0))] * 3,
        out_shape=[jax.ShapeDtypeStruct((T, 512), BF)] * 3,
        compiler_params=_params(1))(h32, h32, h32, h32, gq, gkv, cosk, sin, wuq2, wkv2)


def _cprep_bwd(h32, gq, gkv, wuq2t, wkv2t, cosk, sin, dq, dk, dv):
    T = h32.shape[0]
    tm = 512 if T % 512 == 0 else T

    def body(cq_ref, ckv_ref, ckr_ref, ckrs_ref, gq_ref, gkv_ref, cos_ref, sin_ref, wuq_ref, wkv_ref,
             dq_ref, dk_ref, dv_ref,
             dcq_ref, dckv_ref, dckr_ref, dckrs_ref, dwuq_ref, dwkv_ref, dgq_ref, dgkv_ref):
        n = pl.program_id(0)

        @pl.when(n == 0)
        def _():
            dwuq_ref[...] = jnp.zeros_like(dwuq_ref)
            dwkv_ref[...] = jnp.zeros_like(dwkv_ref)
            dgq_ref[...] = jnp.zeros_like(dgq_ref)
            dgkv_ref[...] = jnp.zeros_like(dgkv_ref)

        cosk_, sin_ = cos_ref[...], sin_ref[...]
        cosq = cosk_ + (_iota((tm, 128), 1) < 64).astype(F32)
        dkr = jnp.zeros((tm, 128), F32)
        plain, swapped = [], []
        for h in range(4):
            hs = slice(h * 128, (h + 1) * 128)
            dqh = dq_ref[:, hs] * MLA_SCALE
            plain.append((dqh * cosq).astype(BF))
            swapped.append((dqh * sin_).astype(BF))
            dkr = dkr + dk_ref[:, hs]
        dq2 = jnp.concatenate(plain + swapped, axis=1)
        dkv2 = jnp.concatenate([dk_ref[...].astype(BF), dv_ref[...].astype(BF)], axis=1)
        dckr_ref[...] = (dkr * cosk_).astype(BF)
        dckrs_ref[...] = (dkr * sin_).astype(BF)

        cq, gq_ = cq_ref[...], gq_ref[...]
        cqn, rq = _rms_fwd(cq, gq_)
        dwuq_ref[...] += _dot_tn(cqn.astype(BF), dq2)
        dcq, dgt = _rms_bwd(cq, gq_, rq, _dot(dq2, wuq_ref[...]))
        dcq_ref[...] = dcq.astype(BF)
        dgq_ref[...] += jnp.sum(dgt, axis=0, keepdims=True)

        ckv, gkv_ = ckv_ref[...], gkv_ref[...]
        ckvn, rkv = _rms_fwd(ckv, gkv_)
        dwkv_ref[...] += _dot_tn(ckvn.astype(BF), dkv2)
        dckv, dgt2 = _rms_bwd(ckv, gkv_, rkv, _dot(dkv2, wkv_ref[...]))
        dckv_ref[...] = dckv.astype(BF)
        dgkv_ref[...] += jnp.sum(dgt2, axis=0, keepdims=True)

    row = lambda w: pl.BlockSpec((tm, w), lambda n: (n, 0))
    return pl.pallas_call(
        body, name="cprep_bwd", grid=(T // tm,),
        in_specs=_cprep_specs(tm) + [pl.BlockSpec((1024, 256), lambda n: (0, 0)),
                                     pl.BlockSpec((1024, 128), lambda n: (0, 0)),
                                     row(512), row(512), row(512)],
        out_specs=[row(256), row(128), row(128), row(128),
                   pl.BlockSpec((256, 1024), lambda n: (0, 0)), pl.BlockSpec((128, 1024), lambda n: (0, 0)),
                   pl.BlockSpec((1, 256), lambda n: (0, 0)), pl.BlockSpec((1, 128), lambda n: (0, 0))],
        out_shape=[jax.ShapeDtypeStruct((T, 256), BF), jax.ShapeDtypeStruct((T, 128), BF),
                   jax.ShapeDtypeStruct((T, 128), BF), jax.ShapeDtypeStruct((T, 128), BF),
                   jax.ShapeDtypeStruct((256, 1024), F32), jax.ShapeDtypeStruct((128, 1024), F32),
                   jax.ShapeDtypeStruct((1, 256), F32), jax.ShapeDtypeStruct((1, 128), F32)],
        compiler_params=_params(1))(h32, h32, h32, h32, gq, gkv, cosk, sin, wuq2t, wkv2t, dq, dk, dv)


MLA_TILE = 512
MLA_HEADS_PER_STEP = 4


def _causal_mask(t):
    return _iota((t, t), 1) <= _iota((t, t), 0)


def _mla_fwd(q, k, v):
    T = q.shape[0]
    tq = MLA_TILE

    def body(q_ref, k_ref, v_ref, o_ref, lse_ref):
        i = pl.program_id(1)
        mask = _causal_mask(tq)
        heads = [slice(128 * h, 128 * h + 128) for h in range(MLA_HEADS_PER_STEP)]
        qs = [q_ref[:, hs] for hs in heads]

        def step(j, carry, masked):
            rows = pl.ds(pl.multiple_of(j * tq, tq), tq)
            out = []
            for hh, hs in enumerate(heads):
                m, acc = carry[hh]
                s = _dot_nt(qs[hh], k_ref[rows, hs])
                if masked:
                    s = jnp.where(mask, s, NEG)
                m_new = jnp.maximum(m, jnp.max(s, axis=-1, keepdims=True))
                p = jnp.exp((s - m_new).astype(BF))
                acc = jnp.exp(m - m_new) * acc + _dot(p, v_ref[rows, hs])
                out.append((m_new, acc))
            return tuple(out)

        init = ((jnp.full((tq, 1), NEG, F32), jnp.zeros((tq, 128), F32)),) * len(heads)
        carry = lax.fori_loop(0, i, lambda j, c: step(j, c, False), init)
        carry = step(i, carry, True)
        for hh, hs in enumerate(heads):
            m, acc = carry[hh]
            l = acc[:, 64:65]
            o_ref[:, hs] = acc * (1.0 / l)
            lse_ref[:, hs] = jnp.broadcast_to(m + jnp.log(l), (tq, 128))

    width = 128 * MLA_HEADS_PER_STEP
    blk = pl.BlockSpec((tq, width), lambda h, i: (i, h))
    full = pl.BlockSpec((T, width), lambda h, i: (0, h))
    return pl.pallas_call(
        body, name="mla_fwd", grid=(4 // MLA_HEADS_PER_STEP, T // tq), in_specs=[blk, full, full],
        out_specs=[blk, blk],
        out_shape=[jax.ShapeDtypeStruct((T, 512), F32), jax.ShapeDtypeStruct((T, 512), F32)],
        compiler_params=_params(2))(q, k, v)


def _mla_bwd(q, k, v, o, lse, do):
    T = q.shape[0]
    tq = MLA_TILE

    def body(q_ref, k_ref, v_ref, o_ref, lse_ref, do_ref, dq_ref, dk_ref, dv_ref):
        i = pl.program_id(1)

        @pl.when(i == 0)
        def _():
            dk_ref[...] = jnp.zeros_like(dk_ref)
            dv_ref[...] = jnp.zeros_like(dv_ref)

        heads = [slice(0, 128), slice(128, 256)]
        mask = _causal_mask(tq)
        qs, dobs, dds, lses = [], [], [], []
        for hs in heads:
            do = do_ref[:, hs]
            qs.append(q_ref[:, hs])
            dobs.append(do.astype(BF))
            dds.append(jnp.sum(do * o_ref[:, hs], axis=-1, keepdims=True))
            lses.append(lse_ref[:, hs.start:hs.start + 1])

        def step(j, dqs, masked):
            rows = pl.ds(pl.multiple_of(j * tq, tq), tq)
            out = []
            for hh, hs in enumerate(heads):
                kj, vj = k_ref[rows, hs], v_ref[rows, hs]
                s = _dot_nt(qs[hh], kj)
                if masked:
                    s = jnp.where(mask, s, NEG)
                p = jnp.exp(s - lses[hh])
                ds = (p * (_dot_nt(dobs[hh], vj) - dds[hh])).astype(BF)
                dk_ref[rows, hs] += _dot_tn(ds, qs[hh])
                dv_ref[rows, hs] += _dot_tn(p.astype(BF), dobs[hh])
                out.append(dqs[hh] + _dot(ds, kj))
            return tuple(out)

        dqs = lax.fori_loop(0, i, lambda j, c: step(j, c, False), (jnp.zeros((tq, 128), F32),) * 2)
        dqs = step(i, dqs, True)
        for hh, hs in enumerate(heads):
            dq_ref[:, hs] = dqs[hh]

    blk = pl.BlockSpec((tq, 256), lambda h, i: (i, h))
    full = pl.BlockSpec((T, 256), lambda h, i: (0, h), pipeline_mode=pl.Buffered(1))
    return pl.pallas_call(
        body, name="mla_bwd", grid=(2, T // tq), in_specs=[blk, full, full, blk, blk, blk],
        out_specs=[blk, full, full],
        out_shape=[jax.ShapeDtypeStruct((T, 512), F32)] * 3,
        compiler_params=_params(2))(q, k, v, o, lse, do)


def _sb_tile(qk, rr, strict, masked, upper):
    z2 = qk * (SB_SCALE * LOG2E)
    l1 = jnp.log2(1.0 + jnp.exp2(-jnp.abs(z2)))
    lk = -jnp.maximum(z2, 0.0) - l1
    if masked:
        lk = jnp.where(strict, lk, 0.0)
    after = rr + _dot(lk.astype(BF), upper)
    ll = jnp.minimum(z2, 0.0) - l1
    a = jnp.exp2(ll + after)
    if masked:
        a = jnp.where(strict, a, 0.0)
    return ll, a, jnp.sum(lk, axis=-1, keepdims=True)


SB_TQ, SB_TK = 256, 256
SB_DEAD = -160.0


def _sb_walk(trips, one_step, carry):
    def alive(c):
        t, cr = c
        top = jnp.maximum(jnp.max(cr[0][0]), jnp.max(cr[1][0]))
        return jnp.logical_and(t < trips, top > SB_DEAD)

    def body(c):
        t, cr = c
        return t + 1, one_step(t, cr)

    return lax.while_loop(alive, body, (jnp.int32(0), carry))[1]


def _sb_consts(tq, tk):
    row, col = _iota((tq, tk), 0), _iota((tq, tk), 1)
    strict = [col + d * tk < row for d in range(tq // tk)]
    r2, c2 = _iota((tk, tk), 0), _iota((tk, tk), 1)
    return strict, (r2 > c2).astype(BF), (r2 < c2).astype(BF)


def _sb_fwd(h16):
    T = h16.shape[0]
    tq, tk = SB_TQ, SB_TK
    nd = tq // tk

    def body(q_ref, k_ref, v_ref, o_ref):
        i = pl.program_id(1)
        strict, upper, _ = _sb_consts(tq, tk)
        lane = _iota((tq, 128), 1)
        q2 = q_ref[...]
        qms = [jnp.where(lane < 64, q2, jnp.zeros_like(q2)), jnp.where(lane >= 64, q2, jnp.zeros_like(q2))]

        def step(j, carry, d):
            rows = pl.ds(pl.multiple_of(j * tk, tk), tk)
            kj, vj = k_ref[rows, :], v_ref[rows, :]
            out = []
            for hh in range(2):
                rr, acc = carry[hh]
                _, a, rs = _sb_tile(_dot_nt(qms[hh], kj), rr, None if d is None else strict[d], d is not None, upper)
                out.append((rr + rs, acc + _dot(a.astype(BF), vj)))
            return tuple(out)


        carry = ((jnp.zeros((tq, 1), F32), jnp.zeros((tq, 128), F32)),) * 2
        for d in reversed(range(nd)):
            carry = step(nd * i + d, carry, d)
        carry = _sb_walk(nd * i, lambda t, c: step(nd * i - 1 - t, c, None), carry)
        o_ref[...] = jnp.where(lane < 64, carry[0][1], carry[1][1])

    return pl.pallas_call(
        body, name="sb_fwd", grid=(2, T // tq),
        in_specs=[pl.BlockSpec((tq, 128), lambda p, i: (i, C_DQ // 128 + p)),
                  pl.BlockSpec((T, 128), lambda p, i: (0, C_DK // 128 + p)),
                  pl.BlockSpec((T, 128), lambda p, i: (0, C_DV // 128 + p))],
        out_specs=pl.BlockSpec((tq, 128), lambda p, i: (i, p)),
        out_shape=jax.ShapeDtypeStruct((T, 256), F32),
        compiler_params=_params(2))(h16, h16, h16)


def _sb_bwd(h16, yd, dyd):
    T = h16.shape[0]
    tq, tk = SB_TQ, SB_TK
    nd = tq // tk

    def body(q_ref, k_ref, v_ref, o_ref, do_ref, dq_ref, dk_ref, dv_ref):
        i = pl.program_id(1)

        @pl.when(i == 0)
        def _():
            dk_ref[...] = jnp.zeros_like(dk_ref)
            dv_ref[...] = jnp.zeros_like(dv_ref)

        strict, upper, before = _sb_consts(tq, tk)
        lane = _iota((tq, 128), 1)
        lane_k = _iota((tk, 128), 1)
        q2 = q_ref[...]
        dob2 = do_ref[...].astype(BF)
        doo = dob2.astype(F32) * o_ref[...]
        mines = [lane < 64, lane >= 64]
        qms = [jnp.where(m, q2, jnp.zeros_like(q2)) for m in mines]
        doms = [jnp.where(m, dob2, jnp.zeros_like(dob2)) for m in mines]
        dds = [jnp.sum(jnp.where(m, doo, 0.0), axis=-1, keepdims=True) for m in mines]

        def step(j, carry, d):
            rows = pl.ds(pl.multiple_of(j * tk, tk), tk)
            kj, vj = k_ref[rows, :], v_ref[rows, :]
            out, dks, dvs = [], [], []
            for hh in range(2):
                rr, sg, dq = carry[hh]
                ll, a, rs = _sb_tile(_dot_nt(qms[hh], kj), rr, None if d is None else strict[d], d is not None,
                                     upper)
                ab = a.astype(BF)
                g = _dot_nt(doms[hh], vj) * ab.astype(F32)
                gs = jnp.sum(g, axis=-1, keepdims=True)
                pre = (dds[hh] - sg - gs) + _dot(g.astype(BF), before)
                dz = g - jnp.exp2(ll) * (g + pre)
                if d is not None:
                    dz = jnp.where(strict[d], dz, 0.0)
                dzb = dz.astype(BF)
                dks.append(_dot_tn(dzb, q2))
                dvs.append(_dot_tn(ab, dob2))
                out.append((rr + rs, sg + gs, dq + _dot(dzb, kj)))
            dk_ref[rows, :] += jnp.where(lane_k < 64, dks[0], dks[1]) * SB_SCALE
            dv_ref[rows, :] += jnp.where(lane_k < 64, dvs[0], dvs[1])
            return tuple(out)


        zero = jnp.zeros((tq, 1), F32)
        carry = ((zero, zero, jnp.zeros((tq, 128), F32)),) * 2
        for d in reversed(range(nd)):
            carry = step(nd * i + d, carry, d)
        carry = _sb_walk(nd * i, lambda t, c: step(nd * i - 1 - t, c, None), carry)
        dq_ref[...] = jnp.where(lane < 64, carry[0][2], carry[1][2]) * SB_SCALE

    blk = lambda c: pl.BlockSpec((tq, 128), lambda p, i: (i, c // 128 + p))
    full = lambda c: pl.BlockSpec((T, 128), lambda p, i: (0, c // 128 + p))
    return pl.pallas_call(
        body, name="sb_bwd", grid=(2, T // tq),
        in_specs=[blk(C_DQ), full(C_DK), full(C_DV), blk(0), blk(0)],
        out_specs=[blk(0), full(0), full(0)],
        out_shape=[jax.ShapeDtypeStruct((T, 256), F32)] * 3,
        compiler_params=_params(2))(h16, h16, h16, yd, dyd)


def _compact_c(ycp):
    return jnp.concatenate([ycp[:, h * 128:h * 128 + 64] for h in range(4)], axis=1)


def _post_fwd(ya, yb, ycp, yd, h32, ggrp, wout, gpost, x):
    T = x.shape[0]
    tm = 256

    def body(ya_ref, yb_ref, yc_ref, yd_ref, gate_ref, gg_ref, w_ref, gp_ref, x_ref, xn_ref, ym_ref, o_ref):
        ys = [ya_ref[...], yb_ref[...], _compact_c(yc_ref[...]), yd_ref[...]]
        gate = gate_ref[...]
        sil = gate * (1.0 / (1.0 + jnp.exp(-gate)))
        parts = []
        for gi in range(4):
            ng, _ = _rms_fwd(ys[gi], gg_ref[:, gi * 256:(gi + 1) * 256])
            parts.append(ng * sil[:, gi * 256:(gi + 1) * 256])
        ym = jnp.concatenate(parts, axis=1).astype(BF)
        ym_ref[...] = ym
        o = _dot(ym, w_ref[...])
        o_ref[...] = o
        on, _ = _rms_fwd(o, gp_ref[...])
        xn_ref[...] = x_ref[...] + on

    row = lambda w: pl.BlockSpec((tm, w), lambda n: (n, 0))
    vec = pl.BlockSpec((1, 1024), lambda n: (0, 0))
    return pl.pallas_call(
        body, name="post_fwd", grid=(T // tm,),
        in_specs=[row(256), row(256), row(512), row(256), pl.BlockSpec((tm, 1024), lambda n: (n, C_GATE // 1024)),
                  vec, pl.BlockSpec((1024, 1024), lambda n: (0, 0)), vec, row(1024)],
        out_specs=[row(1024), row(1024), row(1024)],
        out_shape=[jax.ShapeDtypeStruct((T, 1024), F32), jax.ShapeDtypeStruct((T, 1024), BF),
                   jax.ShapeDtypeStruct((T, 1024), F32)],
        compiler_params=_params(1))(ya, yb, ycp, yd, h32, ggrp, wout, gpost, x)


def _post_bwd(dx, o, gpost, woutt, ya, yb, ycp, yd, h32, ggrp, ym):
    T = dx.shape[0]
    tm = 256

    def body(dx_ref, o_ref, gp_ref, w_ref, ya_ref, yb_ref, yc_ref, yd_ref, gate_ref, gg_ref, ym_ref,
             dw_ref, dya_ref, dyb_ref, dyc_ref, dyd_ref, dgate_ref, dgp_ref, dgg_ref):
        n = pl.program_id(0)

        @pl.when(n == 0)
        def _():
            dw_ref[...] = jnp.zeros_like(dw_ref)
            dgp_ref[...] = jnp.zeros_like(dgp_ref)
            dgg_ref[...] = jnp.zeros_like(dgg_ref)

        ov, gp = o_ref[...], gp_ref[...]
        _, ro = _rms_fwd(ov, gp)
        do, dgt = _rms_bwd(ov, gp, ro, dx_ref[...])
        dgp_ref[...] += jnp.sum(dgt, axis=0, keepdims=True)
        dob = do.astype(BF)
        dw_ref[...] += _dot_tn(ym_ref[...], dob)
        dym = _dot(dob, w_ref[...])
        gate = gate_ref[...]
        sg = 1.0 / (1.0 + jnp.exp(-gate))
        sil = gate * sg
        dsil = sg * (1.0 + gate * (1.0 - sg))
        ys = [ya_ref[...], yb_ref[...], _compact_c(yc_ref[...]), yd_ref[...]]
        dys = []
        for gi in range(4):
            gs = slice(gi * 256, (gi + 1) * 256)
            gg = gg_ref[:, gs]
            ng, rg = _rms_fwd(ys[gi], gg)
            dgate_ref[:, gs] = (dym[:, gs] * ng * dsil[:, gs]).astype(BF)
            dy, dgt2 = _rms_bwd(ys[gi], gg, rg, dym[:, gs] * sil[:, gs])
            dgg_ref[:, gs] += jnp.sum(dgt2, axis=0, keepdims=True)
            dys.append(dy)
        dya_ref[...] = dys[0]
        dyb_ref[...] = dys[1]
        dyd_ref[...] = dys[3]
        z64 = jnp.zeros((tm, 64), F32)
        dyc_ref[...] = jnp.concatenate(
            [piece for h in range(4) for piece in (dys[2][:, h * 64:(h + 1) * 64], z64)], axis=1)

    row = lambda w: pl.BlockSpec((tm, w), lambda n: (n, 0))
    vec = pl.BlockSpec((1, 1024), lambda n: (0, 0))
    return pl.pallas_call(
        body, name="post_bwd", grid=(T // tm,),
        in_specs=[row(1024), row(1024), vec, pl.BlockSpec((1024, 1024), lambda n: (0, 0)),
                  row(256), row(256), row(512), row(256),
                  pl.BlockSpec((tm, 1024), lambda n: (n, C_GATE // 1024)), vec, row(1024)],
        out_specs=[pl.BlockSpec((1024, 1024), lambda n: (0, 0)), row(256), row(256), row(512), row(256), row(1024),
                   vec, vec],
        out_shape=[jax.ShapeDtypeStruct((1024, 1024), F32), jax.ShapeDtypeStruct((T, 256), F32),
                   jax.ShapeDtypeStruct((T, 256), F32), jax.ShapeDtypeStruct((T, 512), F32),
                   jax.ShapeDtypeStruct((T, 256), F32), jax.ShapeDtypeStruct((T, 1024), BF),
                   jax.ShapeDtypeStruct((1, 1024), F32), jax.ShapeDtypeStruct((1, 1024), F32)],
        compiler_params=_params(1))(dx, o, gpost, woutt, ya, yb, ycp, yd, h32, ggrp, ym)


def _loss_head(y, tgt):
    T = y.shape[0]
    tm = 512 if T % 512 == 0 else T

    def body(y_ref, t_ref, s_ref, dy_ref):
        n = pl.program_id(0)

        @pl.when(n == 0)
        def _():
            s_ref[...] = jnp.zeros_like(s_ref)

        d = y_ref[...] - t_ref[...]
        s_ref[...] += jnp.sum(d * d, axis=0, keepdims=True)
        dy_ref[...] = d * (1.0 / D_MODEL)

    row = pl.BlockSpec((tm, 1024), lambda n: (n, 0))
    return pl.pallas_call(
        body, name="loss_head", grid=(T // tm,), in_specs=[row, row],
        out_specs=[pl.BlockSpec((1, 1024), lambda n: (0, 0)), row],
        out_shape=[jax.ShapeDtypeStruct((1, 1024), F32), jax.ShapeDtypeStruct((T, 1024), F32)],
        compiler_params=_params(1))(y, tgt)


def _swap_rows32(a):
    return jnp.concatenate([a[16:32], a[0:16]], axis=0)


def _pad_w_uq(w):
    z = lambda n: jnp.zeros((w.shape[0], n), w.dtype)
    a = [p for h in range(4) for p in (w[:, 96 * h:96 * h + 96], z(32))]
    b = [p for h in range(4) for p in (z(64), _swap32(w[:, 96 * h + 64:96 * h + 96]), z(32))]
    return jnp.concatenate(a + b, axis=1)


def _unpad_w_uq(d):
    out = []
    for h in range(4):
        out.append(d[:, 128 * h:128 * h + 64])
        out.append(d[:, 128 * h + 64:128 * h + 96] + _swap32(d[:, 512 + 128 * h + 64:512 + 128 * h + 96]))
    return jnp.concatenate(out, axis=1)


def _pad_w_ukv(w):
    z = jnp.zeros((w.shape[0], 64), w.dtype)
    a = [p for h in range(4) for p in (w[:, 128 * h:128 * h + 64], z)]
    b = [p for h in range(4) for p in (w[:, 128 * h + 64:128 * h + 128], z)]
    return jnp.concatenate(a + b, axis=1)


def _unpad_w_ukv(d):
    return jnp.concatenate([p for h in range(4) for p in (d[:, 128 * h:128 * h + 64],
                                                          d[:, 512 + 128 * h:512 + 128 * h + 64])], axis=1)


def _rope_tables(pos):
    freqs = 10000.0 ** (-jnp.arange(16, dtype=F32) / 16)
    ang = pos.astype(F32)[:, None] * freqs
    c, s = jnp.cos(ang), jnp.sin(ang)
    z = lambda n: jnp.zeros((pos.shape[0], n), F32)
    return (jnp.concatenate([z(64), c, c, z(32)], axis=1), jnp.concatenate([z(64), -s, s, z(32)], axis=1))


def _layer_weights(W, l):
    wuq2 = _pad_w_uq(W["mla_w_uq"][l])
    wkv2 = _pad_w_ukv(W["mla_w_ukv"][l])
    wout = W["w_out"][l]
    cw = jnp.concatenate([W["conv_w"][l].astype(F32), jnp.zeros((5, 256), F32)], axis=0)
    return dict(
        wpt=W["wpt"][l], wuq2=wuq2.astype(BF), wuq2t=wuq2.T.astype(BF),
        wkv2=wkv2.astype(BF), wkv2t=wkv2.T.astype(BF), wout=wout.astype(BF), woutt=wout.T.astype(BF),
        cw=cw, cb=W["conv_b"][l][None, :], sinks=W["attn_sinks"][l],
        gpre=W["norm_pre"][l][None, :], gq=W["mla_q_norm"][l][None, :], gkv=W["mla_kv_norm"][l][None, :],
        ggrp=W["group_norm"][l][None, :], gpost=W["norm_post"][l][None, :])


def _local_step(x, pos, W, tgt):
    cosk, sin = _rope_tables(pos)
    saved = []
    for l in range(DEPTH):
        lw = _layer_weights(W, l)
        h32, h16, xn = _inproj_fwd(x, lw["gpre"], lw["wpt"])
        ya = _swa_fwd(h16, lw["sinks"])
        yb = _conv_fwd(h32, lw["cw"], lw["cb"])
        qc, kc, vc = _cprep_fwd(h32, lw["gq"], lw["gkv"], lw["wuq2"], lw["wkv2"], cosk, sin)
        ycp, lse = _mla_fwd(qc, kc, vc)
        yd = _sb_fwd(h16)
        x_new, ym, o = _post_fwd(ya, yb, ycp, yd, h32, lw["ggrp"], lw["wout"], lw["gpost"], x)
        saved.append(dict(lw=lw, x=x, h32=h32, h16=h16, xn=xn, ya=ya, yb=yb, qc=qc, kc=kc, vc=vc, ycp=ycp,
                          lse=lse, yd=yd, ym=ym, o=o))
        x = x_new
    sq, dx = _loss_head(x, tgt)

    grads = {k: [None] * DEPTH for k in ("norm_pre", "w_in_pt", "attn_sinks", "conv_w", "conv_b", "mla_q_norm",
                                         "mla_w_uq", "mla_kv_norm", "mla_w_ukv", "group_norm", "w_out",
                                         "norm_post")}
    for l in reversed(range(DEPTH)):
        s = saved[l]
        lw = s["lw"]
        dwout, dya, dyb, dycp, dyd, dgate, dgpost, dggrp = _post_bwd(
            dx, s["o"], lw["gpost"], lw["woutt"], s["ya"], s["yb"], s["ycp"], s["yd"], s["h32"], lw["ggrp"], s["ym"])
        grads["norm_post"][l] = dgpost[0]
        grads["group_norm"][l] = dggrp[0]
        grads["w_out"][l] = dwout
        sdq, sdk, sdv = _sb_bwd(s["h16"], s["yd"], dyd)
        mdq, mdk, mdv = _mla_bwd(s["qc"], s["kc"], s["vc"], s["ycp"], s["lse"], dycp)
        dcq, dckv, dckr, dckrs, dwuq2, dwkv2, dgq, dgkv = _cprep_bwd(
            s["h32"], lw["gq"], lw["gkv"], lw["wuq2t"], lw["wkv2t"], cosk, sin, mdq, mdk, mdv)
        grads["mla_q_norm"][l] = dgq[0]
        grads["mla_kv_norm"][l] = dgkv[0]
        grads["mla_w_uq"][l] = _unpad_w_uq(dwuq2)
        grads["mla_w_ukv"][l] = _unpad_w_ukv(dwkv2)
        dbb, dbc, dbx, dcw = _conv_bwd(s["h32"], lw["cw"], lw["cb"], dyb)
        grads["conv_w"][l] = dcw[0:3]
        grads["conv_b"][l] = dcw[3]
        adq, adk, adv, dsk = _swa_bwd(s["h16"], lw["sinks"], dya)
        grads["attn_sinks"][l] = dsk[0, 0:4]
        parts = [dgate, adq, adk, adv, dbb, dbc, dbx, dcq, dckv, dckr, dckrs, sdq, sdk, sdv]
        dwp, dx, dgpre = _inproj_bwd(parts, lw["wpt"], s["x"], s["xn"], lw["gpre"], dx)
        grads["w_in_pt"][l] = dwp.T
        grads["norm_pre"][l] = dgpre[0]
    return sq, dx, grads


SMALL_SHARDED = ("conv_w", "mla_w_uq", "mla_w_ukv")
REPLICATED = ("norm_pre", "attn_sinks", "conv_b", "mla_q_norm", "mla_kv_norm", "group_norm", "norm_post")
ORDER = ("norm_pre", "w_in", "attn_sinks", "conv_w", "conv_b", "mla_q_norm", "mla_w_uq", "mla_kv_norm",
         "mla_w_ukv", "group_norm", "w_out", "norm_post")
W_IN_COLS = 436
W_IN_WIN = 440
SMALL_ROWS = 48


def _pack_small(arrs, dtype):
    flat = jnp.concatenate([a.reshape(-1).astype(dtype) for a in arrs])
    flat = jnp.concatenate([flat, jnp.zeros((SMALL_ROWS * D_MODEL - flat.shape[0],), dtype)])
    return flat.reshape(SMALL_ROWS, D_MODEL)


def _pack_state(ps, c):
    k = len(ps)
    wt = jnp.transpose(jnp.stack([p["w_in"] for p in ps]), (0, 1, 3, 2))
    win = lax.dynamic_update_slice(jnp.zeros((k, DEPTH, W_IN_WIN, D_MODEL), F32), wt, (0, 0, 4 * c, 0))
    wout = jnp.stack([p["w_out"] for p in ps]).reshape(k, DEPTH * 128, D_MODEL)
    flat = jnp.stack([jnp.concatenate([p[n].reshape(-1) for n in SMALL_SHARDED + REPLICATED]) for p in ps])
    small = jnp.pad(flat, ((0, 0), (0, SMALL_ROWS * D_MODEL - flat.shape[1]))).reshape(k, SMALL_ROWS, D_MODEL)
    return jnp.concatenate([win.reshape(k, DEPTH * W_IN_WIN, D_MODEL), wout, small], axis=1)


def _unpack_state(buf, p, c):
    k = buf.shape[0]
    nw = DEPTH * W_IN_WIN
    win = lax.dynamic_slice(buf[:, 0:nw].reshape(k, DEPTH, W_IN_WIN, D_MODEL), (0, 0, 4 * c, 0),
                            (k, DEPTH, W_IN_COLS, D_MODEL))
    out = {"w_in": jnp.transpose(win, (0, 1, 3, 2)),
           "w_out": buf[:, nw:nw + DEPTH * 128].reshape(k, DEPTH, 128, D_MODEL)}
    flat = buf[:, nw + DEPTH * 128:].reshape(k, SMALL_ROWS * D_MODEL)
    off = 0
    for n in SMALL_SHARDED + REPLICATED:
        size = int(np.prod(p[n].shape))
        out[n] = flat[:, off:off + size].reshape((k,) + p[n].shape)
        off += size
    return out


def _rows_of_w_in_t(lo, hi, padded, kr):
    segs = ((0, 1664, padded, C_AQ), (1664, 1696, kr, 0), (1696, 2464, padded, C_DQ), (2464, 3488, padded, C_GATE))
    out = []
    for s0, s1, src, base in segs:
        a, b = max(lo, s0), min(hi, s1)
        if a < b:
            out.append(src[base + a - s0:base + b - s0])
    return out


def _me():
    return lax.axis_index("x"), lax.axis_index("y"), lax.axis_index("c")


def _all_gather(block):
    R, C = block.shape

    def body(src_ref, out_ref, send_sems, recv_sems, local_sem):
        x, y, c = _me()
        me, sibling = (x, y, c), (x, y, 1 - c)
        chips = [(1 - x, y), (x, 1 - y), (1 - x, 1 - y)]

        def slot(px, py, pc):
            return out_ref.at[4 * px + 2 * py + pc]

        def copy(k, block, to, src=None):
            return pltpu.make_async_remote_copy(
                src_ref=slot(*block) if src is None else src, dst_ref=slot(*block), send_sem=send_sems.at[k],
                recv_sem=recv_sems.at[k], device_id=to, device_id_type=MESH)

        mine = pltpu.make_async_copy(src_ref, slot(*me), local_sem)
        mine.start()
        first = [copy(0, me, sibling, src=src_ref)]
        first += [copy(1 + j, me, (*chip, c), src=src_ref) for j, chip in enumerate(chips)]
        for cp in first:
            cp.start()
        passed = [copy(4 + j, (*chip, c), sibling) for j, chip in enumerate(chips)]
        for j, chip in enumerate(chips):
            copy(1 + j, (*chip, c), me).wait_recv()
            passed[j].start()
        copy(0, sibling, me).wait_recv()
        for j, chip in enumerate(chips):
            copy(4 + j, (*chip, 1 - c), me).wait_recv()
        for cp in first + passed:
            cp.wait_send()
        mine.wait()

    return pl.pallas_call(
        body, name="all_gather", out_shape=jax.ShapeDtypeStruct((N_DEV, R, C), block.dtype),
        in_specs=[pl.BlockSpec(memory_space=pl.ANY)], out_specs=pl.BlockSpec(memory_space=pl.ANY),
        scratch_shapes=[pltpu.SemaphoreType.DMA((N_DEV - 1,)), pltpu.SemaphoreType.DMA((N_DEV - 1,)),
                        pltpu.SemaphoreType.DMA])(block)


N_CHIP = 4


def _sibling_swap(blocks):
    _, R, C = blocks.shape

    def body(src_ref, out_ref, send_sems, recv_sems):
        x, y, c = _me()
        copies = [pltpu.make_async_remote_copy(
            src_ref=src_ref.at[2 * j + 1 - c], dst_ref=out_ref.at[j], send_sem=send_sems.at[j],
            recv_sem=recv_sems.at[j], device_id=(x, y, 1 - c), device_id_type=MESH) for j in range(N_CHIP)]
        for cp in copies:
            cp.start()
        for cp in copies:
            cp.wait()

    return pl.pallas_call(
        body, name="sibling_swap", out_shape=jax.ShapeDtypeStruct((N_CHIP, R, C), blocks.dtype),
        in_specs=[pl.BlockSpec(memory_space=pl.ANY)], out_specs=pl.BlockSpec(memory_space=pl.ANY),
        scratch_shapes=[pltpu.SemaphoreType.DMA((N_CHIP,)), pltpu.SemaphoreType.DMA((N_CHIP,))])(blocks)


def _pair_sum(a, b):
    n, R, C = a.shape
    tr = 592 if R % 592 == 0 else R

    def body(a_ref, b_ref, o_ref):
        o_ref[...] = (a_ref[...].astype(F32) + b_ref[...].astype(F32)).astype(BF)

    spec = pl.BlockSpec((1, tr, C), lambda j, r: (j, r, 0))
    return pl.pallas_call(body, name="pair_sum", grid=(n, R // tr), in_specs=[spec, spec], out_specs=spec,
                          out_shape=jax.ShapeDtypeStruct(a.shape, BF), compiler_params=_params(2))(a, b)


def _chip_exchange(sums):
    _, R, C = sums.shape

    def body(src_ref, out_ref, send_sems, recv_sems, local_sem):
        x, y, c = _me()
        here = 2 * x + y
        mine = pltpu.make_async_copy(src_ref.at[here], out_ref.at[here], local_sem)
        mine.start()
        copies = []
        for k in range(1, N_CHIP):
            px, py = x ^ (k >> 1), y ^ (k & 1)
            copies.append(pltpu.make_async_remote_copy(
                src_ref=src_ref.at[2 * px + py], dst_ref=out_ref.at[here], send_sem=send_sems.at[k - 1],
                recv_sem=recv_sems.at[k - 1], device_id=(px, py, c), device_id_type=MESH))
        for cp in copies:
            cp.start()
        for cp in copies:
            cp.wait()
        mine.wait()

    return pl.pallas_call(
        body, name="chip_exchange", out_shape=jax.ShapeDtypeStruct((N_CHIP, R, C), sums.dtype),
        in_specs=[pl.BlockSpec(memory_space=pl.ANY)], out_specs=pl.BlockSpec(memory_space=pl.ANY),
        scratch_shapes=[pltpu.SemaphoreType.DMA((N_CHIP - 1,)), pltpu.SemaphoreType.DMA((N_CHIP - 1,)),
                        pltpu.SemaphoreType.DMA])(sums)


def _adamw(parts, state):
    _, R, C = state.shape
    n_parts = parts.shape[0]
    tr = 32
    assert R % tr == 0

    def body(p_ref, s_ref, o_ref):
        g = p_ref[0].astype(F32)
        for k in range(1, n_parts):
            g = g + p_ref[k].astype(F32)
        o_ref[0] = g
        m_ = ADAM_B1 * s_ref[1] + (1.0 - ADAM_B1) * g
        v_ = ADAM_B2 * s_ref[2] + (1.0 - ADAM_B2) * (g * g)
        o_ref[2] = m_
        o_ref[3] = v_
        m_hat = m_ / (1.0 - ADAM_B1 ** ADAM_STEP)
        v_hat = v_ / (1.0 - ADAM_B2 ** ADAM_STEP)
        o_ref[1] = -ADAM_LR * (m_hat / (jnp.sqrt(v_hat) + ADAM_EPS) + ADAM_WD * s_ref[0])

    return pl.pallas_call(
        body, name="adamw", grid=(R // tr,),
        in_specs=[pl.BlockSpec((n_parts, tr, C), lambda n: (0, n, 0)), pl.BlockSpec((3, tr, C), lambda n: (0, n, 0))],
        out_specs=pl.BlockSpec((4, tr, C), lambda n: (0, n, 0)), out_shape=jax.ShapeDtypeStruct((4, R, C), F32),
        compiler_params=_params(1))(parts, state)


def kernel(x, positions, norm_pre, w_in, attn_sinks, conv_w, conv_b, mla_q_norm, mla_w_uq, mla_kv_norm, mla_w_ukv, group_norm, w_out, norm_post, loss_target, m_norm_pre, m_w_in, m_attn_sinks, m_conv_w, m_conv_b, m_mla_q_norm, m_mla_w_uq, m_mla_kv_norm, m_mla_w_ukv, m_group_norm, m_w_out, m_norm_post, v_norm_pre, v_w_in, v_attn_sinks, v_conv_w, v_conv_b, v_mla_q_norm, v_mla_w_uq, v_mla_kv_norm, v_mla_w_ukv, v_group_norm, v_w_out, v_norm_post):
    local = dict(norm_pre=norm_pre, w_in=w_in, attn_sinks=attn_sinks, conv_w=conv_w, conv_b=conv_b,
                 mla_q_norm=mla_q_norm, mla_w_uq=mla_w_uq, mla_kv_norm=mla_kv_norm, mla_w_ukv=mla_w_ukv,
                 group_norm=group_norm, w_out=w_out, norm_post=norm_post)
    mom = dict(norm_pre=m_norm_pre, w_in=m_w_in, attn_sinks=m_attn_sinks, conv_w=m_conv_w, conv_b=m_conv_b,
               mla_q_norm=m_mla_q_norm, mla_w_uq=m_mla_w_uq, mla_kv_norm=m_mla_kv_norm, mla_w_ukv=m_mla_w_ukv,
               group_norm=m_group_norm, w_out=m_w_out, norm_post=m_norm_post)
    vel = dict(norm_pre=v_norm_pre, w_in=v_w_in, attn_sinks=v_attn_sinks, conv_w=v_conv_w, conv_b=v_conv_b,
               mla_q_norm=v_mla_q_norm, mla_w_uq=v_mla_w_uq, mla_kv_norm=v_mla_kv_norm, mla_w_ukv=v_mla_w_ukv,
               group_norm=v_group_norm, w_out=v_w_out, norm_post=v_norm_post)

    c = lax.axis_index("c")

    wt = jnp.transpose(w_in, (0, 2, 1)).astype(BF)
    wt = jnp.concatenate([wt, jnp.zeros((DEPTH, 448 - W_IN_COLS, D_MODEL), BF)], axis=1)
    payload = jnp.concatenate([wt.reshape(DEPTH * 448, D_MODEL), w_out.astype(BF).reshape(DEPTH * 128, D_MODEL),
                               _pack_small([local[n] for n in SMALL_SHARDED], BF)], axis=0)
    gathered = _all_gather(payload)
    W = {n: local[n] for n in REPLICATED}

    def nat_rows(l, lo, hi):
        out, r = [], lo
        while r < hi:
            d = r // W_IN_COLS
            e = min(hi, (d + 1) * W_IN_COLS)
            out.append(gathered[d, 448 * l + r - W_IN_COLS * d:448 * l + e - W_IN_COLS * d])
            r = e
        return out

    z = lambda n: [jnp.zeros((n, D_MODEL), BF)]
    W["wpt"] = [jnp.concatenate(nat_rows(l, 2464, 3488) + nat_rows(l, 0, 1664) + z(64) + nat_rows(l, 1664, 1696)
                                + z(96) + nat_rows(l, 1680, 1696) + nat_rows(l, 1664, 1680) + z(32)
                                + nat_rows(l, 1696, 2464) + z(NP - C_END), axis=0) for l in range(DEPTH)]
    wo0 = DEPTH * 448
    W["w_out"] = gathered[:, wo0:wo0 + DEPTH * 128].reshape(N_DEV, DEPTH, 128, D_MODEL).transpose(1, 0, 2, 3).reshape(
        DEPTH, D_MODEL, D_MODEL)
    flat = gathered[:, wo0 + DEPTH * 128:].reshape(N_DEV, SMALL_ROWS * D_MODEL)
    off = 0
    for n in SMALL_SHARDED:
        depth, rows, width = local[n].shape
        size = depth * rows * width
        W[n] = flat[:, off:off + size].reshape(N_DEV, depth, rows, width).transpose(1, 2, 0, 3).reshape(
            depth, rows, N_DEV * width)
        off += size

    sq, grad_x, g = _local_step(x[0], positions[0], W, loss_target[0])
    loss = lax.psum(0.5 / D_MODEL * jnp.sum(sq), ("x", "y", "c"))

    cols = []
    for n in SMALL_SHARDED:
        depth, rows, width = local[n].shape
        cols.append(jnp.stack(g[n]).reshape(depth, rows, N_DEV, width).transpose(2, 0, 1, 3).reshape(N_DEV, -1))
    rep = jnp.concatenate([a.reshape(-1) for n in REPLICATED for a in g[n]])
    cols.append(jnp.broadcast_to(rep[None], (N_DEV, rep.shape[0])))
    small = jnp.concatenate(cols, axis=1)
    small = jnp.pad(small, ((0, 0), (0, SMALL_ROWS * D_MODEL - small.shape[1]))).reshape(N_DEV, SMALL_ROWS, D_MODEL)
    krs = [p[C_CKR + 64:C_CKR + 96] + _swap_rows32(p[C_CKRS + 64:C_CKRS + 96]) for p in g["w_in_pt"]]
    pieces = []
    for d in range(N_DEV):
        lo = W_IN_COLS * d // 8 * 8
        for l in range(DEPTH):
            pieces += _rows_of_w_in_t(lo, lo + W_IN_WIN, g["w_in_pt"][l], krs[l])
        pieces += [g["w_out"][l][128 * d:128 * (d + 1)] for l in range(DEPTH)]
        pieces.append(small[d])
    blocks = jnp.concatenate(pieces, axis=0).astype(BF).reshape(N_DEV, -1, D_MODEL)
    mine = lax.dynamic_index_in_dim(blocks.reshape(N_CHIP, 2, -1, D_MODEL), c, axis=1, keepdims=False)
    received = _chip_exchange(_pair_sum(mine, _sibling_swap(blocks)))

    out = _unpack_state(_adamw(received, _pack_state([local, mom, vel], c)), local, c)
    return (loss, grad_x[None], *[out[n][t] for t in range(4) for n in ORDER])
```

```python
import functools

import jax
import jax.numpy as jnp
import numpy as np
from jax import lax
from jax.experimental import pallas as pl
from jax.experimental.pallas import tpu as pltpu

F32 = jnp.float32
BF = jnp.bfloat16
MESH = pl.DeviceIdType.MESH

D_MODEL = 1024
DEPTH = 2
EPS = 1e-6
N_DEV = 8
VMEM_LIMIT = 56 * 1024 * 1024
NEG = -1e30
MLA_SCALE = 96.0 ** -0.5
SB_SCALE = 0.125
LOG2E = 1.4426950408889634

NP = 3840
C_GATE = 0
C_AQ = 1024
C_AK = 1280
C_AV = 1408
C_BB = 1536
C_BC = 1792
C_BX = 2048
C_CQ = 2304
C_CKV = 2560
C_CKR = 2688
C_CKRS = 2816
C_DQ = 2944
C_DK = 3200
C_DV = 3456
C_END = 3712

def _swap32(a):
    return jnp.concatenate([a[:, 16:32], a[:, 0:16]], axis=1)

ADAM_LR, ADAM_B1, ADAM_B2, ADAM_EPS, ADAM_WD, ADAM_STEP = 0.001, 0.9, 0.999, 1e-08, 0.01, 10


def _dot(a, b):
    return jnp.dot(a, b, preferred_element_type=F32)


def _dot_nt(a, b):
    return lax.dot_general(a, b, (((1,), (1,)), ((), ())), preferred_element_type=F32)


def _dot_tn(a, b):
    return lax.dot_general(a, b, (((0,), (0,)), ((), ())), preferred_element_type=F32)


def _params(n_grid):
    return pltpu.CompilerParams(dimension_semantics=("arbitrary",) * n_grid, vmem_limit_bytes=VMEM_LIMIT)


def _rms_fwd(x, g):
    r = lax.rsqrt(jnp.mean(x * x, axis=-1, keepdims=True) + EPS)
    return (x * r) * g, r


def _rms_bwd(x, g, r, dy, width=None):
    n = x.shape[-1] if width is None else width
    u = dy * g
    dx = r * u - x * (r * r * r) * (jnp.sum(x * u, axis=-1, keepdims=True) / n)
    return dx, dy * (x * r)


def _iota(shape, axis):
    return lax.broadcasted_iota(jnp.int32, shape, axis)


def _inproj_fwd(x, g, wpt):
    T = x.shape[0]
    tm = 256

    def body(x_ref, g_ref, w_ref, h32_ref, h16_ref, xn_ref):
        xn, _ = _rms_fwd(x_ref[...], g_ref[...])
        xn = xn.astype(BF)
        xn_ref[...] = xn
        h = _dot_nt(xn, w_ref[...])
        h32_ref[...] = h
        h16_ref[...] = h.astype(BF)

    return pl.pallas_call(
        body, name="inproj_fwd", grid=(T // tm,),
        in_specs=[pl.BlockSpec((tm, D_MODEL), lambda n: (n, 0)),
                  pl.BlockSpec((1, D_MODEL), lambda n: (0, 0)),
                  pl.BlockSpec((NP, D_MODEL), lambda n: (0, 0))],
        out_specs=[pl.BlockSpec((tm, NP), lambda n: (n, 0)),
                   pl.BlockSpec((tm, NP), lambda n: (n, 0)),
                   pl.BlockSpec((tm, D_MODEL), lambda n: (n, 0))],
        out_shape=[jax.ShapeDtypeStruct((T, NP), F32), jax.ShapeDtypeStruct((T, NP), BF),
                   jax.ShapeDtypeStruct((T, D_MODEL), BF)],
        compiler_params=_params(1))(x, g, wpt)


def _inproj_bwd(parts, wpt, x, xn, g, dxo):
    T = x.shape[0]
    tm = 256
    np_ = len(parts)
    chunk = NP // 3
    assert sum(p.shape[1] for p in parts) == C_END and chunk % 128 == 0

    def body(*refs):
        part_refs = refs[:np_]
        w_ref, x_ref, xn_ref, g_ref, dxo_ref, dw_ref, dx_ref, dg_ref = refs[np_:]
        n = pl.program_id(0)

        @pl.when(n == 0)
        def _():
            dw_ref[...] = jnp.zeros_like(dw_ref)
            dg_ref[...] = jnp.zeros_like(dg_ref)

        dh = jnp.concatenate([r[...].astype(BF) for r in part_refs] + [jnp.zeros((tm, NP - C_END), BF)], axis=1)
        xnv = xn_ref[...]
        for cb in range(3):
            cs = slice(cb * chunk, (cb + 1) * chunk)
            dw_ref[:, cs] += _dot_tn(xnv, dh[:, cs])
        dxn = _dot(dh, w_ref[...])
        xv = x_ref[...]
        _, r = _rms_fwd(xv, g_ref[...])
        dx, dgt = _rms_bwd(xv, g_ref[...], r, dxn)
        dx_ref[...] = dxo_ref[...] + dx
        dg_ref[...] += jnp.sum(dgt, axis=0, keepdims=True)

    once = pl.Buffered(1)
    return pl.pallas_call(
        body, name="inproj_bwd", grid=(T // tm,),
        in_specs=[pl.BlockSpec((tm, p.shape[1]), lambda n: (n, 0)) for p in parts]
        + [pl.BlockSpec((NP, D_MODEL), lambda n: (0, 0), pipeline_mode=once),
           pl.BlockSpec((tm, D_MODEL), lambda n: (n, 0)),
           pl.BlockSpec((tm, D_MODEL), lambda n: (n, 0)),
           pl.BlockSpec((1, D_MODEL), lambda n: (0, 0)),
           pl.BlockSpec((tm, D_MODEL), lambda n: (n, 0))],
        out_specs=[pl.BlockSpec((D_MODEL, NP), lambda n: (0, 0), pipeline_mode=once),
                   pl.BlockSpec((tm, D_MODEL), lambda n: (n, 0)),
                   pl.BlockSpec((1, D_MODEL), lambda n: (0, 0))],
        out_shape=[jax.ShapeDtypeStruct((D_MODEL, NP), F32), jax.ShapeDtypeStruct((T, D_MODEL), F32),
                   jax.ShapeDtypeStruct((1, D_MODEL), F32)],
        compiler_params=_params(1))(*parts, wpt, x, xn, g, dxo)


SWA_BLK = 128
SWA_TQ = 1024


def _bdot_nt(a, b):
    return lax.dot_general(a, b, (((2,), (2,)), ((0,), (0,))), preferred_element_type=F32)


def _bdot(a, b):
    return lax.dot_general(a, b, (((2,), (1,)), ((0,), (0,))), preferred_element_type=F32)


def _bdot_tn(a, b):
    return lax.dot_general(a, b, (((1,), (1,)), ((0,), (0,))), preferred_element_type=F32)


def _swa_probs(q, kc, kp, sink, mask_c, mask_p):
    sc = jnp.where(mask_c, _bdot_nt(q, kc) * SB_SCALE, NEG)
    sp = jnp.where(mask_p, _bdot_nt(q, kp) * SB_SCALE, NEG)
    m = jnp.maximum(jnp.maximum(jnp.max(sc, axis=-1, keepdims=True), jnp.max(sp, axis=-1, keepdims=True)), sink)
    pc = jnp.exp(sc - m)
    pp = jnp.exp(sp - m)
    ps = jnp.exp(sink - m)
    inv = 1.0 / (jnp.sum(pc, axis=-1, keepdims=True) + jnp.sum(pp, axis=-1, keepdims=True) + ps)
    return pc * inv, pp * inv, ps * inv


def _swa_masks(n, nb):
    blk = _iota((nb, SWA_BLK, SWA_BLK), 0)
    row = _iota((nb, SWA_BLK, SWA_BLK), 1)
    col = _iota((nb, SWA_BLK, SWA_BLK), 2)
    return col <= row, jnp.logical_and(col > row, jnp.logical_or(blk > 0, n > 0))


def _swa_specs(tq):
    halo = tq // SWA_BLK
    return [pl.BlockSpec(memory_space=pltpu.SMEM),
            pl.BlockSpec((tq, 256), lambda n: (n, C_AQ // 256)),
            pl.BlockSpec((tq, 128), lambda n: (n, C_AK // 128)),
            pl.BlockSpec((SWA_BLK, 128), lambda n: (jnp.maximum(n * halo - 1, 0), C_AK // 128)),
            pl.BlockSpec((tq, 128), lambda n: (n, C_AV // 128)),
            pl.BlockSpec((SWA_BLK, 128), lambda n: (jnp.maximum(n * halo - 1, 0), C_AV // 128))]


def _swa_blocked(cur_ref, prev_ref, gs, nb):
    cur = cur_ref[:, gs].reshape(nb, SWA_BLK, 64)
    prev = jnp.concatenate([prev_ref[:, gs].reshape(1, SWA_BLK, 64), cur[:nb - 1]], axis=0) if nb > 1 \
        else prev_ref[:, gs].reshape(1, SWA_BLK, 64)
    return cur, prev


def _swa_fwd(h16, sinks):
    T = h16.shape[0]
    tq = SWA_TQ if T % SWA_TQ == 0 else SWA_BLK
    nb = tq // SWA_BLK

    def body(s_ref, q_ref, kc_ref, kp_ref, vc_ref, vp_ref, o_ref):
        n = pl.program_id(0)
        mask_c, mask_p = _swa_masks(n, nb)
        for h in range(4):
            hs = slice(h * 64, (h + 1) * 64)
            gs = slice(h // 2 * 64, (h // 2 + 1) * 64)
            kc, kp = _swa_blocked(kc_ref, kp_ref, gs, nb)
            vc, vp = _swa_blocked(vc_ref, vp_ref, gs, nb)
            pc, pp, _ = _swa_probs(q_ref[:, hs].reshape(nb, SWA_BLK, 64), kc, kp, s_ref[h], mask_c, mask_p)
            o_ref[:, hs] = (_bdot(pc.astype(BF), vc) + _bdot(pp.astype(BF), vp)).reshape(tq, 64)

    return pl.pallas_call(
        body, name="swa_fwd", grid=(T // tq,), in_specs=_swa_specs(tq),
        out_specs=pl.BlockSpec((tq, 256), lambda n: (n, 0)),
        out_shape=jax.ShapeDtypeStruct((T, 256), F32),
        compiler_params=_params(1))(sinks, h16, h16, h16, h16, h16)


def _swa_bwd(h16, sinks, dya):
    T = h16.shape[0]
    tq = SWA_TQ if T % SWA_TQ == 0 else SWA_BLK
    nb = tq // SWA_BLK

    def body(s_ref, q_ref, kc_ref, kp_ref, vc_ref, vp_ref, do_ref, dq_ref, dk_ref, dv_ref, ds_ref):
        n = pl.program_id(0)

        @pl.when(n == 0)
        def _():
            dk_ref[...] = jnp.zeros_like(dk_ref)
            dv_ref[...] = jnp.zeros_like(dv_ref)
            ds_ref[...] = jnp.zeros_like(ds_ref)

        mask_c, mask_p = _swa_masks(n, nb)
        rows = pl.ds(pl.multiple_of(n * tq, tq), tq)
        before = pl.ds(pl.multiple_of(jnp.maximum(n * nb - 1, 0) * SWA_BLK, SWA_BLK), SWA_BLK)
        lane = _iota((8, 128), 1)
        row8 = _iota((8, 128), 0)

        def to_keys(own, prev):
            if nb == 1:
                return own
            return own + jnp.concatenate([prev[1:], jnp.zeros((1, SWA_BLK, 64), F32)], axis=0)

        for h in range(4):
            hs = slice(h * 64, (h + 1) * 64)
            gs = slice(h // 2 * 64, (h // 2 + 1) * 64)
            q = q_ref[:, hs].reshape(nb, SWA_BLK, 64)
            kc, kp = _swa_blocked(kc_ref, kp_ref, gs, nb)
            vc, vp = _swa_blocked(vc_ref, vp_ref, gs, nb)
            pc, pp, ps = _swa_probs(q, kc, kp, s_ref[h], mask_c, mask_p)
            pcb, ppb = pc.astype(BF), pp.astype(BF)
            do = do_ref[:, hs].reshape(nb, SWA_BLK, 64)
            dob = do.astype(BF)
            o = _bdot(pcb, vc) + _bdot(ppb, vp)
            dd = jnp.sum(do * o, axis=-1, keepdims=True)
            dsc = (pc * (_bdot_nt(dob, vc) - dd) * SB_SCALE).astype(BF)
            dsp = (pp * (_bdot_nt(dob, vp) - dd) * SB_SCALE).astype(BF)
            dq_ref[:, hs] = (_bdot(dsc, kc) + _bdot(dsp, kp)).reshape(tq, 64).astype(BF)
            dkp, dvp = _bdot_tn(dsp, q), _bdot_tn(ppb, dob)
            dk_ref[rows, gs] += to_keys(_bdot_tn(dsc, q), dkp).reshape(tq, 64)
            dv_ref[rows, gs] += to_keys(_bdot_tn(pcb, dob), dvp).reshape(tq, 64)
            dk_ref[before, gs] += dkp[0]
            dv_ref[before, gs] += dvp[0]
            ds_ref[...] += jnp.where(jnp.logical_and(lane == h, row8 == 0), -jnp.sum(ps * dd), 0.0)

    return pl.pallas_call(
        body, name="swa_bwd", grid=(T // tq,),
        in_specs=_swa_specs(tq) + [pl.BlockSpec((tq, 256), lambda n: (n, 0))],
        out_specs=[pl.BlockSpec((tq, 256), lambda n: (n, 0)),
                   pl.BlockSpec((T, 128), lambda n: (0, 0)),
                   pl.BlockSpec((T, 128), lambda n: (0, 0)),
                   pl.BlockSpec((8, 128), lambda n: (0, 0))],
        out_shape=[jax.ShapeDtypeStruct((T, 256), BF), jax.ShapeDtypeStruct((T, 128), F32),
                   jax.ShapeDtypeStruct((T, 128), F32), jax.ShapeDtypeStruct((8, 128), F32)],
        compiler_params=_params(1))(sinks, h16, h16, h16, h16, h16, dya)


def _conv_u(bc_ref, bx_ref, bch_ref, bxh_ref, n, tm):
    u = bc_ref[...] * bx_ref[...]
    uh = bch_ref[...] * bxh_ref[...] * (n > 0).astype(F32)
    rowi = _iota((tm, 256), 0)
    u1 = jnp.where(rowi == 0, uh[7:8, :], pltpu.roll(u, 1, axis=0))
    u2 = jnp.where(rowi == 0, uh[6:7, :], jnp.where(rowi == 1, uh[7:8, :], pltpu.roll(u, 2, axis=0)))
    return u, u1, u2


def _conv_fwd(h32, cw, cb):
    T = h32.shape[0]
    tm = 512 if T % 512 == 0 else T
    hb = tm // 8

    def body(bb_ref, bc_ref, bx_ref, bch_ref, bxh_ref, w_ref, b_ref, o_ref):
        n = pl.program_id(0)
        u, u1, u2 = _conv_u(bc_ref, bx_ref, bch_ref, bxh_ref, n, tm)
        y = w_ref[0:1, :] * u2 + w_ref[1:2, :] * u1 + w_ref[2:3, :] * u + b_ref[...]
        o_ref[...] = bb_ref[...] * y

    halo = lambda c: pl.BlockSpec((8, 256), lambda n: (jnp.maximum(n * hb - 1, 0), c // 256))
    return pl.pallas_call(
        body, name="conv_fwd", grid=(T // tm,),
        in_specs=[pl.BlockSpec((tm, 256), lambda n: (n, C_BB // 256)),
                  pl.BlockSpec((tm, 256), lambda n: (n, C_BC // 256)),
                  pl.BlockSpec((tm, 256), lambda n: (n, C_BX // 256)),
                  halo(C_BC), halo(C_BX),
                  pl.BlockSpec((8, 256), lambda n: (0, 0)),
                  pl.BlockSpec((1, 256), lambda n: (0, 0))],
        out_specs=pl.BlockSpec((tm, 256), lambda n: (n, 0)),
        out_shape=jax.ShapeDtypeStruct((T, 256), F32),
        compiler_params=_params(1))(h32, h32, h32, h32, h32, cw, cb)


def _conv_bwd(h32, cw, cb, dyb):
    T = h32.shape[0]
    tm = 512 if T % 512 == 0 else T
    hb = tm // 8
    nt = T // tm

    def body(bb_ref, bc_ref, bx_ref, bch_ref, bxh_ref, bbn_ref, dy_ref, dyn_ref, w_ref, b_ref,
             dbb_ref, dbc_ref, dbx_ref, dw_ref):
        n = pl.program_id(0)

        @pl.when(n == 0)
        def _():
            dw_ref[...] = jnp.zeros_like(dw_ref)

        u, u1, u2 = _conv_u(bc_ref, bx_ref, bch_ref, bxh_ref, n, tm)
        w0, w1, w2 = w_ref[0:1, :], w_ref[1:2, :], w_ref[2:3, :]
        y = w0 * u2 + w1 * u1 + w2 * u + b_ref[...]
        dyb_ = dy_ref[...]
        dbb_ref[...] = (dyb_ * y).astype(BF)
        dy = dyb_ * bb_ref[...]
        dyn = dyn_ref[...] * bbn_ref[...] * (n < nt - 1).astype(F32)
        rowi = _iota((tm, 256), 0)
        dy1 = jnp.where(rowi == tm - 1, dyn[0:1, :], pltpu.roll(dy, tm - 1, axis=0))
        dy2 = jnp.where(rowi == tm - 2, dyn[0:1, :],
                        jnp.where(rowi == tm - 1, dyn[1:2, :], pltpu.roll(dy, tm - 2, axis=0)))
        du = w2 * dy + w1 * dy1 + w0 * dy2
        dbc_ref[...] = (du * bx_ref[...]).astype(BF)
        dbx_ref[...] = (du * bc_ref[...]).astype(BF)
        dw_ref[0:1, :] += jnp.sum(dy * u2, axis=0, keepdims=True)
        dw_ref[1:2, :] += jnp.sum(dy * u1, axis=0, keepdims=True)
        dw_ref[2:3, :] += jnp.sum(dy * u, axis=0, keepdims=True)
        dw_ref[3:4, :] += jnp.sum(dy, axis=0, keepdims=True)

    halo = lambda c: pl.BlockSpec((8, 256), lambda n: (jnp.maximum(n * hb - 1, 0), c // 256))
    nxt = lambda c: pl.BlockSpec((8, 256), lambda n: (jnp.minimum((n + 1) * hb, T // 8 - 1), c // 256))
    cur = lambda c: pl.BlockSpec((tm, 256), lambda n: (n, c // 256))
    return pl.pallas_call(
        body, name="conv_bwd", grid=(nt,),
        in_specs=[cur(C_BB), cur(C_BC), cur(C_BX), halo(C_BC), halo(C_BX), nxt(C_BB),
                  cur(0), nxt(0),
                  pl.BlockSpec((8, 256), lambda n: (0, 0)),
                  pl.BlockSpec((1, 256), lambda n: (0, 0))],
        out_specs=[cur(0), cur(0), cur(0), pl.BlockSpec((8, 256), lambda n: (0, 0))],
        out_shape=[jax.ShapeDtypeStruct((T, 256), BF)] * 3 + [jax.ShapeDtypeStruct((8, 256), F32)],
        compiler_params=_params(1))(h32, h32, h32, h32, h32, h32, dyb, dyb, cw, cb)


def _cprep_specs(tm):
    return [pl.BlockSpec((tm, 256), lambda n: (n, C_CQ // 256)),
            pl.BlockSpec((tm, 128), lambda n: (n, C_CKV // 128)),
            pl.BlockSpec((tm, 128), lambda n: (n, C_CKR // 128)),
            pl.BlockSpec((tm, 128), lambda n: (n, C_CKRS // 128)),
            pl.BlockSpec((1, 256), lambda n: (0, 0)),
            pl.BlockSpec((1, 128), lambda n: (0, 0)),
            pl.BlockSpec((tm, 128), lambda n: (n, 0)),
            pl.BlockSpec((tm, 128), lambda n: (n, 0))]


def _cprep_fwd(h32, gq, gkv, wuq2, wkv2, cosk, sin):
    T = h32.shape[0]
    tm = 512 if T % 512 == 0 else T

    def body(cq_ref, ckv_ref, ckr_ref, ckrs_ref, gq_ref, gkv_ref, cos_ref, sin_ref, wuq_ref, wkv_ref,
             q_ref, k_ref, v_ref):
        cosk_, sin_ = cos_ref[...], sin_ref[...]
        cosq = cosk_ + (_iota((tm, 128), 1) < 64).astype(F32)
        cqn, _ = _rms_fwd(cq_ref[...], gq_ref[...])
        q2 = _dot(cqn.astype(BF), wuq_ref[...])
        ckvn, _ = _rms_fwd(ckv_ref[...], gkv_ref[...])
        kv2 = _dot(ckvn.astype(BF), wkv_ref[...])
        kr = ckr_ref[...] * cosk_ + ckrs_ref[...] * sin_
        for h in range(4):
            hs = slice(h * 128, (h + 1) * 128)
            q_ref[:, hs] = ((q2[:, hs] * cosq + q2[:, 512 + h * 128:512 + (h + 1) * 128] * sin_) * MLA_SCALE).astype(BF)
            k_ref[:, hs] = (kv2[:, hs] + kr).astype(BF)
        ones = _iota((tm, 512), 1) % 128 == 64
        v_ref[...] = jnp.where(ones, 1.0, kv2[:, 512:]).astype(BF)

    return pl.pallas_call(
        body, name="cprep_fwd", grid=(T // tm,),
        in_specs=_cprep_specs(tm) + [pl.BlockSpec((256, 1024), lambda n: (0, 0)),
                                     pl.BlockSpec((128, 1024), lambda n: (0, 0))],
        out_specs=[pl.BlockSpec((tm, 512), lambda n: (n, 0))] * 3,
        out_shape=[jax.ShapeDtypeStruct((T, 512), BF)] * 3,
        compiler_params=_params(1))(h32, h32, h32, h32, gq, gkv, cosk, sin, wuq2, wkv2)


def _cprep_bwd(h32, gq, gkv, wuq2t, wkv2t, cosk, sin, dq, dk, dv):
    T = h32.shape[0]
    tm = 512 if T % 512 == 0 else T

    def body(cq_ref, ckv_ref, ckr_ref, ckrs_ref, gq_ref, gkv_ref, cos_ref, sin_ref, wuq_ref, wkv_ref,
             dq_ref, dk_ref, dv_ref,
             dcq_ref, dckv_ref, dckr_ref, dckrs_ref, dwuq_ref, dwkv_ref, dgq_ref, dgkv_ref):
        n = pl.program_id(0)

        @pl.when(n == 0)
        def _():
            dwuq_ref[...] = jnp.zeros_like(dwuq_ref)
            dwkv_ref[...] = jnp.zeros_like(dwkv_ref)
            dgq_ref[...] = jnp.zeros_like(dgq_ref)
            dgkv_ref[...] = jnp.zeros_like(dgkv_ref)

        cosk_, sin_ = cos_ref[...], sin_ref[...]
        cosq = cosk_ + (_iota((tm, 128), 1) < 64).astype(F32)
        dkr = jnp.zeros((tm, 128), F32)
        plain, swapped = [], []
        for h in range(4):
            hs = slice(h * 128, (h + 1) * 128)
            dqh = dq_ref[:, hs] * MLA_SCALE
            plain.append((dqh * cosq).astype(BF))
            swapped.append((dqh * sin_).astype(BF))
            dkr = dkr + dk_ref[:, hs]
        dq2 = jnp.concatenate(plain + swapped, axis=1)
        dkv2 = jnp.concatenate([dk_ref[...].astype(BF), dv_ref[...].astype(BF)], axis=1)
        dckr_ref[...] = (dkr * cosk_).astype(BF)
        dckrs_ref[...] = (dkr * sin_).astype(BF)

        cq, gq_ = cq_ref[...], gq_ref[...]
        cqn, rq = _rms_fwd(cq, gq_)
        dwuq_ref[...] += _dot_tn(cqn.astype(BF), dq2)
        dcq, dgt = _rms_bwd(cq, gq_, rq, _dot(dq2, wuq_ref[...]))
        dcq_ref[...] = dcq.astype(BF)
        dgq_ref[...] += jnp.sum(dgt, axis=0, keepdims=True)

        ckv, gkv_ = ckv_ref[...], gkv_ref[...]
        ckvn, rkv = _rms_fwd(ckv, gkv_)
        dwkv_ref[...] += _dot_tn(ckvn.astype(BF), dkv2)
        dckv, dgt2 = _rms_bwd(ckv, gkv_, rkv, _dot(dkv2, wkv_ref[...]))
        dckv_ref[...] = dckv.astype(BF)
        dgkv_ref[...] += jnp.sum(dgt2, axis=0, keepdims=True)

    row = lambda w: pl.BlockSpec((tm, w), lambda n: (n, 0))
    return pl.pallas_call(
        body, name="cprep_bwd", grid=(T // tm,),
        in_specs=_cprep_specs(tm) + [pl.BlockSpec((1024, 256), lambda n: (0, 0)),
                                     pl.BlockSpec((1024, 128), lambda n: (0, 0)),
                                     row(512), row(512), row(512)],
        out_specs=[row(256), row(128), row(128), row(128),
                   pl.BlockSpec((256, 1024), lambda n: (0, 0)), pl.BlockSpec((128, 1024), lambda n: (0, 0)),
                   pl.BlockSpec((1, 256), lambda n: (0, 0)), pl.BlockSpec((1, 128), lambda n: (0, 0))],
        out_shape=[jax.ShapeDtypeStruct((T, 256), BF), jax.ShapeDtypeStruct((T, 128), BF),
                   jax.ShapeDtypeStruct((T, 128), BF), jax.ShapeDtypeStruct((T, 128), BF),
                   jax.ShapeDtypeStruct((256, 1024), F32), jax.ShapeDtypeStruct((128, 1024), F32),
                   jax.ShapeDtypeStruct((1, 256), F32), jax.ShapeDtypeStruct((1, 128), F32)],
        compiler_params=_params(1))(h32, h32, h32, h32, gq, gkv, cosk, sin, wuq2t, wkv2t, dq, dk, dv)


MLA_TILE = 512
MLA_HEADS_PER_STEP = 4


def _causal_mask(t):
    return _iota((t, t), 1) <= _iota((t, t), 0)


def _mla_fwd(q, k, v):
    T = q.shape[0]
    tq = MLA_TILE

    def body(q_ref, k_ref, v_ref, o_ref, lse_ref):
        i = pl.program_id(1)
        mask = _causal_mask(tq)
        heads = [slice(128 * h, 128 * h + 128) for h in range(MLA_HEADS_PER_STEP)]
        qs = [q_ref[:, hs] for hs in heads]

        def step(j, carry, masked):
            rows = pl.ds(pl.multiple_of(j * tq, tq), tq)
            out = []
            for hh, hs in enumerate(heads):
                m, acc = carry[hh]
                s = _dot_nt(qs[hh], k_ref[rows, hs])
                if masked:
                    s = jnp.where(mask, s, NEG)
                m_new = jnp.maximum(m, jnp.max(s, axis=-1, keepdims=True))
                p = jnp.exp((s - m_new).astype(BF))
                acc = jnp.exp(m - m_new) * acc + _dot(p, v_ref[rows, hs])
                out.append((m_new, acc))
            return tuple(out)

        init = ((jnp.full((tq, 1), NEG, F32), jnp.zeros((tq, 128), F32)),) * len(heads)
        carry = lax.fori_loop(0, i, lambda j, c: step(j, c, False), init)
        carry = step(i, carry, True)
        for hh, hs in enumerate(heads):
            m, acc = carry[hh]
            l = acc[:, 64:65]
            o_ref[:, hs] = acc * (1.0 / l)
            lse_ref[:, hs] = jnp.broadcast_to(m + jnp.log(l), (tq, 128))

    width = 128 * MLA_HEADS_PER_STEP
    blk = pl.BlockSpec((tq, width), lambda h, i: (i, h))
    full = pl.BlockSpec((T, width), lambda h, i: (0, h))
    return pl.pallas_call(
        body, name="mla_fwd", grid=(4 // MLA_HEADS_PER_STEP, T // tq), in_specs=[blk, full, full],
        out_specs=[blk, blk],
        out_shape=[jax.ShapeDtypeStruct((T, 512), F32), jax.ShapeDtypeStruct((T, 512), F32)],
        compiler_params=_params(2))(q, k, v)


def _mla_bwd(q, k, v, o, lse, do):
    T = q.shape[0]
    tq = MLA_TILE

    def body(q_ref, k_ref, v_ref, o_ref, lse_ref, do_ref, dq_ref, dk_ref, dv_ref):
        i = pl.program_id(1)

        @pl.when(i == 0)
        def _():
            dk_ref[...] = jnp.zeros_like(dk_ref)
            dv_ref[...] = jnp.zeros_like(dv_ref)

        heads = [slice(0, 128), slice(128, 256)]
        mask = _causal_mask(tq)
        qs, dobs, dds, lses = [], [], [], []
        for hs in heads:
            do = do_ref[:, hs]
            qs.append(q_ref[:, hs])
            dobs.append(do.astype(BF))
            dds.append(jnp.sum(do * o_ref[:, hs], axis=-1, keepdims=True))
            lses.append(lse_ref[:, hs.start:hs.start + 1])

        def step(j, dqs, masked):
            rows = pl.ds(pl.multiple_of(j * tq, tq), tq)
            out = []
            for hh, hs in enumerate(heads):
                kj, vj = k_ref[rows, hs], v_ref[rows, hs]
                s = _dot_nt(qs[hh], kj)
                if masked:
                    s = jnp.where(mask, s, NEG)
                p = jnp.exp(s - lses[hh])
                ds = (p * (_dot_nt(dobs[hh], vj) - dds[hh])).astype(BF)
                dk_ref[rows, hs] += _dot_tn(ds, qs[hh])
                dv_ref[rows, hs] += _dot_tn(p.astype(BF), dobs[hh])
                out.append(dqs[hh] + _dot(ds, kj))
            return tuple(out)

        dqs = lax.fori_loop(0, i, lambda j, c: step(j, c, False), (jnp.zeros((tq, 128), F32),) * 2)
        dqs = step(i, dqs, True)
        for hh, hs in enumerate(heads):
            dq_ref[:, hs] = dqs[hh]

    blk = pl.BlockSpec((tq, 256), lambda h, i: (i, h))
    full = pl.BlockSpec((T, 256), lambda h, i: (0, h), pipeline_mode=pl.Buffered(1))
    return pl.pallas_call(
        body, name="mla_bwd", grid=(2, T // tq), in_specs=[blk, full, full, blk, blk, blk],
        out_specs=[blk, full, full],
        out_shape=[jax.ShapeDtypeStruct((T, 512), F32)] * 3,
        compiler_params=_params(2))(q, k, v, o, lse, do)


def _sb_tile(qk, rr, strict, masked, upper):
    z2 = qk * (SB_SCALE * LOG2E)
    l1 = jnp.log2(1.0 + jnp.exp2(-jnp.abs(z2)))
    lk = -jnp.maximum(z2, 0.0) - l1
    if masked:
        lk = jnp.where(strict, lk, 0.0)
    after = rr + _dot(lk.astype(BF), upper)
    ll = jnp.minimum(z2, 0.0) - l1
    a = jnp.exp2(ll + after)
    if masked:
        a = jnp.where(strict, a, 0.0)
    return ll, a, jnp.sum(lk, axis=-1, keepdims=True)


SB_TQ, SB_TK = 256, 256
SB_DEAD = -160.0


def _sb_walk(trips, one_step, carry):
    def alive(c):
        t, cr = c
        top = jnp.maximum(jnp.max(cr[0][0]), jnp.max(cr[1][0]))
        return jnp.logical_and(t < trips, top > SB_DEAD)

    def body(c):
        t, cr = c
        return t + 1, one_step(t, cr)

    return lax.while_loop(alive, body, (jnp.int32(0), carry))[1]


def _sb_consts(tq, tk):
    row, col = _iota((tq, tk), 0), _iota((tq, tk), 1)
    strict = [col + d * tk < row for d in range(tq // tk)]
    r2, c2 = _iota((tk, tk), 0), _iota((tk, tk), 1)
    return strict, (r2 > c2).astype(BF), (r2 < c2).astype(BF)


def _sb_fwd(h16):
    T = h16.shape[0]
    tq, tk = SB_TQ, SB_TK
    nd = tq // tk

    def body(q_ref, k_ref, v_ref, o_ref):
        i = pl.program_id(1)
        strict, upper, _ = _sb_consts(tq, tk)
        lane = _iota((tq, 128), 1)
        q2 = q_ref[...]
        qms = [jnp.where(lane < 64, q2, jnp.zeros_like(q2)), jnp.where(lane >= 64, q2, jnp.zeros_like(q2))]

        def step(j, carry, d):
            rows = pl.ds(pl.multiple_of(j * tk, tk), tk)
            kj, vj = k_ref[rows, :], v_ref[rows, :]
            out = []
            for hh in range(2):
                rr, acc = carry[hh]
                _, a, rs = _sb_tile(_dot_nt(qms[hh], kj), rr, None if d is None else strict[d], d is not None, upper)
                out.append((rr + rs, acc + _dot(a.astype(BF), vj)))
            return tuple(out)


        carry = ((jnp.zeros((tq, 1), F32), jnp.zeros((tq, 128), F32)),) * 2
        for d in reversed(range(nd)):
            carry = step(nd * i + d, carry, d)
        carry = _sb_walk(nd * i, lambda t, c: step(nd * i - 1 - t, c, None), carry)
        o_ref[...] = jnp.where(lane < 64, carry[0][1], carry[1][1])

    return pl.pallas_call(
        body, name="sb_fwd", grid=(2, T // tq),
        in_specs=[pl.BlockSpec((tq, 128), lambda p, i: (i, C_DQ // 128 + p)),
                  pl.BlockSpec((T, 128), lambda p, i: (0, C_DK // 128 + p)),
                  pl.BlockSpec((T, 128), lambda p, i: (0, C_DV // 128 + p))],
        out_specs=pl.BlockSpec((tq, 128), lambda p, i: (i, p)),
        out_shape=jax.ShapeDtypeStruct((T, 256), F32),
        compiler_params=_params(2))(h16, h16, h16)


def _sb_bwd(h16, yd, dyd):
    T = h16.shape[0]
    tq, tk = SB_TQ, SB_TK
    nd = tq // tk

    def body(q_ref, k_ref, v_ref, o_ref, do_ref, dq_ref, dk_ref, dv_ref):
        i = pl.program_id(1)

        @pl.when(i == 0)
        def _():
            dk_ref[...] = jnp.zeros_like(dk_ref)
            dv_ref[...] = jnp.zeros_like(dv_ref)

        strict, upper, before = _sb_consts(tq, tk)
        lane = _iota((tq, 128), 1)
        lane_k = _iota((tk, 128), 1)
        q2 = q_ref[...]
        dob2 = do_ref[...].astype(BF)
        doo = dob2.astype(F32) * o_ref[...]
        mines = [lane < 64, lane >= 64]
        qms = [jnp.where(m, q2, jnp.zeros_like(q2)) for m in mines]
        doms = [jnp.where(m, dob2, jnp.zeros_like(dob2)) for m in mines]
        dds = [jnp.sum(jnp.where(m, doo, 0.0), axis=-1, keepdims=True) for m in mines]

        def step(j, carry, d):
            rows = pl.ds(pl.multiple_of(j * tk, tk), tk)
            kj, vj = k_ref[rows, :], v_ref[rows, :]
            out, dks, dvs = [], [], []
            for hh in range(2):
                rr, sg, dq = carry[hh]
                ll, a, rs = _sb_tile(_dot_nt(qms[hh], kj), rr, None if d is None else strict[d], d is not None,
                                     upper)
                ab = a.astype(BF)
                g = _dot_nt(doms[hh], vj) * ab.astype(F32)
                gs = jnp.sum(g, axis=-1, keepdims=True)
                pre = (dds[hh] - sg - gs) + _dot(g.astype(BF), before)
                dz = g - jnp.exp2(ll) * (g + pre)
                if d is not None:
                    dz = jnp.where(strict[d], dz, 0.0)
                dzb = dz.astype(BF)
                dks.append(_dot_tn(dzb, q2))
                dvs.append(_dot_tn(ab, dob2))
                out.append((rr + rs, sg + gs, dq + _dot(dzb, kj)))
            dk_ref[rows, :] += jnp.where(lane_k < 64, dks[0], dks[1]) * SB_SCALE
            dv_ref[rows, :] += jnp.where(lane_k < 64, dvs[0], dvs[1])
            return tuple(out)


        zero = jnp.zeros((tq, 1), F32)
        carry = ((zero, zero, jnp.zeros((tq, 128), F32)),) * 2
        for d in reversed(range(nd)):
            carry = step(nd * i + d, carry, d)
        carry = _sb_walk(nd * i, lambda t, c: step(nd * i - 1 - t, c, None), carry)
        dq_ref[...] = jnp.where(lane < 64, carry[0][2], carry[1][2]) * SB_SCALE

    blk = lambda c: pl.BlockSpec((tq, 128), lambda p, i: (i, c // 128 + p))
    full = lambda c: pl.BlockSpec((T, 128), lambda p, i: (0, c // 128 + p))
    return pl.pallas_call(
        body, name="sb_bwd", grid=(2, T // tq),
        in_specs=[blk(C_DQ), full(C_DK), full(C_DV), blk(0), blk(0)],
        out_specs=[blk(0), full(0), full(0)],
        out_shape=[jax.ShapeDtypeStruct((T, 256), F32)] * 3,
        compiler_params=_params(2))(h16, h16, h16, yd, dyd)


def _compact_c(ycp):
    return jnp.concatenate([ycp[:, h * 128:h * 128 + 64] for h in range(4)], axis=1)


def _post_fwd(ya, yb, ycp, yd, h32, ggrp, wout, gpost, x):
    T = x.shape[0]
    tm = 256

    def body(ya_ref, yb_ref, yc_ref, yd_ref, gate_ref, gg_ref, w_ref, gp_ref, x_ref, xn_ref, ym_ref, o_ref):
        ys = [ya_ref[...], yb_ref[...], _compact_c(yc_ref[...]), yd_ref[...]]
        gate = gate_ref[...]
        sil = gate * (1.0 / (1.0 + jnp.exp(-gate)))
        parts = []
        for gi in range(4):
            ng, _ = _rms_fwd(ys[gi], gg_ref[:, gi * 256:(gi + 1) * 256])
            parts.append(ng * sil[:, gi * 256:(gi + 1) * 256])
        ym = jnp.concatenate(parts, axis=1).astype(BF)
        ym_ref[...] = ym
        o = _dot(ym, w_ref[...])
        o_ref[...] = o
        on, _ = _rms_fwd(o, gp_ref[...])
        xn_ref[...] = x_ref[...] + on

    row = lambda w: pl.BlockSpec((tm, w), lambda n: (n, 0))
    vec = pl.BlockSpec((1, 1024), lambda n: (0, 0))
    return pl.pallas_call(
        body, name="post_fwd", grid=(T // tm,),
        in_specs=[row(256), row(256), row(512), row(256), pl.BlockSpec((tm, 1024), lambda n: (n, C_GATE // 1024)),
                  vec, pl.BlockSpec((1024, 1024), lambda n: (0, 0)), vec, row(1024)],
        out_specs=[row(1024), row(1024), row(1024)],
        out_shape=[jax.ShapeDtypeStruct((T, 1024), F32), jax.ShapeDtypeStruct((T, 1024), BF),
                   jax.ShapeDtypeStruct((T, 1024), F32)],
        compiler_params=_params(1))(ya, yb, ycp, yd, h32, ggrp, wout, gpost, x)


def _post_bwd(dx, o, gpost, woutt, ya, yb, ycp, yd, h32, ggrp, ym):
    T = dx.shape[0]
    tm = 256

    def body(dx_ref, o_ref, gp_ref, w_ref, ya_ref, yb_ref, yc_ref, yd_ref, gate_ref, gg_ref, ym_ref,
             dw_ref, dya_ref, dyb_ref, dyc_ref, dyd_ref, dgate_ref, dgp_ref, dgg_ref):
        n = pl.program_id(0)

        @pl.when(n == 0)
        def _():
            dw_ref[...] = jnp.zeros_like(dw_ref)
            dgp_ref[...] = jnp.zeros_like(dgp_ref)
            dgg_ref[...] = jnp.zeros_like(dgg_ref)

        ov, gp = o_ref[...], gp_ref[...]
        _, ro = _rms_fwd(ov, gp)
        do, dgt = _rms_bwd(ov, gp, ro, dx_ref[...])
        dgp_ref[...] += jnp.sum(dgt, axis=0, keepdims=True)
        dob = do.astype(BF)
        dw_ref[...] += _dot_tn(ym_ref[...], dob)
        dym = _dot(dob, w_ref[...])
        gate = gate_ref[...]
        sg = 1.0 / (1.0 + jnp.exp(-gate))
        sil = gate * sg
        dsil = sg * (1.0 + gate * (1.0 - sg))
        ys = [ya_ref[...], yb_ref[...], _compact_c(yc_ref[...]), yd_ref[...]]
        dys = []
        for gi in range(4):
            gs = slice(gi * 256, (gi + 1) * 256)
            gg = gg_ref[:, gs]
            ng, rg = _rms_fwd(ys[gi], gg)
            dgate_ref[:, gs] = (dym[:, gs] * ng * dsil[:, gs]).astype(BF)
            dy, dgt2 = _rms_bwd(ys[gi], gg, rg, dym[:, gs] * sil[:, gs])
            dgg_ref[:, gs] += jnp.sum(dgt2, axis=0, keepdims=True)
            dys.append(dy)
        dya_ref[...] = dys[0]
        dyb_ref[...] = dys[1]
        dyd_ref[...] = dys[3]
        z64 = jnp.zeros((tm, 64), F32)
        dyc_ref[...] = jnp.concatenate(
            [piece for h in range(4) for piece in (dys[2][:, h * 64:(h + 1) * 64], z64)], axis=1)

    row = lambda w: pl.BlockSpec((tm, w), lambda n: (n, 0))
    vec = pl.BlockSpec((1, 1024), lambda n: (0, 0))
    return pl.pallas_call(
        body, name="post_bwd", grid=(T // tm,),
        in_specs=[row(1024), row(1024), vec, pl.BlockSpec((1024, 1024), lambda n: (0, 0)),
                  row(256), row(256), row(512), row(256),
                  pl.BlockSpec((tm, 1024), lambda n: (n, C_GATE // 1024)), vec, row(1024)],
        out_specs=[pl.BlockSpec((1024, 1024), lambda n: (0, 0)), row(256), row(256), row(512), row(256), row(1024),
                   vec, vec],
        out_shape=[jax.ShapeDtypeStruct((1024, 1024), F32), jax.ShapeDtypeStruct((T, 256), F32),
                   jax.ShapeDtypeStruct((T, 256), F32), jax.ShapeDtypeStruct((T, 512), F32),
                   jax.ShapeDtypeStruct((T, 256), F32), jax.ShapeDtypeStruct((T, 1024), BF),
                   jax.ShapeDtypeStruct((1, 1024), F32), jax.ShapeDtypeStruct((1, 1024), F32)],
        compiler_params=_params(1))(dx, o, gpost, woutt, ya, yb, ycp, yd, h32, ggrp, ym)


def _loss_head(y, tgt):
    T = y.shape[0]
    tm = 512 if T % 512 == 0 else T

    def body(y_ref, t_ref, s_ref, dy_ref):
        n = pl.program_id(0)

        @pl.when(n == 0)
        def _():
            s_ref[...] = jnp.zeros_like(s_ref)

        d = y_ref[...] - t_ref[...]
        s_ref[...] += jnp.sum(d * d, axis=0, keepdims=True)
        dy_ref[...] = d * (1.0 / D_MODEL)

    row = pl.BlockSpec((tm, 1024), lambda n: (n, 0))
    return pl.pallas_call(
        body, name="loss_head", grid=(T // tm,), in_specs=[row, row],
        out_specs=[pl.BlockSpec((1, 1024), lambda n: (0, 0)), row],
        out_shape=[jax.ShapeDtypeStruct((1, 1024), F32), jax.ShapeDtypeStruct((T, 1024), F32)],
        compiler_params=_params(1))(y, tgt)


def _swap_rows32(a):
    return jnp.concatenate([a[16:32], a[0:16]], axis=0)


def _pad_w_uq(w):
    z = lambda n: jnp.zeros((w.shape[0], n), w.dtype)
    a = [p for h in range(4) for p in (w[:, 96 * h:96 * h + 96], z(32))]
    b = [p for h in range(4) for p in (z(64), _swap32(w[:, 96 * h + 64:96 * h + 96]), z(32))]
    return jnp.concatenate(a + b, axis=1)


def _unpad_w_uq(d):
    out = []
    for h in range(4):
        out.append(d[:, 128 * h:128 * h + 64])
        out.append(d[:, 128 * h + 64:128 * h + 96] + _swap32(d[:, 512 + 128 * h + 64:512 + 128 * h + 96]))
    return jnp.concatenate(out, axis=1)


def _pad_w_ukv(w):
    z = jnp.zeros((w.shape[0], 64), w.dtype)
    a = [p for h in range(4) for p in (w[:, 128 * h:128 * h + 64], z)]
    b = [p for h in range(4) for p in (w[:, 128 * h + 64:128 * h + 128], z)]
    return jnp.concatenate(a + b, axis=1)


def _unpad_w_ukv(d):
    return jnp.concatenate([p for h in range(4) for p in (d[:, 128 * h:128 * h + 64],
                                                          d[:, 512 + 128 * h:512 + 128 * h + 64])], axis=1)


def _rope_tables(pos):
    freqs = 10000.0 ** (-jnp.arange(16, dtype=F32) / 16)
    ang = pos.astype(F32)[:, None] * freqs
    c, s = jnp.cos(ang), jnp.sin(ang)
    z = lambda n: jnp.zeros((pos.shape[0], n), F32)
    return (jnp.concatenate([z(64), c, c, z(32)], axis=1), jnp.concatenate([z(64), -s, s, z(32)], axis=1))


def _layer_weights(W, l):
    wuq2 = _pad_w_uq(W["mla_w_uq"][l])
    wkv2 = _pad_w_ukv(W["mla_w_ukv"][l])
    wout = W["w_out"][l]
    cw = jnp.concatenate([W["conv_w"][l].astype(F32), jnp.zeros((5, 256), F32)], axis=0)
    return dict(
        wpt=W["wpt"][l], wuq2=wuq2.astype(BF), wuq2t=wuq2.T.astype(BF),
        wkv2=wkv2.astype(BF), wkv2t=wkv2.T.astype(BF), wout=wout.astype(BF), woutt=wout.T.astype(BF),
        cw=cw, cb=W["conv_b"][l][None, :], sinks=W["attn_sinks"][l],
        gpre=W["norm_pre"][l][None, :], gq=W["mla_q_norm"][l][None, :], gkv=W["mla_kv_norm"][l][None, :],
        ggrp=W["group_norm"][l][None, :], gpost=W["norm_post"][l][None, :])


def _local_step(x, pos, W, tgt):
    cosk, sin = _rope_tables(pos)
    saved = []
    for l in range(DEPTH):
        lw = _layer_weights(W, l)
        h32, h16, xn = _inproj_fwd(x, lw["gpre"], lw["wpt"])
        ya = _swa_fwd(h16, lw["sinks"])
        yb = _conv_fwd(h32, lw["cw"], lw["cb"])
        qc, kc, vc = _cprep_fwd(h32, lw["gq"], lw["gkv"], lw["wuq2"], lw["wkv2"], cosk, sin)
        ycp, lse = _mla_fwd(qc, kc, vc)
        yd = _sb_fwd(h16)
        x_new, ym, o = _post_fwd(ya, yb, ycp, yd, h32, lw["ggrp"], lw["wout"], lw["gpost"], x)
        saved.append(dict(lw=lw, x=x, h32=h32, h16=h16, xn=xn, ya=ya, yb=yb, qc=qc, kc=kc, vc=vc, ycp=ycp,
                          lse=lse, yd=yd, ym=ym, o=o))
        x = x_new
    sq, dx = _loss_head(x, tgt)

    grads = {k: [None] * DEPTH for k in ("norm_pre", "w_in_pt", "attn_sinks", "conv_w", "conv_b", "mla_q_norm",
                                         "mla_w_uq", "mla_kv_norm", "mla_w_ukv", "group_norm", "w_out",
                                         "norm_post")}
    for l in reversed(range(DEPTH)):
        s = saved[l]
        lw = s["lw"]
        dwout, dya, dyb, dycp, dyd, dgate, dgpost, dggrp = _post_bwd(
            dx, s["o"], lw["gpost"], lw["woutt"], s["ya"], s["yb"], s["ycp"], s["yd"], s["h32"], lw["ggrp"], s["ym"])
        grads["norm_post"][l] = dgpost[0]
        grads["group_norm"][l] = dggrp[0]
        grads["w_out"][l] = dwout
        sdq, sdk, sdv = _sb_bwd(s["h16"], s["yd"], dyd)
        mdq, mdk, mdv = _mla_bwd(s["qc"], s["kc"], s["vc"], s["ycp"], s["lse"], dycp)
        dcq, dckv, dckr, dckrs, dwuq2, dwkv2, dgq, dgkv = _cprep_bwd(
            s["h32"], lw["gq"], lw["gkv"], lw["wuq2t"], lw["wkv2t"], cosk, sin, mdq, mdk, mdv)
        grads["mla_q_norm"][l] = dgq[0]
        grads["mla_kv_norm"][l] = dgkv[0]
        grads["mla_w_uq"][l] = _unpad_w_uq(dwuq2)
        grads["mla_w_ukv"][l] = _unpad_w_ukv(dwkv2)
        dbb, dbc, dbx, dcw = _conv_bwd(s["h32"], lw["cw"], lw["cb"], dyb)
        grads["conv_w"][l] = dcw[0:3]
        grads["conv_b"][l] = dcw[3]
        adq, adk, adv, dsk = _swa_bwd(s["h16"], lw["sinks"], dya)
        grads["attn_sinks"][l] = dsk[0, 0:4]
        parts = [dgate, adq, adk, adv, dbb, dbc, dbx, dcq, dckv, dckr, dckrs, sdq, sdk, sdv]
        dwp, dx, dgpre = _inproj_bwd(parts, lw["wpt"], s["x"], s["xn"], lw["gpre"], dx)
        grads["w_in_pt"][l] = dwp.T
        grads["norm_pre"][l] = dgpre[0]
    return sq, dx, grads


SMALL_SHARDED = ("conv_w", "mla_w_uq", "mla_w_ukv")
REPLICATED = ("norm_pre", "attn_sinks", "conv_b", "mla_q_norm", "mla_kv_norm", "group_norm", "norm_post")
ORDER = ("norm_pre", "w_in", "attn_sinks", "conv_w", "conv_b", "mla_q_norm", "mla_w_uq", "mla_kv_norm",
         "mla_w_ukv", "group_norm", "w_out", "norm_post")
W_IN_COLS = 436
W_IN_WIN = 440
SMALL_ROWS = 48


def _pack_small(arrs, dtype):
    flat = jnp.concatenate([a.reshape(-1).astype(dtype) for a in arrs])
    flat = jnp.concatenate([flat, jnp.zeros((SMALL_ROWS * D_MODEL - flat.shape[0],), dtype)])
    return flat.reshape(SMALL_ROWS, D_MODEL)


def _pack_state(ps, c):
    k = len(ps)
    wt = jnp.transpose(jnp.stack([p["w_in"] for p in ps]), (0, 1, 3, 2))
    gap = jnp.zeros((k, DEPTH, W_IN_WIN - W_IN_COLS, D_MODEL), F32)
    win = jnp.where(c == 0, jnp.concatenate([wt, gap], axis=2), jnp.concatenate([gap, wt], axis=2))
    wout = jnp.stack([p["w_out"] for p in ps]).reshape(k, DEPTH * 128, D_MODEL)
    flat = jnp.stack([jnp.concatenate([p[n].reshape(-1) for n in SMALL_SHARDED + REPLICATED]) for p in ps])
    small = jnp.pad(flat, ((0, 0), (0, SMALL_ROWS * D_MODEL - flat.shape[1]))).reshape(k, SMALL_ROWS, D_MODEL)
    return jnp.concatenate([win.reshape(k, DEPTH * W_IN_WIN, D_MODEL), wout, small], axis=1)


def _unpack_state(buf, p, c):
    k = buf.shape[0]
    nw = DEPTH * W_IN_WIN
    win = buf[:, 0:nw].reshape(k, DEPTH, W_IN_WIN, D_MODEL)
    win = jnp.where(c == 0, win[:, :, 0:W_IN_COLS], win[:, :, W_IN_WIN - W_IN_COLS:W_IN_WIN])
    out = {"w_in": jnp.transpose(win, (0, 1, 3, 2)),
           "w_out": buf[:, nw:nw + DEPTH * 128].reshape(k, DEPTH, 128, D_MODEL)}
    flat = buf[:, nw + DEPTH * 128:].reshape(k, SMALL_ROWS * D_MODEL)
    off = 0
    for n in SMALL_SHARDED + REPLICATED:
        size = int(np.prod(p[n].shape))
        out[n] = flat[:, off:off + size].reshape((k,) + p[n].shape)
        off += size
    return out


def _rows_of_w_in_t(lo, hi, padded, kr):
    segs = ((0, 1664, padded, C_AQ), (1664, 1696, kr, 0), (1696, 2464, padded, C_DQ), (2464, 3488, padded, C_GATE))
    out = []
    for s0, s1, src, base in segs:
        a, b = max(lo, s0), min(hi, s1)
        if a < b:
            out.append(src[base + a - s0:base + b - s0])
    return out


def _me():
    return lax.axis_index("x"), lax.axis_index("y"), lax.axis_index("c")


def _all_gather(block):
    R, C = block.shape

    def body(src_ref, out_ref, send_sems, recv_sems, local_sem):
        x, y, c = _me()
        me, sibling = (x, y, c), (x, y, 1 - c)
        chips = [(1 - x, y), (x, 1 - y), (1 - x, 1 - y)]

        def slot(px, py, pc):
            return out_ref.at[4 * px + 2 * py + pc]

        def copy(k, block, to, src=None):
            return pltpu.make_async_remote_copy(
                src_ref=slot(*block) if src is None else src, dst_ref=slot(*block), send_sem=send_sems.at[k],
                recv_sem=recv_sems.at[k], device_id=to, device_id_type=MESH)

        mine = pltpu.make_async_copy(src_ref, slot(*me), local_sem)
        mine.start()
        first = [copy(0, me, sibling, src=src_ref)]
        first += [copy(1 + j, me, (*chip, c), src=src_ref) for j, chip in enumerate(chips)]
        for cp in first:
            cp.start()
        passed = [copy(4 + j, (*chip, c), sibling) for j, chip in enumerate(chips)]
        for j, chip in enumerate(chips):
            copy(1 + j, (*chip, c), me).wait_recv()
            passed[j].start()
        copy(0, sibling, me).wait_recv()
        for j, chip in enumerate(chips):
            copy(4 + j, (*chip, 1 - c), me).wait_recv()
        for cp in first + passed:
            cp.wait_send()
        mine.wait()

    return pl.pallas_call(
        body, name="all_gather", out_shape=jax.ShapeDtypeStruct((N_DEV, R, C), block.dtype),
        in_specs=[pl.BlockSpec(memory_space=pl.ANY)], out_specs=pl.BlockSpec(memory_space=pl.ANY),
        scratch_shapes=[pltpu.SemaphoreType.DMA((N_DEV - 1,)), pltpu.SemaphoreType.DMA((N_DEV - 1,)),
                        pltpu.SemaphoreType.DMA])(block)


N_CHIP = 4


def _sibling_swap(blocks):
    _, R, C = blocks.shape

    def body(src_ref, out_ref, send_sems, recv_sems):
        x, y, c = _me()
        copies = [pltpu.make_async_remote_copy(
            src_ref=src_ref.at[2 * j + 1 - c], dst_ref=out_ref.at[j], send_sem=send_sems.at[j],
            recv_sem=recv_sems.at[j], device_id=(x, y, 1 - c), device_id_type=MESH) for j in range(N_CHIP)]
        for cp in copies:
            cp.start()
        for cp in copies:
            cp.wait()

    return pl.pallas_call(
        body, name="sibling_swap", out_shape=jax.ShapeDtypeStruct((N_CHIP, R, C), blocks.dtype),
        in_specs=[pl.BlockSpec(memory_space=pl.ANY)], out_specs=pl.BlockSpec(memory_space=pl.ANY),
        scratch_shapes=[pltpu.SemaphoreType.DMA((N_CHIP,)), pltpu.SemaphoreType.DMA((N_CHIP,))])(blocks)


def _pair_sum(a, b):
    n, R, C = a.shape
    tr = 592 if R % 592 == 0 else R

    def body(a_ref, b_ref, o_ref):
        o_ref[...] = (a_ref[...].astype(F32) + b_ref[...].astype(F32)).astype(BF)

    spec = pl.BlockSpec((1, tr, C), lambda j, r: (j, r, 0))
    return pl.pallas_call(body, name="pair_sum", grid=(n, R // tr), in_specs=[spec, spec], out_specs=spec,
                          out_shape=jax.ShapeDtypeStruct(a.shape, BF), compiler_params=_params(2))(a, b)


def _chip_exchange(sums):
    _, R, C = sums.shape

    def body(src_ref, out_ref, send_sems, recv_sems, local_sem):
        x, y, c = _me()
        here = 2 * x + y
        mine = pltpu.make_async_copy(src_ref.at[here], out_ref.at[here], local_sem)
        mine.start()
        copies = []
        for k in range(1, N_CHIP):
            px, py = x ^ (k >> 1), y ^ (k & 1)
            copies.append(pltpu.make_async_remote_copy(
                src_ref=src_ref.at[2 * px + py], dst_ref=out_ref.at[here], send_sem=send_sems.at[k - 1],
                recv_sem=recv_sems.at[k - 1], device_id=(px, py, c), device_id_type=MESH))
        for cp in copies:
            cp.start()
        for cp in copies:
            cp.wait()
        mine.wait()

    return pl.pallas_call(
        body, name="chip_exchange", out_shape=jax.ShapeDtypeStruct((N_CHIP, R, C), sums.dtype),
        in_specs=[pl.BlockSpec(memory_space=pl.ANY)], out_specs=pl.BlockSpec(memory_space=pl.ANY),
        scratch_shapes=[pltpu.SemaphoreType.DMA((N_CHIP - 1,)), pltpu.SemaphoreType.DMA((N_CHIP - 1,)),
                        pltpu.SemaphoreType.DMA])(sums)


def _adamw(parts, state):
    _, R, C = state.shape
    n_parts = parts.shape[0]
    tr = 32
    assert R % tr == 0

    def body(p_ref, s_ref, o_ref):
        g = p_ref[0].astype(F32)
        for k in range(1, n_parts):
            g = g + p_ref[k].astype(F32)
        o_ref[0] = g
        m_ = ADAM_B1 * s_ref[1] + (1.0 - ADAM_B1) * g
        v_ = ADAM_B2 * s_ref[2] + (1.0 - ADAM_B2) * (g * g)
        o_ref[2] = m_
        o_ref[3] = v_
        m_hat = m_ / (1.0 - ADAM_B1 ** ADAM_STEP)
        v_hat = v_ / (1.0 - ADAM_B2 ** ADAM_STEP)
        o_ref[1] = -ADAM_LR * (m_hat / (jnp.sqrt(v_hat) + ADAM_EPS) + ADAM_WD * s_ref[0])

    return pl.pallas_call(
        body, name="adamw", grid=(R // tr,),
        in_specs=[pl.BlockSpec((n_parts, tr, C), lambda n: (0, n, 0)), pl.BlockSpec((3, tr, C), lambda n: (0, n, 0))],
        out_specs=pl.BlockSpec((4, tr, C), lambda n: (0, n, 0)), out_shape=jax.ShapeDtypeStruct((4, R, C), F32),
        compiler_params=_params(1))(parts, state)


def kernel(x, positions, norm_pre, w_in, attn_sinks, conv_w, conv_b, mla_q_norm, mla_w_uq, mla_kv_norm, mla_w_ukv, group_norm, w_out, norm_post, loss_target, m_norm_pre, m_w_in, m_attn_sinks, m_conv_w, m_conv_b, m_mla_q_norm, m_mla_w_uq, m_mla_kv_norm, m_mla_w_ukv, m_group_norm, m_w_out, m_norm_post, v_norm_pre, v_w_in, v_attn_sinks, v_conv_w, v_conv_b, v_mla_q_norm, v_mla_w_uq, v_mla_kv_norm, v_mla_w_ukv, v_group_norm, v_w_out, v_norm_post):
    local = dict(norm_pre=norm_pre, w_in=w_in, attn_sinks=attn_sinks, conv_w=conv_w, conv_b=conv_b,
                 mla_q_norm=mla_q_norm, mla_w_uq=mla_w_uq, mla_kv_norm=mla_kv_norm, mla_w_ukv=mla_w_ukv,
                 group_norm=group_norm, w_out=w_out, norm_post=norm_post)
    mom = dict(norm_pre=m_norm_pre, w_in=m_w_in, attn_sinks=m_attn_sinks, conv_w=m_conv_w, conv_b=m_conv_b,
               mla_q_norm=m_mla_q_norm, mla_w_uq=m_mla_w_uq, mla_kv_norm=m_mla_kv_norm, mla_w_ukv=m_mla_w_ukv,
               group_norm=m_group_norm, w_out=m_w_out, norm_post=m_norm_post)
    vel = dict(norm_pre=v_norm_pre, w_in=v_w_in, attn_sinks=v_attn_sinks, conv_w=v_conv_w, conv_b=v_conv_b,
               mla_q_norm=v_mla_q_norm, mla_w_uq=v_mla_w_uq, mla_kv_norm=v_mla_kv_norm, mla_w_ukv=v_mla_w_ukv,
               group_norm=v_group_norm, w_out=v_w_out, norm_post=v_norm_post)

    c = lax.axis_index("c")

    tile = 16
    slot_rows = 464
    shift = 8 * lax.axis_index("y") + 4 * c
    wt = lax.dynamic_update_slice(jnp.zeros((DEPTH, slot_rows, D_MODEL), BF),
                                  jnp.transpose(w_in, (0, 2, 1)).astype(BF), (0, shift, 0))
    payload = jnp.concatenate([wt.reshape(DEPTH * slot_rows, D_MODEL),
                               w_out.astype(BF).reshape(DEPTH * 128, D_MODEL),
                               _pack_small([local[n] for n in SMALL_SHARDED], BF)], axis=0)
    gathered = _all_gather(payload)
    W = {n: local[n] for n in REPLICATED}

    def nat_rows(l, lo, hi):
        def piece(d, r0, r1):
            base = slot_rows * l - (W_IN_COLS * d) // tile * tile
            return gathered[d, base + r0:base + r1]

        out, run = [], None
        for r0 in range(lo, hi, tile):
            d0, d1 = r0 // W_IN_COLS, (r0 + tile - 1) // W_IN_COLS
            if d0 == d1 and run is not None and run[0] == d0:
                run = (d0, run[1], r0 + tile)
                continue
            if run is not None:
                out.append(piece(*run))
                run = None
            if d0 == d1:
                run = (d0, r0, r0 + tile)
            else:
                out.append(piece(d0, r0, r0 + tile) + piece(d1, r0, r0 + tile))
        if run is not None:
            out.append(piece(*run))
        return out

    z = lambda n: [jnp.zeros((n, D_MODEL), BF)]
    W["wpt"] = [jnp.concatenate(nat_rows(l, 2464, 3488) + nat_rows(l, 0, 1664) + z(64) + nat_rows(l, 1664, 1696)
                                + z(96) + nat_rows(l, 1680, 1696) + nat_rows(l, 1664, 1680) + z(32)
                                + nat_rows(l, 1696, 2464) + z(NP - C_END), axis=0) for l in range(DEPTH)]
    wo0 = DEPTH * slot_rows
    W["w_out"] = gathered[:, wo0:wo0 + DEPTH * 128].reshape(N_DEV, DEPTH, 128, D_MODEL).transpose(1, 0, 2, 3).reshape(
        DEPTH, D_MODEL, D_MODEL)
    flat = gathered[:, wo0 + DEPTH * 128:].reshape(N_DEV, SMALL_ROWS * D_MODEL)
    off = 0
    for n in SMALL_SHARDED:
        depth, rows, width = local[n].shape
        size = depth * rows * width
        W[n] = flat[:, off:off + size].reshape(N_DEV, depth, rows, width).transpose(1, 2, 0, 3).reshape(
            depth, rows, N_DEV * width)
        off += size

    sq, grad_x, g = _local_step(x[0], positions[0], W, loss_target[0])
    loss = lax.psum(0.5 / D_MODEL * jnp.sum(sq), ("x", "y", "c"))

    cols = []
    for n in SMALL_SHARDED:
        depth, rows, width = local[n].shape
        cols.append(jnp.stack(g[n]).reshape(depth, rows, N_DEV, width).transpose(2, 0, 1, 3).reshape(N_DEV, -1))
    rep = jnp.concatenate([a.reshape(-1) for n in REPLICATED for a in g[n]])
    cols.append(jnp.broadcast_to(rep[None], (N_DEV, rep.shape[0])))
    small = jnp.concatenate(cols, axis=1)
    small = jnp.pad(small, ((0, 0), (0, SMALL_ROWS * D_MODEL - small.shape[1]))).reshape(N_DEV, SMALL_ROWS, D_MODEL)
    krs = [p[C_CKR + 64:C_CKR + 96] + _swap_rows32(p[C_CKRS + 64:C_CKRS + 96]) for p in g["w_in_pt"]]
    pieces = []
    for d in range(N_DEV):
        lo = W_IN_COLS * d // 8 * 8
        for l in range(DEPTH):
            pieces += _rows_of_w_in_t(lo, lo + W_IN_WIN, g["w_in_pt"][l], krs[l])
        pieces += [g["w_out"][l][128 * d:128 * (d + 1)] for l in range(DEPTH)]
        pieces.append(small[d])
    blocks = jnp.concatenate(pieces, axis=0).astype(BF).reshape(N_DEV, -1, D_MODEL)
    mine = lax.dynamic_index_in_dim(blocks.reshape(N_CHIP, 2, -1, D_MODEL), c, axis=1, keepdims=False)
    received = _chip_exchange(_pair_sum(mine, _sibling_swap(blocks)))

    out = _unpack_state(_adamw(received, _pack_state([local, mom, vel], c)), local, c)
    return (loss, grad_x[None], *[out[n][t] for t in range(4) for n in ORDER])
```

```python
import functools

import jax
import jax.numpy as jnp
import numpy as np
from jax import lax
from jax.experimental import pallas as pl
from jax.experimental.pallas import tpu as pltpu

F32 = jnp.float32
BF = jnp.bfloat16
MESH = pl.DeviceIdType.MESH

D_MODEL = 1024
DEPTH = 2
EPS = 1e-6
N_DEV = 8
VMEM_LIMIT = 56 * 1024 * 1024
NEG = -1e30
MLA_SCALE = 96.0 ** -0.5
SB_SCALE = 0.125
LOG2E = 1.4426950408889634

NP = 3840
C_GATE = 0
C_AQ = 1024
C_AK = 1280
C_AV = 1408
C_BB = 1536
C_BC = 1792
C_BX = 2048
C_CQ = 2304
C_CKV = 2560
C_CKR = 2688
C_CKRS = 2816
C_DQ = 2944
C_DK = 3200
C_DV = 3456
C_END = 3712

def _swap32(a):
    return jnp.concatenate([a[:, 16:32], a[:, 0:16]], axis=1)

ADAM_LR, ADAM_B1, ADAM_B2, ADAM_EPS, ADAM_WD, ADAM_STEP = 0.001, 0.9, 0.999, 1e-08, 0.01, 10


def _dot(a, b):
    return jnp.dot(a, b, preferred_element_type=F32)


def _dot_nt(a, b):
    return lax.dot_general(a, b, (((1,), (1,)), ((), ())), preferred_element_type=F32)


def _dot_tn(a, b):
    return lax.dot_general(a, b, (((0,), (0,)), ((), ())), preferred_element_type=F32)


def _params(n_grid):
    return pltpu.CompilerParams(dimension_semantics=("arbitrary",) * n_grid, vmem_limit_bytes=VMEM_LIMIT)


def _rms_fwd(x, g):
    r = lax.rsqrt(jnp.mean(x * x, axis=-1, keepdims=True) + EPS)
    return (x * r) * g, r


def _rms_bwd(x, g, r, dy, width=None):
    n = x.shape[-1] if width is None else width
    u = dy * g
    dx = r * u - x * (r * r * r) * (jnp.sum(x * u, axis=-1, keepdims=True) / n)
    return dx, dy * (x * r)


def _iota(shape, axis):
    return lax.broadcasted_iota(jnp.int32, shape, axis)


def _inproj_fwd(x, g, wpt):
    T = x.shape[0]
    tm = 256

    def body(x_ref, g_ref, w_ref, h32_ref, h16_ref, xn_ref):
        xn, _ = _rms_fwd(x_ref[...], g_ref[...])
        xn = xn.astype(BF)
        xn_ref[...] = xn
        h = _dot_nt(xn, w_ref[...])
        h32_ref[...] = h
        h16_ref[...] = h.astype(BF)

    return pl.pallas_call(
        body, name="inproj_fwd", grid=(T // tm,),
        in_specs=[pl.BlockSpec((tm, D_MODEL), lambda n: (n, 0)),
                  pl.BlockSpec((1, D_MODEL), lambda n: (0, 0)),
                  pl.BlockSpec((NP, D_MODEL), lambda n: (0, 0))],
        out_specs=[pl.BlockSpec((tm, NP), lambda n: (n, 0)),
                   pl.BlockSpec((tm, NP), lambda n: (n, 0)),
                   pl.BlockSpec((tm, D_MODEL), lambda n: (n, 0))],
        out_shape=[jax.ShapeDtypeStruct((T, NP), F32), jax.ShapeDtypeStruct((T, NP), BF),
                   jax.ShapeDtypeStruct((T, D_MODEL), BF)],
        compiler_params=_params(1))(x, g, wpt)


def _inproj_bwd(parts, wpt, x, xn, g, dxo):
    T = x.shape[0]
    tm = 256
    np_ = len(parts)
    chunk = NP // 3
    assert sum(p.shape[1] for p in parts) == C_END and chunk % 128 == 0

    def body(*refs):
        part_refs = refs[:np_]
        w_ref, x_ref, xn_ref, g_ref, dxo_ref, dw_ref, dx_ref, dg_ref = refs[np_:]
        n = pl.program_id(0)

        @pl.when(n == 0)
        def _():
            dw_ref[...] = jnp.zeros_like(dw_ref)
            dg_ref[...] = jnp.zeros_like(dg_ref)

        dh = jnp.concatenate([r[...].astype(BF) for r in part_refs] + [jnp.zeros((tm, NP - C_END), BF)], axis=1)
        xnv = xn_ref[...]
        for cb in range(3):
            cs = slice(cb * chunk, (cb + 1) * chunk)
            dw_ref[:, cs] += _dot_tn(xnv, dh[:, cs])
        dxn = _dot(dh, w_ref[...])
        xv = x_ref[...]
        _, r = _rms_fwd(xv, g_ref[...])
        dx, dgt = _rms_bwd(xv, g_ref[...], r, dxn)
        dx_ref[...] = dxo_ref[...] + dx
        dg_ref[...] += jnp.sum(dgt, axis=0, keepdims=True)

    once = pl.Buffered(1)
    return pl.pallas_call(
        body, name="inproj_bwd", grid=(T // tm,),
        in_specs=[pl.BlockSpec((tm, p.shape[1]), lambda n: (n, 0)) for p in parts]
        + [pl.BlockSpec((NP, D_MODEL), lambda n: (0, 0), pipeline_mode=once),
           pl.BlockSpec((tm, D_MODEL), lambda n: (n, 0)),
           pl.BlockSpec((tm, D_MODEL), lambda n: (n, 0)),
           pl.BlockSpec((1, D_MODEL), lambda n: (0, 0)),
           pl.BlockSpec((tm, D_MODEL), lambda n: (n, 0))],
        out_specs=[pl.BlockSpec((D_MODEL, NP), lambda n: (0, 0), pipeline_mode=once),
                   pl.BlockSpec((tm, D_MODEL), lambda n: (n, 0)),
                   pl.BlockSpec((1, D_MODEL), lambda n: (0, 0))],
        out_shape=[jax.ShapeDtypeStruct((D_MODEL, NP), F32), jax.ShapeDtypeStruct((T, D_MODEL), F32),
                   jax.ShapeDtypeStruct((1, D_MODEL), F32)],
        compiler_params=_params(1))(*parts, wpt, x, xn, g, dxo)


SWA_BLK = 128
SWA_TQ = 1024


def _bdot_nt(a, b):
    return lax.dot_general(a, b, (((2,), (2,)), ((0,), (0,))), preferred_element_type=F32)


def _bdot(a, b):
    return lax.dot_general(a, b, (((2,), (1,)), ((0,), (0,))), preferred_element_type=F32)


def _bdot_tn(a, b):
    return lax.dot_general(a, b, (((1,), (1,)), ((0,), (0,))), preferred_element_type=F32)


def _swa_probs(q, kc, kp, sink, mask_c, mask_p):
    sc = jnp.where(mask_c, _bdot_nt(q, kc) * SB_SCALE, NEG)
    sp = jnp.where(mask_p, _bdot_nt(q, kp) * SB_SCALE, NEG)
    m = jnp.maximum(jnp.maximum(jnp.max(sc, axis=-1, keepdims=True), jnp.max(sp, axis=-1, keepdims=True)), sink)
    pc = jnp.exp(sc - m)
    pp = jnp.exp(sp - m)
    ps = jnp.exp(sink - m)
    inv = 1.0 / (jnp.sum(pc, axis=-1, keepdims=True) + jnp.sum(pp, axis=-1, keepdims=True) + ps)
    return pc * inv, pp * inv, ps * inv


def _swa_masks(n, nb):
    blk = _iota((nb, SWA_BLK, SWA_BLK), 0)
    row = _iota((nb, SWA_BLK, SWA_BLK), 1)
    col = _iota((nb, SWA_BLK, SWA_BLK), 2)
    return col <= row, jnp.logical_and(col > row, jnp.logical_or(blk > 0, n > 0))


def _swa_specs(tq):
    halo = tq // SWA_BLK
    return [pl.BlockSpec(memory_space=pltpu.SMEM),
            pl.BlockSpec((tq, 256), lambda n: (n, C_AQ // 256)),
            pl.BlockSpec((tq, 128), lambda n: (n, C_AK // 128)),
            pl.BlockSpec((SWA_BLK, 128), lambda n: (jnp.maximum(n * halo - 1, 0), C_AK // 128)),
            pl.BlockSpec((tq, 128), lambda n: (n, C_AV // 128)),
            pl.BlockSpec((SWA_BLK, 128), lambda n: (jnp.maximum(n * halo - 1, 0), C_AV // 128))]


def _swa_blocked(cur_ref, prev_ref, gs, nb):
    cur = cur_ref[:, gs].reshape(nb, SWA_BLK, 64)
    prev = jnp.concatenate([prev_ref[:, gs].reshape(1, SWA_BLK, 64), cur[:nb - 1]], axis=0) if nb > 1 \
        else prev_ref[:, gs].reshape(1, SWA_BLK, 64)
    return cur, prev


def _swa_fwd(h16, sinks):
    T = h16.shape[0]
    tq = SWA_TQ if T % SWA_TQ == 0 else SWA_BLK
    nb = tq // SWA_BLK

    def body(s_ref, q_ref, kc_ref, kp_ref, vc_ref, vp_ref, o_ref):
        n = pl.program_id(0)
        mask_c, mask_p = _swa_masks(n, nb)
        for h in range(4):
            hs = slice(h * 64, (h + 1) * 64)
            gs = slice(h // 2 * 64, (h // 2 + 1) * 64)
            kc, kp = _swa_blocked(kc_ref, kp_ref, gs, nb)
            vc, vp = _swa_blocked(vc_ref, vp_ref, gs, nb)
            pc, pp, _ = _swa_probs(q_ref[:, hs].reshape(nb, SWA_BLK, 64), kc, kp, s_ref[h], mask_c, mask_p)
            o_ref[:, hs] = (_bdot(pc.astype(BF), vc) + _bdot(pp.astype(BF), vp)).reshape(tq, 64)

    return pl.pallas_call(
        body, name="swa_fwd", grid=(T // tq,), in_specs=_swa_specs(tq),
        out_specs=pl.BlockSpec((tq, 256), lambda n: (n, 0)),
        out_shape=jax.ShapeDtypeStruct((T, 256), F32),
        compiler_params=_params(1))(sinks, h16, h16, h16, h16, h16)


def _swa_bwd(h16, sinks, dya):
    T = h16.shape[0]
    tq = SWA_TQ if T % SWA_TQ == 0 else SWA_BLK
    nb = tq // SWA_BLK

    def body(s_ref, q_ref, kc_ref, kp_ref, vc_ref, vp_ref, do_ref, dq_ref, dk_ref, dv_ref, ds_ref):
        n = pl.program_id(0)

        @pl.when(n == 0)
        def _():
            dk_ref[...] = jnp.zeros_like(dk_ref)
            dv_ref[...] = jnp.zeros_like(dv_ref)
            ds_ref[...] = jnp.zeros_like(ds_ref)

        mask_c, mask_p = _swa_masks(n, nb)
        rows = pl.ds(pl.multiple_of(n * tq, tq), tq)
        before = pl.ds(pl.multiple_of(jnp.maximum(n * nb - 1, 0) * SWA_BLK, SWA_BLK), SWA_BLK)
        lane = _iota((8, 128), 1)
        row8 = _iota((8, 128), 0)

        def to_keys(own, prev):
            if nb == 1:
                return own
            return own + jnp.concatenate([prev[1:], jnp.zeros((1, SWA_BLK, 64), F32)], axis=0)

        for h in range(4):
            hs = slice(h * 64, (h + 1) * 64)
            gs = slice(h // 2 * 64, (h // 2 + 1) * 64)
            q = q_ref[:, hs].reshape(nb, SWA_BLK, 64)
            kc, kp = _swa_blocked(kc_ref, kp_ref, gs, nb)
            vc, vp = _swa_blocked(vc_ref, vp_ref, gs, nb)
            pc, pp, ps = _swa_probs(q, kc, kp, s_ref[h], mask_c, mask_p)
            pcb, ppb = pc.astype(BF), pp.astype(BF)
            do = do_ref[:, hs].reshape(nb, SWA_BLK, 64)
            dob = do.astype(BF)
            o = _bdot(pcb, vc) + _bdot(ppb, vp)
            dd = jnp.sum(do * o, axis=-1, keepdims=True)
            dsc = (pc * (_bdot_nt(dob, vc) - dd) * SB_SCALE).astype(BF)
            dsp = (pp * (_bdot_nt(dob, vp) - dd) * SB_SCALE).astype(BF)
            dq_ref[:, hs] = (_bdot(dsc, kc) + _bdot(dsp, kp)).reshape(tq, 64).astype(BF)
            dkp, dvp = _bdot_tn(dsp, q), _bdot_tn(ppb, dob)
            dk_ref[rows, gs] += to_keys(_bdot_tn(dsc, q), dkp).reshape(tq, 64)
            dv_ref[rows, gs] += to_keys(_bdot_tn(pcb, dob), dvp).reshape(tq, 64)
            dk_ref[before, gs] += dkp[0]
            dv_ref[before, gs] += dvp[0]
            ds_ref[...] += jnp.where(jnp.logical_and(lane == h, row8 == 0), -jnp.sum(ps * dd), 0.0)

    return pl.pallas_call(
        body, name="swa_bwd", grid=(T // tq,),
        in_specs=_swa_specs(tq) + [pl.BlockSpec((tq, 256), lambda n: (n, 0))],
        out_specs=[pl.BlockSpec((tq, 256), lambda n: (n, 0)),
                   pl.BlockSpec((T, 128), lambda n: (0, 0)),
                   pl.BlockSpec((T, 128), lambda n: (0, 0)),
                   pl.BlockSpec((8, 128), lambda n: (0, 0))],
        out_shape=[jax.ShapeDtypeStruct((T, 256), BF), jax.ShapeDtypeStruct((T, 128), F32),
                   jax.ShapeDtypeStruct((T, 128), F32), jax.ShapeDtypeStruct((8, 128), F32)],
        compiler_params=_params(1))(sinks, h16, h16, h16, h16, h16, dya)


def _conv_u(bc_ref, bx_ref, bch_ref, bxh_ref, n, tm):
    u = bc_ref[...] * bx_ref[...]
    uh = bch_ref[...] * bxh_ref[...] * (n > 0).astype(F32)
    rowi = _iota((tm, 256), 0)
    u1 = jnp.where(rowi == 0, uh[7:8, :], pltpu.roll(u, 1, axis=0))
    u2 = jnp.where(rowi == 0, uh[6:7, :], jnp.where(rowi == 1, uh[7:8, :], pltpu.roll(u, 2, axis=0)))
    return u, u1, u2


def _conv_fwd(h32, cw, cb):
    T = h32.shape[0]
    tm = 512 if T % 512 == 0 else T
    hb = tm // 8

    def body(bb_ref, bc_ref, bx_ref, bch_ref, bxh_ref, w_ref, b_ref, o_ref):
        n = pl.program_id(0)
        u, u1, u2 = _conv_u(bc_ref, bx_ref, bch_ref, bxh_ref, n, tm)
        y = w_ref[0:1, :] * u2 + w_ref[1:2, :] * u1 + w_ref[2:3, :] * u + b_ref[...]
        o_ref[...] = bb_ref[...] * y

    halo = lambda c: pl.BlockSpec((8, 256), lambda n: (jnp.maximum(n * hb - 1, 0), c // 256))
    return pl.pallas_call(
        body, name="conv_fwd", grid=(T // tm,),
        in_specs=[pl.BlockSpec((tm, 256), lambda n: (n, C_BB // 256)),
                  pl.BlockSpec((tm, 256), lambda n: (n, C_BC // 256)),
                  pl.BlockSpec((tm, 256), lambda n: (n, C_BX // 256)),
                  halo(C_BC), halo(C_BX),
                  pl.BlockSpec((8, 256), lambda n: (0, 0)),
                  pl.BlockSpec((1, 256), lambda n: (0, 0))],
        out_specs=pl.BlockSpec((tm, 256), lambda n: (n, 0)),
        out_shape=jax.ShapeDtypeStruct((T, 256), F32),
        compiler_params=_params(1))(h32, h32, h32, h32, h32, cw, cb)


def _conv_bwd(h32, cw, cb, dyb):
    T = h32.shape[0]
    tm = 512 if T % 512 == 0 else T
    hb = tm // 8
    nt = T // tm

    def body(bb_ref, bc_ref, bx_ref, bch_ref, bxh_ref, bbn_ref, dy_ref, dyn_ref, w_ref, b_ref,
             dbb_ref, dbc_ref, dbx_ref, dw_ref):
        n = pl.program_id(0)

        @pl.when(n == 0)
        def _():
            dw_ref[...] = jnp.zeros_like(dw_ref)

        u, u1, u2 = _conv_u(bc_ref, bx_ref, bch_ref, bxh_ref, n, tm)
        w0, w1, w2 = w_ref[0:1, :], w_ref[1:2, :], w_ref[2:3, :]
        y = w0 * u2 + w1 * u1 + w2 * u + b_ref[...]
        dyb_ = dy_ref[...]
        dbb_ref[...] = (dyb_ * y).astype(BF)
        dy = dyb_ * bb_ref[...]
        dyn = dyn_ref[...] * bbn_ref[...] * (n < nt - 1).astype(F32)
        rowi = _iota((tm, 256), 0)
        dy1 = jnp.where(rowi == tm - 1, dyn[0:1, :], pltpu.roll(dy, tm - 1, axis=0))
        dy2 = jnp.where(rowi == tm - 2, dyn[0:1, :],
                        jnp.where(rowi == tm - 1, dyn[1:2, :], pltpu.roll(dy, tm - 2, axis=0)))
        du = w2 * dy + w1 * dy1 + w0 * dy2
        dbc_ref[...] = (du * bx_ref[...]).astype(BF)
        dbx_ref[...] = (du * bc_ref[...]).astype(BF)
        dw_ref[0:1, :] += jnp.sum(dy * u2, axis=0, keepdims=True)
        dw_ref[1:2, :] += jnp.sum(dy * u1, axis=0, keepdims=True)
        dw_ref[2:3, :] += jnp.sum(dy * u, axis=0, keepdims=True)
        dw_ref[3:4, :] += jnp.sum(dy, axis=0, keepdims=True)

    halo = lambda c: pl.BlockSpec((8, 256), lambda n: (jnp.maximum(n * hb - 1, 0), c // 256))
    nxt = lambda c: pl.BlockSpec((8, 256), lambda n: (jnp.minimum((n + 1) * hb, T // 8 - 1), c // 256))
    cur = lambda c: pl.BlockSpec((tm, 256), lambda n: (n, c // 256))
    return pl.pallas_call(
        body, name="conv_bwd", grid=(nt,),
        in_specs=[cur(C_BB), cur(C_BC), cur(C_BX), halo(C_BC), halo(C_BX), nxt(C_BB),
                  cur(0), nxt(0),
                  pl.BlockSpec((8, 256), lambda n: (0, 0)),
                  pl.BlockSpec((1, 256), lambda n: (0, 0))],
        out_specs=[cur(0), cur(0), cur(0), pl.BlockSpec((8, 256), lambda n: (0, 0))],
        out_shape=[jax.ShapeDtypeStruct((T, 256), BF)] * 3 + [jax.ShapeDtypeStruct((8, 256), F32)],
        compiler_params=_params(1))(h32, h32, h32, h32, h32, h32, dyb, dyb, cw, cb)


def _cprep_specs(tm):
    return [pl.BlockSpec((tm, 256), lambda n: (n, C_CQ // 256)),
            pl.BlockSpec((tm, 128), lambda n: (n, C_CKV // 128)),
            pl.BlockSpec((tm, 128), lambda n: (n, C_CKR // 128)),
            pl.BlockSpec((tm, 128), lambda n: (n, C_CKRS // 128)),
            pl.BlockSpec((1, 256), lambda n: (0, 0)),
            pl.BlockSpec((1, 128), lambda n: (0, 0)),
            pl.BlockSpec((tm, 128), lambda n: (n, 0)),
            pl.BlockSpec((tm, 128), lambda n: (n, 0))]


def _cprep_fwd(h32, gq, gkv, wuq2, wkv2, cosk, sin):
    T = h32.shape[0]
    tm = 512 if T % 512 == 0 else T

    def body(cq_ref, ckv_ref, ckr_ref, ckrs_ref, gq_ref, gkv_ref, cos_ref, sin_ref, wuq_ref, wkv_ref,
             q_ref, k_ref, v_ref):
        cosk_, sin_ = cos_ref[...], sin_ref[...]
        cosq = cosk_ + (_iota((tm, 128), 1) < 64).astype(F32)
        cqn, _ = _rms_fwd(cq_ref[...], gq_ref[...])
        q2 = _dot(cqn.astype(BF), wuq_ref[...])
        ckvn, _ = _rms_fwd(ckv_ref[...], gkv_ref[...])
        kv2 = _dot(ckvn.astype(BF), wkv_ref[...])
        kr = ckr_ref[...] * cosk_ + ckrs_ref[...] * sin_
        for h in range(4):
            hs = slice(h * 128, (h + 1) * 128)
            q_ref[:, hs] = ((q2[:, hs] * cosq + q2[:, 512 + h * 128:512 + (h + 1) * 128] * sin_) * MLA_SCALE).astype(BF)
            k_ref[:, hs] = (kv2[:, hs] + kr).astype(BF)
        ones = _iota((tm, 512), 1) % 128 == 64
        v_ref[...] = jnp.where(ones, 1.0, kv2[:, 512:]).astype(BF)

    return pl.pallas_call(
        body, name="cprep_fwd", grid=(T // tm,),
        in_specs=_cprep_specs(tm) + [pl.BlockSpec((256, 1024), lambda n: (0, 0)),
                                     pl.BlockSpec((128, 1024), lambda n: (0, 0))],
        out_specs=[pl.BlockSpec((tm, 512), lambda n: (n, 0))] * 3,
        out_shape=[jax.ShapeDtypeStruct((T, 512), BF)] * 3,
        compiler_params=_params(1))(h32, h32, h32, h32, gq, gkv, cosk, sin, wuq2, wkv2)


def _cprep_bwd(h32, gq, gkv, wuq2t, wkv2t, cosk, sin, dq, dk, dv):
    T = h32.shape[0]
    tm = 512 if T % 512 == 0 else T

    def body(cq_ref, ckv_ref, ckr_ref, ckrs_ref, gq_ref, gkv_ref, cos_ref, sin_ref, wuq_ref, wkv_ref,
             dq_ref, dk_ref, dv_ref,
             dcq_ref, dckv_ref, dckr_ref, dckrs_ref, dwuq_ref, dwkv_ref, dgq_ref, dgkv_ref):
        n = pl.program_id(0)

        @pl.when(n == 0)
        def _():
            dwuq_ref[...] = jnp.zeros_like(dwuq_ref)
            dwkv_ref[...] = jnp.zeros_like(dwkv_ref)
            dgq_ref[...] = jnp.zeros_like(dgq_ref)
            dgkv_ref[...] = jnp.zeros_like(dgkv_ref)

        cosk_, sin_ = cos_ref[...], sin_ref[...]
        cosq = cosk_ + (_iota((tm, 128), 1) < 64).astype(F32)
        dkr = jnp.zeros((tm, 128), F32)
        plain, swapped = [], []
        for h in range(4):
            hs = slice(h * 128, (h + 1) * 128)
            dqh = dq_ref[:, hs] * MLA_SCALE
            plain.append((dqh * cosq).astype(BF))
            swapped.append((dqh * sin_).astype(BF))
            dkr = dkr + dk_ref[:, hs]
        dq2 = jnp.concatenate(plain + swapped, axis=1)
        dkv2 = jnp.concatenate([dk_ref[...].astype(BF), dv_ref[...].astype(BF)], axis=1)
        dckr_ref[...] = (dkr * cosk_).astype(BF)
        dckrs_ref[...] = (dkr * sin_).astype(BF)

        cq, gq_ = cq_ref[...], gq_ref[...]
        cqn, rq = _rms_fwd(cq, gq_)
        dwuq_ref[...] += _dot_tn(cqn.astype(BF), dq2)
        dcq, dgt = _rms_bwd(cq, gq_, rq, _dot(dq2, wuq_ref[...]))
        dcq_ref[...] = dcq.astype(BF)
        dgq_ref[...] += jnp.sum(dgt, axis=0, keepdims=True)

        ckv, gkv_ = ckv_ref[...], gkv_ref[...]
        ckvn, rkv = _rms_fwd(ckv, gkv_)
        dwkv_ref[...] += _dot_tn(ckvn.astype(BF), dkv2)
        dckv, dgt2 = _rms_bwd(ckv, gkv_, rkv, _dot(dkv2, wkv_ref[...]))
        dckv_ref[...] = dckv.astype(BF)
        dgkv_ref[...] += jnp.sum(dgt2, axis=0, keepdims=True)

    row = lambda w: pl.BlockSpec((tm, w), lambda n: (n, 0))
    return pl.pallas_call(
        body, name="cprep_bwd", grid=(T // tm,),
        in_specs=_cprep_specs(tm) + [pl.BlockSpec((1024, 256), lambda n: (0, 0)),
                                     pl.BlockSpec((1024, 128), lambda n: (0, 0)),
                                     row(512), row(512), row(512)],
        out_specs=[row(256), row(128), row(128), row(128),
                   pl.BlockSpec((256, 1024), lambda n: (0, 0)), pl.BlockSpec((128, 1024), lambda n: (0, 0)),
                   pl.BlockSpec((1, 256), lambda n: (0, 0)), pl.BlockSpec((1, 128), lambda n: (0, 0))],
        out_shape=[jax.ShapeDtypeStruct((T, 256), BF), jax.ShapeDtypeStruct((T, 128), BF),
                   jax.ShapeDtypeStruct((T, 128), BF), jax.ShapeDtypeStruct((T, 128), BF),
                   jax.ShapeDtypeStruct((256, 1024), F32), jax.ShapeDtypeStruct((128, 1024), F32),
                   jax.ShapeDtypeStruct((1, 256), F32), jax.ShapeDtypeStruct((1, 128), F32)],
        compiler_params=_params(1))(h32, h32, h32, h32, gq, gkv, cosk, sin, wuq2t, wkv2t, dq, dk, dv)


MLA_TILE = 512
MLA_HEADS_PER_STEP = 4


def _causal_mask(t):
    return _iota((t, t), 1) <= _iota((t, t), 0)


def _mla_fwd(q, k, v):
    T = q.shape[0]
    tq = MLA_TILE

    def body(q_ref, k_ref, v_ref, o_ref, lse_ref):
        i = pl.program_id(1)
        mask = _causal_mask(tq)
        heads = [slice(128 * h, 128 * h + 128) for h in range(MLA_HEADS_PER_STEP)]
        qs = [q_ref[:, hs] for hs in heads]

        def step(j, carry, masked):
            rows = pl.ds(pl.multiple_of(j * tq, tq), tq)
            out = []
            for hh, hs in enumerate(heads):
                m, acc = carry[hh]
                s = _dot_nt(qs[hh], k_ref[rows, hs])
                if masked:
                    s = jnp.where(mask, s, NEG)
                m_new = jnp.maximum(m, jnp.max(s, axis=-1, keepdims=True))
                p = jnp.exp((s - m_new).astype(BF))
                acc = jnp.exp(m - m_new) * acc + _dot(p, v_ref[rows, hs])
                out.append((m_new, acc))
            return tuple(out)

        init = ((jnp.full((tq, 1), NEG, F32), jnp.zeros((tq, 128), F32)),) * len(heads)
        carry = lax.fori_loop(0, i // 2, lambda t, c: step(2 * t + 1, step(2 * t, c, False), False), init)
        carry = lax.cond(i % 2 == 1, lambda c: step(i - 1, c, False), lambda c: c, carry)
        carry = step(i, carry, True)
        for hh, hs in enumerate(heads):
            m, acc = carry[hh]
            l = acc[:, 64:65]
            o_ref[:, hs] = acc * (1.0 / l)
            lse_ref[:, hs] = jnp.broadcast_to(m + jnp.log(l), (tq, 128))

    width = 128 * MLA_HEADS_PER_STEP
    blk = pl.BlockSpec((tq, width), lambda h, i: (i, h))
    full = pl.BlockSpec((T, width), lambda h, i: (0, h))
    return pl.pallas_call(
        body, name="mla_fwd", grid=(4 // MLA_HEADS_PER_STEP, T // tq), in_specs=[blk, full, full],
        out_specs=[blk, blk],
        out_shape=[jax.ShapeDtypeStruct((T, 512), F32), jax.ShapeDtypeStruct((T, 512), F32)],
        compiler_params=_params(2))(q, k, v)


def _mla_bwd(q, k, v, o, lse, do):
    T = q.shape[0]
    tq = MLA_TILE

    def body(q_ref, k_ref, v_ref, o_ref, lse_ref, do_ref, dq_ref, dk_ref, dv_ref):
        i = pl.program_id(1)

        @pl.when(i == 0)
        def _():
            dk_ref[...] = jnp.zeros_like(dk_ref)
            dv_ref[...] = jnp.zeros_like(dv_ref)

        heads = [slice(0, 128), slice(128, 256)]
        mask = _causal_mask(tq)
        qs, dobs, dds, lses = [], [], [], []
        for hs in heads:
            do = do_ref[:, hs]
            qs.append(q_ref[:, hs])
            dobs.append(do.astype(BF))
            dds.append(jnp.sum(do * o_ref[:, hs], axis=-1, keepdims=True))
            lses.append(lse_ref[:, hs.start:hs.start + 1])

        def step(j, dqs, masked):
            rows = pl.ds(pl.multiple_of(j * tq, tq), tq)
            out = []
            for hh, hs in enumerate(heads):
                kj, vj = k_ref[rows, hs], v_ref[rows, hs]
                s = _dot_nt(qs[hh], kj)
                if masked:
                    s = jnp.where(mask, s, NEG)
                p = jnp.exp(s - lses[hh])
                ds = (p * (_dot_nt(dobs[hh], vj) - dds[hh])).astype(BF)
                dk_ref[rows, hs] += _dot_tn(ds, qs[hh])
                dv_ref[rows, hs] += _dot_tn(p.astype(BF), dobs[hh])
                out.append(dqs[hh] + _dot(ds, kj))
            return tuple(out)

        dqs = lax.fori_loop(0, i // 2, lambda t, c: step(2 * t + 1, step(2 * t, c, False), False),
                            (jnp.zeros((tq, 128), F32),) * 2)
        dqs = lax.cond(i % 2 == 1, lambda c: step(i - 1, c, False), lambda c: c, dqs)
        dqs = step(i, dqs, True)
        for hh, hs in enumerate(heads):
            dq_ref[:, hs] = dqs[hh]

    blk = pl.BlockSpec((tq, 256), lambda h, i: (i, h))
    full = pl.BlockSpec((T, 256), lambda h, i: (0, h), pipeline_mode=pl.Buffered(1))
    return pl.pallas_call(
        body, name="mla_bwd", grid=(2, T // tq), in_specs=[blk, full, full, blk, blk, blk],
        out_specs=[blk, full, full],
        out_shape=[jax.ShapeDtypeStruct((T, 512), F32)] * 3,
        compiler_params=_params(2))(q, k, v, o, lse, do)


def _sb_tile(qk, rr, strict, masked, upper):
    z2 = qk * (SB_SCALE * LOG2E)
    l1 = jnp.log2(1.0 + jnp.exp2(-jnp.abs(z2)))
    lk = -jnp.maximum(z2, 0.0) - l1
    if masked:
        lk = jnp.where(strict, lk, 0.0)
    after = rr + _dot(lk.astype(BF), upper)
    ll = jnp.minimum(z2, 0.0) - l1
    a = jnp.exp2(ll + after)
    if masked:
        a = jnp.where(strict, a, 0.0)
    return ll, a, jnp.sum(lk, axis=-1, keepdims=True)


SB_TQ, SB_TK = 256, 256
SB_DEAD = -160.0


def _sb_walk(trips, one_step, carry):
    def alive(c):
        t, cr = c
        top = jnp.maximum(jnp.max(cr[0][0]), jnp.max(cr[1][0]))
        return jnp.logical_and(t < trips, top > SB_DEAD)

    def body(c):
        t, cr = c
        return t + 1, one_step(t, cr)

    return lax.while_loop(alive, body, (jnp.int32(0), carry))[1]


def _sb_consts(tq, tk):
    row, col = _iota((tq, tk), 0), _iota((tq, tk), 1)
    strict = [col + d * tk < row for d in range(tq // tk)]
    r2, c2 = _iota((tk, tk), 0), _iota((tk, tk), 1)
    return strict, (r2 > c2).astype(BF), (r2 < c2).astype(BF)


def _sb_fwd(h16):
    T = h16.shape[0]
    tq, tk = SB_TQ, SB_TK
    nd = tq // tk

    def body(q_ref, k_ref, v_ref, o_ref):
        i = pl.program_id(1)
        strict, upper, _ = _sb_consts(tq, tk)
        lane = _iota((tq, 128), 1)
        q2 = q_ref[...]
        qms = [jnp.where(lane < 64, q2, jnp.zeros_like(q2)), jnp.where(lane >= 64, q2, jnp.zeros_like(q2))]

        def step(j, carry, d):
            rows = pl.ds(pl.multiple_of(j * tk, tk), tk)
            kj, vj = k_ref[rows, :], v_ref[rows, :]
            out = []
            for hh in range(2):
                rr, acc = carry[hh]
                _, a, rs = _sb_tile(_dot_nt(qms[hh], kj), rr, None if d is None else strict[d], d is not None, upper)
                out.append((rr + rs, acc + _dot(a.astype(BF), vj)))
            return tuple(out)


        carry = ((jnp.zeros((tq, 1), F32), jnp.zeros((tq, 128), F32)),) * 2
        for d in reversed(range(nd)):
            carry = step(nd * i + d, carry, d)
        carry = _sb_walk(nd * i, lambda t, c: step(nd * i - 1 - t, c, None), carry)
        o_ref[...] = jnp.where(lane < 64, carry[0][1], carry[1][1])

    return pl.pallas_call(
        body, name="sb_fwd", grid=(2, T // tq),
        in_specs=[pl.BlockSpec((tq, 128), lambda p, i: (i, C_DQ // 128 + p)),
                  pl.BlockSpec((T, 128), lambda p, i: (0, C_DK // 128 + p)),
                  pl.BlockSpec((T, 128), lambda p, i: (0, C_DV // 128 + p))],
        out_specs=pl.BlockSpec((tq, 128), lambda p, i: (i, p)),
        out_shape=jax.ShapeDtypeStruct((T, 256), F32),
        compiler_params=_params(2))(h16, h16, h16)


def _sb_bwd(h16, yd, dyd):
    T = h16.shape[0]
    tq, tk = SB_TQ, SB_TK
    nd = tq // tk

    def body(q_ref, k_ref, v_ref, o_ref, do_ref, dq_ref, dk_ref, dv_ref):
        i = pl.program_id(1)

        @pl.when(i == 0)
        def _():
            dk_ref[...] = jnp.zeros_like(dk_ref)
            dv_ref[...] = jnp.zeros_like(dv_ref)

        strict, upper, before = _sb_consts(tq, tk)
        lane = _iota((tq, 128), 1)
        lane_k = _iota((tk, 128), 1)
        q2 = q_ref[...]
        dob2 = do_ref[...].astype(BF)
        doo = dob2.astype(F32) * o_ref[...]
        mines = [lane < 64, lane >= 64]
        qms = [jnp.where(m, q2, jnp.zeros_like(q2)) for m in mines]
        doms = [jnp.where(m, dob2, jnp.zeros_like(dob2)) for m in mines]
        dds = [jnp.sum(jnp.where(m, doo, 0.0), axis=-1, keepdims=True) for m in mines]

        def step(j, carry, d):
            rows = pl.ds(pl.multiple_of(j * tk, tk), tk)
            kj, vj = k_ref[rows, :], v_ref[rows, :]
            out, dks, dvs = [], [], []
            for hh in range(2):
                rr, sg, dq = carry[hh]
                ll, a, rs = _sb_tile(_dot_nt(qms[hh], kj), rr, None if d is None else strict[d], d is not None,
                                     upper)
                ab = a.astype(BF)
                g = _dot_nt(doms[hh], vj) * ab.astype(F32)
                gs = jnp.sum(g, axis=-1, keepdims=True)
                pre = (dds[hh] - sg - gs) + _dot(g.astype(BF), before)
                dz = g - jnp.exp2(ll) * (g + pre)
                if d is not None:
                    dz = jnp.where(strict[d], dz, 0.0)
                dzb = dz.astype(BF)
                dks.append(_dot_tn(dzb, q2))
                dvs.append(_dot_tn(ab, dob2))
                out.append((rr + rs, sg + gs, dq + _dot(dzb, kj)))
            dk_ref[rows, :] += jnp.where(lane_k < 64, dks[0], dks[1]) * SB_SCALE
            dv_ref[rows, :] += jnp.where(lane_k < 64, dvs[0], dvs[1])
            return tuple(out)


        zero = jnp.zeros((tq, 1), F32)
        carry = ((zero, zero, jnp.zeros((tq, 128), F32)),) * 2
        for d in reversed(range(nd)):
            carry = step(nd * i + d, carry, d)
        carry = _sb_walk(nd * i, lambda t, c: step(nd * i - 1 - t, c, None), carry)
        dq_ref[...] = jnp.where(lane < 64, carry[0][2], carry[1][2]) * SB_SCALE

    blk = lambda c: pl.BlockSpec((tq, 128), lambda p, i: (i, c // 128 + p))
    full = lambda c: pl.BlockSpec((T, 128), lambda p, i: (0, c // 128 + p))
    return pl.pallas_call(
        body, name="sb_bwd", grid=(2, T // tq),
        in_specs=[blk(C_DQ), full(C_DK), full(C_DV), blk(0), blk(0)],
        out_specs=[blk(0), full(0), full(0)],
        out_shape=[jax.ShapeDtypeStruct((T, 256), F32)] * 3,
        compiler_params=_params(2))(h16, h16, h16, yd, dyd)


def _compact_c(ycp):
    return jnp.concatenate([ycp[:, h * 128:h * 128 + 64] for h in range(4)], axis=1)


def _post_fwd(ya, yb, ycp, yd, h32, ggrp, wout, gpost, x):
    T = x.shape[0]
    tm = 256

    def body(ya_ref, yb_ref, yc_ref, yd_ref, gate_ref, gg_ref, w_ref, gp_ref, x_ref, xn_ref, ym_ref, o_ref):
        ys = [ya_ref[...], yb_ref[...], _compact_c(yc_ref[...]), yd_ref[...]]
        gate = gate_ref[...]
        sil = gate * (1.0 / (1.0 + jnp.exp(-gate)))
        parts = []
        for gi in range(4):
            ng, _ = _rms_fwd(ys[gi], gg_ref[:, gi * 256:(gi + 1) * 256])
            parts.append(ng * sil[:, gi * 256:(gi + 1) * 256])
        ym = jnp.concatenate(parts, axis=1).astype(BF)
        ym_ref[...] = ym
        o = _dot(ym, w_ref[...])
        o_ref[...] = o
        on, _ = _rms_fwd(o, gp_ref[...])
        xn_ref[...] = x_ref[...] + on

    row = lambda w: pl.BlockSpec((tm, w), lambda n: (n, 0))
    vec = pl.BlockSpec((1, 1024), lambda n: (0, 0))
    return pl.pallas_call(
        body, name="post_fwd", grid=(T // tm,),
        in_specs=[row(256), row(256), row(512), row(256), pl.BlockSpec((tm, 1024), lambda n: (n, C_GATE // 1024)),
                  vec, pl.BlockSpec((1024, 1024), lambda n: (0, 0)), vec, row(1024)],
        out_specs=[row(1024), row(1024), row(1024)],
        out_shape=[jax.ShapeDtypeStruct((T, 1024), F32), jax.ShapeDtypeStruct((T, 1024), BF),
                   jax.ShapeDtypeStruct((T, 1024), F32)],
        compiler_params=_params(1))(ya, yb, ycp, yd, h32, ggrp, wout, gpost, x)


def _post_bwd(dx, o, gpost, woutt, ya, yb, ycp, yd, h32, ggrp, ym):
    T = dx.shape[0]
    tm = 256

    def body(dx_ref, o_ref, gp_ref, w_ref, ya_ref, yb_ref, yc_ref, yd_ref, gate_ref, gg_ref, ym_ref,
             dw_ref, dya_ref, dyb_ref, dyc_ref, dyd_ref, dgate_ref, dgp_ref, dgg_ref):
        n = pl.program_id(0)

        @pl.when(n == 0)
        def _():
            dw_ref[...] = jnp.zeros_like(dw_ref)
            dgp_ref[...] = jnp.zeros_like(dgp_ref)
            dgg_ref[...] = jnp.zeros_like(dgg_ref)

        ov, gp = o_ref[...], gp_ref[...]
        _, ro = _rms_fwd(ov, gp)
        do, dgt = _rms_bwd(ov, gp, ro, dx_ref[...])
        dgp_ref[...] += jnp.sum(dgt, axis=0, keepdims=True)
        dob = do.astype(BF)
        dw_ref[...] += _dot_tn(ym_ref[...], dob)
        dym = _dot(dob, w_ref[...])
        gate = gate_ref[...]
        sg = 1.0 / (1.0 + jnp.exp(-gate))
        sil = gate * sg
        dsil = sg * (1.0 + gate * (1.0 - sg))
        ys = [ya_ref[...], yb_ref[...], _compact_c(yc_ref[...]), yd_ref[...]]
        dys = []
        for gi in range(4):
            gs = slice(gi * 256, (gi + 1) * 256)
            gg = gg_ref[:, gs]
            ng, rg = _rms_fwd(ys[gi], gg)
            dgate_ref[:, gs] = (dym[:, gs] * ng * dsil[:, gs]).astype(BF)
            dy, dgt2 = _rms_bwd(ys[gi], gg, rg, dym[:, gs] * sil[:, gs])
            dgg_ref[:, gs] += jnp.sum(dgt2, axis=0, keepdims=True)
            dys.append(dy)
        dya_ref[...] = dys[0]
        dyb_ref[...] = dys[1]
        dyd_ref[...] = dys[3]
        z64 = jnp.zeros((tm, 64), F32)
        dyc_ref[...] = jnp.concatenate(
            [piece for h in range(4) for piece in (dys[2][:, h * 64:(h + 1) * 64], z64)], axis=1)

    row = lambda w: pl.BlockSpec((tm, w), lambda n: (n, 0))
    vec = pl.BlockSpec((1, 1024), lambda n: (0, 0))
    return pl.pallas_call(
        body, name="post_bwd", grid=(T // tm,),
        in_specs=[row(1024), row(1024), vec, pl.BlockSpec((1024, 1024), lambda n: (0, 0)),
                  row(256), row(256), row(512), row(256),
                  pl.BlockSpec((tm, 1024), lambda n: (n, C_GATE // 1024)), vec, row(1024)],
        out_specs=[pl.BlockSpec((1024, 1024), lambda n: (0, 0)), row(256), row(256), row(512), row(256), row(1024),
                   vec, vec],
        out_shape=[jax.ShapeDtypeStruct((1024, 1024), F32), jax.ShapeDtypeStruct((T, 256), F32),
                   jax.ShapeDtypeStruct((T, 256), F32), jax.ShapeDtypeStruct((T, 512), F32),
                   jax.ShapeDtypeStruct((T, 256), F32), jax.ShapeDtypeStruct((T, 1024), BF),
                   jax.ShapeDtypeStruct((1, 1024), F32), jax.ShapeDtypeStruct((1, 1024), F32)],
        compiler_params=_params(1))(dx, o, gpost, woutt, ya, yb, ycp, yd, h32, ggrp, ym)


def _loss_head(y, tgt):
    T = y.shape[0]
    tm = 512 if T % 512 == 0 else T

    def body(y_ref, t_ref, s_ref, dy_ref):
        n = pl.program_id(0)

        @pl.when(n == 0)
        def _():
            s_ref[...] = jnp.zeros_like(s_ref)

        d = y_ref[...] - t_ref[...]
        s_ref[...] += jnp.sum(d * d, axis=0, keepdims=True)
        dy_ref[...] = d * (1.0 / D_MODEL)

    row = pl.BlockSpec((tm, 1024), lambda n: (n, 0))
    return pl.pallas_call(
        body, name="loss_head", grid=(T // tm,), in_specs=[row, row],
        out_specs=[pl.BlockSpec((1, 1024), lambda n: (0, 0)), row],
        out_shape=[jax.ShapeDtypeStruct((1, 1024), F32), jax.ShapeDtypeStruct((T, 1024), F32)],
        compiler_params=_params(1))(y, tgt)


def _swap_rows32(a):
    return jnp.concatenate([a[16:32], a[0:16]], axis=0)


def _pad_w_uq(w):
    z = lambda n: jnp.zeros((w.shape[0], n), w.dtype)
    a = [p for h in range(4) for p in (w[:, 96 * h:96 * h + 96], z(32))]
    b = [p for h in range(4) for p in (z(64), _swap32(w[:, 96 * h + 64:96 * h + 96]), z(32))]
    return jnp.concatenate(a + b, axis=1)


def _unpad_w_uq(d):
    out = []
    for h in range(4):
        out.append(d[:, 128 * h:128 * h + 64])
        out.append(d[:, 128 * h + 64:128 * h + 96] + _swap32(d[:, 512 + 128 * h + 64:512 + 128 * h + 96]))
    return jnp.concatenate(out, axis=1)


def _pad_w_ukv(w):
    z = jnp.zeros((w.shape[0], 64), w.dtype)
    a = [p for h in range(4) for p in (w[:, 128 * h:128 * h + 64], z)]
    b = [p for h in range(4) for p in (w[:, 128 * h + 64:128 * h + 128], z)]
    return jnp.concatenate(a + b, axis=1)


def _unpad_w_ukv(d):
    return jnp.concatenate([p for h in range(4) for p in (d[:, 128 * h:128 * h + 64],
                                                          d[:, 512 + 128 * h:512 + 128 * h + 64])], axis=1)


def _rope_tables(pos):
    freqs = 10000.0 ** (-jnp.arange(16, dtype=F32) / 16)
    ang = pos.astype(F32)[:, None] * freqs
    c, s = jnp.cos(ang), jnp.sin(ang)
    z = lambda n: jnp.zeros((pos.shape[0], n), F32)
    return (jnp.concatenate([z(64), c, c, z(32)], axis=1), jnp.concatenate([z(64), -s, s, z(32)], axis=1))


def _layer_weights(W, l):
    wuq2 = _pad_w_uq(W["mla_w_uq"][l])
    wkv2 = _pad_w_ukv(W["mla_w_ukv"][l])
    wout = W["w_out"][l]
    cw = jnp.concatenate([W["conv_w"][l].astype(F32), jnp.zeros((5, 256), F32)], axis=0)
    return dict(
        wpt=W["wpt"][l], wuq2=wuq2.astype(BF), wuq2t=wuq2.T.astype(BF),
        wkv2=wkv2.astype(BF), wkv2t=wkv2.T.astype(BF), wout=wout.astype(BF), woutt=wout.T.astype(BF),
        cw=cw, cb=W["conv_b"][l][None, :], sinks=W["attn_sinks"][l],
        gpre=W["norm_pre"][l][None, :], gq=W["mla_q_norm"][l][None, :], gkv=W["mla_kv_norm"][l][None, :],
        ggrp=W["group_norm"][l][None, :], gpost=W["norm_post"][l][None, :])


def _local_step(x, pos, W, tgt):
    cosk, sin = _rope_tables(pos)
    saved = []
    for l in range(DEPTH):
        lw = _layer_weights(W, l)
        h32, h16, xn = _inproj_fwd(x, lw["gpre"], lw["wpt"])
        ya = _swa_fwd(h16, lw["sinks"])
        yb = _conv_fwd(h32, lw["cw"], lw["cb"])
        qc, kc, vc = _cprep_fwd(h32, lw["gq"], lw["gkv"], lw["wuq2"], lw["wkv2"], cosk, sin)
        ycp, lse = _mla_fwd(qc, kc, vc)
        yd = _sb_fwd(h16)
        x_new, ym, o = _post_fwd(ya, yb, ycp, yd, h32, lw["ggrp"], lw["wout"], lw["gpost"], x)
        saved.append(dict(lw=lw, x=x, h32=h32, h16=h16, xn=xn, ya=ya, yb=yb, qc=qc, kc=kc, vc=vc, ycp=ycp,
                          lse=lse, yd=yd, ym=ym, o=o))
        x = x_new
    sq, dx = _loss_head(x, tgt)

    grads = {k: [None] * DEPTH for k in ("norm_pre", "w_in_pt", "attn_sinks", "conv_w", "conv_b", "mla_q_norm",
                                         "mla_w_uq", "mla_kv_norm", "mla_w_ukv", "group_norm", "w_out",
                                         "norm_post")}
    for l in reversed(range(DEPTH)):
        s = saved[l]
        lw = s["lw"]
        dwout, dya, dyb, dycp, dyd, dgate, dgpost, dggrp = _post_bwd(
            dx, s["o"], lw["gpost"], lw["woutt"], s["ya"], s["yb"], s["ycp"], s["yd"], s["h32"], lw["ggrp"], s["ym"])
        grads["norm_post"][l] = dgpost[0]
        grads["group_norm"][l] = dggrp[0]
        grads["w_out"][l] = dwout
        sdq, sdk, sdv = _sb_bwd(s["h16"], s["yd"], dyd)
        mdq, mdk, mdv = _mla_bwd(s["qc"], s["kc"], s["vc"], s["ycp"], s["lse"], dycp)
        dcq, dckv, dckr, dckrs, dwuq2, dwkv2, dgq, dgkv = _cprep_bwd(
            s["h32"], lw["gq"], lw["gkv"], lw["wuq2t"], lw["wkv2t"], cosk, sin, mdq, mdk, mdv)
        grads["mla_q_norm"][l] = dgq[0]
        grads["mla_kv_norm"][l] = dgkv[0]
        grads["mla_w_uq"][l] = _unpad_w_uq(dwuq2)
        grads["mla_w_ukv"][l] = _unpad_w_ukv(dwkv2)
        dbb, dbc, dbx, dcw = _conv_bwd(s["h32"], lw["cw"], lw["cb"], dyb)
        grads["conv_w"][l] = dcw[0:3]
        grads["conv_b"][l] = dcw[3]
        adq, adk, adv, dsk = _swa_bwd(s["h16"], lw["sinks"], dya)
        grads["attn_sinks"][l] = dsk[0, 0:4]
        parts = [dgate, adq, adk, adv, dbb, dbc, dbx, dcq, dckv, dckr, dckrs, sdq, sdk, sdv]
        dwp, dx, dgpre = _inproj_bwd(parts, lw["wpt"], s["x"], s["xn"], lw["gpre"], dx)
        grads["w_in_pt"][l] = dwp.T
        grads["norm_pre"][l] = dgpre[0]
    return sq, dx, grads


SMALL_SHARDED = ("conv_w", "mla_w_uq", "mla_w_ukv")
REPLICATED = ("norm_pre", "attn_sinks", "conv_b", "mla_q_norm", "mla_kv_norm", "group_norm", "norm_post")
ORDER = ("norm_pre", "w_in", "attn_sinks", "conv_w", "conv_b", "mla_q_norm", "mla_w_uq", "mla_kv_norm",
         "mla_w_ukv", "group_norm", "w_out", "norm_post")
W_IN_COLS = 436
W_IN_WIN = 440
SMALL_ROWS = 48


def _pack_small(arrs, dtype):
    flat = jnp.concatenate([a.reshape(-1).astype(dtype) for a in arrs])
    flat = jnp.concatenate([flat, jnp.zeros((SMALL_ROWS * D_MODEL - flat.shape[0],), dtype)])
    return flat.reshape(SMALL_ROWS, D_MODEL)


def _pack_state(ps, c):
    k = len(ps)
    wt = jnp.transpose(jnp.stack([p["w_in"] for p in ps]), (0, 1, 3, 2))
    gap = jnp.zeros((k, DEPTH, W_IN_WIN - W_IN_COLS, D_MODEL), F32)
    win = jnp.where(c == 0, jnp.concatenate([wt, gap], axis=2), jnp.concatenate([gap, wt], axis=2))
    wout = jnp.stack([p["w_out"] for p in ps]).reshape(k, DEPTH * 128, D_MODEL)
    flat = jnp.stack([jnp.concatenate([p[n].reshape(-1) for n in SMALL_SHARDED + REPLICATED]) for p in ps])
    small = jnp.pad(flat, ((0, 0), (0, SMALL_ROWS * D_MODEL - flat.shape[1]))).reshape(k, SMALL_ROWS, D_MODEL)
    return jnp.concatenate([win.reshape(k, DEPTH * W_IN_WIN, D_MODEL), wout, small], axis=1)


def _unpack_state(buf, p, c):
    k = buf.shape[0]
    nw = DEPTH * W_IN_WIN
    win = buf[:, 0:nw].reshape(k, DEPTH, W_IN_WIN, D_MODEL)
    win = jnp.where(c == 0, win[:, :, 0:W_IN_COLS], win[:, :, W_IN_WIN - W_IN_COLS:W_IN_WIN])
    out = {"w_in": jnp.transpose(win, (0, 1, 3, 2)),
           "w_out": buf[:, nw:nw + DEPTH * 128].reshape(k, DEPTH, 128, D_MODEL)}
    flat = buf[:, nw + DEPTH * 128:].reshape(k, SMALL_ROWS * D_MODEL)
    off = 0
    for n in SMALL_SHARDED + REPLICATED:
        size = int(np.prod(p[n].shape))
        out[n] = flat[:, off:off + size].reshape((k,) + p[n].shape)
        off += size
    return out


def _rows_of_w_in_t(lo, hi, padded, kr):
    segs = ((0, 1664, padded, C_AQ), (1664, 1696, kr, 0), (1696, 2464, padded, C_DQ), (2464, 3488, padded, C_GATE))
    out = []
    for s0, s1, src, base in segs:
        a, b = max(lo, s0), min(hi, s1)
        if a < b:
            out.append(src[base + a - s0:base + b - s0])
    return out


def _me():
    return lax.axis_index("x"), lax.axis_index("y"), lax.axis_index("c")


def _all_gather(block):
    R, C = block.shape

    def body(src_ref, out_ref, send_sems, recv_sems, local_sem):
        x, y, c = _me()
        me, sibling = (x, y, c), (x, y, 1 - c)
        chips = [(1 - x, y), (x, 1 - y), (1 - x, 1 - y)]

        def slot(px, py, pc):
            return out_ref.at[4 * px + 2 * py + pc]

        def copy(k, block, to, src=None):
            return pltpu.make_async_remote_copy(
                src_ref=slot(*block) if src is None else src, dst_ref=slot(*block), send_sem=send_sems.at[k],
                recv_sem=recv_sems.at[k], device_id=to, device_id_type=MESH)

        mine = pltpu.make_async_copy(src_ref, slot(*me), local_sem)
        mine.start()
        first = [copy(0, me, sibling, src=src_ref)]
        first += [copy(1 + j, me, (*chip, c), src=src_ref) for j, chip in enumerate(chips)]
        for cp in first:
            cp.start()
        passed = [copy(4 + j, (*chip, c), sibling) for j, chip in enumerate(chips)]
        for j, chip in enumerate(chips):
            copy(1 + j, (*chip, c), me).wait_recv()
            passed[j].start()
        copy(0, sibling, me).wait_recv()
        for j, chip in enumerate(chips):
            copy(4 + j, (*chip, 1 - c), me).wait_recv()
        for cp in first + passed:
            cp.wait_send()
        mine.wait()

    return pl.pallas_call(
        body, name="all_gather", out_shape=jax.ShapeDtypeStruct((N_DEV, R, C), block.dtype),
        in_specs=[pl.BlockSpec(memory_space=pl.ANY)], out_specs=pl.BlockSpec(memory_space=pl.ANY),
        scratch_shapes=[pltpu.SemaphoreType.DMA((N_DEV - 1,)), pltpu.SemaphoreType.DMA((N_DEV - 1,)),
                        pltpu.SemaphoreType.DMA])(block)


N_CHIP = 4


def _sibling_swap(blocks):
    _, R, C = blocks.shape

    def body(src_ref, out_ref, send_sems, recv_sems):
        x, y, c = _me()
        copies = [pltpu.make_async_remote_copy(
            src_ref=src_ref.at[2 * j + 1 - c], dst_ref=out_ref.at[j], send_sem=send_sems.at[j],
            recv_sem=recv_sems.at[j], device_id=(x, y, 1 - c), device_id_type=MESH) for j in range(N_CHIP)]
        for cp in copies:
            cp.start()
        for cp in copies:
            cp.wait()

    return pl.pallas_call(
        body, name="sibling_swap", out_shape=jax.ShapeDtypeStruct((N_CHIP, R, C), blocks.dtype),
        in_specs=[pl.BlockSpec(memory_space=pl.ANY)], out_specs=pl.BlockSpec(memory_space=pl.ANY),
        scratch_shapes=[pltpu.SemaphoreType.DMA((N_CHIP,)), pltpu.SemaphoreType.DMA((N_CHIP,))])(blocks)


def _pair_sum(a, b):
    n, R, C = a.shape
    tr = 592 if R % 592 == 0 else R

    def body(a_ref, b_ref, o_ref):
        o_ref[...] = (a_ref[...].astype(F32) + b_ref[...].astype(F32)).astype(BF)

    spec = pl.BlockSpec((1, tr, C), lambda j, r: (j, r, 0))
    return pl.pallas_call(body, name="pair_sum", grid=(n, R // tr), in_specs=[spec, spec], out_specs=spec,
                          out_shape=jax.ShapeDtypeStruct(a.shape, BF), compiler_params=_params(2))(a, b)


def _chip_exchange(sums):
    _, R, C = sums.shape

    def body(src_ref, out_ref, send_sems, recv_sems, local_sem):
        x, y, c = _me()
        here = 2 * x + y
        mine = pltpu.make_async_copy(src_ref.at[here], out_ref.at[here], local_sem)
        mine.start()
        copies = []
        for k in range(1, N_CHIP):
            px, py = x ^ (k >> 1), y ^ (k & 1)
            copies.append(pltpu.make_async_remote_copy(
                src_ref=src_ref.at[2 * px + py], dst_ref=out_ref.at[here], send_sem=send_sems.at[k - 1],
                recv_sem=recv_sems.at[k - 1], device_id=(px, py, c), device_id_type=MESH))
        for cp in copies:
            cp.start()
        for cp in copies:
            cp.wait()
        mine.wait()

    return pl.pallas_call(
        body, name="chip_exchange", out_shape=jax.ShapeDtypeStruct((N_CHIP, R, C), sums.dtype),
        in_specs=[pl.BlockSpec(memory_space=pl.ANY)], out_specs=pl.BlockSpec(memory_space=pl.ANY),
        scratch_shapes=[pltpu.SemaphoreType.DMA((N_CHIP - 1,)), pltpu.SemaphoreType.DMA((N_CHIP - 1,)),
                        pltpu.SemaphoreType.DMA])(sums)


def _adamw(parts, state):
    _, R, C = state.shape
    n_parts = parts.shape[0]
    tr = 32
    assert R % tr == 0

    def body(p_ref, s_ref, o_ref):
        g = p_ref[0].astype(F32)
        for k in range(1, n_parts):
            g = g + p_ref[k].astype(F32)
        o_ref[0] = g
        m_ = ADAM_B1 * s_ref[1] + (1.0 - ADAM_B1) * g
        v_ = ADAM_B2 * s_ref[2] + (1.0 - ADAM_B2) * (g * g)
        o_ref[2] = m_
        o_ref[3] = v_
        m_hat = m_ / (1.0 - ADAM_B1 ** ADAM_STEP)
        v_hat = v_ / (1.0 - ADAM_B2 ** ADAM_STEP)
        o_ref[1] = -ADAM_LR * (m_hat / (jnp.sqrt(v_hat) + ADAM_EPS) + ADAM_WD * s_ref[0])

    return pl.pallas_call(
        body, name="adamw", grid=(R // tr,),
        in_specs=[pl.BlockSpec((n_parts, tr, C), lambda n: (0, n, 0)), pl.BlockSpec((3, tr, C), lambda n: (0, n, 0))],
        out_specs=pl.BlockSpec((4, tr, C), lambda n: (0, n, 0)), out_shape=jax.ShapeDtypeStruct((4, R, C), F32),
        compiler_params=_params(1))(parts, state)


def kernel(x, positions, norm_pre, w_in, attn_sinks, conv_w, conv_b, mla_q_norm, mla_w_uq, mla_kv_norm, mla_w_ukv, group_norm, w_out, norm_post, loss_target, m_norm_pre, m_w_in, m_attn_sinks, m_conv_w, m_conv_b, m_mla_q_norm, m_mla_w_uq, m_mla_kv_norm, m_mla_w_ukv, m_group_norm, m_w_out, m_norm_post, v_norm_pre, v_w_in, v_attn_sinks, v_conv_w, v_conv_b, v_mla_q_norm, v_mla_w_uq, v_mla_kv_norm, v_mla_w_ukv, v_group_norm, v_w_out, v_norm_post):
    local = dict(norm_pre=norm_pre, w_in=w_in, attn_sinks=attn_sinks, conv_w=conv_w, conv_b=conv_b,
                 mla_q_norm=mla_q_norm, mla_w_uq=mla_w_uq, mla_kv_norm=mla_kv_norm, mla_w_ukv=mla_w_ukv,
                 group_norm=group_norm, w_out=w_out, norm_post=norm_post)
    mom = dict(norm_pre=m_norm_pre, w_in=m_w_in, attn_sinks=m_attn_sinks, conv_w=m_conv_w, conv_b=m_conv_b,
               mla_q_norm=m_mla_q_norm, mla_w_uq=m_mla_w_uq, mla_kv_norm=m_mla_kv_norm, mla_w_ukv=m_mla_w_ukv,
               group_norm=m_group_norm, w_out=m_w_out, norm_post=m_norm_post)
    vel = dict(norm_pre=v_norm_pre, w_in=v_w_in, attn_sinks=v_attn_sinks, conv_w=v_conv_w, conv_b=v_conv_b,
               mla_q_norm=v_mla_q_norm, mla_w_uq=v_mla_w_uq, mla_kv_norm=v_mla_kv_norm, mla_w_ukv=v_mla_w_ukv,
               group_norm=v_group_norm, w_out=v_w_out, norm_post=v_norm_post)

    c = lax.axis_index("c")

    tile = 16
    slot_rows = 464
    shift = 8 * lax.axis_index("y") + 4 * c
    wt = lax.dynamic_update_slice(jnp.zeros((DEPTH, slot_rows, D_MODEL), BF),
                                  jnp.transpose(w_in, (0, 2, 1)).astype(BF), (0, shift, 0))
    payload = jnp.concatenate([wt.reshape(DEPTH * slot_rows, D_MODEL),
                               w_out.astype(BF).reshape(DEPTH * 128, D_MODEL),
                               _pack_small([local[n] for n in SMALL_SHARDED], BF)], axis=0)
    gathered = _all_gather(payload)
    W = {n: local[n] for n in REPLICATED}

    def nat_rows(l, lo, hi):
        def piece(d, r0, r1):
            base = slot_rows * l - (W_IN_COLS * d) // tile * tile
            return gathered[d, base + r0:base + r1]

        out, run = [], None
        for r0 in range(lo, hi, tile):
            d0, d1 = r0 // W_IN_COLS, (r0 + tile - 1) // W_IN_COLS
            if d0 == d1 and run is not None and run[0] == d0:
                run = (d0, run[1], r0 + tile)
                continue
            if run is not None:
                out.append(piece(*run))
                run = None
            if d0 == d1:
                run = (d0, r0, r0 + tile)
            else:
                out.append(piece(d0, r0, r0 + tile) + piece(d1, r0, r0 + tile))
        if run is not None:
            out.append(piece(*run))
        return out

    z = lambda n: [jnp.zeros((n, D_MODEL), BF)]
    W["wpt"] = [jnp.concatenate(nat_rows(l, 2464, 3488) + nat_rows(l, 0, 1664) + z(64) + nat_rows(l, 1664, 1696)
                                + z(96) + nat_rows(l, 1680, 1696) + nat_rows(l, 1664, 1680) + z(32)
                                + nat_rows(l, 1696, 2464) + z(NP - C_END), axis=0) for l in range(DEPTH)]
    wo0 = DEPTH * slot_rows
    W["w_out"] = gathered[:, wo0:wo0 + DEPTH * 128].reshape(N_DEV, DEPTH, 128, D_MODEL).transpose(1, 0, 2, 3).reshape(
        DEPTH, D_MODEL, D_MODEL)
    flat = gathered[:, wo0 + DEPTH * 128:].reshape(N_DEV, SMALL_ROWS * D_MODEL)
    off = 0
    for n in SMALL_SHARDED:
        depth, rows, width = local[n].shape
        size = depth * rows * width
        W[n] = flat[:, off:off + size].reshape(N_DEV, depth, rows, width).transpose(1, 2, 0, 3).reshape(
            depth, rows, N_DEV * width)
        off += size

    sq, grad_x, g = _local_step(x[0], positions[0], W, loss_target[0])
    loss = lax.psum(0.5 / D_MODEL * jnp.sum(sq), ("x", "y", "c"))

    cols = []
    for n in SMALL_SHARDED:
        depth, rows, width = local[n].shape
        cols.append(jnp.stack(g[n]).reshape(depth, rows, N_DEV, width).transpose(2, 0, 1, 3).reshape(N_DEV, -1))
    rep = jnp.concatenate([a.reshape(-1) for n in REPLICATED for a in g[n]])
    cols.append(jnp.broadcast_to(rep[None], (N_DEV, rep.shape[0])))
    small = jnp.concatenate(cols, axis=1)
    small = jnp.pad(small, ((0, 0), (0, SMALL_ROWS * D_MODEL - small.shape[1]))).reshape(N_DEV, SMALL_ROWS, D_MODEL)
    krs = [p[C_CKR + 64:C_CKR + 96] + _swap_rows32(p[C_CKRS + 64:C_CKRS + 96]) for p in g["w_in_pt"]]
    pieces = []
    for d in range(N_DEV):
        lo = W_IN_COLS * d // 8 * 8
        for l in range(DEPTH):
            pieces += _rows_of_w_in_t(lo, lo + W_IN_WIN, g["w_in_pt"][l], krs[l])
        pieces += [g["w_out"][l][128 * d:128 * (d + 1)] for l in range(DEPTH)]
        pieces.append(small[d])
    blocks = jnp.concatenate(pieces, axis=0).astype(BF).reshape(N_DEV, -1, D_MODEL)
    mine = lax.dynamic_index_in_dim(blocks.reshape(N_CHIP, 2, -1, D_MODEL), c, axis=1, keepdims=False)
    received = _chip_exchange(_pair_sum(mine, _sibling_swap(blocks)))

    out = _unpack_state(_adamw(received, _pack_state([local, mom, vel], c)), local, c)
    return (loss, grad_x[None], *[out[n][t] for t in range(4) for n in ORDER])
```

```python
import functools

import jax
import jax.numpy as jnp
import numpy as np
from jax import lax
from jax.experimental import pallas as pl
from jax.experimental.pallas import tpu as pltpu

F32 = jnp.float32
BF = jnp.bfloat16
MESH = pl.DeviceIdType.MESH

D_MODEL = 1024
DEPTH = 2
EPS = 1e-6
N_DEV = 8
VMEM_LIMIT = 56 * 1024 * 1024
NEG = -1e30
MLA_SCALE = 96.0 ** -0.5
SB_SCALE = 0.125
LOG2E = 1.4426950408889634

NP = 3840
C_GATE = 0
C_AQ = 1024
C_AK = 1280
C_AV = 1408
C_BB = 1536
C_BC = 1792
C_BX = 2048
C_CQ = 2304
C_CKV = 2560
C_CKR = 2688
C_CKRS = 2816
C_DQ = 2944
C_DK = 3200
C_DV = 3456
C_END = 3712

def _swap32(a):
    return jnp.concatenate([a[:, 16:32], a[:, 0:16]], axis=1)

ADAM_LR, ADAM_B1, ADAM_B2, ADAM_EPS, ADAM_WD, ADAM_STEP = 0.001, 0.9, 0.999, 1e-08, 0.01, 10


def _dot(a, b):
    return jnp.dot(a, b, preferred_element_type=F32)


def _dot_nt(a, b):
    return lax.dot_general(a, b, (((1,), (1,)), ((), ())), preferred_element_type=F32)


def _dot_tn(a, b):
    return lax.dot_general(a, b, (((0,), (0,)), ((), ())), preferred_element_type=F32)


def _params(n_grid):
    return pltpu.CompilerParams(dimension_semantics=("arbitrary",) * n_grid, vmem_limit_bytes=VMEM_LIMIT)


def _rms_fwd(x, g):
    r = lax.rsqrt(jnp.mean(x * x, axis=-1, keepdims=True) + EPS)
    return (x * r) * g, r


def _rms_bwd(x, g, r, dy, width=None):
    n = x.shape[-1] if width is None else width
    u = dy * g
    dx = r * u - x * (r * r * r) * (jnp.sum(x * u, axis=-1, keepdims=True) / n)
    return dx, dy * (x * r)


def _iota(shape, axis):
    return lax.broadcasted_iota(jnp.int32, shape, axis)


def _inproj_fwd(x, g, wpt):
    T = x.shape[0]
    tm = 256

    def body(x_ref, g_ref, w_ref, h32_ref, h16_ref, xn_ref):
        xn, _ = _rms_fwd(x_ref[...], g_ref[...])
        xn = xn.astype(BF)
        xn_ref[...] = xn
        h = _dot_nt(xn, w_ref[...])
        h32_ref[...] = h
        h16_ref[...] = h.astype(BF)

    return pl.pallas_call(
        body, name="inproj_fwd", grid=(T // tm,),
        in_specs=[pl.BlockSpec((tm, D_MODEL), lambda n: (n, 0)),
                  pl.BlockSpec((1, D_MODEL), lambda n: (0, 0)),
                  pl.BlockSpec((NP, D_MODEL), lambda n: (0, 0))],
        out_specs=[pl.BlockSpec((tm, NP), lambda n: (n, 0)),
                   pl.BlockSpec((tm, NP), lambda n: (n, 0)),
                   pl.BlockSpec((tm, D_MODEL), lambda n: (n, 0))],
        out_shape=[jax.ShapeDtypeStruct((T, NP), F32), jax.ShapeDtypeStruct((T, NP), BF),
                   jax.ShapeDtypeStruct((T, D_MODEL), BF)],
        compiler_params=_params(1))(x, g, wpt)


def _inproj_bwd(parts, wpt, x, xn, g, dxo):
    T = x.shape[0]
    tm = 256
    np_ = len(parts)
    chunk = NP // 3
    assert sum(p.shape[1] for p in parts) == C_END and chunk % 128 == 0

    def body(*refs):
        part_refs = refs[:np_]
        w_ref, x_ref, xn_ref, g_ref, dxo_ref, dw_ref, dx_ref, dg_ref = refs[np_:]
        n = pl.program_id(0)

        @pl.when(n == 0)
        def _():
            dw_ref[...] = jnp.zeros_like(dw_ref)
            dg_ref[...] = jnp.zeros_like(dg_ref)

        dh = jnp.concatenate([r[...].astype(BF) for r in part_refs] + [jnp.zeros((tm, NP - C_END), BF)], axis=1)
        xnv = xn_ref[...]
        for cb in range(3):
            cs = slice(cb * chunk, (cb + 1) * chunk)
            dw_ref[:, cs] += _dot_tn(xnv, dh[:, cs])
        dxn = _dot(dh, w_ref[...])
        xv = x_ref[...]
        _, r = _rms_fwd(xv, g_ref[...])
        dx, dgt = _rms_bwd(xv, g_ref[...], r, dxn)
        dx_ref[...] = dxo_ref[...] + dx
        dg_ref[...] += jnp.sum(dgt, axis=0, keepdims=True)

    once = pl.Buffered(1)
    return pl.pallas_call(
        body, name="inproj_bwd", grid=(T // tm,),
        in_specs=[pl.BlockSpec((tm, p.shape[1]), lambda n: (n, 0)) for p in parts]
        + [pl.BlockSpec((NP, D_MODEL), lambda n: (0, 0), pipeline_mode=once),
           pl.BlockSpec((tm, D_MODEL), lambda n: (n, 0)),
           pl.BlockSpec((tm, D_MODEL), lambda n: (n, 0)),
           pl.BlockSpec((1, D_MODEL), lambda n: (0, 0)),
           pl.BlockSpec((tm, D_MODEL), lambda n: (n, 0))],
        out_specs=[pl.BlockSpec((D_MODEL, NP), lambda n: (0, 0), pipeline_mode=once),
                   pl.BlockSpec((tm, D_MODEL), lambda n: (n, 0)),
                   pl.BlockSpec((1, D_MODEL), lambda n: (0, 0))],
        out_shape=[jax.ShapeDtypeStruct((D_MODEL, NP), F32), jax.ShapeDtypeStruct((T, D_MODEL), F32),
                   jax.ShapeDtypeStruct((1, D_MODEL), F32)],
        compiler_params=_params(1))(*parts, wpt, x, xn, g, dxo)


SWA_BLK = 128
SWA_TQ = 1024


def _bdot_nt(a, b):
    return lax.dot_general(a, b, (((2,), (2,)), ((0,), (0,))), preferred_element_type=F32)


def _bdot(a, b):
    return lax.dot_general(a, b, (((2,), (1,)), ((0,), (0,))), preferred_element_type=F32)


def _bdot_tn(a, b):
    return lax.dot_general(a, b, (((1,), (1,)), ((0,), (0,))), preferred_element_type=F32)


def _swa_probs(q, kc, kp, sink, mask_c, mask_p):
    sc = jnp.where(mask_c, _bdot_nt(q, kc) * SB_SCALE, NEG)
    sp = jnp.where(mask_p, _bdot_nt(q, kp) * SB_SCALE, NEG)
    m = jnp.maximum(jnp.maximum(jnp.max(sc, axis=-1, keepdims=True), jnp.max(sp, axis=-1, keepdims=True)), sink)
    pc = jnp.exp(sc - m)
    pp = jnp.exp(sp - m)
    ps = jnp.exp(sink - m)
    inv = 1.0 / (jnp.sum(pc, axis=-1, keepdims=True) + jnp.sum(pp, axis=-1, keepdims=True) + ps)
    return pc * inv, pp * inv, ps * inv


def _swa_masks(n, nb):
    blk = _iota((nb, SWA_BLK, SWA_BLK), 0)
    row = _iota((nb, SWA_BLK, SWA_BLK), 1)
    col = _iota((nb, SWA_BLK, SWA_BLK), 2)
    return col <= row, jnp.logical_and(col > row, jnp.logical_or(blk > 0, n > 0))


def _swa_specs(tq):
    halo = tq // SWA_BLK
    return [pl.BlockSpec(memory_space=pltpu.SMEM),
            pl.BlockSpec((tq, 256), lambda n: (n, C_AQ // 256)),
            pl.BlockSpec((tq, 128), lambda n: (n, C_AK // 128)),
            pl.BlockSpec((SWA_BLK, 128), lambda n: (jnp.maximum(n * halo - 1, 0), C_AK // 128)),
            pl.BlockSpec((tq, 128), lambda n: (n, C_AV // 128)),
            pl.BlockSpec((SWA_BLK, 128), lambda n: (jnp.maximum(n * halo - 1, 0), C_AV // 128))]


def _swa_blocked(cur_ref, prev_ref, gs, nb):
    cur = cur_ref[:, gs].reshape(nb, SWA_BLK, 64)
    prev = jnp.concatenate([prev_ref[:, gs].reshape(1, SWA_BLK, 64), cur[:nb - 1]], axis=0) if nb > 1 \
        else prev_ref[:, gs].reshape(1, SWA_BLK, 64)
    return cur, prev


def _swa_fwd(h16, sinks):
    T = h16.shape[0]
    tq = SWA_TQ if T % SWA_TQ == 0 else SWA_BLK
    nb = tq // SWA_BLK

    def body(s_ref, q_ref, kc_ref, kp_ref, vc_ref, vp_ref, o_ref):
        n = pl.program_id(0)
        mask_c, mask_p = _swa_masks(n, nb)
        for h in range(4):
            hs = slice(h * 64, (h + 1) * 64)
            gs = slice(h // 2 * 64, (h // 2 + 1) * 64)
            kc, kp = _swa_blocked(kc_ref, kp_ref, gs, nb)
            vc, vp = _swa_blocked(vc_ref, vp_ref, gs, nb)
            pc, pp, _ = _swa_probs(q_ref[:, hs].reshape(nb, SWA_BLK, 64), kc, kp, s_ref[h], mask_c, mask_p)
            o_ref[:, hs] = (_bdot(pc.astype(BF), vc) + _bdot(pp.astype(BF), vp)).reshape(tq, 64)

    return pl.pallas_call(
        body, name="swa_fwd", grid=(T // tq,), in_specs=_swa_specs(tq),
        out_specs=pl.BlockSpec((tq, 256), lambda n: (n, 0)),
        out_shape=jax.ShapeDtypeStruct((T, 256), F32),
        compiler_params=_params(1))(sinks, h16, h16, h16, h16, h16)


def _swa_bwd(h16, sinks, dya):
    T = h16.shape[0]
    tq = SWA_TQ if T % SWA_TQ == 0 else SWA_BLK
    nb = tq // SWA_BLK

    def body(s_ref, q_ref, kc_ref, kp_ref, vc_ref, vp_ref, do_ref, dq_ref, dk_ref, dv_ref, ds_ref):
        n = pl.program_id(0)

        @pl.when(n == 0)
        def _():
            dk_ref[...] = jnp.zeros_like(dk_ref)
            dv_ref[...] = jnp.zeros_like(dv_ref)
            ds_ref[...] = jnp.zeros_like(ds_ref)

        mask_c, mask_p = _swa_masks(n, nb)
        rows = pl.ds(pl.multiple_of(n * tq, tq), tq)
        before = pl.ds(pl.multiple_of(jnp.maximum(n * nb - 1, 0) * SWA_BLK, SWA_BLK), SWA_BLK)
        lane = _iota((8, 128), 1)
        row8 = _iota((8, 128), 0)

        def to_keys(own, prev):
            if nb == 1:
                return own
            return own + jnp.concatenate([prev[1:], jnp.zeros((1, SWA_BLK, 64), F32)], axis=0)

        for h in range(4):
            hs = slice(h * 64, (h + 1) * 64)
            gs = slice(h // 2 * 64, (h // 2 + 1) * 64)
            q = q_ref[:, hs].reshape(nb, SWA_BLK, 64)
            kc, kp = _swa_blocked(kc_ref, kp_ref, gs, nb)
            vc, vp = _swa_blocked(vc_ref, vp_ref, gs, nb)
            pc, pp, ps = _swa_probs(q, kc, kp, s_ref[h], mask_c, mask_p)
            pcb, ppb = pc.astype(BF), pp.astype(BF)
            do = do_ref[:, hs].reshape(nb, SWA_BLK, 64)
            dob = do.astype(BF)
            o = _bdot(pcb, vc) + _bdot(ppb, vp)
            dd = jnp.sum(do * o, axis=-1, keepdims=True)
            dsc = (pc * (_bdot_nt(dob, vc) - dd) * SB_SCALE).astype(BF)
            dsp = (pp * (_bdot_nt(dob, vp) - dd) * SB_SCALE).astype(BF)
            dq_ref[:, hs] = (_bdot(dsc, kc) + _bdot(dsp, kp)).reshape(tq, 64).astype(BF)
            dkp, dvp = _bdot_tn(dsp, q), _bdot_tn(ppb, dob)
            dk_ref[rows, gs] += to_keys(_bdot_tn(dsc, q), dkp).reshape(tq, 64)
            dv_ref[rows, gs] += to_keys(_bdot_tn(pcb, dob), dvp).reshape(tq, 64)
            dk_ref[before, gs] += dkp[0]
            dv_ref[before, gs] += dvp[0]
            ds_ref[...] += jnp.where(jnp.logical_and(lane == h, row8 == 0), -jnp.sum(ps * dd), 0.0)

    return pl.pallas_call(
        body, name="swa_bwd", grid=(T // tq,),
        in_specs=_swa_specs(tq) + [pl.BlockSpec((tq, 256), lambda n: (n, 0))],
        out_specs=[pl.BlockSpec((tq, 256), lambda n: (n, 0)),
                   pl.BlockSpec((T, 128), lambda n: (0, 0)),
                   pl.BlockSpec((T, 128), lambda n: (0, 0)),
                   pl.BlockSpec((8, 128), lambda n: (0, 0))],
        out_shape=[jax.ShapeDtypeStruct((T, 256), BF), jax.ShapeDtypeStruct((T, 128), F32),
                   jax.ShapeDtypeStruct((T, 128), F32), jax.ShapeDtypeStruct((8, 128), F32)],
        compiler_params=_params(1))(sinks, h16, h16, h16, h16, h16, dya)


def _conv_u(bc_ref, bx_ref, bch_ref, bxh_ref, n, tm):
    u = bc_ref[...] * bx_ref[...]
    uh = bch_ref[...] * bxh_ref[...] * (n > 0).astype(F32)
    rowi = _iota((tm, 256), 0)
    u1 = jnp.where(rowi == 0, uh[7:8, :], pltpu.roll(u, 1, axis=0))
    u2 = jnp.where(rowi == 0, uh[6:7, :], jnp.where(rowi == 1, uh[7:8, :], pltpu.roll(u, 2, axis=0)))
    return u, u1, u2


def _conv_fwd(h32, cw, cb):
    T = h32.shape[0]
    tm = 512 if T % 512 == 0 else T
    hb = tm // 8

    def body(bb_ref, bc_ref, bx_ref, bch_ref, bxh_ref, w_ref, b_ref, o_ref):
        n = pl.program_id(0)
        u, u1, u2 = _conv_u(bc_ref, bx_ref, bch_ref, bxh_ref, n, tm)
        y = w_ref[0:1, :] * u2 + w_ref[1:2, :] * u1 + w_ref[2:3, :] * u + b_ref[...]
        o_ref[...] = bb_ref[...] * y

    halo = lambda c: pl.BlockSpec((8, 256), lambda n: (jnp.maximum(n * hb - 1, 0), c // 256))
    return pl.pallas_call(
        body, name="conv_fwd", grid=(T // tm,),
        in_specs=[pl.BlockSpec((tm, 256), lambda n: (n, C_BB // 256)),
                  pl.BlockSpec((tm, 256), lambda n: (n, C_BC // 256)),
                  pl.BlockSpec((tm, 256), lambda n: (n, C_BX // 256)),
                  halo(C_BC), halo(C_BX),
                  pl.BlockSpec((8, 256), lambda n: (0, 0)),
                  pl.BlockSpec((1, 256), lambda n: (0, 0))],
        out_specs=pl.BlockSpec((tm, 256), lambda n: (n, 0)),
        out_shape=jax.ShapeDtypeStruct((T, 256), F32),
        compiler_params=_params(1))(h32, h32, h32, h32, h32, cw, cb)


def _conv_bwd(h32, cw, cb, dyb):
    T = h32.shape[0]
    tm = 512 if T % 512 == 0 else T
    hb = tm // 8
    nt = T // tm

    def body(bb_ref, bc_ref, bx_ref, bch_ref, bxh_ref, bbn_ref, dy_ref, dyn_ref, w_ref, b_ref,
             dbb_ref, dbc_ref, dbx_ref, dw_ref):
        n = pl.program_id(0)

        @pl.when(n == 0)
        def _():
            dw_ref[...] = jnp.zeros_like(dw_ref)

        u, u1, u2 = _conv_u(bc_ref, bx_ref, bch_ref, bxh_ref, n, tm)
        w0, w1, w2 = w_ref[0:1, :], w_ref[1:2, :], w_ref[2:3, :]
        y = w0 * u2 + w1 * u1 + w2 * u + b_ref[...]
        dyb_ = dy_ref[...]
        dbb_ref[...] = (dyb_ * y).astype(BF)
        dy = dyb_ * bb_ref[...]
        dyn = dyn_ref[...] * bbn_ref[...] * (n < nt - 1).astype(F32)
        rowi = _iota((tm, 256), 0)
        dy1 = jnp.where(rowi == tm - 1, dyn[0:1, :], pltpu.roll(dy, tm - 1, axis=0))
        dy2 = jnp.where(rowi == tm - 2, dyn[0:1, :],
                        jnp.where(rowi == tm - 1, dyn[1:2, :], pltpu.roll(dy, tm - 2, axis=0)))
        du = w2 * dy + w1 * dy1 + w0 * dy2
        dbc_ref[...] = (du * bx_ref[...]).astype(BF)
        dbx_ref[...] = (du * bc_ref[...]).astype(BF)
        dw_ref[0:1, :] += jnp.sum(dy * u2, axis=0, keepdims=True)
        dw_ref[1:2, :] += jnp.sum(dy * u1, axis=0, keepdims=True)
        dw_ref[2:3, :] += jnp.sum(dy * u, axis=0, keepdims=True)
        dw_ref[3:4, :] += jnp.sum(dy, axis=0, keepdims=True)

    halo = lambda c: pl.BlockSpec((8, 256), lambda n: (jnp.maximum(n * hb - 1, 0), c // 256))
    nxt = lambda c: pl.BlockSpec((8, 256), lambda n: (jnp.minimum((n + 1) * hb, T // 8 - 1), c // 256))
    cur = lambda c: pl.BlockSpec((tm, 256), lambda n: (n, c // 256))
    return pl.pallas_call(
        body, name="conv_bwd", grid=(nt,),
        in_specs=[cur(C_BB), cur(C_BC), cur(C_BX), halo(C_BC), halo(C_BX), nxt(C_BB),
                  cur(0), nxt(0),
                  pl.BlockSpec((8, 256), lambda n: (0, 0)),
                  pl.BlockSpec((1, 256), lambda n: (0, 0))],
        out_specs=[cur(0), cur(0), cur(0), pl.BlockSpec((8, 256), lambda n: (0, 0))],
        out_shape=[jax.ShapeDtypeStruct((T, 256), BF)] * 3 + [jax.ShapeDtypeStruct((8, 256), F32)],
        compiler_params=_params(1))(h32, h32, h32, h32, h32, h32, dyb, dyb, cw, cb)


def _cprep_specs(tm):
    return [pl.BlockSpec((tm, 256), lambda n: (n, C_CQ // 256)),
            pl.BlockSpec((tm, 128), lambda n: (n, C_CKV // 128)),
            pl.BlockSpec((tm, 128), lambda n: (n, C_CKR // 128)),
            pl.BlockSpec((tm, 128), lambda n: (n, C_CKRS // 128)),
            pl.BlockSpec((1, 256), lambda n: (0, 0)),
            pl.BlockSpec((1, 128), lambda n: (0, 0)),
            pl.BlockSpec((tm, 128), lambda n: (n, 0)),
            pl.BlockSpec((tm, 128), lambda n: (n, 0))]


def _cprep_fwd(h32, gq, gkv, wuq2, wkv2, cosk, sin):
    T = h32.shape[0]
    tm = 512 if T % 512 == 0 else T

    def body(cq_ref, ckv_ref, ckr_ref, ckrs_ref, gq_ref, gkv_ref, cos_ref, sin_ref, wuq_ref, wkv_ref,
             q_ref, k_ref, v_ref):
        cosk_, sin_ = cos_ref[...], sin_ref[...]
        cosq = cosk_ + (_iota((tm, 128), 1) < 64).astype(F32)
        cqn, _ = _rms_fwd(cq_ref[...], gq_ref[...])
        q2 = _dot(cqn.astype(BF), wuq_ref[...])
        ckvn, _ = _rms_fwd(ckv_ref[...], gkv_ref[...])
        kv2 = _dot(ckvn.astype(BF), wkv_ref[...])
        kr = ckr_ref[...] * cosk_ + ckrs_ref[...] * sin_
        for h in range(4):
            hs = slice(h * 128, (h + 1) * 128)
            q_ref[:, hs] = ((q2[:, hs] * cosq + q2[:, 512 + h * 128:512 + (h + 1) * 128] * sin_) * MLA_SCALE).astype(BF)
            k_ref[:, hs] = (kv2[:, hs] + kr).astype(BF)
        ones = _iota((tm, 512), 1) % 128 == 64
        v_ref[...] = jnp.where(ones, 1.0, kv2[:, 512:]).astype(BF)

    return pl.pallas_call(
        body, name="cprep_fwd", grid=(T // tm,),
        in_specs=_cprep_specs(tm) + [pl.BlockSpec((256, 1024), lambda n: (0, 0)),
                                     pl.BlockSpec((128, 1024), lambda n: (0, 0))],
        out_specs=[pl.BlockSpec((tm, 512), lambda n: (n, 0))] * 3,
        out_shape=[jax.ShapeDtypeStruct((T, 512), BF)] * 3,
        compiler_params=_params(1))(h32, h32, h32, h32, gq, gkv, cosk, sin, wuq2, wkv2)


def _cprep_bwd(h32, gq, gkv, wuq2t, wkv2t, cosk, sin, dq, dk, dv):
    T = h32.shape[0]
    tm = 512 if T % 512 == 0 else T

    def body(cq_ref, ckv_ref, ckr_ref, ckrs_ref, gq_ref, gkv_ref, cos_ref, sin_ref, wuq_ref, wkv_ref,
             dq_ref, dk_ref, dv_ref,
             dcq_ref, dckv_ref, dckr_ref, dckrs_ref, dwuq_ref, dwkv_ref, dgq_ref, dgkv_ref):
        n = pl.program_id(0)

        @pl.when(n == 0)
        def _():
            dwuq_ref[...] = jnp.zeros_like(dwuq_ref)
            dwkv_ref[...] = jnp.zeros_like(dwkv_ref)
            dgq_ref[...] = jnp.zeros_like(dgq_ref)
            dgkv_ref[...] = jnp.zeros_like(dgkv_ref)

        cosk_, sin_ = cos_ref[...], sin_ref[...]
        cosq = cosk_ + (_iota((tm, 128), 1) < 64).astype(F32)
        dkr = jnp.zeros((tm, 128), F32)
        plain, swapped = [], []
        for h in range(4):
            hs = slice(h * 128, (h + 1) * 128)
            dqh = dq_ref[:, hs] * MLA_SCALE
            plain.append((dqh * cosq).astype(BF))
            swapped.append((dqh * sin_).astype(BF))
            dkr = dkr + dk_ref[:, hs]
        dq2 = jnp.concatenate(plain + swapped, axis=1)
        dkv2 = jnp.concatenate([dk_ref[...].astype(BF), dv_ref[...].astype(BF)], axis=1)
        dckr_ref[...] = (dkr * cosk_).astype(BF)
        dckrs_ref[...] = (dkr * sin_).astype(BF)

        cq, gq_ = cq_ref[...], gq_ref[...]
        cqn, rq = _rms_fwd(cq, gq_)
        dwuq_ref[...] += _dot_tn(cqn.astype(BF), dq2)
        dcq, dgt = _rms_bwd(cq, gq_, rq, _dot(dq2, wuq_ref[...]))
        dcq_ref[...] = dcq.astype(BF)
        dgq_ref[...] += jnp.sum(dgt, axis=0, keepdims=True)

        ckv, gkv_ = ckv_ref[...], gkv_ref[...]
        ckvn, rkv = _rms_fwd(ckv, gkv_)
        dwkv_ref[...] += _dot_tn(ckvn.astype(BF), dkv2)
        dckv, dgt2 = _rms_bwd(ckv, gkv_, rkv, _dot(dkv2, wkv_ref[...]))
        dckv_ref[...] = dckv.astype(BF)
        dgkv_ref[...] += jnp.sum(dgt2, axis=0, keepdims=True)

    row = lambda w: pl.BlockSpec((tm, w), lambda n: (n, 0))
    return pl.pallas_call(
        body, name="cprep_bwd", grid=(T // tm,),
        in_specs=_cprep_specs(tm) + [pl.BlockSpec((1024, 256), lambda n: (0, 0)),
                                     pl.BlockSpec((1024, 128), lambda n: (0, 0)),
                                     row(512), row(512), row(512)],
        out_specs=[row(256), row(128), row(128), row(128),
                   pl.BlockSpec((256, 1024), lambda n: (0, 0)), pl.BlockSpec((128, 1024), lambda n: (0, 0)),
                   pl.BlockSpec((1, 256), lambda n: (0, 0)), pl.BlockSpec((1, 128), lambda n: (0, 0))],
        out_shape=[jax.ShapeDtypeStruct((T, 256), BF), jax.ShapeDtypeStruct((T, 128), BF),
                   jax.ShapeDtypeStruct((T, 128), BF), jax.ShapeDtypeStruct((T, 128), BF),
                   jax.ShapeDtypeStruct((256, 1024), F32), jax.ShapeDtypeStruct((128, 1024), F32),
                   jax.ShapeDtypeStruct((1, 256), F32), jax.ShapeDtypeStruct((1, 128), F32)],
        compiler_params=_params(1))(h32, h32, h32, h32, gq, gkv, cosk, sin, wuq2t, wkv2t, dq, dk, dv)


MLA_TILE = 512
MLA_HEADS_PER_STEP = 4


def _causal_mask(t):
    return _iota((t, t), 1) <= _iota((t, t), 0)


def _mla_fwd(q, k, v):
    T = q.shape[0]
    tq = MLA_TILE

    def body(q_ref, k_ref, v_ref, o_ref, lse_ref):
        i = pl.program_id(1)
        mask = _causal_mask(tq)
        heads = [slice(128 * h, 128 * h + 128) for h in range(MLA_HEADS_PER_STEP)]
        qs = [q_ref[:, hs] for hs in heads]

        def step(j, carry, masked):
            rows = pl.ds(pl.multiple_of(j * tq, tq), tq)
            out = []
            for hh, hs in enumerate(heads):
                m, acc = carry[hh]
                s = _dot_nt(qs[hh], k_ref[rows, hs])
                if masked:
                    s = jnp.where(mask, s, NEG)
                m_new = jnp.maximum(m, jnp.max(s, axis=-1, keepdims=True))
                p = jnp.exp((s - m_new).astype(BF))
                acc = jnp.exp(m - m_new) * acc + _dot(p, v_ref[rows, hs])
                out.append((m_new, acc))
            return tuple(out)

        init = ((jnp.full((tq, 1), NEG, F32), jnp.zeros((tq, 128), F32)),) * len(heads)
        carry = lax.fori_loop(0, i // 2, lambda t, c: step(2 * t + 1, step(2 * t, c, False), False), init)
        carry = lax.cond(i % 2 == 1, lambda c: step(i - 1, c, False), lambda c: c, carry)
        carry = step(i, carry, True)
        for hh, hs in enumerate(heads):
            m, acc = carry[hh]
            l = acc[:, 64:65]
            o_ref[:, hs] = acc * (1.0 / l)
            lse_ref[:, hs] = jnp.broadcast_to(m + jnp.log(l), (tq, 128))

    width = 128 * MLA_HEADS_PER_STEP
    blk = pl.BlockSpec((tq, width), lambda h, i: (i, h))
    full = pl.BlockSpec((T, width), lambda h, i: (0, h))
    return pl.pallas_call(
        body, name="mla_fwd", grid=(4 // MLA_HEADS_PER_STEP, T // tq), in_specs=[blk, full, full],
        out_specs=[blk, blk],
        out_shape=[jax.ShapeDtypeStruct((T, 512), F32), jax.ShapeDtypeStruct((T, 512), F32)],
        compiler_params=_params(2))(q, k, v)


def _mla_bwd(q, k, v, o, lse, do):
    T = q.shape[0]
    tq = MLA_TILE

    def body(q_ref, k_ref, v_ref, o_ref, lse_ref, do_ref, dq_ref, dk_ref, dv_ref):
        i = pl.program_id(1)

        @pl.when(i == 0)
        def _():
            dk_ref[...] = jnp.zeros_like(dk_ref)
            dv_ref[...] = jnp.zeros_like(dv_ref)

        heads = [slice(0, 128), slice(128, 256)]
        mask = _causal_mask(tq)
        qs, dobs, dds, lses = [], [], [], []
        for hs in heads:
            do = do_ref[:, hs]
            qs.append(q_ref[:, hs])
            dobs.append(do.astype(BF))
            dds.append(jnp.sum(do * o_ref[:, hs], axis=-1, keepdims=True))
            lses.append(lse_ref[:, hs.start:hs.start + 1])

        def step(j, dqs, masked):
            rows = pl.ds(pl.multiple_of(j * tq, tq), tq)
            out = []
            for hh, hs in enumerate(heads):
                kj, vj = k_ref[rows, hs], v_ref[rows, hs]
                s = _dot_nt(qs[hh], kj)
                if masked:
                    s = jnp.where(mask, s, NEG)
                p = jnp.exp(s - lses[hh])
                ds = (p * (_dot_nt(dobs[hh], vj) - dds[hh])).astype(BF)
                dk_ref[rows, hs] += _dot_tn(ds, qs[hh])
                dv_ref[rows, hs] += _dot_tn(p.astype(BF), dobs[hh])
                out.append(dqs[hh] + _dot(ds, kj))
            return tuple(out)

        dqs = lax.fori_loop(0, i // 2, lambda t, c: step(2 * t + 1, step(2 * t, c, False), False),
                            (jnp.zeros((tq, 128), F32),) * 2)
        dqs = lax.cond(i % 2 == 1, lambda c: step(i - 1, c, False), lambda c: c, dqs)
        dqs = step(i, dqs, True)
        for hh, hs in enumerate(heads):
            dq_ref[:, hs] = dqs[hh]

    blk = pl.BlockSpec((tq, 256), lambda h, i: (i, h))
    full = pl.BlockSpec((T, 256), lambda h, i: (0, h), pipeline_mode=pl.Buffered(1))
    return pl.pallas_call(
        body, name="mla_bwd", grid=(2, T // tq), in_specs=[blk, full, full, blk, blk, blk],
        out_specs=[blk, full, full],
        out_shape=[jax.ShapeDtypeStruct((T, 512), F32)] * 3,
        compiler_params=_params(2))(q, k, v, o, lse, do)


def _sb_tile(qk, rr, strict, masked, upper):
    z2 = qk * (SB_SCALE * LOG2E)
    l1 = jnp.log2(1.0 + jnp.exp2(-jnp.abs(z2)))
    lk = -jnp.maximum(z2, 0.0) - l1
    if masked:
        lk = jnp.where(strict, lk, 0.0)
    after = rr + _dot(lk.astype(BF), upper)
    ll = jnp.minimum(z2, 0.0) - l1
    a = jnp.exp2(ll + after)
    if masked:
        a = jnp.where(strict, a, 0.0)
    return ll, a, jnp.sum(lk, axis=-1, keepdims=True)


SB_TQ, SB_TK = 256, 256
SB_DEAD = -160.0


def _sb_walk(trips, one_step, carry):
    def alive(c):
        t, cr = c
        top = jnp.maximum(jnp.max(cr[0][0]), jnp.max(cr[1][0]))
        return jnp.logical_and(t < trips, top > SB_DEAD)

    def body(c):
        t, cr = c
        return t + 1, one_step(t, cr)

    return lax.while_loop(alive, body, (jnp.int32(0), carry))[1]


def _sb_consts(tq, tk):
    row, col = _iota((tq, tk), 0), _iota((tq, tk), 1)
    strict = [col + d * tk < row for d in range(tq // tk)]
    r2, c2 = _iota((tk, tk), 0), _iota((tk, tk), 1)
    return strict, (r2 > c2).astype(BF), (r2 < c2).astype(BF)


def _sb_fwd(h16):
    T = h16.shape[0]
    tq, tk = SB_TQ, SB_TK
    nd = tq // tk

    def body(q_ref, k_ref, v_ref, o_ref):
        i = pl.program_id(1)
        strict, upper, _ = _sb_consts(tq, tk)
        lane = _iota((tq, 128), 1)
        q2 = q_ref[...]
        qms = [jnp.where(lane < 64, q2, jnp.zeros_like(q2)), jnp.where(lane >= 64, q2, jnp.zeros_like(q2))]

        def step(j, carry, d):
            rows = pl.ds(pl.multiple_of(j * tk, tk), tk)
            kj, vj = k_ref[rows, :], v_ref[rows, :]
            out = []
            for hh in range(2):
                rr, acc = carry[hh]
                _, a, rs = _sb_tile(_dot_nt(qms[hh], kj), rr, None if d is None else strict[d], d is not None, upper)
                out.append((rr + rs, acc + _dot(a.astype(BF), vj)))
            return tuple(out)


        carry = ((jnp.zeros((tq, 1), F32), jnp.zeros((tq, 128), F32)),) * 2
        for d in reversed(range(nd)):
            carry = step(nd * i + d, carry, d)
        carry = _sb_walk(nd * i, lambda t, c: step(nd * i - 1 - t, c, None), carry)
        o_ref[...] = jnp.where(lane < 64, carry[0][1], carry[1][1])

    return pl.pallas_call(
        body, name="sb_fwd", grid=(2, T // tq),
        in_specs=[pl.BlockSpec((tq, 128), lambda p, i: (i, C_DQ // 128 + p)),
                  pl.BlockSpec((T, 128), lambda p, i: (0, C_DK // 128 + p)),
                  pl.BlockSpec((T, 128), lambda p, i: (0, C_DV // 128 + p))],
        out_specs=pl.BlockSpec((tq, 128), lambda p, i: (i, p)),
        out_shape=jax.ShapeDtypeStruct((T, 256), F32),
        compiler_params=_params(2))(h16, h16, h16)


def _sb_bwd(h16, yd, dyd):
    T = h16.shape[0]
    tq, tk = SB_TQ, SB_TK
    nd = tq // tk

    def body(q_ref, k_ref, v_ref, o_ref, do_ref, dq_ref, dk_ref, dv_ref):
        i = pl.program_id(1)

        @pl.when(i == 0)
        def _():
            dk_ref[...] = jnp.zeros_like(dk_ref)
            dv_ref[...] = jnp.zeros_like(dv_ref)

        strict, upper, before = _sb_consts(tq, tk)
        lane = _iota((tq, 128), 1)
        lane_k = _iota((tk, 128), 1)
        q2 = q_ref[...]
        dob2 = do_ref[...].astype(BF)
        doo = dob2.astype(F32) * o_ref[...]
        mines = [lane < 64, lane >= 64]
        qms = [jnp.where(m, q2, jnp.zeros_like(q2)) for m in mines]
        doms = [jnp.where(m, dob2, jnp.zeros_like(dob2)) for m in mines]
        dds = [jnp.sum(jnp.where(m, doo, 0.0), axis=-1, keepdims=True) for m in mines]

        def step(j, carry, d):
            rows = pl.ds(pl.multiple_of(j * tk, tk), tk)
            kj, vj = k_ref[rows, :], v_ref[rows, :]
            out, dks, dvs = [], [], []
            for hh in range(2):
                rr, sg, dq = carry[hh]
                ll, a, rs = _sb_tile(_dot_nt(qms[hh], kj), rr, None if d is None else strict[d], d is not None,
                                     upper)
                ab = a.astype(BF)
                g = _dot_nt(doms[hh], vj) * ab.astype(F32)
                gs = jnp.sum(g, axis=-1, keepdims=True)
                pre = (dds[hh] - sg - gs) + _dot(g.astype(BF), before)
                dz = g - jnp.exp2(ll) * (g + pre)
                if d is not None:
                    dz = jnp.where(strict[d], dz, 0.0)
                dzb = dz.astype(BF)
                dks.append(_dot_tn(dzb, q2))
                dvs.append(_dot_tn(ab, dob2))
                out.append((rr + rs, sg + gs, dq + _dot(dzb, kj)))
            dk_ref[rows, :] += jnp.where(lane_k < 64, dks[0], dks[1]) * SB_SCALE
            dv_ref[rows, :] += jnp.where(lane_k < 64, dvs[0], dvs[1])
            return tuple(out)


        zero = jnp.zeros((tq, 1), F32)
        carry = ((zero, zero, jnp.zeros((tq, 128), F32)),) * 2
        for d in reversed(range(nd)):
            carry = step(nd * i + d, carry, d)
        carry = _sb_walk(nd * i, lambda t, c: step(nd * i - 1 - t, c, None), carry)
        dq_ref[...] = jnp.where(lane < 64, carry[0][2], carry[1][2]) * SB_SCALE

    blk = lambda c: pl.BlockSpec((tq, 128), lambda p, i: (i, c // 128 + p))
    full = lambda c: pl.BlockSpec((T, 128), lambda p, i: (0, c // 128 + p))
    return pl.pallas_call(
        body, name="sb_bwd", grid=(2, T // tq),
        in_specs=[blk(C_DQ), full(C_DK), full(C_DV), blk(0), blk(0)],
        out_specs=[blk(0), full(0), full(0)],
        out_shape=[jax.ShapeDtypeStruct((T, 256), F32)] * 3,
        compiler_params=_params(2))(h16, h16, h16, yd, dyd)


def _compact_c(ycp):
    return jnp.concatenate([ycp[:, h * 128:h * 128 + 64] for h in range(4)], axis=1)


def _post_fwd(ya, yb, ycp, yd, h32, ggrp, wout, gpost, x):
    T = x.shape[0]
    tm = 256

    def body(ya_ref, yb_ref, yc_ref, yd_ref, gate_ref, gg_ref, w_ref, gp_ref, x_ref, xn_ref, ym_ref, o_ref):
        ys = [ya_ref[...], yb_ref[...], _compact_c(yc_ref[...]), yd_ref[...]]
        gate = gate_ref[...]
        sil = gate * (1.0 / (1.0 + jnp.exp(-gate)))
        parts = []
        for gi in range(4):
            ng, _ = _rms_fwd(ys[gi], gg_ref[:, gi * 256:(gi + 1) * 256])
            parts.append(ng * sil[:, gi * 256:(gi + 1) * 256])
        ym = jnp.concatenate(parts, axis=1).astype(BF)
        ym_ref[...] = ym
        o = _dot(ym, w_ref[...])
        o_ref[...] = o
        on, _ = _rms_fwd(o, gp_ref[...])
        xn_ref[...] = x_ref[...] + on

    row = lambda w: pl.BlockSpec((tm, w), lambda n: (n, 0))
    vec = pl.BlockSpec((1, 1024), lambda n: (0, 0))
    return pl.pallas_call(
        body, name="post_fwd", grid=(T // tm,),
        in_specs=[row(256), row(256), row(512), row(256), pl.BlockSpec((tm, 1024), lambda n: (n, C_GATE // 1024)),
                  vec, pl.BlockSpec((1024, 1024), lambda n: (0, 0)), vec, row(1024)],
        out_specs=[row(1024), row(1024), row(1024)],
        out_shape=[jax.ShapeDtypeStruct((T, 1024), F32), jax.ShapeDtypeStruct((T, 1024), BF),
                   jax.ShapeDtypeStruct((T, 1024), F32)],
        compiler_params=_params(1))(ya, yb, ycp, yd, h32, ggrp, wout, gpost, x)


def _post_bwd(dx, o, gpost, woutt, ya, yb, ycp, yd, h32, ggrp, ym):
    T = dx.shape[0]
    tm = 256

    def body(dx_ref, o_ref, gp_ref, w_ref, ya_ref, yb_ref, yc_ref, yd_ref, gate_ref, gg_ref, ym_ref,
             dw_ref, dya_ref, dyb_ref, dyc_ref, dyd_ref, dgate_ref, dgp_ref, dgg_ref):
        n = pl.program_id(0)

        @pl.when(n == 0)
        def _():
            dw_ref[...] = jnp.zeros_like(dw_ref)
            dgp_ref[...] = jnp.zeros_like(dgp_ref)
            dgg_ref[...] = jnp.zeros_like(dgg_ref)

        ov, gp = o_ref[...], gp_ref[...]
        _, ro = _rms_fwd(ov, gp)
        do, dgt = _rms_bwd(ov, gp, ro, dx_ref[...])
        dgp_ref[...] += jnp.sum(dgt, axis=0, keepdims=True)
        dob = do.astype(BF)
        dw_ref[...] += _dot_tn(ym_ref[...], dob)
        dym = _dot(dob, w_ref[...])
        gate = gate_ref[...]
        sg = 1.0 / (1.0 + jnp.exp(-gate))
        sil = gate * sg
        dsil = sg * (1.0 + gate * (1.0 - sg))
        ys = [ya_ref[...], yb_ref[...], _compact_c(yc_ref[...]), yd_ref[...]]
        dys = []
        for gi in range(4):
            gs = slice(gi * 256, (gi + 1) * 256)
            gg = gg_ref[:, gs]
            ng, rg = _rms_fwd(ys[gi], gg)
            dgate_ref[:, gs] = (dym[:, gs] * ng * dsil[:, gs]).astype(BF)
            dy, dgt2 = _rms_bwd(ys[gi], gg, rg, dym[:, gs] * sil[:, gs])
            dgg_ref[:, gs] += jnp.sum(dgt2, axis=0, keepdims=True)
            dys.append(dy)
        dya_ref[...] = dys[0]
        dyb_ref[...] = dys[1]
        dyd_ref[...] = dys[3]
        z64 = jnp.zeros((tm, 64), F32)
        dyc_ref[...] = jnp.concatenate(
            [piece for h in range(4) for piece in (dys[2][:, h * 64:(h + 1) * 64], z64)], axis=1)

    row = lambda w: pl.BlockSpec((tm, w), lambda n: (n, 0))
    vec = pl.BlockSpec((1, 1024), lambda n: (0, 0))
    return pl.pallas_call(
        body, name="post_bwd", grid=(T // tm,),
        in_specs=[row(1024), row(1024), vec, pl.BlockSpec((1024, 1024), lambda n: (0, 0)),
                  row(256), row(256), row(512), row(256),
                  pl.BlockSpec((tm, 1024), lambda n: (n, C_GATE // 1024)), vec, row(1024)],
        out_specs=[pl.BlockSpec((1024, 1024), lambda n: (0, 0)), row(256), row(256), row(512), row(256), row(1024),
                   vec, vec],
        out_shape=[jax.ShapeDtypeStruct((1024, 1024), F32), jax.ShapeDtypeStruct((T, 256), F32),
                   jax.ShapeDtypeStruct((T, 256), F32), jax.ShapeDtypeStruct((T, 512), F32),
                   jax.ShapeDtypeStruct((T, 256), F32), jax.ShapeDtypeStruct((T, 1024), BF),
                   jax.ShapeDtypeStruct((1, 1024), F32), jax.ShapeDtypeStruct((1, 1024), F32)],
        compiler_params=_params(1))(dx, o, gpost, woutt, ya, yb, ycp, yd, h32, ggrp, ym)


def _loss_head(y, tgt):
    T = y.shape[0]
    tm = 512 if T % 512 == 0 else T

    def body(y_ref, t_ref, s_ref, dy_ref):
        n = pl.program_id(0)

        @pl.when(n == 0)
        def _():
            s_ref[...] = jnp.zeros_like(s_ref)

        d = y_ref[...] - t_ref[...]
        s_ref[...] += jnp.sum(d * d, axis=0, keepdims=True)
        dy_ref[...] = d * (1.0 / D_MODEL)

    row = pl.BlockSpec((tm, 1024), lambda n: (n, 0))
    return pl.pallas_call(
        body, name="loss_head", grid=(T // tm,), in_specs=[row, row],
        out_specs=[pl.BlockSpec((1, 1024), lambda n: (0, 0)), row],
        out_shape=[jax.ShapeDtypeStruct((1, 1024), F32), jax.ShapeDtypeStruct((T, 1024), F32)],
        compiler_params=_params(1))(y, tgt)


def _swap_rows32(a):
    return jnp.concatenate([a[16:32], a[0:16]], axis=0)


def _pad_w_uq(w):
    z = lambda n: jnp.zeros((w.shape[0], n), w.dtype)
    a = [p for h in range(4) for p in (w[:, 96 * h:96 * h + 96], z(32))]
    b = [p for h in range(4) for p in (z(64), _swap32(w[:, 96 * h + 64:96 * h + 96]), z(32))]
    return jnp.concatenate(a + b, axis=1)


def _unpad_w_uq(d):
    out = []
    for h in range(4):
        out.append(d[:, 128 * h:128 * h + 64])
        out.append(d[:, 128 * h + 64:128 * h + 96] + _swap32(d[:, 512 + 128 * h + 64:512 + 128 * h + 96]))
    return jnp.concatenate(out, axis=1)


def _pad_w_ukv(w):
    z = jnp.zeros((w.shape[0], 64), w.dtype)
    a = [p for h in range(4) for p in (w[:, 128 * h:128 * h + 64], z)]
    b = [p for h in range(4) for p in (w[:, 128 * h + 64:128 * h + 128], z)]
    return jnp.concatenate(a + b, axis=1)


def _unpad_w_ukv(d):
    return jnp.concatenate([p for h in range(4) for p in (d[:, 128 * h:128 * h + 64],
                                                          d[:, 512 + 128 * h:512 + 128 * h + 64])], axis=1)


def _rope_tables(pos):
    freqs = 10000.0 ** (-jnp.arange(16, dtype=F32) / 16)
    ang = pos.astype(F32)[:, None] * freqs
    c, s = jnp.cos(ang), jnp.sin(ang)
    z = lambda n: jnp.zeros((pos.shape[0], n), F32)
    return (jnp.concatenate([z(64), c, c, z(32)], axis=1), jnp.concatenate([z(64), -s, s, z(32)], axis=1))


def _layer_weights(W, l):
    wuq2 = _pad_w_uq(W["mla_w_uq"][l])
    wkv2 = _pad_w_ukv(W["mla_w_ukv"][l])
    wout = W["w_out"][l]
    cw = jnp.concatenate([W["conv_w"][l].astype(F32), jnp.zeros((5, 256), F32)], axis=0)
    return dict(
        wpt=W["wpt"][l], wuq2=wuq2.astype(BF), wuq2t=wuq2.T.astype(BF),
        wkv2=wkv2.astype(BF), wkv2t=wkv2.T.astype(BF), wout=wout.astype(BF), woutt=wout.T.astype(BF),
        cw=cw, cb=W["conv_b"][l][None, :], sinks=W["attn_sinks"][l],
        gpre=W["norm_pre"][l][None, :], gq=W["mla_q_norm"][l][None, :], gkv=W["mla_kv_norm"][l][None, :],
        ggrp=W["group_norm"][l][None, :], gpost=W["norm_post"][l][None, :])


def _local_step(x, pos, W, tgt):
    cosk, sin = _rope_tables(pos)
    saved = []
    for l in range(DEPTH):
        lw = _layer_weights(W, l)
        h32, h16, xn = _inproj_fwd(x, lw["gpre"], lw["wpt"])
        ya = _swa_fwd(h16, lw["sinks"])
        yb = _conv_fwd(h32, lw["cw"], lw["cb"])
        qc, kc, vc = _cprep_fwd(h32, lw["gq"], lw["gkv"], lw["wuq2"], lw["wkv2"], cosk, sin)
        ycp, lse = _mla_fwd(qc, kc, vc)
        yd = _sb_fwd(h16)
        x_new, ym, o = _post_fwd(ya, yb, ycp, yd, h32, lw["ggrp"], lw["wout"], lw["gpost"], x)
        saved.append(dict(lw=lw, x=x, h32=h32, h16=h16, xn=xn, ya=ya, yb=yb, qc=qc, kc=kc, vc=vc, ycp=ycp,
                          lse=lse, yd=yd, ym=ym, o=o))
        x = x_new
    sq, dx = _loss_head(x, tgt)

    grads = {k: [None] * DEPTH for k in ("norm_pre", "w_in_pt", "attn_sinks", "conv_w", "conv_b", "mla_q_norm",
                                         "mla_w_uq", "mla_kv_norm", "mla_w_ukv", "group_norm", "w_out",
                                         "norm_post")}
    for l in reversed(range(DEPTH)):
        s = saved[l]
        lw = s["lw"]
        dwout, dya, dyb, dycp, dyd, dgate, dgpost, dggrp = _post_bwd(
            dx, s["o"], lw["gpost"], lw["woutt"], s["ya"], s["yb"], s["ycp"], s["yd"], s["h32"], lw["ggrp"], s["ym"])
        grads["norm_post"][l] = dgpost[0]
        grads["group_norm"][l] = dggrp[0]
        grads["w_out"][l] = dwout
        sdq, sdk, sdv = _sb_bwd(s["h16"], s["yd"], dyd)
        mdq, mdk, mdv = _mla_bwd(s["qc"], s["kc"], s["vc"], s["ycp"], s["lse"], dycp)
        dcq, dckv, dckr, dckrs, dwuq2, dwkv2, dgq, dgkv = _cprep_bwd(
            s["h32"], lw["gq"], lw["gkv"], lw["wuq2t"], lw["wkv2t"], cosk, sin, mdq, mdk, mdv)
        grads["mla_q_norm"][l] = dgq[0]
        grads["mla_kv_norm"][l] = dgkv[0]
        grads["mla_w_uq"][l] = _unpad_w_uq(dwuq2)
        grads["mla_w_ukv"][l] = _unpad_w_ukv(dwkv2)
        dbb, dbc, dbx, dcw = _conv_bwd(s["h32"], lw["cw"], lw["cb"], dyb)
        grads["conv_w"][l] = dcw[0:3]
        grads["conv_b"][l] = dcw[3]
        adq, adk, adv, dsk = _swa_bwd(s["h16"], lw["sinks"], dya)
        grads["attn_sinks"][l] = dsk[0, 0:4]
        parts = [dgate, adq, adk, adv, dbb, dbc, dbx, dcq, dckv, dckr, dckrs, sdq, sdk, sdv]
        dwp, dx, dgpre = _inproj_bwd(parts, lw["wpt"], s["x"], s["xn"], lw["gpre"], dx)
        grads["w_in_pt"][l] = dwp.T
        grads["norm_pre"][l] = dgpre[0]
    return sq, dx, grads


SMALL_SHARDED = ("conv_w", "mla_w_uq", "mla_w_ukv")
REPLICATED = ("norm_pre", "attn_sinks", "conv_b", "mla_q_norm", "mla_kv_norm", "group_norm", "norm_post")
ORDER = ("norm_pre", "w_in", "attn_sinks", "conv_w", "conv_b", "mla_q_norm", "mla_w_uq", "mla_kv_norm",
         "mla_w_ukv", "group_norm", "w_out", "norm_post")
W_IN_COLS = 436
W_IN_WIN = 440
SMALL_ROWS = 48


def _pack_small(arrs, dtype):
    flat = jnp.concatenate([a.reshape(-1).astype(dtype) for a in arrs])
    flat = jnp.concatenate([flat, jnp.zeros((SMALL_ROWS * D_MODEL - flat.shape[0],), dtype)])
    return flat.reshape(SMALL_ROWS, D_MODEL)


TAIL_ROW0 = DEPTH * W_IN_WIN


def _pack_state(ps):
    k = len(ps)
    wout = jnp.stack([p["w_out"] for p in ps]).reshape(k, DEPTH * 128, D_MODEL)
    flat = jnp.stack([jnp.concatenate([p[n].reshape(-1) for n in SMALL_SHARDED + REPLICATED]) for p in ps])
    small = jnp.pad(flat, ((0, 0), (0, SMALL_ROWS * D_MODEL - flat.shape[1]))).reshape(k, SMALL_ROWS, D_MODEL)
    return jnp.concatenate([wout, small], axis=1)


def _unpack_state(buf, p):
    k = buf.shape[0]
    out = {"w_out": buf[:, 0:DEPTH * 128].reshape(k, DEPTH, 128, D_MODEL)}
    flat = buf[:, DEPTH * 128:].reshape(k, SMALL_ROWS * D_MODEL)
    off = 0
    for n in SMALL_SHARDED + REPLICATED:
        size = int(np.prod(p[n].shape))
        out[n] = flat[:, off:off + size].reshape((k,) + p[n].shape)
        off += size
    return out


def _rows_of_w_in_t(lo, hi, padded, kr):
    segs = ((0, 1664, padded, C_AQ), (1664, 1696, kr, 0), (1696, 2464, padded, C_DQ), (2464, 3488, padded, C_GATE))
    out = []
    for s0, s1, src, base in segs:
        a, b = max(lo, s0), min(hi, s1)
        if a < b:
            out.append(src[base + a - s0:base + b - s0])
    return out


def _me():
    return lax.axis_index("x"), lax.axis_index("y"), lax.axis_index("c")


def _all_gather(block):
    R, C = block.shape

    def body(src_ref, out_ref, send_sems, recv_sems, local_sem):
        x, y, c = _me()
        me, sibling = (x, y, c), (x, y, 1 - c)
        chips = [(1 - x, y), (x, 1 - y), (1 - x, 1 - y)]

        def slot(px, py, pc):
            return out_ref.at[4 * px + 2 * py + pc]

        def copy(k, block, to, src=None):
            return pltpu.make_async_remote_copy(
                src_ref=slot(*block) if src is None else src, dst_ref=slot(*block), send_sem=send_sems.at[k],
                recv_sem=recv_sems.at[k], device_id=to, device_id_type=MESH)

        mine = pltpu.make_async_copy(src_ref, slot(*me), local_sem)
        mine.start()
        first = [copy(0, me, sibling, src=src_ref)]
        first += [copy(1 + j, me, (*chip, c), src=src_ref) for j, chip in enumerate(chips)]
        for cp in first:
            cp.start()
        passed = [copy(4 + j, (*chip, c), sibling) for j, chip in enumerate(chips)]
        for j, chip in enumerate(chips):
            copy(1 + j, (*chip, c), me).wait_recv()
            passed[j].start()
        copy(0, sibling, me).wait_recv()
        for j, chip in enumerate(chips):
            copy(4 + j, (*chip, 1 - c), me).wait_recv()
        for cp in first + passed:
            cp.wait_send()
        mine.wait()

    return pl.pallas_call(
        body, name="all_gather", out_shape=jax.ShapeDtypeStruct((N_DEV, R, C), block.dtype),
        in_specs=[pl.BlockSpec(memory_space=pl.ANY)], out_specs=pl.BlockSpec(memory_space=pl.ANY),
        scratch_shapes=[pltpu.SemaphoreType.DMA((N_DEV - 1,)), pltpu.SemaphoreType.DMA((N_DEV - 1,)),
                        pltpu.SemaphoreType.DMA])(block)


N_CHIP = 4


def _sibling_swap(blocks):
    _, R, C = blocks.shape

    def body(src_ref, out_ref, send_sems, recv_sems):
        x, y, c = _me()
        copies = [pltpu.make_async_remote_copy(
            src_ref=src_ref.at[2 * j + 1 - c], dst_ref=out_ref.at[j], send_sem=send_sems.at[j],
            recv_sem=recv_sems.at[j], device_id=(x, y, 1 - c), device_id_type=MESH) for j in range(N_CHIP)]
        for cp in copies:
            cp.start()
        for cp in copies:
            cp.wait()

    return pl.pallas_call(
        body, name="sibling_swap", out_shape=jax.ShapeDtypeStruct((N_CHIP, R, C), blocks.dtype),
        in_specs=[pl.BlockSpec(memory_space=pl.ANY)], out_specs=pl.BlockSpec(memory_space=pl.ANY),
        scratch_shapes=[pltpu.SemaphoreType.DMA((N_CHIP,)), pltpu.SemaphoreType.DMA((N_CHIP,))])(blocks)


def _pair_sum(a, b):
    n, R, C = a.shape
    tr = 592 if R % 592 == 0 else R

    def body(a_ref, b_ref, o_ref):
        o_ref[...] = (a_ref[...].astype(F32) + b_ref[...].astype(F32)).astype(BF)

    spec = pl.BlockSpec((1, tr, C), lambda j, r: (j, r, 0))
    return pl.pallas_call(body, name="pair_sum", grid=(n, R // tr), in_specs=[spec, spec], out_specs=spec,
                          out_shape=jax.ShapeDtypeStruct(a.shape, BF), compiler_params=_params(2))(a, b)


def _chip_exchange(sums):
    _, R, C = sums.shape

    def body(src_ref, out_ref, send_sems, recv_sems, local_sem):
        x, y, c = _me()
        here = 2 * x + y
        mine = pltpu.make_async_copy(src_ref.at[here], out_ref.at[here], local_sem)
        mine.start()
        copies = []
        for k in range(1, N_CHIP):
            px, py = x ^ (k >> 1), y ^ (k & 1)
            copies.append(pltpu.make_async_remote_copy(
                src_ref=src_ref.at[2 * px + py], dst_ref=out_ref.at[here], send_sem=send_sems.at[k - 1],
                recv_sem=recv_sems.at[k - 1], device_id=(px, py, c), device_id_type=MESH))
        for cp in copies:
            cp.start()
        for cp in copies:
            cp.wait()
        mine.wait()

    return pl.pallas_call(
        body, name="chip_exchange", out_shape=jax.ShapeDtypeStruct((N_CHIP, R, C), sums.dtype),
        in_specs=[pl.BlockSpec(memory_space=pl.ANY)], out_specs=pl.BlockSpec(memory_space=pl.ANY),
        scratch_shapes=[pltpu.SemaphoreType.DMA((N_CHIP - 1,)), pltpu.SemaphoreType.DMA((N_CHIP - 1,)),
                        pltpu.SemaphoreType.DMA])(sums)


def _adamw_update(g, w, m, v):
    m_ = ADAM_B1 * m + (1.0 - ADAM_B1) * g
    v_ = ADAM_B2 * v + (1.0 - ADAM_B2) * (g * g)
    m_hat = m_ / (1.0 - ADAM_B1 ** ADAM_STEP)
    v_hat = v_ / (1.0 - ADAM_B2 ** ADAM_STEP)
    return -ADAM_LR * (m_hat / (jnp.sqrt(v_hat) + ADAM_EPS) + ADAM_WD * w), m_, v_


def _adamw(parts, state):
    _, R, C = state.shape
    n_parts = parts.shape[0]
    tr = 16
    assert R % tr == 0 and TAIL_ROW0 % tr == 0

    def body(p_ref, s_ref, o_ref):
        g = p_ref[0].astype(F32)
        for k in range(1, n_parts):
            g = g + p_ref[k].astype(F32)
        o_ref[0] = g
        o_ref[1], o_ref[2], o_ref[3] = _adamw_update(g, s_ref[0], s_ref[1], s_ref[2])

    return pl.pallas_call(
        body, name="adamw", grid=(R // tr,),
        in_specs=[pl.BlockSpec((n_parts, tr, C), lambda n: (0, n + TAIL_ROW0 // tr, 0)),
                  pl.BlockSpec((3, tr, C), lambda n: (0, n, 0))],
        out_specs=pl.BlockSpec((4, tr, C), lambda n: (0, n, 0)), out_shape=jax.ShapeDtypeStruct((4, R, C), F32),
        compiler_params=_params(1))(parts, state)


def _adamw_w_in(parts, w, m, v, core):
    n_parts = parts.shape[0]
    tc = 256

    def body(core_ref, p_ref, w_ref, m_ref, v_ref, o_ref):
        g_t = p_ref[0].astype(F32)
        for k in range(1, n_parts):
            g_t = g_t + p_ref[k].astype(F32)
        g_t = jnp.concatenate([g_t, jnp.zeros((512 - W_IN_WIN, tc), F32)], axis=0).T
        g = jnp.where(core_ref[0] == 0, g_t[:, 0:W_IN_COLS], g_t[:, W_IN_WIN - W_IN_COLS:W_IN_WIN])
        o_ref[0, 0] = g
        o_ref[1, 0], o_ref[2, 0], o_ref[3, 0] = _adamw_update(g, w_ref[0], m_ref[0], v_ref[0])

    nat = pl.BlockSpec((1, tc, W_IN_COLS), lambda l, j: (l, j, 0))
    return pl.pallas_call(
        body, name="adamw_w_in", grid=(DEPTH, D_MODEL // tc),
        in_specs=[pl.BlockSpec(memory_space=pltpu.SMEM),
                  pl.BlockSpec((n_parts, W_IN_WIN, tc), lambda l, j: (0, l, j)), nat, nat, nat],
        out_specs=pl.BlockSpec((4, 1, tc, W_IN_COLS), lambda l, j: (0, l, j, 0)),
        out_shape=jax.ShapeDtypeStruct((4, DEPTH, D_MODEL, W_IN_COLS), F32),
        compiler_params=_params(2))(core, parts, w, m, v)


def kernel(x, positions, norm_pre, w_in, attn_sinks, conv_w, conv_b, mla_q_norm, mla_w_uq, mla_kv_norm, mla_w_ukv, group_norm, w_out, norm_post, loss_target, m_norm_pre, m_w_in, m_attn_sinks, m_conv_w, m_conv_b, m_mla_q_norm, m_mla_w_uq, m_mla_kv_norm, m_mla_w_ukv, m_group_norm, m_w_out, m_norm_post, v_norm_pre, v_w_in, v_attn_sinks, v_conv_w, v_conv_b, v_mla_q_norm, v_mla_w_uq, v_mla_kv_norm, v_mla_w_ukv, v_group_norm, v_w_out, v_norm_post):
    local = dict(norm_pre=norm_pre, w_in=w_in, attn_sinks=attn_sinks, conv_w=conv_w, conv_b=conv_b,
                 mla_q_norm=mla_q_norm, mla_w_uq=mla_w_uq, mla_kv_norm=mla_kv_norm, mla_w_ukv=mla_w_ukv,
                 group_norm=group_norm, w_out=w_out, norm_post=norm_post)
    mom = dict(norm_pre=m_norm_pre, w_in=m_w_in, attn_sinks=m_attn_sinks, conv_w=m_conv_w, conv_b=m_conv_b,
               mla_q_norm=m_mla_q_norm, mla_w_uq=m_mla_w_uq, mla_kv_norm=m_mla_kv_norm, mla_w_ukv=m_mla_w_ukv,
               group_norm=m_group_norm, w_out=m_w_out, norm_post=m_norm_post)
    vel = dict(norm_pre=v_norm_pre, w_in=v_w_in, attn_sinks=v_attn_sinks, conv_w=v_conv_w, conv_b=v_conv_b,
               mla_q_norm=v_mla_q_norm, mla_w_uq=v_mla_w_uq, mla_kv_norm=v_mla_kv_norm, mla_w_ukv=v_mla_w_ukv,
               group_norm=v_group_norm, w_out=v_w_out, norm_post=v_norm_post)

    c = lax.axis_index("c")

    tile = 16
    slot_rows = 464
    shift = 8 * lax.axis_index("y") + 4 * c
    wt = lax.dynamic_update_slice(jnp.zeros((DEPTH, slot_rows, D_MODEL), BF),
                                  jnp.transpose(w_in, (0, 2, 1)).astype(BF), (0, shift, 0))
    payload = jnp.concatenate([wt.reshape(DEPTH * slot_rows, D_MODEL),
                               w_out.astype(BF).reshape(DEPTH * 128, D_MODEL),
                               _pack_small([local[n] for n in SMALL_SHARDED], BF)], axis=0)
    gathered = _all_gather(payload)
    W = {n: local[n] for n in REPLICATED}

    def nat_rows(l, lo, hi):
        def piece(d, r0, r1):
            base = slot_rows * l - (W_IN_COLS * d) // tile * tile
            return gathered[d, base + r0:base + r1]

        out, run = [], None
        for r0 in range(lo, hi, tile):
            d0, d1 = r0 // W_IN_COLS, (r0 + tile - 1) // W_IN_COLS
            if d0 == d1 and run is not None and run[0] == d0:
                run = (d0, run[1], r0 + tile)
                continue
            if run is not None:
                out.append(piece(*run))
                run = None
            if d0 == d1:
                run = (d0, r0, r0 + tile)
            else:
                out.append(piece(d0, r0, r0 + tile) + piece(d1, r0, r0 + tile))
        if run is not None:
            out.append(piece(*run))
        return out

    z = lambda n: [jnp.zeros((n, D_MODEL), BF)]
    W["wpt"] = [jnp.concatenate(nat_rows(l, 2464, 3488) + nat_rows(l, 0, 1664) + z(64) + nat_rows(l, 1664, 1696)
                                + z(96) + nat_rows(l, 1680, 1696) + nat_rows(l, 1664, 1680) + z(32)
                                + nat_rows(l, 1696, 2464) + z(NP - C_END), axis=0) for l in range(DEPTH)]
    wo0 = DEPTH * slot_rows
    W["w_out"] = gathered[:, wo0:wo0 + DEPTH * 128].reshape(N_DEV, DEPTH, 128, D_MODEL).transpose(1, 0, 2, 3).reshape(
        DEPTH, D_MODEL, D_MODEL)
    flat = gathered[:, wo0 + DEPTH * 128:].reshape(N_DEV, SMALL_ROWS * D_MODEL)
    off = 0
    for n in SMALL_SHARDED:
        depth, rows, width = local[n].shape
        size = depth * rows * width
        W[n] = flat[:, off:off + size].reshape(N_DEV, depth, rows, width).transpose(1, 2, 0, 3).reshape(
            depth, rows, N_DEV * width)
        off += size

    sq, grad_x, g = _local_step(x[0], positions[0], W, loss_target[0])
    loss = lax.psum(0.5 / D_MODEL * jnp.sum(sq), ("x", "y", "c"))

    cols = []
    for n in SMALL_SHARDED:
        depth, rows, width = local[n].shape
        cols.append(jnp.stack(g[n]).reshape(depth, rows, N_DEV, width).transpose(2, 0, 1, 3).reshape(N_DEV, -1))
    rep = jnp.concatenate([a.reshape(-1) for n in REPLICATED for a in g[n]])
    cols.append(jnp.broadcast_to(rep[None], (N_DEV, rep.shape[0])))
    small = jnp.concatenate(cols, axis=1)
    small = jnp.pad(small, ((0, 0), (0, SMALL_ROWS * D_MODEL - small.shape[1]))).reshape(N_DEV, SMALL_ROWS, D_MODEL)
    krs = [p[C_CKR + 64:C_CKR + 96] + _swap_rows32(p[C_CKRS + 64:C_CKRS + 96]) for p in g["w_in_pt"]]
    pieces = []
    for d in range(N_DEV):
        lo = W_IN_COLS * d // 8 * 8
        for l in range(DEPTH):
            pieces += _rows_of_w_in_t(lo, lo + W_IN_WIN, g["w_in_pt"][l], krs[l])
        pieces += [g["w_out"][l][128 * d:128 * (d + 1)] for l in range(DEPTH)]
        pieces.append(small[d])
    blocks = jnp.concatenate(pieces, axis=0).astype(BF).reshape(N_DEV, -1, D_MODEL)
    mine = lax.dynamic_index_in_dim(blocks.reshape(N_CHIP, 2, -1, D_MODEL), c, axis=1, keepdims=False)
    received = _chip_exchange(_pair_sum(mine, _sibling_swap(blocks)))

    out = _unpack_state(_adamw(received, _pack_state([local, mom, vel])), local)
    out["w_in"] = _adamw_w_in(received, w_in, m_w_in, v_w_in, c.astype(jnp.int32).reshape(1))
    return (loss, grad_x[None], *[out[n][t] for t in range(4) for n in ORDER])
```

```python
import functools

import jax
import jax.numpy as jnp
import numpy as np
from jax import lax
from jax.experimental import pallas as pl
from jax.experimental.pallas import tpu as pltpu

F32 = jnp.float32
BF = jnp.bfloat16
MESH = pl.DeviceIdType.MESH

D_MODEL = 1024
DEPTH = 2
EPS = 1e-6
N_DEV = 8
VMEM_LIMIT = 56 * 1024 * 1024
NEG = -1e30
MLA_SCALE = 96.0 ** -0.5
SB_SCALE = 0.125
LOG2E = 1.4426950408889634

NP = 3840
C_GATE = 0
C_AQ = 1024
C_AK = 1280
C_AV = 1408
C_BB = 1536
C_BC = 1792
C_BX = 2048
C_CQ = 2304
C_CKV = 2560
C_CKR = 2688
C_CKRS = 2816
C_DQ = 2944
C_DK = 3200
C_DV = 3456
C_END = 3712

def _swap32(a):
    return jnp.concatenate([a[:, 16:32], a[:, 0:16]], axis=1)

ADAM_LR, ADAM_B1, ADAM_B2, ADAM_EPS, ADAM_WD, ADAM_STEP = 0.001, 0.9, 0.999, 1e-08, 0.01, 10


def _dot(a, b):
    return jnp.dot(a, b, preferred_element_type=F32)


def _dot_nt(a, b):
    return lax.dot_general(a, b, (((1,), (1,)), ((), ())), preferred_element_type=F32)


def _dot_tn(a, b):
    return lax.dot_general(a, b, (((0,), (0,)), ((), ())), preferred_element_type=F32)


def _params(n_grid):
    return pltpu.CompilerParams(dimension_semantics=("arbitrary",) * n_grid, vmem_limit_bytes=VMEM_LIMIT)


def _rms_fwd(x, g):
    r = lax.rsqrt(jnp.mean(x * x, axis=-1, keepdims=True) + EPS)
    return (x * r) * g, r


def _rms_bwd(x, g, r, dy, width=None):
    n = x.shape[-1] if width is None else width
    u = dy * g
    dx = r * u - x * (r * r * r) * (jnp.sum(x * u, axis=-1, keepdims=True) / n)
    return dx, dy * (x * r)


def _iota(shape, axis):
    return lax.broadcasted_iota(jnp.int32, shape, axis)


def _inproj_fwd(x, g, wpt):
    T = x.shape[0]
    tm = 256

    def body(x_ref, g_ref, w_ref, h32_ref, h16_ref, xn_ref):
        xn, _ = _rms_fwd(x_ref[...], g_ref[...])
        xn = xn.astype(BF)
        xn_ref[...] = xn
        h = _dot_nt(xn, w_ref[...])
        h32_ref[...] = h
        h16_ref[...] = h.astype(BF)

    return pl.pallas_call(
        body, name="inproj_fwd", grid=(T // tm,),
        in_specs=[pl.BlockSpec((tm, D_MODEL), lambda n: (n, 0)),
                  pl.BlockSpec((1, D_MODEL), lambda n: (0, 0)),
                  pl.BlockSpec((NP, D_MODEL), lambda n: (0, 0))],
        out_specs=[pl.BlockSpec((tm, NP), lambda n: (n, 0)),
                   pl.BlockSpec((tm, NP), lambda n: (n, 0)),
                   pl.BlockSpec((tm, D_MODEL), lambda n: (n, 0))],
        out_shape=[jax.ShapeDtypeStruct((T, NP), F32), jax.ShapeDtypeStruct((T, NP), BF),
                   jax.ShapeDtypeStruct((T, D_MODEL), BF)],
        compiler_params=_params(1))(x, g, wpt)


def _inproj_bwd(parts, wpt, x, xn, g, dxo):
    T = x.shape[0]
    tm = 256
    np_ = len(parts)
    chunk = NP // 3
    assert sum(p.shape[1] for p in parts) == C_END and chunk % 128 == 0

    def body(*refs):
        part_refs = refs[:np_]
        w_ref, x_ref, xn_ref, g_ref, dxo_ref, dw_ref, dx_ref, dg_ref = refs[np_:]
        n = pl.program_id(0)

        @pl.when(n == 0)
        def _():
            dw_ref[...] = jnp.zeros_like(dw_ref)
            dg_ref[...] = jnp.zeros_like(dg_ref)

        dh = jnp.concatenate([r[...].astype(BF) for r in part_refs] + [jnp.zeros((tm, NP - C_END), BF)], axis=1)
        xnv = xn_ref[...]
        for cb in range(3):
            cs = slice(cb * chunk, (cb + 1) * chunk)
            dw_ref[:, cs] += _dot_tn(xnv, dh[:, cs])
        dxn = _dot(dh, w_ref[...])
        xv = x_ref[...]
        _, r = _rms_fwd(xv, g_ref[...])
        dx, dgt = _rms_bwd(xv, g_ref[...], r, dxn)
        dx_ref[...] = dxo_ref[...] + dx
        dg_ref[...] += jnp.sum(dgt, axis=0, keepdims=True)

    once = pl.Buffered(1)
    return pl.pallas_call(
        body, name="inproj_bwd", grid=(T // tm,),
        in_specs=[pl.BlockSpec((tm, p.shape[1]), lambda n: (n, 0)) for p in parts]
        + [pl.BlockSpec((NP, D_MODEL), lambda n: (0, 0), pipeline_mode=once),
           pl.BlockSpec((tm, D_MODEL), lambda n: (n, 0)),
           pl.BlockSpec((tm, D_MODEL), lambda n: (n, 0)),
           pl.BlockSpec((1, D_MODEL), lambda n: (0, 0)),
           pl.BlockSpec((tm, D_MODEL), lambda n: (n, 0))],
        out_specs=[pl.BlockSpec((D_MODEL, NP), lambda n: (0, 0), pipeline_mode=once),
                   pl.BlockSpec((tm, D_MODEL), lambda n: (n, 0)),
                   pl.BlockSpec((1, D_MODEL), lambda n: (0, 0))],
        out_shape=[jax.ShapeDtypeStruct((D_MODEL, NP), F32), jax.ShapeDtypeStruct((T, D_MODEL), F32),
                   jax.ShapeDtypeStruct((1, D_MODEL), F32)],
        compiler_params=_params(1))(*parts, wpt, x, xn, g, dxo)


SWA_BLK = 128
SWA_TQ = 1024


def _bdot_nt(a, b):
    return lax.dot_general(a, b, (((2,), (2,)), ((0,), (0,))), preferred_element_type=F32)


def _bdot(a, b):
    return lax.dot_general(a, b, (((2,), (1,)), ((0,), (0,))), preferred_element_type=F32)


def _bdot_tn(a, b):
    return lax.dot_general(a, b, (((1,), (1,)), ((0,), (0,))), preferred_element_type=F32)


def _swa_probs(q, kc, kp, sink, mask_c, mask_p):
    sc = jnp.where(mask_c, _bdot_nt(q, kc) * SB_SCALE, NEG)
    sp = jnp.where(mask_p, _bdot_nt(q, kp) * SB_SCALE, NEG)
    m = jnp.maximum(jnp.maximum(jnp.max(sc, axis=-1, keepdims=True), jnp.max(sp, axis=-1, keepdims=True)), sink)
    pc = jnp.exp(sc - m)
    pp = jnp.exp(sp - m)
    ps = jnp.exp(sink - m)
    inv = 1.0 / (jnp.sum(pc, axis=-1, keepdims=True) + jnp.sum(pp, axis=-1, keepdims=True) + ps)
    return pc * inv, pp * inv, ps * inv


def _swa_masks(n, nb):
    blk = _iota((nb, SWA_BLK, SWA_BLK), 0)
    row = _iota((nb, SWA_BLK, SWA_BLK), 1)
    col = _iota((nb, SWA_BLK, SWA_BLK), 2)
    return col <= row, jnp.logical_and(col > row, jnp.logical_or(blk > 0, n > 0))


def _swa_specs(tq):
    halo = tq // SWA_BLK
    return [pl.BlockSpec(memory_space=pltpu.SMEM),
            pl.BlockSpec((tq, 256), lambda n: (n, C_AQ // 256)),
            pl.BlockSpec((tq, 128), lambda n: (n, C_AK // 128)),
            pl.BlockSpec((SWA_BLK, 128), lambda n: (jnp.maximum(n * halo - 1, 0), C_AK // 128)),
            pl.BlockSpec((tq, 128), lambda n: (n, C_AV // 128)),
            pl.BlockSpec((SWA_BLK, 128), lambda n: (jnp.maximum(n * halo - 1, 0), C_AV // 128))]


def _swa_blocked(cur_ref, prev_ref, gs, nb):
    cur = cur_ref[:, gs].reshape(nb, SWA_BLK, 64)
    prev = jnp.concatenate([prev_ref[:, gs].reshape(1, SWA_BLK, 64), cur[:nb - 1]], axis=0) if nb > 1 \
        else prev_ref[:, gs].reshape(1, SWA_BLK, 64)
    return cur, prev


def _swa_fwd(h16, sinks):
    T = h16.shape[0]
    tq = SWA_TQ if T % SWA_TQ == 0 else SWA_BLK
    nb = tq // SWA_BLK

    def body(s_ref, q_ref, kc_ref, kp_ref, vc_ref, vp_ref, o_ref):
        n = pl.program_id(0)
        mask_c, mask_p = _swa_masks(n, nb)
        for h in range(4):
            hs = slice(h * 64, (h + 1) * 64)
            gs = slice(h // 2 * 64, (h // 2 + 1) * 64)
            kc, kp = _swa_blocked(kc_ref, kp_ref, gs, nb)
            vc, vp = _swa_blocked(vc_ref, vp_ref, gs, nb)
            pc, pp, _ = _swa_probs(q_ref[:, hs].reshape(nb, SWA_BLK, 64), kc, kp, s_ref[h], mask_c, mask_p)
            o_ref[:, hs] = (_bdot(pc.astype(BF), vc) + _bdot(pp.astype(BF), vp)).reshape(tq, 64)

    return pl.pallas_call(
        body, name="swa_fwd", grid=(T // tq,), in_specs=_swa_specs(tq),
        out_specs=pl.BlockSpec((tq, 256), lambda n: (n, 0)),
        out_shape=jax.ShapeDtypeStruct((T, 256), F32),
        compiler_params=_params(1))(sinks, h16, h16, h16, h16, h16)


def _swa_bwd(h16, sinks, dya):
    T = h16.shape[0]
    tq = SWA_TQ if T % SWA_TQ == 0 else SWA_BLK
    nb = tq // SWA_BLK

    def body(s_ref, q_ref, kc_ref, kp_ref, vc_ref, vp_ref, do_ref, dq_ref, dk_ref, dv_ref, ds_ref):
        n = pl.program_id(0)

        @pl.when(n == 0)
        def _():
            dk_ref[...] = jnp.zeros_like(dk_ref)
            dv_ref[...] = jnp.zeros_like(dv_ref)
            ds_ref[...] = jnp.zeros_like(ds_ref)

        mask_c, mask_p = _swa_masks(n, nb)
        rows = pl.ds(pl.multiple_of(n * tq, tq), tq)
        before = pl.ds(pl.multiple_of(jnp.maximum(n * nb - 1, 0) * SWA_BLK, SWA_BLK), SWA_BLK)
        lane = _iota((8, 128), 1)
        row8 = _iota((8, 128), 0)

        def to_keys(own, prev):
            if nb == 1:
                return own
            return own + jnp.concatenate([prev[1:], jnp.zeros((1, SWA_BLK, 64), F32)], axis=0)

        for h in range(4):
            hs = slice(h * 64, (h + 1) * 64)
            gs = slice(h // 2 * 64, (h // 2 + 1) * 64)
            q = q_ref[:, hs].reshape(nb, SWA_BLK, 64)
            kc, kp = _swa_blocked(kc_ref, kp_ref, gs, nb)
            vc, vp = _swa_blocked(vc_ref, vp_ref, gs, nb)
            pc, pp, ps = _swa_probs(q, kc, kp, s_ref[h], mask_c, mask_p)
            pcb, ppb = pc.astype(BF), pp.astype(BF)
            do = do_ref[:, hs].reshape(nb, SWA_BLK, 64)
            dob = do.astype(BF)
            o = _bdot(pcb, vc) + _bdot(ppb, vp)
            dd = jnp.sum(do * o, axis=-1, keepdims=True)
            dsc = (pc * (_bdot_nt(dob, vc) - dd) * SB_SCALE).astype(BF)
            dsp = (pp * (_bdot_nt(dob, vp) - dd) * SB_SCALE).astype(BF)
            dq_ref[:, hs] = (_bdot(dsc, kc) + _bdot(dsp, kp)).reshape(tq, 64).astype(BF)
            dkp, dvp = _bdot_tn(dsp, q), _bdot_tn(ppb, dob)
            dk_ref[rows, gs] += to_keys(_bdot_tn(dsc, q), dkp).reshape(tq, 64)
            dv_ref[rows, gs] += to_keys(_bdot_tn(pcb, dob), dvp).reshape(tq, 64)
            dk_ref[before, gs] += dkp[0]
            dv_ref[before, gs] += dvp[0]
            ds_ref[...] += jnp.where(jnp.logical_and(lane == h, row8 == 0), -jnp.sum(ps * dd), 0.0)

    return pl.pallas_call(
        body, name="swa_bwd", grid=(T // tq,),
        in_specs=_swa_specs(tq) + [pl.BlockSpec((tq, 256), lambda n: (n, 0))],
        out_specs=[pl.BlockSpec((tq, 256), lambda n: (n, 0)),
                   pl.BlockSpec((T, 128), lambda n: (0, 0)),
                   pl.BlockSpec((T, 128), lambda n: (0, 0)),
                   pl.BlockSpec((8, 128), lambda n: (0, 0))],
        out_shape=[jax.ShapeDtypeStruct((T, 256), BF), jax.ShapeDtypeStruct((T, 128), F32),
                   jax.ShapeDtypeStruct((T, 128), F32), jax.ShapeDtypeStruct((8, 128), F32)],
        compiler_params=_params(1))(sinks, h16, h16, h16, h16, h16, dya)


def _conv_u(bc_ref, bx_ref, bch_ref, bxh_ref, n, tm):
    u = bc_ref[...] * bx_ref[...]
    uh = bch_ref[...] * bxh_ref[...] * (n > 0).astype(F32)
    rowi = _iota((tm, 256), 0)
    u1 = jnp.where(rowi == 0, uh[7:8, :], pltpu.roll(u, 1, axis=0))
    u2 = jnp.where(rowi == 0, uh[6:7, :], jnp.where(rowi == 1, uh[7:8, :], pltpu.roll(u, 2, axis=0)))
    return u, u1, u2


def _conv_fwd(h32, cw, cb):
    T = h32.shape[0]
    tm = 512 if T % 512 == 0 else T
    hb = tm // 8

    def body(bb_ref, bc_ref, bx_ref, bch_ref, bxh_ref, w_ref, b_ref, o_ref):
        n = pl.program_id(0)
        u, u1, u2 = _conv_u(bc_ref, bx_ref, bch_ref, bxh_ref, n, tm)
        y = w_ref[0:1, :] * u2 + w_ref[1:2, :] * u1 + w_ref[2:3, :] * u + b_ref[...]
        o_ref[...] = bb_ref[...] * y

    halo = lambda c: pl.BlockSpec((8, 256), lambda n: (jnp.maximum(n * hb - 1, 0), c // 256))
    return pl.pallas_call(
        body, name="conv_fwd", grid=(T // tm,),
        in_specs=[pl.BlockSpec((tm, 256), lambda n: (n, C_BB // 256)),
                  pl.BlockSpec((tm, 256), lambda n: (n, C_BC // 256)),
                  pl.BlockSpec((tm, 256), lambda n: (n, C_BX // 256)),
                  halo(C_BC), halo(C_BX),
                  pl.BlockSpec((8, 256), lambda n: (0, 0)),
                  pl.BlockSpec((1, 256), lambda n: (0, 0))],
        out_specs=pl.BlockSpec((tm, 256), lambda n: (n, 0)),
        out_shape=jax.ShapeDtypeStruct((T, 256), F32),
        compiler_params=_params(1))(h32, h32, h32, h32, h32, cw, cb)


def _conv_bwd(h32, cw, cb, dyb):
    T = h32.shape[0]
    tm = 512 if T % 512 == 0 else T
    hb = tm // 8
    nt = T // tm

    def body(bb_ref, bc_ref, bx_ref, bch_ref, bxh_ref, bbn_ref, dy_ref, dyn_ref, w_ref, b_ref,
             dbb_ref, dbc_ref, dbx_ref, dw_ref):
        n = pl.program_id(0)

        @pl.when(n == 0)
        def _():
            dw_ref[...] = jnp.zeros_like(dw_ref)

        u, u1, u2 = _conv_u(bc_ref, bx_ref, bch_ref, bxh_ref, n, tm)
        w0, w1, w2 = w_ref[0:1, :], w_ref[1:2, :], w_ref[2:3, :]
        y = w0 * u2 + w1 * u1 + w2 * u + b_ref[...]
        dyb_ = dy_ref[...]
        dbb_ref[...] = (dyb_ * y).astype(BF)
        dy = dyb_ * bb_ref[...]
        dyn = dyn_ref[...] * bbn_ref[...] * (n < nt - 1).astype(F32)
        rowi = _iota((tm, 256), 0)
        dy1 = jnp.where(rowi == tm - 1, dyn[0:1, :], pltpu.roll(dy, tm - 1, axis=0))
        dy2 = jnp.where(rowi == tm - 2, dyn[0:1, :],
                        jnp.where(rowi == tm - 1, dyn[1:2, :], pltpu.roll(dy, tm - 2, axis=0)))
        du = w2 * dy + w1 * dy1 + w0 * dy2
        dbc_ref[...] = (du * bx_ref[...]).astype(BF)
        dbx_ref[...] = (du * bc_ref[...]).astype(BF)
        dw_ref[0:1, :] += jnp.sum(dy * u2, axis=0, keepdims=True)
        dw_ref[1:2, :] += jnp.sum(dy * u1, axis=0, keepdims=True)
        dw_ref[2:3, :] += jnp.sum(dy * u, axis=0, keepdims=True)
        dw_ref[3:4, :] += jnp.sum(dy, axis=0, keepdims=True)

    halo = lambda c: pl.BlockSpec((8, 256), lambda n: (jnp.maximum(n * hb - 1, 0), c // 256))
    nxt = lambda c: pl.BlockSpec((8, 256), lambda n: (jnp.minimum((n + 1) * hb, T // 8 - 1), c // 256))
    cur = lambda c: pl.BlockSpec((tm, 256), lambda n: (n, c // 256))
    return pl.pallas_call(
        body, name="conv_bwd", grid=(nt,),
        in_specs=[cur(C_BB), cur(C_BC), cur(C_BX), halo(C_BC), halo(C_BX), nxt(C_BB),
                  cur(0), nxt(0),
                  pl.BlockSpec((8, 256), lambda n: (0, 0)),
                  pl.BlockSpec((1, 256), lambda n: (0, 0))],
        out_specs=[cur(0), cur(0), cur(0), pl.BlockSpec((8, 256), lambda n: (0, 0))],
        out_shape=[jax.ShapeDtypeStruct((T, 256), BF)] * 3 + [jax.ShapeDtypeStruct((8, 256), F32)],
        compiler_params=_params(1))(h32, h32, h32, h32, h32, h32, dyb, dyb, cw, cb)


def _cprep_specs(tm):
    return [pl.BlockSpec((tm, 256), lambda n: (n, C_CQ // 256)),
            pl.BlockSpec((tm, 128), lambda n: (n, C_CKV // 128)),
            pl.BlockSpec((tm, 128), lambda n: (n, C_CKR // 128)),
            pl.BlockSpec((tm, 128), lambda n: (n, C_CKRS // 128)),
            pl.BlockSpec((1, 256), lambda n: (0, 0)),
            pl.BlockSpec((1, 128), lambda n: (0, 0)),
            pl.BlockSpec((tm, 128), lambda n: (n, 0)),
            pl.BlockSpec((tm, 128), lambda n: (n, 0))]


def _cprep_fwd(h32, gq, gkv, wuq2, wkv2, cosk, sin):
    T = h32.shape[0]
    tm = 512 if T % 512 == 0 else T

    def body(cq_ref, ckv_ref, ckr_ref, ckrs_ref, gq_ref, gkv_ref, cos_ref, sin_ref, wuq_ref, wkv_ref,
             q_ref, k_ref, v_ref):
        cosk_, sin_ = cos_ref[...], sin_ref[...]
        cosq = cosk_ + (_iota((tm, 128), 1) < 64).astype(F32)
        cqn, _ = _rms_fwd(cq_ref[...], gq_ref[...])
        q2 = _dot(cqn.astype(BF), wuq_ref[...])
        ckvn, _ = _rms_fwd(ckv_ref[...], gkv_ref[...])
        kv2 = _dot(ckvn.astype(BF), wkv_ref[...])
        kr = ckr_ref[...] * cosk_ + ckrs_ref[...] * sin_
        for h in range(4):
            hs = slice(h * 128, (h + 1) * 128)
            q_ref[:, hs] = ((q2[:, hs] * cosq + q2[:, 512 + h * 128:512 + (h + 1) * 128] * sin_) * MLA_SCALE).astype(BF)
            k_ref[:, hs] = (kv2[:, hs] + kr).astype(BF)
        ones = _iota((tm, 512), 1) % 128 == 64
        v_ref[...] = jnp.where(ones, 1.0, kv2[:, 512:]).astype(BF)

    return pl.pallas_call(
        body, name="cprep_fwd", grid=(T // tm,),
        in_specs=_cprep_specs(tm) + [pl.BlockSpec((256, 1024), lambda n: (0, 0)),
                                     pl.BlockSpec((128, 1024), lambda n: (0, 0))],
        out_specs=[pl.BlockSpec((tm, 512), lambda n: (n, 0))] * 3,
        out_shape=[jax.ShapeDtypeStruct((T, 512), BF)] * 3,
        compiler_params=_params(1))(h32, h32, h32, h32, gq, gkv, cosk, sin, wuq2, wkv2)


def _cprep_bwd(h32, gq, gkv, wuq2t, wkv2t, cosk, sin, dq, dk, dv):
    T = h32.shape[0]
    tm = 512 if T % 512 == 0 else T

    def body(cq_ref, ckv_ref, ckr_ref, ckrs_ref, gq_ref, gkv_ref, cos_ref, sin_ref, wuq_ref, wkv_ref,
             dq_ref, dk_ref, dv_ref,
             dcq_ref, dckv_ref, dckr_ref, dckrs_ref, dwuq_ref, dwkv_ref, dgq_ref, dgkv_ref):
        n = pl.program_id(0)

        @pl.when(n == 0)
        def _():
            dwuq_ref[...] = jnp.zeros_like(dwuq_ref)
            dwkv_ref[...] = jnp.zeros_like(dwkv_ref)
            dgq_ref[...] = jnp.zeros_like(dgq_ref)
            dgkv_ref[...] = jnp.zeros_like(dgkv_ref)

        cosk_, sin_ = cos_ref[...], sin_ref[...]
        cosq = cosk_ + (_iota((tm, 128), 1) < 64).astype(F32)
        dkr = jnp.zeros((tm, 128), F32)
        plain, swapped = [], []
        for h in range(4):
            hs = slice(h * 128, (h + 1) * 128)
            dqh = dq_ref[:, hs] * MLA_SCALE
            plain.append((dqh * cosq).astype(BF))
            swapped.append((dqh * sin_).astype(BF))
            dkr = dkr + dk_ref[:, hs]
        dq2 = jnp.concatenate(plain + swapped, axis=1)
        dkv2 = jnp.concatenate([dk_ref[...].astype(BF), dv_ref[...].astype(BF)], axis=1)
        dckr_ref[...] = (dkr * cosk_).astype(BF)
        dckrs_ref[...] = (dkr * sin_).astype(BF)

        cq, gq_ = cq_ref[...], gq_ref[...]
        cqn, rq = _rms_fwd(cq, gq_)
        dwuq_ref[...] += _dot_tn(cqn.astype(BF), dq2)
        dcq, dgt = _rms_bwd(cq, gq_, rq, _dot(dq2, wuq_ref[...]))
        dcq_ref[...] = dcq.astype(BF)
        dgq_ref[...] += jnp.sum(dgt, axis=0, keepdims=True)

        ckv, gkv_ = ckv_ref[...], gkv_ref[...]
        ckvn, rkv = _rms_fwd(ckv, gkv_)
        dwkv_ref[...] += _dot_tn(ckvn.astype(BF), dkv2)
        dckv, dgt2 = _rms_bwd(ckv, gkv_, rkv, _dot(dkv2, wkv_ref[...]))
        dckv_ref[...] = dckv.astype(BF)
        dgkv_ref[...] += jnp.sum(dgt2, axis=0, keepdims=True)

    row = lambda w: pl.BlockSpec((tm, w), lambda n: (n, 0))
    return pl.pallas_call(
        body, name="cprep_bwd", grid=(T // tm,),
        in_specs=_cprep_specs(tm) + [pl.BlockSpec((1024, 256), lambda n: (0, 0)),
                                     pl.BlockSpec((1024, 128), lambda n: (0, 0)),
                                     row(512), row(512), row(512)],
        out_specs=[row(256), row(128), row(128), row(128),
                   pl.BlockSpec((256, 1024), lambda n: (0, 0)), pl.BlockSpec((128, 1024), lambda n: (0, 0)),
                   pl.BlockSpec((1, 256), lambda n: (0, 0)), pl.BlockSpec((1, 128), lambda n: (0, 0))],
        out_shape=[jax.ShapeDtypeStruct((T, 256), BF), jax.ShapeDtypeStruct((T, 128), BF),
                   jax.ShapeDtypeStruct((T, 128), BF), jax.ShapeDtypeStruct((T, 128), BF),
                   jax.ShapeDtypeStruct((256, 1024), F32), jax.ShapeDtypeStruct((128, 1024), F32),
                   jax.ShapeDtypeStruct((1, 256), F32), jax.ShapeDtypeStruct((1, 128), F32)],
        compiler_params=_params(1))(h32, h32, h32, h32, gq, gkv, cosk, sin, wuq2t, wkv2t, dq, dk, dv)


MLA_TILE = 512
MLA_HEADS_PER_STEP = 4


def _causal_mask(t):
    return _iota((t, t), 1) <= _iota((t, t), 0)


def _mla_fwd(q, k, v):
    T = q.shape[0]
    tq = MLA_TILE

    def body(q_ref, k_ref, v_ref, o_ref, lse_ref):
        i = pl.program_id(1)
        mask = _causal_mask(tq)
        heads = [slice(128 * h, 128 * h + 128) for h in range(MLA_HEADS_PER_STEP)]
        qs = [q_ref[:, hs] for hs in heads]

        def step(j, carry, masked):
            rows = pl.ds(pl.multiple_of(j * tq, tq), tq)
            out = []
            for hh, hs in enumerate(heads):
                m, acc = carry[hh]
                s = _dot_nt(qs[hh], k_ref[rows, hs])
                if masked:
                    s = jnp.where(mask, s, NEG)
                m_new = jnp.maximum(m, jnp.max(s, axis=-1, keepdims=True))
                p = jnp.exp((s - m_new).astype(BF))
                acc = jnp.exp(m - m_new) * acc + _dot(p, v_ref[rows, hs])
                out.append((m_new, acc))
            return tuple(out)

        init = ((jnp.full((tq, 1), NEG, F32), jnp.zeros((tq, 128), F32)),) * len(heads)
        carry = lax.fori_loop(0, i // 2, lambda t, c: step(2 * t + 1, step(2 * t, c, False), False), init)
        carry = lax.cond(i % 2 == 1, lambda c: step(i - 1, c, False), lambda c: c, carry)
        carry = step(i, carry, True)
        for hh, hs in enumerate(heads):
            m, acc = carry[hh]
            l = acc[:, 64:65]
            o_ref[:, hs] = acc * (1.0 / l)
            lse_ref[:, hs] = jnp.broadcast_to(m + jnp.log(l), (tq, 128))

    width = 128 * MLA_HEADS_PER_STEP
    blk = pl.BlockSpec((tq, width), lambda h, i: (i, h))
    full = pl.BlockSpec((T, width), lambda h, i: (0, h))
    return pl.pallas_call(
        body, name="mla_fwd", grid=(4 // MLA_HEADS_PER_STEP, T // tq), in_specs=[blk, full, full],
        out_specs=[blk, blk],
        out_shape=[jax.ShapeDtypeStruct((T, 512), F32), jax.ShapeDtypeStruct((T, 512), F32)],
        compiler_params=_params(2))(q, k, v)


def _mla_bwd(q, k, v, o, lse, do):
    T = q.shape[0]
    tq = MLA_TILE

    def body(q_ref, k_ref, v_ref, o_ref, lse_ref, do_ref, dq_ref, dk_ref, dv_ref):
        i = pl.program_id(1)

        @pl.when(i == 0)
        def _():
            dk_ref[...] = jnp.zeros_like(dk_ref)
            dv_ref[...] = jnp.zeros_like(dv_ref)

        heads = [slice(0, 128), slice(128, 256)]
        mask = _causal_mask(tq)
        qs, dobs, dds, lses = [], [], [], []
        for hs in heads:
            do = do_ref[:, hs]
            qs.append(q_ref[:, hs])
            dobs.append(do.astype(BF))
            dds.append(jnp.sum(do * o_ref[:, hs], axis=-1, keepdims=True))
            lses.append(lse_ref[:, hs.start:hs.start + 1])

        def step(j, dqs, masked):
            rows = pl.ds(pl.multiple_of(j * tq, tq), tq)
            out = []
            for hh, hs in enumerate(heads):
                kj, vj = k_ref[rows, hs], v_ref[rows, hs]
                s = _dot_nt(qs[hh], kj)
                if masked:
                    s = jnp.where(mask, s, NEG)
                p = jnp.exp(s - lses[hh])
                ds = (p * (_dot_nt(dobs[hh], vj) - dds[hh])).astype(BF)
                dk_ref[rows, hs] += _dot_tn(ds, qs[hh])
                dv_ref[rows, hs] += _dot_tn(p.astype(BF), dobs[hh])
                out.append(dqs[hh] + _dot(ds, kj))
            return tuple(out)

        dqs = lax.fori_loop(0, i // 2, lambda t, c: step(2 * t + 1, step(2 * t, c, False), False),
                            (jnp.zeros((tq, 128), F32),) * 2)
        dqs = lax.cond(i % 2 == 1, lambda c: step(i - 1, c, False), lambda c: c, dqs)
        dqs = step(i, dqs, True)
        for hh, hs in enumerate(heads):
            dq_ref[:, hs] = dqs[hh]

    blk = pl.BlockSpec((tq, 256), lambda h, i: (i, h))
    full = pl.BlockSpec((T, 256), lambda h, i: (0, h), pipeline_mode=pl.Buffered(1))
    return pl.pallas_call(
        body, name="mla_bwd", grid=(2, T // tq), in_specs=[blk, full, full, blk, blk, blk],
        out_specs=[blk, full, full],
        out_shape=[jax.ShapeDtypeStruct((T, 512), F32)] * 3,
        compiler_params=_params(2))(q, k, v, o, lse, do)


def _sb_tile(qk, rr, strict, masked, upper):
    z2 = qk * (SB_SCALE * LOG2E)
    l1 = jnp.log2(1.0 + jnp.exp2(-jnp.abs(z2)))
    lk = -jnp.maximum(z2, 0.0) - l1
    if masked:
        lk = jnp.where(strict, lk, 0.0)
    after = rr + _dot(lk.astype(BF), upper)
    ll = jnp.minimum(z2, 0.0) - l1
    a = jnp.exp2(ll + after)
    if masked:
        a = jnp.where(strict, a, 0.0)
    return ll, a, jnp.sum(lk, axis=-1, keepdims=True)


SB_TQ, SB_TK = 256, 256
SB_DEAD = -160.0


def _sb_walk(trips, one_step, carry):
    def alive(c):
        t, cr = c
        top = functools.reduce(jnp.maximum, [jnp.max(h[0]) for h in cr])
        return jnp.logical_and(t < trips, top > SB_DEAD)

    def body(c):
        t, cr = c
        return t + 1, one_step(t, cr)

    return lax.while_loop(alive, body, (jnp.int32(0), carry))[1]


def _sb_consts(tq, tk):
    row, col = _iota((tq, tk), 0), _iota((tq, tk), 1)
    strict = [col + d * tk < row for d in range(tq // tk)]
    r2, c2 = _iota((tk, tk), 0), _iota((tk, tk), 1)
    return strict, (r2 > c2).astype(BF), (r2 < c2).astype(BF)


def _sb_fwd(h16):
    T = h16.shape[0]
    tq, tk = SB_TQ, SB_TK
    nd = tq // tk

    def body(q0_ref, q1_ref, k0_ref, k1_ref, v0_ref, v1_ref, o_ref):
        i = pl.program_id(0)
        strict, upper, _ = _sb_consts(tq, tk)
        lane = _iota((tq, 128), 1)
        pairs = [slice(0, 128), slice(128, 256)]
        k_refs, v_refs = (k0_ref, k1_ref), (v0_ref, v1_ref)
        qms = []
        for q_ref in (q0_ref, q1_ref):
            q2 = q_ref[...]
            qms += [jnp.where(lane < 64, q2, jnp.zeros_like(q2)), jnp.where(lane >= 64, q2, jnp.zeros_like(q2))]

        def step(j, carry, d):
            rows = pl.ds(pl.multiple_of(j * tk, tk), tk)
            out = []
            for h in range(4):
                rr, acc = carry[h]
                _, a, rs = _sb_tile(_dot_nt(qms[h], k_refs[h // 2][rows, :]), rr, None if d is None else strict[d],
                                    d is not None, upper)
                out.append((rr + rs, acc + _dot(a.astype(BF), v_refs[h // 2][rows, :])))
            return tuple(out)

        carry = ((jnp.zeros((tq, 1), F32), jnp.zeros((tq, 128), F32)),) * 4
        for d in reversed(range(nd)):
            carry = step(nd * i + d, carry, d)
        carry = _sb_walk(nd * i, lambda t, c: step(nd * i - 1 - t, c, None), carry)
        for p, ps in enumerate(pairs):
            o_ref[:, ps] = jnp.where(lane < 64, carry[2 * p][1], carry[2 * p + 1][1])

    return pl.pallas_call(
        body, name="sb_fwd", grid=(T // tq,),
        in_specs=[pl.BlockSpec((tq, 128), lambda i: (i, C_DQ // 128)),
                  pl.BlockSpec((tq, 128), lambda i: (i, C_DQ // 128 + 1)),
                  pl.BlockSpec((T, 128), lambda i: (0, C_DK // 128)),
                  pl.BlockSpec((T, 128), lambda i: (0, C_DK // 128 + 1)),
                  pl.BlockSpec((T, 128), lambda i: (0, C_DV // 128)),
                  pl.BlockSpec((T, 128), lambda i: (0, C_DV // 128 + 1))],
        out_specs=pl.BlockSpec((tq, 256), lambda i: (i, 0)),
        out_shape=jax.ShapeDtypeStruct((T, 256), F32),
        compiler_params=_params(1))(h16, h16, h16, h16, h16, h16)


def _sb_bwd(h16, yd, dyd):
    T = h16.shape[0]
    tq, tk = SB_TQ, SB_TK
    nd = tq // tk

    def body(q0_ref, q1_ref, k0_ref, k1_ref, v0_ref, v1_ref, o_ref, do_ref, dq_ref, dk_ref, dv_ref):
        i = pl.program_id(0)

        @pl.when(i == 0)
        def _():
            dk_ref[...] = jnp.zeros_like(dk_ref)
            dv_ref[...] = jnp.zeros_like(dv_ref)

        strict, upper, before = _sb_consts(tq, tk)
        lane = _iota((tq, 128), 1)
        lane_k = _iota((tk, 128), 1)
        pairs = [slice(0, 128), slice(128, 256)]
        k_refs, v_refs = (k0_ref, k1_ref), (v0_ref, v1_ref)
        q2s, dob2s, qms, doms, dds = [], [], [], [], []
        for p, q_ref in enumerate((q0_ref, q1_ref)):
            q2 = q_ref[...]
            dob2 = do_ref[:, pairs[p]].astype(BF)
            doo = dob2.astype(F32) * o_ref[:, pairs[p]]
            q2s.append(q2)
            dob2s.append(dob2)
            for mine in (lane < 64, lane >= 64):
                qms.append(jnp.where(mine, q2, jnp.zeros_like(q2)))
                doms.append(jnp.where(mine, dob2, jnp.zeros_like(dob2)))
                dds.append(jnp.sum(jnp.where(mine, doo, 0.0), axis=-1, keepdims=True))

        def step(j, carry, d):
            rows = pl.ds(pl.multiple_of(j * tk, tk), tk)
            out, dks, dvs = [], [], []
            for h in range(4):
                kj, vj = k_refs[h // 2][rows, :], v_refs[h // 2][rows, :]
                rr, sg, dq = carry[h]
                ll, a, rs = _sb_tile(_dot_nt(qms[h], kj), rr, None if d is None else strict[d], d is not None, upper)
                ab = a.astype(BF)
                g = _dot_nt(doms[h], vj) * ab.astype(F32)
                gs = jnp.sum(g, axis=-1, keepdims=True)
                pre = (dds[h] - sg - gs) + _dot(g.astype(BF), before)
                dz = g - jnp.exp2(ll) * (g + pre)
                if d is not None:
                    dz = jnp.where(strict[d], dz, 0.0)
                dzb = dz.astype(BF)
                dks.append(_dot_tn(dzb, q2s[h // 2]))
                dvs.append(_dot_tn(ab, dob2s[h // 2]))
                out.append((rr + rs, sg + gs, dq + _dot(dzb, kj)))
            for p, ps in enumerate(pairs):
                dk_ref[rows, ps] += jnp.where(lane_k < 64, dks[2 * p], dks[2 * p + 1]) * SB_SCALE
                dv_ref[rows, ps] += jnp.where(lane_k < 64, dvs[2 * p], dvs[2 * p + 1])
            return tuple(out)

        zero = jnp.zeros((tq, 1), F32)
        carry = ((zero, zero, jnp.zeros((tq, 128), F32)),) * 4
        for d in reversed(range(nd)):
            carry = step(nd * i + d, carry, d)
        carry = _sb_walk(nd * i, lambda t, c: step(nd * i - 1 - t, c, None), carry)
        for p, ps in enumerate(pairs):
            dq_ref[:, ps] = jnp.where(lane < 64, carry[2 * p][2], carry[2 * p + 1][2]) * SB_SCALE

    blk = lambda c: pl.BlockSpec((tq, 128), lambda i: (i, c // 128))
    full = lambda c: pl.BlockSpec((T, 128), lambda i: (0, c // 128))
    row = pl.BlockSpec((tq, 256), lambda i: (i, 0))
    acc = pl.BlockSpec((T, 256), lambda i: (0, 0))
    return pl.pallas_call(
        body, name="sb_bwd", grid=(T // tq,),
        in_specs=[blk(C_DQ), blk(C_DQ + 128), full(C_DK), full(C_DK + 128), full(C_DV), full(C_DV + 128), row, row],
        out_specs=[row, acc, acc],
        out_shape=[jax.ShapeDtypeStruct((T, 256), F32)] * 3,
        compiler_params=_params(1))(h16, h16, h16, h16, h16, h16, yd, dyd)


def _compact_c(ycp):
    return jnp.concatenate([ycp[:, h * 128:h * 128 + 64] for h in range(4)], axis=1)


def _post_fwd(ya, yb, ycp, yd, h32, ggrp, wout, gpost, x):
    T = x.shape[0]
    tm = 256

    def body(ya_ref, yb_ref, yc_ref, yd_ref, gate_ref, gg_ref, w_ref, gp_ref, x_ref, xn_ref, ym_ref, o_ref):
        ys = [ya_ref[...], yb_ref[...], _compact_c(yc_ref[...]), yd_ref[...]]
        gate = gate_ref[...]
        sil = gate * (1.0 / (1.0 + jnp.exp(-gate)))
        parts = []
        for gi in range(4):
            ng, _ = _rms_fwd(ys[gi], gg_ref[:, gi * 256:(gi + 1) * 256])
            parts.append(ng * sil[:, gi * 256:(gi + 1) * 256])
        ym = jnp.concatenate(parts, axis=1).astype(BF)
        ym_ref[...] = ym
        o = _dot(ym, w_ref[...])
        o_ref[...] = o
        on, _ = _rms_fwd(o, gp_ref[...])
        xn_ref[...] = x_ref[...] + on

    row = lambda w: pl.BlockSpec((tm, w), lambda n: (n, 0))
    vec = pl.BlockSpec((1, 1024), lambda n: (0, 0))
    return pl.pallas_call(
        body, name="post_fwd", grid=(T // tm,),
        in_specs=[row(256), row(256), row(512), row(256), pl.BlockSpec((tm, 1024), lambda n: (n, C_GATE // 1024)),
                  vec, pl.BlockSpec((1024, 1024), lambda n: (0, 0)), vec, row(1024)],
        out_specs=[row(1024), row(1024), row(1024)],
        out_shape=[jax.ShapeDtypeStruct((T, 1024), F32), jax.ShapeDtypeStruct((T, 1024), BF),
                   jax.ShapeDtypeStruct((T, 1024), F32)],
        compiler_params=_params(1))(ya, yb, ycp, yd, h32, ggrp, wout, gpost, x)


def _post_bwd(dx, o, gpost, woutt, ya, yb, ycp, yd, h32, ggrp, ym):
    T = dx.shape[0]
    tm = 256

    def body(dx_ref, o_ref, gp_ref, w_ref, ya_ref, yb_ref, yc_ref, yd_ref, gate_ref, gg_ref, ym_ref,
             dw_ref, dya_ref, dyb_ref, dyc_ref, dyd_ref, dgate_ref, dgp_ref, dgg_ref):
        n = pl.program_id(0)

        @pl.when(n == 0)
        def _():
            dw_ref[...] = jnp.zeros_like(dw_ref)
            dgp_ref[...] = jnp.zeros_like(dgp_ref)
            dgg_ref[...] = jnp.zeros_like(dgg_ref)

        ov, gp = o_ref[...], gp_ref[...]
        _, ro = _rms_fwd(ov, gp)
        do, dgt = _rms_bwd(ov, gp, ro, dx_ref[...])
        dgp_ref[...] += jnp.sum(dgt, axis=0, keepdims=True)
        dob = do.astype(BF)
        dw_ref[...] += _dot_tn(ym_ref[...], dob)
        dym = _dot(dob, w_ref[...])
        gate = gate_ref[...]
        sg = 1.0 / (1.0 + jnp.exp(-gate))
        sil = gate * sg
        dsil = sg * (1.0 + gate * (1.0 - sg))
        ys = [ya_ref[...], yb_ref[...], _compact_c(yc_ref[...]), yd_ref[...]]
        dys = []
        for gi in range(4):
            gs = slice(gi * 256, (gi + 1) * 256)
            gg = gg_ref[:, gs]
            ng, rg = _rms_fwd(ys[gi], gg)
            dgate_ref[:, gs] = (dym[:, gs] * ng * dsil[:, gs]).astype(BF)
            dy, dgt2 = _rms_bwd(ys[gi], gg, rg, dym[:, gs] * sil[:, gs])
            dgg_ref[:, gs] += jnp.sum(dgt2, axis=0, keepdims=True)
            dys.append(dy)
        dya_ref[...] = dys[0]
        dyb_ref[...] = dys[1]
        dyd_ref[...] = dys[3]
        z64 = jnp.zeros((tm, 64), F32)
        dyc_ref[...] = jnp.concatenate(
            [piece for h in range(4) for piece in (dys[2][:, h * 64:(h + 1) * 64], z64)], axis=1)

    row = lambda w: pl.BlockSpec((tm, w), lambda n: (n, 0))
    vec = pl.BlockSpec((1, 1024), lambda n: (0, 0))
    return pl.pallas_call(
        body, name="post_bwd", grid=(T // tm,),
        in_specs=[row(1024), row(1024), vec, pl.BlockSpec((1024, 1024), lambda n: (0, 0)),
                  row(256), row(256), row(512), row(256),
                  pl.BlockSpec((tm, 1024), lambda n: (n, C_GATE // 1024)), vec, row(1024)],
        out_specs=[pl.BlockSpec((1024, 1024), lambda n: (0, 0)), row(256), row(256), row(512), row(256), row(1024),
                   vec, vec],
        out_shape=[jax.ShapeDtypeStruct((1024, 1024), F32), jax.ShapeDtypeStruct((T, 256), F32),
                   jax.ShapeDtypeStruct((T, 256), F32), jax.ShapeDtypeStruct((T, 512), F32),
                   jax.ShapeDtypeStruct((T, 256), F32), jax.ShapeDtypeStruct((T, 1024), BF),
                   jax.ShapeDtypeStruct((1, 1024), F32), jax.ShapeDtypeStruct((1, 1024), F32)],
        compiler_params=_params(1))(dx, o, gpost, woutt, ya, yb, ycp, yd, h32, ggrp, ym)


def _loss_head(y, tgt):
    T = y.shape[0]
    tm = 512 if T % 512 == 0 else T

    def body(y_ref, t_ref, s_ref, dy_ref):
        n = pl.program_id(0)

        @pl.when(n == 0)
        def _():
            s_ref[...] = jnp.zeros_like(s_ref)

        d = y_ref[...] - t_ref[...]
        s_ref[...] += jnp.sum(d * d, axis=0, keepdims=True)
        dy_ref[...] = d * (1.0 / D_MODEL)

    row = pl.BlockSpec((tm, 1024), lambda n: (n, 0))
    return pl.pallas_call(
        body, name="loss_head", grid=(T // tm,), in_specs=[row, row],
        out_specs=[pl.BlockSpec((1, 1024), lambda n: (0, 0)), row],
        out_shape=[jax.ShapeDtypeStruct((1, 1024), F32), jax.ShapeDtypeStruct((T, 1024), F32)],
        compiler_params=_params(1))(y, tgt)


def _swap_rows32(a):
    return jnp.concatenate([a[16:32], a[0:16]], axis=0)


def _pad_w_uq(w):
    z = lambda n: jnp.zeros((w.shape[0], n), w.dtype)
    a = [p for h in range(4) for p in (w[:, 96 * h:96 * h + 96], z(32))]
    b = [p for h in range(4) for p in (z(64), _swap32(w[:, 96 * h + 64:96 * h + 96]), z(32))]
    return jnp.concatenate(a + b, axis=1)


def _unpad_w_uq(d):
    out = []
    for h in range(4):
        out.append(d[:, 128 * h:128 * h + 64])
        out.append(d[:, 128 * h + 64:128 * h + 96] + _swap32(d[:, 512 + 128 * h + 64:512 + 128 * h + 96]))
    return jnp.concatenate(out, axis=1)


def _pad_w_ukv(w):
    z = jnp.zeros((w.shape[0], 64), w.dtype)
    a = [p for h in range(4) for p in (w[:, 128 * h:128 * h + 64], z)]
    b = [p for h in range(4) for p in (w[:, 128 * h + 64:128 * h + 128], z)]
    return jnp.concatenate(a + b, axis=1)


def _unpad_w_ukv(d):
    return jnp.concatenate([p for h in range(4) for p in (d[:, 128 * h:128 * h + 64],
                                                          d[:, 512 + 128 * h:512 + 128 * h + 64])], axis=1)


def _rope_tables(pos):
    freqs = 10000.0 ** (-jnp.arange(16, dtype=F32) / 16)
    ang = pos.astype(F32)[:, None] * freqs
    c, s = jnp.cos(ang), jnp.sin(ang)
    z = lambda n: jnp.zeros((pos.shape[0], n), F32)
    return (jnp.concatenate([z(64), c, c, z(32)], axis=1), jnp.concatenate([z(64), -s, s, z(32)], axis=1))


def _layer_weights(W, l):
    wuq2 = _pad_w_uq(W["mla_w_uq"][l])
    wkv2 = _pad_w_ukv(W["mla_w_ukv"][l])
    wout = W["w_out"][l]
    cw = jnp.concatenate([W["conv_w"][l].astype(F32), jnp.zeros((5, 256), F32)], axis=0)
    return dict(
        wpt=W["wpt"][l], wuq2=wuq2.astype(BF), wuq2t=wuq2.T.astype(BF),
        wkv2=wkv2.astype(BF), wkv2t=wkv2.T.astype(BF), wout=wout.astype(BF), woutt=wout.T.astype(BF),
        cw=cw, cb=W["conv_b"][l][None, :], sinks=W["attn_sinks"][l],
        gpre=W["norm_pre"][l][None, :], gq=W["mla_q_norm"][l][None, :], gkv=W["mla_kv_norm"][l][None, :],
        ggrp=W["group_norm"][l][None, :], gpost=W["norm_post"][l][None, :])


def _local_step(x, pos, W, tgt):
    cosk, sin = _rope_tables(pos)
    saved = []
    for l in range(DEPTH):
        lw = _layer_weights(W, l)
        h32, h16, xn = _inproj_fwd(x, lw["gpre"], lw["wpt"])
        ya = _swa_fwd(h16, lw["sinks"])
        yb = _conv_fwd(h32, lw["cw"], lw["cb"])
        qc, kc, vc = _cprep_fwd(h32, lw["gq"], lw["gkv"], lw["wuq2"], lw["wkv2"], cosk, sin)
        ycp, lse = _mla_fwd(qc, kc, vc)
        yd = _sb_fwd(h16)
        x_new, ym, o = _post_fwd(ya, yb, ycp, yd, h32, lw["ggrp"], lw["wout"], lw["gpost"], x)
        saved.append(dict(lw=lw, x=x, h32=h32, h16=h16, xn=xn, ya=ya, yb=yb, qc=qc, kc=kc, vc=vc, ycp=ycp,
                          lse=lse, yd=yd, ym=ym, o=o))
        x = x_new
    sq, dx = _loss_head(x, tgt)

    grads = {k: [None] * DEPTH for k in ("norm_pre", "w_in_pt", "attn_sinks", "conv_w", "conv_b", "mla_q_norm",
                                         "mla_w_uq", "mla_kv_norm", "mla_w_ukv", "group_norm", "w_out",
                                         "norm_post")}
    for l in reversed(range(DEPTH)):
        s = saved[l]
        lw = s["lw"]
        dwout, dya, dyb, dycp, dyd, dgate, dgpost, dggrp = _post_bwd(
            dx, s["o"], lw["gpost"], lw["woutt"], s["ya"], s["yb"], s["ycp"], s["yd"], s["h32"], lw["ggrp"], s["ym"])
        grads["norm_post"][l] = dgpost[0]
        grads["group_norm"][l] = dggrp[0]
        grads["w_out"][l] = dwout
        sdq, sdk, sdv = _sb_bwd(s["h16"], s["yd"], dyd)
        mdq, mdk, mdv = _mla_bwd(s["qc"], s["kc"], s["vc"], s["ycp"], s["lse"], dycp)
        dcq, dckv, dckr, dckrs, dwuq2, dwkv2, dgq, dgkv = _cprep_bwd(
            s["h32"], lw["gq"], lw["gkv"], lw["wuq2t"], lw["wkv2t"], cosk, sin, mdq, mdk, mdv)
        grads["mla_q_norm"][l] = dgq[0]
        grads["mla_kv_norm"][l] = dgkv[0]
        grads["mla_w_uq"][l] = _unpad_w_uq(dwuq2)
        grads["mla_w_ukv"][l] = _unpad_w_ukv(dwkv2)
        dbb, dbc, dbx, dcw = _conv_bwd(s["h32"], lw["cw"], lw["cb"], dyb)
        grads["conv_w"][l] = dcw[0:3]
        grads["conv_b"][l] = dcw[3]
        adq, adk, adv, dsk = _swa_bwd(s["h16"], lw["sinks"], dya)
        grads["attn_sinks"][l] = dsk[0, 0:4]
        parts = [dgate, adq, adk, adv, dbb, dbc, dbx, dcq, dckv, dckr, dckrs, sdq, sdk, sdv]
        dwp, dx, dgpre = _inproj_bwd(parts, lw["wpt"], s["x"], s["xn"], lw["gpre"], dx)
        grads["w_in_pt"][l] = dwp.T
        grads["norm_pre"][l] = dgpre[0]
    return sq, dx, grads


SMALL_SHARDED = ("conv_w", "mla_w_uq", "mla_w_ukv")
REPLICATED = ("norm_pre", "attn_sinks", "conv_b", "mla_q_norm", "mla_kv_norm", "group_norm", "norm_post")
ORDER = ("norm_pre", "w_in", "attn_sinks", "conv_w", "conv_b", "mla_q_norm", "mla_w_uq", "mla_kv_norm",
         "mla_w_ukv", "group_norm", "w_out", "norm_post")
W_IN_COLS = 436
W_IN_WIN = 440
SMALL_ROWS = 48


def _pack_small(arrs, dtype):
    flat = jnp.concatenate([a.reshape(-1).astype(dtype) for a in arrs])
    flat = jnp.concatenate([flat, jnp.zeros((SMALL_ROWS * D_MODEL - flat.shape[0],), dtype)])
    return flat.reshape(SMALL_ROWS, D_MODEL)


TAIL_ROW0 = DEPTH * W_IN_WIN


def _pack_state(ps):
    k = len(ps)
    wout = jnp.stack([p["w_out"] for p in ps]).reshape(k, DEPTH * 128, D_MODEL)
    flat = jnp.stack([jnp.concatenate([p[n].reshape(-1) for n in SMALL_SHARDED + REPLICATED]) for p in ps])
    small = jnp.pad(flat, ((0, 0), (0, SMALL_ROWS * D_MODEL - flat.shape[1]))).reshape(k, SMALL_ROWS, D_MODEL)
    return jnp.concatenate([wout, small], axis=1)


def _unpack_state(buf, p):
    k = buf.shape[0]
    out = {"w_out": buf[:, 0:DEPTH * 128].reshape(k, DEPTH, 128, D_MODEL)}
    flat = buf[:, DEPTH * 128:].reshape(k, SMALL_ROWS * D_MODEL)
    off = 0
    for n in SMALL_SHARDED + REPLICATED:
        size = int(np.prod(p[n].shape))
        out[n] = flat[:, off:off + size].reshape((k,) + p[n].shape)
        off += size
    return out


def _rows_of_w_in_t(lo, hi, padded, kr):
    segs = ((0, 1664, padded, C_AQ), (1664, 1696, kr, 0), (1696, 2464, padded, C_DQ), (2464, 3488, padded, C_GATE))
    out = []
    for s0, s1, src, base in segs:
        a, b = max(lo, s0), min(hi, s1)
        if a < b:
            out.append(src[base + a - s0:base + b - s0])
    return out


def _me():
    return lax.axis_index("x"), lax.axis_index("y"), lax.axis_index("c")


def _all_gather(block):
    R, C = block.shape

    def body(src_ref, out_ref, send_sems, recv_sems, local_sem):
        x, y, c = _me()
        me, sibling = (x, y, c), (x, y, 1 - c)
        chips = [(1 - x, y), (x, 1 - y), (1 - x, 1 - y)]

        def slot(px, py, pc):
            return out_ref.at[4 * px + 2 * py + pc]

        def copy(k, block, to, src=None):
            return pltpu.make_async_remote_copy(
                src_ref=slot(*block) if src is None else src, dst_ref=slot(*block), send_sem=send_sems.at[k],
                recv_sem=recv_sems.at[k], device_id=to, device_id_type=MESH)

        mine = pltpu.make_async_copy(src_ref, slot(*me), local_sem)
        mine.start()
        first = [copy(0, me, sibling, src=src_ref)]
        first += [copy(1 + j, me, (*chip, c), src=src_ref) for j, chip in enumerate(chips)]
        for cp in first:
            cp.start()
        passed = [copy(4 + j, (*chip, c), sibling) for j, chip in enumerate(chips)]
        for j, chip in enumerate(chips):
            copy(1 + j, (*chip, c), me).wait_recv()
            passed[j].start()
        copy(0, sibling, me).wait_recv()
        for j, chip in enumerate(chips):
            copy(4 + j, (*chip, 1 - c), me).wait_recv()
        for cp in first + passed:
            cp.wait_send()
        mine.wait()

    return pl.pallas_call(
        body, name="all_gather", out_shape=jax.ShapeDtypeStruct((N_DEV, R, C), block.dtype),
        in_specs=[pl.BlockSpec(memory_space=pl.ANY)], out_specs=pl.BlockSpec(memory_space=pl.ANY),
        scratch_shapes=[pltpu.SemaphoreType.DMA((N_DEV - 1,)), pltpu.SemaphoreType.DMA((N_DEV - 1,)),
                        pltpu.SemaphoreType.DMA])(block)


N_CHIP = 4


def _sibling_swap(blocks):
    _, R, C = blocks.shape

    def body(src_ref, out_ref, send_sems, recv_sems):
        x, y, c = _me()
        copies = [pltpu.make_async_remote_copy(
            src_ref=src_ref.at[2 * j + 1 - c], dst_ref=out_ref.at[j], send_sem=send_sems.at[j],
            recv_sem=recv_sems.at[j], device_id=(x, y, 1 - c), device_id_type=MESH) for j in range(N_CHIP)]
        for cp in copies:
            cp.start()
        for cp in copies:
            cp.wait()

    return pl.pallas_call(
        body, name="sibling_swap", out_shape=jax.ShapeDtypeStruct((N_CHIP, R, C), blocks.dtype),
        in_specs=[pl.BlockSpec(memory_space=pl.ANY)], out_specs=pl.BlockSpec(memory_space=pl.ANY),
        scratch_shapes=[pltpu.SemaphoreType.DMA((N_CHIP,)), pltpu.SemaphoreType.DMA((N_CHIP,))])(blocks)


def _pair_sum(a, b):
    n, R, C = a.shape
    tr = 592 if R % 592 == 0 else R

    def body(a_ref, b_ref, o_ref):
        o_ref[...] = (a_ref[...].astype(F32) + b_ref[...].astype(F32)).astype(BF)

    spec = pl.BlockSpec((1, tr, C), lambda j, r: (j, r, 0))
    return pl.pallas_call(body, name="pair_sum", grid=(n, R // tr), in_specs=[spec, spec], out_specs=spec,
                          out_shape=jax.ShapeDtypeStruct(a.shape, BF), compiler_params=_params(2))(a, b)


def _chip_exchange(sums):
    _, R, C = sums.shape

    def body(src_ref, out_ref, send_sems, recv_sems, local_sem):
        x, y, c = _me()
        here = 2 * x + y
        mine = pltpu.make_async_copy(src_ref.at[here], out_ref.at[here], local_sem)
        mine.start()
        copies = []
        for k in range(1, N_CHIP):
            px, py = x ^ (k >> 1), y ^ (k & 1)
            copies.append(pltpu.make_async_remote_copy(
                src_ref=src_ref.at[2 * px + py], dst_ref=out_ref.at[here], send_sem=send_sems.at[k - 1],
                recv_sem=recv_sems.at[k - 1], device_id=(px, py, c), device_id_type=MESH))
        for cp in copies:
            cp.start()
        for cp in copies:
            cp.wait()
        mine.wait()

    return pl.pallas_call(
        body, name="chip_exchange", out_shape=jax.ShapeDtypeStruct((N_CHIP, R, C), sums.dtype),
        in_specs=[pl.BlockSpec(memory_space=pl.ANY)], out_specs=pl.BlockSpec(memory_space=pl.ANY),
        scratch_shapes=[pltpu.SemaphoreType.DMA((N_CHIP - 1,)), pltpu.SemaphoreType.DMA((N_CHIP - 1,)),
                        pltpu.SemaphoreType.DMA])(sums)


def _adamw_update(g, w, m, v):
    m_ = ADAM_B1 * m + (1.0 - ADAM_B1) * g
    v_ = ADAM_B2 * v + (1.0 - ADAM_B2) * (g * g)
    m_hat = m_ / (1.0 - ADAM_B1 ** ADAM_STEP)
    v_hat = v_ / (1.0 - ADAM_B2 ** ADAM_STEP)
    return -ADAM_LR * (m_hat / (jnp.sqrt(v_hat) + ADAM_EPS) + ADAM_WD * w), m_, v_


def _adamw(parts, state):
    _, R, C = state.shape
    n_parts = parts.shape[0]
    tr = 16
    assert R % tr == 0 and TAIL_ROW0 % tr == 0

    def body(p_ref, s_ref, o_ref):
        g = p_ref[0].astype(F32)
        for k in range(1, n_parts):
            g = g + p_ref[k].astype(F32)
        o_ref[0] = g
        o_ref[1], o_ref[2], o_ref[3] = _adamw_update(g, s_ref[0], s_ref[1], s_ref[2])

    return pl.pallas_call(
        body, name="adamw", grid=(R // tr,),
        in_specs=[pl.BlockSpec((n_parts, tr, C), lambda n: (0, n + TAIL_ROW0 // tr, 0)),
                  pl.BlockSpec((3, tr, C), lambda n: (0, n, 0))],
        out_specs=pl.BlockSpec((4, tr, C), lambda n: (0, n, 0)), out_shape=jax.ShapeDtypeStruct((4, R, C), F32),
        compiler_params=_params(1))(parts, state)


def _adamw_w_in(parts, w, m, v, core):
    n_parts = parts.shape[0]
    tc = 256

    def body(core_ref, p_ref, w_ref, m_ref, v_ref, o_ref):
        g_t = p_ref[0].astype(F32)
        for k in range(1, n_parts):
            g_t = g_t + p_ref[k].astype(F32)
        g_t = jnp.concatenate([g_t, jnp.zeros((512 - W_IN_WIN, tc), F32)], axis=0).T
        g = jnp.where(core_ref[0] == 0, g_t[:, 0:W_IN_COLS], g_t[:, W_IN_WIN - W_IN_COLS:W_IN_WIN])
        o_ref[0, 0] = g
        o_ref[1, 0], o_ref[2, 0], o_ref[3, 0] = _adamw_update(g, w_ref[0], m_ref[0], v_ref[0])

    nat = pl.BlockSpec((1, tc, W_IN_COLS), lambda l, j: (l, j, 0))
    return pl.pallas_call(
        body, name="adamw_w_in", grid=(DEPTH, D_MODEL // tc),
        in_specs=[pl.BlockSpec(memory_space=pltpu.SMEM),
                  pl.BlockSpec((n_parts, W_IN_WIN, tc), lambda l, j: (0, l, j)), nat, nat, nat],
        out_specs=pl.BlockSpec((4, 1, tc, W_IN_COLS), lambda l, j: (0, l, j, 0)),
        out_shape=jax.ShapeDtypeStruct((4, DEPTH, D_MODEL, W_IN_COLS), F32),
        compiler_params=_params(2))(core, parts, w, m, v)


def kernel(x, positions, norm_pre, w_in, attn_sinks, conv_w, conv_b, mla_q_norm, mla_w_uq, mla_kv_norm, mla_w_ukv, group_norm, w_out, norm_post, loss_target, m_norm_pre, m_w_in, m_attn_sinks, m_conv_w, m_conv_b, m_mla_q_norm, m_mla_w_uq, m_mla_kv_norm, m_mla_w_ukv, m_group_norm, m_w_out, m_norm_post, v_norm_pre, v_w_in, v_attn_sinks, v_conv_w, v_conv_b, v_mla_q_norm, v_mla_w_uq, v_mla_kv_norm, v_mla_w_ukv, v_group_norm, v_w_out, v_norm_post):
    local = dict(norm_pre=norm_pre, w_in=w_in, attn_sinks=attn_sinks, conv_w=conv_w, conv_b=conv_b,
                 mla_q_norm=mla_q_norm, mla_w_uq=mla_w_uq, mla_kv_norm=mla_kv_norm, mla_w_ukv=mla_w_ukv,
                 group_norm=group_norm, w_out=w_out, norm_post=norm_post)
    mom = dict(norm_pre=m_norm_pre, w_in=m_w_in, attn_sinks=m_attn_sinks, conv_w=m_conv_w, conv_b=m_conv_b,
               mla_q_norm=m_mla_q_norm, mla_w_uq=m_mla_w_uq, mla_kv_norm=m_mla_kv_norm, mla_w_ukv=m_mla_w_ukv,
               group_norm=m_group_norm, w_out=m_w_out, norm_post=m_norm_post)
    vel = dict(norm_pre=v_norm_pre, w_in=v_w_in, attn_sinks=v_attn_sinks, conv_w=v_conv_w, conv_b=v_conv_b,
               mla_q_norm=v_mla_q_norm, mla_w_uq=v_mla_w_uq, mla_kv_norm=v_mla_kv_norm, mla_w_ukv=v_mla_w_ukv,
               group_norm=v_group_norm, w_out=v_w_out, norm_post=v_norm_post)

    c = lax.axis_index("c")

    tile = 16
    slot_rows = 464
    shift = 8 * lax.axis_index("y") + 4 * c
    wt = lax.dynamic_update_slice(jnp.zeros((DEPTH, slot_rows, D_MODEL), BF),
                                  jnp.transpose(w_in, (0, 2, 1)).astype(BF), (0, shift, 0))
    payload = jnp.concatenate([wt.reshape(DEPTH * slot_rows, D_MODEL),
                               w_out.astype(BF).reshape(DEPTH * 128, D_MODEL),
                               _pack_small([local[n] for n in SMALL_SHARDED], BF)], axis=0)
    gathered = _all_gather(payload)
    W = {n: local[n] for n in REPLICATED}

    def nat_rows(l, lo, hi):
        def piece(d, r0, r1):
            base = slot_rows * l - (W_IN_COLS * d) // tile * tile
            return gathered[d, base + r0:base + r1]

        out, run = [], None
        for r0 in range(lo, hi, tile):
            d0, d1 = r0 // W_IN_COLS, (r0 + tile - 1) // W_IN_COLS
            if d0 == d1 and run is not None and run[0] == d0:
                run = (d0, run[1], r0 + tile)
                continue
            if run is not None:
                out.append(piece(*run))
                run = None
            if d0 == d1:
                run = (d0, r0, r0 + tile)
            else:
                out.append(piece(d0, r0, r0 + tile) + piece(d1, r0, r0 + tile))
        if run is not None:
            out.append(piece(*run))
        return out

    z = lambda n: [jnp.zeros((n, D_MODEL), BF)]
    W["wpt"] = [jnp.concatenate(nat_rows(l, 2464, 3488) + nat_rows(l, 0, 1664) + z(64) + nat_rows(l, 1664, 1696)
                                + z(96) + nat_rows(l, 1680, 1696) + nat_rows(l, 1664, 1680) + z(32)
                                + nat_rows(l, 1696, 2464) + z(NP - C_END), axis=0) for l in range(DEPTH)]
    wo0 = DEPTH * slot_rows
    W["w_out"] = gathered[:, wo0:wo0 + DEPTH * 128].reshape(N_DEV, DEPTH, 128, D_MODEL).transpose(1, 0, 2, 3).reshape(
        DEPTH, D_MODEL, D_MODEL)
    flat = gathered[:, wo0 + DEPTH * 128:].reshape(N_DEV, SMALL_ROWS * D_MODEL)
    off = 0
    for n in SMALL_SHARDED:
        depth, rows, width = local[n].shape
        size = depth * rows * width
        W[n] = flat[:, off:off + size].reshape(N_DEV, depth, rows, width).transpose(1, 2, 0, 3).reshape(
            depth, rows, N_DEV * width)
        off += size

    sq, grad_x, g = _local_step(x[0], positions[0], W, loss_target[0])
    loss = lax.psum(0.5 / D_MODEL * jnp.sum(sq), ("x", "y", "c"))

    cols = []
    for n in SMALL_SHARDED:
        depth, rows, width = local[n].shape
        cols.append(jnp.stack(g[n]).reshape(depth, rows, N_DEV, width).transpose(2, 0, 1, 3).reshape(N_DEV, -1))
    rep = jnp.concatenate([a.reshape(-1) for n in REPLICATED for a in g[n]])
    cols.append(jnp.broadcast_to(rep[None], (N_DEV, rep.shape[0])))
    small = jnp.concatenate(cols, axis=1)
    small = jnp.pad(small, ((0, 0), (0, SMALL_ROWS * D_MODEL - small.shape[1]))).reshape(N_DEV, SMALL_ROWS, D_MODEL)
    krs = [p[C_CKR + 64:C_CKR + 96] + _swap_rows32(p[C_CKRS + 64:C_CKRS + 96]) for p in g["w_in_pt"]]
    pieces = []
    for d in range(N_DEV):
        lo = W_IN_COLS * d // 8 * 8
        for l in range(DEPTH):
            pieces += _rows_of_w_in_t(lo, lo + W_IN_WIN, g["w_in_pt"][l], krs[l])
        pieces += [g["w_out"][l][128 * d:128 * (d + 1)] for l in range(DEPTH)]
        pieces.append(small[d])
    blocks = jnp.concatenate(pieces, axis=0).astype(BF).reshape(N_DEV, -1, D_MODEL)
    mine = lax.dynamic_index_in_dim(blocks.reshape(N_CHIP, 2, -1, D_MODEL), c, axis=1, keepdims=False)
    received = _chip_exchange(_pair_sum(mine, _sibling_swap(blocks)))

    out = _unpack_state(_adamw(received, _pack_state([local, mom, vel])), local)
    out["w_in"] = _adamw_w_in(received, w_in, m_w_in, v_w_in, c.astype(jnp.int32).reshape(1))
    return (loss, grad_x[None], *[out[n][t] for t in range(4) for n in ORDER])
```

```python
import functools

import jax
import jax.numpy as jnp
import numpy as np
from jax import lax
from jax.experimental import pallas as pl
from jax.experimental.pallas import tpu as pltpu

F32 = jnp.float32
BF = jnp.bfloat16
MESH = pl.DeviceIdType.MESH

D_MODEL = 1024
DEPTH = 2
EPS = 1e-6
N_DEV = 8
VMEM_LIMIT = 56 * 1024 * 1024
NEG = -1e30
MLA_SCALE = 96.0 ** -0.5
SB_SCALE = 0.125
LOG2E = 1.4426950408889634

NP = 3840
C_GATE = 0
C_AQ = 1024
C_AK = 1280
C_AV = 1408
C_BB = 1536
C_BC = 1792
C_BX = 2048
C_CQ = 2304
C_CKV = 2560
C_CKR = 2688
C_CKRS = 2816
C_DQ = 2944
C_DK = 3200
C_DV = 3456
C_END = 3712

def _swap32(a):
    return jnp.concatenate([a[:, 16:32], a[:, 0:16]], axis=1)

ADAM_LR, ADAM_B1, ADAM_B2, ADAM_EPS, ADAM_WD, ADAM_STEP = 0.001, 0.9, 0.999, 1e-08, 0.01, 10


def _dot(a, b):
    return jnp.dot(a, b, preferred_element_type=F32)


def _dot_nt(a, b):
    return lax.dot_general(a, b, (((1,), (1,)), ((), ())), preferred_element_type=F32)


def _dot_tn(a, b):
    return lax.dot_general(a, b, (((0,), (0,)), ((), ())), preferred_element_type=F32)


def _params(n_grid):
    return pltpu.CompilerParams(dimension_semantics=("arbitrary",) * n_grid, vmem_limit_bytes=VMEM_LIMIT)


def _rms_fwd(x, g):
    r = lax.rsqrt(jnp.mean(x * x, axis=-1, keepdims=True) + EPS)
    return (x * r) * g, r


def _rms_bwd(x, g, r, dy, width=None):
    n = x.shape[-1] if width is None else width
    u = dy * g
    dx = r * u - x * (r * r * r) * (jnp.sum(x * u, axis=-1, keepdims=True) / n)
    return dx, dy * (x * r)


def _iota(shape, axis):
    return lax.broadcasted_iota(jnp.int32, shape, axis)


def _inproj_fwd(x, g, wpt):
    T = x.shape[0]
    tm = 256

    def body(x_ref, g_ref, w_ref, h32_ref, h16_ref, xn_ref):
        xn, _ = _rms_fwd(x_ref[...], g_ref[...])
        xn = xn.astype(BF)
        xn_ref[...] = xn
        h = _dot_nt(xn, w_ref[...])
        h32_ref[...] = h
        h16_ref[...] = h.astype(BF)

    return pl.pallas_call(
        body, name="inproj_fwd", grid=(T // tm,),
        in_specs=[pl.BlockSpec((tm, D_MODEL), lambda n: (n, 0)),
                  pl.BlockSpec((1, D_MODEL), lambda n: (0, 0)),
                  pl.BlockSpec((NP, D_MODEL), lambda n: (0, 0))],
        out_specs=[pl.BlockSpec((tm, NP), lambda n: (n, 0)),
                   pl.BlockSpec((tm, NP), lambda n: (n, 0)),
                   pl.BlockSpec((tm, D_MODEL), lambda n: (n, 0))],
        out_shape=[jax.ShapeDtypeStruct((T, NP), F32), jax.ShapeDtypeStruct((T, NP), BF),
                   jax.ShapeDtypeStruct((T, D_MODEL), BF)],
        compiler_params=_params(1))(x, g, wpt)


def _inproj_bwd(parts, wpt, x, xn, g, dxo):
    T = x.shape[0]
    tm = 256
    np_ = len(parts)
    chunk = NP // 3
    assert sum(p.shape[1] for p in parts) == C_END and chunk % 128 == 0

    def body(*refs):
        part_refs = refs[:np_]
        w_ref, x_ref, xn_ref, g_ref, dxo_ref, dw_ref, dx_ref, dg_ref = refs[np_:]
        n = pl.program_id(0)

        @pl.when(n == 0)
        def _():
            dw_ref[...] = jnp.zeros_like(dw_ref)
            dg_ref[...] = jnp.zeros_like(dg_ref)

        dh = jnp.concatenate([r[...].astype(BF) for r in part_refs] + [jnp.zeros((tm, NP - C_END), BF)], axis=1)
        xnv = xn_ref[...]
        for cb in range(3):
            cs = slice(cb * chunk, (cb + 1) * chunk)
            dw_ref[:, cs] += _dot_tn(xnv, dh[:, cs])
        dxn = _dot(dh, w_ref[...])
        xv = x_ref[...]
        _, r = _rms_fwd(xv, g_ref[...])
        dx, dgt = _rms_bwd(xv, g_ref[...], r, dxn)
        dx_ref[...] = dxo_ref[...] + dx
        dg_ref[...] += jnp.sum(dgt, axis=0, keepdims=True)

    once = pl.Buffered(1)
    return pl.pallas_call(
        body, name="inproj_bwd", grid=(T // tm,),
        in_specs=[pl.BlockSpec((tm, p.shape[1]), lambda n: (n, 0)) for p in parts]
        + [pl.BlockSpec((NP, D_MODEL), lambda n: (0, 0), pipeline_mode=once),
           pl.BlockSpec((tm, D_MODEL), lambda n: (n, 0)),
           pl.BlockSpec((tm, D_MODEL), lambda n: (n, 0)),
           pl.BlockSpec((1, D_MODEL), lambda n: (0, 0)),
           pl.BlockSpec((tm, D_MODEL), lambda n: (n, 0))],
        out_specs=[pl.BlockSpec((D_MODEL, NP), lambda n: (0, 0), pipeline_mode=once),
                   pl.BlockSpec((tm, D_MODEL), lambda n: (n, 0)),
                   pl.BlockSpec((1, D_MODEL), lambda n: (0, 0))],
        out_shape=[jax.ShapeDtypeStruct((D_MODEL, NP), F32), jax.ShapeDtypeStruct((T, D_MODEL), F32),
                   jax.ShapeDtypeStruct((1, D_MODEL), F32)],
        compiler_params=_params(1))(*parts, wpt, x, xn, g, dxo)


SWA_BLK = 128
SWA_TQ = 1024


def _bdot_nt(a, b):
    return lax.dot_general(a, b, (((2,), (2,)), ((0,), (0,))), preferred_element_type=F32)


def _bdot(a, b):
    return lax.dot_general(a, b, (((2,), (1,)), ((0,), (0,))), preferred_element_type=F32)


def _bdot_tn(a, b):
    return lax.dot_general(a, b, (((1,), (1,)), ((0,), (0,))), preferred_element_type=F32)


def _swa_probs(q, kc, kp, sink, mask_c, mask_p):
    sc = jnp.where(mask_c, _bdot_nt(q, kc) * SB_SCALE, NEG)
    sp = jnp.where(mask_p, _bdot_nt(q, kp) * SB_SCALE, NEG)
    m = jnp.maximum(jnp.maximum(jnp.max(sc, axis=-1, keepdims=True), jnp.max(sp, axis=-1, keepdims=True)), sink)
    pc = jnp.exp(sc - m)
    pp = jnp.exp(sp - m)
    ps = jnp.exp(sink - m)
    inv = 1.0 / (jnp.sum(pc, axis=-1, keepdims=True) + jnp.sum(pp, axis=-1, keepdims=True) + ps)
    return pc * inv, pp * inv, ps * inv


def _swa_masks(n, nb):
    blk = _iota((nb, SWA_BLK, SWA_BLK), 0)
    row = _iota((nb, SWA_BLK, SWA_BLK), 1)
    col = _iota((nb, SWA_BLK, SWA_BLK), 2)
    return col <= row, jnp.logical_and(col > row, jnp.logical_or(blk > 0, n > 0))


def _swa_specs(tq):
    halo = tq // SWA_BLK
    return [pl.BlockSpec(memory_space=pltpu.SMEM),
            pl.BlockSpec((tq, 256), lambda n: (n, C_AQ // 256)),
            pl.BlockSpec((tq, 128), lambda n: (n, C_AK // 128)),
            pl.BlockSpec((SWA_BLK, 128), lambda n: (jnp.maximum(n * halo - 1, 0), C_AK // 128)),
            pl.BlockSpec((tq, 128), lambda n: (n, C_AV // 128)),
            pl.BlockSpec((SWA_BLK, 128), lambda n: (jnp.maximum(n * halo - 1, 0), C_AV // 128))]


def _swa_blocked(cur_ref, prev_ref, gs, nb):
    cur = cur_ref[:, gs].reshape(nb, SWA_BLK, 64)
    prev = jnp.concatenate([prev_ref[:, gs].reshape(1, SWA_BLK, 64), cur[:nb - 1]], axis=0) if nb > 1 \
        else prev_ref[:, gs].reshape(1, SWA_BLK, 64)
    return cur, prev


def _swa_fwd(h16, sinks):
    T = h16.shape[0]
    tq = SWA_TQ if T % SWA_TQ == 0 else SWA_BLK
    nb = tq // SWA_BLK

    def body(s_ref, q_ref, kc_ref, kp_ref, vc_ref, vp_ref, o_ref):
        n = pl.program_id(0)
        mask_c, mask_p = _swa_masks(n, nb)
        for h in range(4):
            hs = slice(h * 64, (h + 1) * 64)
            gs = slice(h // 2 * 64, (h // 2 + 1) * 64)
            kc, kp = _swa_blocked(kc_ref, kp_ref, gs, nb)
            vc, vp = _swa_blocked(vc_ref, vp_ref, gs, nb)
            pc, pp, _ = _swa_probs(q_ref[:, hs].reshape(nb, SWA_BLK, 64), kc, kp, s_ref[h], mask_c, mask_p)
            o_ref[:, hs] = (_bdot(pc.astype(BF), vc) + _bdot(pp.astype(BF), vp)).reshape(tq, 64)

    return pl.pallas_call(
        body, name="swa_fwd", grid=(T // tq,), in_specs=_swa_specs(tq),
        out_specs=pl.BlockSpec((tq, 256), lambda n: (n, 0)),
        out_shape=jax.ShapeDtypeStruct((T, 256), F32),
        compiler_params=_params(1))(sinks, h16, h16, h16, h16, h16)


def _swa_bwd(h16, sinks, dya):
    T = h16.shape[0]
    tq = SWA_TQ if T % SWA_TQ == 0 else SWA_BLK
    nb = tq // SWA_BLK

    def body(s_ref, q_ref, kc_ref, kp_ref, vc_ref, vp_ref, do_ref, dq_ref, dk_ref, dv_ref, ds_ref):
        n = pl.program_id(0)

        @pl.when(n == 0)
        def _():
            dk_ref[...] = jnp.zeros_like(dk_ref)
            dv_ref[...] = jnp.zeros_like(dv_ref)
            ds_ref[...] = jnp.zeros_like(ds_ref)

        mask_c, mask_p = _swa_masks(n, nb)
        rows = pl.ds(pl.multiple_of(n * tq, tq), tq)
        before = pl.ds(pl.multiple_of(jnp.maximum(n * nb - 1, 0) * SWA_BLK, SWA_BLK), SWA_BLK)
        lane = _iota((8, 128), 1)
        row8 = _iota((8, 128), 0)

        def to_keys(own, prev):
            if nb == 1:
                return own
            return own + jnp.concatenate([prev[1:], jnp.zeros((1, SWA_BLK, 64), F32)], axis=0)

        for h in range(4):
            hs = slice(h * 64, (h + 1) * 64)
            gs = slice(h // 2 * 64, (h // 2 + 1) * 64)
            q = q_ref[:, hs].reshape(nb, SWA_BLK, 64)
            kc, kp = _swa_blocked(kc_ref, kp_ref, gs, nb)
            vc, vp = _swa_blocked(vc_ref, vp_ref, gs, nb)
            pc, pp, ps = _swa_probs(q, kc, kp, s_ref[h], mask_c, mask_p)
            pcb, ppb = pc.astype(BF), pp.astype(BF)
            do = do_ref[:, hs].reshape(nb, SWA_BLK, 64)
            dob = do.astype(BF)
            o = _bdot(pcb, vc) + _bdot(ppb, vp)
            dd = jnp.sum(do * o, axis=-1, keepdims=True)
            dsc = (pc * (_bdot_nt(dob, vc) - dd) * SB_SCALE).astype(BF)
            dsp = (pp * (_bdot_nt(dob, vp) - dd) * SB_SCALE).astype(BF)
            dq_ref[:, hs] = (_bdot(dsc, kc) + _bdot(dsp, kp)).reshape(tq, 64).astype(BF)
            dkp, dvp = _bdot_tn(dsp, q), _bdot_tn(ppb, dob)
            dk_ref[rows, gs] += to_keys(_bdot_tn(dsc, q), dkp).reshape(tq, 64)
            dv_ref[rows, gs] += to_keys(_bdot_tn(pcb, dob), dvp).reshape(tq, 64)
            dk_ref[before, gs] += dkp[0]
            dv_ref[before, gs] += dvp[0]
            ds_ref[...] += jnp.where(jnp.logical_and(lane == h, row8 == 0), -jnp.sum(ps * dd), 0.0)

    return pl.pallas_call(
        body, name="swa_bwd", grid=(T // tq,),
        in_specs=_swa_specs(tq) + [pl.BlockSpec((tq, 256), lambda n: (n, 0))],
        out_specs=[pl.BlockSpec((tq, 256), lambda n: (n, 0)),
                   pl.BlockSpec((T, 128), lambda n: (0, 0)),
                   pl.BlockSpec((T, 128), lambda n: (0, 0)),
                   pl.BlockSpec((8, 128), lambda n: (0, 0))],
        out_shape=[jax.ShapeDtypeStruct((T, 256), BF), jax.ShapeDtypeStruct((T, 128), F32),
                   jax.ShapeDtypeStruct((T, 128), F32), jax.ShapeDtypeStruct((8, 128), F32)],
        compiler_params=_params(1))(sinks, h16, h16, h16, h16, h16, dya)


def _conv_u(bc_ref, bx_ref, bch_ref, bxh_ref, n, tm):
    u = bc_ref[...] * bx_ref[...]
    uh = bch_ref[...] * bxh_ref[...] * (n > 0).astype(F32)
    rowi = _iota((tm, 256), 0)
    u1 = jnp.where(rowi == 0, uh[7:8, :], pltpu.roll(u, 1, axis=0))
    u2 = jnp.where(rowi == 0, uh[6:7, :], jnp.where(rowi == 1, uh[7:8, :], pltpu.roll(u, 2, axis=0)))
    return u, u1, u2


def _conv_fwd(h32, cw, cb):
    T = h32.shape[0]
    tm = 512 if T % 512 == 0 else T
    hb = tm // 8

    def body(bb_ref, bc_ref, bx_ref, bch_ref, bxh_ref, w_ref, b_ref, o_ref):
        n = pl.program_id(0)
        u, u1, u2 = _conv_u(bc_ref, bx_ref, bch_ref, bxh_ref, n, tm)
        y = w_ref[0:1, :] * u2 + w_ref[1:2, :] * u1 + w_ref[2:3, :] * u + b_ref[...]
        o_ref[...] = bb_ref[...] * y

    halo = lambda c: pl.BlockSpec((8, 256), lambda n: (jnp.maximum(n * hb - 1, 0), c // 256))
    return pl.pallas_call(
        body, name="conv_fwd", grid=(T // tm,),
        in_specs=[pl.BlockSpec((tm, 256), lambda n: (n, C_BB // 256)),
                  pl.BlockSpec((tm, 256), lambda n: (n, C_BC // 256)),
                  pl.BlockSpec((tm, 256), lambda n: (n, C_BX // 256)),
                  halo(C_BC), halo(C_BX),
                  pl.BlockSpec((8, 256), lambda n: (0, 0)),
                  pl.BlockSpec((1, 256), lambda n: (0, 0))],
        out_specs=pl.BlockSpec((tm, 256), lambda n: (n, 0)),
        out_shape=jax.ShapeDtypeStruct((T, 256), F32),
        compiler_params=_params(1))(h32, h32, h32, h32, h32, cw, cb)


def _conv_bwd(h32, cw, cb, dyb):
    T = h32.shape[0]
    tm = 512 if T % 512 == 0 else T
    hb = tm // 8
    nt = T // tm

    def body(bb_ref, bc_ref, bx_ref, bch_ref, bxh_ref, bbn_ref, dy_ref, dyn_ref, w_ref, b_ref,
             dbb_ref, dbc_ref, dbx_ref, dw_ref):
        n = pl.program_id(0)

        @pl.when(n == 0)
        def _():
            dw_ref[...] = jnp.zeros_like(dw_ref)

        u, u1, u2 = _conv_u(bc_ref, bx_ref, bch_ref, bxh_ref, n, tm)
        w0, w1, w2 = w_ref[0:1, :], w_ref[1:2, :], w_ref[2:3, :]
        y = w0 * u2 + w1 * u1 + w2 * u + b_ref[...]
        dyb_ = dy_ref[...]
        dbb_ref[...] = (dyb_ * y).astype(BF)
        dy = dyb_ * bb_ref[...]
        dyn = dyn_ref[...] * bbn_ref[...] * (n < nt - 1).astype(F32)
        rowi = _iota((tm, 256), 0)
        dy1 = jnp.where(rowi == tm - 1, dyn[0:1, :], pltpu.roll(dy, tm - 1, axis=0))
        dy2 = jnp.where(rowi == tm - 2, dyn[0:1, :],
                        jnp.where(rowi == tm - 1, dyn[1:2, :], pltpu.roll(dy, tm - 2, axis=0)))
        du = w2 * dy + w1 * dy1 + w0 * dy2
        dbc_ref[...] = (du * bx_ref[...]).astype(BF)
        dbx_ref[...] = (du * bc_ref[...]).astype(BF)
        dw_ref[0:1, :] += jnp.sum(dy * u2, axis=0, keepdims=True)
        dw_ref[1:2, :] += jnp.sum(dy * u1, axis=0, keepdims=True)
        dw_ref[2:3, :] += jnp.sum(dy * u, axis=0, keepdims=True)
        dw_ref[3:4, :] += jnp.sum(dy, axis=0, keepdims=True)

    halo = lambda c: pl.BlockSpec((8, 256), lambda n: (jnp.maximum(n * hb - 1, 0), c // 256))
    nxt = lambda c: pl.BlockSpec((8, 256), lambda n: (jnp.minimum((n + 1) * hb, T // 8 - 1), c // 256))
    cur = lambda c: pl.BlockSpec((tm, 256), lambda n: (n, c // 256))
    return pl.pallas_call(
        body, name="conv_bwd", grid=(nt,),
        in_specs=[cur(C_BB), cur(C_BC), cur(C_BX), halo(C_BC), halo(C_BX), nxt(C_BB),
                  cur(0), nxt(0),
                  pl.BlockSpec((8, 256), lambda n: (0, 0)),
                  pl.BlockSpec((1, 256), lambda n: (0, 0))],
        out_specs=[cur(0), cur(0), cur(0), pl.BlockSpec((8, 256), lambda n: (0, 0))],
        out_shape=[jax.ShapeDtypeStruct((T, 256), BF)] * 3 + [jax.ShapeDtypeStruct((8, 256), F32)],
        compiler_params=_params(1))(h32, h32, h32, h32, h32, h32, dyb, dyb, cw, cb)


def _cprep_specs(tm):
    return [pl.BlockSpec((tm, 256), lambda n: (n, C_CQ // 256)),
            pl.BlockSpec((tm, 128), lambda n: (n, C_CKV // 128)),
            pl.BlockSpec((tm, 128), lambda n: (n, C_CKR // 128)),
            pl.BlockSpec((tm, 128), lambda n: (n, C_CKRS // 128)),
            pl.BlockSpec((1, 256), lambda n: (0, 0)),
            pl.BlockSpec((1, 128), lambda n: (0, 0)),
            pl.BlockSpec((tm, 128), lambda n: (n, 0)),
            pl.BlockSpec((tm, 128), lambda n: (n, 0))]


def _cprep_fwd(h32, gq, gkv, wuq2, wkv2, cosk, sin):
    T = h32.shape[0]
    tm = 512 if T % 512 == 0 else T

    def body(cq_ref, ckv_ref, ckr_ref, ckrs_ref, gq_ref, gkv_ref, cos_ref, sin_ref, wuq_ref, wkv_ref,
             q_ref, k_ref, v_ref):
        cosk_, sin_ = cos_ref[...], sin_ref[...]
        cosq = cosk_ + (_iota((tm, 128), 1) < 64).astype(F32)
        cqn, _ = _rms_fwd(cq_ref[...], gq_ref[...])
        q2 = _dot(cqn.astype(BF), wuq_ref[...])
        ckvn, _ = _rms_fwd(ckv_ref[...], gkv_ref[...])
        kv2 = _dot(ckvn.astype(BF), wkv_ref[...])
        kr = ckr_ref[...] * cosk_ + ckrs_ref[...] * sin_
        for h in range(4):
            hs = slice(h * 128, (h + 1) * 128)
            q_ref[:, hs] = ((q2[:, hs] * cosq + q2[:, 512 + h * 128:512 + (h + 1) * 128] * sin_) * MLA_SCALE).astype(BF)
            k_ref[:, hs] = (kv2[:, hs] + kr).astype(BF)
        ones = _iota((tm, 512), 1) % 128 == 64
        v_ref[...] = jnp.where(ones, 1.0, kv2[:, 512:]).astype(BF)

    return pl.pallas_call(
        body, name="cprep_fwd", grid=(T // tm,),
        in_specs=_cprep_specs(tm) + [pl.BlockSpec((256, 1024), lambda n: (0, 0)),
                                     pl.BlockSpec((128, 1024), lambda n: (0, 0))],
        out_specs=[pl.BlockSpec((tm, 512), lambda n: (n, 0))] * 3,
        out_shape=[jax.ShapeDtypeStruct((T, 512), BF)] * 3,
        compiler_params=_params(1))(h32, h32, h32, h32, gq, gkv, cosk, sin, wuq2, wkv2)


def _cprep_bwd(h32, gq, gkv, wuq2t, wkv2t, cosk, sin, dq, dk, dv):
    T = h32.shape[0]
    tm = 512 if T % 512 == 0 else T

    def body(cq_ref, ckv_ref, ckr_ref, ckrs_ref, gq_ref, gkv_ref, cos_ref, sin_ref, wuq_ref, wkv_ref,
             dq_ref, dk_ref, dv_ref,
             dcq_ref, dckv_ref, dckr_ref, dckrs_ref, dwuq_ref, dwkv_ref, dgq_ref, dgkv_ref):
        n = pl.program_id(0)

        @pl.when(n == 0)
        def _():
            dwuq_ref[...] = jnp.zeros_like(dwuq_ref)
            dwkv_ref[...] = jnp.zeros_like(dwkv_ref)
            dgq_ref[...] = jnp.zeros_like(dgq_ref)
            dgkv_ref[...] = jnp.zeros_like(dgkv_ref)

        cosk_, sin_ = cos_ref[...], sin_ref[...]
        cosq = cosk_ + (_iota((tm, 128), 1) < 64).astype(F32)
        dkr = jnp.zeros((tm, 128), F32)
        plain, swapped = [], []
        for h in range(4):
            hs = slice(h * 128, (h + 1) * 128)
            dqh = dq_ref[:, hs] * MLA_SCALE
            plain.append((dqh * cosq).astype(BF))
            swapped.append((dqh * sin_).astype(BF))
            dkr = dkr + dk_ref[:, hs]
        dq2 = jnp.concatenate(plain + swapped, axis=1)
        dkv2 = jnp.concatenate([dk_ref[...].astype(BF), dv_ref[...].astype(BF)], axis=1)
        dckr_ref[...] = (dkr * cosk_).astype(BF)
        dckrs_ref[...] = (dkr * sin_).astype(BF)

        cq, gq_ = cq_ref[...], gq_ref[...]
        cqn, rq = _rms_fwd(cq, gq_)
        dwuq_ref[...] += _dot_tn(cqn.astype(BF), dq2)
        dcq, dgt = _rms_bwd(cq, gq_, rq, _dot(dq2, wuq_ref[...]))
        dcq_ref[...] = dcq.astype(BF)
        dgq_ref[...] += jnp.sum(dgt, axis=0, keepdims=True)

        ckv, gkv_ = ckv_ref[...], gkv_ref[...]
        ckvn, rkv = _rms_fwd(ckv, gkv_)
        dwkv_ref[...] += _dot_tn(ckvn.astype(BF), dkv2)
        dckv, dgt2 = _rms_bwd(ckv, gkv_, rkv, _dot(dkv2, wkv_ref[...]))
        dckv_ref[...] = dckv.astype(BF)
        dgkv_ref[...] += jnp.sum(dgt2, axis=0, keepdims=True)

    row = lambda w: pl.BlockSpec((tm, w), lambda n: (n, 0))
    return pl.pallas_call(
        body, name="cprep_bwd", grid=(T // tm,),
        in_specs=_cprep_specs(tm) + [pl.BlockSpec((1024, 256), lambda n: (0, 0)),
                                     pl.BlockSpec((1024, 128), lambda n: (0, 0)),
                                     row(512), row(512), row(512)],
        out_specs=[row(256), row(128), row(128), row(128),
                   pl.BlockSpec((256, 1024), lambda n: (0, 0)), pl.BlockSpec((128, 1024), lambda n: (0, 0)),
                   pl.BlockSpec((1, 256), lambda n: (0, 0)), pl.BlockSpec((1, 128), lambda n: (0, 0))],
        out_shape=[jax.ShapeDtypeStruct((T, 256), BF), jax.ShapeDtypeStruct((T, 128), BF),
                   jax.ShapeDtypeStruct((T, 128), BF), jax.ShapeDtypeStruct((T, 128), BF),
                   jax.ShapeDtypeStruct((256, 1024), F32), jax.ShapeDtypeStruct((128, 1024), F32),
                   jax.ShapeDtypeStruct((1, 256), F32), jax.ShapeDtypeStruct((1, 128), F32)],
        compiler_params=_params(1))(h32, h32, h32, h32, gq, gkv, cosk, sin, wuq2t, wkv2t, dq, dk, dv)


MLA_TILE = 512
MLA_HEADS_PER_STEP = 4


def _causal_mask(t):
    return _iota((t, t), 1) <= _iota((t, t), 0)


def _mla_fwd(q, k, v):
    T = q.shape[0]
    tq = MLA_TILE

    def body(q_ref, k_ref, v_ref, o_ref, lse_ref):
        i = pl.program_id(1)
        mask = _causal_mask(tq)
        heads = [slice(128 * h, 128 * h + 128) for h in range(MLA_HEADS_PER_STEP)]
        qs = [q_ref[:, hs] for hs in heads]

        def step(j, carry, masked):
            rows = pl.ds(pl.multiple_of(j * tq, tq), tq)
            out = []
            for hh, hs in enumerate(heads):
                m, acc = carry[hh]
                s = _dot_nt(qs[hh], k_ref[rows, hs])
                if masked:
                    s = jnp.where(mask, s, NEG)
                m_new = jnp.maximum(m, jnp.max(s, axis=-1, keepdims=True))
                p = jnp.exp((s - m_new).astype(BF))
                acc = jnp.exp(m - m_new) * acc + _dot(p, v_ref[rows, hs])
                out.append((m_new, acc))
            return tuple(out)

        init = ((jnp.full((tq, 1), NEG, F32), jnp.zeros((tq, 128), F32)),) * len(heads)
        carry = lax.fori_loop(0, i // 2, lambda t, c: step(2 * t + 1, step(2 * t, c, False), False), init)
        carry = lax.cond(i % 2 == 1, lambda c: step(i - 1, c, False), lambda c: c, carry)
        carry = step(i, carry, True)
        for hh, hs in enumerate(heads):
            m, acc = carry[hh]
            l = acc[:, 64:65]
            o_ref[:, hs] = acc * (1.0 / l)
            lse_ref[:, hs] = jnp.broadcast_to(m + jnp.log(l), (tq, 128))

    width = 128 * MLA_HEADS_PER_STEP
    blk = pl.BlockSpec((tq, width), lambda h, i: (i, h))
    full = pl.BlockSpec((T, width), lambda h, i: (0, h))
    return pl.pallas_call(
        body, name="mla_fwd", grid=(4 // MLA_HEADS_PER_STEP, T // tq), in_specs=[blk, full, full],
        out_specs=[blk, blk],
        out_shape=[jax.ShapeDtypeStruct((T, 512), F32), jax.ShapeDtypeStruct((T, 512), F32)],
        compiler_params=_params(2))(q, k, v)


def _mla_bwd(q, k, v, o, lse, do):
    T = q.shape[0]
    tq = MLA_TILE

    def body(q_ref, k_ref, v_ref, o_ref, lse_ref, do_ref, dq_ref, dk_ref, dv_ref):
        i = pl.program_id(1)

        @pl.when(i == 0)
        def _():
            dk_ref[...] = jnp.zeros_like(dk_ref)
            dv_ref[...] = jnp.zeros_like(dv_ref)

        heads = [slice(0, 128), slice(128, 256)]
        mask = _causal_mask(tq)
        qs, dobs, dds, lses = [], [], [], []
        for hs in heads:
            do = do_ref[:, hs]
            qs.append(q_ref[:, hs])
            dobs.append(do.astype(BF))
            dds.append(jnp.sum(do * o_ref[:, hs], axis=-1, keepdims=True))
            lses.append(lse_ref[:, hs.start:hs.start + 1])

        def step(j, dqs, masked):
            rows = pl.ds(pl.multiple_of(j * tq, tq), tq)
            out = []
            for hh, hs in enumerate(heads):
                kj, vj = k_ref[rows, hs], v_ref[rows, hs]
                s = _dot_nt(qs[hh], kj)
                if masked:
                    s = jnp.where(mask, s, NEG)
                p = jnp.exp(s - lses[hh])
                ds = (p * (_dot_nt(dobs[hh], vj) - dds[hh])).astype(BF)
                dk_ref[rows, hs] += _dot_tn(ds, qs[hh])
                dv_ref[rows, hs] += _dot_tn(p.astype(BF), dobs[hh])
                out.append(dqs[hh] + _dot(ds, kj))
            return tuple(out)

        dqs = lax.fori_loop(0, i // 2, lambda t, c: step(2 * t + 1, step(2 * t, c, False), False),
                            (jnp.zeros((tq, 128), F32),) * 2)
        dqs = lax.cond(i % 2 == 1, lambda c: step(i - 1, c, False), lambda c: c, dqs)
        dqs = step(i, dqs, True)
        for hh, hs in enumerate(heads):
            dq_ref[:, hs] = dqs[hh]

    blk = pl.BlockSpec((tq, 256), lambda h, i: (i, h))
    full = pl.BlockSpec((T, 256), lambda h, i: (0, h), pipeline_mode=pl.Buffered(1))
    return pl.pallas_call(
        body, name="mla_bwd", grid=(2, T // tq), in_specs=[blk, full, full, blk, blk, blk],
        out_specs=[blk, full, full],
        out_shape=[jax.ShapeDtypeStruct((T, 512), F32)] * 3,
        compiler_params=_params(2))(q, k, v, o, lse, do)


def _sb_tile(qk, rr, strict, masked, upper):
    z2 = qk * (SB_SCALE * LOG2E)
    l1 = jnp.log2(1.0 + jnp.exp2(-jnp.abs(z2)))
    lk = -jnp.maximum(z2, 0.0) - l1
    if masked:
        lk = jnp.where(strict, lk, 0.0)
    after = rr + _dot(lk.astype(BF), upper)
    ll = jnp.minimum(z2, 0.0) - l1
    a = jnp.exp2(ll + after)
    if masked:
        a = jnp.where(strict, a, 0.0)
    return ll, a, jnp.sum(lk, axis=-1, keepdims=True)


SB_TQ, SB_TK = 256, 256
SB_DEAD = -160.0


def _sb_walk(trips, one_step, carry):
    def alive(c):
        t, cr = c
        top = functools.reduce(jnp.maximum, [jnp.max(h[0]) for h in cr])
        return jnp.logical_and(t < trips, top > SB_DEAD)

    def body(c):
        t, cr = c
        return t + 1, one_step(t, cr)

    return lax.while_loop(alive, body, (jnp.int32(0), carry))[1]


def _sb_consts(tq, tk):
    row, col = _iota((tq, tk), 0), _iota((tq, tk), 1)
    strict = [col + d * tk < row for d in range(tq // tk)]
    r2, c2 = _iota((tk, tk), 0), _iota((tk, tk), 1)
    return strict, (r2 > c2).astype(BF), (r2 < c2).astype(BF)


def _sb_fwd(h16):
    T = h16.shape[0]
    tq, tk = SB_TQ, SB_TK
    nd = tq // tk

    def body(q0_ref, q1_ref, k0_ref, k1_ref, v0_ref, v1_ref, o_ref):
        i = pl.program_id(0)
        strict, upper, _ = _sb_consts(tq, tk)
        lane = _iota((tq, 128), 1)
        pairs = [slice(0, 128), slice(128, 256)]
        k_refs, v_refs = (k0_ref, k1_ref), (v0_ref, v1_ref)
        qms = []
        for q_ref in (q0_ref, q1_ref):
            q2 = q_ref[...]
            qms += [jnp.where(lane < 64, q2, jnp.zeros_like(q2)), jnp.where(lane >= 64, q2, jnp.zeros_like(q2))]

        def step(j, carry, d):
            rows = pl.ds(pl.multiple_of(j * tk, tk), tk)
            out = []
            for h in range(4):
                rr, acc = carry[h]
                _, a, rs = _sb_tile(_dot_nt(qms[h], k_refs[h // 2][rows, :]), rr, None if d is None else strict[d],
                                    d is not None, upper)
                out.append((rr + rs, acc + _dot(a.astype(BF), v_refs[h // 2][rows, :])))
            return tuple(out)

        carry = ((jnp.zeros((tq, 1), F32), jnp.zeros((tq, 128), F32)),) * 4
        for d in reversed(range(nd)):
            carry = step(nd * i + d, carry, d)
        carry = _sb_walk(nd * i, lambda t, c: step(nd * i - 1 - t, c, None), carry)
        for p, ps in enumerate(pairs):
            o_ref[:, ps] = jnp.where(lane < 64, carry[2 * p][1], carry[2 * p + 1][1])

    return pl.pallas_call(
        body, name="sb_fwd", grid=(T // tq,),
        in_specs=[pl.BlockSpec((tq, 128), lambda i: (i, C_DQ // 128)),
                  pl.BlockSpec((tq, 128), lambda i: (i, C_DQ // 128 + 1)),
                  pl.BlockSpec((T, 128), lambda i: (0, C_DK // 128)),
                  pl.BlockSpec((T, 128), lambda i: (0, C_DK // 128 + 1)),
                  pl.BlockSpec((T, 128), lambda i: (0, C_DV // 128)),
                  pl.BlockSpec((T, 128), lambda i: (0, C_DV // 128 + 1))],
        out_specs=pl.BlockSpec((tq, 256), lambda i: (i, 0)),
        out_shape=jax.ShapeDtypeStruct((T, 256), F32),
        compiler_params=_params(1))(h16, h16, h16, h16, h16, h16)


def _sb_bwd(h16, yd, dyd):
    T = h16.shape[0]
    tq, tk = SB_TQ, SB_TK
    nd = tq // tk

    def body(q0_ref, q1_ref, k0_ref, k1_ref, v0_ref, v1_ref, o_ref, do_ref, dq_ref, dk_ref, dv_ref):
        i = pl.program_id(0)

        @pl.when(i == 0)
        def _():
            dk_ref[...] = jnp.zeros_like(dk_ref)
            dv_ref[...] = jnp.zeros_like(dv_ref)

        strict, upper, before = _sb_consts(tq, tk)
        lane = _iota((tq, 128), 1)
        lane_k = _iota((tk, 128), 1)
        pairs = [slice(0, 128), slice(128, 256)]
        k_refs, v_refs = (k0_ref, k1_ref), (v0_ref, v1_ref)
        q2s, dob2s, qms, doms, dds = [], [], [], [], []
        for p, q_ref in enumerate((q0_ref, q1_ref)):
            q2 = q_ref[...]
            dob2 = do_ref[:, pairs[p]].astype(BF)
            doo = dob2.astype(F32) * o_ref[:, pairs[p]]
            q2s.append(q2)
            dob2s.append(dob2)
            for mine in (lane < 64, lane >= 64):
                qms.append(jnp.where(mine, q2, jnp.zeros_like(q2)))
                doms.append(jnp.where(mine, dob2, jnp.zeros_like(dob2)))
                dds.append(jnp.sum(jnp.where(mine, doo, 0.0), axis=-1, keepdims=True))

        def step(j, carry, d):
            rows = pl.ds(pl.multiple_of(j * tk, tk), tk)
            out, dks, dvs = [], [], []
            for h in range(4):
                kj, vj = k_refs[h // 2][rows, :], v_refs[h // 2][rows, :]
                rr, sg, dq = carry[h]
                ll, a, rs = _sb_tile(_dot_nt(qms[h], kj), rr, None if d is None else strict[d], d is not None, upper)
                ab = a.astype(BF)
                g = _dot_nt(doms[h], vj) * ab.astype(F32)
                gs = jnp.sum(g, axis=-1, keepdims=True)
                pre = (dds[h] - sg - gs) + _dot(g.astype(BF), before)
                dz = g - jnp.exp2(ll) * (g + pre)
                if d is not None:
                    dz = jnp.where(strict[d], dz, 0.0)
                dzb = dz.astype(BF)
                dks.append(_dot_tn(dzb, q2s[h // 2]))
                dvs.append(_dot_tn(ab, dob2s[h // 2]))
                out.append((rr + rs, sg + gs, dq + _dot(dzb, kj)))
            for p, ps in enumerate(pairs):
                dk_ref[rows, ps] += jnp.where(lane_k < 64, dks[2 * p], dks[2 * p + 1]) * SB_SCALE
                dv_ref[rows, ps] += jnp.where(lane_k < 64, dvs[2 * p], dvs[2 * p + 1])
            return tuple(out)

        zero = jnp.zeros((tq, 1), F32)
        carry = ((zero, zero, jnp.zeros((tq, 128), F32)),) * 4
        for d in reversed(range(nd)):
            carry = step(nd * i + d, carry, d)
        carry = _sb_walk(nd * i, lambda t, c: step(nd * i - 1 - t, c, None), carry)
        for p, ps in enumerate(pairs):
            dq_ref[:, ps] = jnp.where(lane < 64, carry[2 * p][2], carry[2 * p + 1][2]) * SB_SCALE

    blk = lambda c: pl.BlockSpec((tq, 128), lambda i: (i, c // 128))
    full = lambda c: pl.BlockSpec((T, 128), lambda i: (0, c // 128))
    row = pl.BlockSpec((tq, 256), lambda i: (i, 0))
    acc = pl.BlockSpec((T, 256), lambda i: (0, 0))
    return pl.pallas_call(
        body, name="sb_bwd", grid=(T // tq,),
        in_specs=[blk(C_DQ), blk(C_DQ + 128), full(C_DK), full(C_DK + 128), full(C_DV), full(C_DV + 128), row, row],
        out_specs=[row, acc, acc],
        out_shape=[jax.ShapeDtypeStruct((T, 256), F32)] * 3,
        compiler_params=_params(1))(h16, h16, h16, h16, h16, h16, yd, dyd)


def _compact_c(ycp):
    return jnp.concatenate([ycp[:, h * 128:h * 128 + 64] for h in range(4)], axis=1)


def _post_fwd(ya, yb, ycp, yd, h32, ggrp, wout, gpost, x, tgt=None):
    T = x.shape[0]
    tm = 256
    last = tgt is not None

    def body(*refs):
        ya_ref, yb_ref, yc_ref, yd_ref, gate_ref, gg_ref, w_ref, gp_ref, x_ref = refs[:9]
        if last:
            t_ref, xn_ref, ym_ref, o_ref, sq_ref = refs[9:]
        else:
            xn_ref, ym_ref, o_ref = refs[9:]
        ys = [ya_ref[...], yb_ref[...], _compact_c(yc_ref[...]), yd_ref[...]]
        gate = gate_ref[...]
        sil = gate * (1.0 / (1.0 + jnp.exp(-gate)))
        parts = []
        for gi in range(4):
            ng, _ = _rms_fwd(ys[gi], gg_ref[:, gi * 256:(gi + 1) * 256])
            parts.append(ng * sil[:, gi * 256:(gi + 1) * 256])
        ym = jnp.concatenate(parts, axis=1).astype(BF)
        ym_ref[...] = ym
        o = _dot(ym, w_ref[...])
        o_ref[...] = o
        on, _ = _rms_fwd(o, gp_ref[...])
        if last:
            @pl.when(pl.program_id(0) == 0)
            def _():
                sq_ref[...] = jnp.zeros_like(sq_ref)

            d = (x_ref[...] + on) - t_ref[...]
            sq_ref[...] += jnp.sum(d * d, axis=0, keepdims=True)
            xn_ref[...] = d * (1.0 / D_MODEL)
        else:
            xn_ref[...] = x_ref[...] + on

    row = lambda w: pl.BlockSpec((tm, w), lambda n: (n, 0))
    vec = pl.BlockSpec((1, 1024), lambda n: (0, 0))
    return pl.pallas_call(
        body, name="post_fwd", grid=(T // tm,),
        in_specs=[row(256), row(256), row(512), row(256), pl.BlockSpec((tm, 1024), lambda n: (n, C_GATE // 1024)),
                  vec, pl.BlockSpec((1024, 1024), lambda n: (0, 0)), vec, row(1024)] + ([row(1024)] if last else []),
        out_specs=[row(1024), row(1024), row(1024)] + ([vec] if last else []),
        out_shape=[jax.ShapeDtypeStruct((T, 1024), F32), jax.ShapeDtypeStruct((T, 1024), BF),
                   jax.ShapeDtypeStruct((T, 1024), F32)] + ([jax.ShapeDtypeStruct((1, 1024), F32)] if last else []),
        compiler_params=_params(1))(*([ya, yb, ycp, yd, h32, ggrp, wout, gpost, x] + ([tgt] if last else [])))


def _post_bwd(dx, o, gpost, woutt, ya, yb, ycp, yd, h32, ggrp, ym):
    T = dx.shape[0]
    tm = 256

    def body(dx_ref, o_ref, gp_ref, w_ref, ya_ref, yb_ref, yc_ref, yd_ref, gate_ref, gg_ref, ym_ref,
             dw_ref, dya_ref, dyb_ref, dyc_ref, dyd_ref, dgate_ref, dgp_ref, dgg_ref):
        n = pl.program_id(0)

        @pl.when(n == 0)
        def _():
            dw_ref[...] = jnp.zeros_like(dw_ref)
            dgp_ref[...] = jnp.zeros_like(dgp_ref)
            dgg_ref[...] = jnp.zeros_like(dgg_ref)

        ov, gp = o_ref[...], gp_ref[...]
        _, ro = _rms_fwd(ov, gp)
        do, dgt = _rms_bwd(ov, gp, ro, dx_ref[...])
        dgp_ref[...] += jnp.sum(dgt, axis=0, keepdims=True)
        dob = do.astype(BF)
        dw_ref[...] += _dot_tn(ym_ref[...], dob)
        dym = _dot(dob, w_ref[...])
        gate = gate_ref[...]
        sg = 1.0 / (1.0 + jnp.exp(-gate))
        sil = gate * sg
        dsil = sg * (1.0 + gate * (1.0 - sg))
        ys = [ya_ref[...], yb_ref[...], _compact_c(yc_ref[...]), yd_ref[...]]
        dys = []
        for gi in range(4):
            gs = slice(gi * 256, (gi + 1) * 256)
            gg = gg_ref[:, gs]
            ng, rg = _rms_fwd(ys[gi], gg)
            dgate_ref[:, gs] = (dym[:, gs] * ng * dsil[:, gs]).astype(BF)
            dy, dgt2 = _rms_bwd(ys[gi], gg, rg, dym[:, gs] * sil[:, gs])
            dgg_ref[:, gs] += jnp.sum(dgt2, axis=0, keepdims=True)
            dys.append(dy)
        dya_ref[...] = dys[0]
        dyb_ref[...] = dys[1]
        dyd_ref[...] = dys[3]
        z64 = jnp.zeros((tm, 64), F32)
        dyc_ref[...] = jnp.concatenate(
            [piece for h in range(4) for piece in (dys[2][:, h * 64:(h + 1) * 64], z64)], axis=1)

    row = lambda w: pl.BlockSpec((tm, w), lambda n: (n, 0))
    vec = pl.BlockSpec((1, 1024), lambda n: (0, 0))
    return pl.pallas_call(
        body, name="post_bwd", grid=(T // tm,),
        in_specs=[row(1024), row(1024), vec, pl.BlockSpec((1024, 1024), lambda n: (0, 0)),
                  row(256), row(256), row(512), row(256),
                  pl.BlockSpec((tm, 1024), lambda n: (n, C_GATE // 1024)), vec, row(1024)],
        out_specs=[pl.BlockSpec((1024, 1024), lambda n: (0, 0)), row(256), row(256), row(512), row(256), row(1024),
                   vec, vec],
        out_shape=[jax.ShapeDtypeStruct((1024, 1024), F32), jax.ShapeDtypeStruct((T, 256), F32),
                   jax.ShapeDtypeStruct((T, 256), F32), jax.ShapeDtypeStruct((T, 512), F32),
                   jax.ShapeDtypeStruct((T, 256), F32), jax.ShapeDtypeStruct((T, 1024), BF),
                   jax.ShapeDtypeStruct((1, 1024), F32), jax.ShapeDtypeStruct((1, 1024), F32)],
        compiler_params=_params(1))(dx, o, gpost, woutt, ya, yb, ycp, yd, h32, ggrp, ym)


def _swap_rows32(a):
    return jnp.concatenate([a[16:32], a[0:16]], axis=0)


def _pad_w_uq(w):
    z = lambda n: jnp.zeros((w.shape[0], n), w.dtype)
    a = [p for h in range(4) for p in (w[:, 96 * h:96 * h + 96], z(32))]
    b = [p for h in range(4) for p in (z(64), _swap32(w[:, 96 * h + 64:96 * h + 96]), z(32))]
    return jnp.concatenate(a + b, axis=1)


def _unpad_w_uq(d):
    out = []
    for h in range(4):
        out.append(d[:, 128 * h:128 * h + 64])
        out.append(d[:, 128 * h + 64:128 * h + 96] + _swap32(d[:, 512 + 128 * h + 64:512 + 128 * h + 96]))
    return jnp.concatenate(out, axis=1)


def _pad_w_ukv(w):
    z = jnp.zeros((w.shape[0], 64), w.dtype)
    a = [p for h in range(4) for p in (w[:, 128 * h:128 * h + 64], z)]
    b = [p for h in range(4) for p in (w[:, 128 * h + 64:128 * h + 128], z)]
    return jnp.concatenate(a + b, axis=1)


def _unpad_w_ukv(d):
    return jnp.concatenate([p for h in range(4) for p in (d[:, 128 * h:128 * h + 64],
                                                          d[:, 512 + 128 * h:512 + 128 * h + 64])], axis=1)


def _rope_tables(pos):
    freqs = 10000.0 ** (-jnp.arange(16, dtype=F32) / 16)
    ang = pos.astype(F32)[:, None] * freqs
    c, s = jnp.cos(ang), jnp.sin(ang)
    z = lambda n: jnp.zeros((pos.shape[0], n), F32)
    return (jnp.concatenate([z(64), c, c, z(32)], axis=1), jnp.concatenate([z(64), -s, s, z(32)], axis=1))


def _layer_weights(W, l):
    wuq2 = _pad_w_uq(W["mla_w_uq"][l])
    wkv2 = _pad_w_ukv(W["mla_w_ukv"][l])
    wout = W["w_out"][l]
    cw = jnp.concatenate([W["conv_w"][l].astype(F32), jnp.zeros((5, 256), F32)], axis=0)
    return dict(
        wpt=W["wpt"][l], wuq2=wuq2.astype(BF), wuq2t=wuq2.T.astype(BF),
        wkv2=wkv2.astype(BF), wkv2t=wkv2.T.astype(BF), wout=wout.astype(BF), woutt=wout.T.astype(BF),
        cw=cw, cb=W["conv_b"][l][None, :], sinks=W["attn_sinks"][l],
        gpre=W["norm_pre"][l][None, :], gq=W["mla_q_norm"][l][None, :], gkv=W["mla_kv_norm"][l][None, :],
        ggrp=W["group_norm"][l][None, :], gpost=W["norm_post"][l][None, :])


def _local_step(x, pos, W, tgt):
    cosk, sin = _rope_tables(pos)
    saved = []
    for l in range(DEPTH):
        lw = _layer_weights(W, l)
        h32, h16, xn = _inproj_fwd(x, lw["gpre"], lw["wpt"])
        ya = _swa_fwd(h16, lw["sinks"])
        yb = _conv_fwd(h32, lw["cw"], lw["cb"])
        qc, kc, vc = _cprep_fwd(h32, lw["gq"], lw["gkv"], lw["wuq2"], lw["wkv2"], cosk, sin)
        ycp, lse = _mla_fwd(qc, kc, vc)
        yd = _sb_fwd(h16)
        if l < DEPTH - 1:
            x_new, ym, o = _post_fwd(ya, yb, ycp, yd, h32, lw["ggrp"], lw["wout"], lw["gpost"], x)
        else:
            dx, ym, o, sq = _post_fwd(ya, yb, ycp, yd, h32, lw["ggrp"], lw["wout"], lw["gpost"], x, tgt)
        saved.append(dict(lw=lw, x=x, h32=h32, h16=h16, xn=xn, ya=ya, yb=yb, qc=qc, kc=kc, vc=vc, ycp=ycp,
                          lse=lse, yd=yd, ym=ym, o=o))
        if l < DEPTH - 1:
            x = x_new

    grads = {k: [None] * DEPTH for k in ("norm_pre", "w_in_pt", "attn_sinks", "conv_w", "conv_b", "mla_q_norm",
                                         "mla_w_uq", "mla_kv_norm", "mla_w_ukv", "group_norm", "w_out",
                                         "norm_post")}
    for l in reversed(range(DEPTH)):
        s = saved[l]
        lw = s["lw"]
        dwout, dya, dyb, dycp, dyd, dgate, dgpost, dggrp = _post_bwd(
            dx, s["o"], lw["gpost"], lw["woutt"], s["ya"], s["yb"], s["ycp"], s["yd"], s["h32"], lw["ggrp"], s["ym"])
        grads["norm_post"][l] = dgpost[0]
        grads["group_norm"][l] = dggrp[0]
        grads["w_out"][l] = dwout
        sdq, sdk, sdv = _sb_bwd(s["h16"], s["yd"], dyd)
        mdq, mdk, mdv = _mla_bwd(s["qc"], s["kc"], s["vc"], s["ycp"], s["lse"], dycp)
        dcq, dckv, dckr, dckrs, dwuq2, dwkv2, dgq, dgkv = _cprep_bwd(
            s["h32"], lw["gq"], lw["gkv"], lw["wuq2t"], lw["wkv2t"], cosk, sin, mdq, mdk, mdv)
        grads["mla_q_norm"][l] = dgq[0]
        grads["mla_kv_norm"][l] = dgkv[0]
        grads["mla_w_uq"][l] = _unpad_w_uq(dwuq2)
        grads["mla_w_ukv"][l] = _unpad_w_ukv(dwkv2)
        dbb, dbc, dbx, dcw = _conv_bwd(s["h32"], lw["cw"], lw["cb"], dyb)
        grads["conv_w"][l] = dcw[0:3]
        grads["conv_b"][l] = dcw[3]
        adq, adk, adv, dsk = _swa_bwd(s["h16"], lw["sinks"], dya)
        grads["attn_sinks"][l] = dsk[0, 0:4]
        parts = [dgate, adq, adk, adv, dbb, dbc, dbx, dcq, dckv, dckr, dckrs, sdq, sdk, sdv]
        dwp, dx, dgpre = _inproj_bwd(parts, lw["wpt"], s["x"], s["xn"], lw["gpre"], dx)
        grads["w_in_pt"][l] = dwp.T
        grads["norm_pre"][l] = dgpre[0]
    return sq, dx, grads


SMALL_SHARDED = ("conv_w", "mla_w_uq", "mla_w_ukv")
REPLICATED = ("norm_pre", "attn_sinks", "conv_b", "mla_q_norm", "mla_kv_norm", "group_norm", "norm_post")
ORDER = ("norm_pre", "w_in", "attn_sinks", "conv_w", "conv_b", "mla_q_norm", "mla_w_uq", "mla_kv_norm",
         "mla_w_ukv", "group_norm", "w_out", "norm_post")
W_IN_COLS = 436
W_IN_WIN = 440
SMALL_ROWS = 48


def _pack_small(arrs, dtype):
    flat = jnp.concatenate([a.reshape(-1).astype(dtype) for a in arrs])
    flat = jnp.concatenate([flat, jnp.zeros((SMALL_ROWS * D_MODEL - flat.shape[0],), dtype)])
    return flat.reshape(SMALL_ROWS, D_MODEL)


TAIL_ROW0 = DEPTH * W_IN_WIN


def _pack_state(ps):
    k = len(ps)
    wout = jnp.stack([p["w_out"] for p in ps]).reshape(k, DEPTH * 128, D_MODEL)
    flat = jnp.stack([jnp.concatenate([p[n].reshape(-1) for n in SMALL_SHARDED + REPLICATED]) for p in ps])
    small = jnp.pad(flat, ((0, 0), (0, SMALL_ROWS * D_MODEL - flat.shape[1]))).reshape(k, SMALL_ROWS, D_MODEL)
    return jnp.concatenate([wout, small], axis=1)


def _unpack_state(buf, p):
    k = buf.shape[0]
    out = {"w_out": buf[:, 0:DEPTH * 128].reshape(k, DEPTH, 128, D_MODEL)}
    flat = buf[:, DEPTH * 128:].reshape(k, SMALL_ROWS * D_MODEL)
    off = 0
    for n in SMALL_SHARDED + REPLICATED:
        size = int(np.prod(p[n].shape))
        out[n] = flat[:, off:off + size].reshape((k,) + p[n].shape)
        off += size
    return out


def _rows_of_w_in_t(lo, hi, padded, kr):
    segs = ((0, 1664, padded, C_AQ), (1664, 1696, kr, 0), (1696, 2464, padded, C_DQ), (2464, 3488, padded, C_GATE))
    out = []
    for s0, s1, src, base in segs:
        a, b = max(lo, s0), min(hi, s1)
        if a < b:
            out.append(src[base + a - s0:base + b - s0])
    return out


def _me():
    return lax.axis_index("x"), lax.axis_index("y"), lax.axis_index("c")


def _all_gather(block):
    R, C = block.shape

    def body(src_ref, out_ref, send_sems, recv_sems, local_sem):
        x, y, c = _me()
        me, sibling = (x, y, c), (x, y, 1 - c)
        chips = [(1 - x, y), (x, 1 - y), (1 - x, 1 - y)]

        def slot(px, py, pc):
            return out_ref.at[4 * px + 2 * py + pc]

        def copy(k, block, to, src=None):
            return pltpu.make_async_remote_copy(
                src_ref=slot(*block) if src is None else src, dst_ref=slot(*block), send_sem=send_sems.at[k],
                recv_sem=recv_sems.at[k], device_id=to, device_id_type=MESH)

        mine = pltpu.make_async_copy(src_ref, slot(*me), local_sem)
        mine.start()
        first = [copy(0, me, sibling, src=src_ref)]
        first += [copy(1 + j, me, (*chip, c), src=src_ref) for j, chip in enumerate(chips)]
        for cp in first:
            cp.start()
        passed = [copy(4 + j, (*chip, c), sibling) for j, chip in enumerate(chips)]
        for j, chip in enumerate(chips):
            copy(1 + j, (*chip, c), me).wait_recv()
            passed[j].start()
        copy(0, sibling, me).wait_recv()
        for j, chip in enumerate(chips):
            copy(4 + j, (*chip, 1 - c), me).wait_recv()
        for cp in first + passed:
            cp.wait_send()
        mine.wait()

    return pl.pallas_call(
        body, name="all_gather", out_shape=jax.ShapeDtypeStruct((N_DEV, R, C), block.dtype),
        in_specs=[pl.BlockSpec(memory_space=pl.ANY)], out_specs=pl.BlockSpec(memory_space=pl.ANY),
        scratch_shapes=[pltpu.SemaphoreType.DMA((N_DEV - 1,)), pltpu.SemaphoreType.DMA((N_DEV - 1,)),
                        pltpu.SemaphoreType.DMA])(block)


N_CHIP = 4


def _sibling_swap(blocks):
    _, R, C = blocks.shape

    def body(src_ref, out_ref, send_sems, recv_sems):
        x, y, c = _me()
        copies = [pltpu.make_async_remote_copy(
            src_ref=src_ref.at[2 * j + 1 - c], dst_ref=out_ref.at[j], send_sem=send_sems.at[j],
            recv_sem=recv_sems.at[j], device_id=(x, y, 1 - c), device_id_type=MESH) for j in range(N_CHIP)]
        for cp in copies:
            cp.start()
        for cp in copies:
            cp.wait()

    return pl.pallas_call(
        body, name="sibling_swap", out_shape=jax.ShapeDtypeStruct((N_CHIP, R, C), blocks.dtype),
        in_specs=[pl.BlockSpec(memory_space=pl.ANY)], out_specs=pl.BlockSpec(memory_space=pl.ANY),
        scratch_shapes=[pltpu.SemaphoreType.DMA((N_CHIP,)), pltpu.SemaphoreType.DMA((N_CHIP,))])(blocks)


def _pair_sum(a, b):
    n, R, C = a.shape
    tr = 592 if R % 592 == 0 else R

    def body(a_ref, b_ref, o_ref):
        o_ref[...] = (a_ref[...].astype(F32) + b_ref[...].astype(F32)).astype(BF)

    spec = pl.BlockSpec((1, tr, C), lambda j, r: (j, r, 0))
    return pl.pallas_call(body, name="pair_sum", grid=(n, R // tr), in_specs=[spec, spec], out_specs=spec,
                          out_shape=jax.ShapeDtypeStruct(a.shape, BF), compiler_params=_params(2))(a, b)


def _chip_exchange(sums):
    _, R, C = sums.shape

    def body(src_ref, out_ref, send_sems, recv_sems, local_sem):
        x, y, c = _me()
        here = 2 * x + y
        mine = pltpu.make_async_copy(src_ref.at[here], out_ref.at[here], local_sem)
        mine.start()
        copies = []
        for k in range(1, N_CHIP):
            px, py = x ^ (k >> 1), y ^ (k & 1)
            copies.append(pltpu.make_async_remote_copy(
                src_ref=src_ref.at[2 * px + py], dst_ref=out_ref.at[here], send_sem=send_sems.at[k - 1],
                recv_sem=recv_sems.at[k - 1], device_id=(px, py, c), device_id_type=MESH))
        for cp in copies:
            cp.start()
        for cp in copies:
            cp.wait()
        mine.wait()

    return pl.pallas_call(
        body, name="chip_exchange", out_shape=jax.ShapeDtypeStruct((N_CHIP, R, C), sums.dtype),
        in_specs=[pl.BlockSpec(memory_space=pl.ANY)], out_specs=pl.BlockSpec(memory_space=pl.ANY),
        scratch_shapes=[pltpu.SemaphoreType.DMA((N_CHIP - 1,)), pltpu.SemaphoreType.DMA((N_CHIP - 1,)),
                        pltpu.SemaphoreType.DMA])(sums)


def _adamw_update(g, w, m, v):
    m_ = ADAM_B1 * m + (1.0 - ADAM_B1) * g
    v_ = ADAM_B2 * v + (1.0 - ADAM_B2) * (g * g)
    m_hat = m_ / (1.0 - ADAM_B1 ** ADAM_STEP)
    v_hat = v_ / (1.0 - ADAM_B2 ** ADAM_STEP)
    return -ADAM_LR * (m_hat / (jnp.sqrt(v_hat) + ADAM_EPS) + ADAM_WD * w), m_, v_


def _adamw(parts, state):
    _, R, C = state.shape
    n_parts = parts.shape[0]
    tr = 16
    assert R % tr == 0 and TAIL_ROW0 % tr == 0

    def body(p_ref, s_ref, o_ref):
        g = p_ref[0].astype(F32)
        for k in range(1, n_parts):
            g = g + p_ref[k].astype(F32)
        o_ref[0] = g
        o_ref[1], o_ref[2], o_ref[3] = _adamw_update(g, s_ref[0], s_ref[1], s_ref[2])

    return pl.pallas_call(
        body, name="adamw", grid=(R // tr,),
        in_specs=[pl.BlockSpec((n_parts, tr, C), lambda n: (0, n + TAIL_ROW0 // tr, 0)),
                  pl.BlockSpec((3, tr, C), lambda n: (0, n, 0))],
        out_specs=pl.BlockSpec((4, tr, C), lambda n: (0, n, 0)), out_shape=jax.ShapeDtypeStruct((4, R, C), F32),
        compiler_params=_params(1))(parts, state)


def _adamw_w_in(parts, w, m, v, core):
    n_parts = parts.shape[0]
    tc = 256

    def body(core_ref, p_ref, w_ref, m_ref, v_ref, o_ref):
        g_t = p_ref[0].astype(F32)
        for k in range(1, n_parts):
            g_t = g_t + p_ref[k].astype(F32)
        g_t = jnp.concatenate([g_t, jnp.zeros((512 - W_IN_WIN, tc), F32)], axis=0).T
        g = jnp.where(core_ref[0] == 0, g_t[:, 0:W_IN_COLS], g_t[:, W_IN_WIN - W_IN_COLS:W_IN_WIN])
        o_ref[0, 0] = g
        o_ref[1, 0], o_ref[2, 0], o_ref[3, 0] = _adamw_update(g, w_ref[0], m_ref[0], v_ref[0])

    nat = pl.BlockSpec((1, tc, W_IN_COLS), lambda l, j: (l, j, 0))
    return pl.pallas_call(
        body, name="adamw_w_in", grid=(DEPTH, D_MODEL // tc),
        in_specs=[pl.BlockSpec(memory_space=pltpu.SMEM),
                  pl.BlockSpec((n_parts, W_IN_WIN, tc), lambda l, j: (0, l, j)), nat, nat, nat],
        out_specs=pl.BlockSpec((4, 1, tc, W_IN_COLS), lambda l, j: (0, l, j, 0)),
        out_shape=jax.ShapeDtypeStruct((4, DEPTH, D_MODEL, W_IN_COLS), F32),
        compiler_params=_params(2))(core, parts, w, m, v)


def kernel(x, positions, norm_pre, w_in, attn_sinks, conv_w, conv_b, mla_q_norm, mla_w_uq, mla_kv_norm, mla_w_ukv, group_norm, w_out, norm_post, loss_target, m_norm_pre, m_w_in, m_attn_sinks, m_conv_w, m_conv_b, m_mla_q_norm, m_mla_w_uq, m_mla_kv_norm, m_mla_w_ukv, m_group_norm, m_w_out, m_norm_post, v_norm_pre, v_w_in, v_attn_sinks, v_conv_w, v_conv_b, v_mla_q_norm, v_mla_w_uq, v_mla_kv_norm, v_mla_w_ukv, v_group_norm, v_w_out, v_norm_post):
    local = dict(norm_pre=norm_pre, w_in=w_in, attn_sinks=attn_sinks, conv_w=conv_w, conv_b=conv_b,
                 mla_q_norm=mla_q_norm, mla_w_uq=mla_w_uq, mla_kv_norm=mla_kv_norm, mla_w_ukv=mla_w_ukv,
                 group_norm=group_norm, w_out=w_out, norm_post=norm_post)
    mom = dict(norm_pre=m_norm_pre, w_in=m_w_in, attn_sinks=m_attn_sinks, conv_w=m_conv_w, conv_b=m_conv_b,
               mla_q_norm=m_mla_q_norm, mla_w_uq=m_mla_w_uq, mla_kv_norm=m_mla_kv_norm, mla_w_ukv=m_mla_w_ukv,
               group_norm=m_group_norm, w_out=m_w_out, norm_post=m_norm_post)
    vel = dict(norm_pre=v_norm_pre, w_in=v_w_in, attn_sinks=v_attn_sinks, conv_w=v_conv_w, conv_b=v_conv_b,
               mla_q_norm=v_mla_q_norm, mla_w_uq=v_mla_w_uq, mla_kv_norm=v_mla_kv_norm, mla_w_ukv=v_mla_w_ukv,
               group_norm=v_group_norm, w_out=v_w_out, norm_post=v_norm_post)

    c = lax.axis_index("c")

    tile = 16
    slot_rows = 464
    shift = 8 * lax.axis_index("y") + 4 * c
    wt = lax.dynamic_update_slice(jnp.zeros((DEPTH, slot_rows, D_MODEL), BF),
                                  jnp.transpose(w_in, (0, 2, 1)).astype(BF), (0, shift, 0))
    payload = jnp.concatenate([wt.reshape(DEPTH * slot_rows, D_MODEL),
                               w_out.astype(BF).reshape(DEPTH * 128, D_MODEL),
                               _pack_small([local[n] for n in SMALL_SHARDED], BF)], axis=0)
    gathered = _all_gather(payload)
    W = {n: local[n] for n in REPLICATED}

    def nat_rows(l, lo, hi):
        def piece(d, r0, r1):
            base = slot_rows * l - (W_IN_COLS * d) // tile * tile
            return gathered[d, base + r0:base + r1]

        out, run = [], None
        for r0 in range(lo, hi, tile):
            d0, d1 = r0 // W_IN_COLS, (r0 + tile - 1) // W_IN_COLS
            if d0 == d1 and run is not None and run[0] == d0:
                run = (d0, run[1], r0 + tile)
                continue
            if run is not None:
                out.append(piece(*run))
                run = None
            if d0 == d1:
                run = (d0, r0, r0 + tile)
            else:
                out.append(piece(d0, r0, r0 + tile) + piece(d1, r0, r0 + tile))
        if run is not None:
            out.append(piece(*run))
        return out

    z = lambda n: [jnp.zeros((n, D_MODEL), BF)]
    W["wpt"] = [jnp.concatenate(nat_rows(l, 2464, 3488) + nat_rows(l, 0, 1664) + z(64) + nat_rows(l, 1664, 1696)
                                + z(96) + nat_rows(l, 1680, 1696) + nat_rows(l, 1664, 1680) + z(32)
                                + nat_rows(l, 1696, 2464) + z(NP - C_END), axis=0) for l in range(DEPTH)]
    wo0 = DEPTH * slot_rows
    W["w_out"] = gathered[:, wo0:wo0 + DEPTH * 128].reshape(N_DEV, DEPTH, 128, D_MODEL).transpose(1, 0, 2, 3).reshape(
        DEPTH, D_MODEL, D_MODEL)
    flat = gathered[:, wo0 + DEPTH * 128:].reshape(N_DEV, SMALL_ROWS * D_MODEL)
    off = 0
    for n in SMALL_SHARDED:
        depth, rows, width = local[n].shape
        size = depth * rows * width
        W[n] = flat[:, off:off + size].reshape(N_DEV, depth, rows, width).transpose(1, 2, 0, 3).reshape(
            depth, rows, N_DEV * width)
        off += size

    sq, grad_x, g = _local_step(x[0], positions[0], W, loss_target[0])
    loss = lax.psum(0.5 / D_MODEL * jnp.sum(sq), ("x", "y", "c"))

    cols = []
    for n in SMALL_SHARDED:
        depth, rows, width = local[n].shape
        cols.append(jnp.stack(g[n]).reshape(depth, rows, N_DEV, width).transpose(2, 0, 1, 3).reshape(N_DEV, -1))
    rep = jnp.concatenate([a.reshape(-1) for n in REPLICATED for a in g[n]])
    cols.append(jnp.broadcast_to(rep[None], (N_DEV, rep.shape[0])))
    small = jnp.concatenate(cols, axis=1)
    small = jnp.pad(small, ((0, 0), (0, SMALL_ROWS * D_MODEL - small.shape[1]))).reshape(N_DEV, SMALL_ROWS, D_MODEL)
    krs = [p[C_CKR + 64:C_CKR + 96] + _swap_rows32(p[C_CKRS + 64:C_CKRS + 96]) for p in g["w_in_pt"]]
    pieces = []
    for d in range(N_DEV):
        lo = W_IN_COLS * d // 8 * 8
        for l in range(DEPTH):
            pieces += _rows_of_w_in_t(lo, lo + W_IN_WIN, g["w_in_pt"][l], krs[l])
        pieces += [g["w_out"][l][128 * d:128 * (d + 1)] for l in range(DEPTH)]
        pieces.append(small[d])
    blocks = jnp.concatenate(pieces, axis=0).astype(BF).reshape(N_DEV, -1, D_MODEL)
    mine = lax.dynamic_index_in_dim(blocks.reshape(N_CHIP, 2, -1, D_MODEL), c, axis=1, keepdims=False)
    received = _chip_exchange(_pair_sum(mine, _sibling_swap(blocks)))

    out = _unpack_state(_adamw(received, _pack_state([local, mom, vel])), local)
    out["w_in"] = _adamw_w_in(received, w_in, m_w_in, v_w_in, c.astype(jnp.int32).reshape(1))
    return (loss, grad_x[None], *[out[n][t] for t in range(4) for n in ORDER])
```

```python
import functools

import jax
import jax.numpy as jnp
import numpy as np
from jax import lax
from jax.experimental import pallas as pl
from jax.experimental.pallas import tpu as pltpu

F32 = jnp.float32
BF = jnp.bfloat16
MESH = pl.DeviceIdType.MESH

D_MODEL = 1024
DEPTH = 2
EPS = 1e-6
N_DEV = 8
VMEM_LIMIT = 56 * 1024 * 1024
NEG = -1e30
MLA_SCALE = 96.0 ** -0.5
SB_SCALE = 0.125
LOG2E = 1.4426950408889634

NP = 3712
C_GATE = 0
C_AQ = 1024
C_AK = 1280
C_AV = 1408
C_BB = 1536
C_BC = 1792
C_BX = 2048
C_CQ = 2304
C_CKV = 2560
C_CKR = 2688
C_CKRS = 2816
C_DQ = 2944
C_DK = 3200
C_DV = 3456
C_END = 3712

def _swap32(a):
    return jnp.concatenate([a[:, 16:32], a[:, 0:16]], axis=1)

ADAM_LR, ADAM_B1, ADAM_B2, ADAM_EPS, ADAM_WD, ADAM_STEP = 0.001, 0.9, 0.999, 1e-08, 0.01, 10


def _dot(a, b):
    return jnp.dot(a, b, preferred_element_type=F32)


def _dot_nt(a, b):
    return lax.dot_general(a, b, (((1,), (1,)), ((), ())), preferred_element_type=F32)


def _dot_tn(a, b):
    return lax.dot_general(a, b, (((0,), (0,)), ((), ())), preferred_element_type=F32)


def _params(n_grid):
    return pltpu.CompilerParams(dimension_semantics=("arbitrary",) * n_grid, vmem_limit_bytes=VMEM_LIMIT)


def _rms_fwd(x, g):
    r = lax.rsqrt(jnp.mean(x * x, axis=-1, keepdims=True) + EPS)
    return (x * r) * g, r


def _rms_bwd(x, g, r, dy, width=None):
    n = x.shape[-1] if width is None else width
    u = dy * g
    dx = r * u - x * (r * r * r) * (jnp.sum(x * u, axis=-1, keepdims=True) / n)
    return dx, dy * (x * r)


def _iota(shape, axis):
    return lax.broadcasted_iota(jnp.int32, shape, axis)


def _inproj_fwd(x, g, wpt):
    T = x.shape[0]
    tm = 256

    def body(x_ref, g_ref, w_ref, h32_ref, h16_ref, xn_ref):
        xn, _ = _rms_fwd(x_ref[...], g_ref[...])
        xn = xn.astype(BF)
        xn_ref[...] = xn
        h = _dot_nt(xn, w_ref[...])
        h32_ref[...] = h
        h16_ref[...] = h.astype(BF)

    return pl.pallas_call(
        body, name="inproj_fwd", grid=(T // tm,),
        in_specs=[pl.BlockSpec((tm, D_MODEL), lambda n: (n, 0)),
                  pl.BlockSpec((1, D_MODEL), lambda n: (0, 0)),
                  pl.BlockSpec((NP, D_MODEL), lambda n: (0, 0))],
        out_specs=[pl.BlockSpec((tm, NP), lambda n: (n, 0)),
                   pl.BlockSpec((tm, NP), lambda n: (n, 0)),
                   pl.BlockSpec((tm, D_MODEL), lambda n: (n, 0))],
        out_shape=[jax.ShapeDtypeStruct((T, NP), F32), jax.ShapeDtypeStruct((T, NP), BF),
                   jax.ShapeDtypeStruct((T, D_MODEL), BF)],
        compiler_params=_params(1))(x, g, wpt)


def _inproj_bwd(parts, wpt, x, xn, g, dxo):
    T = x.shape[0]
    tm = 256
    np_ = len(parts)
    chunks = (0, 1280, 2560, NP)
    assert sum(p.shape[1] for p in parts) == C_END == NP

    def body(*refs):
        part_refs = refs[:np_]
        w_ref, x_ref, xn_ref, g_ref, dxo_ref, dw_ref, dx_ref, dg_ref = refs[np_:]
        n = pl.program_id(0)

        @pl.when(n == 0)
        def _():
            dw_ref[...] = jnp.zeros_like(dw_ref)
            dg_ref[...] = jnp.zeros_like(dg_ref)

        dh = jnp.concatenate([r[...].astype(BF) for r in part_refs], axis=1)
        xnv = xn_ref[...]
        for lo, hi in zip(chunks[:-1], chunks[1:]):
            dw_ref[:, lo:hi] += _dot_tn(xnv, dh[:, lo:hi])
        dxn = _dot(dh, w_ref[...])
        xv = x_ref[...]
        _, r = _rms_fwd(xv, g_ref[...])
        dx, dgt = _rms_bwd(xv, g_ref[...], r, dxn)
        dx_ref[...] = dxo_ref[...] + dx
        dg_ref[...] += jnp.sum(dgt, axis=0, keepdims=True)

    once = pl.Buffered(1)
    return pl.pallas_call(
        body, name="inproj_bwd", grid=(T // tm,),
        in_specs=[pl.BlockSpec((tm, p.shape[1]), lambda n: (n, 0)) for p in parts]
        + [pl.BlockSpec((NP, D_MODEL), lambda n: (0, 0), pipeline_mode=once),
           pl.BlockSpec((tm, D_MODEL), lambda n: (n, 0)),
           pl.BlockSpec((tm, D_MODEL), lambda n: (n, 0)),
           pl.BlockSpec((1, D_MODEL), lambda n: (0, 0)),
           pl.BlockSpec((tm, D_MODEL), lambda n: (n, 0))],
        out_specs=[pl.BlockSpec((D_MODEL, NP), lambda n: (0, 0), pipeline_mode=once),
                   pl.BlockSpec((tm, D_MODEL), lambda n: (n, 0)),
                   pl.BlockSpec((1, D_MODEL), lambda n: (0, 0))],
        out_shape=[jax.ShapeDtypeStruct((D_MODEL, NP), F32), jax.ShapeDtypeStruct((T, D_MODEL), F32),
                   jax.ShapeDtypeStruct((1, D_MODEL), F32)],
        compiler_params=_params(1))(*parts, wpt, x, xn, g, dxo)


SWA_BLK = 128
SWA_TQ = 1024


def _bdot_nt(a, b):
    return lax.dot_general(a, b, (((2,), (2,)), ((0,), (0,))), preferred_element_type=F32)


def _bdot(a, b):
    return lax.dot_general(a, b, (((2,), (1,)), ((0,), (0,))), preferred_element_type=F32)


def _bdot_tn(a, b):
    return lax.dot_general(a, b, (((1,), (1,)), ((0,), (0,))), preferred_element_type=F32)


def _swa_probs(q, kc, kp, sink, mask_c, mask_p):
    sc = jnp.where(mask_c, _bdot_nt(q, kc) * SB_SCALE, NEG)
    sp = jnp.where(mask_p, _bdot_nt(q, kp) * SB_SCALE, NEG)
    m = jnp.maximum(jnp.maximum(jnp.max(sc, axis=-1, keepdims=True), jnp.max(sp, axis=-1, keepdims=True)), sink)
    pc = jnp.exp(sc - m)
    pp = jnp.exp(sp - m)
    ps = jnp.exp(sink - m)
    inv = 1.0 / (jnp.sum(pc, axis=-1, keepdims=True) + jnp.sum(pp, axis=-1, keepdims=True) + ps)
    return pc * inv, pp * inv, ps * inv


def _swa_masks(n, nb):
    blk = _iota((nb, SWA_BLK, SWA_BLK), 0)
    row = _iota((nb, SWA_BLK, SWA_BLK), 1)
    col = _iota((nb, SWA_BLK, SWA_BLK), 2)
    return col <= row, jnp.logical_and(col > row, jnp.logical_or(blk > 0, n > 0))


def _swa_specs(tq):
    halo = tq // SWA_BLK
    return [pl.BlockSpec(memory_space=pltpu.SMEM),
            pl.BlockSpec((tq, 256), lambda n: (n, C_AQ // 256)),
            pl.BlockSpec((tq, 128), lambda n: (n, C_AK // 128)),
            pl.BlockSpec((SWA_BLK, 128), lambda n: (jnp.maximum(n * halo - 1, 0), C_AK // 128)),
            pl.BlockSpec((tq, 128), lambda n: (n, C_AV // 128)),
            pl.BlockSpec((SWA_BLK, 128), lambda n: (jnp.maximum(n * halo - 1, 0), C_AV // 128))]


def _swa_blocked(cur_ref, prev_ref, gs, nb):
    cur = cur_ref[:, gs].reshape(nb, SWA_BLK, 64)
    prev = jnp.concatenate([prev_ref[:, gs].reshape(1, SWA_BLK, 64), cur[:nb - 1]], axis=0) if nb > 1 \
        else prev_ref[:, gs].reshape(1, SWA_BLK, 64)
    return cur, prev


def _swa_fwd(h16, sinks):
    T = h16.shape[0]
    tq = SWA_TQ if T % SWA_TQ == 0 else SWA_BLK
    nb = tq // SWA_BLK

    def body(s_ref, q_ref, kc_ref, kp_ref, vc_ref, vp_ref, o_ref):
        n = pl.program_id(0)
        mask_c, mask_p = _swa_masks(n, nb)
        for h in range(4):
            hs = slice(h * 64, (h + 1) * 64)
            gs = slice(h // 2 * 64, (h // 2 + 1) * 64)
            kc, kp = _swa_blocked(kc_ref, kp_ref, gs, nb)
            vc, vp = _swa_blocked(vc_ref, vp_ref, gs, nb)
            pc, pp, _ = _swa_probs(q_ref[:, hs].reshape(nb, SWA_BLK, 64), kc, kp, s_ref[h], mask_c, mask_p)
            o_ref[:, hs] = (_bdot(pc.astype(BF), vc) + _bdot(pp.astype(BF), vp)).reshape(tq, 64)

    return pl.pallas_call(
        body, name="swa_fwd", grid=(T // tq,), in_specs=_swa_specs(tq),
        out_specs=pl.BlockSpec((tq, 256), lambda n: (n, 0)),
        out_shape=jax.ShapeDtypeStruct((T, 256), F32),
        compiler_params=_params(1))(sinks, h16, h16, h16, h16, h16)


def _swa_bwd(h16, sinks, dya):
    T = h16.shape[0]
    tq = SWA_TQ if T % SWA_TQ == 0 else SWA_BLK
    nb = tq // SWA_BLK

    def body(s_ref, q_ref, kc_ref, kp_ref, vc_ref, vp_ref, do_ref, dq_ref, dk_ref, dv_ref, ds_ref):
        n = pl.program_id(0)

        @pl.when(n == 0)
        def _():
            dk_ref[...] = jnp.zeros_like(dk_ref)
            dv_ref[...] = jnp.zeros_like(dv_ref)
            ds_ref[...] = jnp.zeros_like(ds_ref)

        mask_c, mask_p = _swa_masks(n, nb)
        rows = pl.ds(pl.multiple_of(n * tq, tq), tq)
        before = pl.ds(pl.multiple_of(jnp.maximum(n * nb - 1, 0) * SWA_BLK, SWA_BLK), SWA_BLK)
        lane = _iota((8, 128), 1)
        row8 = _iota((8, 128), 0)

        def to_keys(own, prev):
            if nb == 1:
                return own
            return own + jnp.concatenate([prev[1:], jnp.zeros((1, SWA_BLK, 64), F32)], axis=0)

        for h in range(4):
            hs = slice(h * 64, (h + 1) * 64)
            gs = slice(h // 2 * 64, (h // 2 + 1) * 64)
            q = q_ref[:, hs].reshape(nb, SWA_BLK, 64)
            kc, kp = _swa_blocked(kc_ref, kp_ref, gs, nb)
            vc, vp = _swa_blocked(vc_ref, vp_ref, gs, nb)
            pc, pp, ps = _swa_probs(q, kc, kp, s_ref[h], mask_c, mask_p)
            pcb, ppb = pc.astype(BF), pp.astype(BF)
            do = do_ref[:, hs].reshape(nb, SWA_BLK, 64)
            dob = do.astype(BF)
            o = _bdot(pcb, vc) + _bdot(ppb, vp)
            dd = jnp.sum(do * o, axis=-1, keepdims=True)
            dsc = (pc * (_bdot_nt(dob, vc) - dd) * SB_SCALE).astype(BF)
            dsp = (pp * (_bdot_nt(dob, vp) - dd) * SB_SCALE).astype(BF)
            dq_ref[:, hs] = (_bdot(dsc, kc) + _bdot(dsp, kp)).reshape(tq, 64).astype(BF)
            dkp, dvp = _bdot_tn(dsp, q), _bdot_tn(ppb, dob)
            dk_ref[rows, gs] += to_keys(_bdot_tn(dsc, q), dkp).reshape(tq, 64)
            dv_ref[rows, gs] += to_keys(_bdot_tn(pcb, dob), dvp).reshape(tq, 64)
            dk_ref[before, gs] += dkp[0]
            dv_ref[before, gs] += dvp[0]
            ds_ref[...] += jnp.where(jnp.logical_and(lane == h, row8 == 0), -jnp.sum(ps * dd), 0.0)

    return pl.pallas_call(
        body, name="swa_bwd", grid=(T // tq,),
        in_specs=_swa_specs(tq) + [pl.BlockSpec((tq, 256), lambda n: (n, 0))],
        out_specs=[pl.BlockSpec((tq, 256), lambda n: (n, 0)),
                   pl.BlockSpec((T, 128), lambda n: (0, 0)),
                   pl.BlockSpec((T, 128), lambda n: (0, 0)),
                   pl.BlockSpec((8, 128), lambda n: (0, 0))],
        out_shape=[jax.ShapeDtypeStruct((T, 256), BF), jax.ShapeDtypeStruct((T, 128), F32),
                   jax.ShapeDtypeStruct((T, 128), F32), jax.ShapeDtypeStruct((8, 128), F32)],
        compiler_params=_params(1))(sinks, h16, h16, h16, h16, h16, dya)


def _conv_u(bc_ref, bx_ref, bch_ref, bxh_ref, n, tm):
    u = bc_ref[...] * bx_ref[...]
    uh = bch_ref[...] * bxh_ref[...] * (n > 0).astype(F32)
    rowi = _iota((tm, 256), 0)
    u1 = jnp.where(rowi == 0, uh[7:8, :], pltpu.roll(u, 1, axis=0))
    u2 = jnp.where(rowi == 0, uh[6:7, :], jnp.where(rowi == 1, uh[7:8, :], pltpu.roll(u, 2, axis=0)))
    return u, u1, u2


def _conv_fwd(h32, cw, cb):
    T = h32.shape[0]
    tm = 512 if T % 512 == 0 else T
    hb = tm // 8

    def body(bb_ref, bc_ref, bx_ref, bch_ref, bxh_ref, w_ref, b_ref, o_ref):
        n = pl.program_id(0)
        u, u1, u2 = _conv_u(bc_ref, bx_ref, bch_ref, bxh_ref, n, tm)
        y = w_ref[0:1, :] * u2 + w_ref[1:2, :] * u1 + w_ref[2:3, :] * u + b_ref[...]
        o_ref[...] = bb_ref[...] * y

    halo = lambda c: pl.BlockSpec((8, 256), lambda n: (jnp.maximum(n * hb - 1, 0), c // 256))
    return pl.pallas_call(
        body, name="conv_fwd", grid=(T // tm,),
        in_specs=[pl.BlockSpec((tm, 256), lambda n: (n, C_BB // 256)),
                  pl.BlockSpec((tm, 256), lambda n: (n, C_BC // 256)),
                  pl.BlockSpec((tm, 256), lambda n: (n, C_BX // 256)),
                  halo(C_BC), halo(C_BX),
                  pl.BlockSpec((8, 256), lambda n: (0, 0)),
                  pl.BlockSpec((1, 256), lambda n: (0, 0))],
        out_specs=pl.BlockSpec((tm, 256), lambda n: (n, 0)),
        out_shape=jax.ShapeDtypeStruct((T, 256), F32),
        compiler_params=_params(1))(h32, h32, h32, h32, h32, cw, cb)


def _conv_bwd(h32, cw, cb, dyb):
    T = h32.shape[0]
    tm = 512 if T % 512 == 0 else T
    hb = tm // 8
    nt = T // tm

    def body(bb_ref, bc_ref, bx_ref, bch_ref, bxh_ref, bbn_ref, dy_ref, dyn_ref, w_ref, b_ref,
             dbb_ref, dbc_ref, dbx_ref, dw_ref):
        n = pl.program_id(0)

        @pl.when(n == 0)
        def _():
            dw_ref[...] = jnp.zeros_like(dw_ref)

        u, u1, u2 = _conv_u(bc_ref, bx_ref, bch_ref, bxh_ref, n, tm)
        w0, w1, w2 = w_ref[0:1, :], w_ref[1:2, :], w_ref[2:3, :]
        y = w0 * u2 + w1 * u1 + w2 * u + b_ref[...]
        dyb_ = dy_ref[...]
        dbb_ref[...] = (dyb_ * y).astype(BF)
        dy = dyb_ * bb_ref[...]
        dyn = dyn_ref[...] * bbn_ref[...] * (n < nt - 1).astype(F32)
        rowi = _iota((tm, 256), 0)
        dy1 = jnp.where(rowi == tm - 1, dyn[0:1, :], pltpu.roll(dy, tm - 1, axis=0))
        dy2 = jnp.where(rowi == tm - 2, dyn[0:1, :],
                        jnp.where(rowi == tm - 1, dyn[1:2, :], pltpu.roll(dy, tm - 2, axis=0)))
        du = w2 * dy + w1 * dy1 + w0 * dy2
        dbc_ref[...] = (du * bx_ref[...]).astype(BF)
        dbx_ref[...] = (du * bc_ref[...]).astype(BF)
        dw_ref[0:1, :] += jnp.sum(dy * u2, axis=0, keepdims=True)
        dw_ref[1:2, :] += jnp.sum(dy * u1, axis=0, keepdims=True)
        dw_ref[2:3, :] += jnp.sum(dy * u, axis=0, keepdims=True)
        dw_ref[3:4, :] += jnp.sum(dy, axis=0, keepdims=True)

    halo = lambda c: pl.BlockSpec((8, 256), lambda n: (jnp.maximum(n * hb - 1, 0), c // 256))
    nxt = lambda c: pl.BlockSpec((8, 256), lambda n: (jnp.minimum((n + 1) * hb, T // 8 - 1), c // 256))
    cur = lambda c: pl.BlockSpec((tm, 256), lambda n: (n, c // 256))
    return pl.pallas_call(
        body, name="conv_bwd", grid=(nt,),
        in_specs=[cur(C_BB), cur(C_BC), cur(C_BX), halo(C_BC), halo(C_BX), nxt(C_BB),
                  cur(0), nxt(0),
                  pl.BlockSpec((8, 256), lambda n: (0, 0)),
                  pl.BlockSpec((1, 256), lambda n: (0, 0))],
        out_specs=[cur(0), cur(0), cur(0), pl.BlockSpec((8, 256), lambda n: (0, 0))],
        out_shape=[jax.ShapeDtypeStruct((T, 256), BF)] * 3 + [jax.ShapeDtypeStruct((8, 256), F32)],
        compiler_params=_params(1))(h32, h32, h32, h32, h32, h32, dyb, dyb, cw, cb)


def _cprep_specs(tm):
    return [pl.BlockSpec((tm, 256), lambda n: (n, C_CQ // 256)),
            pl.BlockSpec((tm, 128), lambda n: (n, C_CKV // 128)),
            pl.BlockSpec((tm, 128), lambda n: (n, C_CKR // 128)),
            pl.BlockSpec((tm, 128), lambda n: (n, C_CKRS // 128)),
            pl.BlockSpec((1, 256), lambda n: (0, 0)),
            pl.BlockSpec((1, 128), lambda n: (0, 0)),
            pl.BlockSpec((tm, 128), lambda n: (n, 0)),
            pl.BlockSpec((tm, 128), lambda n: (n, 0))]


def _cprep_fwd(h32, gq, gkv, wuq2, wkv2, cosk, sin):
    T = h32.shape[0]
    tm = 512 if T % 512 == 0 else T

    def body(cq_ref, ckv_ref, ckr_ref, ckrs_ref, gq_ref, gkv_ref, cos_ref, sin_ref, wuq_ref, wkv_ref,
             q_ref, k_ref, v_ref):
        cosk_, sin_ = cos_ref[...], sin_ref[...]
        cosq = cosk_ + (_iota((tm, 128), 1) < 64).astype(F32)
        cqn, _ = _rms_fwd(cq_ref[...], gq_ref[...])
        q2 = _dot(cqn.astype(BF), wuq_ref[...])
        ckvn, _ = _rms_fwd(ckv_ref[...], gkv_ref[...])
        kv2 = _dot(ckvn.astype(BF), wkv_ref[...])
        kr = ckr_ref[...] * cosk_ + ckrs_ref[...] * sin_
        for h in range(4):
            hs = slice(h * 128, (h + 1) * 128)
            q_ref[:, hs] = ((q2[:, hs] * cosq + q2[:, 512 + h * 128:512 + (h + 1) * 128] * sin_) * MLA_SCALE).astype(BF)
            k_ref[:, hs] = (kv2[:, hs] + kr).astype(BF)
        ones = _iota((tm, 512), 1) % 128 == 64
        v_ref[...] = jnp.where(ones, 1.0, kv2[:, 512:]).astype(BF)

    return pl.pallas_call(
        body, name="cprep_fwd", grid=(T // tm,),
        in_specs=_cprep_specs(tm) + [pl.BlockSpec((256, 1024), lambda n: (0, 0)),
                                     pl.BlockSpec((128, 1024), lambda n: (0, 0))],
        out_specs=[pl.BlockSpec((tm, 512), lambda n: (n, 0))] * 3,
        out_shape=[jax.ShapeDtypeStruct((T, 512), BF)] * 3,
        compiler_params=_params(1))(h32, h32, h32, h32, gq, gkv, cosk, sin, wuq2, wkv2)


def _cprep_bwd(h32, gq, gkv, wuq2t, wkv2t, cosk, sin, dq, dk, dv):
    T = h32.shape[0]
    tm = 512 if T % 512 == 0 else T

    def body(cq_ref, ckv_ref, ckr_ref, ckrs_ref, gq_ref, gkv_ref, cos_ref, sin_ref, wuq_ref, wkv_ref,
             dq_ref, dk_ref, dv_ref,
             dcq_ref, dckv_ref, dckr_ref, dckrs_ref, dwuq_ref, dwkv_ref, dgq_ref, dgkv_ref):
        n = pl.program_id(0)

        @pl.when(n == 0)
        def _():
            dwuq_ref[...] = jnp.zeros_like(dwuq_ref)
            dwkv_ref[...] = jnp.zeros_like(dwkv_ref)
            dgq_ref[...] = jnp.zeros_like(dgq_ref)
            dgkv_ref[...] = jnp.zeros_like(dgkv_ref)

        cosk_, sin_ = cos_ref[...], sin_ref[...]
        cosq = cosk_ + (_iota((tm, 128), 1) < 64).astype(F32)
        dkr = jnp.zeros((tm, 128), F32)
        plain, swapped = [], []
        for h in range(4):
            hs = slice(h * 128, (h + 1) * 128)
            dqh = dq_ref[:, hs] * MLA_SCALE
            plain.append((dqh * cosq).astype(BF))
            swapped.append((dqh * sin_).astype(BF))
            dkr = dkr + dk_ref[:, hs]
        dq2 = jnp.concatenate(plain + swapped, axis=1)
        dkv2 = jnp.concatenate([dk_ref[...].astype(BF), dv_ref[...].astype(BF)], axis=1)
        dckr_ref[...] = (dkr * cosk_).astype(BF)
        dckrs_ref[...] = (dkr * sin_).astype(BF)

        cq, gq_ = cq_ref[...], gq_ref[...]
        cqn, rq = _rms_fwd(cq, gq_)
        dwuq_ref[...] += _dot_tn(cqn.astype(BF), dq2)
        dcq, dgt = _rms_bwd(cq, gq_, rq, _dot(dq2, wuq_ref[...]))
        dcq_ref[...] = dcq.astype(BF)
        dgq_ref[...] += jnp.sum(dgt, axis=0, keepdims=True)

        ckv, gkv_ = ckv_ref[...], gkv_ref[...]
        ckvn, rkv = _rms_fwd(ckv, gkv_)
        dwkv_ref[...] += _dot_tn(ckvn.astype(BF), dkv2)
        dckv, dgt2 = _rms_bwd(ckv, gkv_, rkv, _dot(dkv2, wkv_ref[...]))
        dckv_ref[...] = dckv.astype(BF)
        dgkv_ref[...] += jnp.sum(dgt2, axis=0, keepdims=True)

    row = lambda w: pl.BlockSpec((tm, w), lambda n: (n, 0))
    return pl.pallas_call(
        body, name="cprep_bwd", grid=(T // tm,),
        in_specs=_cprep_specs(tm) + [pl.BlockSpec((1024, 256), lambda n: (0, 0)),
                                     pl.BlockSpec((1024, 128), lambda n: (0, 0)),
                                     row(512), row(512), row(512)],
        out_specs=[row(256), row(128), row(128), row(128),
                   pl.BlockSpec((256, 1024), lambda n: (0, 0)), pl.BlockSpec((128, 1024), lambda n: (0, 0)),
                   pl.BlockSpec((1, 256), lambda n: (0, 0)), pl.BlockSpec((1, 128), lambda n: (0, 0))],
        out_shape=[jax.ShapeDtypeStruct((T, 256), BF), jax.ShapeDtypeStruct((T, 128), BF),
                   jax.ShapeDtypeStruct((T, 128), BF), jax.ShapeDtypeStruct((T, 128), BF),
                   jax.ShapeDtypeStruct((256, 1024), F32), jax.ShapeDtypeStruct((128, 1024), F32),
                   jax.ShapeDtypeStruct((1, 256), F32), jax.ShapeDtypeStruct((1, 128), F32)],
        compiler_params=_params(1))(h32, h32, h32, h32, gq, gkv, cosk, sin, wuq2t, wkv2t, dq, dk, dv)


MLA_TILE = 512
MLA_HEADS_PER_STEP = 4


def _causal_mask(t):
    return _iota((t, t), 1) <= _iota((t, t), 0)


def _mla_fwd(q, k, v):
    T = q.shape[0]
    tq = MLA_TILE

    def body(q_ref, k_ref, v_ref, o_ref, lse_ref):
        i = pl.program_id(1)
        mask = _causal_mask(tq)
        heads = [slice(128 * h, 128 * h + 128) for h in range(MLA_HEADS_PER_STEP)]
        qs = [q_ref[:, hs] for hs in heads]

        def step(j, carry, masked):
            rows = pl.ds(pl.multiple_of(j * tq, tq), tq)
            out = []
            for hh, hs in enumerate(heads):
                m, acc = carry[hh]
                s = _dot_nt(qs[hh], k_ref[rows, hs])
                if masked:
                    s = jnp.where(mask, s, NEG)
                m_new = jnp.maximum(m, jnp.max(s, axis=-1, keepdims=True))
                p = jnp.exp((s - m_new).astype(BF))
                acc = jnp.exp(m - m_new) * acc + _dot(p, v_ref[rows, hs])
                out.append((m_new, acc))
            return tuple(out)

        init = ((jnp.full((tq, 1), NEG, F32), jnp.zeros((tq, 128), F32)),) * len(heads)
        carry = lax.fori_loop(0, i // 2, lambda t, c: step(2 * t + 1, step(2 * t, c, False), False), init)
        carry = lax.cond(i % 2 == 1, lambda c: step(i - 1, c, False), lambda c: c, carry)
        carry = step(i, carry, True)
        for hh, hs in enumerate(heads):
            m, acc = carry[hh]
            l = acc[:, 64:65]
            o_ref[:, hs] = acc * (1.0 / l)
            lse_ref[:, hs] = jnp.broadcast_to(m + jnp.log(l), (tq, 128))

    width = 128 * MLA_HEADS_PER_STEP
    blk = pl.BlockSpec((tq, width), lambda h, i: (i, h))
    full = pl.BlockSpec((T, width), lambda h, i: (0, h))
    return pl.pallas_call(
        body, name="mla_fwd", grid=(4 // MLA_HEADS_PER_STEP, T // tq), in_specs=[blk, full, full],
        out_specs=[blk, blk],
        out_shape=[jax.ShapeDtypeStruct((T, 512), F32), jax.ShapeDtypeStruct((T, 512), F32)],
        compiler_params=_params(2))(q, k, v)


def _mla_bwd(q, k, v, o, lse, do):
    T = q.shape[0]
    tq = MLA_TILE

    def body(q_ref, k_ref, v_ref, o_ref, lse_ref, do_ref, dq_ref, dk_ref, dv_ref):
        i = pl.program_id(1)

        @pl.when(i == 0)
        def _():
            dk_ref[...] = jnp.zeros_like(dk_ref)
            dv_ref[...] = jnp.zeros_like(dv_ref)

        heads = [slice(0, 128), slice(128, 256)]
        mask = _causal_mask(tq)
        qs, dobs, dds, lses = [], [], [], []
        for hs in heads:
            do = do_ref[:, hs]
            qs.append(q_ref[:, hs])
            dobs.append(do.astype(BF))
            dds.append(jnp.sum(do * o_ref[:, hs], axis=-1, keepdims=True))
            lses.append(lse_ref[:, hs.start:hs.start + 1])

        def step(j, dqs, masked):
            rows = pl.ds(pl.multiple_of(j * tq, tq), tq)
            out = []
            for hh, hs in enumerate(heads):
                kj, vj = k_ref[rows, hs], v_ref[rows, hs]
                s = _dot_nt(qs[hh], kj)
                if masked:
                    s = jnp.where(mask, s, NEG)
                p = jnp.exp(s - lses[hh])
                ds = (p * (_dot_nt(dobs[hh], vj) - dds[hh])).astype(BF)
                dk_ref[rows, hs] += _dot_tn(ds, qs[hh])
                dv_ref[rows, hs] += _dot_tn(p.astype(BF), dobs[hh])
                out.append(dqs[hh] + _dot(ds, kj))
            return tuple(out)

        dqs = lax.fori_loop(0, i // 2, lambda t, c: step(2 * t + 1, step(2 * t, c, False), False),
                            (jnp.zeros((tq, 128), F32),) * 2)
        dqs = lax.cond(i % 2 == 1, lambda c: step(i - 1, c, False), lambda c: c, dqs)
        dqs = step(i, dqs, True)
        for hh, hs in enumerate(heads):
            dq_ref[:, hs] = dqs[hh]

    blk = pl.BlockSpec((tq, 256), lambda h, i: (i, h))
    full = pl.BlockSpec((T, 256), lambda h, i: (0, h), pipeline_mode=pl.Buffered(1))
    return pl.pallas_call(
        body, name="mla_bwd", grid=(2, T // tq), in_specs=[blk, full, full, blk, blk, blk],
        out_specs=[blk, full, full],
        out_shape=[jax.ShapeDtypeStruct((T, 512), F32)] * 3,
        compiler_params=_params(2))(q, k, v, o, lse, do)


def _sb_tile(qk, rr, strict, masked, upper):
    z2 = qk * (SB_SCALE * LOG2E)
    l1 = jnp.log2(1.0 + jnp.exp2(-jnp.abs(z2)))
    lk = -jnp.maximum(z2, 0.0) - l1
    if masked:
        lk = jnp.where(strict, lk, 0.0)
    after = rr + _dot(lk.astype(BF), upper)
    ll = jnp.minimum(z2, 0.0) - l1
    a = jnp.exp2(ll + after)
    if masked:
        a = jnp.where(strict, a, 0.0)
    return ll, a, jnp.sum(lk, axis=-1, keepdims=True)


SB_TQ, SB_TK = 256, 256
SB_DEAD = -160.0


def _sb_walk(trips, one_step, carry):
    def alive(c):
        t, cr = c
        top = functools.reduce(jnp.maximum, [jnp.max(h[0]) for h in cr])
        return jnp.logical_and(t < trips, top > SB_DEAD)

    def body(c):
        t, cr = c
        return t + 1, one_step(t, cr)

    return lax.while_loop(alive, body, (jnp.int32(0), carry))[1]


def _sb_consts(tq, tk):
    row, col = _iota((tq, tk), 0), _iota((tq, tk), 1)
    strict = [col + d * tk < row for d in range(tq // tk)]
    r2, c2 = _iota((tk, tk), 0), _iota((tk, tk), 1)
    return strict, (r2 > c2).astype(BF), (r2 < c2).astype(BF)


def _sb_fwd(h16):
    T = h16.shape[0]
    tq, tk = SB_TQ, SB_TK
    nd = tq // tk

    def body(q0_ref, q1_ref, k0_ref, k1_ref, v0_ref, v1_ref, o_ref):
        i = pl.program_id(0)
        strict, upper, _ = _sb_consts(tq, tk)
        lane = _iota((tq, 128), 1)
        pairs = [slice(0, 128), slice(128, 256)]
        k_refs, v_refs = (k0_ref, k1_ref), (v0_ref, v1_ref)
        qms = []
        for q_ref in (q0_ref, q1_ref):
            q2 = q_ref[...]
            qms += [jnp.where(lane < 64, q2, jnp.zeros_like(q2)), jnp.where(lane >= 64, q2, jnp.zeros_like(q2))]

        def step(j, carry, d):
            rows = pl.ds(pl.multiple_of(j * tk, tk), tk)
            out = []
            for h in range(4):
                rr, acc = carry[h]
                _, a, rs = _sb_tile(_dot_nt(qms[h], k_refs[h // 2][rows, :]), rr, None if d is None else strict[d],
                                    d is not None, upper)
                out.append((rr + rs, acc + _dot(a.astype(BF), v_refs[h // 2][rows, :])))
            return tuple(out)

        carry = ((jnp.zeros((tq, 1), F32), jnp.zeros((tq, 128), F32)),) * 4
        for d in reversed(range(nd)):
            carry = step(nd * i + d, carry, d)
        carry = _sb_walk(nd * i, lambda t, c: step(nd * i - 1 - t, c, None), carry)
        for p, ps in enumerate(pairs):
            o_ref[:, ps] = jnp.where(lane < 64, carry[2 * p][1], carry[2 * p + 1][1])

    return pl.pallas_call(
        body, name="sb_fwd", grid=(T // tq,),
        in_specs=[pl.BlockSpec((tq, 128), lambda i: (i, C_DQ // 128)),
                  pl.BlockSpec((tq, 128), lambda i: (i, C_DQ // 128 + 1)),
                  pl.BlockSpec((T, 128), lambda i: (0, C_DK // 128)),
                  pl.BlockSpec((T, 128), lambda i: (0, C_DK // 128 + 1)),
                  pl.BlockSpec((T, 128), lambda i: (0, C_DV // 128)),
                  pl.BlockSpec((T, 128), lambda i: (0, C_DV // 128 + 1))],
        out_specs=pl.BlockSpec((tq, 256), lambda i: (i, 0)),
        out_shape=jax.ShapeDtypeStruct((T, 256), F32),
        compiler_params=_params(1))(h16, h16, h16, h16, h16, h16)


def _sb_bwd(h16, yd, dyd):
    T = h16.shape[0]
    tq, tk = SB_TQ, SB_TK
    nd = tq // tk

    def body(q0_ref, q1_ref, k0_ref, k1_ref, v0_ref, v1_ref, o_ref, do_ref, dq_ref, dk_ref, dv_ref):
        i = pl.program_id(0)

        @pl.when(i == 0)
        def _():
            dk_ref[...] = jnp.zeros_like(dk_ref)
            dv_ref[...] = jnp.zeros_like(dv_ref)

        strict, upper, before = _sb_consts(tq, tk)
        lane = _iota((tq, 128), 1)
        lane_k = _iota((tk, 128), 1)
        pairs = [slice(0, 128), slice(128, 256)]
        k_refs, v_refs = (k0_ref, k1_ref), (v0_ref, v1_ref)
        q2s, dob2s, qms, doms, dds = [], [], [], [], []
        for p, q_ref in enumerate((q0_ref, q1_ref)):
            q2 = q_ref[...]
            dob2 = do_ref[:, pairs[p]].astype(BF)
            doo = dob2.astype(F32) * o_ref[:, pairs[p]]
            q2s.append(q2)
            dob2s.append(dob2)
            for mine in (lane < 64, lane >= 64):
                qms.append(jnp.where(mine, q2, jnp.zeros_like(q2)))
                doms.append(jnp.where(mine, dob2, jnp.zeros_like(dob2)))
                dds.append(jnp.sum(jnp.where(mine, doo, 0.0), axis=-1, keepdims=True))

        def step(j, carry, d):
            rows = pl.ds(pl.multiple_of(j * tk, tk), tk)
            out, dks, dvs = [], [], []
            for h in range(4):
                kj, vj = k_refs[h // 2][rows, :], v_refs[h // 2][rows, :]
                rr, sg, dq = carry[h]
                ll, a, rs = _sb_tile(_dot_nt(qms[h], kj), rr, None if d is None else strict[d], d is not None, upper)
                ab = a.astype(BF)
                g = _dot_nt(doms[h], vj) * ab.astype(F32)
                gs = jnp.sum(g, axis=-1, keepdims=True)
                pre = (dds[h] - sg - gs) + _dot(g.astype(BF), before)
                dz = g - jnp.exp2(ll) * (g + pre)
                if d is not None:
                    dz = jnp.where(strict[d], dz, 0.0)
                dzb = dz.astype(BF)
                dks.append(_dot_tn(dzb, q2s[h // 2]))
                dvs.append(_dot_tn(ab, dob2s[h // 2]))
                out.append((rr + rs, sg + gs, dq + _dot(dzb, kj)))
            for p, ps in enumerate(pairs):
                dk_ref[rows, ps] += jnp.where(lane_k < 64, dks[2 * p], dks[2 * p + 1]) * SB_SCALE
                dv_ref[rows, ps] += jnp.where(lane_k < 64, dvs[2 * p], dvs[2 * p + 1])
            return tuple(out)

        zero = jnp.zeros((tq, 1), F32)
        carry = ((zero, zero, jnp.zeros((tq, 128), F32)),) * 4
        for d in reversed(range(nd)):
            carry = step(nd * i + d, carry, d)
        carry = _sb_walk(nd * i, lambda t, c: step(nd * i - 1 - t, c, None), carry)
        for p, ps in enumerate(pairs):
            dq_ref[:, ps] = jnp.where(lane < 64, carry[2 * p][2], carry[2 * p + 1][2]) * SB_SCALE

    blk = lambda c: pl.BlockSpec((tq, 128), lambda i: (i, c // 128))
    full = lambda c: pl.BlockSpec((T, 128), lambda i: (0, c // 128))
    row = pl.BlockSpec((tq, 256), lambda i: (i, 0))
    acc = pl.BlockSpec((T, 256), lambda i: (0, 0))
    return pl.pallas_call(
        body, name="sb_bwd", grid=(T // tq,),
        in_specs=[blk(C_DQ), blk(C_DQ + 128), full(C_DK), full(C_DK + 128), full(C_DV), full(C_DV + 128), row, row],
        out_specs=[row, acc, acc],
        out_shape=[jax.ShapeDtypeStruct((T, 256), F32)] * 3,
        compiler_params=_params(1))(h16, h16, h16, h16, h16, h16, yd, dyd)


def _compact_c(ycp):
    return jnp.concatenate([ycp[:, h * 128:h * 128 + 64] for h in range(4)], axis=1)


def _post_fwd(ya, yb, ycp, yd, h32, ggrp, wout, gpost, x, tgt=None):
    T = x.shape[0]
    tm = 256
    last = tgt is not None

    def body(*refs):
        ya_ref, yb_ref, yc_ref, yd_ref, gate_ref, gg_ref, w_ref, gp_ref, x_ref = refs[:9]
        if last:
            t_ref, xn_ref, ym_ref, o_ref, sq_ref = refs[9:]
        else:
            xn_ref, ym_ref, o_ref = refs[9:]
        ys = [ya_ref[...], yb_ref[...], _compact_c(yc_ref[...]), yd_ref[...]]
        gate = gate_ref[...]
        sil = gate * (1.0 / (1.0 + jnp.exp(-gate)))
        parts = []
        for gi in range(4):
            ng, _ = _rms_fwd(ys[gi], gg_ref[:, gi * 256:(gi + 1) * 256])
            parts.append(ng * sil[:, gi * 256:(gi + 1) * 256])
        ym = jnp.concatenate(parts, axis=1).astype(BF)
        ym_ref[...] = ym
        o = _dot(ym, w_ref[...])
        o_ref[...] = o
        on, _ = _rms_fwd(o, gp_ref[...])
        if last:
            @pl.when(pl.program_id(0) == 0)
            def _():
                sq_ref[...] = jnp.zeros_like(sq_ref)

            d = (x_ref[...] + on) - t_ref[...]
            sq_ref[...] += jnp.sum(d * d, axis=0, keepdims=True)
            xn_ref[...] = d * (1.0 / D_MODEL)
        else:
            xn_ref[...] = x_ref[...] + on

    row = lambda w: pl.BlockSpec((tm, w), lambda n: (n, 0))
    vec = pl.BlockSpec((1, 1024), lambda n: (0, 0))
    return pl.pallas_call(
        body, name="post_fwd", grid=(T // tm,),
        in_specs=[row(256), row(256), row(512), row(256), pl.BlockSpec((tm, 1024), lambda n: (n, C_GATE // 1024)),
                  vec, pl.BlockSpec((1024, 1024), lambda n: (0, 0)), vec, row(1024)] + ([row(1024)] if last else []),
        out_specs=[row(1024), row(1024), row(1024)] + ([vec] if last else []),
        out_shape=[jax.ShapeDtypeStruct((T, 1024), F32), jax.ShapeDtypeStruct((T, 1024), BF),
                   jax.ShapeDtypeStruct((T, 1024), F32)] + ([jax.ShapeDtypeStruct((1, 1024), F32)] if last else []),
        compiler_params=_params(1))(*([ya, yb, ycp, yd, h32, ggrp, wout, gpost, x] + ([tgt] if last else [])))


def _post_bwd(dx, o, gpost, woutt, ya, yb, ycp, yd, h32, ggrp, ym):
    T = dx.shape[0]
    tm = 256

    def body(dx_ref, o_ref, gp_ref, w_ref, ya_ref, yb_ref, yc_ref, yd_ref, gate_ref, gg_ref, ym_ref,
             dw_ref, dya_ref, dyb_ref, dyc_ref, dyd_ref, dgate_ref, dgp_ref, dgg_ref):
        n = pl.program_id(0)

        @pl.when(n == 0)
        def _():
            dw_ref[...] = jnp.zeros_like(dw_ref)
            dgp_ref[...] = jnp.zeros_like(dgp_ref)
            dgg_ref[...] = jnp.zeros_like(dgg_ref)

        ov, gp = o_ref[...], gp_ref[...]
        _, ro = _rms_fwd(ov, gp)
        do, dgt = _rms_bwd(ov, gp, ro, dx_ref[...])
        dgp_ref[...] += jnp.sum(dgt, axis=0, keepdims=True)
        dob = do.astype(BF)
        dw_ref[...] += _dot_tn(ym_ref[...], dob)
        dym = _dot(dob, w_ref[...])
        gate = gate_ref[...]
        sg = 1.0 / (1.0 + jnp.exp(-gate))
        sil = gate * sg
        dsil = sg * (1.0 + gate * (1.0 - sg))
        ys = [ya_ref[...], yb_ref[...], _compact_c(yc_ref[...]), yd_ref[...]]
        dys = []
        for gi in range(4):
            gs = slice(gi * 256, (gi + 1) * 256)
            gg = gg_ref[:, gs]
            ng, rg = _rms_fwd(ys[gi], gg)
            dgate_ref[:, gs] = (dym[:, gs] * ng * dsil[:, gs]).astype(BF)
            dy, dgt2 = _rms_bwd(ys[gi], gg, rg, dym[:, gs] * sil[:, gs])
            dgg_ref[:, gs] += jnp.sum(dgt2, axis=0, keepdims=True)
            dys.append(dy)
        dya_ref[...] = dys[0]
        dyb_ref[...] = dys[1]
        dyd_ref[...] = dys[3]
        z64 = jnp.zeros((tm, 64), F32)
        dyc_ref[...] = jnp.concatenate(
            [piece for h in range(4) for piece in (dys[2][:, h * 64:(h + 1) * 64], z64)], axis=1)

    row = lambda w: pl.BlockSpec((tm, w), lambda n: (n, 0))
    vec = pl.BlockSpec((1, 1024), lambda n: (0, 0))
    return pl.pallas_call(
        body, name="post_bwd", grid=(T // tm,),
        in_specs=[row(1024), row(1024), vec, pl.BlockSpec((1024, 1024), lambda n: (0, 0)),
                  row(256), row(256), row(512), row(256),
                  pl.BlockSpec((tm, 1024), lambda n: (n, C_GATE // 1024)), vec, row(1024)],
        out_specs=[pl.BlockSpec((1024, 1024), lambda n: (0, 0)), row(256), row(256), row(512), row(256), row(1024),
                   vec, vec],
        out_shape=[jax.ShapeDtypeStruct((1024, 1024), F32), jax.ShapeDtypeStruct((T, 256), F32),
                   jax.ShapeDtypeStruct((T, 256), F32), jax.ShapeDtypeStruct((T, 512), F32),
                   jax.ShapeDtypeStruct((T, 256), F32), jax.ShapeDtypeStruct((T, 1024), BF),
                   jax.ShapeDtypeStruct((1, 1024), F32), jax.ShapeDtypeStruct((1, 1024), F32)],
        compiler_params=_params(1))(dx, o, gpost, woutt, ya, yb, ycp, yd, h32, ggrp, ym)


def _swap_rows32(a):
    return jnp.concatenate([a[16:32], a[0:16]], axis=0)


def _pad_w_uq(w):
    z = lambda n: jnp.zeros((w.shape[0], n), w.dtype)
    a = [p for h in range(4) for p in (w[:, 96 * h:96 * h + 96], z(32))]
    b = [p for h in range(4) for p in (z(64), _swap32(w[:, 96 * h + 64:96 * h + 96]), z(32))]
    return jnp.concatenate(a + b, axis=1)


def _unpad_w_uq(d):
    out = []
    for h in range(4):
        out.append(d[:, 128 * h:128 * h + 64])
        out.append(d[:, 128 * h + 64:128 * h + 96] + _swap32(d[:, 512 + 128 * h + 64:512 + 128 * h + 96]))
    return jnp.concatenate(out, axis=1)


def _pad_w_ukv(w):
    z = jnp.zeros((w.shape[0], 64), w.dtype)
    a = [p for h in range(4) for p in (w[:, 128 * h:128 * h + 64], z)]
    b = [p for h in range(4) for p in (w[:, 128 * h + 64:128 * h + 128], z)]
    return jnp.concatenate(a + b, axis=1)


def _unpad_w_ukv(d):
    return jnp.concatenate([p for h in range(4) for p in (d[:, 128 * h:128 * h + 64],
                                                          d[:, 512 + 128 * h:512 + 128 * h + 64])], axis=1)


def _rope_tables(pos):
    freqs = 10000.0 ** (-jnp.arange(16, dtype=F32) / 16)
    ang = pos.astype(F32)[:, None] * freqs
    c, s = jnp.cos(ang), jnp.sin(ang)
    z = lambda n: jnp.zeros((pos.shape[0], n), F32)
    return (jnp.concatenate([z(64), c, c, z(32)], axis=1), jnp.concatenate([z(64), -s, s, z(32)], axis=1))


def _layer_weights(W, l):
    wuq2 = _pad_w_uq(W["mla_w_uq"][l])
    wkv2 = _pad_w_ukv(W["mla_w_ukv"][l])
    wout = W["w_out"][l]
    cw = jnp.concatenate([W["conv_w"][l].astype(F32), jnp.zeros((5, 256), F32)], axis=0)
    return dict(
        wpt=W["wpt"][l], wuq2=wuq2.astype(BF), wuq2t=wuq2.T.astype(BF),
        wkv2=wkv2.astype(BF), wkv2t=wkv2.T.astype(BF), wout=wout.astype(BF), woutt=wout.T.astype(BF),
        cw=cw, cb=W["conv_b"][l][None, :], sinks=W["attn_sinks"][l],
        gpre=W["norm_pre"][l][None, :], gq=W["mla_q_norm"][l][None, :], gkv=W["mla_kv_norm"][l][None, :],
        ggrp=W["group_norm"][l][None, :], gpost=W["norm_post"][l][None, :])


def _local_step(x, pos, W, tgt):
    cosk, sin = _rope_tables(pos)
    saved = []
    for l in range(DEPTH):
        lw = _layer_weights(W, l)
        h32, h16, xn = _inproj_fwd(x, lw["gpre"], lw["wpt"])
        ya = _swa_fwd(h16, lw["sinks"])
        yb = _conv_fwd(h32, lw["cw"], lw["cb"])
        qc, kc, vc = _cprep_fwd(h32, lw["gq"], lw["gkv"], lw["wuq2"], lw["wkv2"], cosk, sin)
        ycp, lse = _mla_fwd(qc, kc, vc)
        yd = _sb_fwd(h16)
        if l < DEPTH - 1:
            x_new, ym, o = _post_fwd(ya, yb, ycp, yd, h32, lw["ggrp"], lw["wout"], lw["gpost"], x)
        else:
            dx, ym, o, sq = _post_fwd(ya, yb, ycp, yd, h32, lw["ggrp"], lw["wout"], lw["gpost"], x, tgt)
        saved.append(dict(lw=lw, x=x, h32=h32, h16=h16, xn=xn, ya=ya, yb=yb, qc=qc, kc=kc, vc=vc, ycp=ycp,
                          lse=lse, yd=yd, ym=ym, o=o))
        if l < DEPTH - 1:
            x = x_new

    grads = {k: [None] * DEPTH for k in ("norm_pre", "w_in_pt", "attn_sinks", "conv_w", "conv_b", "mla_q_norm",
                                         "mla_w_uq", "mla_kv_norm", "mla_w_ukv", "group_norm", "w_out",
                                         "norm_post")}
    for l in reversed(range(DEPTH)):
        s = saved[l]
        lw = s["lw"]
        dwout, dya, dyb, dycp, dyd, dgate, dgpost, dggrp = _post_bwd(
            dx, s["o"], lw["gpost"], lw["woutt"], s["ya"], s["yb"], s["ycp"], s["yd"], s["h32"], lw["ggrp"], s["ym"])
        grads["norm_post"][l] = dgpost[0]
        grads["group_norm"][l] = dggrp[0]
        grads["w_out"][l] = dwout
        sdq, sdk, sdv = _sb_bwd(s["h16"], s["yd"], dyd)
        mdq, mdk, mdv = _mla_bwd(s["qc"], s["kc"], s["vc"], s["ycp"], s["lse"], dycp)
        dcq, dckv, dckr, dckrs, dwuq2, dwkv2, dgq, dgkv = _cprep_bwd(
            s["h32"], lw["gq"], lw["gkv"], lw["wuq2t"], lw["wkv2t"], cosk, sin, mdq, mdk, mdv)
        grads["mla_q_norm"][l] = dgq[0]
        grads["mla_kv_norm"][l] = dgkv[0]
        grads["mla_w_uq"][l] = _unpad_w_uq(dwuq2)
        grads["mla_w_ukv"][l] = _unpad_w_ukv(dwkv2)
        dbb, dbc, dbx, dcw = _conv_bwd(s["h32"], lw["cw"], lw["cb"], dyb)
        grads["conv_w"][l] = dcw[0:3]
        grads["conv_b"][l] = dcw[3]
        adq, adk, adv, dsk = _swa_bwd(s["h16"], lw["sinks"], dya)
        grads["attn_sinks"][l] = dsk[0, 0:4]
        parts = [dgate, adq, adk, adv, dbb, dbc, dbx, dcq, dckv, dckr, dckrs, sdq, sdk, sdv]
        dwp, dx, dgpre = _inproj_bwd(parts, lw["wpt"], s["x"], s["xn"], lw["gpre"], dx)
        grads["w_in_pt"][l] = dwp.T
        grads["norm_pre"][l] = dgpre[0]
    return sq, dx, grads


SMALL_SHARDED = ("conv_w", "mla_w_uq", "mla_w_ukv")
REPLICATED = ("norm_pre", "attn_sinks", "conv_b", "mla_q_norm", "mla_kv_norm", "group_norm", "norm_post")
ORDER = ("norm_pre", "w_in", "attn_sinks", "conv_w", "conv_b", "mla_q_norm", "mla_w_uq", "mla_kv_norm",
         "mla_w_ukv", "group_norm", "w_out", "norm_post")
W_IN_COLS = 436
W_IN_WIN = 440
SMALL_ROWS = 48


def _pack_small(arrs, dtype):
    flat = jnp.concatenate([a.reshape(-1).astype(dtype) for a in arrs])
    flat = jnp.concatenate([flat, jnp.zeros((SMALL_ROWS * D_MODEL - flat.shape[0],), dtype)])
    return flat.reshape(SMALL_ROWS, D_MODEL)


TAIL_ROW0 = DEPTH * W_IN_WIN


def _pack_state(ps):
    k = len(ps)
    wout = jnp.stack([p["w_out"] for p in ps]).reshape(k, DEPTH * 128, D_MODEL)
    flat = jnp.stack([jnp.concatenate([p[n].reshape(-1) for n in SMALL_SHARDED + REPLICATED]) for p in ps])
    small = jnp.pad(flat, ((0, 0), (0, SMALL_ROWS * D_MODEL - flat.shape[1]))).reshape(k, SMALL_ROWS, D_MODEL)
    return jnp.concatenate([wout, small], axis=1)


def _unpack_state(buf, p):
    k = buf.shape[0]
    out = {"w_out": buf[:, 0:DEPTH * 128].reshape(k, DEPTH, 128, D_MODEL)}
    flat = buf[:, DEPTH * 128:].reshape(k, SMALL_ROWS * D_MODEL)
    off = 0
    for n in SMALL_SHARDED + REPLICATED:
        size = int(np.prod(p[n].shape))
        out[n] = flat[:, off:off + size].reshape((k,) + p[n].shape)
        off += size
    return out


def _rows_of_w_in_t(lo, hi, padded, kr):
    segs = ((0, 1664, padded, C_AQ), (1664, 1696, kr, 0), (1696, 2464, padded, C_DQ), (2464, 3488, padded, C_GATE))
    out = []
    for s0, s1, src, base in segs:
        a, b = max(lo, s0), min(hi, s1)
        if a < b:
            out.append(src[base + a - s0:base + b - s0])
    return out


def _me():
    return lax.axis_index("x"), lax.axis_index("y"), lax.axis_index("c")


def _all_gather(block):
    R, C = block.shape

    def body(src_ref, out_ref, send_sems, recv_sems, local_sem):
        x, y, c = _me()
        me, sibling = (x, y, c), (x, y, 1 - c)
        chips = [(1 - x, y), (x, 1 - y), (1 - x, 1 - y)]

        def slot(px, py, pc):
            return out_ref.at[4 * px + 2 * py + pc]

        def copy(k, block, to, src=None):
            return pltpu.make_async_remote_copy(
                src_ref=slot(*block) if src is None else src, dst_ref=slot(*block), send_sem=send_sems.at[k],
                recv_sem=recv_sems.at[k], device_id=to, device_id_type=MESH)

        mine = pltpu.make_async_copy(src_ref, slot(*me), local_sem)
        mine.start()
        first = [copy(0, me, sibling, src=src_ref)]
        first += [copy(1 + j, me, (*chip, c), src=src_ref) for j, chip in enumerate(chips)]
        for cp in first:
            cp.start()
        passed = [copy(4 + j, (*chip, c), sibling) for j, chip in enumerate(chips)]
        for j, chip in enumerate(chips):
            copy(1 + j, (*chip, c), me).wait_recv()
            passed[j].start()
        copy(0, sibling, me).wait_recv()
        for j, chip in enumerate(chips):
            copy(4 + j, (*chip, 1 - c), me).wait_recv()
        for cp in first + passed:
            cp.wait_send()
        mine.wait()

    return pl.pallas_call(
        body, name="all_gather", out_shape=jax.ShapeDtypeStruct((N_DEV, R, C), block.dtype),
        in_specs=[pl.BlockSpec(memory_space=pl.ANY)], out_specs=pl.BlockSpec(memory_space=pl.ANY),
        scratch_shapes=[pltpu.SemaphoreType.DMA((N_DEV - 1,)), pltpu.SemaphoreType.DMA((N_DEV - 1,)),
                        pltpu.SemaphoreType.DMA])(block)


N_CHIP = 4


def _sibling_swap(blocks):
    _, R, C = blocks.shape

    def body(src_ref, out_ref, send_sems, recv_sems):
        x, y, c = _me()
        copies = [pltpu.make_async_remote_copy(
            src_ref=src_ref.at[2 * j + 1 - c], dst_ref=out_ref.at[j], send_sem=send_sems.at[j],
            recv_sem=recv_sems.at[j], device_id=(x, y, 1 - c), device_id_type=MESH) for j in range(N_CHIP)]
        for cp in copies:
            cp.start()
        for cp in copies:
            cp.wait()

    return pl.pallas_call(
        body, name="sibling_swap", out_shape=jax.ShapeDtypeStruct((N_CHIP, R, C), blocks.dtype),
        in_specs=[pl.BlockSpec(memory_space=pl.ANY)], out_specs=pl.BlockSpec(memory_space=pl.ANY),
        scratch_shapes=[pltpu.SemaphoreType.DMA((N_CHIP,)), pltpu.SemaphoreType.DMA((N_CHIP,))])(blocks)


def _pair_sum(a, b):
    n, R, C = a.shape
    tr = 592 if R % 592 == 0 else R

    def body(a_ref, b_ref, o_ref):
        o_ref[...] = (a_ref[...].astype(F32) + b_ref[...].astype(F32)).astype(BF)

    spec = pl.BlockSpec((1, tr, C), lambda j, r: (j, r, 0))
    return pl.pallas_call(body, name="pair_sum", grid=(n, R // tr), in_specs=[spec, spec], out_specs=spec,
                          out_shape=jax.ShapeDtypeStruct(a.shape, BF), compiler_params=_params(2))(a, b)


def _chip_exchange(sums):
    _, R, C = sums.shape

    def body(src_ref, out_ref, send_sems, recv_sems, local_sem):
        x, y, c = _me()
        here = 2 * x + y
        mine = pltpu.make_async_copy(src_ref.at[here], out_ref.at[here], local_sem)
        mine.start()
        copies = []
        for k in range(1, N_CHIP):
            px, py = x ^ (k >> 1), y ^ (k & 1)
            copies.append(pltpu.make_async_remote_copy(
                src_ref=src_ref.at[2 * px + py], dst_ref=out_ref.at[here], send_sem=send_sems.at[k - 1],
                recv_sem=recv_sems.at[k - 1], device_id=(px, py, c), device_id_type=MESH))
        for cp in copies:
            cp.start()
        for cp in copies:
            cp.wait()
        mine.wait()

    return pl.pallas_call(
        body, name="chip_exchange", out_shape=jax.ShapeDtypeStruct((N_CHIP, R, C), sums.dtype),
        in_specs=[pl.BlockSpec(memory_space=pl.ANY)], out_specs=pl.BlockSpec(memory_space=pl.ANY),
        scratch_shapes=[pltpu.SemaphoreType.DMA((N_CHIP - 1,)), pltpu.SemaphoreType.DMA((N_CHIP - 1,)),
                        pltpu.SemaphoreType.DMA])(sums)


def _adamw_update(g, w, m, v):
    m_ = ADAM_B1 * m + (1.0 - ADAM_B1) * g
    v_ = ADAM_B2 * v + (1.0 - ADAM_B2) * (g * g)
    m_hat = m_ / (1.0 - ADAM_B1 ** ADAM_STEP)
    v_hat = v_ / (1.0 - ADAM_B2 ** ADAM_STEP)
    return -ADAM_LR * (m_hat / (jnp.sqrt(v_hat) + ADAM_EPS) + ADAM_WD * w), m_, v_


def _adamw(parts, state):
    _, R, C = state.shape
    n_parts = parts.shape[0]
    tr = 16
    assert R % tr == 0 and TAIL_ROW0 % tr == 0

    def body(p_ref, s_ref, o_ref):
        g = p_ref[0].astype(F32)
        for k in range(1, n_parts):
            g = g + p_ref[k].astype(F32)
        o_ref[0] = g
        o_ref[1], o_ref[2], o_ref[3] = _adamw_update(g, s_ref[0], s_ref[1], s_ref[2])

    return pl.pallas_call(
        body, name="adamw", grid=(R // tr,),
        in_specs=[pl.BlockSpec((n_parts, tr, C), lambda n: (0, n + TAIL_ROW0 // tr, 0)),
                  pl.BlockSpec((3, tr, C), lambda n: (0, n, 0))],
        out_specs=pl.BlockSpec((4, tr, C), lambda n: (0, n, 0)), out_shape=jax.ShapeDtypeStruct((4, R, C), F32),
        compiler_params=_params(1))(parts, state)


def _adamw_w_in(parts, w, m, v, core):
    n_parts = parts.shape[0]
    tc = 256

    def body(core_ref, p_ref, w_ref, m_ref, v_ref, o_ref):
        g_t = p_ref[0].astype(F32)
        for k in range(1, n_parts):
            g_t = g_t + p_ref[k].astype(F32)
        g_t = jnp.concatenate([g_t, jnp.zeros((512 - W_IN_WIN, tc), F32)], axis=0).T
        g = jnp.where(core_ref[0] == 0, g_t[:, 0:W_IN_COLS], g_t[:, W_IN_WIN - W_IN_COLS:W_IN_WIN])
        o_ref[0, 0] = g
        o_ref[1, 0], o_ref[2, 0], o_ref[3, 0] = _adamw_update(g, w_ref[0], m_ref[0], v_ref[0])

    nat = pl.BlockSpec((1, tc, W_IN_COLS), lambda l, j: (l, j, 0))
    return pl.pallas_call(
        body, name="adamw_w_in", grid=(DEPTH, D_MODEL // tc),
        in_specs=[pl.BlockSpec(memory_space=pltpu.SMEM),
                  pl.BlockSpec((n_parts, W_IN_WIN, tc), lambda l, j: (0, l, j)), nat, nat, nat],
        out_specs=pl.BlockSpec((4, 1, tc, W_IN_COLS), lambda l, j: (0, l, j, 0)),
        out_shape=jax.ShapeDtypeStruct((4, DEPTH, D_MODEL, W_IN_COLS), F32),
        compiler_params=_params(2))(core, parts, w, m, v)


def kernel(x, positions, norm_pre, w_in, attn_sinks, conv_w, conv_b, mla_q_norm, mla_w_uq, mla_kv_norm, mla_w_ukv, group_norm, w_out, norm_post, loss_target, m_norm_pre, m_w_in, m_attn_sinks, m_conv_w, m_conv_b, m_mla_q_norm, m_mla_w_uq, m_mla_kv_norm, m_mla_w_ukv, m_group_norm, m_w_out, m_norm_post, v_norm_pre, v_w_in, v_attn_sinks, v_conv_w, v_conv_b, v_mla_q_norm, v_mla_w_uq, v_mla_kv_norm, v_mla_w_ukv, v_group_norm, v_w_out, v_norm_post):
    local = dict(norm_pre=norm_pre, w_in=w_in, attn_sinks=attn_sinks, conv_w=conv_w, conv_b=conv_b,
                 mla_q_norm=mla_q_norm, mla_w_uq=mla_w_uq, mla_kv_norm=mla_kv_norm, mla_w_ukv=mla_w_ukv,
                 group_norm=group_norm, w_out=w_out, norm_post=norm_post)
    mom = dict(norm_pre=m_norm_pre, w_in=m_w_in, attn_sinks=m_attn_sinks, conv_w=m_conv_w, conv_b=m_conv_b,
               mla_q_norm=m_mla_q_norm, mla_w_uq=m_mla_w_uq, mla_kv_norm=m_mla_kv_norm, mla_w_ukv=m_mla_w_ukv,
               group_norm=m_group_norm, w_out=m_w_out, norm_post=m_norm_post)
    vel = dict(norm_pre=v_norm_pre, w_in=v_w_in, attn_sinks=v_attn_sinks, conv_w=v_conv_w, conv_b=v_conv_b,
               mla_q_norm=v_mla_q_norm, mla_w_uq=v_mla_w_uq, mla_kv_norm=v_mla_kv_norm, mla_w_ukv=v_mla_w_ukv,
               group_norm=v_group_norm, w_out=v_w_out, norm_post=v_norm_post)

    c = lax.axis_index("c")

    tile = 16
    slot_rows = 464
    shift = 8 * lax.axis_index("y") + 4 * c
    wt = lax.dynamic_update_slice(jnp.zeros((DEPTH, slot_rows, D_MODEL), BF),
                                  jnp.transpose(w_in, (0, 2, 1)).astype(BF), (0, shift, 0))
    payload = jnp.concatenate([wt.reshape(DEPTH * slot_rows, D_MODEL),
                               w_out.astype(BF).reshape(DEPTH * 128, D_MODEL),
                               _pack_small([local[n] for n in SMALL_SHARDED], BF)], axis=0)
    gathered = _all_gather(payload)
    W = {n: local[n] for n in REPLICATED}

    def nat_rows(l, lo, hi):
        def piece(d, r0, r1):
            base = slot_rows * l - (W_IN_COLS * d) // tile * tile
            return gathered[d, base + r0:base + r1]

        out, run = [], None
        for r0 in range(lo, hi, tile):
            d0, d1 = r0 // W_IN_COLS, (r0 + tile - 1) // W_IN_COLS
            if d0 == d1 and run is not None and run[0] == d0:
                run = (d0, run[1], r0 + tile)
                continue
            if run is not None:
                out.append(piece(*run))
                run = None
            if d0 == d1:
                run = (d0, r0, r0 + tile)
            else:
                out.append(piece(d0, r0, r0 + tile) + piece(d1, r0, r0 + tile))
        if run is not None:
            out.append(piece(*run))
        return out

    z = lambda n: [jnp.zeros((n, D_MODEL), BF)]
    W["wpt"] = [jnp.concatenate(nat_rows(l, 2464, 3488) + nat_rows(l, 0, 1664) + z(64) + nat_rows(l, 1664, 1696)
                                + z(96) + nat_rows(l, 1680, 1696) + nat_rows(l, 1664, 1680) + z(32)
                                + nat_rows(l, 1696, 2464), axis=0) for l in range(DEPTH)]
    wo0 = DEPTH * slot_rows
    W["w_out"] = gathered[:, wo0:wo0 + DEPTH * 128].reshape(N_DEV, DEPTH, 128, D_MODEL).transpose(1, 0, 2, 3).reshape(
        DEPTH, D_MODEL, D_MODEL)
    flat = gathered[:, wo0 + DEPTH * 128:].reshape(N_DEV, SMALL_ROWS * D_MODEL)
    off = 0
    for n in SMALL_SHARDED:
        depth, rows, width = local[n].shape
        size = depth * rows * width
        W[n] = flat[:, off:off + size].reshape(N_DEV, depth, rows, width).transpose(1, 2, 0, 3).reshape(
            depth, rows, N_DEV * width)
        off += size

    sq, grad_x, g = _local_step(x[0], positions[0], W, loss_target[0])
    loss = lax.psum(0.5 / D_MODEL * jnp.sum(sq), ("x", "y", "c"))

    cols = []
    for n in SMALL_SHARDED:
        depth, rows, width = local[n].shape
        cols.append(jnp.stack(g[n]).reshape(depth, rows, N_DEV, width).transpose(2, 0, 1, 3).reshape(N_DEV, -1))
    rep = jnp.concatenate([a.reshape(-1) for n in REPLICATED for a in g[n]])
    cols.append(jnp.broadcast_to(rep[None], (N_DEV, rep.shape[0])))
    small = jnp.concatenate(cols, axis=1)
    small = jnp.pad(small, ((0, 0), (0, SMALL_ROWS * D_MODEL - small.shape[1]))).reshape(N_DEV, SMALL_ROWS, D_MODEL)
    krs = [p[C_CKR + 64:C_CKR + 96] + _swap_rows32(p[C_CKRS + 64:C_CKRS + 96]) for p in g["w_in_pt"]]
    pieces = []
    for d in range(N_DEV):
        lo = W_IN_COLS * d // 8 * 8
        for l in range(DEPTH):
            pieces += _rows_of_w_in_t(lo, lo + W_IN_WIN, g["w_in_pt"][l], krs[l])
        pieces += [g["w_out"][l][128 * d:128 * (d + 1)] for l in range(DEPTH)]
        pieces.append(small[d])
    blocks = jnp.concatenate(pieces, axis=0).astype(BF).reshape(N_DEV, -1, D_MODEL)
    mine = lax.dynamic_index_in_dim(blocks.reshape(N_CHIP, 2, -1, D_MODEL), c, axis=1, keepdims=False)
    received = _chip_exchange(_pair_sum(mine, _sibling_swap(blocks)))

    out = _unpack_state(_adamw(received, _pack_state([local, mom, vel])), local)
    out["w_in"] = _adamw_w_in(received, w_in, m_w_in, v_w_in, c.astype(jnp.int32).reshape(1))
    return (loss, grad_x[None], *[out[n][t] for t in range(4) for n in ORDER])
```

```python
import functools

import jax
import jax.numpy as jnp
import numpy as np
from jax import lax
from jax.experimental import pallas as pl
from jax.experimental.pallas import tpu as pltpu

F32 = jnp.float32
BF = jnp.bfloat16
MESH = pl.DeviceIdType.MESH

D_MODEL = 1024
DEPTH = 2
EPS = 1e-6
N_DEV = 8
VMEM_LIMIT = 56 * 1024 * 1024
NEG = -1e30
MLA_SCALE = 96.0 ** -0.5
SB_SCALE = 0.125
LOG2E = 1.4426950408889634

NP = 3712
C_GATE = 0
C_AQ = 1024
C_AK = 1280
C_AV = 1408
C_BB = 1536
C_BC = 1792
C_BX = 2048
C_CQ = 2304
C_CKV = 2560
C_CKR = 2688
C_CKRS = 2816
C_DQ = 2944
C_DK = 3200
C_DV = 3456
C_END = 3712

def _swap32(a):
    return jnp.concatenate([a[:, 16:32], a[:, 0:16]], axis=1)

ADAM_LR, ADAM_B1, ADAM_B2, ADAM_EPS, ADAM_WD, ADAM_STEP = 0.001, 0.9, 0.999, 1e-08, 0.01, 10


def _dot(a, b):
    return jnp.dot(a, b, preferred_element_type=F32)


def _dot_nt(a, b):
    return lax.dot_general(a, b, (((1,), (1,)), ((), ())), preferred_element_type=F32)


def _dot_tn(a, b):
    return lax.dot_general(a, b, (((0,), (0,)), ((), ())), preferred_element_type=F32)


def _params(n_grid):
    return pltpu.CompilerParams(dimension_semantics=("arbitrary",) * n_grid, vmem_limit_bytes=VMEM_LIMIT)


def _rms_fwd(x, g):
    r = lax.rsqrt(jnp.mean(x * x, axis=-1, keepdims=True) + EPS)
    return (x * r) * g, r


def _rms_bwd(x, g, r, dy, width=None):
    n = x.shape[-1] if width is None else width
    u = dy * g
    dx = r * u - x * (r * r * r) * (jnp.sum(x * u, axis=-1, keepdims=True) / n)
    return dx, dy * (x * r)


def _iota(shape, axis):
    return lax.broadcasted_iota(jnp.int32, shape, axis)


def _inproj_fwd(x, g, wpt):
    T = x.shape[0]
    tm = 256

    def body(x_ref, g_ref, w_ref, h32_ref, h16_ref, xn_ref):
        xn, _ = _rms_fwd(x_ref[...], g_ref[...])
        xn = xn.astype(BF)
        xn_ref[...] = xn
        h = _dot_nt(xn, w_ref[...])
        h32_ref[...] = h
        h16_ref[...] = h.astype(BF)

    return pl.pallas_call(
        body, name="inproj_fwd", grid=(T // tm,),
        in_specs=[pl.BlockSpec((tm, D_MODEL), lambda n: (n, 0)),
                  pl.BlockSpec((1, D_MODEL), lambda n: (0, 0)),
                  pl.BlockSpec((NP, D_MODEL), lambda n: (0, 0))],
        out_specs=[pl.BlockSpec((tm, NP), lambda n: (n, 0)),
                   pl.BlockSpec((tm, NP), lambda n: (n, 0)),
                   pl.BlockSpec((tm, D_MODEL), lambda n: (n, 0))],
        out_shape=[jax.ShapeDtypeStruct((T, NP), F32), jax.ShapeDtypeStruct((T, NP), BF),
                   jax.ShapeDtypeStruct((T, D_MODEL), BF)],
        compiler_params=_params(1))(x, g, wpt)


def _inproj_bwd(parts, wpt, x, xn, g, dxo):
    T = x.shape[0]
    tm = 256
    np_ = len(parts)
    chunks = (0, 1280, 2560, NP)
    assert sum(p.shape[1] for p in parts) == C_END == NP

    def body(*refs):
        part_refs = refs[:np_]
        w_ref, x_ref, xn_ref, g_ref, dxo_ref, dw_ref, dx_ref, dg_ref = refs[np_:]
        n = pl.program_id(0)

        @pl.when(n == 0)
        def _():
            dw_ref[...] = jnp.zeros_like(dw_ref)
            dg_ref[...] = jnp.zeros_like(dg_ref)

        dh = jnp.concatenate([r[...].astype(BF) for r in part_refs], axis=1)
        xnv = xn_ref[...]
        for lo, hi in zip(chunks[:-1], chunks[1:]):
            dw_ref[:, lo:hi] += _dot_tn(xnv, dh[:, lo:hi])
        dxn = _dot(dh, w_ref[...])
        xv = x_ref[...]
        _, r = _rms_fwd(xv, g_ref[...])
        dx, dgt = _rms_bwd(xv, g_ref[...], r, dxn)
        dx_ref[...] = dxo_ref[...] + dx
        dg_ref[...] += jnp.sum(dgt, axis=0, keepdims=True)

    once = pl.Buffered(1)
    return pl.pallas_call(
        body, name="inproj_bwd", grid=(T // tm,),
        in_specs=[pl.BlockSpec((tm, p.shape[1]), lambda n: (n, 0)) for p in parts]
        + [pl.BlockSpec((NP, D_MODEL), lambda n: (0, 0), pipeline_mode=once),
           pl.BlockSpec((tm, D_MODEL), lambda n: (n, 0)),
           pl.BlockSpec((tm, D_MODEL), lambda n: (n, 0)),
           pl.BlockSpec((1, D_MODEL), lambda n: (0, 0)),
           pl.BlockSpec((tm, D_MODEL), lambda n: (n, 0))],
        out_specs=[pl.BlockSpec((D_MODEL, NP), lambda n: (0, 0), pipeline_mode=once),
                   pl.BlockSpec((tm, D_MODEL), lambda n: (n, 0)),
                   pl.BlockSpec((1, D_MODEL), lambda n: (0, 0))],
        out_shape=[jax.ShapeDtypeStruct((D_MODEL, NP), F32), jax.ShapeDtypeStruct((T, D_MODEL), F32),
                   jax.ShapeDtypeStruct((1, D_MODEL), F32)],
        compiler_params=_params(1))(*parts, wpt, x, xn, g, dxo)


SWA_BLK = 128
SWA_TQ = 1024


def _bdot_nt(a, b):
    return lax.dot_general(a, b, (((2,), (2,)), ((0,), (0,))), preferred_element_type=F32)


def _bdot(a, b):
    return lax.dot_general(a, b, (((2,), (1,)), ((0,), (0,))), preferred_element_type=F32)


def _bdot_tn(a, b):
    return lax.dot_general(a, b, (((1,), (1,)), ((0,), (0,))), preferred_element_type=F32)


def _swa_probs(q, kc, kp, sink, mask_c, mask_p):
    sc = jnp.where(mask_c, _bdot_nt(q, kc) * SB_SCALE, NEG)
    sp = jnp.where(mask_p, _bdot_nt(q, kp) * SB_SCALE, NEG)
    m = jnp.maximum(jnp.maximum(jnp.max(sc, axis=-1, keepdims=True), jnp.max(sp, axis=-1, keepdims=True)), sink)
    pc = jnp.exp(sc - m)
    pp = jnp.exp(sp - m)
    ps = jnp.exp(sink - m)
    inv = 1.0 / (jnp.sum(pc, axis=-1, keepdims=True) + jnp.sum(pp, axis=-1, keepdims=True) + ps)
    return pc * inv, pp * inv, ps * inv


def _swa_masks(n, nb):
    blk = _iota((nb, SWA_BLK, SWA_BLK), 0)
    row = _iota((nb, SWA_BLK, SWA_BLK), 1)
    col = _iota((nb, SWA_BLK, SWA_BLK), 2)
    return col <= row, jnp.logical_and(col > row, jnp.logical_or(blk > 0, n > 0))


def _swa_specs(tq):
    halo = tq // SWA_BLK
    return [pl.BlockSpec(memory_space=pltpu.SMEM),
            pl.BlockSpec((tq, 256), lambda n: (n, C_AQ // 256)),
            pl.BlockSpec((tq, 128), lambda n: (n, C_AK // 128)),
            pl.BlockSpec((SWA_BLK, 128), lambda n: (jnp.maximum(n * halo - 1, 0), C_AK // 128)),
            pl.BlockSpec((tq, 128), lambda n: (n, C_AV // 128)),
            pl.BlockSpec((SWA_BLK, 128), lambda n: (jnp.maximum(n * halo - 1, 0), C_AV // 128))]


def _swa_blocked(cur_ref, prev_ref, gs, nb):
    cur = cur_ref[:, gs].reshape(nb, SWA_BLK, 64)
    prev = jnp.concatenate([prev_ref[:, gs].reshape(1, SWA_BLK, 64), cur[:nb - 1]], axis=0) if nb > 1 \
        else prev_ref[:, gs].reshape(1, SWA_BLK, 64)
    return cur, prev


def _swa_fwd(h16, sinks):
    T = h16.shape[0]
    tq = SWA_TQ if T % SWA_TQ == 0 else SWA_BLK
    nb = tq // SWA_BLK

    def body(s_ref, q_ref, kc_ref, kp_ref, vc_ref, vp_ref, o_ref):
        n = pl.program_id(0)
        mask_c, mask_p = _swa_masks(n, nb)
        for h in range(4):
            hs = slice(h * 64, (h + 1) * 64)
            gs = slice(h // 2 * 64, (h // 2 + 1) * 64)
            kc, kp = _swa_blocked(kc_ref, kp_ref, gs, nb)
            vc, vp = _swa_blocked(vc_ref, vp_ref, gs, nb)
            pc, pp, _ = _swa_probs(q_ref[:, hs].reshape(nb, SWA_BLK, 64), kc, kp, s_ref[h], mask_c, mask_p)
            o_ref[:, hs] = (_bdot(pc.astype(BF), vc) + _bdot(pp.astype(BF), vp)).reshape(tq, 64)

    return pl.pallas_call(
        body, name="swa_fwd", grid=(T // tq,), in_specs=_swa_specs(tq),
        out_specs=pl.BlockSpec((tq, 256), lambda n: (n, 0)),
        out_shape=jax.ShapeDtypeStruct((T, 256), F32),
        compiler_params=_params(1))(sinks, h16, h16, h16, h16, h16)


def _swa_bwd(h16, sinks, dya):
    T = h16.shape[0]
    tq = SWA_TQ if T % SWA_TQ == 0 else SWA_BLK
    nb = tq // SWA_BLK

    def body(s_ref, q_ref, kc_ref, kp_ref, vc_ref, vp_ref, do_ref, dq_ref, dk_ref, dv_ref, ds_ref):
        n = pl.program_id(0)

        @pl.when(n == 0)
        def _():
            dk_ref[...] = jnp.zeros_like(dk_ref)
            dv_ref[...] = jnp.zeros_like(dv_ref)
            ds_ref[...] = jnp.zeros_like(ds_ref)

        mask_c, mask_p = _swa_masks(n, nb)
        rows = pl.ds(pl.multiple_of(n * tq, tq), tq)
        before = pl.ds(pl.multiple_of(jnp.maximum(n * nb - 1, 0) * SWA_BLK, SWA_BLK), SWA_BLK)
        lane = _iota((8, 128), 1)
        row8 = _iota((8, 128), 0)

        def to_keys(own, prev):
            if nb == 1:
                return own
            return own + jnp.concatenate([prev[1:], jnp.zeros((1, SWA_BLK, 64), F32)], axis=0)

        for h in range(4):
            hs = slice(h * 64, (h + 1) * 64)
            gs = slice(h // 2 * 64, (h // 2 + 1) * 64)
            q = q_ref[:, hs].reshape(nb, SWA_BLK, 64)
            kc, kp = _swa_blocked(kc_ref, kp_ref, gs, nb)
            vc, vp = _swa_blocked(vc_ref, vp_ref, gs, nb)
            pc, pp, ps = _swa_probs(q, kc, kp, s_ref[h], mask_c, mask_p)
            pcb, ppb = pc.astype(BF), pp.astype(BF)
            do = do_ref[:, hs].reshape(nb, SWA_BLK, 64)
            dob = do.astype(BF)
            o = _bdot(pcb, vc) + _bdot(ppb, vp)
            dd = jnp.sum(do * o, axis=-1, keepdims=True)
            dsc = (pc * (_bdot_nt(dob, vc) - dd) * SB_SCALE).astype(BF)
            dsp = (pp * (_bdot_nt(dob, vp) - dd) * SB_SCALE).astype(BF)
            dq_ref[:, hs] = (_bdot(dsc, kc) + _bdot(dsp, kp)).reshape(tq, 64).astype(BF)
            dkp, dvp = _bdot_tn(dsp, q), _bdot_tn(ppb, dob)
            dk_ref[rows, gs] += to_keys(_bdot_tn(dsc, q), dkp).reshape(tq, 64)
            dv_ref[rows, gs] += to_keys(_bdot_tn(pcb, dob), dvp).reshape(tq, 64)
            dk_ref[before, gs] += dkp[0]
            dv_ref[before, gs] += dvp[0]
            ds_ref[...] += jnp.where(jnp.logical_and(lane == h, row8 == 0), -jnp.sum(ps * dd), 0.0)

    return pl.pallas_call(
        body, name="swa_bwd", grid=(T // tq,),
        in_specs=_swa_specs(tq) + [pl.BlockSpec((tq, 256), lambda n: (n, 0))],
        out_specs=[pl.BlockSpec((tq, 256), lambda n: (n, 0)),
                   pl.BlockSpec((T, 128), lambda n: (0, 0)),
                   pl.BlockSpec((T, 128), lambda n: (0, 0)),
                   pl.BlockSpec((8, 128), lambda n: (0, 0))],
        out_shape=[jax.ShapeDtypeStruct((T, 256), BF), jax.ShapeDtypeStruct((T, 128), F32),
                   jax.ShapeDtypeStruct((T, 128), F32), jax.ShapeDtypeStruct((8, 128), F32)],
        compiler_params=_params(1))(sinks, h16, h16, h16, h16, h16, dya)


def _conv_u(bc_ref, bx_ref, bch_ref, bxh_ref, n, tm):
    u = bc_ref[...] * bx_ref[...]
    uh = bch_ref[...] * bxh_ref[...] * (n > 0).astype(F32)
    rowi = _iota((tm, 256), 0)
    u1 = jnp.where(rowi == 0, uh[7:8, :], pltpu.roll(u, 1, axis=0))
    u2 = jnp.where(rowi == 0, uh[6:7, :], jnp.where(rowi == 1, uh[7:8, :], pltpu.roll(u, 2, axis=0)))
    return u, u1, u2


def _conv_fwd(h32, cw, cb):
    T = h32.shape[0]
    tm = 512 if T % 512 == 0 else T
    hb = tm // 8

    def body(bb_ref, bc_ref, bx_ref, bch_ref, bxh_ref, w_ref, b_ref, o_ref):
        n = pl.program_id(0)
        u, u1, u2 = _conv_u(bc_ref, bx_ref, bch_ref, bxh_ref, n, tm)
        y = w_ref[0:1, :] * u2 + w_ref[1:2, :] * u1 + w_ref[2:3, :] * u + b_ref[...]
        o_ref[...] = bb_ref[...] * y

    halo = lambda c: pl.BlockSpec((8, 256), lambda n: (jnp.maximum(n * hb - 1, 0), c // 256))
    return pl.pallas_call(
        body, name="conv_fwd", grid=(T // tm,),
        in_specs=[pl.BlockSpec((tm, 256), lambda n: (n, C_BB // 256)),
                  pl.BlockSpec((tm, 256), lambda n: (n, C_BC // 256)),
                  pl.BlockSpec((tm, 256), lambda n: (n, C_BX // 256)),
                  halo(C_BC), halo(C_BX),
                  pl.BlockSpec((8, 256), lambda n: (0, 0)),
                  pl.BlockSpec((1, 256), lambda n: (0, 0))],
        out_specs=pl.BlockSpec((tm, 256), lambda n: (n, 0)),
        out_shape=jax.ShapeDtypeStruct((T, 256), F32),
        compiler_params=_params(1))(h32, h32, h32, h32, h32, cw, cb)


def _conv_bwd(h32, cw, cb, dyb):
    T = h32.shape[0]
    tm = 512 if T % 512 == 0 else T
    hb = tm // 8
    nt = T // tm

    def body(bb_ref, bc_ref, bx_ref, bch_ref, bxh_ref, bbn_ref, dy_ref, dyn_ref, w_ref, b_ref,
             dbb_ref, dbc_ref, dbx_ref, dw_ref):
        n = pl.program_id(0)

        @pl.when(n == 0)
        def _():
            dw_ref[...] = jnp.zeros_like(dw_ref)

        u, u1, u2 = _conv_u(bc_ref, bx_ref, bch_ref, bxh_ref, n, tm)
        w0, w1, w2 = w_ref[0:1, :], w_ref[1:2, :], w_ref[2:3, :]
        y = w0 * u2 + w1 * u1 + w2 * u + b_ref[...]
        dyb_ = dy_ref[...]
        dbb_ref[...] = (dyb_ * y).astype(BF)
        dy = dyb_ * bb_ref[...]
        dyn = dyn_ref[...] * bbn_ref[...] * (n < nt - 1).astype(F32)
        rowi = _iota((tm, 256), 0)
        dy1 = jnp.where(rowi == tm - 1, dyn[0:1, :], pltpu.roll(dy, tm - 1, axis=0))
        dy2 = jnp.where(rowi == tm - 2, dyn[0:1, :],
                        jnp.where(rowi == tm - 1, dyn[1:2, :], pltpu.roll(dy, tm - 2, axis=0)))
        du = w2 * dy + w1 * dy1 + w0 * dy2
        dbc_ref[...] = (du * bx_ref[...]).astype(BF)
        dbx_ref[...] = (du * bc_ref[...]).astype(BF)
        dw_ref[0:1, :] += jnp.sum(dy * u2, axis=0, keepdims=True)
        dw_ref[1:2, :] += jnp.sum(dy * u1, axis=0, keepdims=True)
        dw_ref[2:3, :] += jnp.sum(dy * u, axis=0, keepdims=True)
        dw_ref[3:4, :] += jnp.sum(dy, axis=0, keepdims=True)

    halo = lambda c: pl.BlockSpec((8, 256), lambda n: (jnp.maximum(n * hb - 1, 0), c // 256))
    nxt = lambda c: pl.BlockSpec((8, 256), lambda n: (jnp.minimum((n + 1) * hb, T // 8 - 1), c // 256))
    cur = lambda c: pl.BlockSpec((tm, 256), lambda n: (n, c // 256))
    return pl.pallas_call(
        body, name="conv_bwd", grid=(nt,),
        in_specs=[cur(C_BB), cur(C_BC), cur(C_BX), halo(C_BC), halo(C_BX), nxt(C_BB),
                  cur(0), nxt(0),
                  pl.BlockSpec((8, 256), lambda n: (0, 0)),
                  pl.BlockSpec((1, 256), lambda n: (0, 0))],
        out_specs=[cur(0), cur(0), cur(0), pl.BlockSpec((8, 256), lambda n: (0, 0))],
        out_shape=[jax.ShapeDtypeStruct((T, 256), BF)] * 3 + [jax.ShapeDtypeStruct((8, 256), F32)],
        compiler_params=_params(1))(h32, h32, h32, h32, h32, h32, dyb, dyb, cw, cb)


def _cprep_specs(tm):
    return [pl.BlockSpec((tm, 256), lambda n: (n, C_CQ // 256)),
            pl.BlockSpec((tm, 128), lambda n: (n, C_CKV // 128)),
            pl.BlockSpec((tm, 128), lambda n: (n, C_CKR // 128)),
            pl.BlockSpec((tm, 128), lambda n: (n, C_CKRS // 128)),
            pl.BlockSpec((1, 256), lambda n: (0, 0)),
            pl.BlockSpec((1, 128), lambda n: (0, 0)),
            pl.BlockSpec((tm, 128), lambda n: (n, 0)),
            pl.BlockSpec((tm, 128), lambda n: (n, 0))]


def _cprep_fwd(h32, gq, gkv, wuq2, wkv2, cosk, sin):
    T = h32.shape[0]
    tm = 512 if T % 512 == 0 else T

    def body(cq_ref, ckv_ref, ckr_ref, ckrs_ref, gq_ref, gkv_ref, cos_ref, sin_ref, wuq_ref, wkv_ref,
             q_ref, k_ref, v_ref):
        cosk_, sin_ = cos_ref[...], sin_ref[...]
        cosq = cosk_ + (_iota((tm, 128), 1) < 64).astype(F32)
        cqn, _ = _rms_fwd(cq_ref[...], gq_ref[...])
        q2 = _dot(cqn.astype(BF), wuq_ref[...])
        ckvn, _ = _rms_fwd(ckv_ref[...], gkv_ref[...])
        kv2 = _dot(ckvn.astype(BF), wkv_ref[...])
        kr = ckr_ref[...] * cosk_ + ckrs_ref[...] * sin_
        for h in range(4):
            hs = slice(h * 128, (h + 1) * 128)
            q_ref[:, hs] = ((q2[:, hs] * cosq + q2[:, 512 + h * 128:512 + (h + 1) * 128] * sin_) * MLA_SCALE).astype(BF)
            k_ref[:, hs] = (kv2[:, hs] + kr).astype(BF)
        ones = _iota((tm, 512), 1) % 128 == 64
        v_ref[...] = jnp.where(ones, 1.0, kv2[:, 512:]).astype(BF)

    return pl.pallas_call(
        body, name="cprep_fwd", grid=(T // tm,),
        in_specs=_cprep_specs(tm) + [pl.BlockSpec((256, 1024), lambda n: (0, 0)),
                                     pl.BlockSpec((128, 1024), lambda n: (0, 0))],
        out_specs=[pl.BlockSpec((tm, 512), lambda n: (n, 0))] * 3,
        out_shape=[jax.ShapeDtypeStruct((T, 512), BF)] * 3,
        compiler_params=_params(1))(h32, h32, h32, h32, gq, gkv, cosk, sin, wuq2, wkv2)


def _cprep_bwd(h32, gq, gkv, wuq2t, wkv2t, cosk, sin, dq, dk, dv):
    T = h32.shape[0]
    tm = 512 if T % 512 == 0 else T

    def body(cq_ref, ckv_ref, ckr_ref, ckrs_ref, gq_ref, gkv_ref, cos_ref, sin_ref, wuq_ref, wkv_ref,
             dq_ref, dk_ref, dv_ref,
             dcq_ref, dckv_ref, dckr_ref, dckrs_ref, dwuq_ref, dwkv_ref, dgq_ref, dgkv_ref):
        n = pl.program_id(0)

        @pl.when(n == 0)
        def _():
            dwuq_ref[...] = jnp.zeros_like(dwuq_ref)
            dwkv_ref[...] = jnp.zeros_like(dwkv_ref)
            dgq_ref[...] = jnp.zeros_like(dgq_ref)
            dgkv_ref[...] = jnp.zeros_like(dgkv_ref)

        cosk_, sin_ = cos_ref[...], sin_ref[...]
        cosq = cosk_ + (_iota((tm, 128), 1) < 64).astype(F32)
        dkr = jnp.zeros((tm, 128), F32)
        plain, swapped = [], []
        for h in range(4):
            hs = slice(h * 128, (h + 1) * 128)
            dqh = dq_ref[:, hs] * MLA_SCALE
            plain.append((dqh * cosq).astype(BF))
            swapped.append((dqh * sin_).astype(BF))
            dkr = dkr + dk_ref[:, hs]
        dq2 = jnp.concatenate(plain + swapped, axis=1)
        dkv2 = jnp.concatenate([dk_ref[...].astype(BF), dv_ref[...].astype(BF)], axis=1)
        dckr_ref[...] = (dkr * cosk_).astype(BF)
        dckrs_ref[...] = (dkr * sin_).astype(BF)

        cq, gq_ = cq_ref[...], gq_ref[...]
        cqn, rq = _rms_fwd(cq, gq_)
        dwuq_ref[...] += _dot_tn(cqn.astype(BF), dq2)
        dcq, dgt = _rms_bwd(cq, gq_, rq, _dot(dq2, wuq_ref[...]))
        dcq_ref[...] = dcq.astype(BF)
        dgq_ref[...] += jnp.sum(dgt, axis=0, keepdims=True)

        ckv, gkv_ = ckv_ref[...], gkv_ref[...]
        ckvn, rkv = _rms_fwd(ckv, gkv_)
        dwkv_ref[...] += _dot_tn(ckvn.astype(BF), dkv2)
        dckv, dgt2 = _rms_bwd(ckv, gkv_, rkv, _dot(dkv2, wkv_ref[...]))
        dckv_ref[...] = dckv.astype(BF)
        dgkv_ref[...] += jnp.sum(dgt2, axis=0, keepdims=True)

    row = lambda w: pl.BlockSpec((tm, w), lambda n: (n, 0))
    return pl.pallas_call(
        body, name="cprep_bwd", grid=(T // tm,),
        in_specs=_cprep_specs(tm) + [pl.BlockSpec((1024, 256), lambda n: (0, 0)),
                                     pl.BlockSpec((1024, 128), lambda n: (0, 0)),
                                     row(512), row(512), row(512)],
        out_specs=[row(256), row(128), row(128), row(128),
                   pl.BlockSpec((256, 1024), lambda n: (0, 0)), pl.BlockSpec((128, 1024), lambda n: (0, 0)),
                   pl.BlockSpec((1, 256), lambda n: (0, 0)), pl.BlockSpec((1, 128), lambda n: (0, 0))],
        out_shape=[jax.ShapeDtypeStruct((T, 256), BF), jax.ShapeDtypeStruct((T, 128), BF),
                   jax.ShapeDtypeStruct((T, 128), BF), jax.ShapeDtypeStruct((T, 128), BF),
                   jax.ShapeDtypeStruct((256, 1024), F32), jax.ShapeDtypeStruct((128, 1024), F32),
                   jax.ShapeDtypeStruct((1, 256), F32), jax.ShapeDtypeStruct((1, 128), F32)],
        compiler_params=_params(1))(h32, h32, h32, h32, gq, gkv, cosk, sin, wuq2t, wkv2t, dq, dk, dv)


MLA_TILE = 512
MLA_HEADS_PER_STEP = 4


def _causal_mask(t):
    return _iota((t, t), 1) <= _iota((t, t), 0)


def _mla_fwd(q, k, v):
    T = q.shape[0]
    tq = MLA_TILE

    def body(q_ref, k_ref, v_ref, o_ref, lse_ref):
        i = pl.program_id(1)
        mask = _causal_mask(tq)
        heads = [slice(128 * h, 128 * h + 128) for h in range(MLA_HEADS_PER_STEP)]
        qs = [q_ref[:, hs] for hs in heads]

        def step(j, carry, masked):
            rows = pl.ds(pl.multiple_of(j * tq, tq), tq)
            out = []
            for hh, hs in enumerate(heads):
                m, acc = carry[hh]
                s = _dot_nt(qs[hh], k_ref[rows, hs])
                if masked:
                    s = jnp.where(mask, s, NEG)
                m_new = jnp.maximum(m, jnp.max(s, axis=-1, keepdims=True))
                p = jnp.exp((s - m_new).astype(BF))
                acc = jnp.exp(m - m_new) * acc + _dot(p, v_ref[rows, hs])
                out.append((m_new, acc))
            return tuple(out)

        init = ((jnp.full((tq, 1), NEG, F32), jnp.zeros((tq, 128), F32)),) * len(heads)
        carry = lax.fori_loop(0, i // 2, lambda t, c: step(2 * t + 1, step(2 * t, c, False), False), init)
        carry = lax.cond(i % 2 == 1, lambda c: step(i - 1, c, False), lambda c: c, carry)
        carry = step(i, carry, True)
        for hh, hs in enumerate(heads):
            m, acc = carry[hh]
            l = acc[:, 64:65]
            o_ref[:, hs] = acc * (1.0 / l)
            lse_ref[:, hs] = jnp.broadcast_to(m + jnp.log(l), (tq, 128))

    width = 128 * MLA_HEADS_PER_STEP
    blk = pl.BlockSpec((tq, width), lambda h, i: (i, h))
    full = pl.BlockSpec((T, width), lambda h, i: (0, h))
    return pl.pallas_call(
        body, name="mla_fwd", grid=(4 // MLA_HEADS_PER_STEP, T // tq), in_specs=[blk, full, full],
        out_specs=[blk, blk],
        out_shape=[jax.ShapeDtypeStruct((T, 512), F32), jax.ShapeDtypeStruct((T, 512), F32)],
        compiler_params=_params(2))(q, k, v)


def _mla_bwd(q, k, v, o, lse, do):
    T = q.shape[0]
    tq = MLA_TILE

    def body(q_ref, k_ref, v_ref, o_ref, lse_ref, do_ref, dq_ref, dk_ref, dv_ref):
        i = pl.program_id(1)

        @pl.when(i == 0)
        def _():
            dk_ref[...] = jnp.zeros_like(dk_ref)
            dv_ref[...] = jnp.zeros_like(dv_ref)

        heads = [slice(0, 128), slice(128, 256)]
        mask = _causal_mask(tq)
        qs, dobs, dds, lses = [], [], [], []
        for hs in heads:
            do = do_ref[:, hs]
            qs.append(q_ref[:, hs])
            dobs.append(do.astype(BF))
            dds.append(jnp.sum(do * o_ref[:, hs], axis=-1, keepdims=True))
            lses.append(lse_ref[:, hs.start:hs.start + 1])

        def step(j, dqs, masked):
            rows = pl.ds(pl.multiple_of(j * tq, tq), tq)
            out = []
            for hh, hs in enumerate(heads):
                kj, vj = k_ref[rows, hs], v_ref[rows, hs]
                s = _dot_nt(qs[hh], kj)
                if masked:
                    s = jnp.where(mask, s, NEG)
                p = jnp.exp(s - lses[hh])
                ds = (p * (_dot_nt(dobs[hh], vj) - dds[hh])).astype(BF)
                dk_ref[rows, hs] += _dot_tn(ds, qs[hh])
                dv_ref[rows, hs] += _dot_tn(p.astype(BF), dobs[hh])
                out.append(dqs[hh] + _dot(ds, kj))
            return tuple(out)

        dqs = lax.fori_loop(0, i // 2, lambda t, c: step(2 * t + 1, step(2 * t, c, False), False),
                            (jnp.zeros((tq, 128), F32),) * 2)
        dqs = lax.cond(i % 2 == 1, lambda c: step(i - 1, c, False), lambda c: c, dqs)
        dqs = step(i, dqs, True)
        for hh, hs in enumerate(heads):
            dq_ref[:, hs] = dqs[hh]

    blk = pl.BlockSpec((tq, 256), lambda h, i: (i, h))
    full = pl.BlockSpec((T, 256), lambda h, i: (0, h), pipeline_mode=pl.Buffered(1))
    return pl.pallas_call(
        body, name="mla_bwd", grid=(2, T // tq), in_specs=[blk, full, full, blk, blk, blk],
        out_specs=[blk, full, full],
        out_shape=[jax.ShapeDtypeStruct((T, 512), F32)] * 3,
        compiler_params=_params(2))(q, k, v, o, lse, do)


def _sb_tile(qk, rr, strict, masked, upper):
    z2 = qk * (SB_SCALE * LOG2E)
    l1 = jnp.log2(1.0 + jnp.exp2(-jnp.abs(z2)))
    lk = -jnp.maximum(z2, 0.0) - l1
    if masked:
        lk = jnp.where(strict, lk, 0.0)
    after = rr + _dot(lk.astype(BF), upper)
    ll = jnp.minimum(z2, 0.0) - l1
    a = jnp.exp2(ll + after)
    if masked:
        a = jnp.where(strict, a, 0.0)
    return ll, a, jnp.sum(lk, axis=-1, keepdims=True)


SB_TQ, SB_TK = 256, 256
SB_DEAD = -160.0


def _sb_walk(trips, one_step, carry):
    def alive(c):
        t, cr = c
        top = functools.reduce(jnp.maximum, [jnp.max(h[0]) for h in cr])
        return jnp.logical_and(t < trips, top > SB_DEAD)

    def body(c):
        t, cr = c
        return t + 1, one_step(t, cr)

    return lax.while_loop(alive, body, (jnp.int32(0), carry))[1]


def _sb_consts(tq, tk):
    row, col = _iota((tq, tk), 0), _iota((tq, tk), 1)
    strict = [col + d * tk < row for d in range(tq // tk)]
    r2, c2 = _iota((tk, tk), 0), _iota((tk, tk), 1)
    return strict, (r2 > c2).astype(BF), (r2 < c2).astype(BF)


def _sb_fwd(h16):
    T = h16.shape[0]
    tq, tk = SB_TQ, SB_TK
    nd = tq // tk

    def body(q0_ref, q1_ref, k0_ref, k1_ref, v0_ref, v1_ref, o_ref):
        i = pl.program_id(0)
        strict, upper, _ = _sb_consts(tq, tk)
        lane = _iota((tq, 128), 1)
        pairs = [slice(0, 128), slice(128, 256)]
        k_refs, v_refs = (k0_ref, k1_ref), (v0_ref, v1_ref)
        qms = []
        for q_ref in (q0_ref, q1_ref):
            q2 = q_ref[...]
            qms += [jnp.where(lane < 64, q2, jnp.zeros_like(q2)), jnp.where(lane >= 64, q2, jnp.zeros_like(q2))]

        def step(j, carry, d):
            rows = pl.ds(pl.multiple_of(j * tk, tk), tk)
            out = []
            for h in range(4):
                rr, acc = carry[h]
                _, a, rs = _sb_tile(_dot_nt(qms[h], k_refs[h // 2][rows, :]), rr, None if d is None else strict[d],
                                    d is not None, upper)
                out.append((rr + rs, acc + _dot(a.astype(BF), v_refs[h // 2][rows, :])))
            return tuple(out)

        carry = ((jnp.zeros((tq, 1), F32), jnp.zeros((tq, 128), F32)),) * 4
        for d in reversed(range(nd)):
            carry = step(nd * i + d, carry, d)
        carry = _sb_walk(nd * i, lambda t, c: step(nd * i - 1 - t, c, None), carry)
        for p, ps in enumerate(pairs):
            o_ref[:, ps] = jnp.where(lane < 64, carry[2 * p][1], carry[2 * p + 1][1])

    return pl.pallas_call(
        body, name="sb_fwd", grid=(T // tq,),
        in_specs=[pl.BlockSpec((tq, 128), lambda i: (i, C_DQ // 128)),
                  pl.BlockSpec((tq, 128), lambda i: (i, C_DQ // 128 + 1)),
                  pl.BlockSpec((T, 128), lambda i: (0, C_DK // 128)),
                  pl.BlockSpec((T, 128), lambda i: (0, C_DK // 128 + 1)),
                  pl.BlockSpec((T, 128), lambda i: (0, C_DV // 128)),
                  pl.BlockSpec((T, 128), lambda i: (0, C_DV // 128 + 1))],
        out_specs=pl.BlockSpec((tq, 256), lambda i: (i, 0)),
        out_shape=jax.ShapeDtypeStruct((T, 256), F32),
        compiler_params=_params(1))(h16, h16, h16, h16, h16, h16)


def _sb_bwd(h16, yd, dyd):
    T = h16.shape[0]
    tq, tk = SB_TQ, SB_TK
    nd = tq // tk

    def body(q0_ref, q1_ref, k0_ref, k1_ref, v0_ref, v1_ref, o_ref, do_ref, dq_ref, dk_ref, dv_ref):
        i = pl.program_id(0)

        @pl.when(i == 0)
        def _():
            dk_ref[...] = jnp.zeros_like(dk_ref)
            dv_ref[...] = jnp.zeros_like(dv_ref)

        strict, upper, before = _sb_consts(tq, tk)
        lane = _iota((tq, 128), 1)
        lane_k = _iota((tk, 128), 1)
        pairs = [slice(0, 128), slice(128, 256)]
        k_refs, v_refs = (k0_ref, k1_ref), (v0_ref, v1_ref)
        q2s, dob2s, qms, doms, dds = [], [], [], [], []
        for p, q_ref in enumerate((q0_ref, q1_ref)):
            q2 = q_ref[...]
            dob2 = do_ref[:, pairs[p]].astype(BF)
            doo = dob2.astype(F32) * o_ref[:, pairs[p]]
            q2s.append(q2)
            dob2s.append(dob2)
            for mine in (lane < 64, lane >= 64):
                qms.append(jnp.where(mine, q2, jnp.zeros_like(q2)))
                doms.append(jnp.where(mine, dob2, jnp.zeros_like(dob2)))
                dds.append(jnp.sum(jnp.where(mine, doo, 0.0), axis=-1, keepdims=True))

        def step(j, carry, d):
            rows = pl.ds(pl.multiple_of(j * tk, tk), tk)
            out, dks, dvs = [], [], []
            for h in range(4):
                kj, vj = k_refs[h // 2][rows, :], v_refs[h // 2][rows, :]
                rr, sg, dq = carry[h]
                ll, a, rs = _sb_tile(_dot_nt(qms[h], kj), rr, None if d is None else strict[d], d is not None, upper)
                ab = a.astype(BF)
                g = _dot_nt(doms[h], vj) * ab.astype(F32)
                gs = jnp.sum(g, axis=-1, keepdims=True)
                pre = (dds[h] - sg - gs) + _dot(g.astype(BF), before)
                dz = g - jnp.exp2(ll) * (g + pre)
                if d is not None:
                    dz = jnp.where(strict[d], dz, 0.0)
                dzb = dz.astype(BF)
                dks.append(_dot_tn(dzb, q2s[h // 2]))
                dvs.append(_dot_tn(ab, dob2s[h // 2]))
                out.append((rr + rs, sg + gs, dq + _dot(dzb, kj)))
            for p, ps in enumerate(pairs):
                dk_ref[rows, ps] += jnp.where(lane_k < 64, dks[2 * p], dks[2 * p + 1]) * SB_SCALE
                dv_ref[rows, ps] += jnp.where(lane_k < 64, dvs[2 * p], dvs[2 * p + 1])
            return tuple(out)

        zero = jnp.zeros((tq, 1), F32)
        carry = ((zero, zero, jnp.zeros((tq, 128), F32)),) * 4
        for d in reversed(range(nd)):
            carry = step(nd * i + d, carry, d)
        carry = _sb_walk(nd * i, lambda t, c: step(nd * i - 1 - t, c, None), carry)
        for p, ps in enumerate(pairs):
            dq_ref[:, ps] = jnp.where(lane < 64, carry[2 * p][2], carry[2 * p + 1][2]) * SB_SCALE

    blk = lambda c: pl.BlockSpec((tq, 128), lambda i: (i, c // 128))
    full = lambda c: pl.BlockSpec((T, 128), lambda i: (0, c // 128))
    row = pl.BlockSpec((tq, 256), lambda i: (i, 0))
    acc = pl.BlockSpec((T, 256), lambda i: (0, 0))
    return pl.pallas_call(
        body, name="sb_bwd", grid=(T // tq,),
        in_specs=[blk(C_DQ), blk(C_DQ + 128), full(C_DK), full(C_DK + 128), full(C_DV), full(C_DV + 128), row, row],
        out_specs=[row, acc, acc],
        out_shape=[jax.ShapeDtypeStruct((T, 256), F32)] * 3,
        compiler_params=_params(1))(h16, h16, h16, h16, h16, h16, yd, dyd)


def _compact_c(ycp):
    return jnp.concatenate([ycp[:, h * 128:h * 128 + 64] for h in range(4)], axis=1)


def _post_fwd(ya, yb, ycp, yd, h32, ggrp, wout, gpost, x, tgt=None):
    T = x.shape[0]
    tm = 256
    last = tgt is not None

    def body(*refs):
        ya_ref, yb_ref, yc_ref, yd_ref, gate_ref, gg_ref, w_ref, gp_ref, x_ref = refs[:9]
        if last:
            t_ref, xn_ref, o_ref, sq_ref = refs[9:]
        else:
            xn_ref, o_ref = refs[9:]
        ys = [ya_ref[...], yb_ref[...], _compact_c(yc_ref[...]), yd_ref[...]]
        gate = gate_ref[...]
        sil = gate * (1.0 / (1.0 + jnp.exp(-gate)))
        parts = []
        for gi in range(4):
            ng, _ = _rms_fwd(ys[gi], gg_ref[:, gi * 256:(gi + 1) * 256])
            parts.append(ng * sil[:, gi * 256:(gi + 1) * 256])
        o = _dot(jnp.concatenate(parts, axis=1).astype(BF), w_ref[...])
        o_ref[...] = o
        on, _ = _rms_fwd(o, gp_ref[...])
        if last:
            @pl.when(pl.program_id(0) == 0)
            def _():
                sq_ref[...] = jnp.zeros_like(sq_ref)

            d = (x_ref[...] + on) - t_ref[...]
            sq_ref[...] += jnp.sum(d * d, axis=0, keepdims=True)
            xn_ref[...] = d * (1.0 / D_MODEL)
        else:
            xn_ref[...] = x_ref[...] + on

    row = lambda w: pl.BlockSpec((tm, w), lambda n: (n, 0))
    vec = pl.BlockSpec((1, 1024), lambda n: (0, 0))
    return pl.pallas_call(
        body, name="post_fwd", grid=(T // tm,),
        in_specs=[row(256), row(256), row(512), row(256), pl.BlockSpec((tm, 1024), lambda n: (n, C_GATE // 1024)),
                  vec, pl.BlockSpec((1024, 1024), lambda n: (0, 0)), vec, row(1024)] + ([row(1024)] if last else []),
        out_specs=[row(1024), row(1024)] + ([vec] if last else []),
        out_shape=[jax.ShapeDtypeStruct((T, 1024), F32), jax.ShapeDtypeStruct((T, 1024), F32)]
        + ([jax.ShapeDtypeStruct((1, 1024), F32)] if last else []),
        compiler_params=_params(1))(*([ya, yb, ycp, yd, h32, ggrp, wout, gpost, x] + ([tgt] if last else [])))


def _post_bwd(dx, o, gpost, woutt, ya, yb, ycp, yd, h32, ggrp):
    T = dx.shape[0]
    tm = 256

    def body(dx_ref, o_ref, gp_ref, w_ref, ya_ref, yb_ref, yc_ref, yd_ref, gate_ref, gg_ref,
             dw_ref, dya_ref, dyb_ref, dyc_ref, dyd_ref, dgate_ref, dgp_ref, dgg_ref):
        n = pl.program_id(0)

        @pl.when(n == 0)
        def _():
            dw_ref[...] = jnp.zeros_like(dw_ref)
            dgp_ref[...] = jnp.zeros_like(dgp_ref)
            dgg_ref[...] = jnp.zeros_like(dgg_ref)

        ov, gp = o_ref[...], gp_ref[...]
        _, ro = _rms_fwd(ov, gp)
        do, dgt = _rms_bwd(ov, gp, ro, dx_ref[...])
        dgp_ref[...] += jnp.sum(dgt, axis=0, keepdims=True)
        dob = do.astype(BF)
        gate = gate_ref[...]
        sg = 1.0 / (1.0 + jnp.exp(-gate))
        sil = gate * sg
        dsil = sg * (1.0 + gate * (1.0 - sg))
        ys = [ya_ref[...], yb_ref[...], _compact_c(yc_ref[...]), yd_ref[...]]
        normed = [_rms_fwd(ys[gi], gg_ref[:, gi * 256:(gi + 1) * 256]) for gi in range(4)]
        ym = jnp.concatenate([normed[gi][0] * sil[:, gi * 256:(gi + 1) * 256] for gi in range(4)], axis=1).astype(BF)
        dw_ref[...] += _dot_tn(ym, dob)
        dym = _dot(dob, w_ref[...])
        dys = []
        for gi in range(4):
            gs = slice(gi * 256, (gi + 1) * 256)
            gg = gg_ref[:, gs]
            ng, rg = normed[gi]
            dgate_ref[:, gs] = (dym[:, gs] * ng * dsil[:, gs]).astype(BF)
            dy, dgt2 = _rms_bwd(ys[gi], gg, rg, dym[:, gs] * sil[:, gs])
            dgg_ref[:, gs] += jnp.sum(dgt2, axis=0, keepdims=True)
            dys.append(dy)
        dya_ref[...] = dys[0]
        dyb_ref[...] = dys[1]
        dyd_ref[...] = dys[3]
        z64 = jnp.zeros((tm, 64), F32)
        dyc_ref[...] = jnp.concatenate(
            [piece for h in range(4) for piece in (dys[2][:, h * 64:(h + 1) * 64], z64)], axis=1)

    row = lambda w: pl.BlockSpec((tm, w), lambda n: (n, 0))
    vec = pl.BlockSpec((1, 1024), lambda n: (0, 0))
    return pl.pallas_call(
        body, name="post_bwd", grid=(T // tm,),
        in_specs=[row(1024), row(1024), vec, pl.BlockSpec((1024, 1024), lambda n: (0, 0)),
                  row(256), row(256), row(512), row(256),
                  pl.BlockSpec((tm, 1024), lambda n: (n, C_GATE // 1024)), vec],
        out_specs=[pl.BlockSpec((1024, 1024), lambda n: (0, 0)), row(256), row(256), row(512), row(256), row(1024),
                   vec, vec],
        out_shape=[jax.ShapeDtypeStruct((1024, 1024), F32), jax.ShapeDtypeStruct((T, 256), F32),
                   jax.ShapeDtypeStruct((T, 256), F32), jax.ShapeDtypeStruct((T, 512), F32),
                   jax.ShapeDtypeStruct((T, 256), F32), jax.ShapeDtypeStruct((T, 1024), BF),
                   jax.ShapeDtypeStruct((1, 1024), F32), jax.ShapeDtypeStruct((1, 1024), F32)],
        compiler_params=_params(1))(dx, o, gpost, woutt, ya, yb, ycp, yd, h32, ggrp)


def _swap_rows32(a):
    return jnp.concatenate([a[16:32], a[0:16]], axis=0)


def _pad_w_uq(w):
    z = lambda n: jnp.zeros((w.shape[0], n), w.dtype)
    a = [p for h in range(4) for p in (w[:, 96 * h:96 * h + 96], z(32))]
    b = [p for h in range(4) for p in (z(64), _swap32(w[:, 96 * h + 64:96 * h + 96]), z(32))]
    return jnp.concatenate(a + b, axis=1)


def _unpad_w_uq(d):
    out = []
    for h in range(4):
        out.append(d[:, 128 * h:128 * h + 64])
        out.append(d[:, 128 * h + 64:128 * h + 96] + _swap32(d[:, 512 + 128 * h + 64:512 + 128 * h + 96]))
    return jnp.concatenate(out, axis=1)


def _pad_w_ukv(w):
    z = jnp.zeros((w.shape[0], 64), w.dtype)
    a = [p for h in range(4) for p in (w[:, 128 * h:128 * h + 64], z)]
    b = [p for h in range(4) for p in (w[:, 128 * h + 64:128 * h + 128], z)]
    return jnp.concatenate(a + b, axis=1)


def _unpad_w_ukv(d):
    return jnp.concatenate([p for h in range(4) for p in (d[:, 128 * h:128 * h + 64],
                                                          d[:, 512 + 128 * h:512 + 128 * h + 64])], axis=1)


def _rope_tables(pos):
    freqs = 10000.0 ** (-jnp.arange(16, dtype=F32) / 16)
    ang = pos.astype(F32)[:, None] * freqs
    c, s = jnp.cos(ang), jnp.sin(ang)
    z = lambda n: jnp.zeros((pos.shape[0], n), F32)
    return (jnp.concatenate([z(64), c, c, z(32)], axis=1), jnp.concatenate([z(64), -s, s, z(32)], axis=1))


def _layer_weights(W, l):
    wuq2 = _pad_w_uq(W["mla_w_uq"][l])
    wkv2 = _pad_w_ukv(W["mla_w_ukv"][l])
    wout = W["w_out"][l]
    cw = jnp.concatenate([W["conv_w"][l].astype(F32), jnp.zeros((5, 256), F32)], axis=0)
    return dict(
        wpt=W["wpt"][l], wuq2=wuq2.astype(BF), wuq2t=wuq2.T.astype(BF),
        wkv2=wkv2.astype(BF), wkv2t=wkv2.T.astype(BF), wout=wout.astype(BF), woutt=wout.T.astype(BF),
        cw=cw, cb=W["conv_b"][l][None, :], sinks=W["attn_sinks"][l],
        gpre=W["norm_pre"][l][None, :], gq=W["mla_q_norm"][l][None, :], gkv=W["mla_kv_norm"][l][None, :],
        ggrp=W["group_norm"][l][None, :], gpost=W["norm_post"][l][None, :])


def _local_step(x, pos, W, tgt):
    cosk, sin = _rope_tables(pos)
    saved = []
    for l in range(DEPTH):
        lw = _layer_weights(W, l)
        h32, h16, xn = _inproj_fwd(x, lw["gpre"], lw["wpt"])
        ya = _swa_fwd(h16, lw["sinks"])
        yb = _conv_fwd(h32, lw["cw"], lw["cb"])
        qc, kc, vc = _cprep_fwd(h32, lw["gq"], lw["gkv"], lw["wuq2"], lw["wkv2"], cosk, sin)
        ycp, lse = _mla_fwd(qc, kc, vc)
        yd = _sb_fwd(h16)
        if l < DEPTH - 1:
            x_new, o = _post_fwd(ya, yb, ycp, yd, h32, lw["ggrp"], lw["wout"], lw["gpost"], x)
        else:
            dx, o, sq = _post_fwd(ya, yb, ycp, yd, h32, lw["ggrp"], lw["wout"], lw["gpost"], x, tgt)
        saved.append(dict(lw=lw, x=x, h32=h32, h16=h16, xn=xn, ya=ya, yb=yb, qc=qc, kc=kc, vc=vc, ycp=ycp,
                          lse=lse, yd=yd, o=o))
        if l < DEPTH - 1:
            x = x_new

    grads = {k: [None] * DEPTH for k in ("norm_pre", "w_in_pt", "attn_sinks", "conv_w", "conv_b", "mla_q_norm",
                                         "mla_w_uq", "mla_kv_norm", "mla_w_ukv", "group_norm", "w_out",
                                         "norm_post")}
    for l in reversed(range(DEPTH)):
        s = saved[l]
        lw = s["lw"]
        dwout, dya, dyb, dycp, dyd, dgate, dgpost, dggrp = _post_bwd(
            dx, s["o"], lw["gpost"], lw["woutt"], s["ya"], s["yb"], s["ycp"], s["yd"], s["h32"], lw["ggrp"])
        grads["norm_post"][l] = dgpost[0]
        grads["group_norm"][l] = dggrp[0]
        grads["w_out"][l] = dwout
        sdq, sdk, sdv = _sb_bwd(s["h16"], s["yd"], dyd)
        mdq, mdk, mdv = _mla_bwd(s["qc"], s["kc"], s["vc"], s["ycp"], s["lse"], dycp)
        dcq, dckv, dckr, dckrs, dwuq2, dwkv2, dgq, dgkv = _cprep_bwd(
            s["h32"], lw["gq"], lw["gkv"], lw["wuq2t"], lw["wkv2t"], cosk, sin, mdq, mdk, mdv)
        grads["mla_q_norm"][l] = dgq[0]
        grads["mla_kv_norm"][l] = dgkv[0]
        grads["mla_w_uq"][l] = _unpad_w_uq(dwuq2)
        grads["mla_w_ukv"][l] = _unpad_w_ukv(dwkv2)
        dbb, dbc, dbx, dcw = _conv_bwd(s["h32"], lw["cw"], lw["cb"], dyb)
        grads["conv_w"][l] = dcw[0:3]
        grads["conv_b"][l] = dcw[3]
        adq, adk, adv, dsk = _swa_bwd(s["h16"], lw["sinks"], dya)
        grads["attn_sinks"][l] = dsk[0, 0:4]
        parts = [dgate, adq, adk, adv, dbb, dbc, dbx, dcq, dckv, dckr, dckrs, sdq, sdk, sdv]
        dwp, dx, dgpre = _inproj_bwd(parts, lw["wpt"], s["x"], s["xn"], lw["gpre"], dx)
        grads["w_in_pt"][l] = dwp.T
        grads["norm_pre"][l] = dgpre[0]
    return sq, dx, grads


SMALL_SHARDED = ("conv_w", "mla_w_uq", "mla_w_ukv")
REPLICATED = ("norm_pre", "attn_sinks", "conv_b", "mla_q_norm", "mla_kv_norm", "group_norm", "norm_post")
ORDER = ("norm_pre", "w_in", "attn_sinks", "conv_w", "conv_b", "mla_q_norm", "mla_w_uq", "mla_kv_norm",
         "mla_w_ukv", "group_norm", "w_out", "norm_post")
W_IN_COLS = 436
W_IN_WIN = 440
SMALL_ROWS = 48


def _pack_small(arrs, dtype):
    flat = jnp.concatenate([a.reshape(-1).astype(dtype) for a in arrs])
    flat = jnp.concatenate([flat, jnp.zeros((SMALL_ROWS * D_MODEL - flat.shape[0],), dtype)])
    return flat.reshape(SMALL_ROWS, D_MODEL)


TAIL_ROW0 = DEPTH * W_IN_WIN


def _pack_state(ps):
    k = len(ps)
    wout = jnp.stack([p["w_out"] for p in ps]).reshape(k, DEPTH * 128, D_MODEL)
    flat = jnp.stack([jnp.concatenate([p[n].reshape(-1) for n in SMALL_SHARDED + REPLICATED]) for p in ps])
    small = jnp.pad(flat, ((0, 0), (0, SMALL_ROWS * D_MODEL - flat.shape[1]))).reshape(k, SMALL_ROWS, D_MODEL)
    return jnp.concatenate([wout, small], axis=1)


def _unpack_state(buf, p):
    k = buf.shape[0]
    out = {"w_out": buf[:, 0:DEPTH * 128].reshape(k, DEPTH, 128, D_MODEL)}
    flat = buf[:, DEPTH * 128:].reshape(k, SMALL_ROWS * D_MODEL)
    off = 0
    for n in SMALL_SHARDED + REPLICATED:
        size = int(np.prod(p[n].shape))
        out[n] = flat[:, off:off + size].reshape((k,) + p[n].shape)
        off += size
    return out


def _rows_of_w_in_t(lo, hi, padded, kr):
    segs = ((0, 1664, padded, C_AQ), (1664, 1696, kr, 0), (1696, 2464, padded, C_DQ), (2464, 3488, padded, C_GATE))
    out = []
    for s0, s1, src, base in segs:
        a, b = max(lo, s0), min(hi, s1)
        if a < b:
            out.append(src[base + a - s0:base + b - s0])
    return out


def _me():
    return lax.axis_index("x"), lax.axis_index("y"), lax.axis_index("c")


def _all_gather(block):
    R, C = block.shape

    def body(src_ref, out_ref, send_sems, recv_sems, local_sem):
        x, y, c = _me()
        me, sibling = (x, y, c), (x, y, 1 - c)
        chips = [(1 - x, y), (x, 1 - y), (1 - x, 1 - y)]

        def slot(px, py, pc):
            return out_ref.at[4 * px + 2 * py + pc]

        def copy(k, block, to, src=None):
            return pltpu.make_async_remote_copy(
                src_ref=slot(*block) if src is None else src, dst_ref=slot(*block), send_sem=send_sems.at[k],
                recv_sem=recv_sems.at[k], device_id=to, device_id_type=MESH)

        mine = pltpu.make_async_copy(src_ref, slot(*me), local_sem)
        mine.start()
        first = [copy(0, me, sibling, src=src_ref)]
        first += [copy(1 + j, me, (*chip, c), src=src_ref) for j, chip in enumerate(chips)]
        for cp in first:
            cp.start()
        passed = [copy(4 + j, (*chip, c), sibling) for j, chip in enumerate(chips)]
        for j, chip in enumerate(chips):
            copy(1 + j, (*chip, c), me).wait_recv()
            passed[j].start()
        copy(0, sibling, me).wait_recv()
        for j, chip in enumerate(chips):
            copy(4 + j, (*chip, 1 - c), me).wait_recv()
        for cp in first + passed:
            cp.wait_send()
        mine.wait()

    return pl.pallas_call(
        body, name="all_gather", out_shape=jax.ShapeDtypeStruct((N_DEV, R, C), block.dtype),
        in_specs=[pl.BlockSpec(memory_space=pl.ANY)], out_specs=pl.BlockSpec(memory_space=pl.ANY),
        scratch_shapes=[pltpu.SemaphoreType.DMA((N_DEV - 1,)), pltpu.SemaphoreType.DMA((N_DEV - 1,)),
                        pltpu.SemaphoreType.DMA])(block)


N_CHIP = 4


def _sibling_swap(blocks):
    _, R, C = blocks.shape

    def body(src_ref, out_ref, send_sems, recv_sems):
        x, y, c = _me()
        copies = [pltpu.make_async_remote_copy(
            src_ref=src_ref.at[2 * j + 1 - c], dst_ref=out_ref.at[j], send_sem=send_sems.at[j],
            recv_sem=recv_sems.at[j], device_id=(x, y, 1 - c), device_id_type=MESH) for j in range(N_CHIP)]
        for cp in copies:
            cp.start()
        for cp in copies:
            cp.wait()

    return pl.pallas_call(
        body, name="sibling_swap", out_shape=jax.ShapeDtypeStruct((N_CHIP, R, C), blocks.dtype),
        in_specs=[pl.BlockSpec(memory_space=pl.ANY)], out_specs=pl.BlockSpec(memory_space=pl.ANY),
        scratch_shapes=[pltpu.SemaphoreType.DMA((N_CHIP,)), pltpu.SemaphoreType.DMA((N_CHIP,))])(blocks)


def _pair_sum(a, b):
    n, R, C = a.shape
    tr = 592 if R % 592 == 0 else R

    def body(a_ref, b_ref, o_ref):
        o_ref[...] = (a_ref[...].astype(F32) + b_ref[...].astype(F32)).astype(BF)

    spec = pl.BlockSpec((1, tr, C), lambda j, r: (j, r, 0))
    return pl.pallas_call(body, name="pair_sum", grid=(n, R // tr), in_specs=[spec, spec], out_specs=spec,
                          out_shape=jax.ShapeDtypeStruct(a.shape, BF), compiler_params=_params(2))(a, b)


def _chip_exchange(sums):
    _, R, C = sums.shape

    def body(src_ref, out_ref, send_sems, recv_sems, local_sem):
        x, y, c = _me()
        here = 2 * x + y
        mine = pltpu.make_async_copy(src_ref.at[here], out_ref.at[here], local_sem)
        mine.start()
        copies = []
        for k in range(1, N_CHIP):
            px, py = x ^ (k >> 1), y ^ (k & 1)
            copies.append(pltpu.make_async_remote_copy(
                src_ref=src_ref.at[2 * px + py], dst_ref=out_ref.at[here], send_sem=send_sems.at[k - 1],
                recv_sem=recv_sems.at[k - 1], device_id=(px, py, c), device_id_type=MESH))
        for cp in copies:
            cp.start()
        for cp in copies:
            cp.wait()
        mine.wait()

    return pl.pallas_call(
        body, name="chip_exchange", out_shape=jax.ShapeDtypeStruct((N_CHIP, R, C), sums.dtype),
        in_specs=[pl.BlockSpec(memory_space=pl.ANY)], out_specs=pl.BlockSpec(memory_space=pl.ANY),
        scratch_shapes=[pltpu.SemaphoreType.DMA((N_CHIP - 1,)), pltpu.SemaphoreType.DMA((N_CHIP - 1,)),
                        pltpu.SemaphoreType.DMA])(sums)


def _adamw_update(g, w, m, v):
    m_ = ADAM_B1 * m + (1.0 - ADAM_B1) * g
    v_ = ADAM_B2 * v + (1.0 - ADAM_B2) * (g * g)
    m_hat = m_ / (1.0 - ADAM_B1 ** ADAM_STEP)
    v_hat = v_ / (1.0 - ADAM_B2 ** ADAM_STEP)
    return -ADAM_LR * (m_hat / (jnp.sqrt(v_hat) + ADAM_EPS) + ADAM_WD * w), m_, v_


def _adamw(parts, state):
    _, R, C = state.shape
    n_parts = parts.shape[0]
    tr = 16
    assert R % tr == 0 and TAIL_ROW0 % tr == 0

    def body(p_ref, s_ref, o_ref):
        g = p_ref[0].astype(F32)
        for k in range(1, n_parts):
            g = g + p_ref[k].astype(F32)
        o_ref[0] = g
        o_ref[1], o_ref[2], o_ref[3] = _adamw_update(g, s_ref[0], s_ref[1], s_ref[2])

    return pl.pallas_call(
        body, name="adamw", grid=(R // tr,),
        in_specs=[pl.BlockSpec((n_parts, tr, C), lambda n: (0, n + TAIL_ROW0 // tr, 0)),
                  pl.BlockSpec((3, tr, C), lambda n: (0, n, 0))],
        out_specs=pl.BlockSpec((4, tr, C), lambda n: (0, n, 0)), out_shape=jax.ShapeDtypeStruct((4, R, C), F32),
        compiler_params=_params(1))(parts, state)


def _adamw_w_in(parts, w, m, v, core):
    n_parts = parts.shape[0]
    tc = 256

    def body(core_ref, p_ref, w_ref, m_ref, v_ref, o_ref):
        g_t = p_ref[0].astype(F32)
        for k in range(1, n_parts):
            g_t = g_t + p_ref[k].astype(F32)
        g_t = jnp.concatenate([g_t, jnp.zeros((512 - W_IN_WIN, tc), F32)], axis=0).T
        g = jnp.where(core_ref[0] == 0, g_t[:, 0:W_IN_COLS], g_t[:, W_IN_WIN - W_IN_COLS:W_IN_WIN])
        o_ref[0, 0] = g
        o_ref[1, 0], o_ref[2, 0], o_ref[3, 0] = _adamw_update(g, w_ref[0], m_ref[0], v_ref[0])

    nat = pl.BlockSpec((1, tc, W_IN_COLS), lambda l, j: (l, j, 0))
    return pl.pallas_call(
        body, name="adamw_w_in", grid=(DEPTH, D_MODEL // tc),
        in_specs=[pl.BlockSpec(memory_space=pltpu.SMEM),
                  pl.BlockSpec((n_parts, W_IN_WIN, tc), lambda l, j: (0, l, j)), nat, nat, nat],
        out_specs=pl.BlockSpec((4, 1, tc, W_IN_COLS), lambda l, j: (0, l, j, 0)),
        out_shape=jax.ShapeDtypeStruct((4, DEPTH, D_MODEL, W_IN_COLS), F32),
        compiler_params=_params(2))(core, parts, w, m, v)


def kernel(x, positions, norm_pre, w_in, attn_sinks, conv_w, conv_b, mla_q_norm, mla_w_uq, mla_kv_norm, mla_w_ukv, group_norm, w_out, norm_post, loss_target, m_norm_pre, m_w_in, m_attn_sinks, m_conv_w, m_conv_b, m_mla_q_norm, m_mla_w_uq, m_mla_kv_norm, m_mla_w_ukv, m_group_norm, m_w_out, m_norm_post, v_norm_pre, v_w_in, v_attn_sinks, v_conv_w, v_conv_b, v_mla_q_norm, v_mla_w_uq, v_mla_kv_norm, v_mla_w_ukv, v_group_norm, v_w_out, v_norm_post):
    local = dict(norm_pre=norm_pre, w_in=w_in, attn_sinks=attn_sinks, conv_w=conv_w, conv_b=conv_b,
                 mla_q_norm=mla_q_norm, mla_w_uq=mla_w_uq, mla_kv_norm=mla_kv_norm, mla_w_ukv=mla_w_ukv,
                 group_norm=group_norm, w_out=w_out, norm_post=norm_post)
    mom = dict(norm_pre=m_norm_pre, w_in=m_w_in, attn_sinks=m_attn_sinks, conv_w=m_conv_w, conv_b=m_conv_b,
               mla_q_norm=m_mla_q_norm, mla_w_uq=m_mla_w_uq, mla_kv_norm=m_mla_kv_norm, mla_w_ukv=m_mla_w_ukv,
               group_norm=m_group_norm, w_out=m_w_out, norm_post=m_norm_post)
    vel = dict(norm_pre=v_norm_pre, w_in=v_w_in, attn_sinks=v_attn_sinks, conv_w=v_conv_w, conv_b=v_conv_b,
               mla_q_norm=v_mla_q_norm, mla_w_uq=v_mla_w_uq, mla_kv_norm=v_mla_kv_norm, mla_w_ukv=v_mla_w_ukv,
               group_norm=v_group_norm, w_out=v_w_out, norm_post=v_norm_post)

    c = lax.axis_index("c")

    tile = 16
    slot_rows = 464
    shift = 8 * lax.axis_index("y") + 4 * c
    wt = lax.dynamic_update_slice(jnp.zeros((DEPTH, slot_rows, D_MODEL), BF),
                                  jnp.transpose(w_in, (0, 2, 1)).astype(BF), (0, shift, 0))
    payload = jnp.concatenate([wt.reshape(DEPTH * slot_rows, D_MODEL),
                               w_out.astype(BF).reshape(DEPTH * 128, D_MODEL),
                               _pack_small([local[n] for n in SMALL_SHARDED], BF)], axis=0)
    gathered = _all_gather(payload)
    W = {n: local[n] for n in REPLICATED}

    def nat_rows(l, lo, hi):
        def piece(d, r0, r1):
            base = slot_rows * l - (W_IN_COLS * d) // tile * tile
            return gathered[d, base + r0:base + r1]

        out, run = [], None
        for r0 in range(lo, hi, tile):
            d0, d1 = r0 // W_IN_COLS, (r0 + tile - 1) // W_IN_COLS
            if d0 == d1 and run is not None and run[0] == d0:
                run = (d0, run[1], r0 + tile)
                continue
            if run is not None:
                out.append(piece(*run))
                run = None
            if d0 == d1:
                run = (d0, r0, r0 + tile)
            else:
                out.append(piece(d0, r0, r0 + tile) + piece(d1, r0, r0 + tile))
        if run is not None:
            out.append(piece(*run))
        return out

    z = lambda n: [jnp.zeros((n, D_MODEL), BF)]
    W["wpt"] = [jnp.concatenate(nat_rows(l, 2464, 3488) + nat_rows(l, 0, 1664) + z(64) + nat_rows(l, 1664, 1696)
                                + z(96) + nat_rows(l, 1680, 1696) + nat_rows(l, 1664, 1680) + z(32)
                                + nat_rows(l, 1696, 2464), axis=0) for l in range(DEPTH)]
    wo0 = DEPTH * slot_rows
    W["w_out"] = gathered[:, wo0:wo0 + DEPTH * 128].reshape(N_DEV, DEPTH, 128, D_MODEL).transpose(1, 0, 2, 3).reshape(
        DEPTH, D_MODEL, D_MODEL)
    flat = gathered[:, wo0 + DEPTH * 128:].reshape(N_DEV, SMALL_ROWS * D_MODEL)
    off = 0
    for n in SMALL_SHARDED:
        depth, rows, width = local[n].shape
        size = depth * rows * width
        W[n] = flat[:, off:off + size].reshape(N_DEV, depth, rows, width).transpose(1, 2, 0, 3).reshape(
            depth, rows, N_DEV * width)
        off += size

    sq, grad_x, g = _local_step(x[0], positions[0], W, loss_target[0])
    loss = lax.psum(0.5 / D_MODEL * jnp.sum(sq), ("x", "y", "c"))

    cols = []
    for n in SMALL_SHARDED:
        depth, rows, width = local[n].shape
        cols.append(jnp.stack(g[n]).reshape(depth, rows, N_DEV, width).transpose(2, 0, 1, 3).reshape(N_DEV, -1))
    rep = jnp.concatenate([a.reshape(-1) for n in REPLICATED for a in g[n]])
    cols.append(jnp.broadcast_to(rep[None], (N_DEV, rep.shape[0])))
    small = jnp.concatenate(cols, axis=1)
    small = jnp.pad(small, ((0, 0), (0, SMALL_ROWS * D_MODEL - small.shape[1]))).reshape(N_DEV, SMALL_ROWS, D_MODEL)
    krs = [p[C_CKR + 64:C_CKR + 96] + _swap_rows32(p[C_CKRS + 64:C_CKRS + 96]) for p in g["w_in_pt"]]
    pieces = []
    for d in range(N_DEV):
        lo = W_IN_COLS * d // 8 * 8
        for l in range(DEPTH):
            pieces += _rows_of_w_in_t(lo, lo + W_IN_WIN, g["w_in_pt"][l], krs[l])
        pieces += [g["w_out"][l][128 * d:128 * (d + 1)] for l in range(DEPTH)]
        pieces.append(small[d])
    blocks = jnp.concatenate(pieces, axis=0).astype(BF).reshape(N_DEV, -1, D_MODEL)
    mine = lax.dynamic_index_in_dim(blocks.reshape(N_CHIP, 2, -1, D_MODEL), c, axis=1, keepdims=False)
    received = _chip_exchange(_pair_sum(mine, _sibling_swap(blocks)))

    out = _unpack_state(_adamw(received, _pack_state([local, mom, vel])), local)
    out["w_in"] = _adamw_w_in(received, w_in, m_w_in, v_w_in, c.astype(jnp.int32).reshape(1))
    return (loss, grad_x[None], *[out[n][t] for t in range(4) for n in ORDER])
```

```python
import functools

import jax
import jax.numpy as jnp
import numpy as np
from jax import lax
from jax.experimental import pallas as pl
from jax.experimental.pallas import tpu as pltpu

F32 = jnp.float32
BF = jnp.bfloat16
MESH = pl.DeviceIdType.MESH

D_MODEL = 1024
DEPTH = 2
EPS = 1e-6
N_DEV = 8
VMEM_LIMIT = 56 * 1024 * 1024
NEG = -1e30
MLA_SCALE = 96.0 ** -0.5
SB_SCALE = 0.125
LOG2E = 1.4426950408889634

NP = 3712
NF = 2432
NB = 1280
C_GATE = 0
C_BB = 1024
C_BC = 1280
C_BX = 1536
C_CQ = 1792
C_CKV = 2048
C_CKR = 2176
C_CKRS = 2304
C_AQ = 0
C_AK = 256
C_AV = 384
C_DQ = 512
C_DK = 768
C_DV = 1024
C_END = 3712

def _swap32(a):
    return jnp.concatenate([a[:, 16:32], a[:, 0:16]], axis=1)

ADAM_LR, ADAM_B1, ADAM_B2, ADAM_EPS, ADAM_WD, ADAM_STEP = 0.001, 0.9, 0.999, 1e-08, 0.01, 10


def _dot(a, b):
    return jnp.dot(a, b, preferred_element_type=F32)


def _dot_nt(a, b):
    return lax.dot_general(a, b, (((1,), (1,)), ((), ())), preferred_element_type=F32)


def _dot_tn(a, b):
    return lax.dot_general(a, b, (((0,), (0,)), ((), ())), preferred_element_type=F32)


def _params(n_grid):
    return pltpu.CompilerParams(dimension_semantics=("arbitrary",) * n_grid, vmem_limit_bytes=VMEM_LIMIT)


def _rms_fwd(x, g):
    r = lax.rsqrt(jnp.mean(x * x, axis=-1, keepdims=True) + EPS)
    return (x * r) * g, r


def _rms_bwd(x, g, r, dy, width=None):
    n = x.shape[-1] if width is None else width
    u = dy * g
    dx = r * u - x * (r * r * r) * (jnp.sum(x * u, axis=-1, keepdims=True) / n)
    return dx, dy * (x * r)


def _iota(shape, axis):
    return lax.broadcasted_iota(jnp.int32, shape, axis)


def _inproj_fwd(x, g, wpt):
    T = x.shape[0]
    tm = 256

    def body(x_ref, g_ref, w_ref, h32_ref, h16_ref, xn_ref):
        xn, _ = _rms_fwd(x_ref[...], g_ref[...])
        xn = xn.astype(BF)
        xn_ref[...] = xn
        h = _dot_nt(xn, w_ref[...])
        h32_ref[...] = h[:, :NF]
        h16_ref[...] = h[:, NF:].astype(BF)

    return pl.pallas_call(
        body, name="inproj_fwd", grid=(T // tm,),
        in_specs=[pl.BlockSpec((tm, D_MODEL), lambda n: (n, 0)),
                  pl.BlockSpec((1, D_MODEL), lambda n: (0, 0)),
                  pl.BlockSpec((NP, D_MODEL), lambda n: (0, 0))],
        out_specs=[pl.BlockSpec((tm, NF), lambda n: (n, 0)),
                   pl.BlockSpec((tm, NB), lambda n: (n, 0)),
                   pl.BlockSpec((tm, D_MODEL), lambda n: (n, 0))],
        out_shape=[jax.ShapeDtypeStruct((T, NF), F32), jax.ShapeDtypeStruct((T, NB), BF),
                   jax.ShapeDtypeStruct((T, D_MODEL), BF)],
        compiler_params=_params(1))(x, g, wpt)


def _inproj_bwd(parts, wpt, x, xn, g, dxo):
    T = x.shape[0]
    tm = 256
    np_ = len(parts)
    chunks = (0, 1280, 2560, NP)
    assert sum(p.shape[1] for p in parts) == C_END == NP

    def body(*refs):
        part_refs = refs[:np_]
        w_ref, x_ref, xn_ref, g_ref, dxo_ref, dw_ref, dx_ref, dg_ref = refs[np_:]
        n = pl.program_id(0)

        @pl.when(n == 0)
        def _():
            dw_ref[...] = jnp.zeros_like(dw_ref)
            dg_ref[...] = jnp.zeros_like(dg_ref)

        dh = jnp.concatenate([r[...].astype(BF) for r in part_refs], axis=1)
        xnv = xn_ref[...]
        for lo, hi in zip(chunks[:-1], chunks[1:]):
            dw_ref[:, lo:hi] += _dot_tn(xnv, dh[:, lo:hi])
        dxn = _dot(dh, w_ref[...])
        xv = x_ref[...]
        _, r = _rms_fwd(xv, g_ref[...])
        dx, dgt = _rms_bwd(xv, g_ref[...], r, dxn)
        dx_ref[...] = dxo_ref[...] + dx
        dg_ref[...] += jnp.sum(dgt, axis=0, keepdims=True)

    once = pl.Buffered(1)
    return pl.pallas_call(
        body, name="inproj_bwd", grid=(T // tm,),
        in_specs=[pl.BlockSpec((tm, p.shape[1]), lambda n: (n, 0)) for p in parts]
        + [pl.BlockSpec((NP, D_MODEL), lambda n: (0, 0), pipeline_mode=once),
           pl.BlockSpec((tm, D_MODEL), lambda n: (n, 0)),
           pl.BlockSpec((tm, D_MODEL), lambda n: (n, 0)),
           pl.BlockSpec((1, D_MODEL), lambda n: (0, 0)),
           pl.BlockSpec((tm, D_MODEL), lambda n: (n, 0))],
        out_specs=[pl.BlockSpec((D_MODEL, NP), lambda n: (0, 0), pipeline_mode=once),
                   pl.BlockSpec((tm, D_MODEL), lambda n: (n, 0)),
                   pl.BlockSpec((1, D_MODEL), lambda n: (0, 0))],
        out_shape=[jax.ShapeDtypeStruct((D_MODEL, NP), F32), jax.ShapeDtypeStruct((T, D_MODEL), F32),
                   jax.ShapeDtypeStruct((1, D_MODEL), F32)],
        compiler_params=_params(1))(*parts, wpt, x, xn, g, dxo)


SWA_BLK = 128
SWA_TQ = 1024


def _bdot_nt(a, b):
    return lax.dot_general(a, b, (((2,), (2,)), ((0,), (0,))), preferred_element_type=F32)


def _bdot(a, b):
    return lax.dot_general(a, b, (((2,), (1,)), ((0,), (0,))), preferred_element_type=F32)


def _bdot_tn(a, b):
    return lax.dot_general(a, b, (((1,), (1,)), ((0,), (0,))), preferred_element_type=F32)


def _swa_probs(q, kc, kp, sink, mask_c, mask_p):
    sc = jnp.where(mask_c, _bdot_nt(q, kc) * SB_SCALE, NEG)
    sp = jnp.where(mask_p, _bdot_nt(q, kp) * SB_SCALE, NEG)
    m = jnp.maximum(jnp.maximum(jnp.max(sc, axis=-1, keepdims=True), jnp.max(sp, axis=-1, keepdims=True)), sink)
    pc = jnp.exp(sc - m)
    pp = jnp.exp(sp - m)
    ps = jnp.exp(sink - m)
    inv = 1.0 / (jnp.sum(pc, axis=-1, keepdims=True) + jnp.sum(pp, axis=-1, keepdims=True) + ps)
    return pc * inv, pp * inv, ps * inv


def _swa_masks(n, nb):
    blk = _iota((nb, SWA_BLK, SWA_BLK), 0)
    row = _iota((nb, SWA_BLK, SWA_BLK), 1)
    col = _iota((nb, SWA_BLK, SWA_BLK), 2)
    return col <= row, jnp.logical_and(col > row, jnp.logical_or(blk > 0, n > 0))


def _swa_specs(tq):
    halo = tq // SWA_BLK
    return [pl.BlockSpec(memory_space=pltpu.SMEM),
            pl.BlockSpec((tq, 256), lambda n: (n, C_AQ // 256)),
            pl.BlockSpec((tq, 128), lambda n: (n, C_AK // 128)),
            pl.BlockSpec((SWA_BLK, 128), lambda n: (jnp.maximum(n * halo - 1, 0), C_AK // 128)),
            pl.BlockSpec((tq, 128), lambda n: (n, C_AV // 128)),
            pl.BlockSpec((SWA_BLK, 128), lambda n: (jnp.maximum(n * halo - 1, 0), C_AV // 128))]


def _swa_blocked(cur_ref, prev_ref, gs, nb):
    cur = cur_ref[:, gs].reshape(nb, SWA_BLK, 64)
    prev = jnp.concatenate([prev_ref[:, gs].reshape(1, SWA_BLK, 64), cur[:nb - 1]], axis=0) if nb > 1 \
        else prev_ref[:, gs].reshape(1, SWA_BLK, 64)
    return cur, prev


def _swa_fwd(h16, sinks):
    T = h16.shape[0]
    tq = SWA_TQ if T % SWA_TQ == 0 else SWA_BLK
    nb = tq // SWA_BLK

    def body(s_ref, q_ref, kc_ref, kp_ref, vc_ref, vp_ref, o_ref):
        n = pl.program_id(0)
        mask_c, mask_p = _swa_masks(n, nb)
        for h in range(4):
            hs = slice(h * 64, (h + 1) * 64)
            gs = slice(h // 2 * 64, (h // 2 + 1) * 64)
            kc, kp = _swa_blocked(kc_ref, kp_ref, gs, nb)
            vc, vp = _swa_blocked(vc_ref, vp_ref, gs, nb)
            pc, pp, _ = _swa_probs(q_ref[:, hs].reshape(nb, SWA_BLK, 64), kc, kp, s_ref[h], mask_c, mask_p)
            o_ref[:, hs] = (_bdot(pc.astype(BF), vc) + _bdot(pp.astype(BF), vp)).reshape(tq, 64)

    return pl.pallas_call(
        body, name="swa_fwd", grid=(T // tq,), in_specs=_swa_specs(tq),
        out_specs=pl.BlockSpec((tq, 256), lambda n: (n, 0)),
        out_shape=jax.ShapeDtypeStruct((T, 256), F32),
        compiler_params=_params(1))(sinks, h16, h16, h16, h16, h16)


def _swa_bwd(h16, sinks, dya):
    T = h16.shape[0]
    tq = SWA_TQ if T % SWA_TQ == 0 else SWA_BLK
    nb = tq // SWA_BLK

    def body(s_ref, q_ref, kc_ref, kp_ref, vc_ref, vp_ref, do_ref, dq_ref, dk_ref, dv_ref, ds_ref):
        n = pl.program_id(0)

        @pl.when(n == 0)
        def _():
            dk_ref[...] = jnp.zeros_like(dk_ref)
            dv_ref[...] = jnp.zeros_like(dv_ref)
            ds_ref[...] = jnp.zeros_like(ds_ref)

        mask_c, mask_p = _swa_masks(n, nb)
        rows = pl.ds(pl.multiple_of(n * tq, tq), tq)
        before = pl.ds(pl.multiple_of(jnp.maximum(n * nb - 1, 0) * SWA_BLK, SWA_BLK), SWA_BLK)
        lane = _iota((8, 128), 1)
        row8 = _iota((8, 128), 0)

        def to_keys(own, prev):
            if nb == 1:
                return own
            return own + jnp.concatenate([prev[1:], jnp.zeros((1, SWA_BLK, 64), F32)], axis=0)

        for h in range(4):
            hs = slice(h * 64, (h + 1) * 64)
            gs = slice(h // 2 * 64, (h // 2 + 1) * 64)
            q = q_ref[:, hs].reshape(nb, SWA_BLK, 64)
            kc, kp = _swa_blocked(kc_ref, kp_ref, gs, nb)
            vc, vp = _swa_blocked(vc_ref, vp_ref, gs, nb)
            pc, pp, ps = _swa_probs(q, kc, kp, s_ref[h], mask_c, mask_p)
            pcb, ppb = pc.astype(BF), pp.astype(BF)
            do = do_ref[:, hs].reshape(nb, SWA_BLK, 64)
            dob = do.astype(BF)
            o = _bdot(pcb, vc) + _bdot(ppb, vp)
            dd = jnp.sum(do * o, axis=-1, keepdims=True)
            dsc = (pc * (_bdot_nt(dob, vc) - dd) * SB_SCALE).astype(BF)
            dsp = (pp * (_bdot_nt(dob, vp) - dd) * SB_SCALE).astype(BF)
            dq_ref[:, hs] = (_bdot(dsc, kc) + _bdot(dsp, kp)).reshape(tq, 64).astype(BF)
            dkp, dvp = _bdot_tn(dsp, q), _bdot_tn(ppb, dob)
            dk_ref[rows, gs] += to_keys(_bdot_tn(dsc, q), dkp).reshape(tq, 64)
            dv_ref[rows, gs] += to_keys(_bdot_tn(pcb, dob), dvp).reshape(tq, 64)
            dk_ref[before, gs] += dkp[0]
            dv_ref[before, gs] += dvp[0]
            ds_ref[...] += jnp.where(jnp.logical_and(lane == h, row8 == 0), -jnp.sum(ps * dd), 0.0)

    return pl.pallas_call(
        body, name="swa_bwd", grid=(T // tq,),
        in_specs=_swa_specs(tq) + [pl.BlockSpec((tq, 256), lambda n: (n, 0))],
        out_specs=[pl.BlockSpec((tq, 256), lambda n: (n, 0)),
                   pl.BlockSpec((T, 128), lambda n: (0, 0)),
                   pl.BlockSpec((T, 128), lambda n: (0, 0)),
                   pl.BlockSpec((8, 128), lambda n: (0, 0))],
        out_shape=[jax.ShapeDtypeStruct((T, 256), BF), jax.ShapeDtypeStruct((T, 128), F32),
                   jax.ShapeDtypeStruct((T, 128), F32), jax.ShapeDtypeStruct((8, 128), F32)],
        compiler_params=_params(1))(sinks, h16, h16, h16, h16, h16, dya)


def _conv_u(bc_ref, bx_ref, bch_ref, bxh_ref, n, tm):
    u = bc_ref[...] * bx_ref[...]
    uh = bch_ref[...] * bxh_ref[...] * (n > 0).astype(F32)
    rowi = _iota((tm, 256), 0)
    u1 = jnp.where(rowi == 0, uh[7:8, :], pltpu.roll(u, 1, axis=0))
    u2 = jnp.where(rowi == 0, uh[6:7, :], jnp.where(rowi == 1, uh[7:8, :], pltpu.roll(u, 2, axis=0)))
    return u, u1, u2


def _conv_fwd(h32, cw, cb):
    T = h32.shape[0]
    tm = 512 if T % 512 == 0 else T
    hb = tm // 8

    def body(bb_ref, bc_ref, bx_ref, bch_ref, bxh_ref, w_ref, b_ref, o_ref):
        n = pl.program_id(0)
        u, u1, u2 = _conv_u(bc_ref, bx_ref, bch_ref, bxh_ref, n, tm)
        y = w_ref[0:1, :] * u2 + w_ref[1:2, :] * u1 + w_ref[2:3, :] * u + b_ref[...]
        o_ref[...] = bb_ref[...] * y

    halo = lambda c: pl.BlockSpec((8, 256), lambda n: (jnp.maximum(n * hb - 1, 0), c // 256))
    return pl.pallas_call(
        body, name="conv_fwd", grid=(T // tm,),
        in_specs=[pl.BlockSpec((tm, 256), lambda n: (n, C_BB // 256)),
                  pl.BlockSpec((tm, 256), lambda n: (n, C_BC // 256)),
                  pl.BlockSpec((tm, 256), lambda n: (n, C_BX // 256)),
                  halo(C_BC), halo(C_BX),
                  pl.BlockSpec((8, 256), lambda n: (0, 0)),
                  pl.BlockSpec((1, 256), lambda n: (0, 0))],
        out_specs=pl.BlockSpec((tm, 256), lambda n: (n, 0)),
        out_shape=jax.ShapeDtypeStruct((T, 256), F32),
        compiler_params=_params(1))(h32, h32, h32, h32, h32, cw, cb)


def _conv_bwd(h32, cw, cb, dyb):
    T = h32.shape[0]
    tm = 512 if T % 512 == 0 else T
    hb = tm // 8
    nt = T // tm

    def body(bb_ref, bc_ref, bx_ref, bch_ref, bxh_ref, bbn_ref, dy_ref, dyn_ref, w_ref, b_ref,
             dbb_ref, dbc_ref, dbx_ref, dw_ref):
        n = pl.program_id(0)

        @pl.when(n == 0)
        def _():
            dw_ref[...] = jnp.zeros_like(dw_ref)

        u, u1, u2 = _conv_u(bc_ref, bx_ref, bch_ref, bxh_ref, n, tm)
        w0, w1, w2 = w_ref[0:1, :], w_ref[1:2, :], w_ref[2:3, :]
        y = w0 * u2 + w1 * u1 + w2 * u + b_ref[...]
        dyb_ = dy_ref[...]
        dbb_ref[...] = (dyb_ * y).astype(BF)
        dy = dyb_ * bb_ref[...]
        dyn = dyn_ref[...] * bbn_ref[...] * (n < nt - 1).astype(F32)
        rowi = _iota((tm, 256), 0)
        dy1 = jnp.where(rowi == tm - 1, dyn[0:1, :], pltpu.roll(dy, tm - 1, axis=0))
        dy2 = jnp.where(rowi == tm - 2, dyn[0:1, :],
                        jnp.where(rowi == tm - 1, dyn[1:2, :], pltpu.roll(dy, tm - 2, axis=0)))
        du = w2 * dy + w1 * dy1 + w0 * dy2
        dbc_ref[...] = (du * bx_ref[...]).astype(BF)
        dbx_ref[...] = (du * bc_ref[...]).astype(BF)
        dw_ref[0:1, :] += jnp.sum(dy * u2, axis=0, keepdims=True)
        dw_ref[1:2, :] += jnp.sum(dy * u1, axis=0, keepdims=True)
        dw_ref[2:3, :] += jnp.sum(dy * u, axis=0, keepdims=True)
        dw_ref[3:4, :] += jnp.sum(dy, axis=0, keepdims=True)

    halo = lambda c: pl.BlockSpec((8, 256), lambda n: (jnp.maximum(n * hb - 1, 0), c // 256))
    nxt = lambda c: pl.BlockSpec((8, 256), lambda n: (jnp.minimum((n + 1) * hb, T // 8 - 1), c // 256))
    cur = lambda c: pl.BlockSpec((tm, 256), lambda n: (n, c // 256))
    return pl.pallas_call(
        body, name="conv_bwd", grid=(nt,),
        in_specs=[cur(C_BB), cur(C_BC), cur(C_BX), halo(C_BC), halo(C_BX), nxt(C_BB),
                  cur(0), nxt(0),
                  pl.BlockSpec((8, 256), lambda n: (0, 0)),
                  pl.BlockSpec((1, 256), lambda n: (0, 0))],
        out_specs=[cur(0), cur(0), cur(0), pl.BlockSpec((8, 256), lambda n: (0, 0))],
        out_shape=[jax.ShapeDtypeStruct((T, 256), BF)] * 3 + [jax.ShapeDtypeStruct((8, 256), F32)],
        compiler_params=_params(1))(h32, h32, h32, h32, h32, h32, dyb, dyb, cw, cb)


def _cprep_specs(tm):
    return [pl.BlockSpec((tm, 256), lambda n: (n, C_CQ // 256)),
            pl.BlockSpec((tm, 128), lambda n: (n, C_CKV // 128)),
            pl.BlockSpec((tm, 128), lambda n: (n, C_CKR // 128)),
            pl.BlockSpec((tm, 128), lambda n: (n, C_CKRS // 128)),
            pl.BlockSpec((1, 256), lambda n: (0, 0)),
            pl.BlockSpec((1, 128), lambda n: (0, 0)),
            pl.BlockSpec((tm, 128), lambda n: (n, 0)),
            pl.BlockSpec((tm, 128), lambda n: (n, 0))]


def _cprep_fwd(h32, gq, gkv, wuq2, wkv2, cosk, sin):
    T = h32.shape[0]
    tm = 512 if T % 512 == 0 else T

    def body(cq_ref, ckv_ref, ckr_ref, ckrs_ref, gq_ref, gkv_ref, cos_ref, sin_ref, wuq_ref, wkv_ref,
             q_ref, k_ref, v_ref):
        cosk_, sin_ = cos_ref[...], sin_ref[...]
        cosq = cosk_ + (_iota((tm, 128), 1) < 64).astype(F32)
        cqn, _ = _rms_fwd(cq_ref[...], gq_ref[...])
        q2 = _dot(cqn.astype(BF), wuq_ref[...])
        ckvn, _ = _rms_fwd(ckv_ref[...], gkv_ref[...])
        kv2 = _dot(ckvn.astype(BF), wkv_ref[...])
        kr = ckr_ref[...] * cosk_ + ckrs_ref[...] * sin_
        for h in range(4):
            hs = slice(h * 128, (h + 1) * 128)
            q_ref[:, hs] = ((q2[:, hs] * cosq + q2[:, 512 + h * 128:512 + (h + 1) * 128] * sin_) * MLA_SCALE).astype(BF)
            k_ref[:, hs] = (kv2[:, hs] + kr).astype(BF)
        ones = _iota((tm, 512), 1) % 128 == 64
        v_ref[...] = jnp.where(ones, 1.0, kv2[:, 512:]).astype(BF)

    return pl.pallas_call(
        body, name="cprep_fwd", grid=(T // tm,),
        in_specs=_cprep_specs(tm) + [pl.BlockSpec((256, 1024), lambda n: (0, 0)),
                                     pl.BlockSpec((128, 1024), lambda n: (0, 0))],
        out_specs=[pl.BlockSpec((tm, 512), lambda n: (n, 0))] * 3,
        out_shape=[jax.ShapeDtypeStruct((T, 512), BF)] * 3,
        compiler_params=_params(1))(h32, h32, h32, h32, gq, gkv, cosk, sin, wuq2, wkv2)


def _cprep_bwd(h32, gq, gkv, wuq2t, wkv2t, cosk, sin, dq, dk, dv):
    T = h32.shape[0]
    tm = 512 if T % 512 == 0 else T

    def body(cq_ref, ckv_ref, ckr_ref, ckrs_ref, gq_ref, gkv_ref, cos_ref, sin_ref, wuq_ref, wkv_ref,
             dq_ref, dk_ref, dv_ref,
             dcq_ref, dckv_ref, dckr_ref, dckrs_ref, dwuq_ref, dwkv_ref, dgq_ref, dgkv_ref):
        n = pl.program_id(0)

        @pl.when(n == 0)
        def _():
            dwuq_ref[...] = jnp.zeros_like(dwuq_ref)
            dwkv_ref[...] = jnp.zeros_like(dwkv_ref)
            dgq_ref[...] = jnp.zeros_like(dgq_ref)
            dgkv_ref[...] = jnp.zeros_like(dgkv_ref)

        cosk_, sin_ = cos_ref[...], sin_ref[...]
        cosq = cosk_ + (_iota((tm, 128), 1) < 64).astype(F32)
        dkr = jnp.zeros((tm, 128), F32)
        plain, swapped = [], []
        for h in range(4):
            hs = slice(h * 128, (h + 1) * 128)
            dqh = dq_ref[:, hs] * MLA_SCALE
            plain.append((dqh * cosq).astype(BF))
            swapped.append((dqh * sin_).astype(BF))
            dkr = dkr + dk_ref[:, hs]
        dq2 = jnp.concatenate(plain + swapped, axis=1)
        dkv2 = jnp.concatenate([dk_ref[...].astype(BF), dv_ref[...].astype(BF)], axis=1)
        dckr_ref[...] = (dkr * cosk_).astype(BF)
        dckrs_ref[...] = (dkr * sin_).astype(BF)

        cq, gq_ = cq_ref[...], gq_ref[...]
        cqn, rq = _rms_fwd(cq, gq_)
        dwuq_ref[...] += _dot_tn(cqn.astype(BF), dq2)
        dcq, dgt = _rms_bwd(cq, gq_, rq, _dot(dq2, wuq_ref[...]))
        dcq_ref[...] = dcq.astype(BF)
        dgq_ref[...] += jnp.sum(dgt, axis=0, keepdims=True)

        ckv, gkv_ = ckv_ref[...], gkv_ref[...]
        ckvn, rkv = _rms_fwd(ckv, gkv_)
        dwkv_ref[...] += _dot_tn(ckvn.astype(BF), dkv2)
        dckv, dgt2 = _rms_bwd(ckv, gkv_, rkv, _dot(dkv2, wkv_ref[...]))
        dckv_ref[...] = dckv.astype(BF)
        dgkv_ref[...] += jnp.sum(dgt2, axis=0, keepdims=True)

    row = lambda w: pl.BlockSpec((tm, w), lambda n: (n, 0))
    return pl.pallas_call(
        body, name="cprep_bwd", grid=(T // tm,),
        in_specs=_cprep_specs(tm) + [pl.BlockSpec((1024, 256), lambda n: (0, 0)),
                                     pl.BlockSpec((1024, 128), lambda n: (0, 0)),
                                     row(512), row(512), row(512)],
        out_specs=[row(256), row(128), row(128), row(128),
                   pl.BlockSpec((256, 1024), lambda n: (0, 0)), pl.BlockSpec((128, 1024), lambda n: (0, 0)),
                   pl.BlockSpec((1, 256), lambda n: (0, 0)), pl.BlockSpec((1, 128), lambda n: (0, 0))],
        out_shape=[jax.ShapeDtypeStruct((T, 256), BF), jax.ShapeDtypeStruct((T, 128), BF),
                   jax.ShapeDtypeStruct((T, 128), BF), jax.ShapeDtypeStruct((T, 128), BF),
                   jax.ShapeDtypeStruct((256, 1024), F32), jax.ShapeDtypeStruct((128, 1024), F32),
                   jax.ShapeDtypeStruct((1, 256), F32), jax.ShapeDtypeStruct((1, 128), F32)],
        compiler_params=_params(1))(h32, h32, h32, h32, gq, gkv, cosk, sin, wuq2t, wkv2t, dq, dk, dv)


MLA_TILE = 512
MLA_HEADS_PER_STEP = 4


def _causal_mask(t):
    return _iota((t, t), 1) <= _iota((t, t), 0)


def _mla_fwd(q, k, v):
    T = q.shape[0]
    tq = MLA_TILE

    def body(q_ref, k_ref, v_ref, o_ref, lse_ref):
        i = pl.program_id(1)
        mask = _causal_mask(tq)
        heads = [slice(128 * h, 128 * h + 128) for h in range(MLA_HEADS_PER_STEP)]
        qs = [q_ref[:, hs] for hs in heads]

        def step(j, carry, masked):
            rows = pl.ds(pl.multiple_of(j * tq, tq), tq)
            out = []
            for hh, hs in enumerate(heads):
                m, acc = carry[hh]
                s = _dot_nt(qs[hh], k_ref[rows, hs])
                if masked:
                    s = jnp.where(mask, s, NEG)
                m_new = jnp.maximum(m, jnp.max(s, axis=-1, keepdims=True))
                p = jnp.exp((s - m_new).astype(BF))
                acc = jnp.exp(m - m_new) * acc + _dot(p, v_ref[rows, hs])
                out.append((m_new, acc))
            return tuple(out)

        init = ((jnp.full((tq, 1), NEG, F32), jnp.zeros((tq, 128), F32)),) * len(heads)
        carry = lax.fori_loop(0, i // 2, lambda t, c: step(2 * t + 1, step(2 * t, c, False), False), init)
        carry = lax.cond(i % 2 == 1, lambda c: step(i - 1, c, False), lambda c: c, carry)
        carry = step(i, carry, True)
        for hh, hs in enumerate(heads):
            m, acc = carry[hh]
            l = acc[:, 64:65]
            o_ref[:, hs] = acc * (1.0 / l)
            lse_ref[:, hs] = jnp.broadcast_to(m + jnp.log(l), (tq, 128))

    width = 128 * MLA_HEADS_PER_STEP
    blk = pl.BlockSpec((tq, width), lambda h, i: (i, h))
    full = pl.BlockSpec((T, width), lambda h, i: (0, h))
    return pl.pallas_call(
        body, name="mla_fwd", grid=(4 // MLA_HEADS_PER_STEP, T // tq), in_specs=[blk, full, full],
        out_specs=[blk, blk],
        out_shape=[jax.ShapeDtypeStruct((T, 512), F32), jax.ShapeDtypeStruct((T, 512), F32)],
        compiler_params=_params(2))(q, k, v)


def _mla_bwd(q, k, v, o, lse, do):
    T = q.shape[0]
    tq = MLA_TILE

    def body(q_ref, k_ref, v_ref, o_ref, lse_ref, do_ref, dq_ref, dk_ref, dv_ref):
        i = pl.program_id(1)

        @pl.when(i == 0)
        def _():
            dk_ref[...] = jnp.zeros_like(dk_ref)
            dv_ref[...] = jnp.zeros_like(dv_ref)

        heads = [slice(0, 128), slice(128, 256)]
        mask = _causal_mask(tq)
        qs, dobs, dds, lses = [], [], [], []
        for hs in heads:
            do = do_ref[:, hs]
            qs.append(q_ref[:, hs])
            dobs.append(do.astype(BF))
            dds.append(jnp.sum(do * o_ref[:, hs], axis=-1, keepdims=True))
            lses.append(lse_ref[:, hs.start:hs.start + 1])

        def step(j, dqs, masked):
            rows = pl.ds(pl.multiple_of(j * tq, tq), tq)
            out = []
            for hh, hs in enumerate(heads):
                kj, vj = k_ref[rows, hs], v_ref[rows, hs]
                s = _dot_nt(qs[hh], kj)
                if masked:
                    s = jnp.where(mask, s, NEG)
                p = jnp.exp(s - lses[hh])
                ds = (p * (_dot_nt(dobs[hh], vj) - dds[hh])).astype(BF)
                dk_ref[rows, hs] += _dot_tn(ds, qs[hh])
                dv_ref[rows, hs] += _dot_tn(p.astype(BF), dobs[hh])
                out.append(dqs[hh] + _dot(ds, kj))
            return tuple(out)

        dqs = lax.fori_loop(0, i // 2, lambda t, c: step(2 * t + 1, step(2 * t, c, False), False),
                            (jnp.zeros((tq, 128), F32),) * 2)
        dqs = lax.cond(i % 2 == 1, lambda c: step(i - 1, c, False), lambda c: c, dqs)
        dqs = step(i, dqs, True)
        for hh, hs in enumerate(heads):
            dq_ref[:, hs] = dqs[hh]

    blk = pl.BlockSpec((tq, 256), lambda h, i: (i, h))
    full = pl.BlockSpec((T, 256), lambda h, i: (0, h), pipeline_mode=pl.Buffered(1))
    return pl.pallas_call(
        body, name="mla_bwd", grid=(2, T // tq), in_specs=[blk, full, full, blk, blk, blk],
        out_specs=[blk, full, full],
        out_shape=[jax.ShapeDtypeStruct((T, 512), F32)] * 3,
        compiler_params=_params(2))(q, k, v, o, lse, do)


def _sb_tile(qk, rr, strict, masked, upper):
    z2 = qk * (SB_SCALE * LOG2E)
    l1 = jnp.log2(1.0 + jnp.exp2(-jnp.abs(z2)))
    lk = -jnp.maximum(z2, 0.0) - l1
    if masked:
        lk = jnp.where(strict, lk, 0.0)
    after = rr + _dot(lk.astype(BF), upper)
    ll = jnp.minimum(z2, 0.0) - l1
    a = jnp.exp2(ll + after)
    if masked:
        a = jnp.where(strict, a, 0.0)
    return ll, a, jnp.sum(lk, axis=-1, keepdims=True)


SB_TQ, SB_TK = 256, 256
SB_DEAD = -160.0


def _sb_walk(trips, one_step, carry):
    def alive(c):
        t, cr = c
        top = functools.reduce(jnp.maximum, [jnp.max(h[0]) for h in cr])
        return jnp.logical_and(t < trips, top > SB_DEAD)

    def body(c):
        t, cr = c
        return t + 1, one_step(t, cr)

    return lax.while_loop(alive, body, (jnp.int32(0), carry))[1]


def _sb_consts(tq, tk):
    row, col = _iota((tq, tk), 0), _iota((tq, tk), 1)
    strict = [col + d * tk < row for d in range(tq // tk)]
    r2, c2 = _iota((tk, tk), 0), _iota((tk, tk), 1)
    return strict, (r2 > c2).astype(BF), (r2 < c2).astype(BF)


def _sb_fwd(h16):
    T = h16.shape[0]
    tq, tk = SB_TQ, SB_TK
    nd = tq // tk

    def body(q0_ref, q1_ref, k0_ref, k1_ref, v0_ref, v1_ref, o_ref):
        i = pl.program_id(0)
        strict, upper, _ = _sb_consts(tq, tk)
        lane = _iota((tq, 128), 1)
        pairs = [slice(0, 128), slice(128, 256)]
        k_refs, v_refs = (k0_ref, k1_ref), (v0_ref, v1_ref)
        qms = []
        for q_ref in (q0_ref, q1_ref):
            q2 = q_ref[...]
            qms += [jnp.where(lane < 64, q2, jnp.zeros_like(q2)), jnp.where(lane >= 64, q2, jnp.zeros_like(q2))]

        def step(j, carry, d):
            rows = pl.ds(pl.multiple_of(j * tk, tk), tk)
            out = []
            for h in range(4):
                rr, acc = carry[h]
                _, a, rs = _sb_tile(_dot_nt(qms[h], k_refs[h // 2][rows, :]), rr, None if d is None else strict[d],
                                    d is not None, upper)
                out.append((rr + rs, acc + _dot(a.astype(BF), v_refs[h // 2][rows, :])))
            return tuple(out)

        carry = ((jnp.zeros((tq, 1), F32), jnp.zeros((tq, 128), F32)),) * 4
        for d in reversed(range(nd)):
            carry = step(nd * i + d, carry, d)
        carry = _sb_walk(nd * i, lambda t, c: step(nd * i - 1 - t, c, None), carry)
        for p, ps in enumerate(pairs):
            o_ref[:, ps] = jnp.where(lane < 64, carry[2 * p][1], carry[2 * p + 1][1])

    return pl.pallas_call(
        body, name="sb_fwd", grid=(T // tq,),
        in_specs=[pl.BlockSpec((tq, 128), lambda i: (i, C_DQ // 128)),
                  pl.BlockSpec((tq, 128), lambda i: (i, C_DQ // 128 + 1)),
                  pl.BlockSpec((T, 128), lambda i: (0, C_DK // 128)),
                  pl.BlockSpec((T, 128), lambda i: (0, C_DK // 128 + 1)),
                  pl.BlockSpec((T, 128), lambda i: (0, C_DV // 128)),
                  pl.BlockSpec((T, 128), lambda i: (0, C_DV // 128 + 1))],
        out_specs=pl.BlockSpec((tq, 256), lambda i: (i, 0)),
        out_shape=jax.ShapeDtypeStruct((T, 256), F32),
        compiler_params=_params(1))(h16, h16, h16, h16, h16, h16)


def _sb_bwd(h16, yd, dyd):
    T = h16.shape[0]
    tq, tk = SB_TQ, SB_TK
    nd = tq // tk

    def body(q0_ref, q1_ref, k0_ref, k1_ref, v0_ref, v1_ref, o_ref, do_ref, dq_ref, dk_ref, dv_ref):
        i = pl.program_id(0)

        @pl.when(i == 0)
        def _():
            dk_ref[...] = jnp.zeros_like(dk_ref)
            dv_ref[...] = jnp.zeros_like(dv_ref)

        strict, upper, before = _sb_consts(tq, tk)
        lane = _iota((tq, 128), 1)
        lane_k = _iota((tk, 128), 1)
        pairs = [slice(0, 128), slice(128, 256)]
        k_refs, v_refs = (k0_ref, k1_ref), (v0_ref, v1_ref)
        q2s, dob2s, qms, doms, dds = [], [], [], [], []
        for p, q_ref in enumerate((q0_ref, q1_ref)):
            q2 = q_ref[...]
            dob2 = do_ref[:, pairs[p]].astype(BF)
            doo = dob2.astype(F32) * o_ref[:, pairs[p]]
            q2s.append(q2)
            dob2s.append(dob2)
            for mine in (lane < 64, lane >= 64):
                qms.append(jnp.where(mine, q2, jnp.zeros_like(q2)))
                doms.append(jnp.where(mine, dob2, jnp.zeros_like(dob2)))
                dds.append(jnp.sum(jnp.where(mine, doo, 0.0), axis=-1, keepdims=True))

        def step(j, carry, d):
            rows = pl.ds(pl.multiple_of(j * tk, tk), tk)
            out, dks, dvs = [], [], []
            for h in range(4):
                kj, vj = k_refs[h // 2][rows, :], v_refs[h // 2][rows, :]
                rr, sg, dq = carry[h]
                ll, a, rs = _sb_tile(_dot_nt(qms[h], kj), rr, None if d is None else strict[d], d is not None, upper)
                ab = a.astype(BF)
                g = _dot_nt(doms[h], vj) * ab.astype(F32)
                gs = jnp.sum(g, axis=-1, keepdims=True)
                pre = (dds[h] - sg - gs) + _dot(g.astype(BF), before)
                dz = g - jnp.exp2(ll) * (g + pre)
                if d is not None:
                    dz = jnp.where(strict[d], dz, 0.0)
                dzb = dz.astype(BF)
                dks.append(_dot_tn(dzb, q2s[h // 2]))
                dvs.append(_dot_tn(ab, dob2s[h // 2]))
                out.append((rr + rs, sg + gs, dq + _dot(dzb, kj)))
            for p, ps in enumerate(pairs):
                dk_ref[rows, ps] += jnp.where(lane_k < 64, dks[2 * p], dks[2 * p + 1]) * SB_SCALE
                dv_ref[rows, ps] += jnp.where(lane_k < 64, dvs[2 * p], dvs[2 * p + 1])
            return tuple(out)

        zero = jnp.zeros((tq, 1), F32)
        carry = ((zero, zero, jnp.zeros((tq, 128), F32)),) * 4
        for d in reversed(range(nd)):
            carry = step(nd * i + d, carry, d)
        carry = _sb_walk(nd * i, lambda t, c: step(nd * i - 1 - t, c, None), carry)
        for p, ps in enumerate(pairs):
            dq_ref[:, ps] = jnp.where(lane < 64, carry[2 * p][2], carry[2 * p + 1][2]) * SB_SCALE

    blk = lambda c: pl.BlockSpec((tq, 128), lambda i: (i, c // 128))
    full = lambda c: pl.BlockSpec((T, 128), lambda i: (0, c // 128))
    row = pl.BlockSpec((tq, 256), lambda i: (i, 0))
    acc = pl.BlockSpec((T, 256), lambda i: (0, 0))
    return pl.pallas_call(
        body, name="sb_bwd", grid=(T // tq,),
        in_specs=[blk(C_DQ), blk(C_DQ + 128), full(C_DK), full(C_DK + 128), full(C_DV), full(C_DV + 128), row, row],
        out_specs=[row, acc, acc],
        out_shape=[jax.ShapeDtypeStruct((T, 256), F32)] * 3,
        compiler_params=_params(1))(h16, h16, h16, h16, h16, h16, yd, dyd)


def _compact_c(ycp):
    return jnp.concatenate([ycp[:, h * 128:h * 128 + 64] for h in range(4)], axis=1)


def _post_fwd(ya, yb, ycp, yd, h32, ggrp, wout, gpost, x, tgt=None):
    T = x.shape[0]
    tm = 256
    last = tgt is not None

    def body(*refs):
        ya_ref, yb_ref, yc_ref, yd_ref, gate_ref, gg_ref, w_ref, gp_ref, x_ref = refs[:9]
        if last:
            t_ref, xn_ref, o_ref, sq_ref = refs[9:]
        else:
            xn_ref, o_ref = refs[9:]
        ys = [ya_ref[...], yb_ref[...], _compact_c(yc_ref[...]), yd_ref[...]]
        gate = gate_ref[...]
        sil = gate * (1.0 / (1.0 + jnp.exp(-gate)))
        parts = []
        for gi in range(4):
            ng, _ = _rms_fwd(ys[gi], gg_ref[:, gi * 256:(gi + 1) * 256])
            parts.append(ng * sil[:, gi * 256:(gi + 1) * 256])
        o = _dot(jnp.concatenate(parts, axis=1).astype(BF), w_ref[...])
        o_ref[...] = o
        on, _ = _rms_fwd(o, gp_ref[...])
        if last:
            @pl.when(pl.program_id(0) == 0)
            def _():
                sq_ref[...] = jnp.zeros_like(sq_ref)

            d = (x_ref[...] + on) - t_ref[...]
            sq_ref[...] += jnp.sum(d * d, axis=0, keepdims=True)
            xn_ref[...] = d * (1.0 / D_MODEL)
        else:
            xn_ref[...] = x_ref[...] + on

    row = lambda w: pl.BlockSpec((tm, w), lambda n: (n, 0))
    vec = pl.BlockSpec((1, 1024), lambda n: (0, 0))
    return pl.pallas_call(
        body, name="post_fwd", grid=(T // tm,),
        in_specs=[row(256), row(256), row(512), row(256), pl.BlockSpec((tm, 1024), lambda n: (n, C_GATE // 1024)),
                  vec, pl.BlockSpec((1024, 1024), lambda n: (0, 0)), vec, row(1024)] + ([row(1024)] if last else []),
        out_specs=[row(1024), row(1024)] + ([vec] if last else []),
        out_shape=[jax.ShapeDtypeStruct((T, 1024), F32), jax.ShapeDtypeStruct((T, 1024), F32)]
        + ([jax.ShapeDtypeStruct((1, 1024), F32)] if last else []),
        compiler_params=_params(1))(*([ya, yb, ycp, yd, h32, ggrp, wout, gpost, x] + ([tgt] if last else [])))


def _post_bwd(dx, o, gpost, woutt, ya, yb, ycp, yd, h32, ggrp):
    T = dx.shape[0]
    tm = 256

    def body(dx_ref, o_ref, gp_ref, w_ref, ya_ref, yb_ref, yc_ref, yd_ref, gate_ref, gg_ref,
             dw_ref, dya_ref, dyb_ref, dyc_ref, dyd_ref, dgate_ref, dgp_ref, dgg_ref):
        n = pl.program_id(0)

        @pl.when(n == 0)
        def _():
            dw_ref[...] = jnp.zeros_like(dw_ref)
            dgp_ref[...] = jnp.zeros_like(dgp_ref)
            dgg_ref[...] = jnp.zeros_like(dgg_ref)

        ov, gp = o_ref[...], gp_ref[...]
        _, ro = _rms_fwd(ov, gp)
        do, dgt = _rms_bwd(ov, gp, ro, dx_ref[...])
        dgp_ref[...] += jnp.sum(dgt, axis=0, keepdims=True)
        dob = do.astype(BF)
        gate = gate_ref[...]
        sg = 1.0 / (1.0 + jnp.exp(-gate))
        sil = gate * sg
        dsil = sg * (1.0 + gate * (1.0 - sg))
        ys = [ya_ref[...], yb_ref[...], _compact_c(yc_ref[...]), yd_ref[...]]
        normed = [_rms_fwd(ys[gi], gg_ref[:, gi * 256:(gi + 1) * 256]) for gi in range(4)]
        ym = jnp.concatenate([normed[gi][0] * sil[:, gi * 256:(gi + 1) * 256] for gi in range(4)], axis=1).astype(BF)
        dw_ref[...] += _dot_tn(ym, dob)
        dym = _dot(dob, w_ref[...])
        dys = []
        for gi in range(4):
            gs = slice(gi * 256, (gi + 1) * 256)
            gg = gg_ref[:, gs]
            ng, rg = normed[gi]
            dgate_ref[:, gs] = (dym[:, gs] * ng * dsil[:, gs]).astype(BF)
            dy, dgt2 = _rms_bwd(ys[gi], gg, rg, dym[:, gs] * sil[:, gs])
            dgg_ref[:, gs] += jnp.sum(dgt2, axis=0, keepdims=True)
            dys.append(dy)
        dya_ref[...] = dys[0]
        dyb_ref[...] = dys[1]
        dyd_ref[...] = dys[3]
        z64 = jnp.zeros((tm, 64), F32)
        dyc_ref[...] = jnp.concatenate(
            [piece for h in range(4) for piece in (dys[2][:, h * 64:(h + 1) * 64], z64)], axis=1)

    row = lambda w: pl.BlockSpec((tm, w), lambda n: (n, 0))
    vec = pl.BlockSpec((1, 1024), lambda n: (0, 0))
    return pl.pallas_call(
        body, name="post_bwd", grid=(T // tm,),
        in_specs=[row(1024), row(1024), vec, pl.BlockSpec((1024, 1024), lambda n: (0, 0)),
                  row(256), row(256), row(512), row(256),
                  pl.BlockSpec((tm, 1024), lambda n: (n, C_GATE // 1024)), vec],
        out_specs=[pl.BlockSpec((1024, 1024), lambda n: (0, 0)), row(256), row(256), row(512), row(256), row(1024),
                   vec, vec],
        out_shape=[jax.ShapeDtypeStruct((1024, 1024), F32), jax.ShapeDtypeStruct((T, 256), F32),
                   jax.ShapeDtypeStruct((T, 256), F32), jax.ShapeDtypeStruct((T, 512), F32),
                   jax.ShapeDtypeStruct((T, 256), F32), jax.ShapeDtypeStruct((T, 1024), BF),
                   jax.ShapeDtypeStruct((1, 1024), F32), jax.ShapeDtypeStruct((1, 1024), F32)],
        compiler_params=_params(1))(dx, o, gpost, woutt, ya, yb, ycp, yd, h32, ggrp)


def _swap_rows32(a):
    return jnp.concatenate([a[16:32], a[0:16]], axis=0)


def _pad_w_uq(w):
    z = lambda n: jnp.zeros((w.shape[0], n), w.dtype)
    a = [p for h in range(4) for p in (w[:, 96 * h:96 * h + 96], z(32))]
    b = [p for h in range(4) for p in (z(64), _swap32(w[:, 96 * h + 64:96 * h + 96]), z(32))]
    return jnp.concatenate(a + b, axis=1)


def _unpad_w_uq(d):
    out = []
    for h in range(4):
        out.append(d[:, 128 * h:128 * h + 64])
        out.append(d[:, 128 * h + 64:128 * h + 96] + _swap32(d[:, 512 + 128 * h + 64:512 + 128 * h + 96]))
    return jnp.concatenate(out, axis=1)


def _pad_w_ukv(w):
    z = jnp.zeros((w.shape[0], 64), w.dtype)
    a = [p for h in range(4) for p in (w[:, 128 * h:128 * h + 64], z)]
    b = [p for h in range(4) for p in (w[:, 128 * h + 64:128 * h + 128], z)]
    return jnp.concatenate(a + b, axis=1)


def _unpad_w_ukv(d):
    return jnp.concatenate([p for h in range(4) for p in (d[:, 128 * h:128 * h + 64],
                                                          d[:, 512 + 128 * h:512 + 128 * h + 64])], axis=1)


def _rope_tables(pos):
    freqs = 10000.0 ** (-jnp.arange(16, dtype=F32) / 16)
    ang = pos.astype(F32)[:, None] * freqs
    c, s = jnp.cos(ang), jnp.sin(ang)
    z = lambda n: jnp.zeros((pos.shape[0], n), F32)
    return (jnp.concatenate([z(64), c, c, z(32)], axis=1), jnp.concatenate([z(64), -s, s, z(32)], axis=1))


def _layer_weights(W, l):
    wuq2 = _pad_w_uq(W["mla_w_uq"][l])
    wkv2 = _pad_w_ukv(W["mla_w_ukv"][l])
    wout = W["w_out"][l]
    cw = jnp.concatenate([W["conv_w"][l].astype(F32), jnp.zeros((5, 256), F32)], axis=0)
    return dict(
        wpt=W["wpt"][l], wuq2=wuq2.astype(BF), wuq2t=wuq2.T.astype(BF),
        wkv2=wkv2.astype(BF), wkv2t=wkv2.T.astype(BF), wout=wout.astype(BF), woutt=wout.T.astype(BF),
        cw=cw, cb=W["conv_b"][l][None, :], sinks=W["attn_sinks"][l],
        gpre=W["norm_pre"][l][None, :], gq=W["mla_q_norm"][l][None, :], gkv=W["mla_kv_norm"][l][None, :],
        ggrp=W["group_norm"][l][None, :], gpost=W["norm_post"][l][None, :])


def _local_step(x, pos, W, tgt):
    cosk, sin = _rope_tables(pos)
    saved = []
    for l in range(DEPTH):
        lw = _layer_weights(W, l)
        h32, h16, xn = _inproj_fwd(x, lw["gpre"], lw["wpt"])
        ya = _swa_fwd(h16, lw["sinks"])
        yb = _conv_fwd(h32, lw["cw"], lw["cb"])
        qc, kc, vc = _cprep_fwd(h32, lw["gq"], lw["gkv"], lw["wuq2"], lw["wkv2"], cosk, sin)
        ycp, lse = _mla_fwd(qc, kc, vc)
        yd = _sb_fwd(h16)
        if l < DEPTH - 1:
            x_new, o = _post_fwd(ya, yb, ycp, yd, h32, lw["ggrp"], lw["wout"], lw["gpost"], x)
        else:
            dx, o, sq = _post_fwd(ya, yb, ycp, yd, h32, lw["ggrp"], lw["wout"], lw["gpost"], x, tgt)
        saved.append(dict(lw=lw, x=x, h32=h32, h16=h16, xn=xn, ya=ya, yb=yb, qc=qc, kc=kc, vc=vc, ycp=ycp,
                          lse=lse, yd=yd, o=o))
        if l < DEPTH - 1:
            x = x_new

    grads = {k: [None] * DEPTH for k in ("norm_pre", "w_in_pt", "attn_sinks", "conv_w", "conv_b", "mla_q_norm",
                                         "mla_w_uq", "mla_kv_norm", "mla_w_ukv", "group_norm", "w_out",
                                         "norm_post")}
    for l in reversed(range(DEPTH)):
        s = saved[l]
        lw = s["lw"]
        dwout, dya, dyb, dycp, dyd, dgate, dgpost, dggrp = _post_bwd(
            dx, s["o"], lw["gpost"], lw["woutt"], s["ya"], s["yb"], s["ycp"], s["yd"], s["h32"], lw["ggrp"])
        grads["norm_post"][l] = dgpost[0]
        grads["group_norm"][l] = dggrp[0]
        grads["w_out"][l] = dwout
        sdq, sdk, sdv = _sb_bwd(s["h16"], s["yd"], dyd)
        mdq, mdk, mdv = _mla_bwd(s["qc"], s["kc"], s["vc"], s["ycp"], s["lse"], dycp)
        dcq, dckv, dckr, dckrs, dwuq2, dwkv2, dgq, dgkv = _cprep_bwd(
            s["h32"], lw["gq"], lw["gkv"], lw["wuq2t"], lw["wkv2t"], cosk, sin, mdq, mdk, mdv)
        grads["mla_q_norm"][l] = dgq[0]
        grads["mla_kv_norm"][l] = dgkv[0]
        grads["mla_w_uq"][l] = _unpad_w_uq(dwuq2)
        grads["mla_w_ukv"][l] = _unpad_w_ukv(dwkv2)
        dbb, dbc, dbx, dcw = _conv_bwd(s["h32"], lw["cw"], lw["cb"], dyb)
        grads["conv_w"][l] = dcw[0:3]
        grads["conv_b"][l] = dcw[3]
        adq, adk, adv, dsk = _swa_bwd(s["h16"], lw["sinks"], dya)
        grads["attn_sinks"][l] = dsk[0, 0:4]
        parts = [dgate, dbb, dbc, dbx, dcq, dckv, dckr, dckrs, adq, adk, adv, sdq, sdk, sdv]
        dwp, dx, dgpre = _inproj_bwd(parts, lw["wpt"], s["x"], s["xn"], lw["gpre"], dx)
        grads["w_in_pt"][l] = dwp.T
        grads["norm_pre"][l] = dgpre[0]
    return sq, dx, grads


SMALL_SHARDED = ("conv_w", "mla_w_uq", "mla_w_ukv")
REPLICATED = ("norm_pre", "attn_sinks", "conv_b", "mla_q_norm", "mla_kv_norm", "group_norm", "norm_post")
ORDER = ("norm_pre", "w_in", "attn_sinks", "conv_w", "conv_b", "mla_q_norm", "mla_w_uq", "mla_kv_norm",
         "mla_w_ukv", "group_norm", "w_out", "norm_post")
W_IN_COLS = 436
W_IN_WIN = 440
SMALL_ROWS = 48


def _pack_small(arrs, dtype):
    flat = jnp.concatenate([a.reshape(-1).astype(dtype) for a in arrs])
    flat = jnp.concatenate([flat, jnp.zeros((SMALL_ROWS * D_MODEL - flat.shape[0],), dtype)])
    return flat.reshape(SMALL_ROWS, D_MODEL)


TAIL_ROW0 = DEPTH * W_IN_WIN


def _pack_state(ps):
    k = len(ps)
    wout = jnp.stack([p["w_out"] for p in ps]).reshape(k, DEPTH * 128, D_MODEL)
    flat = jnp.stack([jnp.concatenate([p[n].reshape(-1) for n in SMALL_SHARDED + REPLICATED]) for p in ps])
    small = jnp.pad(flat, ((0, 0), (0, SMALL_ROWS * D_MODEL - flat.shape[1]))).reshape(k, SMALL_ROWS, D_MODEL)
    return jnp.concatenate([wout, small], axis=1)


def _unpack_state(buf, p):
    k = buf.shape[0]
    out = {"w_out": buf[:, 0:DEPTH * 128].reshape(k, DEPTH, 128, D_MODEL)}
    flat = buf[:, DEPTH * 128:].reshape(k, SMALL_ROWS * D_MODEL)
    off = 0
    for n in SMALL_SHARDED + REPLICATED:
        size = int(np.prod(p[n].shape))
        out[n] = flat[:, off:off + size].reshape((k,) + p[n].shape)
        off += size
    return out


def _rows_of_w_in_t(lo, hi, padded, kr):
    segs = ((0, 512, padded, NF + C_AQ), (512, 1664, padded, C_BB), (1664, 1696, kr, 0),
            (1696, 2464, padded, NF + C_DQ), (2464, 3488, padded, C_GATE))
    out = []
    for s0, s1, src, base in segs:
        a, b = max(lo, s0), min(hi, s1)
        if a < b:
            out.append(src[base + a - s0:base + b - s0])
    return out


def _me():
    return lax.axis_index("x"), lax.axis_index("y"), lax.axis_index("c")


def _all_gather(block):
    R, C = block.shape

    def body(src_ref, out_ref, send_sems, recv_sems, local_sem):
        x, y, c = _me()
        me, sibling = (x, y, c), (x, y, 1 - c)
        chips = [(1 - x, y), (x, 1 - y), (1 - x, 1 - y)]

        def slot(px, py, pc):
            return out_ref.at[4 * px + 2 * py + pc]

        def copy(k, block, to, src=None):
            return pltpu.make_async_remote_copy(
                src_ref=slot(*block) if src is None else src, dst_ref=slot(*block), send_sem=send_sems.at[k],
                recv_sem=recv_sems.at[k], device_id=to, device_id_type=MESH)

        mine = pltpu.make_async_copy(src_ref, slot(*me), local_sem)
        mine.start()
        first = [copy(0, me, sibling, src=src_ref)]
        first += [copy(1 + j, me, (*chip, c), src=src_ref) for j, chip in enumerate(chips)]
        for cp in first:
            cp.start()
        passed = [copy(4 + j, (*chip, c), sibling) for j, chip in enumerate(chips)]
        for j, chip in enumerate(chips):
            copy(1 + j, (*chip, c), me).wait_recv()
            passed[j].start()
        copy(0, sibling, me).wait_recv()
        for j, chip in enumerate(chips):
            copy(4 + j, (*chip, 1 - c), me).wait_recv()
        for cp in first + passed:
            cp.wait_send()
        mine.wait()

    return pl.pallas_call(
        body, name="all_gather", out_shape=jax.ShapeDtypeStruct((N_DEV, R, C), block.dtype),
        in_specs=[pl.BlockSpec(memory_space=pl.ANY)], out_specs=pl.BlockSpec(memory_space=pl.ANY),
        scratch_shapes=[pltpu.SemaphoreType.DMA((N_DEV - 1,)), pltpu.SemaphoreType.DMA((N_DEV - 1,)),
                        pltpu.SemaphoreType.DMA])(block)


N_CHIP = 4


def _sibling_swap(blocks):
    _, R, C = blocks.shape

    def body(src_ref, out_ref, send_sems, recv_sems):
        x, y, c = _me()
        copies = [pltpu.make_async_remote_copy(
            src_ref=src_ref.at[2 * j + 1 - c], dst_ref=out_ref.at[j], send_sem=send_sems.at[j],
            recv_sem=recv_sems.at[j], device_id=(x, y, 1 - c), device_id_type=MESH) for j in range(N_CHIP)]
        for cp in copies:
            cp.start()
        for cp in copies:
            cp.wait()

    return pl.pallas_call(
        body, name="sibling_swap", out_shape=jax.ShapeDtypeStruct((N_CHIP, R, C), blocks.dtype),
        in_specs=[pl.BlockSpec(memory_space=pl.ANY)], out_specs=pl.BlockSpec(memory_space=pl.ANY),
        scratch_shapes=[pltpu.SemaphoreType.DMA((N_CHIP,)), pltpu.SemaphoreType.DMA((N_CHIP,))])(blocks)


def _pair_sum(a, b):
    n, R, C = a.shape
    tr = 592 if R % 592 == 0 else R

    def body(a_ref, b_ref, o_ref):
        o_ref[...] = (a_ref[...].astype(F32) + b_ref[...].astype(F32)).astype(BF)

    spec = pl.BlockSpec((1, tr, C), lambda j, r: (j, r, 0))
    return pl.pallas_call(body, name="pair_sum", grid=(n, R // tr), in_specs=[spec, spec], out_specs=spec,
                          out_shape=jax.ShapeDtypeStruct(a.shape, BF), compiler_params=_params(2))(a, b)


def _chip_exchange(sums):
    _, R, C = sums.shape

    def body(src_ref, out_ref, send_sems, recv_sems, local_sem):
        x, y, c = _me()
        here = 2 * x + y
        mine = pltpu.make_async_copy(src_ref.at[here], out_ref.at[here], local_sem)
        mine.start()
        copies = []
        for k in range(1, N_CHIP):
            px, py = x ^ (k >> 1), y ^ (k & 1)
            copies.append(pltpu.make_async_remote_copy(
                src_ref=src_ref.at[2 * px + py], dst_ref=out_ref.at[here], send_sem=send_sems.at[k - 1],
                recv_sem=recv_sems.at[k - 1], device_id=(px, py, c), device_id_type=MESH))
        for cp in copies:
            cp.start()
        for cp in copies:
            cp.wait()
        mine.wait()

    return pl.pallas_call(
        body, name="chip_exchange", out_shape=jax.ShapeDtypeStruct((N_CHIP, R, C), sums.dtype),
        in_specs=[pl.BlockSpec(memory_space=pl.ANY)], out_specs=pl.BlockSpec(memory_space=pl.ANY),
        scratch_shapes=[pltpu.SemaphoreType.DMA((N_CHIP - 1,)), pltpu.SemaphoreType.DMA((N_CHIP - 1,)),
                        pltpu.SemaphoreType.DMA])(sums)


def _adamw_update(g, w, m, v):
    m_ = ADAM_B1 * m + (1.0 - ADAM_B1) * g
    v_ = ADAM_B2 * v + (1.0 - ADAM_B2) * (g * g)
    m_hat = m_ / (1.0 - ADAM_B1 ** ADAM_STEP)
    v_hat = v_ / (1.0 - ADAM_B2 ** ADAM_STEP)
    return -ADAM_LR * (m_hat / (jnp.sqrt(v_hat) + ADAM_EPS) + ADAM_WD * w), m_, v_


def _adamw(parts, state):
    _, R, C = state.shape
    n_parts = parts.shape[0]
    tr = 16
    assert R % tr == 0 and TAIL_ROW0 % tr == 0

    def body(p_ref, s_ref, o_ref):
        g = p_ref[0].astype(F32)
        for k in range(1, n_parts):
            g = g + p_ref[k].astype(F32)
        o_ref[0] = g
        o_ref[1], o_ref[2], o_ref[3] = _adamw_update(g, s_ref[0], s_ref[1], s_ref[2])

    return pl.pallas_call(
        body, name="adamw", grid=(R // tr,),
        in_specs=[pl.BlockSpec((n_parts, tr, C), lambda n: (0, n + TAIL_ROW0 // tr, 0)),
                  pl.BlockSpec((3, tr, C), lambda n: (0, n, 0))],
        out_specs=pl.BlockSpec((4, tr, C), lambda n: (0, n, 0)), out_shape=jax.ShapeDtypeStruct((4, R, C), F32),
        compiler_params=_params(1))(parts, state)


def _adamw_w_in(parts, w, m, v, core):
    n_parts = parts.shape[0]
    tc = 256

    def body(core_ref, p_ref, w_ref, m_ref, v_ref, o_ref):
        g_t = p_ref[0].astype(F32)
        for k in range(1, n_parts):
            g_t = g_t + p_ref[k].astype(F32)
        g_t = jnp.concatenate([g_t, jnp.zeros((512 - W_IN_WIN, tc), F32)], axis=0).T
        g = jnp.where(core_ref[0] == 0, g_t[:, 0:W_IN_COLS], g_t[:, W_IN_WIN - W_IN_COLS:W_IN_WIN])
        o_ref[0, 0] = g
        o_ref[1, 0], o_ref[2, 0], o_ref[3, 0] = _adamw_update(g, w_ref[0], m_ref[0], v_ref[0])

    nat = pl.BlockSpec((1, tc, W_IN_COLS), lambda l, j: (l, j, 0))
    return pl.pallas_call(
        body, name="adamw_w_in", grid=(DEPTH, D_MODEL // tc),
        in_specs=[pl.BlockSpec(memory_space=pltpu.SMEM),
                  pl.BlockSpec((n_parts, W_IN_WIN, tc), lambda l, j: (0, l, j)), nat, nat, nat],
        out_specs=pl.BlockSpec((4, 1, tc, W_IN_COLS), lambda l, j: (0, l, j, 0)),
        out_shape=jax.ShapeDtypeStruct((4, DEPTH, D_MODEL, W_IN_COLS), F32),
        compiler_params=_params(2))(core, parts, w, m, v)


def kernel(x, positions, norm_pre, w_in, attn_sinks, conv_w, conv_b, mla_q_norm, mla_w_uq, mla_kv_norm, mla_w_ukv, group_norm, w_out, norm_post, loss_target, m_norm_pre, m_w_in, m_attn_sinks, m_conv_w, m_conv_b, m_mla_q_norm, m_mla_w_uq, m_mla_kv_norm, m_mla_w_ukv, m_group_norm, m_w_out, m_norm_post, v_norm_pre, v_w_in, v_attn_sinks, v_conv_w, v_conv_b, v_mla_q_norm, v_mla_w_uq, v_mla_kv_norm, v_mla_w_ukv, v_group_norm, v_w_out, v_norm_post):
    local = dict(norm_pre=norm_pre, w_in=w_in, attn_sinks=attn_sinks, conv_w=conv_w, conv_b=conv_b,
                 mla_q_norm=mla_q_norm, mla_w_uq=mla_w_uq, mla_kv_norm=mla_kv_norm, mla_w_ukv=mla_w_ukv,
                 group_norm=group_norm, w_out=w_out, norm_post=norm_post)
    mom = dict(norm_pre=m_norm_pre, w_in=m_w_in, attn_sinks=m_attn_sinks, conv_w=m_conv_w, conv_b=m_conv_b,
               mla_q_norm=m_mla_q_norm, mla_w_uq=m_mla_w_uq, mla_kv_norm=m_mla_kv_norm, mla_w_ukv=m_mla_w_ukv,
               group_norm=m_group_norm, w_out=m_w_out, norm_post=m_norm_post)
    vel = dict(norm_pre=v_norm_pre, w_in=v_w_in, attn_sinks=v_attn_sinks, conv_w=v_conv_w, conv_b=v_conv_b,
               mla_q_norm=v_mla_q_norm, mla_w_uq=v_mla_w_uq, mla_kv_norm=v_mla_kv_norm, mla_w_ukv=v_mla_w_ukv,
               group_norm=v_group_norm, w_out=v_w_out, norm_post=v_norm_post)

    c = lax.axis_index("c")

    tile = 16
    slot_rows = 464
    shift = 8 * lax.axis_index("y") + 4 * c
    wt = lax.dynamic_update_slice(jnp.zeros((DEPTH, slot_rows, D_MODEL), BF),
                                  jnp.transpose(w_in, (0, 2, 1)).astype(BF), (0, shift, 0))
    payload = jnp.concatenate([wt.reshape(DEPTH * slot_rows, D_MODEL),
                               w_out.astype(BF).reshape(DEPTH * 128, D_MODEL),
                               _pack_small([local[n] for n in SMALL_SHARDED], BF)], axis=0)
    gathered = _all_gather(payload)
    W = {n: local[n] for n in REPLICATED}

    def nat_rows(l, lo, hi):
        def piece(d, r0, r1):
            base = slot_rows * l - (W_IN_COLS * d) // tile * tile
            return gathered[d, base + r0:base + r1]

        out, run = [], None
        for r0 in range(lo, hi, tile):
            d0, d1 = r0 // W_IN_COLS, (r0 + tile - 1) // W_IN_COLS
            if d0 == d1 and run is not None and run[0] == d0:
                run = (d0, run[1], r0 + tile)
                continue
            if run is not None:
                out.append(piece(*run))
                run = None
            if d0 == d1:
                run = (d0, r0, r0 + tile)
            else:
                out.append(piece(d0, r0, r0 + tile) + piece(d1, r0, r0 + tile))
        if run is not None:
            out.append(piece(*run))
        return out

    z = lambda n: [jnp.zeros((n, D_MODEL), BF)]
    W["wpt"] = [jnp.concatenate(nat_rows(l, 2464, 3488) + nat_rows(l, 512, 1664) + z(64) + nat_rows(l, 1664, 1696)
                                + z(96) + nat_rows(l, 1680, 1696) + nat_rows(l, 1664, 1680) + z(32)
                                + nat_rows(l, 0, 512) + nat_rows(l, 1696, 2464), axis=0) for l in range(DEPTH)]
    wo0 = DEPTH * slot_rows
    W["w_out"] = gathered[:, wo0:wo0 + DEPTH * 128].reshape(N_DEV, DEPTH, 128, D_MODEL).transpose(1, 0, 2, 3).reshape(
        DEPTH, D_MODEL, D_MODEL)
    flat = gathered[:, wo0 + DEPTH * 128:].reshape(N_DEV, SMALL_ROWS * D_MODEL)
    off = 0
    for n in SMALL_SHARDED:
        depth, rows, width = local[n].shape
        size = depth * rows * width
        W[n] = flat[:, off:off + size].reshape(N_DEV, depth, rows, width).transpose(1, 2, 0, 3).reshape(
            depth, rows, N_DEV * width)
        off += size

    sq, grad_x, g = _local_step(x[0], positions[0], W, loss_target[0])
    loss = lax.psum(0.5 / D_MODEL * jnp.sum(sq), ("x", "y", "c"))

    cols = []
    for n in SMALL_SHARDED:
        depth, rows, width = local[n].shape
        cols.append(jnp.stack(g[n]).reshape(depth, rows, N_DEV, width).transpose(2, 0, 1, 3).reshape(N_DEV, -1))
    rep = jnp.concatenate([a.reshape(-1) for n in REPLICATED for a in g[n]])
    cols.append(jnp.broadcast_to(rep[None], (N_DEV, rep.shape[0])))
    small = jnp.concatenate(cols, axis=1)
    small = jnp.pad(small, ((0, 0), (0, SMALL_ROWS * D_MODEL - small.shape[1]))).reshape(N_DEV, SMALL_ROWS, D_MODEL)
    krs = [p[C_CKR + 64:C_CKR + 96] + _swap_rows32(p[C_CKRS + 64:C_CKRS + 96]) for p in g["w_in_pt"]]
    pieces = []
    for d in range(N_DEV):
        lo = W_IN_COLS * d // 8 * 8
        for l in range(DEPTH):
            pieces += _rows_of_w_in_t(lo, lo + W_IN_WIN, g["w_in_pt"][l], krs[l])
        pieces += [g["w_out"][l][128 * d:128 * (d + 1)] for l in range(DEPTH)]
        pieces.append(small[d])
    blocks = jnp.concatenate(pieces, axis=0).astype(BF).reshape(N_DEV, -1, D_MODEL)
    mine = lax.dynamic_index_in_dim(blocks.reshape(N_CHIP, 2, -1, D_MODEL), c, axis=1, keepdims=False)
    received = _chip_exchange(_pair_sum(mine, _sibling_swap(blocks)))

    out = _unpack_state(_adamw(received, _pack_state([local, mom, vel])), local)
    out["w_in"] = _adamw_w_in(received, w_in, m_w_in, v_w_in, c.astype(jnp.int32).reshape(1))
    return (loss, grad_x[None], *[out[n][t] for t in range(4) for n in ORDER])
```

```python
import functools

import jax
import jax.numpy as jnp
import numpy as np
from jax import lax
from jax.experimental import pallas as pl
from jax.experimental.pallas import tpu as pltpu

F32 = jnp.float32
BF = jnp.bfloat16
MESH = pl.DeviceIdType.MESH

D_MODEL = 1024
DEPTH = 2
EPS = 1e-6
N_DEV = 8
VMEM_LIMIT = 56 * 1024 * 1024
NEG = -1e30
MLA_SCALE = 96.0 ** -0.5
SB_SCALE = 0.125
LOG2E = 1.4426950408889634

NP = 3712
NF = 2432
NB = 1280
C_GATE = 0
C_BB = 1024
C_BC = 1280
C_BX = 1536
C_CQ = 1792
C_CKV = 2048
C_CKR = 2176
C_CKRS = 2304
C_AQ = 0
C_AK = 256
C_AV = 384
C_DQ = 512
C_DK = 768
C_DV = 1024
C_END = 3712

def _swap32(a):
    return jnp.concatenate([a[:, 16:32], a[:, 0:16]], axis=1)

ADAM_LR, ADAM_B1, ADAM_B2, ADAM_EPS, ADAM_WD, ADAM_STEP = 0.001, 0.9, 0.999, 1e-08, 0.01, 10


def _dot(a, b):
    return jnp.dot(a, b, preferred_element_type=F32)


def _dot_nt(a, b):
    return lax.dot_general(a, b, (((1,), (1,)), ((), ())), preferred_element_type=F32)


def _dot_tn(a, b):
    return lax.dot_general(a, b, (((0,), (0,)), ((), ())), preferred_element_type=F32)


def _params(n_grid):
    return pltpu.CompilerParams(dimension_semantics=("arbitrary",) * n_grid, vmem_limit_bytes=VMEM_LIMIT)


def _rms_fwd(x, g):
    r = lax.rsqrt(jnp.mean(x * x, axis=-1, keepdims=True) + EPS)
    return (x * r) * g, r


def _rms_bwd(x, g, r, dy, width=None):
    n = x.shape[-1] if width is None else width
    u = dy * g
    dx = r * u - x * (r * r * r) * (jnp.sum(x * u, axis=-1, keepdims=True) / n)
    return dx, dy * (x * r)


def _iota(shape, axis):
    return lax.broadcasted_iota(jnp.int32, shape, axis)


def _inproj_fwd(x, g, wpt):
    T = x.shape[0]
    tm = 256

    def body(x_ref, g_ref, w_ref, h32_ref, h16_ref, xn_ref):
        xn, _ = _rms_fwd(x_ref[...], g_ref[...])
        xn = xn.astype(BF)
        xn_ref[...] = xn
        h = _dot_nt(xn, w_ref[...])
        h32_ref[...] = h[:, :NF]
        h16_ref[...] = h[:, NF:].astype(BF)

    return pl.pallas_call(
        body, name="inproj_fwd", grid=(T // tm,),
        in_specs=[pl.BlockSpec((tm, D_MODEL), lambda n: (n, 0)),
                  pl.BlockSpec((1, D_MODEL), lambda n: (0, 0)),
                  pl.BlockSpec((NP, D_MODEL), lambda n: (0, 0))],
        out_specs=[pl.BlockSpec((tm, NF), lambda n: (n, 0)),
                   pl.BlockSpec((tm, NB), lambda n: (n, 0)),
                   pl.BlockSpec((tm, D_MODEL), lambda n: (n, 0))],
        out_shape=[jax.ShapeDtypeStruct((T, NF), F32), jax.ShapeDtypeStruct((T, NB), BF),
                   jax.ShapeDtypeStruct((T, D_MODEL), BF)],
        compiler_params=_params(1))(x, g, wpt)


def _inproj_bwd(parts, wpt, x, xn, g, dxo):
    T = x.shape[0]
    tm = 256
    np_ = len(parts)
    chunks = (0, 1280, 2560, NP)
    assert sum(p.shape[1] for p in parts) == C_END == NP

    def body(*refs):
        part_refs = refs[:np_]
        w_ref, x_ref, xn_ref, g_ref, dxo_ref, dw_ref, dx_ref, dg_ref = refs[np_:]
        n = pl.program_id(0)

        @pl.when(n == 0)
        def _():
            dw_ref[...] = jnp.zeros_like(dw_ref)
            dg_ref[...] = jnp.zeros_like(dg_ref)

        dh = jnp.concatenate([r[...].astype(BF) for r in part_refs], axis=1)
        xnv = xn_ref[...]
        for lo, hi in zip(chunks[:-1], chunks[1:]):
            dw_ref[:, lo:hi] += _dot_tn(xnv, dh[:, lo:hi])
        dxn = _dot(dh, w_ref[...])
        xv = x_ref[...]
        _, r = _rms_fwd(xv, g_ref[...])
        dx, dgt = _rms_bwd(xv, g_ref[...], r, dxn)
        dx_ref[...] = dxo_ref[...] + dx
        dg_ref[...] += jnp.sum(dgt, axis=0, keepdims=True)

    once = pl.Buffered(1)
    return pl.pallas_call(
        body, name="inproj_bwd", grid=(T // tm,),
        in_specs=[pl.BlockSpec((tm, p.shape[1]), lambda n: (n, 0)) for p in parts]
        + [pl.BlockSpec((NP, D_MODEL), lambda n: (0, 0), pipeline_mode=once),
           pl.BlockSpec((tm, D_MODEL), lambda n: (n, 0)),
           pl.BlockSpec((tm, D_MODEL), lambda n: (n, 0)),
           pl.BlockSpec((1, D_MODEL), lambda n: (0, 0)),
           pl.BlockSpec((tm, D_MODEL), lambda n: (n, 0))],
        out_specs=[pl.BlockSpec((D_MODEL, NP), lambda n: (0, 0), pipeline_mode=once),
                   pl.BlockSpec((tm, D_MODEL), lambda n: (n, 0)),
                   pl.BlockSpec((1, D_MODEL), lambda n: (0, 0))],
        out_shape=[jax.ShapeDtypeStruct((D_MODEL, NP), F32), jax.ShapeDtypeStruct((T, D_MODEL), F32),
                   jax.ShapeDtypeStruct((1, D_MODEL), F32)],
        compiler_params=_params(1))(*parts, wpt, x, xn, g, dxo)


SWA_BLK = 128
SWA_TQ = 1024


def _bdot_nt(a, b):
    return lax.dot_general(a, b, (((2,), (2,)), ((0,), (0,))), preferred_element_type=F32)


def _bdot(a, b):
    return lax.dot_general(a, b, (((2,), (1,)), ((0,), (0,))), preferred_element_type=F32)


def _bdot_tn(a, b):
    return lax.dot_general(a, b, (((1,), (1,)), ((0,), (0,))), preferred_element_type=F32)


def _swa_probs(q, kc, kp, sink, mask_c, mask_p):
    sc = jnp.where(mask_c, _bdot_nt(q, kc) * SB_SCALE, NEG)
    sp = jnp.where(mask_p, _bdot_nt(q, kp) * SB_SCALE, NEG)
    m = jnp.maximum(jnp.maximum(jnp.max(sc, axis=-1, keepdims=True), jnp.max(sp, axis=-1, keepdims=True)), sink)
    pc = jnp.exp(sc - m)
    pp = jnp.exp(sp - m)
    ps = jnp.exp(sink - m)
    inv = 1.0 / (jnp.sum(pc, axis=-1, keepdims=True) + jnp.sum(pp, axis=-1, keepdims=True) + ps)
    return pc * inv, pp * inv, ps * inv


def _swa_masks(n, nb):
    blk = _iota((nb, SWA_BLK, SWA_BLK), 0)
    row = _iota((nb, SWA_BLK, SWA_BLK), 1)
    col = _iota((nb, SWA_BLK, SWA_BLK), 2)
    return col <= row, jnp.logical_and(col > row, jnp.logical_or(blk > 0, n > 0))


def _swa_specs(tq):
    halo = tq // SWA_BLK
    return [pl.BlockSpec(memory_space=pltpu.SMEM),
            pl.BlockSpec((tq, 256), lambda n: (n, C_AQ // 256)),
            pl.BlockSpec((tq, 128), lambda n: (n, C_AK // 128)),
            pl.BlockSpec((SWA_BLK, 128), lambda n: (jnp.maximum(n * halo - 1, 0), C_AK // 128)),
            pl.BlockSpec((tq, 128), lambda n: (n, C_AV // 128)),
            pl.BlockSpec((SWA_BLK, 128), lambda n: (jnp.maximum(n * halo - 1, 0), C_AV // 128))]


def _swa_blocked(cur_ref, prev_ref, gs, nb):
    cur = cur_ref[:, gs].reshape(nb, SWA_BLK, 64)
    prev = jnp.concatenate([prev_ref[:, gs].reshape(1, SWA_BLK, 64), cur[:nb - 1]], axis=0) if nb > 1 \
        else prev_ref[:, gs].reshape(1, SWA_BLK, 64)
    return cur, prev


def _swa_masks2(n, nb):
    mask_c, mask_p = _swa_masks(n, nb)
    return jnp.concatenate([mask_c, mask_c], axis=1), jnp.concatenate([mask_p, mask_p], axis=1)


def _swa_stacked(ref, g, nb):
    a = ref[:, 128 * g:128 * g + 64].reshape(nb, SWA_BLK, 64)
    b = ref[:, 128 * g + 64:128 * g + 128].reshape(nb, SWA_BLK, 64)
    return jnp.concatenate([a, b], axis=1)


def _swa_fwd(h16, sinks):
    T = h16.shape[0]
    tq = SWA_TQ if T % SWA_TQ == 0 else SWA_BLK
    nb = tq // SWA_BLK

    def body(s_ref, q_ref, kc_ref, kp_ref, vc_ref, vp_ref, o_ref):
        n = pl.program_id(0)
        mask_c, mask_p = _swa_masks2(n, nb)
        first = _iota((1, 2 * SWA_BLK, 1), 1) < SWA_BLK
        for g in range(2):
            gs = slice(g * 64, (g + 1) * 64)
            kc, kp = _swa_blocked(kc_ref, kp_ref, gs, nb)
            vc, vp = _swa_blocked(vc_ref, vp_ref, gs, nb)
            q = _swa_stacked(q_ref, g, nb)
            sink = jnp.where(first, s_ref[2 * g], s_ref[2 * g + 1])
            pc, pp, _ = _swa_probs(q, kc, kp, sink, mask_c, mask_p)
            o = _bdot(pc.astype(BF), vc) + _bdot(pp.astype(BF), vp)
            o_ref[:, 128 * g:128 * g + 64] = o[:, :SWA_BLK].reshape(tq, 64)
            o_ref[:, 128 * g + 64:128 * g + 128] = o[:, SWA_BLK:].reshape(tq, 64)

    return pl.pallas_call(
        body, name="swa_fwd", grid=(T // tq,), in_specs=_swa_specs(tq),
        out_specs=pl.BlockSpec((tq, 256), lambda n: (n, 0)),
        out_shape=jax.ShapeDtypeStruct((T, 256), F32),
        compiler_params=_params(1))(sinks, h16, h16, h16, h16, h16)


def _swa_bwd(h16, sinks, dya):
    T = h16.shape[0]
    tq = SWA_TQ if T % SWA_TQ == 0 else SWA_BLK
    nb = tq // SWA_BLK

    def body(s_ref, q_ref, kc_ref, kp_ref, vc_ref, vp_ref, do_ref, dq_ref, dk_ref, dv_ref, ds_ref):
        n = pl.program_id(0)

        @pl.when(n == 0)
        def _():
            dk_ref[...] = jnp.zeros_like(dk_ref)
            dv_ref[...] = jnp.zeros_like(dv_ref)
            ds_ref[...] = jnp.zeros_like(ds_ref)

        mask_c, mask_p = _swa_masks2(n, nb)
        first = _iota((1, 2 * SWA_BLK, 1), 1) < SWA_BLK
        rows = pl.ds(pl.multiple_of(n * tq, tq), tq)
        before = pl.ds(pl.multiple_of(jnp.maximum(n * nb - 1, 0) * SWA_BLK, SWA_BLK), SWA_BLK)
        lane = _iota((8, 128), 1)
        row8 = _iota((8, 128), 0)

        def to_keys(own, prev):
            if nb == 1:
                return own
            return own + jnp.concatenate([prev[1:], jnp.zeros((1, SWA_BLK, 64), F32)], axis=0)

        for g in range(2):
            gs = slice(g * 64, (g + 1) * 64)
            q = _swa_stacked(q_ref, g, nb)
            kc, kp = _swa_blocked(kc_ref, kp_ref, gs, nb)
            vc, vp = _swa_blocked(vc_ref, vp_ref, gs, nb)
            sink = jnp.where(first, s_ref[2 * g], s_ref[2 * g + 1])
            pc, pp, ps = _swa_probs(q, kc, kp, sink, mask_c, mask_p)
            pcb, ppb = pc.astype(BF), pp.astype(BF)
            do = _swa_stacked(do_ref, g, nb)
            dob = do.astype(BF)
            o = _bdot(pcb, vc) + _bdot(ppb, vp)
            dd = jnp.sum(do * o, axis=-1, keepdims=True)
            dsc = (pc * (_bdot_nt(dob, vc) - dd) * SB_SCALE).astype(BF)
            dsp = (pp * (_bdot_nt(dob, vp) - dd) * SB_SCALE).astype(BF)
            dq = (_bdot(dsc, kc) + _bdot(dsp, kp)).astype(BF)
            dq_ref[:, 128 * g:128 * g + 64] = dq[:, :SWA_BLK].reshape(tq, 64)
            dq_ref[:, 128 * g + 64:128 * g + 128] = dq[:, SWA_BLK:].reshape(tq, 64)
            dkp, dvp = _bdot_tn(dsp, q), _bdot_tn(ppb, dob)
            dk_ref[rows, gs] += to_keys(_bdot_tn(dsc, q), dkp).reshape(tq, 64)
            dv_ref[rows, gs] += to_keys(_bdot_tn(pcb, dob), dvp).reshape(tq, 64)
            dk_ref[before, gs] += dkp[0]
            dv_ref[before, gs] += dvp[0]
            psd = ps * dd
            for hh in range(2):
                part = psd[:, hh * SWA_BLK:(hh + 1) * SWA_BLK]
                ds_ref[...] += jnp.where(jnp.logical_and(lane == 2 * g + hh, row8 == 0), -jnp.sum(part), 0.0)

    return pl.pallas_call(
        body, name="swa_bwd", grid=(T // tq,),
        in_specs=_swa_specs(tq) + [pl.BlockSpec((tq, 256), lambda n: (n, 0))],
        out_specs=[pl.BlockSpec((tq, 256), lambda n: (n, 0)),
                   pl.BlockSpec((T, 128), lambda n: (0, 0)),
                   pl.BlockSpec((T, 128), lambda n: (0, 0)),
                   pl.BlockSpec((8, 128), lambda n: (0, 0))],
        out_shape=[jax.ShapeDtypeStruct((T, 256), BF), jax.ShapeDtypeStruct((T, 128), F32),
                   jax.ShapeDtypeStruct((T, 128), F32), jax.ShapeDtypeStruct((8, 128), F32)],
        compiler_params=_params(1))(sinks, h16, h16, h16, h16, h16, dya)


def _conv_u(bc_ref, bx_ref, bch_ref, bxh_ref, n, tm):
    u = bc_ref[...] * bx_ref[...]
    uh = bch_ref[...] * bxh_ref[...] * (n > 0).astype(F32)
    rowi = _iota((tm, 256), 0)
    u1 = jnp.where(rowi == 0, uh[7:8, :], pltpu.roll(u, 1, axis=0))
    u2 = jnp.where(rowi == 0, uh[6:7, :], jnp.where(rowi == 1, uh[7:8, :], pltpu.roll(u, 2, axis=0)))
    return u, u1, u2


def _conv_fwd(h32, cw, cb):
    T = h32.shape[0]
    tm = 512 if T % 512 == 0 else T
    hb = tm // 8

    def body(bb_ref, bc_ref, bx_ref, bch_ref, bxh_ref, w_ref, b_ref, o_ref):
        n = pl.program_id(0)
        u, u1, u2 = _conv_u(bc_ref, bx_ref, bch_ref, bxh_ref, n, tm)
        y = w_ref[0:1, :] * u2 + w_ref[1:2, :] * u1 + w_ref[2:3, :] * u + b_ref[...]
        o_ref[...] = bb_ref[...] * y

    halo = lambda c: pl.BlockSpec((8, 256), lambda n: (jnp.maximum(n * hb - 1, 0), c // 256))
    return pl.pallas_call(
        body, name="conv_fwd", grid=(T // tm,),
        in_specs=[pl.BlockSpec((tm, 256), lambda n: (n, C_BB // 256)),
                  pl.BlockSpec((tm, 256), lambda n: (n, C_BC // 256)),
                  pl.BlockSpec((tm, 256), lambda n: (n, C_BX // 256)),
                  halo(C_BC), halo(C_BX),
                  pl.BlockSpec((8, 256), lambda n: (0, 0)),
                  pl.BlockSpec((1, 256), lambda n: (0, 0))],
        out_specs=pl.BlockSpec((tm, 256), lambda n: (n, 0)),
        out_shape=jax.ShapeDtypeStruct((T, 256), F32),
        compiler_params=_params(1))(h32, h32, h32, h32, h32, cw, cb)


def _conv_bwd(h32, cw, cb, dyb):
    T = h32.shape[0]
    tm = 512 if T % 512 == 0 else T
    hb = tm // 8
    nt = T // tm

    def body(bb_ref, bc_ref, bx_ref, bch_ref, bxh_ref, bbn_ref, dy_ref, dyn_ref, w_ref, b_ref,
             dbb_ref, dbc_ref, dbx_ref, dw_ref):
        n = pl.program_id(0)

        @pl.when(n == 0)
        def _():
            dw_ref[...] = jnp.zeros_like(dw_ref)

        u, u1, u2 = _conv_u(bc_ref, bx_ref, bch_ref, bxh_ref, n, tm)
        w0, w1, w2 = w_ref[0:1, :], w_ref[1:2, :], w_ref[2:3, :]
        y = w0 * u2 + w1 * u1 + w2 * u + b_ref[...]
        dyb_ = dy_ref[...]
        dbb_ref[...] = (dyb_ * y).astype(BF)
        dy = dyb_ * bb_ref[...]
        dyn = dyn_ref[...] * bbn_ref[...] * (n < nt - 1).astype(F32)
        rowi = _iota((tm, 256), 0)
        dy1 = jnp.where(rowi == tm - 1, dyn[0:1, :], pltpu.roll(dy, tm - 1, axis=0))
        dy2 = jnp.where(rowi == tm - 2, dyn[0:1, :],
                        jnp.where(rowi == tm - 1, dyn[1:2, :], pltpu.roll(dy, tm - 2, axis=0)))
        du = w2 * dy + w1 * dy1 + w0 * dy2
        dbc_ref[...] = (du * bx_ref[...]).astype(BF)
        dbx_ref[...] = (du * bc_ref[...]).astype(BF)
        dw_ref[0:1, :] += jnp.sum(dy * u2, axis=0, keepdims=True)
        dw_ref[1:2, :] += jnp.sum(dy * u1, axis=0, keepdims=True)
        dw_ref[2:3, :] += jnp.sum(dy * u, axis=0, keepdims=True)
        dw_ref[3:4, :] += jnp.sum(dy, axis=0, keepdims=True)

    halo = lambda c: pl.BlockSpec((8, 256), lambda n: (jnp.maximum(n * hb - 1, 0), c // 256))
    nxt = lambda c: pl.BlockSpec((8, 256), lambda n: (jnp.minimum((n + 1) * hb, T // 8 - 1), c // 256))
    cur = lambda c: pl.BlockSpec((tm, 256), lambda n: (n, c // 256))
    return pl.pallas_call(
        body, name="conv_bwd", grid=(nt,),
        in_specs=[cur(C_BB), cur(C_BC), cur(C_BX), halo(C_BC), halo(C_BX), nxt(C_BB),
                  cur(0), nxt(0),
                  pl.BlockSpec((8, 256), lambda n: (0, 0)),
                  pl.BlockSpec((1, 256), lambda n: (0, 0))],
        out_specs=[cur(0), cur(0), cur(0), pl.BlockSpec((8, 256), lambda n: (0, 0))],
        out_shape=[jax.ShapeDtypeStruct((T, 256), BF)] * 3 + [jax.ShapeDtypeStruct((8, 256), F32)],
        compiler_params=_params(1))(h32, h32, h32, h32, h32, h32, dyb, dyb, cw, cb)


def _cprep_specs(tm):
    return [pl.BlockSpec((tm, 256), lambda n: (n, C_CQ // 256)),
            pl.BlockSpec((tm, 128), lambda n: (n, C_CKV // 128)),
            pl.BlockSpec((tm, 128), lambda n: (n, C_CKR // 128)),
            pl.BlockSpec((tm, 128), lambda n: (n, C_CKRS // 128)),
            pl.BlockSpec((1, 256), lambda n: (0, 0)),
            pl.BlockSpec((1, 128), lambda n: (0, 0)),
            pl.BlockSpec((tm, 128), lambda n: (n, 0)),
            pl.BlockSpec((tm, 128), lambda n: (n, 0))]


def _cprep_fwd(h32, gq, gkv, wuq2, wkv2, cosk, sin):
    T = h32.shape[0]
    tm = 512 if T % 512 == 0 else T

    def body(cq_ref, ckv_ref, ckr_ref, ckrs_ref, gq_ref, gkv_ref, cos_ref, sin_ref, wuq_ref, wkv_ref,
             q_ref, k_ref, v_ref):
        cosk_, sin_ = cos_ref[...], sin_ref[...]
        cosq = cosk_ + (_iota((tm, 128), 1) < 64).astype(F32)
        cqn, _ = _rms_fwd(cq_ref[...], gq_ref[...])
        q2 = _dot(cqn.astype(BF), wuq_ref[...])
        ckvn, _ = _rms_fwd(ckv_ref[...], gkv_ref[...])
        kv2 = _dot(ckvn.astype(BF), wkv_ref[...])
        kr = ckr_ref[...] * cosk_ + ckrs_ref[...] * sin_
        for h in range(4):
            hs = slice(h * 128, (h + 1) * 128)
            q_ref[:, hs] = ((q2[:, hs] * cosq + q2[:, 512 + h * 128:512 + (h + 1) * 128] * sin_) * MLA_SCALE).astype(BF)
            k_ref[:, hs] = (kv2[:, hs] + kr).astype(BF)
        ones = _iota((tm, 512), 1) % 128 == 64
        v_ref[...] = jnp.where(ones, 1.0, kv2[:, 512:]).astype(BF)

    return pl.pallas_call(
        body, name="cprep_fwd", grid=(T // tm,),
        in_specs=_cprep_specs(tm) + [pl.BlockSpec((256, 1024), lambda n: (0, 0)),
                                     pl.BlockSpec((128, 1024), lambda n: (0, 0))],
        out_specs=[pl.BlockSpec((tm, 512), lambda n: (n, 0))] * 3,
        out_shape=[jax.ShapeDtypeStruct((T, 512), BF)] * 3,
        compiler_params=_params(1))(h32, h32, h32, h32, gq, gkv, cosk, sin, wuq2, wkv2)


def _cprep_bwd(h32, gq, gkv, wuq2t, wkv2t, cosk, sin, dq, dk, dv):
    T = h32.shape[0]
    tm = 512 if T % 512 == 0 else T

    def body(cq_ref, ckv_ref, ckr_ref, ckrs_ref, gq_ref, gkv_ref, cos_ref, sin_ref, wuq_ref, wkv_ref,
             dq_ref, dk_ref, dv_ref,
             dcq_ref, dckv_ref, dckr_ref, dckrs_ref, dwuq_ref, dwkv_ref, dgq_ref, dgkv_ref):
        n = pl.program_id(0)

        @pl.when(n == 0)
        def _():
            dwuq_ref[...] = jnp.zeros_like(dwuq_ref)
            dwkv_ref[...] = jnp.zeros_like(dwkv_ref)
            dgq_ref[...] = jnp.zeros_like(dgq_ref)
            dgkv_ref[...] = jnp.zeros_like(dgkv_ref)

        cosk_, sin_ = cos_ref[...], sin_ref[...]
        cosq = cosk_ + (_iota((tm, 128), 1) < 64).astype(F32)
        dkr = jnp.zeros((tm, 128), F32)
        plain, swapped = [], []
        for h in range(4):
            hs = slice(h * 128, (h + 1) * 128)
            dqh = dq_ref[:, hs] * MLA_SCALE
            plain.append((dqh * cosq).astype(BF))
            swapped.append((dqh * sin_).astype(BF))
            dkr = dkr + dk_ref[:, hs]
        dq2 = jnp.concatenate(plain + swapped, axis=1)
        dkv2 = jnp.concatenate([dk_ref[...].astype(BF), dv_ref[...].astype(BF)], axis=1)
        dckr_ref[...] = (dkr * cosk_).astype(BF)
        dckrs_ref[...] = (dkr * sin_).astype(BF)

        cq, gq_ = cq_ref[...], gq_ref[...]
        cqn, rq = _rms_fwd(cq, gq_)
        dwuq_ref[...] += _dot_tn(cqn.astype(BF), dq2)
        dcq, dgt = _rms_bwd(cq, gq_, rq, _dot(dq2, wuq_ref[...]))
        dcq_ref[...] = dcq.astype(BF)
        dgq_ref[...] += jnp.sum(dgt, axis=0, keepdims=True)

        ckv, gkv_ = ckv_ref[...], gkv_ref[...]
        ckvn, rkv = _rms_fwd(ckv, gkv_)
        dwkv_ref[...] += _dot_tn(ckvn.astype(BF), dkv2)
        dckv, dgt2 = _rms_bwd(ckv, gkv_, rkv, _dot(dkv2, wkv_ref[...]))
        dckv_ref[...] = dckv.astype(BF)
        dgkv_ref[...] += jnp.sum(dgt2, axis=0, keepdims=True)

    row = lambda w: pl.BlockSpec((tm, w), lambda n: (n, 0))
    return pl.pallas_call(
        body, name="cprep_bwd", grid=(T // tm,),
        in_specs=_cprep_specs(tm) + [pl.BlockSpec((1024, 256), lambda n: (0, 0)),
                                     pl.BlockSpec((1024, 128), lambda n: (0, 0)),
                                     row(512), row(512), row(512)],
        out_specs=[row(256), row(128), row(128), row(128),
                   pl.BlockSpec((256, 1024), lambda n: (0, 0)), pl.BlockSpec((128, 1024), lambda n: (0, 0)),
                   pl.BlockSpec((1, 256), lambda n: (0, 0)), pl.BlockSpec((1, 128), lambda n: (0, 0))],
        out_shape=[jax.ShapeDtypeStruct((T, 256), BF), jax.ShapeDtypeStruct((T, 128), BF),
                   jax.ShapeDtypeStruct((T, 128), BF), jax.ShapeDtypeStruct((T, 128), BF),
                   jax.ShapeDtypeStruct((256, 1024), F32), jax.ShapeDtypeStruct((128, 1024), F32),
                   jax.ShapeDtypeStruct((1, 256), F32), jax.ShapeDtypeStruct((1, 128), F32)],
        compiler_params=_params(1))(h32, h32, h32, h32, gq, gkv, cosk, sin, wuq2t, wkv2t, dq, dk, dv)


MLA_TILE = 512
MLA_HEADS_PER_STEP = 4


def _causal_mask(t):
    return _iota((t, t), 1) <= _iota((t, t), 0)


def _mla_fwd(q, k, v):
    T = q.shape[0]
    tq = MLA_TILE

    def body(q_ref, k_ref, v_ref, o_ref, lse_ref):
        i = pl.program_id(1)
        mask = _causal_mask(tq)
        heads = [slice(128 * h, 128 * h + 128) for h in range(MLA_HEADS_PER_STEP)]
        qs = [q_ref[:, hs] for hs in heads]

        def step(j, carry, masked):
            rows = pl.ds(pl.multiple_of(j * tq, tq), tq)
            out = []
            for hh, hs in enumerate(heads):
                m, acc = carry[hh]
                s = _dot_nt(qs[hh], k_ref[rows, hs])
                if masked:
                    s = jnp.where(mask, s, NEG)
                m_new = jnp.maximum(m, jnp.max(s, axis=-1, keepdims=True))
                p = jnp.exp((s - m_new).astype(BF))
                acc = jnp.exp(m - m_new) * acc + _dot(p, v_ref[rows, hs])
                out.append((m_new, acc))
            return tuple(out)

        init = ((jnp.full((tq, 1), NEG, F32), jnp.zeros((tq, 128), F32)),) * len(heads)
        carry = lax.fori_loop(0, i // 2, lambda t, c: step(2 * t + 1, step(2 * t, c, False), False), init)
        carry = lax.cond(i % 2 == 1, lambda c: step(i - 1, c, False), lambda c: c, carry)
        carry = step(i, carry, True)
        for hh, hs in enumerate(heads):
            m, acc = carry[hh]
            l = acc[:, 64:65]
            o_ref[:, hs] = acc * (1.0 / l)
            lse_ref[:, hs] = jnp.broadcast_to(m + jnp.log(l), (tq, 128))

    width = 128 * MLA_HEADS_PER_STEP
    blk = pl.BlockSpec((tq, width), lambda h, i: (i, h))
    full = pl.BlockSpec((T, width), lambda h, i: (0, h))
    return pl.pallas_call(
        body, name="mla_fwd", grid=(4 // MLA_HEADS_PER_STEP, T // tq), in_specs=[blk, full, full],
        out_specs=[blk, blk],
        out_shape=[jax.ShapeDtypeStruct((T, 512), F32), jax.ShapeDtypeStruct((T, 512), F32)],
        compiler_params=_params(2))(q, k, v)


def _mla_bwd(q, k, v, o, lse, do):
    T = q.shape[0]
    tq = MLA_TILE

    def body(q_ref, k_ref, v_ref, o_ref, lse_ref, do_ref, dq_ref, dk_ref, dv_ref):
        i = pl.program_id(1)

        @pl.when(i == 0)
        def _():
            dk_ref[...] = jnp.zeros_like(dk_ref)
            dv_ref[...] = jnp.zeros_like(dv_ref)

        heads = [slice(0, 128), slice(128, 256)]
        mask = _causal_mask(tq)
        qs, dobs, dds, lses = [], [], [], []
        for hs in heads:
            do = do_ref[:, hs]
            qs.append(q_ref[:, hs])
            dobs.append(do.astype(BF))
            dds.append(jnp.sum(do * o_ref[:, hs], axis=-1, keepdims=True))
            lses.append(lse_ref[:, hs.start:hs.start + 1])

        def step(j, dqs, masked):
            rows = pl.ds(pl.multiple_of(j * tq, tq), tq)
            out = []
            for hh, hs in enumerate(heads):
                kj, vj = k_ref[rows, hs], v_ref[rows, hs]
                s = _dot_nt(qs[hh], kj)
                if masked:
                    s = jnp.where(mask, s, NEG)
                p = jnp.exp(s - lses[hh])
                ds = (p * (_dot_nt(dobs[hh], vj) - dds[hh])).astype(BF)
                dk_ref[rows, hs] += _dot_tn(ds, qs[hh])
                dv_ref[rows, hs] += _dot_tn(p.astype(BF), dobs[hh])
                out.append(dqs[hh] + _dot(ds, kj))
            return tuple(out)

        dqs = lax.fori_loop(0, i // 2, lambda t, c: step(2 * t + 1, step(2 * t, c, False), False),
                            (jnp.zeros((tq, 128), F32),) * 2)
        dqs = lax.cond(i % 2 == 1, lambda c: step(i - 1, c, False), lambda c: c, dqs)
        dqs = step(i, dqs, True)
        for hh, hs in enumerate(heads):
            dq_ref[:, hs] = dqs[hh]

    blk = pl.BlockSpec((tq, 256), lambda h, i: (i, h))
    full = pl.BlockSpec((T, 256), lambda h, i: (0, h), pipeline_mode=pl.Buffered(1))
    return pl.pallas_call(
        body, name="mla_bwd", grid=(2, T // tq), in_specs=[blk, full, full, blk, blk, blk],
        out_specs=[blk, full, full],
        out_shape=[jax.ShapeDtypeStruct((T, 512), F32)] * 3,
        compiler_params=_params(2))(q, k, v, o, lse, do)


def _sb_tile(qk, rr, strict, masked, upper):
    z2 = qk * (SB_SCALE * LOG2E)
    l1 = jnp.log2(1.0 + jnp.exp2(-jnp.abs(z2)))
    lk = -jnp.maximum(z2, 0.0) - l1
    if masked:
        lk = jnp.where(strict, lk, 0.0)
    after = rr + _dot(lk.astype(BF), upper)
    ll = jnp.minimum(z2, 0.0) - l1
    a = jnp.exp2(ll + after)
    if masked:
        a = jnp.where(strict, a, 0.0)
    return ll, a, jnp.sum(lk, axis=-1, keepdims=True)


SB_TQ, SB_TK = 256, 256
SB_DEAD = -160.0


def _sb_walk(trips, one_step, carry):
    def alive(c):
        t, cr = c
        top = functools.reduce(jnp.maximum, [jnp.max(h[0]) for h in cr])
        return jnp.logical_and(t < trips, top > SB_DEAD)

    def body(c):
        t, cr = c
        return t + 1, one_step(t, cr)

    return lax.while_loop(alive, body, (jnp.int32(0), carry))[1]


def _sb_consts(tq, tk):
    row, col = _iota((tq, tk), 0), _iota((tq, tk), 1)
    strict = [col + d * tk < row for d in range(tq // tk)]
    r2, c2 = _iota((tk, tk), 0), _iota((tk, tk), 1)
    return strict, (r2 > c2).astype(BF), (r2 < c2).astype(BF)


def _sb_fwd(h16):
    T = h16.shape[0]
    tq, tk = SB_TQ, SB_TK
    nd = tq // tk

    def body(q0_ref, q1_ref, k0_ref, k1_ref, v0_ref, v1_ref, o_ref):
        i = pl.program_id(0)
        strict, upper, _ = _sb_consts(tq, tk)
        lane = _iota((tq, 128), 1)
        pairs = [slice(0, 128), slice(128, 256)]
        k_refs, v_refs = (k0_ref, k1_ref), (v0_ref, v1_ref)
        qms = []
        for q_ref in (q0_ref, q1_ref):
            q2 = q_ref[...]
            qms += [jnp.where(lane < 64, q2, jnp.zeros_like(q2)), jnp.where(lane >= 64, q2, jnp.zeros_like(q2))]

        def step(j, carry, d):
            rows = pl.ds(pl.multiple_of(j * tk, tk), tk)
            out = []
            for h in range(4):
                rr, acc = carry[h]
                _, a, rs = _sb_tile(_dot_nt(qms[h], k_refs[h // 2][rows, :]), rr, None if d is None else strict[d],
                                    d is not None, upper)
                out.append((rr + rs, acc + _dot(a.astype(BF), v_refs[h // 2][rows, :])))
            return tuple(out)

        carry = ((jnp.zeros((tq, 1), F32), jnp.zeros((tq, 128), F32)),) * 4
        for d in reversed(range(nd)):
            carry = step(nd * i + d, carry, d)
        carry = _sb_walk(nd * i, lambda t, c: step(nd * i - 1 - t, c, None), carry)
        for p, ps in enumerate(pairs):
            o_ref[:, ps] = jnp.where(lane < 64, carry[2 * p][1], carry[2 * p + 1][1])

    return pl.pallas_call(
        body, name="sb_fwd", grid=(T // tq,),
        in_specs=[pl.BlockSpec((tq, 128), lambda i: (i, C_DQ // 128)),
                  pl.BlockSpec((tq, 128), lambda i: (i, C_DQ // 128 + 1)),
                  pl.BlockSpec((T, 128), lambda i: (0, C_DK // 128)),
                  pl.BlockSpec((T, 128), lambda i: (0, C_DK // 128 + 1)),
                  pl.BlockSpec((T, 128), lambda i: (0, C_DV // 128)),
                  pl.BlockSpec((T, 128), lambda i: (0, C_DV // 128 + 1))],
        out_specs=pl.BlockSpec((tq, 256), lambda i: (i, 0)),
        out_shape=jax.ShapeDtypeStruct((T, 256), F32),
        compiler_params=_params(1))(h16, h16, h16, h16, h16, h16)


def _sb_bwd(h16, yd, dyd):
    T = h16.shape[0]
    tq, tk = SB_TQ, SB_TK
    nd = tq // tk

    def body(q0_ref, q1_ref, k0_ref, k1_ref, v0_ref, v1_ref, o_ref, do_ref, dq_ref, dk_ref, dv_ref):
        i = pl.program_id(0)

        @pl.when(i == 0)
        def _():
            dk_ref[...] = jnp.zeros_like(dk_ref)
            dv_ref[...] = jnp.zeros_like(dv_ref)

        strict, upper, before = _sb_consts(tq, tk)
        lane = _iota((tq, 128), 1)
        lane_k = _iota((tk, 128), 1)
        pairs = [slice(0, 128), slice(128, 256)]
        k_refs, v_refs = (k0_ref, k1_ref), (v0_ref, v1_ref)
        q2s, dob2s, qms, doms, dds = [], [], [], [], []
        for p, q_ref in enumerate((q0_ref, q1_ref)):
            q2 = q_ref[...]
            dob2 = do_ref[:, pairs[p]].astype(BF)
            doo = dob2.astype(F32) * o_ref[:, pairs[p]]
            q2s.append(q2)
            dob2s.append(dob2)
            for mine in (lane < 64, lane >= 64):
                qms.append(jnp.where(mine, q2, jnp.zeros_like(q2)))
                doms.append(jnp.where(mine, dob2, jnp.zeros_like(dob2)))
                dds.append(jnp.sum(jnp.where(mine, doo, 0.0), axis=-1, keepdims=True))

        def step(j, carry, d):
            rows = pl.ds(pl.multiple_of(j * tk, tk), tk)
            out, dks, dvs = [], [], []
            for h in range(4):
                kj, vj = k_refs[h // 2][rows, :], v_refs[h // 2][rows, :]
                rr, sg, dq = carry[h]
                ll, a, rs = _sb_tile(_dot_nt(qms[h], kj), rr, None if d is None else strict[d], d is not None, upper)
                ab = a.astype(BF)
                g = _dot_nt(doms[h], vj) * ab.astype(F32)
                gs = jnp.sum(g, axis=-1, keepdims=True)
                pre = (dds[h] - sg - gs) + _dot(g.astype(BF), before)
                dz = g - jnp.exp2(ll) * (g + pre)
                if d is not None:
                    dz = jnp.where(strict[d], dz, 0.0)
                dzb = dz.astype(BF)
                dks.append(_dot_tn(dzb, q2s[h // 2]))
                dvs.append(_dot_tn(ab, dob2s[h // 2]))
                out.append((rr + rs, sg + gs, dq + _dot(dzb, kj)))
            for p, ps in enumerate(pairs):
                dk_ref[rows, ps] += jnp.where(lane_k < 64, dks[2 * p], dks[2 * p + 1]) * SB_SCALE
                dv_ref[rows, ps] += jnp.where(lane_k < 64, dvs[2 * p], dvs[2 * p + 1])
            return tuple(out)

        zero = jnp.zeros((tq, 1), F32)
        carry = ((zero, zero, jnp.zeros((tq, 128), F32)),) * 4
        for d in reversed(range(nd)):
            carry = step(nd * i + d, carry, d)
        carry = _sb_walk(nd * i, lambda t, c: step(nd * i - 1 - t, c, None), carry)
        for p, ps in enumerate(pairs):
            dq_ref[:, ps] = jnp.where(lane < 64, carry[2 * p][2], carry[2 * p + 1][2]) * SB_SCALE

    blk = lambda c: pl.BlockSpec((tq, 128), lambda i: (i, c // 128))
    full = lambda c: pl.BlockSpec((T, 128), lambda i: (0, c // 128))
    row = pl.BlockSpec((tq, 256), lambda i: (i, 0))
    acc = pl.BlockSpec((T, 256), lambda i: (0, 0))
    return pl.pallas_call(
        body, name="sb_bwd", grid=(T // tq,),
        in_specs=[blk(C_DQ), blk(C_DQ + 128), full(C_DK), full(C_DK + 128), full(C_DV), full(C_DV + 128), row, row],
        out_specs=[row, acc, acc],
        out_shape=[jax.ShapeDtypeStruct((T, 256), F32)] * 3,
        compiler_params=_params(1))(h16, h16, h16, h16, h16, h16, yd, dyd)


def _compact_c(ycp):
    return jnp.concatenate([ycp[:, h * 128:h * 128 + 64] for h in range(4)], axis=1)


def _post_fwd(ya, yb, ycp, yd, h32, ggrp, wout, gpost, x, tgt=None):
    T = x.shape[0]
    tm = 256
    last = tgt is not None

    def body(*refs):
        ya_ref, yb_ref, yc_ref, yd_ref, gate_ref, gg_ref, w_ref, gp_ref, x_ref = refs[:9]
        if last:
            t_ref, xn_ref, o_ref, sq_ref = refs[9:]
        else:
            xn_ref, o_ref = refs[9:]
        ys = [ya_ref[...], yb_ref[...], _compact_c(yc_ref[...]), yd_ref[...]]
        gate = gate_ref[...]
        sil = gate * (1.0 / (1.0 + jnp.exp(-gate)))
        parts = []
        for gi in range(4):
            ng, _ = _rms_fwd(ys[gi], gg_ref[:, gi * 256:(gi + 1) * 256])
            parts.append(ng * sil[:, gi * 256:(gi + 1) * 256])
        o = _dot(jnp.concatenate(parts, axis=1).astype(BF), w_ref[...])
        o_ref[...] = o
        on, _ = _rms_fwd(o, gp_ref[...])
        if last:
            @pl.when(pl.program_id(0) == 0)
            def _():
                sq_ref[...] = jnp.zeros_like(sq_ref)

            d = (x_ref[...] + on) - t_ref[...]
            sq_ref[...] += jnp.sum(d * d, axis=0, keepdims=True)
            xn_ref[...] = d * (1.0 / D_MODEL)
        else:
            xn_ref[...] = x_ref[...] + on

    row = lambda w: pl.BlockSpec((tm, w), lambda n: (n, 0))
    vec = pl.BlockSpec((1, 1024), lambda n: (0, 0))
    return pl.pallas_call(
        body, name="post_fwd", grid=(T // tm,),
        in_specs=[row(256), row(256), row(512), row(256), pl.BlockSpec((tm, 1024), lambda n: (n, C_GATE // 1024)),
                  vec, pl.BlockSpec((1024, 1024), lambda n: (0, 0)), vec, row(1024)] + ([row(1024)] if last else []),
        out_specs=[row(1024), row(1024)] + ([vec] if last else []),
        out_shape=[jax.ShapeDtypeStruct((T, 1024), F32), jax.ShapeDtypeStruct((T, 1024), F32)]
        + ([jax.ShapeDtypeStruct((1, 1024), F32)] if last else []),
        compiler_params=_params(1))(*([ya, yb, ycp, yd, h32, ggrp, wout, gpost, x] + ([tgt] if last else [])))


def _post_bwd(dx, o, gpost, woutt, ya, yb, ycp, yd, h32, ggrp):
    T = dx.shape[0]
    tm = 256

    def body(dx_ref, o_ref, gp_ref, w_ref, ya_ref, yb_ref, yc_ref, yd_ref, gate_ref, gg_ref,
             dw_ref, dya_ref, dyb_ref, dyc_ref, dyd_ref, dgate_ref, dgp_ref, dgg_ref):
        n = pl.program_id(0)

        @pl.when(n == 0)
        def _():
            dw_ref[...] = jnp.zeros_like(dw_ref)
            dgp_ref[...] = jnp.zeros_like(dgp_ref)
            dgg_ref[...] = jnp.zeros_like(dgg_ref)

        ov, gp = o_ref[...], gp_ref[...]
        _, ro = _rms_fwd(ov, gp)
        do, dgt = _rms_bwd(ov, gp, ro, dx_ref[...])
        dgp_ref[...] += jnp.sum(dgt, axis=0, keepdims=True)
        dob = do.astype(BF)
        gate = gate_ref[...]
        sg = 1.0 / (1.0 + jnp.exp(-gate))
        sil = gate * sg
        dsil = sg * (1.0 + gate * (1.0 - sg))
        ys = [ya_ref[...], yb_ref[...], _compact_c(yc_ref[...]), yd_ref[...]]
        normed = [_rms_fwd(ys[gi], gg_ref[:, gi * 256:(gi + 1) * 256]) for gi in range(4)]
        ym = jnp.concatenate([normed[gi][0] * sil[:, gi * 256:(gi + 1) * 256] for gi in range(4)], axis=1).astype(BF)
        dw_ref[...] += _dot_tn(ym, dob)
        dym = _dot(dob, w_ref[...])
        dys = []
        for gi in range(4):
            gs = slice(gi * 256, (gi + 1) * 256)
            gg = gg_ref[:, gs]
            ng, rg = normed[gi]
            dgate_ref[:, gs] = (dym[:, gs] * ng * dsil[:, gs]).astype(BF)
            dy, dgt2 = _rms_bwd(ys[gi], gg, rg, dym[:, gs] * sil[:, gs])
            dgg_ref[:, gs] += jnp.sum(dgt2, axis=0, keepdims=True)
            dys.append(dy)
        dya_ref[...] = dys[0]
        dyb_ref[...] = dys[1]
        dyd_ref[...] = dys[3]
        z64 = jnp.zeros((tm, 64), F32)
        dyc_ref[...] = jnp.concatenate(
            [piece for h in range(4) for piece in (dys[2][:, h * 64:(h + 1) * 64], z64)], axis=1)

    row = lambda w: pl.BlockSpec((tm, w), lambda n: (n, 0))
    vec = pl.BlockSpec((1, 1024), lambda n: (0, 0))
    return pl.pallas_call(
        body, name="post_bwd", grid=(T // tm,),
        in_specs=[row(1024), row(1024), vec, pl.BlockSpec((1024, 1024), lambda n: (0, 0)),
                  row(256), row(256), row(512), row(256),
                  pl.BlockSpec((tm, 1024), lambda n: (n, C_GATE // 1024)), vec],
        out_specs=[pl.BlockSpec((1024, 1024), lambda n: (0, 0)), row(256), row(256), row(512), row(256), row(1024),
                   vec, vec],
        out_shape=[jax.ShapeDtypeStruct((1024, 1024), F32), jax.ShapeDtypeStruct((T, 256), F32),
                   jax.ShapeDtypeStruct((T, 256), F32), jax.ShapeDtypeStruct((T, 512), F32),
                   jax.ShapeDtypeStruct((T, 256), F32), jax.ShapeDtypeStruct((T, 1024), BF),
                   jax.ShapeDtypeStruct((1, 1024), F32), jax.ShapeDtypeStruct((1, 1024), F32)],
        compiler_params=_params(1))(dx, o, gpost, woutt, ya, yb, ycp, yd, h32, ggrp)


def _swap_rows32(a):
    return jnp.concatenate([a[16:32], a[0:16]], axis=0)


def _pad_w_uq(w):
    z = lambda n: jnp.zeros((w.shape[0], n), w.dtype)
    a = [p for h in range(4) for p in (w[:, 96 * h:96 * h + 96], z(32))]
    b = [p for h in range(4) for p in (z(64), _swap32(w[:, 96 * h + 64:96 * h + 96]), z(32))]
    return jnp.concatenate(a + b, axis=1)


def _unpad_w_uq(d):
    out = []
    for h in range(4):
        out.append(d[:, 128 * h:128 * h + 64])
        out.append(d[:, 128 * h + 64:128 * h + 96] + _swap32(d[:, 512 + 128 * h + 64:512 + 128 * h + 96]))
    return jnp.concatenate(out, axis=1)


def _pad_w_ukv(w):
    z = jnp.zeros((w.shape[0], 64), w.dtype)
    a = [p for h in range(4) for p in (w[:, 128 * h:128 * h + 64], z)]
    b = [p for h in range(4) for p in (w[:, 128 * h + 64:128 * h + 128], z)]
    return jnp.concatenate(a + b, axis=1)


def _unpad_w_ukv(d):
    return jnp.concatenate([p for h in range(4) for p in (d[:, 128 * h:128 * h + 64],
                                                          d[:, 512 + 128 * h:512 + 128 * h + 64])], axis=1)


def _rope_tables(pos):
    freqs = 10000.0 ** (-jnp.arange(16, dtype=F32) / 16)
    ang = pos.astype(F32)[:, None] * freqs
    c, s = jnp.cos(ang), jnp.sin(ang)
    z = lambda n: jnp.zeros((pos.shape[0], n), F32)
    return (jnp.concatenate([z(64), c, c, z(32)], axis=1), jnp.concatenate([z(64), -s, s, z(32)], axis=1))


def _layer_weights(W, l):
    wuq2 = _pad_w_uq(W["mla_w_uq"][l])
    wkv2 = _pad_w_ukv(W["mla_w_ukv"][l])
    wout = W["w_out"][l]
    cw = jnp.concatenate([W["conv_w"][l].astype(F32), jnp.zeros((5, 256), F32)], axis=0)
    return dict(
        wpt=W["wpt"][l], wuq2=wuq2.astype(BF), wuq2t=wuq2.T.astype(BF),
        wkv2=wkv2.astype(BF), wkv2t=wkv2.T.astype(BF), wout=wout.astype(BF), woutt=wout.T.astype(BF),
        cw=cw, cb=W["conv_b"][l][None, :], sinks=W["attn_sinks"][l],
        gpre=W["norm_pre"][l][None, :], gq=W["mla_q_norm"][l][None, :], gkv=W["mla_kv_norm"][l][None, :],
        ggrp=W["group_norm"][l][None, :], gpost=W["norm_post"][l][None, :])


def _local_step(x, pos, W, tgt):
    cosk, sin = _rope_tables(pos)
    saved = []
    for l in range(DEPTH):
        lw = _layer_weights(W, l)
        h32, h16, xn = _inproj_fwd(x, lw["gpre"], lw["wpt"])
        ya = _swa_fwd(h16, lw["sinks"])
        yb = _conv_fwd(h32, lw["cw"], lw["cb"])
        qc, kc, vc = _cprep_fwd(h32, lw["gq"], lw["gkv"], lw["wuq2"], lw["wkv2"], cosk, sin)
        ycp, lse = _mla_fwd(qc, kc, vc)
        yd = _sb_fwd(h16)
        if l < DEPTH - 1:
            x_new, o = _post_fwd(ya, yb, ycp, yd, h32, lw["ggrp"], lw["wout"], lw["gpost"], x)
        else:
            dx, o, sq = _post_fwd(ya, yb, ycp, yd, h32, lw["ggrp"], lw["wout"], lw["gpost"], x, tgt)
        saved.append(dict(lw=lw, x=x, h32=h32, h16=h16, xn=xn, ya=ya, yb=yb, qc=qc, kc=kc, vc=vc, ycp=ycp,
                          lse=lse, yd=yd, o=o))
        if l < DEPTH - 1:
            x = x_new

    grads = {k: [None] * DEPTH for k in ("norm_pre", "w_in_pt", "attn_sinks", "conv_w", "conv_b", "mla_q_norm",
                                         "mla_w_uq", "mla_kv_norm", "mla_w_ukv", "group_norm", "w_out",
                                         "norm_post")}
    for l in reversed(range(DEPTH)):
        s = saved[l]
        lw = s["lw"]
        dwout, dya, dyb, dycp, dyd, dgate, dgpost, dggrp = _post_bwd(
            dx, s["o"], lw["gpost"], lw["woutt"], s["ya"], s["yb"], s["ycp"], s["yd"], s["h32"], lw["ggrp"])
        grads["norm_post"][l] = dgpost[0]
        grads["group_norm"][l] = dggrp[0]
        grads["w_out"][l] = dwout
        sdq, sdk, sdv = _sb_bwd(s["h16"], s["yd"], dyd)
        mdq, mdk, mdv = _mla_bwd(s["qc"], s["kc"], s["vc"], s["ycp"], s["lse"], dycp)
        dcq, dckv, dckr, dckrs, dwuq2, dwkv2, dgq, dgkv = _cprep_bwd(
            s["h32"], lw["gq"], lw["gkv"], lw["wuq2t"], lw["wkv2t"], cosk, sin, mdq, mdk, mdv)
        grads["mla_q_norm"][l] = dgq[0]
        grads["mla_kv_norm"][l] = dgkv[0]
        grads["mla_w_uq"][l] = _unpad_w_uq(dwuq2)
        grads["mla_w_ukv"][l] = _unpad_w_ukv(dwkv2)
        dbb, dbc, dbx, dcw = _conv_bwd(s["h32"], lw["cw"], lw["cb"], dyb)
        grads["conv_w"][l] = dcw[0:3]
        grads["conv_b"][l] = dcw[3]
        adq, adk, adv, dsk = _swa_bwd(s["h16"], lw["sinks"], dya)
        grads["attn_sinks"][l] = dsk[0, 0:4]
        parts = [dgate, dbb, dbc, dbx, dcq, dckv, dckr, dckrs, adq, adk, adv, sdq, sdk, sdv]
        dwp, dx, dgpre = _inproj_bwd(parts, lw["wpt"], s["x"], s["xn"], lw["gpre"], dx)
        grads["w_in_pt"][l] = dwp.T
        grads["norm_pre"][l] = dgpre[0]
    return sq, dx, grads


SMALL_SHARDED = ("conv_w", "mla_w_uq", "mla_w_ukv")
REPLICATED = ("norm_pre", "attn_sinks", "conv_b", "mla_q_norm", "mla_kv_norm", "group_norm", "norm_post")
ORDER = ("norm_pre", "w_in", "attn_sinks", "conv_w", "conv_b", "mla_q_norm", "mla_w_uq", "mla_kv_norm",
         "mla_w_ukv", "group_norm", "w_out", "norm_post")
W_IN_COLS = 436
W_IN_WIN = 440
SMALL_ROWS = 48


def _pack_small(arrs, dtype):
    flat = jnp.concatenate([a.reshape(-1).astype(dtype) for a in arrs])
    flat = jnp.concatenate([flat, jnp.zeros((SMALL_ROWS * D_MODEL - flat.shape[0],), dtype)])
    return flat.reshape(SMALL_ROWS, D_MODEL)


TAIL_ROW0 = DEPTH * W_IN_WIN


def _pack_state(ps):
    k = len(ps)
    wout = jnp.stack([p["w_out"] for p in ps]).reshape(k, DEPTH * 128, D_MODEL)
    flat = jnp.stack([jnp.concatenate([p[n].reshape(-1) for n in SMALL_SHARDED + REPLICATED]) for p in ps])
    small = jnp.pad(flat, ((0, 0), (0, SMALL_ROWS * D_MODEL - flat.shape[1]))).reshape(k, SMALL_ROWS, D_MODEL)
    return jnp.concatenate([wout, small], axis=1)


def _unpack_state(buf, p):
    k = buf.shape[0]
    out = {"w_out": buf[:, 0:DEPTH * 128].reshape(k, DEPTH, 128, D_MODEL)}
    flat = buf[:, DEPTH * 128:].reshape(k, SMALL_ROWS * D_MODEL)
    off = 0
    for n in SMALL_SHARDED + REPLICATED:
        size = int(np.prod(p[n].shape))
        out[n] = flat[:, off:off + size].reshape((k,) + p[n].shape)
        off += size
    return out


def _rows_of_w_in_t(lo, hi, padded, kr):
    segs = ((0, 512, padded, NF + C_AQ), (512, 1664, padded, C_BB), (1664, 1696, kr, 0),
            (1696, 2464, padded, NF + C_DQ), (2464, 3488, padded, C_GATE))
    out = []
    for s0, s1, src, base in segs:
        a, b = max(lo, s0), min(hi, s1)
        if a < b:
            out.append(src[base + a - s0:base + b - s0])
    return out


def _me():
    return lax.axis_index("x"), lax.axis_index("y"), lax.axis_index("c")


def _all_gather(block):
    R, C = block.shape

    def body(src_ref, out_ref, send_sems, recv_sems, local_sem):
        x, y, c = _me()
        me, sibling = (x, y, c), (x, y, 1 - c)
        chips = [(1 - x, y), (x, 1 - y), (1 - x, 1 - y)]

        def slot(px, py, pc):
            return out_ref.at[4 * px + 2 * py + pc]

        def copy(k, block, to, src=None):
            return pltpu.make_async_remote_copy(
                src_ref=slot(*block) if src is None else src, dst_ref=slot(*block), send_sem=send_sems.at[k],
                recv_sem=recv_sems.at[k], device_id=to, device_id_type=MESH)

        mine = pltpu.make_async_copy(src_ref, slot(*me), local_sem)
        mine.start()
        first = [copy(0, me, sibling, src=src_ref)]
        first += [copy(1 + j, me, (*chip, c), src=src_ref) for j, chip in enumerate(chips)]
        for cp in first:
            cp.start()
        passed = [copy(4 + j, (*chip, c), sibling) for j, chip in enumerate(chips)]
        for j, chip in enumerate(chips):
            copy(1 + j, (*chip, c), me).wait_recv()
            passed[j].start()
        copy(0, sibling, me).wait_recv()
        for j, chip in enumerate(chips):
            copy(4 + j, (*chip, 1 - c), me).wait_recv()
        for cp in first + passed:
            cp.wait_send()
        mine.wait()

    return pl.pallas_call(
        body, name="all_gather", out_shape=jax.ShapeDtypeStruct((N_DEV, R, C), block.dtype),
        in_specs=[pl.BlockSpec(memory_space=pl.ANY)], out_specs=pl.BlockSpec(memory_space=pl.ANY),
        scratch_shapes=[pltpu.SemaphoreType.DMA((N_DEV - 1,)), pltpu.SemaphoreType.DMA((N_DEV - 1,)),
                        pltpu.SemaphoreType.DMA])(block)


N_CHIP = 4


def _sibling_swap(blocks):
    _, R, C = blocks.shape

    def body(src_ref, out_ref, send_sems, recv_sems):
        x, y, c = _me()
        copies = [pltpu.make_async_remote_copy(
            src_ref=src_ref.at[2 * j + 1 - c], dst_ref=out_ref.at[j], send_sem=send_sems.at[j],
            recv_sem=recv_sems.at[j], device_id=(x, y, 1 - c), device_id_type=MESH) for j in range(N_CHIP)]
        for cp in copies:
            cp.start()
        for cp in copies:
            cp.wait()

    return pl.pallas_call(
        body, name="sibling_swap", out_shape=jax.ShapeDtypeStruct((N_CHIP, R, C), blocks.dtype),
        in_specs=[pl.BlockSpec(memory_space=pl.ANY)], out_specs=pl.BlockSpec(memory_space=pl.ANY),
        scratch_shapes=[pltpu.SemaphoreType.DMA((N_CHIP,)), pltpu.SemaphoreType.DMA((N_CHIP,))])(blocks)


def _pair_sum(a, b):
    n, R, C = a.shape
    tr = 592 if R % 592 == 0 else R

    def body(a_ref, b_ref, o_ref):
        o_ref[...] = (a_ref[...].astype(F32) + b_ref[...].astype(F32)).astype(BF)

    spec = pl.BlockSpec((1, tr, C), lambda j, r: (j, r, 0))
    return pl.pallas_call(body, name="pair_sum", grid=(n, R // tr), in_specs=[spec, spec], out_specs=spec,
                          out_shape=jax.ShapeDtypeStruct(a.shape, BF), compiler_params=_params(2))(a, b)


def _chip_exchange(sums):
    _, R, C = sums.shape

    def body(src_ref, out_ref, send_sems, recv_sems, local_sem):
        x, y, c = _me()
        here = 2 * x + y
        mine = pltpu.make_async_copy(src_ref.at[here], out_ref.at[here], local_sem)
        mine.start()
        copies = []
        for k in range(1, N_CHIP):
            px, py = x ^ (k >> 1), y ^ (k & 1)
            copies.append(pltpu.make_async_remote_copy(
                src_ref=src_ref.at[2 * px + py], dst_ref=out_ref.at[here], send_sem=send_sems.at[k - 1],
                recv_sem=recv_sems.at[k - 1], device_id=(px, py, c), device_id_type=MESH))
        for cp in copies:
            cp.start()
        for cp in copies:
            cp.wait()
        mine.wait()

    return pl.pallas_call(
        body, name="chip_exchange", out_shape=jax.ShapeDtypeStruct((N_CHIP, R, C), sums.dtype),
        in_specs=[pl.BlockSpec(memory_space=pl.ANY)], out_specs=pl.BlockSpec(memory_space=pl.ANY),
        scratch_shapes=[pltpu.SemaphoreType.DMA((N_CHIP - 1,)), pltpu.SemaphoreType.DMA((N_CHIP - 1,)),
                        pltpu.SemaphoreType.DMA])(sums)


def _adamw_update(g, w, m, v):
    m_ = ADAM_B1 * m + (1.0 - ADAM_B1) * g
    v_ = ADAM_B2 * v + (1.0 - ADAM_B2) * (g * g)
    m_hat = m_ / (1.0 - ADAM_B1 ** ADAM_STEP)
    v_hat = v_ / (1.0 - ADAM_B2 ** ADAM_STEP)
    return -ADAM_LR * (m_hat / (jnp.sqrt(v_hat) + ADAM_EPS) + ADAM_WD * w), m_, v_


def _adamw(parts, state):
    _, R, C = state.shape
    n_parts = parts.shape[0]
    tr = 16
    assert R % tr == 0 and TAIL_ROW0 % tr == 0

    def body(p_ref, s_ref, o_ref):
        g = p_ref[0].astype(F32)
        for k in range(1, n_parts):
            g = g + p_ref[k].astype(F32)
        o_ref[0] = g
        o_ref[1], o_ref[2], o_ref[3] = _adamw_update(g, s_ref[0], s_ref[1], s_ref[2])

    return pl.pallas_call(
        body, name="adamw", grid=(R // tr,),
        in_specs=[pl.BlockSpec((n_parts, tr, C), lambda n: (0, n + TAIL_ROW0 // tr, 0)),
                  pl.BlockSpec((3, tr, C), lambda n: (0, n, 0))],
        out_specs=pl.BlockSpec((4, tr, C), lambda n: (0, n, 0)), out_shape=jax.ShapeDtypeStruct((4, R, C), F32),
        compiler_params=_params(1))(parts, state)


def _adamw_w_in(parts, w, m, v, core):
    n_parts = parts.shape[0]
    tc = 256

    def body(core_ref, p_ref, w_ref, m_ref, v_ref, o_ref):
        g_t = p_ref[0].astype(F32)
        for k in range(1, n_parts):
            g_t = g_t + p_ref[k].astype(F32)
        g_t = jnp.concatenate([g_t, jnp.zeros((512 - W_IN_WIN, tc), F32)], axis=0).T
        g = jnp.where(core_ref[0] == 0, g_t[:, 0:W_IN_COLS], g_t[:, W_IN_WIN - W_IN_COLS:W_IN_WIN])
        o_ref[0, 0] = g
        o_ref[1, 0], o_ref[2, 0], o_ref[3, 0] = _adamw_update(g, w_ref[0], m_ref[0], v_ref[0])

    nat = pl.BlockSpec((1, tc, W_IN_COLS), lambda l, j: (l, j, 0))
    return pl.pallas_call(
        body, name="adamw_w_in", grid=(DEPTH, D_MODEL // tc),
        in_specs=[pl.BlockSpec(memory_space=pltpu.SMEM),
                  pl.BlockSpec((n_parts, W_IN_WIN, tc), lambda l, j: (0, l, j)), nat, nat, nat],
        out_specs=pl.BlockSpec((4, 1, tc, W_IN_COLS), lambda l, j: (0, l, j, 0)),
        out_shape=jax.ShapeDtypeStruct((4, DEPTH, D_MODEL, W_IN_COLS), F32),
        compiler_params=_params(2))(core, parts, w, m, v)


def kernel(x, positions, norm_pre, w_in, attn_sinks, conv_w, conv_b, mla_q_norm, mla_w_uq, mla_kv_norm, mla_w_ukv, group_norm, w_out, norm_post, loss_target, m_norm_pre, m_w_in, m_attn_sinks, m_conv_w, m_conv_b, m_mla_q_norm, m_mla_w_uq, m_mla_kv_norm, m_mla_w_ukv, m_group_norm, m_w_out, m_norm_post, v_norm_pre, v_w_in, v_attn_sinks, v_conv_w, v_conv_b, v_mla_q_norm, v_mla_w_uq, v_mla_kv_norm, v_mla_w_ukv, v_group_norm, v_w_out, v_norm_post):
    local = dict(norm_pre=norm_pre, w_in=w_in, attn_sinks=attn_sinks, conv_w=conv_w, conv_b=conv_b,
                 mla_q_norm=mla_q_norm, mla_w_uq=mla_w_uq, mla_kv_norm=mla_kv_norm, mla_w_ukv=mla_w_ukv,
                 group_norm=group_norm, w_out=w_out, norm_post=norm_post)
    mom = dict(norm_pre=m_norm_pre, w_in=m_w_in, attn_sinks=m_attn_sinks, conv_w=m_conv_w, conv_b=m_conv_b,
               mla_q_norm=m_mla_q_norm, mla_w_uq=m_mla_w_uq, mla_kv_norm=m_mla_kv_norm, mla_w_ukv=m_mla_w_ukv,
               group_norm=m_group_norm, w_out=m_w_out, norm_post=m_norm_post)
    vel = dict(norm_pre=v_norm_pre, w_in=v_w_in, attn_sinks=v_attn_sinks, conv_w=v_conv_w, conv_b=v_conv_b,
               mla_q_norm=v_mla_q_norm, mla_w_uq=v_mla_w_uq, mla_kv_norm=v_mla_kv_norm, mla_w_ukv=v_mla_w_ukv,
               group_norm=v_group_norm, w_out=v_w_out, norm_post=v_norm_post)

    c = lax.axis_index("c")

    tile = 16
    slot_rows = 464
    shift = 8 * lax.axis_index("y") + 4 * c
    wt = lax.dynamic_update_slice(jnp.zeros((DEPTH, slot_rows, D_MODEL), BF),
                                  jnp.transpose(w_in, (0, 2, 1)).astype(BF), (0, shift, 0))
    payload = jnp.concatenate([wt.reshape(DEPTH * slot_rows, D_MODEL),
                               w_out.astype(BF).reshape(DEPTH * 128, D_MODEL),
                               _pack_small([local[n] for n in SMALL_SHARDED], BF)], axis=0)
    gathered = _all_gather(payload)
    W = {n: local[n] for n in REPLICATED}

    def nat_rows(l, lo, hi):
        def piece(d, r0, r1):
            base = slot_rows * l - (W_IN_COLS * d) // tile * tile
            return gathered[d, base + r0:base + r1]

        out, run = [], None
        for r0 in range(lo, hi, tile):
            d0, d1 = r0 // W_IN_COLS, (r0 + tile - 1) // W_IN_COLS
            if d0 == d1 and run is not None and run[0] == d0:
                run = (d0, run[1], r0 + tile)
                continue
            if run is not None:
                out.append(piece(*run))
                run = None
            if d0 == d1:
                run = (d0, r0, r0 + tile)
            else:
                out.append(piece(d0, r0, r0 + tile) + piece(d1, r0, r0 + tile))
        if run is not None:
            out.append(piece(*run))
        return out

    z = lambda n: [jnp.zeros((n, D_MODEL), BF)]
    W["wpt"] = [jnp.concatenate(nat_rows(l, 2464, 3488) + nat_rows(l, 512, 1664) + z(64) + nat_rows(l, 1664, 1696)
                                + z(96) + nat_rows(l, 1680, 1696) + nat_rows(l, 1664, 1680) + z(32)
                                + nat_rows(l, 0, 512) + nat_rows(l, 1696, 2464), axis=0) for l in range(DEPTH)]
    wo0 = DEPTH * slot_rows
    W["w_out"] = gathered[:, wo0:wo0 + DEPTH * 128].reshape(N_DEV, DEPTH, 128, D_MODEL).transpose(1, 0, 2, 3).reshape(
        DEPTH, D_MODEL, D_MODEL)
    flat = gathered[:, wo0 + DEPTH * 128:].reshape(N_DEV, SMALL_ROWS * D_MODEL)
    off = 0
    for n in SMALL_SHARDED:
        depth, rows, width = local[n].shape
        size = depth * rows * width
        W[n] = flat[:, off:off + size].reshape(N_DEV, depth, rows, width).transpose(1, 2, 0, 3).reshape(
            depth, rows, N_DEV * width)
        off += size

    sq, grad_x, g = _local_step(x[0], positions[0], W, loss_target[0])
    loss = lax.psum(0.5 / D_MODEL * jnp.sum(sq), ("x", "y", "c"))

    cols = []
    for n in SMALL_SHARDED:
        depth, rows, width = local[n].shape
        cols.append(jnp.stack(g[n]).reshape(depth, rows, N_DEV, width).transpose(2, 0, 1, 3).reshape(N_DEV, -1))
    rep = jnp.concatenate([a.reshape(-1) for n in REPLICATED for a in g[n]])
    cols.append(jnp.broadcast_to(rep[None], (N_DEV, rep.shape[0])))
    small = jnp.concatenate(cols, axis=1)
    small = jnp.pad(small, ((0, 0), (0, SMALL_ROWS * D_MODEL - small.shape[1]))).reshape(N_DEV, SMALL_ROWS, D_MODEL)
    krs = [p[C_CKR + 64:C_CKR + 96] + _swap_rows32(p[C_CKRS + 64:C_CKRS + 96]) for p in g["w_in_pt"]]
    pieces = []
    for d in range(N_DEV):
        lo = W_IN_COLS * d // 8 * 8
        for l in range(DEPTH):
            pieces += _rows_of_w_in_t(lo, lo + W_IN_WIN, g["w_in_pt"][l], krs[l])
        pieces += [g["w_out"][l][128 * d:128 * (d + 1)] for l in range(DEPTH)]
        pieces.append(small[d])
    blocks = jnp.concatenate(pieces, axis=0).astype(BF).reshape(N_DEV, -1, D_MODEL)
    mine = lax.dynamic_index_in_dim(blocks.reshape(N_CHIP, 2, -1, D_MODEL), c, axis=1, keepdims=False)
    received = _chip_exchange(_pair_sum(mine, _sibling_swap(blocks)))

    out = _unpack_state(_adamw(received, _pack_state([local, mom, vel])), local)
    out["w_in"] = _adamw_w_in(received, w_in, m_w_in, v_w_in, c.astype(jnp.int32).reshape(1))
    return (loss, grad_x[None], *[out[n][t] for t in range(4) for n in ORDER])
```

```python
import functools

import jax
import jax.numpy as jnp
import numpy as np
from jax import lax
from jax.experimental import pallas as pl
from jax.experimental.pallas import tpu as pltpu

F32 = jnp.float32
BF = jnp.bfloat16
MESH = pl.DeviceIdType.MESH

D_MODEL = 1024
DEPTH = 2
EPS = 1e-6
N_DEV = 8
VMEM_LIMIT = 56 * 1024 * 1024
NEG = -1e30
MLA_SCALE = 96.0 ** -0.5
SB_SCALE = 0.125
LOG2E = 1.4426950408889634

NP = 3712
NF = 2432
NB = 1280
C_GATE = 0
C_BB = 1024
C_BC = 1280
C_BX = 1536
C_CQ = 1792
C_CKV = 2048
C_CKR = 2176
C_CKRS = 2304
C_AQ = 0
C_AK = 256
C_AV = 384
C_DQ = 512
C_DK = 768
C_DV = 1024
C_END = 3712

def _swap32(a):
    return jnp.concatenate([a[:, 16:32], a[:, 0:16]], axis=1)

ADAM_LR, ADAM_B1, ADAM_B2, ADAM_EPS, ADAM_WD, ADAM_STEP = 0.001, 0.9, 0.999, 1e-08, 0.01, 10


def _dot(a, b):
    return jnp.dot(a, b, preferred_element_type=F32)


def _dot_nt(a, b):
    return lax.dot_general(a, b, (((1,), (1,)), ((), ())), preferred_element_type=F32)


def _dot_tn(a, b):
    return lax.dot_general(a, b, (((0,), (0,)), ((), ())), preferred_element_type=F32)


def _params(n_grid):
    return pltpu.CompilerParams(dimension_semantics=("arbitrary",) * n_grid, vmem_limit_bytes=VMEM_LIMIT)


def _rms_fwd(x, g):
    r = lax.rsqrt(jnp.mean(x * x, axis=-1, keepdims=True) + EPS)
    return (x * r) * g, r


def _rms_bwd(x, g, r, dy, width=None):
    n = x.shape[-1] if width is None else width
    u = dy * g
    dx = r * u - x * (r * r * r) * (jnp.sum(x * u, axis=-1, keepdims=True) / n)
    return dx, dy * (x * r)


def _iota(shape, axis):
    return lax.broadcasted_iota(jnp.int32, shape, axis)


def _inproj_fwd(x, g, wpt):
    T = x.shape[0]
    tm = 256

    def body(x_ref, g_ref, w_ref, h32_ref, h16_ref, xn_ref):
        xn, _ = _rms_fwd(x_ref[...], g_ref[...])
        xn = xn.astype(BF)
        xn_ref[...] = xn
        h = _dot_nt(xn, w_ref[...])
        h32_ref[...] = h[:, :NF]
        h16_ref[...] = h[:, NF:].astype(BF)

    return pl.pallas_call(
        body, name="inproj_fwd", grid=(T // tm,),
        in_specs=[pl.BlockSpec((tm, D_MODEL), lambda n: (n, 0)),
                  pl.BlockSpec((1, D_MODEL), lambda n: (0, 0)),
                  pl.BlockSpec((NP, D_MODEL), lambda n: (0, 0))],
        out_specs=[pl.BlockSpec((tm, NF), lambda n: (n, 0)),
                   pl.BlockSpec((tm, NB), lambda n: (n, 0)),
                   pl.BlockSpec((tm, D_MODEL), lambda n: (n, 0))],
        out_shape=[jax.ShapeDtypeStruct((T, NF), F32), jax.ShapeDtypeStruct((T, NB), BF),
                   jax.ShapeDtypeStruct((T, D_MODEL), BF)],
        compiler_params=_params(1))(x, g, wpt)


def _inproj_bwd(parts, wpt, x, xn, g, dxo):
    T = x.shape[0]
    tm = 256
    np_ = len(parts)
    chunks = (0, 1280, 2560, NP)
    assert sum(p.shape[1] for p in parts) == C_END == NP

    def body(*refs):
        part_refs = refs[:np_]
        w_ref, x_ref, xn_ref, g_ref, dxo_ref, dw_ref, dx_ref, dg_ref = refs[np_:]
        n = pl.program_id(0)

        @pl.when(n == 0)
        def _():
            dw_ref[...] = jnp.zeros_like(dw_ref)
            dg_ref[...] = jnp.zeros_like(dg_ref)

        dh = jnp.concatenate([r[...].astype(BF) for r in part_refs], axis=1)
        xnv = xn_ref[...]
        for lo, hi in zip(chunks[:-1], chunks[1:]):
            dw_ref[:, lo:hi] += _dot_tn(xnv, dh[:, lo:hi])
        dxn = _dot(dh, w_ref[...])
        xv = x_ref[...]
        _, r = _rms_fwd(xv, g_ref[...])
        dx, dgt = _rms_bwd(xv, g_ref[...], r, dxn)
        dx_ref[...] = dxo_ref[...] + dx
        dg_ref[...] += jnp.sum(dgt, axis=0, keepdims=True)

    once = pl.Buffered(1)
    return pl.pallas_call(
        body, name="inproj_bwd", grid=(T // tm,),
        in_specs=[pl.BlockSpec((tm, p.shape[1]), lambda n: (n, 0)) for p in parts]
        + [pl.BlockSpec((NP, D_MODEL), lambda n: (0, 0), pipeline_mode=once),
           pl.BlockSpec((tm, D_MODEL), lambda n: (n, 0)),
           pl.BlockSpec((tm, D_MODEL), lambda n: (n, 0)),
           pl.BlockSpec((1, D_MODEL), lambda n: (0, 0)),
           pl.BlockSpec((tm, D_MODEL), lambda n: (n, 0))],
        out_specs=[pl.BlockSpec((D_MODEL, NP), lambda n: (0, 0), pipeline_mode=once),
                   pl.BlockSpec((tm, D_MODEL), lambda n: (n, 0)),
                   pl.BlockSpec((1, D_MODEL), lambda n: (0, 0))],
        out_shape=[jax.ShapeDtypeStruct((D_MODEL, NP), F32), jax.ShapeDtypeStruct((T, D_MODEL), F32),
                   jax.ShapeDtypeStruct((1, D_MODEL), F32)],
        compiler_params=_params(1))(*parts, wpt, x, xn, g, dxo)


SWA_BLK = 128
SWA_TQ = 2048


def _bdot_nt(a, b):
    return lax.dot_general(a, b, (((2,), (2,)), ((0,), (0,))), preferred_element_type=F32)


def _bdot(a, b):
    return lax.dot_general(a, b, (((2,), (1,)), ((0,), (0,))), preferred_element_type=F32)


def _bdot_tn(a, b):
    return lax.dot_general(a, b, (((1,), (1,)), ((0,), (0,))), preferred_element_type=F32)


def _swa_probs(q, kc, kp, sink, mask_c, mask_p):
    sc = jnp.where(mask_c, _bdot_nt(q, kc) * SB_SCALE, NEG)
    sp = jnp.where(mask_p, _bdot_nt(q, kp) * SB_SCALE, NEG)
    m = jnp.maximum(jnp.maximum(jnp.max(sc, axis=-1, keepdims=True), jnp.max(sp, axis=-1, keepdims=True)), sink)
    pc = jnp.exp(sc - m)
    pp = jnp.exp(sp - m)
    ps = jnp.exp(sink - m)
    inv = 1.0 / (jnp.sum(pc, axis=-1, keepdims=True) + jnp.sum(pp, axis=-1, keepdims=True) + ps)
    return pc * inv, pp * inv, ps * inv


def _swa_masks(n, nb):
    blk = _iota((nb, SWA_BLK, SWA_BLK), 0)
    row = _iota((nb, SWA_BLK, SWA_BLK), 1)
    col = _iota((nb, SWA_BLK, SWA_BLK), 2)
    return col <= row, jnp.logical_and(col > row, jnp.logical_or(blk > 0, n > 0))


def _swa_specs(tq):
    halo = tq // SWA_BLK
    return [pl.BlockSpec(memory_space=pltpu.SMEM),
            pl.BlockSpec((tq, 256), lambda n: (n, C_AQ // 256)),
            pl.BlockSpec((tq, 128), lambda n: (n, C_AK // 128)),
            pl.BlockSpec((SWA_BLK, 128), lambda n: (jnp.maximum(n * halo - 1, 0), C_AK // 128)),
            pl.BlockSpec((tq, 128), lambda n: (n, C_AV // 128)),
            pl.BlockSpec((SWA_BLK, 128), lambda n: (jnp.maximum(n * halo - 1, 0), C_AV // 128))]


def _swa_blocked(cur_ref, prev_ref, gs, nb):
    cur = cur_ref[:, gs].reshape(nb, SWA_BLK, 64)
    prev = jnp.concatenate([prev_ref[:, gs].reshape(1, SWA_BLK, 64), cur[:nb - 1]], axis=0) if nb > 1 \
        else prev_ref[:, gs].reshape(1, SWA_BLK, 64)
    return cur, prev


def _swa_masks2(n, nb):
    mask_c, mask_p = _swa_masks(n, nb)
    return jnp.concatenate([mask_c, mask_c], axis=1), jnp.concatenate([mask_p, mask_p], axis=1)


def _swa_stacked(ref, g, nb):
    a = ref[:, 128 * g:128 * g + 64].reshape(nb, SWA_BLK, 64)
    b = ref[:, 128 * g + 64:128 * g + 128].reshape(nb, SWA_BLK, 64)
    return jnp.concatenate([a, b], axis=1)


def _swa_fwd(h16, sinks):
    T = h16.shape[0]
    tq = SWA_TQ if T % SWA_TQ == 0 else SWA_BLK
    nb = tq // SWA_BLK

    def body(s_ref, q_ref, kc_ref, kp_ref, vc_ref, vp_ref, o_ref):
        n = pl.program_id(0)
        mask_c, mask_p = _swa_masks2(n, nb)
        first = _iota((1, 2 * SWA_BLK, 1), 1) < SWA_BLK
        for g in range(2):
            gs = slice(g * 64, (g + 1) * 64)
            kc, kp = _swa_blocked(kc_ref, kp_ref, gs, nb)
            vc, vp = _swa_blocked(vc_ref, vp_ref, gs, nb)
            q = _swa_stacked(q_ref, g, nb)
            sink = jnp.where(first, s_ref[2 * g], s_ref[2 * g + 1])
            pc, pp, _ = _swa_probs(q, kc, kp, sink, mask_c, mask_p)
            o = _bdot(pc.astype(BF), vc) + _bdot(pp.astype(BF), vp)
            o_ref[:, 128 * g:128 * g + 64] = o[:, :SWA_BLK].reshape(tq, 64)
            o_ref[:, 128 * g + 64:128 * g + 128] = o[:, SWA_BLK:].reshape(tq, 64)

    return pl.pallas_call(
        body, name="swa_fwd", grid=(T // tq,), in_specs=_swa_specs(tq),
        out_specs=pl.BlockSpec((tq, 256), lambda n: (n, 0)),
        out_shape=jax.ShapeDtypeStruct((T, 256), F32),
        compiler_params=_params(1))(sinks, h16, h16, h16, h16, h16)


def _swa_bwd(h16, sinks, dya):
    T = h16.shape[0]
    tq = SWA_TQ if T % SWA_TQ == 0 else SWA_BLK
    nb = tq // SWA_BLK

    def body(s_ref, q_ref, kc_ref, kp_ref, vc_ref, vp_ref, do_ref, dq_ref, dk_ref, dv_ref, ds_ref):
        n = pl.program_id(0)

        @pl.when(n == 0)
        def _():
            dk_ref[...] = jnp.zeros_like(dk_ref)
            dv_ref[...] = jnp.zeros_like(dv_ref)
            ds_ref[...] = jnp.zeros_like(ds_ref)

        mask_c, mask_p = _swa_masks2(n, nb)
        first = _iota((1, 2 * SWA_BLK, 1), 1) < SWA_BLK
        rows = pl.ds(pl.multiple_of(n * tq, tq), tq)
        before = pl.ds(pl.multiple_of(jnp.maximum(n * nb - 1, 0) * SWA_BLK, SWA_BLK), SWA_BLK)
        lane = _iota((8, 128), 1)
        row8 = _iota((8, 128), 0)

        def to_keys(own, prev):
            if nb == 1:
                return own
            return own + jnp.concatenate([prev[1:], jnp.zeros((1, SWA_BLK, 64), F32)], axis=0)

        for g in range(2):
            gs = slice(g * 64, (g + 1) * 64)
            q = _swa_stacked(q_ref, g, nb)
            kc, kp = _swa_blocked(kc_ref, kp_ref, gs, nb)
            vc, vp = _swa_blocked(vc_ref, vp_ref, gs, nb)
            sink = jnp.where(first, s_ref[2 * g], s_ref[2 * g + 1])
            pc, pp, ps = _swa_probs(q, kc, kp, sink, mask_c, mask_p)
            pcb, ppb = pc.astype(BF), pp.astype(BF)
            do = _swa_stacked(do_ref, g, nb)
            dob = do.astype(BF)
            o = _bdot(pcb, vc) + _bdot(ppb, vp)
            dd = jnp.sum(do * o, axis=-1, keepdims=True)
            dsc = (pc * (_bdot_nt(dob, vc) - dd) * SB_SCALE).astype(BF)
            dsp = (pp * (_bdot_nt(dob, vp) - dd) * SB_SCALE).astype(BF)
            dq = (_bdot(dsc, kc) + _bdot(dsp, kp)).astype(BF)
            dq_ref[:, 128 * g:128 * g + 64] = dq[:, :SWA_BLK].reshape(tq, 64)
            dq_ref[:, 128 * g + 64:128 * g + 128] = dq[:, SWA_BLK:].reshape(tq, 64)
            dkp, dvp = _bdot_tn(dsp, q), _bdot_tn(ppb, dob)
            dk_ref[rows, gs] += to_keys(_bdot_tn(dsc, q), dkp).reshape(tq, 64)
            dv_ref[rows, gs] += to_keys(_bdot_tn(pcb, dob), dvp).reshape(tq, 64)
            dk_ref[before, gs] += dkp[0]
            dv_ref[before, gs] += dvp[0]
            psd = ps * dd
            for hh in range(2):
                part = psd[:, hh * SWA_BLK:(hh + 1) * SWA_BLK]
                ds_ref[...] += jnp.where(jnp.logical_and(lane == 2 * g + hh, row8 == 0), -jnp.sum(part), 0.0)

    return pl.pallas_call(
        body, name="swa_bwd", grid=(T // tq,),
        in_specs=_swa_specs(tq) + [pl.BlockSpec((tq, 256), lambda n: (n, 0))],
        out_specs=[pl.BlockSpec((tq, 256), lambda n: (n, 0)),
                   pl.BlockSpec((T, 128), lambda n: (0, 0)),
                   pl.BlockSpec((T, 128), lambda n: (0, 0)),
                   pl.BlockSpec((8, 128), lambda n: (0, 0))],
        out_shape=[jax.ShapeDtypeStruct((T, 256), BF), jax.ShapeDtypeStruct((T, 128), F32),
                   jax.ShapeDtypeStruct((T, 128), F32), jax.ShapeDtypeStruct((8, 128), F32)],
        compiler_params=_params(1))(sinks, h16, h16, h16, h16, h16, dya)


def _conv_u(bc_ref, bx_ref, bch_ref, bxh_ref, n, tm):
    u = bc_ref[...] * bx_ref[...]
    uh = bch_ref[...] * bxh_ref[...] * (n > 0).astype(F32)
    rowi = _iota((tm, 256), 0)
    u1 = jnp.where(rowi == 0, uh[7:8, :], pltpu.roll(u, 1, axis=0))
    u2 = jnp.where(rowi == 0, uh[6:7, :], jnp.where(rowi == 1, uh[7:8, :], pltpu.roll(u, 2, axis=0)))
    return u, u1, u2


def _conv_fwd(h32, cw, cb):
    T = h32.shape[0]
    tm = 512 if T % 512 == 0 else T
    hb = tm // 8

    def body(bb_ref, bc_ref, bx_ref, bch_ref, bxh_ref, w_ref, b_ref, o_ref):
        n = pl.program_id(0)
        u, u1, u2 = _conv_u(bc_ref, bx_ref, bch_ref, bxh_ref, n, tm)
        y = w_ref[0:1, :] * u2 + w_ref[1:2, :] * u1 + w_ref[2:3, :] * u + b_ref[...]
        o_ref[...] = bb_ref[...] * y

    halo = lambda c: pl.BlockSpec((8, 256), lambda n: (jnp.maximum(n * hb - 1, 0), c // 256))
    return pl.pallas_call(
        body, name="conv_fwd", grid=(T // tm,),
        in_specs=[pl.BlockSpec((tm, 256), lambda n: (n, C_BB // 256)),
                  pl.BlockSpec((tm, 256), lambda n: (n, C_BC // 256)),
                  pl.BlockSpec((tm, 256), lambda n: (n, C_BX // 256)),
                  halo(C_BC), halo(C_BX),
                  pl.BlockSpec((8, 256), lambda n: (0, 0)),
                  pl.BlockSpec((1, 256), lambda n: (0, 0))],
        out_specs=pl.BlockSpec((tm, 256), lambda n: (n, 0)),
        out_shape=jax.ShapeDtypeStruct((T, 256), F32),
        compiler_params=_params(1))(h32, h32, h32, h32, h32, cw, cb)


def _conv_bwd(h32, cw, cb, dyb):
    T = h32.shape[0]
    tm = 512 if T % 512 == 0 else T
    hb = tm // 8
    nt = T // tm

    def body(bb_ref, bc_ref, bx_ref, bch_ref, bxh_ref, bbn_ref, dy_ref, dyn_ref, w_ref, b_ref,
             dbb_ref, dbc_ref, dbx_ref, dw_ref):
        n = pl.program_id(0)

        @pl.when(n == 0)
        def _():
            dw_ref[...] = jnp.zeros_like(dw_ref)

        u, u1, u2 = _conv_u(bc_ref, bx_ref, bch_ref, bxh_ref, n, tm)
        w0, w1, w2 = w_ref[0:1, :], w_ref[1:2, :], w_ref[2:3, :]
        y = w0 * u2 + w1 * u1 + w2 * u + b_ref[...]
        dyb_ = dy_ref[...]
        dbb_ref[...] = (dyb_ * y).astype(BF)
        dy = dyb_ * bb_ref[...]
        dyn = dyn_ref[...] * bbn_ref[...] * (n < nt - 1).astype(F32)
        rowi = _iota((tm, 256), 0)
        dy1 = jnp.where(rowi == tm - 1, dyn[0:1, :], pltpu.roll(dy, tm - 1, axis=0))
        dy2 = jnp.where(rowi == tm - 2, dyn[0:1, :],
                        jnp.where(rowi == tm - 1, dyn[1:2, :], pltpu.roll(dy, tm - 2, axis=0)))
        du = w2 * dy + w1 * dy1 + w0 * dy2
        dbc_ref[...] = (du * bx_ref[...]).astype(BF)
        dbx_ref[...] = (du * bc_ref[...]).astype(BF)
        dw_ref[0:1, :] += jnp.sum(dy * u2, axis=0, keepdims=True)
        dw_ref[1:2, :] += jnp.sum(dy * u1, axis=0, keepdims=True)
        dw_ref[2:3, :] += jnp.sum(dy * u, axis=0, keepdims=True)
        dw_ref[3:4, :] += jnp.sum(dy, axis=0, keepdims=True)

    halo = lambda c: pl.BlockSpec((8, 256), lambda n: (jnp.maximum(n * hb - 1, 0), c // 256))
    nxt = lambda c: pl.BlockSpec((8, 256), lambda n: (jnp.minimum((n + 1) * hb, T // 8 - 1), c // 256))
    cur = lambda c: pl.BlockSpec((tm, 256), lambda n: (n, c // 256))
    return pl.pallas_call(
        body, name="conv_bwd", grid=(nt,),
        in_specs=[cur(C_BB), cur(C_BC), cur(C_BX), halo(C_BC), halo(C_BX), nxt(C_BB),
                  cur(0), nxt(0),
                  pl.BlockSpec((8, 256), lambda n: (0, 0)),
                  pl.BlockSpec((1, 256), lambda n: (0, 0))],
        out_specs=[cur(0), cur(0), cur(0), pl.BlockSpec((8, 256), lambda n: (0, 0))],
        out_shape=[jax.ShapeDtypeStruct((T, 256), BF)] * 3 + [jax.ShapeDtypeStruct((8, 256), F32)],
        compiler_params=_params(1))(h32, h32, h32, h32, h32, h32, dyb, dyb, cw, cb)


def _cprep_specs(tm):
    return [pl.BlockSpec((tm, 256), lambda n: (n, C_CQ // 256)),
            pl.BlockSpec((tm, 128), lambda n: (n, C_CKV // 128)),
            pl.BlockSpec((tm, 128), lambda n: (n, C_CKR // 128)),
            pl.BlockSpec((tm, 128), lambda n: (n, C_CKRS // 128)),
            pl.BlockSpec((1, 256), lambda n: (0, 0)),
            pl.BlockSpec((1, 128), lambda n: (0, 0)),
            pl.BlockSpec((tm, 128), lambda n: (n, 0)),
            pl.BlockSpec((tm, 128), lambda n: (n, 0))]


def _cprep_fwd(h32, gq, gkv, wuq2, wkv2, cosk, sin):
    T = h32.shape[0]
    tm = 512 if T % 512 == 0 else T

    def body(cq_ref, ckv_ref, ckr_ref, ckrs_ref, gq_ref, gkv_ref, cos_ref, sin_ref, wuq_ref, wkv_ref,
             q_ref, k_ref, v_ref):
        cosk_, sin_ = cos_ref[...], sin_ref[...]
        cosq = cosk_ + (_iota((tm, 128), 1) < 64).astype(F32)
        cqn, _ = _rms_fwd(cq_ref[...], gq_ref[...])
        q2 = _dot(cqn.astype(BF), wuq_ref[...])
        ckvn, _ = _rms_fwd(ckv_ref[...], gkv_ref[...])
        kv2 = _dot(ckvn.astype(BF), wkv_ref[...])
        kr = ckr_ref[...] * cosk_ + ckrs_ref[...] * sin_
        for h in range(4):
            hs = slice(h * 128, (h + 1) * 128)
            q_ref[:, hs] = ((q2[:, hs] * cosq + q2[:, 512 + h * 128:512 + (h + 1) * 128] * sin_) * MLA_SCALE).astype(BF)
            k_ref[:, hs] = (kv2[:, hs] + kr).astype(BF)
        ones = _iota((tm, 512), 1) % 128 == 64
        v_ref[...] = jnp.where(ones, 1.0, kv2[:, 512:]).astype(BF)

    return pl.pallas_call(
        body, name="cprep_fwd", grid=(T // tm,),
        in_specs=_cprep_specs(tm) + [pl.BlockSpec((256, 1024), lambda n: (0, 0)),
                                     pl.BlockSpec((128, 1024), lambda n: (0, 0))],
        out_specs=[pl.BlockSpec((tm, 512), lambda n: (n, 0))] * 3,
        out_shape=[jax.ShapeDtypeStruct((T, 512), BF)] * 3,
        compiler_params=_params(1))(h32, h32, h32, h32, gq, gkv, cosk, sin, wuq2, wkv2)


def _cprep_bwd(h32, gq, gkv, wuq2t, wkv2t, cosk, sin, dq, dk, dv):
    T = h32.shape[0]
    tm = 512 if T % 512 == 0 else T

    def body(cq_ref, ckv_ref, ckr_ref, ckrs_ref, gq_ref, gkv_ref, cos_ref, sin_ref, wuq_ref, wkv_ref,
             dq_ref, dk_ref, dv_ref,
             dcq_ref, dckv_ref, dckr_ref, dckrs_ref, dwuq_ref, dwkv_ref, dgq_ref, dgkv_ref):
        n = pl.program_id(0)

        @pl.when(n == 0)
        def _():
            dwuq_ref[...] = jnp.zeros_like(dwuq_ref)
            dwkv_ref[...] = jnp.zeros_like(dwkv_ref)
            dgq_ref[...] = jnp.zeros_like(dgq_ref)
            dgkv_ref[...] = jnp.zeros_like(dgkv_ref)

        cosk_, sin_ = cos_ref[...], sin_ref[...]
        cosq = cosk_ + (_iota((tm, 128), 1) < 64).astype(F32)
        dkr = jnp.zeros((tm, 128), F32)
        plain, swapped = [], []
        for h in range(4):
            hs = slice(h * 128, (h + 1) * 128)
            dqh = dq_ref[:, hs] * MLA_SCALE
            plain.append((dqh * cosq).astype(BF))
            swapped.append((dqh * sin_).astype(BF))
            dkr = dkr + dk_ref[:, hs]
        dq2 = jnp.concatenate(plain + swapped, axis=1)
        dkv2 = jnp.concatenate([dk_ref[...].astype(BF), dv_ref[...].astype(BF)], axis=1)
        dckr_ref[...] = (dkr * cosk_).astype(BF)
        dckrs_ref[...] = (dkr * sin_).astype(BF)

        cq, gq_ = cq_ref[...], gq_ref[...]
        cqn, rq = _rms_fwd(cq, gq_)
        dwuq_ref[...] += _dot_tn(cqn.astype(BF), dq2)
        dcq, dgt = _rms_bwd(cq, gq_, rq, _dot(dq2, wuq_ref[...]))
        dcq_ref[...] = dcq.astype(BF)
        dgq_ref[...] += jnp.sum(dgt, axis=0, keepdims=True)

        ckv, gkv_ = ckv_ref[...], gkv_ref[...]
        ckvn, rkv = _rms_fwd(ckv, gkv_)
        dwkv_ref[...] += _dot_tn(ckvn.astype(BF), dkv2)
        dckv, dgt2 = _rms_bwd(ckv, gkv_, rkv, _dot(dkv2, wkv_ref[...]))
        dckv_ref[...] = dckv.astype(BF)
        dgkv_ref[...] += jnp.sum(dgt2, axis=0, keepdims=True)

    row = lambda w: pl.BlockSpec((tm, w), lambda n: (n, 0))
    return pl.pallas_call(
        body, name="cprep_bwd", grid=(T // tm,),
        in_specs=_cprep_specs(tm) + [pl.BlockSpec((1024, 256), lambda n: (0, 0)),
                                     pl.BlockSpec((1024, 128), lambda n: (0, 0)),
                                     row(512), row(512), row(512)],
        out_specs=[row(256), row(128), row(128), row(128),
                   pl.BlockSpec((256, 1024), lambda n: (0, 0)), pl.BlockSpec((128, 1024), lambda n: (0, 0)),
                   pl.BlockSpec((1, 256), lambda n: (0, 0)), pl.BlockSpec((1, 128), lambda n: (0, 0))],
        out_shape=[jax.ShapeDtypeStruct((T, 256), BF), jax.ShapeDtypeStruct((T, 128), BF),
                   jax.ShapeDtypeStruct((T, 128), BF), jax.ShapeDtypeStruct((T, 128), BF),
                   jax.ShapeDtypeStruct((256, 1024), F32), jax.ShapeDtypeStruct((128, 1024), F32),
                   jax.ShapeDtypeStruct((1, 256), F32), jax.ShapeDtypeStruct((1, 128), F32)],
        compiler_params=_params(1))(h32, h32, h32, h32, gq, gkv, cosk, sin, wuq2t, wkv2t, dq, dk, dv)


MLA_TILE = 512
MLA_HEADS_PER_STEP = 4


def _causal_mask(t):
    return _iota((t, t), 1) <= _iota((t, t), 0)


def _mla_fwd(q, k, v):
    T = q.shape[0]
    tq = MLA_TILE

    def body(q_ref, k_ref, v_ref, o_ref, lse_ref):
        i = pl.program_id(1)
        mask = _causal_mask(tq)
        heads = [slice(128 * h, 128 * h + 128) for h in range(MLA_HEADS_PER_STEP)]
        qs = [q_ref[:, hs] for hs in heads]

        def step(j, carry, masked):
            rows = pl.ds(pl.multiple_of(j * tq, tq), tq)
            out = []
            for hh, hs in enumerate(heads):
                m, acc = carry[hh]
                s = _dot_nt(qs[hh], k_ref[rows, hs])
                if masked:
                    s = jnp.where(mask, s, NEG)
                m_new = jnp.maximum(m, jnp.max(s, axis=-1, keepdims=True))
                p = jnp.exp((s - m_new).astype(BF))
                acc = jnp.exp(m - m_new) * acc + _dot(p, v_ref[rows, hs])
                out.append((m_new, acc))
            return tuple(out)

        init = ((jnp.full((tq, 1), NEG, F32), jnp.zeros((tq, 128), F32)),) * len(heads)
        carry = lax.fori_loop(0, i // 2, lambda t, c: step(2 * t + 1, step(2 * t, c, False), False), init)
        carry = lax.cond(i % 2 == 1, lambda c: step(i - 1, c, False), lambda c: c, carry)
        carry = step(i, carry, True)
        for hh, hs in enumerate(heads):
            m, acc = carry[hh]
            l = acc[:, 64:65]
            o_ref[:, hs] = acc * (1.0 / l)
            lse_ref[:, hs] = jnp.broadcast_to(m + jnp.log(l), (tq, 128))

    width = 128 * MLA_HEADS_PER_STEP
    blk = pl.BlockSpec((tq, width), lambda h, i: (i, h))
    full = pl.BlockSpec((T, width), lambda h, i: (0, h))
    return pl.pallas_call(
        body, name="mla_fwd", grid=(4 // MLA_HEADS_PER_STEP, T // tq), in_specs=[blk, full, full],
        out_specs=[blk, blk],
        out_shape=[jax.ShapeDtypeStruct((T, 512), F32), jax.ShapeDtypeStruct((T, 512), F32)],
        compiler_params=_params(2))(q, k, v)


def _mla_bwd(q, k, v, o, lse, do):
    T = q.shape[0]
    tq = MLA_TILE

    def body(q_ref, k_ref, v_ref, o_ref, lse_ref, do_ref, dq_ref, dk_ref, dv_ref):
        i = pl.program_id(1)

        @pl.when(i == 0)
        def _():
            dk_ref[...] = jnp.zeros_like(dk_ref)
            dv_ref[...] = jnp.zeros_like(dv_ref)

        heads = [slice(0, 128), slice(128, 256)]
        mask = _causal_mask(tq)
        qs, dobs, dds, lses = [], [], [], []
        for hs in heads:
            do = do_ref[:, hs]
            qs.append(q_ref[:, hs])
            dobs.append(do.astype(BF))
            dds.append(jnp.sum(do * o_ref[:, hs], axis=-1, keepdims=True))
            lses.append(lse_ref[:, hs.start:hs.start + 1])

        def step(j, dqs, masked):
            rows = pl.ds(pl.multiple_of(j * tq, tq), tq)
            out = []
            for hh, hs in enumerate(heads):
                kj, vj = k_ref[rows, hs], v_ref[rows, hs]
                s = _dot_nt(qs[hh], kj)
                if masked:
                    s = jnp.where(mask, s, NEG)
                p = jnp.exp(s - lses[hh])
                ds = (p * (_dot_nt(dobs[hh], vj) - dds[hh])).astype(BF)
                dk_ref[rows, hs] += _dot_tn(ds, qs[hh])
                dv_ref[rows, hs] += _dot_tn(p.astype(BF), dobs[hh])
                out.append(dqs[hh] + _dot(ds, kj))
            return tuple(out)

        dqs = lax.fori_loop(0, i // 2, lambda t, c: step(2 * t + 1, step(2 * t, c, False), False),
                            (jnp.zeros((tq, 128), F32),) * 2)
        dqs = lax.cond(i % 2 == 1, lambda c: step(i - 1, c, False), lambda c: c, dqs)
        dqs = step(i, dqs, True)
        for hh, hs in enumerate(heads):
            dq_ref[:, hs] = dqs[hh]

    blk = pl.BlockSpec((tq, 256), lambda h, i: (i, h))
    full = pl.BlockSpec((T, 256), lambda h, i: (0, h), pipeline_mode=pl.Buffered(1))
    return pl.pallas_call(
        body, name="mla_bwd", grid=(2, T // tq), in_specs=[blk, full, full, blk, blk, blk],
        out_specs=[blk, full, full],
        out_shape=[jax.ShapeDtypeStruct((T, 512), F32)] * 3,
        compiler_params=_params(2))(q, k, v, o, lse, do)


def _sb_tile(qk, rr, strict, masked, upper):
    z2 = qk * (SB_SCALE * LOG2E)
    l1 = jnp.log2(1.0 + jnp.exp2(-jnp.abs(z2)))
    lk = -jnp.maximum(z2, 0.0) - l1
    if masked:
        lk = jnp.where(strict, lk, 0.0)
    after = rr + _dot(lk.astype(BF), upper)
    ll = jnp.minimum(z2, 0.0) - l1
    a = jnp.exp2(ll + after)
    if masked:
        a = jnp.where(strict, a, 0.0)
    return ll, a, jnp.sum(lk, axis=-1, keepdims=True)


SB_TQ, SB_TK = 256, 256
SB_DEAD = -160.0


def _sb_walk(trips, one_step, carry):
    def alive(c):
        t, cr = c
        top = functools.reduce(jnp.maximum, [jnp.max(h[0]) for h in cr])
        return jnp.logical_and(t < trips, top > SB_DEAD)

    def body(c):
        t, cr = c
        return t + 1, one_step(t, cr)

    return lax.while_loop(alive, body, (jnp.int32(0), carry))[1]


def _sb_consts(tq, tk):
    row, col = _iota((tq, tk), 0), _iota((tq, tk), 1)
    strict = [col + d * tk < row for d in range(tq // tk)]
    r2, c2 = _iota((tk, tk), 0), _iota((tk, tk), 1)
    return strict, (r2 > c2).astype(BF), (r2 < c2).astype(BF)


def _sb_fwd(h16):
    T = h16.shape[0]
    tq, tk = SB_TQ, SB_TK
    nd = tq // tk

    def body(q0_ref, q1_ref, k0_ref, k1_ref, v0_ref, v1_ref, o_ref):
        i = pl.program_id(0)
        strict, upper, _ = _sb_consts(tq, tk)
        lane = _iota((tq, 128), 1)
        pairs = [slice(0, 128), slice(128, 256)]
        k_refs, v_refs = (k0_ref, k1_ref), (v0_ref, v1_ref)
        qms = []
        for q_ref in (q0_ref, q1_ref):
            q2 = q_ref[...]
            qms += [jnp.where(lane < 64, q2, jnp.zeros_like(q2)), jnp.where(lane >= 64, q2, jnp.zeros_like(q2))]

        def step(j, carry, d):
            rows = pl.ds(pl.multiple_of(j * tk, tk), tk)
            out = []
            for h in range(4):
                rr, acc = carry[h]
                _, a, rs = _sb_tile(_dot_nt(qms[h], k_refs[h // 2][rows, :]), rr, None if d is None else strict[d],
                                    d is not None, upper)
                out.append((rr + rs, acc + _dot(a.astype(BF), v_refs[h // 2][rows, :])))
            return tuple(out)

        carry = ((jnp.zeros((tq, 1), F32), jnp.zeros((tq, 128), F32)),) * 4
        for d in reversed(range(nd)):
            carry = step(nd * i + d, carry, d)
        carry = _sb_walk(nd * i, lambda t, c: step(nd * i - 1 - t, c, None), carry)
        for p, ps in enumerate(pairs):
            o_ref[:, ps] = jnp.where(lane < 64, carry[2 * p][1], carry[2 * p + 1][1])

    return pl.pallas_call(
        body, name="sb_fwd", grid=(T // tq,),
        in_specs=[pl.BlockSpec((tq, 128), lambda i: (i, C_DQ // 128)),
                  pl.BlockSpec((tq, 128), lambda i: (i, C_DQ // 128 + 1)),
                  pl.BlockSpec((T, 128), lambda i: (0, C_DK // 128)),
                  pl.BlockSpec((T, 128), lambda i: (0, C_DK // 128 + 1)),
                  pl.BlockSpec((T, 128), lambda i: (0, C_DV // 128)),
                  pl.BlockSpec((T, 128), lambda i: (0, C_DV // 128 + 1))],
        out_specs=pl.BlockSpec((tq, 256), lambda i: (i, 0)),
        out_shape=jax.ShapeDtypeStruct((T, 256), F32),
        compiler_params=_params(1))(h16, h16, h16, h16, h16, h16)


def _sb_bwd(h16, yd, dyd):
    T = h16.shape[0]
    tq, tk = SB_TQ, SB_TK
    nd = tq // tk

    def body(q0_ref, q1_ref, k0_ref, k1_ref, v0_ref, v1_ref, o_ref, do_ref, dq_ref, dk_ref, dv_ref):
        i = pl.program_id(0)

        @pl.when(i == 0)
        def _():
            dk_ref[...] = jnp.zeros_like(dk_ref)
            dv_ref[...] = jnp.zeros_like(dv_ref)

        strict, upper, before = _sb_consts(tq, tk)
        lane = _iota((tq, 128), 1)
        lane_k = _iota((tk, 128), 1)
        pairs = [slice(0, 128), slice(128, 256)]
        k_refs, v_refs = (k0_ref, k1_ref), (v0_ref, v1_ref)
        q2s, dob2s, qms, doms, dds = [], [], [], [], []
        for p, q_ref in enumerate((q0_ref, q1_ref)):
            q2 = q_ref[...]
            dob2 = do_ref[:, pairs[p]].astype(BF)
            doo = dob2.astype(F32) * o_ref[:, pairs[p]]
            q2s.append(q2)
            dob2s.append(dob2)
            for mine in (lane < 64, lane >= 64):
                qms.append(jnp.where(mine, q2, jnp.zeros_like(q2)))
                doms.append(jnp.where(mine, dob2, jnp.zeros_like(dob2)))
                dds.append(jnp.sum(jnp.where(mine, doo, 0.0), axis=-1, keepdims=True))

        def step(j, carry, d):
            rows = pl.ds(pl.multiple_of(j * tk, tk), tk)
            out, dks, dvs = [], [], []
            for h in range(4):
                kj, vj = k_refs[h // 2][rows, :], v_refs[h // 2][rows, :]
                rr, sg, dq = carry[h]
                ll, a, rs = _sb_tile(_dot_nt(qms[h], kj), rr, None if d is None else strict[d], d is not None, upper)
                ab = a.astype(BF)
                g = _dot_nt(doms[h], vj) * ab.astype(F32)
                gs = jnp.sum(g, axis=-1, keepdims=True)
                pre = (dds[h] - sg - gs) + _dot(g.astype(BF), before)
                dz = g - jnp.exp2(ll) * (g + pre)
                if d is not None:
                    dz = jnp.where(strict[d], dz, 0.0)
                dzb = dz.astype(BF)
                dks.append(_dot_tn(dzb, q2s[h // 2]))
                dvs.append(_dot_tn(ab, dob2s[h // 2]))
                out.append((rr + rs, sg + gs, dq + _dot(dzb, kj)))
            for p, ps in enumerate(pairs):
                dk_ref[rows, ps] += jnp.where(lane_k < 64, dks[2 * p], dks[2 * p + 1]) * SB_SCALE
                dv_ref[rows, ps] += jnp.where(lane_k < 64, dvs[2 * p], dvs[2 * p + 1])
            return tuple(out)

        zero = jnp.zeros((tq, 1), F32)
        carry = ((zero, zero, jnp.zeros((tq, 128), F32)),) * 4
        for d in reversed(range(nd)):
            carry = step(nd * i + d, carry, d)
        carry = _sb_walk(nd * i, lambda t, c: step(nd * i - 1 - t, c, None), carry)
        for p, ps in enumerate(pairs):
            dq_ref[:, ps] = jnp.where(lane < 64, carry[2 * p][2], carry[2 * p + 1][2]) * SB_SCALE

    blk = lambda c: pl.BlockSpec((tq, 128), lambda i: (i, c // 128))
    full = lambda c: pl.BlockSpec((T, 128), lambda i: (0, c // 128))
    row = pl.BlockSpec((tq, 256), lambda i: (i, 0))
    acc = pl.BlockSpec((T, 256), lambda i: (0, 0))
    return pl.pallas_call(
        body, name="sb_bwd", grid=(T // tq,),
        in_specs=[blk(C_DQ), blk(C_DQ + 128), full(C_DK), full(C_DK + 128), full(C_DV), full(C_DV + 128), row, row],
        out_specs=[row, acc, acc],
        out_shape=[jax.ShapeDtypeStruct((T, 256), F32)] * 3,
        compiler_params=_params(1))(h16, h16, h16, h16, h16, h16, yd, dyd)


def _compact_c(ycp):
    return jnp.concatenate([ycp[:, h * 128:h * 128 + 64] for h in range(4)], axis=1)


def _post_fwd(ya, yb, ycp, yd, h32, ggrp, wout, gpost, x, tgt=None):
    T = x.shape[0]
    tm = 256
    last = tgt is not None

    def body(*refs):
        ya_ref, yb_ref, yc_ref, yd_ref, gate_ref, gg_ref, w_ref, gp_ref, x_ref = refs[:9]
        if last:
            t_ref, xn_ref, o_ref, sq_ref = refs[9:]
        else:
            xn_ref, o_ref = refs[9:]
        ys = [ya_ref[...], yb_ref[...], _compact_c(yc_ref[...]), yd_ref[...]]
        gate = gate_ref[...]
        sil = gate * (1.0 / (1.0 + jnp.exp(-gate)))
        parts = []
        for gi in range(4):
            ng, _ = _rms_fwd(ys[gi], gg_ref[:, gi * 256:(gi + 1) * 256])
            parts.append(ng * sil[:, gi * 256:(gi + 1) * 256])
        o = _dot(jnp.concatenate(parts, axis=1).astype(BF), w_ref[...])
        o_ref[...] = o
        on, _ = _rms_fwd(o, gp_ref[...])
        if last:
            @pl.when(pl.program_id(0) == 0)
            def _():
                sq_ref[...] = jnp.zeros_like(sq_ref)

            d = (x_ref[...] + on) - t_ref[...]
            sq_ref[...] += jnp.sum(d * d, axis=0, keepdims=True)
            xn_ref[...] = d * (1.0 / D_MODEL)
        else:
            xn_ref[...] = x_ref[...] + on

    row = lambda w: pl.BlockSpec((tm, w), lambda n: (n, 0))
    vec = pl.BlockSpec((1, 1024), lambda n: (0, 0))
    return pl.pallas_call(
        body, name="post_fwd", grid=(T // tm,),
        in_specs=[row(256), row(256), row(512), row(256), pl.BlockSpec((tm, 1024), lambda n: (n, C_GATE // 1024)),
                  vec, pl.BlockSpec((1024, 1024), lambda n: (0, 0)), vec, row(1024)] + ([row(1024)] if last else []),
        out_specs=[row(1024), row(1024)] + ([vec] if last else []),
        out_shape=[jax.ShapeDtypeStruct((T, 1024), F32), jax.ShapeDtypeStruct((T, 1024), F32)]
        + ([jax.ShapeDtypeStruct((1, 1024), F32)] if last else []),
        compiler_params=_params(1))(*([ya, yb, ycp, yd, h32, ggrp, wout, gpost, x] + ([tgt] if last else [])))


def _post_bwd(dx, o, gpost, woutt, ya, yb, ycp, yd, h32, ggrp):
    T = dx.shape[0]
    tm = 256

    def body(dx_ref, o_ref, gp_ref, w_ref, ya_ref, yb_ref, yc_ref, yd_ref, gate_ref, gg_ref,
             dw_ref, dya_ref, dyb_ref, dyc_ref, dyd_ref, dgate_ref, dgp_ref, dgg_ref):
        n = pl.program_id(0)

        @pl.when(n == 0)
        def _():
            dw_ref[...] = jnp.zeros_like(dw_ref)
            dgp_ref[...] = jnp.zeros_like(dgp_ref)
            dgg_ref[...] = jnp.zeros_like(dgg_ref)

        ov, gp = o_ref[...], gp_ref[...]
        _, ro = _rms_fwd(ov, gp)
        do, dgt = _rms_bwd(ov, gp, ro, dx_ref[...])
        dgp_ref[...] += jnp.sum(dgt, axis=0, keepdims=True)
        dob = do.astype(BF)
        gate = gate_ref[...]
        sg = 1.0 / (1.0 + jnp.exp(-gate))
        sil = gate * sg
        dsil = sg * (1.0 + gate * (1.0 - sg))
        ys = [ya_ref[...], yb_ref[...], _compact_c(yc_ref[...]), yd_ref[...]]
        normed = [_rms_fwd(ys[gi], gg_ref[:, gi * 256:(gi + 1) * 256]) for gi in range(4)]
        ym = jnp.concatenate([normed[gi][0] * sil[:, gi * 256:(gi + 1) * 256] for gi in range(4)], axis=1).astype(BF)
        dw_ref[...] += _dot_tn(ym, dob)
        dym = _dot(dob, w_ref[...])
        dys = []
        for gi in range(4):
            gs = slice(gi * 256, (gi + 1) * 256)
            gg = gg_ref[:, gs]
            ng, rg = normed[gi]
            dgate_ref[:, gs] = (dym[:, gs] * ng * dsil[:, gs]).astype(BF)
            dy, dgt2 = _rms_bwd(ys[gi], gg, rg, dym[:, gs] * sil[:, gs])
            dgg_ref[:, gs] += jnp.sum(dgt2, axis=0, keepdims=True)
            dys.append(dy)
        dya_ref[...] = dys[0]
        dyb_ref[...] = dys[1]
        dyd_ref[...] = dys[3]
        z64 = jnp.zeros((tm, 64), F32)
        dyc_ref[...] = jnp.concatenate(
            [piece for h in range(4) for piece in (dys[2][:, h * 64:(h + 1) * 64], z64)], axis=1)

    row = lambda w: pl.BlockSpec((tm, w), lambda n: (n, 0))
    vec = pl.BlockSpec((1, 1024), lambda n: (0, 0))
    return pl.pallas_call(
        body, name="post_bwd", grid=(T // tm,),
        in_specs=[row(1024), row(1024), vec, pl.BlockSpec((1024, 1024), lambda n: (0, 0)),
                  row(256), row(256), row(512), row(256),
                  pl.BlockSpec((tm, 1024), lambda n: (n, C_GATE // 1024)), vec],
        out_specs=[pl.BlockSpec((1024, 1024), lambda n: (0, 0)), row(256), row(256), row(512), row(256), row(1024),
                   vec, vec],
        out_shape=[jax.ShapeDtypeStruct((1024, 1024), F32), jax.ShapeDtypeStruct((T, 256), F32),
                   jax.ShapeDtypeStruct((T, 256), F32), jax.ShapeDtypeStruct((T, 512), F32),
                   jax.ShapeDtypeStruct((T, 256), F32), jax.ShapeDtypeStruct((T, 1024), BF),
                   jax.ShapeDtypeStruct((1, 1024), F32), jax.ShapeDtypeStruct((1, 1024), F32)],
        compiler_params=_params(1))(dx, o, gpost, woutt, ya, yb, ycp, yd, h32, ggrp)


def _swap_rows32(a):
    return jnp.concatenate([a[16:32], a[0:16]], axis=0)


def _pad_w_uq(w):
    z = lambda n: jnp.zeros((w.shape[0], n), w.dtype)
    a = [p for h in range(4) for p in (w[:, 96 * h:96 * h + 96], z(32))]
    b = [p for h in range(4) for p in (z(64), _swap32(w[:, 96 * h + 64:96 * h + 96]), z(32))]
    return jnp.concatenate(a + b, axis=1)


def _unpad_w_uq(d):
    out = []
    for h in range(4):
        out.append(d[:, 128 * h:128 * h + 64])
        out.append(d[:, 128 * h + 64:128 * h + 96] + _swap32(d[:, 512 + 128 * h + 64:512 + 128 * h + 96]))
    return jnp.concatenate(out, axis=1)


def _pad_w_ukv(w):
    z = jnp.zeros((w.shape[0], 64), w.dtype)
    a = [p for h in range(4) for p in (w[:, 128 * h:128 * h + 64], z)]
    b = [p for h in range(4) for p in (w[:, 128 * h + 64:128 * h + 128], z)]
    return jnp.concatenate(a + b, axis=1)


def _unpad_w_ukv(d):
    return jnp.concatenate([p for h in range(4) for p in (d[:, 128 * h:128 * h + 64],
                                                          d[:, 512 + 128 * h:512 + 128 * h + 64])], axis=1)


def _rope_tables(pos):
    freqs = 10000.0 ** (-jnp.arange(16, dtype=F32) / 16)
    ang = pos.astype(F32)[:, None] * freqs
    c, s = jnp.cos(ang), jnp.sin(ang)
    z = lambda n: jnp.zeros((pos.shape[0], n), F32)
    return (jnp.concatenate([z(64), c, c, z(32)], axis=1), jnp.concatenate([z(64), -s, s, z(32)], axis=1))


def _layer_weights(W, l):
    wuq2 = _pad_w_uq(W["mla_w_uq"][l])
    wkv2 = _pad_w_ukv(W["mla_w_ukv"][l])
    wout = W["w_out"][l]
    cw = jnp.concatenate([W["conv_w"][l].astype(F32), jnp.zeros((5, 256), F32)], axis=0)
    return dict(
        wpt=W["wpt"][l], wuq2=wuq2.astype(BF), wuq2t=wuq2.T.astype(BF),
        wkv2=wkv2.astype(BF), wkv2t=wkv2.T.astype(BF), wout=wout.astype(BF), woutt=wout.T.astype(BF),
        cw=cw, cb=W["conv_b"][l][None, :], sinks=W["attn_sinks"][l],
        gpre=W["norm_pre"][l][None, :], gq=W["mla_q_norm"][l][None, :], gkv=W["mla_kv_norm"][l][None, :],
        ggrp=W["group_norm"][l][None, :], gpost=W["norm_post"][l][None, :])


def _local_step(x, pos, W, tgt):
    cosk, sin = _rope_tables(pos)
    saved = []
    for l in range(DEPTH):
        lw = _layer_weights(W, l)
        h32, h16, xn = _inproj_fwd(x, lw["gpre"], lw["wpt"])
        ya = _swa_fwd(h16, lw["sinks"])
        yb = _conv_fwd(h32, lw["cw"], lw["cb"])
        qc, kc, vc = _cprep_fwd(h32, lw["gq"], lw["gkv"], lw["wuq2"], lw["wkv2"], cosk, sin)
        ycp, lse = _mla_fwd(qc, kc, vc)
        yd = _sb_fwd(h16)
        if l < DEPTH - 1:
            x_new, o = _post_fwd(ya, yb, ycp, yd, h32, lw["ggrp"], lw["wout"], lw["gpost"], x)
        else:
            dx, o, sq = _post_fwd(ya, yb, ycp, yd, h32, lw["ggrp"], lw["wout"], lw["gpost"], x, tgt)
        saved.append(dict(lw=lw, x=x, h32=h32, h16=h16, xn=xn, ya=ya, yb=yb, qc=qc, kc=kc, vc=vc, ycp=ycp,
                          lse=lse, yd=yd, o=o))
        if l < DEPTH - 1:
            x = x_new

    grads = {k: [None] * DEPTH for k in ("norm_pre", "w_in_pt", "attn_sinks", "conv_w", "conv_b", "mla_q_norm",
                                         "mla_w_uq", "mla_kv_norm", "mla_w_ukv", "group_norm", "w_out",
                                         "norm_post")}
    for l in reversed(range(DEPTH)):
        s = saved[l]
        lw = s["lw"]
        dwout, dya, dyb, dycp, dyd, dgate, dgpost, dggrp = _post_bwd(
            dx, s["o"], lw["gpost"], lw["woutt"], s["ya"], s["yb"], s["ycp"], s["yd"], s["h32"], lw["ggrp"])
        grads["norm_post"][l] = dgpost[0]
        grads["group_norm"][l] = dggrp[0]
        grads["w_out"][l] = dwout
        sdq, sdk, sdv = _sb_bwd(s["h16"], s["yd"], dyd)
        mdq, mdk, mdv = _mla_bwd(s["qc"], s["kc"], s["vc"], s["ycp"], s["lse"], dycp)
        dcq, dckv, dckr, dckrs, dwuq2, dwkv2, dgq, dgkv = _cprep_bwd(
            s["h32"], lw["gq"], lw["gkv"], lw["wuq2t"], lw["wkv2t"], cosk, sin, mdq, mdk, mdv)
        grads["mla_q_norm"][l] = dgq[0]
        grads["mla_kv_norm"][l] = dgkv[0]
        grads["mla_w_uq"][l] = _unpad_w_uq(dwuq2)
        grads["mla_w_ukv"][l] = _unpad_w_ukv(dwkv2)
        dbb, dbc, dbx, dcw = _conv_bwd(s["h32"], lw["cw"], lw["cb"], dyb)
        grads["conv_w"][l] = dcw[0:3]
        grads["conv_b"][l] = dcw[3]
        adq, adk, adv, dsk = _swa_bwd(s["h16"], lw["sinks"], dya)
        grads["attn_sinks"][l] = dsk[0, 0:4]
        parts = [dgate, dbb, dbc, dbx, dcq, dckv, dckr, dckrs, adq, adk, adv, sdq, sdk, sdv]
        dwp, dx, dgpre = _inproj_bwd(parts, lw["wpt"], s["x"], s["xn"], lw["gpre"], dx)
        grads["w_in_pt"][l] = dwp.T
        grads["norm_pre"][l] = dgpre[0]
    return sq, dx, grads


SMALL_SHARDED = ("conv_w", "mla_w_uq", "mla_w_ukv")
REPLICATED = ("norm_pre", "attn_sinks", "conv_b", "mla_q_norm", "mla_kv_norm", "group_norm", "norm_post")
ORDER = ("norm_pre", "w_in", "attn_sinks", "conv_w", "conv_b", "mla_q_norm", "mla_w_uq", "mla_kv_norm",
         "mla_w_ukv", "group_norm", "w_out", "norm_post")
W_IN_COLS = 436
W_IN_WIN = 440
SMALL_ROWS = 48


def _pack_small(arrs, dtype):
    flat = jnp.concatenate([a.reshape(-1).astype(dtype) for a in arrs])
    flat = jnp.concatenate([flat, jnp.zeros((SMALL_ROWS * D_MODEL - flat.shape[0],), dtype)])
    return flat.reshape(SMALL_ROWS, D_MODEL)


TAIL_ROW0 = DEPTH * W_IN_WIN


def _pack_state(ps):
    k = len(ps)
    wout = jnp.stack([p["w_out"] for p in ps]).reshape(k, DEPTH * 128, D_MODEL)
    flat = jnp.stack([jnp.concatenate([p[n].reshape(-1) for n in SMALL_SHARDED + REPLICATED]) for p in ps])
    small = jnp.pad(flat, ((0, 0), (0, SMALL_ROWS * D_MODEL - flat.shape[1]))).reshape(k, SMALL_ROWS, D_MODEL)
    return jnp.concatenate([wout, small], axis=1)


def _unpack_state(buf, p):
    k = buf.shape[0]
    out = {"w_out": buf[:, 0:DEPTH * 128].reshape(k, DEPTH, 128, D_MODEL)}
    flat = buf[:, DEPTH * 128:].reshape(k, SMALL_ROWS * D_MODEL)
    off = 0
    for n in SMALL_SHARDED + REPLICATED:
        size = int(np.prod(p[n].shape))
        out[n] = flat[:, off:off + size].reshape((k,) + p[n].shape)
        off += size
    return out


def _rows_of_w_in_t(lo, hi, padded, kr):
    segs = ((0, 512, padded, NF + C_AQ), (512, 1664, padded, C_BB), (1664, 1696, kr, 0),
            (1696, 2464, padded, NF + C_DQ), (2464, 3488, padded, C_GATE))
    out = []
    for s0, s1, src, base in segs:
        a, b = max(lo, s0), min(hi, s1)
        if a < b:
            out.append(src[base + a - s0:base + b - s0])
    return out


def _me():
    return lax.axis_index("x"), lax.axis_index("y"), lax.axis_index("c")


def _all_gather(block):
    R, C = block.shape

    def body(src_ref, out_ref, send_sems, recv_sems, local_sem):
        x, y, c = _me()
        me, sibling = (x, y, c), (x, y, 1 - c)
        chips = [(1 - x, y), (x, 1 - y), (1 - x, 1 - y)]

        def slot(px, py, pc):
            return out_ref.at[4 * px + 2 * py + pc]

        def copy(k, block, to, src=None):
            return pltpu.make_async_remote_copy(
                src_ref=slot(*block) if src is None else src, dst_ref=slot(*block), send_sem=send_sems.at[k],
                recv_sem=recv_sems.at[k], device_id=to, device_id_type=MESH)

        mine = pltpu.make_async_copy(src_ref, slot(*me), local_sem)
        mine.start()
        first = [copy(0, me, sibling, src=src_ref)]
        first += [copy(1 + j, me, (*chip, c), src=src_ref) for j, chip in enumerate(chips)]
        for cp in first:
            cp.start()
        passed = [copy(4 + j, (*chip, c), sibling) for j, chip in enumerate(chips)]
        for j, chip in enumerate(chips):
            copy(1 + j, (*chip, c), me).wait_recv()
            passed[j].start()
        copy(0, sibling, me).wait_recv()
        for j, chip in enumerate(chips):
            copy(4 + j, (*chip, 1 - c), me).wait_recv()
        for cp in first + passed:
            cp.wait_send()
        mine.wait()

    return pl.pallas_call(
        body, name="all_gather", out_shape=jax.ShapeDtypeStruct((N_DEV, R, C), block.dtype),
        in_specs=[pl.BlockSpec(memory_space=pl.ANY)], out_specs=pl.BlockSpec(memory_space=pl.ANY),
        scratch_shapes=[pltpu.SemaphoreType.DMA((N_DEV - 1,)), pltpu.SemaphoreType.DMA((N_DEV - 1,)),
                        pltpu.SemaphoreType.DMA])(block)


N_CHIP = 4


def _sibling_swap(blocks):
    _, R, C = blocks.shape

    def body(src_ref, out_ref, send_sems, recv_sems):
        x, y, c = _me()
        copies = [pltpu.make_async_remote_copy(
            src_ref=src_ref.at[2 * j + 1 - c], dst_ref=out_ref.at[j], send_sem=send_sems.at[j],
            recv_sem=recv_sems.at[j], device_id=(x, y, 1 - c), device_id_type=MESH) for j in range(N_CHIP)]
        for cp in copies:
            cp.start()
        for cp in copies:
            cp.wait()

    return pl.pallas_call(
        body, name="sibling_swap", out_shape=jax.ShapeDtypeStruct((N_CHIP, R, C), blocks.dtype),
        in_specs=[pl.BlockSpec(memory_space=pl.ANY)], out_specs=pl.BlockSpec(memory_space=pl.ANY),
        scratch_shapes=[pltpu.SemaphoreType.DMA((N_CHIP,)), pltpu.SemaphoreType.DMA((N_CHIP,))])(blocks)


def _pair_sum(a, b):
    n, R, C = a.shape
    tr = 592 if R % 592 == 0 else R

    def body(a_ref, b_ref, o_ref):
        o_ref[...] = (a_ref[...].astype(F32) + b_ref[...].astype(F32)).astype(BF)

    spec = pl.BlockSpec((1, tr, C), lambda j, r: (j, r, 0))
    return pl.pallas_call(body, name="pair_sum", grid=(n, R // tr), in_specs=[spec, spec], out_specs=spec,
                          out_shape=jax.ShapeDtypeStruct(a.shape, BF), compiler_params=_params(2))(a, b)


def _chip_exchange(sums):
    _, R, C = sums.shape

    def body(src_ref, out_ref, send_sems, recv_sems, local_sem):
        x, y, c = _me()
        here = 2 * x + y
        mine = pltpu.make_async_copy(src_ref.at[here], out_ref.at[here], local_sem)
        mine.start()
        copies = []
        for k in range(1, N_CHIP):
            px, py = x ^ (k >> 1), y ^ (k & 1)
            copies.append(pltpu.make_async_remote_copy(
                src_ref=src_ref.at[2 * px + py], dst_ref=out_ref.at[here], send_sem=send_sems.at[k - 1],
                recv_sem=recv_sems.at[k - 1], device_id=(px, py, c), device_id_type=MESH))
        for cp in copies:
            cp.start()
        for cp in copies:
            cp.wait()
        mine.wait()

    return pl.pallas_call(
        body, name="chip_exchange", out_shape=jax.ShapeDtypeStruct((N_CHIP, R, C), sums.dtype),
        in_specs=[pl.BlockSpec(memory_space=pl.ANY)], out_specs=pl.BlockSpec(memory_space=pl.ANY),
        scratch_shapes=[pltpu.SemaphoreType.DMA((N_CHIP - 1,)), pltpu.SemaphoreType.DMA((N_CHIP - 1,)),
                        pltpu.SemaphoreType.DMA])(sums)


def _adamw_update(g, w, m, v):
    m_ = ADAM_B1 * m + (1.0 - ADAM_B1) * g
    v_ = ADAM_B2 * v + (1.0 - ADAM_B2) * (g * g)
    m_hat = m_ / (1.0 - ADAM_B1 ** ADAM_STEP)
    v_hat = v_ / (1.0 - ADAM_B2 ** ADAM_STEP)
    return -ADAM_LR * (m_hat / (jnp.sqrt(v_hat) + ADAM_EPS) + ADAM_WD * w), m_, v_


def _adamw(parts, state):
    _, R, C = state.shape
    n_parts = parts.shape[0]
    tr = 16
    assert R % tr == 0 and TAIL_ROW0 % tr == 0

    def body(p_ref, s_ref, o_ref):
        g = p_ref[0].astype(F32)
        for k in range(1, n_parts):
            g = g + p_ref[k].astype(F32)
        o_ref[0] = g
        o_ref[1], o_ref[2], o_ref[3] = _adamw_update(g, s_ref[0], s_ref[1], s_ref[2])

    return pl.pallas_call(
        body, name="adamw", grid=(R // tr,),
        in_specs=[pl.BlockSpec((n_parts, tr, C), lambda n: (0, n + TAIL_ROW0 // tr, 0)),
                  pl.BlockSpec((3, tr, C), lambda n: (0, n, 0))],
        out_specs=pl.BlockSpec((4, tr, C), lambda n: (0, n, 0)), out_shape=jax.ShapeDtypeStruct((4, R, C), F32),
        compiler_params=_params(1))(parts, state)


def _adamw_w_in(parts, w, m, v, core):
    n_parts = parts.shape[0]
    tc = 256

    def body(core_ref, p_ref, w_ref, m_ref, v_ref, o_ref):
        g_t = p_ref[0].astype(F32)
        for k in range(1, n_parts):
            g_t = g_t + p_ref[k].astype(F32)
        g_t = jnp.concatenate([g_t, jnp.zeros((512 - W_IN_WIN, tc), F32)], axis=0).T
        g = jnp.where(core_ref[0] == 0, g_t[:, 0:W_IN_COLS], g_t[:, W_IN_WIN - W_IN_COLS:W_IN_WIN])
        o_ref[0, 0] = g
        o_ref[1, 0], o_ref[2, 0], o_ref[3, 0] = _adamw_update(g, w_ref[0], m_ref[0], v_ref[0])

    nat = pl.BlockSpec((1, tc, W_IN_COLS), lambda l, j: (l, j, 0))
    return pl.pallas_call(
        body, name="adamw_w_in", grid=(DEPTH, D_MODEL // tc),
        in_specs=[pl.BlockSpec(memory_space=pltpu.SMEM),
                  pl.BlockSpec((n_parts, W_IN_WIN, tc), lambda l, j: (0, l, j)), nat, nat, nat],
        out_specs=pl.BlockSpec((4, 1, tc, W_IN_COLS), lambda l, j: (0, l, j, 0)),
        out_shape=jax.ShapeDtypeStruct((4, DEPTH, D_MODEL, W_IN_COLS), F32),
        compiler_params=_params(2))(core, parts, w, m, v)


def kernel(x, positions, norm_pre, w_in, attn_sinks, conv_w, conv_b, mla_q_norm, mla_w_uq, mla_kv_norm, mla_w_ukv, group_norm, w_out, norm_post, loss_target, m_norm_pre, m_w_in, m_attn_sinks, m_conv_w, m_conv_b, m_mla_q_norm, m_mla_w_uq, m_mla_kv_norm, m_mla_w_ukv, m_group_norm, m_w_out, m_norm_post, v_norm_pre, v_w_in, v_attn_sinks, v_conv_w, v_conv_b, v_mla_q_norm, v_mla_w_uq, v_mla_kv_norm, v_mla_w_ukv, v_group_norm, v_w_out, v_norm_post):
    local = dict(norm_pre=norm_pre, w_in=w_in, attn_sinks=attn_sinks, conv_w=conv_w, conv_b=conv_b,
                 mla_q_norm=mla_q_norm, mla_w_uq=mla_w_uq, mla_kv_norm=mla_kv_norm, mla_w_ukv=mla_w_ukv,
                 group_norm=group_norm, w_out=w_out, norm_post=norm_post)
    mom = dict(norm_pre=m_norm_pre, w_in=m_w_in, attn_sinks=m_attn_sinks, conv_w=m_conv_w, conv_b=m_conv_b,
               mla_q_norm=m_mla_q_norm, mla_w_uq=m_mla_w_uq, mla_kv_norm=m_mla_kv_norm, mla_w_ukv=m_mla_w_ukv,
               group_norm=m_group_norm, w_out=m_w_out, norm_post=m_norm_post)
    vel = dict(norm_pre=v_norm_pre, w_in=v_w_in, attn_sinks=v_attn_sinks, conv_w=v_conv_w, conv_b=v_conv_b,
               mla_q_norm=v_mla_q_norm, mla_w_uq=v_mla_w_uq, mla_kv_norm=v_mla_kv_norm, mla_w_ukv=v_mla_w_ukv,
               group_norm=v_group_norm, w_out=v_w_out, norm_post=v_norm_post)

    c = lax.axis_index("c")

    tile = 16
    slot_rows = 464
    shift = 8 * lax.axis_index("y") + 4 * c
    wt = lax.dynamic_update_slice(jnp.zeros((DEPTH, slot_rows, D_MODEL), BF),
                                  jnp.transpose(w_in, (0, 2, 1)).astype(BF), (0, shift, 0))
    payload = jnp.concatenate([wt.reshape(DEPTH * slot_rows, D_MODEL),
                               w_out.astype(BF).reshape(DEPTH * 128, D_MODEL),
                               _pack_small([local[n] for n in SMALL_SHARDED], BF)], axis=0)
    gathered = _all_gather(payload)
    W = {n: local[n] for n in REPLICATED}

    def nat_rows(l, lo, hi):
        def piece(d, r0, r1):
            base = slot_rows * l - (W_IN_COLS * d) // tile * tile
            return gathered[d, base + r0:base + r1]

        out, run = [], None
        for r0 in range(lo, hi, tile):
            d0, d1 = r0 // W_IN_COLS, (r0 + tile - 1) // W_IN_COLS
            if d0 == d1 and run is not None and run[0] == d0:
                run = (d0, run[1], r0 + tile)
                continue
            if run is not None:
                out.append(piece(*run))
                run = None
            if d0 == d1:
                run = (d0, r0, r0 + tile)
            else:
                out.append(piece(d0, r0, r0 + tile) + piece(d1, r0, r0 + tile))
        if run is not None:
            out.append(piece(*run))
        return out

    z = lambda n: [jnp.zeros((n, D_MODEL), BF)]
    W["wpt"] = [jnp.concatenate(nat_rows(l, 2464, 3488) + nat_rows(l, 512, 1664) + z(64) + nat_rows(l, 1664, 1696)
                                + z(96) + nat_rows(l, 1680, 1696) + nat_rows(l, 1664, 1680) + z(32)
                                + nat_rows(l, 0, 512) + nat_rows(l, 1696, 2464), axis=0) for l in range(DEPTH)]
    wo0 = DEPTH * slot_rows
    W["w_out"] = gathered[:, wo0:wo0 + DEPTH * 128].reshape(N_DEV, DEPTH, 128, D_MODEL).transpose(1, 0, 2, 3).reshape(
        DEPTH, D_MODEL, D_MODEL)
    flat = gathered[:, wo0 + DEPTH * 128:].reshape(N_DEV, SMALL_ROWS * D_MODEL)
    off = 0
    for n in SMALL_SHARDED:
        depth, rows, width = local[n].shape
        size = depth * rows * width
        W[n] = flat[:, off:off + size].reshape(N_DEV, depth, rows, width).transpose(1, 2, 0, 3).reshape(
            depth, rows, N_DEV * width)
        off += size

    sq, grad_x, g = _local_step(x[0], positions[0], W, loss_target[0])
    loss = lax.psum(0.5 / D_MODEL * jnp.sum(sq), ("x", "y", "c"))

    cols = []
    for n in SMALL_SHARDED:
        depth, rows, width = local[n].shape
        cols.append(jnp.stack(g[n]).reshape(depth, rows, N_DEV, width).transpose(2, 0, 1, 3).reshape(N_DEV, -1))
    rep = jnp.concatenate([a.reshape(-1) for n in REPLICATED for a in g[n]])
    cols.append(jnp.broadcast_to(rep[None], (N_DEV, rep.shape[0])))
    small = jnp.concatenate(cols, axis=1)
    small = jnp.pad(small, ((0, 0), (0, SMALL_ROWS * D_MODEL - small.shape[1]))).reshape(N_DEV, SMALL_ROWS, D_MODEL)
    krs = [p[C_CKR + 64:C_CKR + 96] + _swap_rows32(p[C_CKRS + 64:C_CKRS + 96]) for p in g["w_in_pt"]]
    pieces = []
    for d in range(N_DEV):
        lo = W_IN_COLS * d // 8 * 8
        for l in range(DEPTH):
            pieces += _rows_of_w_in_t(lo, lo + W_IN_WIN, g["w_in_pt"][l], krs[l])
        pieces += [g["w_out"][l][128 * d:128 * (d + 1)] for l in range(DEPTH)]
        pieces.append(small[d])
    blocks = jnp.concatenate(pieces, axis=0).astype(BF).reshape(N_DEV, -1, D_MODEL)
    mine = lax.dynamic_index_in_dim(blocks.reshape(N_CHIP, 2, -1, D_MODEL), c, axis=1, keepdims=False)
    received = _chip_exchange(_pair_sum(mine, _sibling_swap(blocks)))

    out = _unpack_state(_adamw(received, _pack_state([local, mom, vel])), local)
    out["w_in"] = _adamw_w_in(received, w_in, m_w_in, v_w_in, c.astype(jnp.int32).reshape(1))
    return (loss, grad_x[None], *[out[n][t] for t in range(4) for n in ORDER])
```

```python
import functools

import jax
import jax.numpy as jnp
import numpy as np
from jax import lax
from jax.experimental import pallas as pl
from jax.experimental.pallas import tpu as pltpu

F32 = jnp.float32
BF = jnp.bfloat16
MESH = pl.DeviceIdType.MESH

D_MODEL = 1024
DEPTH = 2
EPS = 1e-6
N_DEV = 8
VMEM_LIMIT = 56 * 1024 * 1024
NEG = -1e30
MLA_SCALE = 96.0 ** -0.5
SB_SCALE = 0.125
LOG2E = 1.4426950408889634

NP = 3712
NF = 2432
NB = 1280
C_GATE = 0
C_BB = 1024
C_BC = 1280
C_BX = 1536
C_CQ = 1792
C_CKV = 2048
C_CKR = 2176
C_CKRS = 2304
C_AQ = 0
C_AK = 256
C_AV = 384
C_DQ = 512
C_DK = 768
C_DV = 1024
C_END = 3712

def _swap32(a):
    return jnp.concatenate([a[:, 16:32], a[:, 0:16]], axis=1)

ADAM_LR, ADAM_B1, ADAM_B2, ADAM_EPS, ADAM_WD, ADAM_STEP = 0.001, 0.9, 0.999, 1e-08, 0.01, 10


def _dot(a, b):
    return jnp.dot(a, b, preferred_element_type=F32)


def _dot_nt(a, b):
    return lax.dot_general(a, b, (((1,), (1,)), ((), ())), preferred_element_type=F32)


def _dot_tn(a, b):
    return lax.dot_general(a, b, (((0,), (0,)), ((), ())), preferred_element_type=F32)


def _params(n_grid):
    return pltpu.CompilerParams(dimension_semantics=("arbitrary",) * n_grid, vmem_limit_bytes=VMEM_LIMIT)


def _rms_fwd(x, g):
    r = lax.rsqrt(jnp.mean(x * x, axis=-1, keepdims=True) + EPS)
    return (x * r) * g, r


def _rms_bwd(x, g, r, dy, width=None):
    n = x.shape[-1] if width is None else width
    u = dy * g
    dx = r * u - x * (r * r * r) * (jnp.sum(x * u, axis=-1, keepdims=True) / n)
    return dx, dy * (x * r)


def _iota(shape, axis):
    return lax.broadcasted_iota(jnp.int32, shape, axis)


def _inproj_fwd(x, g, wpt):
    T = x.shape[0]
    tm = 256

    def body(x_ref, g_ref, w_ref, h32_ref, h16_ref, xn_ref):
        xn, _ = _rms_fwd(x_ref[...], g_ref[...])
        xn = xn.astype(BF)
        xn_ref[...] = xn
        h = _dot_nt(xn, w_ref[...])
        h32_ref[...] = h[:, :NF]
        h16_ref[...] = h[:, NF:].astype(BF)

    return pl.pallas_call(
        body, name="inproj_fwd", grid=(T // tm,),
        in_specs=[pl.BlockSpec((tm, D_MODEL), lambda n: (n, 0)),
                  pl.BlockSpec((1, D_MODEL), lambda n: (0, 0)),
                  pl.BlockSpec((NP, D_MODEL), lambda n: (0, 0))],
        out_specs=[pl.BlockSpec((tm, NF), lambda n: (n, 0)),
                   pl.BlockSpec((tm, NB), lambda n: (n, 0)),
                   pl.BlockSpec((tm, D_MODEL), lambda n: (n, 0))],
        out_shape=[jax.ShapeDtypeStruct((T, NF), F32), jax.ShapeDtypeStruct((T, NB), BF),
                   jax.ShapeDtypeStruct((T, D_MODEL), BF)],
        compiler_params=_params(1))(x, g, wpt)


def _inproj_bwd(parts, wpt, x, xn, g, dxo):
    T = x.shape[0]
    tm = 256
    np_ = len(parts)
    chunks = (0, 1280, 2560, NP)
    assert sum(p.shape[1] for p in parts) == C_END == NP

    def body(*refs):
        part_refs = refs[:np_]
        w_ref, x_ref, xn_ref, g_ref, dxo_ref, dw_ref, dx_ref, dg_ref = refs[np_:]
        n = pl.program_id(0)

        @pl.when(n == 0)
        def _():
            dw_ref[...] = jnp.zeros_like(dw_ref)
            dg_ref[...] = jnp.zeros_like(dg_ref)

        dh = jnp.concatenate([r[...].astype(BF) for r in part_refs], axis=1)
        xnv = xn_ref[...]
        for lo, hi in zip(chunks[:-1], chunks[1:]):
            dw_ref[:, lo:hi] += _dot_tn(xnv, dh[:, lo:hi])
        dxn = _dot(dh, w_ref[...])
        xv = x_ref[...]
        _, r = _rms_fwd(xv, g_ref[...])
        dx, dgt = _rms_bwd(xv, g_ref[...], r, dxn)
        dx_ref[...] = dxo_ref[...] + dx
        dg_ref[...] += jnp.sum(dgt, axis=0, keepdims=True)

    once = pl.Buffered(1)
    return pl.pallas_call(
        body, name="inproj_bwd", grid=(T // tm,),
        in_specs=[pl.BlockSpec((tm, p.shape[1]), lambda n: (n, 0)) for p in parts]
        + [pl.BlockSpec((NP, D_MODEL), lambda n: (0, 0), pipeline_mode=once),
           pl.BlockSpec((tm, D_MODEL), lambda n: (n, 0)),
           pl.BlockSpec((tm, D_MODEL), lambda n: (n, 0)),
           pl.BlockSpec((1, D_MODEL), lambda n: (0, 0)),
           pl.BlockSpec((tm, D_MODEL), lambda n: (n, 0))],
        out_specs=[pl.BlockSpec((D_MODEL, NP), lambda n: (0, 0), pipeline_mode=once),
                   pl.BlockSpec((tm, D_MODEL), lambda n: (n, 0)),
                   pl.BlockSpec((1, D_MODEL), lambda n: (0, 0))],
        out_shape=[jax.ShapeDtypeStruct((D_MODEL, NP), F32), jax.ShapeDtypeStruct((T, D_MODEL), F32),
                   jax.ShapeDtypeStruct((1, D_MODEL), F32)],
        compiler_params=_params(1))(*parts, wpt, x, xn, g, dxo)


SWA_BLK = 128
SWA_TQ = 2048


def _bdot_nt(a, b):
    return lax.dot_general(a, b, (((2,), (2,)), ((0,), (0,))), preferred_element_type=F32)


def _bdot(a, b):
    return lax.dot_general(a, b, (((2,), (1,)), ((0,), (0,))), preferred_element_type=F32)


def _bdot_tn(a, b):
    return lax.dot_general(a, b, (((1,), (1,)), ((0,), (0,))), preferred_element_type=F32)


def _swa_probs(q, kc, kp, sink, mask_c, mask_p):
    sc = jnp.where(mask_c, _bdot_nt(q, kc) * SB_SCALE, NEG)
    sp = jnp.where(mask_p, _bdot_nt(q, kp) * SB_SCALE, NEG)
    m = jnp.maximum(jnp.maximum(jnp.max(sc, axis=-1, keepdims=True), jnp.max(sp, axis=-1, keepdims=True)), sink)
    pc = jnp.exp(sc - m)
    pp = jnp.exp(sp - m)
    ps = jnp.exp(sink - m)
    inv = 1.0 / (jnp.sum(pc, axis=-1, keepdims=True) + jnp.sum(pp, axis=-1, keepdims=True) + ps)
    return pc * inv, pp * inv, ps * inv


def _swa_masks(n, nb):
    blk = _iota((nb, SWA_BLK, SWA_BLK), 0)
    row = _iota((nb, SWA_BLK, SWA_BLK), 1)
    col = _iota((nb, SWA_BLK, SWA_BLK), 2)
    return col <= row, jnp.logical_and(col > row, jnp.logical_or(blk > 0, n > 0))


def _swa_specs(tq):
    halo = tq // SWA_BLK
    return [pl.BlockSpec(memory_space=pltpu.SMEM),
            pl.BlockSpec((tq, 256), lambda n: (n, C_AQ // 256)),
            pl.BlockSpec((tq, 128), lambda n: (n, C_AK // 128)),
            pl.BlockSpec((SWA_BLK, 128), lambda n: (jnp.maximum(n * halo - 1, 0), C_AK // 128)),
            pl.BlockSpec((tq, 128), lambda n: (n, C_AV // 128)),
            pl.BlockSpec((SWA_BLK, 128), lambda n: (jnp.maximum(n * halo - 1, 0), C_AV // 128))]


def _swa_blocked(cur_ref, prev_ref, gs, nb):
    cur = cur_ref[:, gs].reshape(nb, SWA_BLK, 64)
    prev = jnp.concatenate([prev_ref[:, gs].reshape(1, SWA_BLK, 64), cur[:nb - 1]], axis=0) if nb > 1 \
        else prev_ref[:, gs].reshape(1, SWA_BLK, 64)
    return cur, prev


def _swa_masks2(n, nb):
    mask_c, mask_p = _swa_masks(n, nb)
    return jnp.concatenate([mask_c, mask_c], axis=1), jnp.concatenate([mask_p, mask_p], axis=1)


def _swa_stacked(ref, g, nb):
    a = ref[:, 128 * g:128 * g + 64].reshape(nb, SWA_BLK, 64)
    b = ref[:, 128 * g + 64:128 * g + 128].reshape(nb, SWA_BLK, 64)
    return jnp.concatenate([a, b], axis=1)


def _swa_fwd(h16, sinks):
    T = h16.shape[0]
    tq = SWA_TQ if T % SWA_TQ == 0 else SWA_BLK
    nb = tq // SWA_BLK

    def body(s_ref, q_ref, kc_ref, kp_ref, vc_ref, vp_ref, o_ref):
        n = pl.program_id(0)
        mask_c, mask_p = _swa_masks2(n, nb)
        first = _iota((1, 2 * SWA_BLK, 1), 1) < SWA_BLK
        for g in range(2):
            gs = slice(g * 64, (g + 1) * 64)
            kc, kp = _swa_blocked(kc_ref, kp_ref, gs, nb)
            vc, vp = _swa_blocked(vc_ref, vp_ref, gs, nb)
            q = _swa_stacked(q_ref, g, nb)
            sink = jnp.where(first, s_ref[2 * g], s_ref[2 * g + 1])
            pc, pp, _ = _swa_probs(q, kc, kp, sink, mask_c, mask_p)
            o = _bdot(pc.astype(BF), vc) + _bdot(pp.astype(BF), vp)
            o_ref[:, 128 * g:128 * g + 64] = o[:, :SWA_BLK].reshape(tq, 64)
            o_ref[:, 128 * g + 64:128 * g + 128] = o[:, SWA_BLK:].reshape(tq, 64)

    return pl.pallas_call(
        body, name="swa_fwd", grid=(T // tq,), in_specs=_swa_specs(tq),
        out_specs=pl.BlockSpec((tq, 256), lambda n: (n, 0)),
        out_shape=jax.ShapeDtypeStruct((T, 256), F32),
        compiler_params=_params(1))(sinks, h16, h16, h16, h16, h16)


def _swa_bwd(h16, sinks, dya):
    T = h16.shape[0]
    tq = SWA_TQ if T % SWA_TQ == 0 else SWA_BLK
    nb = tq // SWA_BLK

    def body(s_ref, q_ref, kc_ref, kp_ref, vc_ref, vp_ref, do_ref, dq_ref, dk_ref, dv_ref, ds_ref):
        n = pl.program_id(0)

        @pl.when(n == 0)
        def _():
            dk_ref[...] = jnp.zeros_like(dk_ref)
            dv_ref[...] = jnp.zeros_like(dv_ref)
            ds_ref[...] = jnp.zeros_like(ds_ref)

        mask_c, mask_p = _swa_masks2(n, nb)
        first = _iota((1, 2 * SWA_BLK, 1), 1) < SWA_BLK
        rows = pl.ds(pl.multiple_of(n * tq, tq), tq)
        before = pl.ds(pl.multiple_of(jnp.maximum(n * nb - 1, 0) * SWA_BLK, SWA_BLK), SWA_BLK)
        lane = _iota((8, 128), 1)
        row8 = _iota((8, 128), 0)

        def to_keys(own, prev):
            if nb == 1:
                return own
            return own + jnp.concatenate([prev[1:], jnp.zeros((1, SWA_BLK, 64), F32)], axis=0)

        for g in range(2):
            gs = slice(g * 64, (g + 1) * 64)
            q = _swa_stacked(q_ref, g, nb)
            kc, kp = _swa_blocked(kc_ref, kp_ref, gs, nb)
            vc, vp = _swa_blocked(vc_ref, vp_ref, gs, nb)
            sink = jnp.where(first, s_ref[2 * g], s_ref[2 * g + 1])
            pc, pp, ps = _swa_probs(q, kc, kp, sink, mask_c, mask_p)
            pcb, ppb = pc.astype(BF), pp.astype(BF)
            do = _swa_stacked(do_ref, g, nb)
            dob = do.astype(BF)
            o = _bdot(pcb, vc) + _bdot(ppb, vp)
            dd = jnp.sum(do * o, axis=-1, keepdims=True)
            dsc = (pc * (_bdot_nt(dob, vc) - dd) * SB_SCALE).astype(BF)
            dsp = (pp * (_bdot_nt(dob, vp) - dd) * SB_SCALE).astype(BF)
            dq = (_bdot(dsc, kc) + _bdot(dsp, kp)).astype(BF)
            dq_ref[:, 128 * g:128 * g + 64] = dq[:, :SWA_BLK].reshape(tq, 64)
            dq_ref[:, 128 * g + 64:128 * g + 128] = dq[:, SWA_BLK:].reshape(tq, 64)
            dkp, dvp = _bdot_tn(dsp, q), _bdot_tn(ppb, dob)
            dk_ref[rows, gs] += to_keys(_bdot_tn(dsc, q), dkp).reshape(tq, 64)
            dv_ref[rows, gs] += to_keys(_bdot_tn(pcb, dob), dvp).reshape(tq, 64)
            dk_ref[before, gs] += dkp[0]
            dv_ref[before, gs] += dvp[0]
            psd = ps * dd
            for hh in range(2):
                part = psd[:, hh * SWA_BLK:(hh + 1) * SWA_BLK]
                ds_ref[...] += jnp.where(jnp.logical_and(lane == 2 * g + hh, row8 == 0), -jnp.sum(part), 0.0)

    return pl.pallas_call(
        body, name="swa_bwd", grid=(T // tq,),
        in_specs=_swa_specs(tq) + [pl.BlockSpec((tq, 256), lambda n: (n, 0))],
        out_specs=[pl.BlockSpec((tq, 256), lambda n: (n, 0)),
                   pl.BlockSpec((T, 128), lambda n: (0, 0)),
                   pl.BlockSpec((T, 128), lambda n: (0, 0)),
                   pl.BlockSpec((8, 128), lambda n: (0, 0))],
        out_shape=[jax.ShapeDtypeStruct((T, 256), BF), jax.ShapeDtypeStruct((T, 128), F32),
                   jax.ShapeDtypeStruct((T, 128), F32), jax.ShapeDtypeStruct((8, 128), F32)],
        compiler_params=_params(1))(sinks, h16, h16, h16, h16, h16, dya)


def _conv_u(bc_ref, bx_ref, bch_ref, bxh_ref, n, tm):
    u = bc_ref[...] * bx_ref[...]
    uh = bch_ref[...] * bxh_ref[...] * (n > 0).astype(F32)
    rowi = _iota((tm, 256), 0)
    u1 = jnp.where(rowi == 0, uh[7:8, :], pltpu.roll(u, 1, axis=0))
    u2 = jnp.where(rowi == 0, uh[6:7, :], jnp.where(rowi == 1, uh[7:8, :], pltpu.roll(u, 2, axis=0)))
    return u, u1, u2


def _conv_fwd(h32, cw, cb):
    T = h32.shape[0]
    tm = 512 if T % 512 == 0 else T
    hb = tm // 8

    def body(bb_ref, bc_ref, bx_ref, bch_ref, bxh_ref, w_ref, b_ref, o_ref):
        n = pl.program_id(0)
        u, u1, u2 = _conv_u(bc_ref, bx_ref, bch_ref, bxh_ref, n, tm)
        y = w_ref[0:1, :] * u2 + w_ref[1:2, :] * u1 + w_ref[2:3, :] * u + b_ref[...]
        o_ref[...] = bb_ref[...] * y

    halo = lambda c: pl.BlockSpec((8, 256), lambda n: (jnp.maximum(n * hb - 1, 0), c // 256))
    return pl.pallas_call(
        body, name="conv_fwd", grid=(T // tm,),
        in_specs=[pl.BlockSpec((tm, 256), lambda n: (n, C_BB // 256)),
                  pl.BlockSpec((tm, 256), lambda n: (n, C_BC // 256)),
                  pl.BlockSpec((tm, 256), lambda n: (n, C_BX // 256)),
                  halo(C_BC), halo(C_BX),
                  pl.BlockSpec((8, 256), lambda n: (0, 0)),
                  pl.BlockSpec((1, 256), lambda n: (0, 0))],
        out_specs=pl.BlockSpec((tm, 256), lambda n: (n, 0)),
        out_shape=jax.ShapeDtypeStruct((T, 256), F32),
        compiler_params=_params(1))(h32, h32, h32, h32, h32, cw, cb)


def _conv_bwd(h32, cw, cb, dyb):
    T = h32.shape[0]
    tm = 512 if T % 512 == 0 else T
    hb = tm // 8
    nt = T // tm

    def body(bb_ref, bc_ref, bx_ref, bch_ref, bxh_ref, bbn_ref, dy_ref, dyn_ref, w_ref, b_ref,
             dbb_ref, dbc_ref, dbx_ref, dw_ref):
        n = pl.program_id(0)

        @pl.when(n == 0)
        def _():
            dw_ref[...] = jnp.zeros_like(dw_ref)

        u, u1, u2 = _conv_u(bc_ref, bx_ref, bch_ref, bxh_ref, n, tm)
        w0, w1, w2 = w_ref[0:1, :], w_ref[1:2, :], w_ref[2:3, :]
        y = w0 * u2 + w1 * u1 + w2 * u + b_ref[...]
        dyb_ = dy_ref[...]
        dbb_ref[...] = (dyb_ * y).astype(BF)
        dy = dyb_ * bb_ref[...]
        dyn = dyn_ref[...] * bbn_ref[...] * (n < nt - 1).astype(F32)
        rowi = _iota((tm, 256), 0)
        dy1 = jnp.where(rowi == tm - 1, dyn[0:1, :], pltpu.roll(dy, tm - 1, axis=0))
        dy2 = jnp.where(rowi == tm - 2, dyn[0:1, :],
                        jnp.where(rowi == tm - 1, dyn[1:2, :], pltpu.roll(dy, tm - 2, axis=0)))
        du = w2 * dy + w1 * dy1 + w0 * dy2
        dbc_ref[...] = (du * bx_ref[...]).astype(BF)
        dbx_ref[...] = (du * bc_ref[...]).astype(BF)
        dw_ref[0:1, :] += jnp.sum(dy * u2, axis=0, keepdims=True)
        dw_ref[1:2, :] += jnp.sum(dy * u1, axis=0, keepdims=True)
        dw_ref[2:3, :] += jnp.sum(dy * u, axis=0, keepdims=True)
        dw_ref[3:4, :] += jnp.sum(dy, axis=0, keepdims=True)

    halo = lambda c: pl.BlockSpec((8, 256), lambda n: (jnp.maximum(n * hb - 1, 0), c // 256))
    nxt = lambda c: pl.BlockSpec((8, 256), lambda n: (jnp.minimum((n + 1) * hb, T // 8 - 1), c // 256))
    cur = lambda c: pl.BlockSpec((tm, 256), lambda n: (n, c // 256))
    return pl.pallas_call(
        body, name="conv_bwd", grid=(nt,),
        in_specs=[cur(C_BB), cur(C_BC), cur(C_BX), halo(C_BC), halo(C_BX), nxt(C_BB),
                  cur(0), nxt(0),
                  pl.BlockSpec((8, 256), lambda n: (0, 0)),
                  pl.BlockSpec((1, 256), lambda n: (0, 0))],
        out_specs=[cur(0), cur(0), cur(0), pl.BlockSpec((8, 256), lambda n: (0, 0))],
        out_shape=[jax.ShapeDtypeStruct((T, 256), BF)] * 3 + [jax.ShapeDtypeStruct((8, 256), F32)],
        compiler_params=_params(1))(h32, h32, h32, h32, h32, h32, dyb, dyb, cw, cb)


def _cprep_specs(tm):
    return [pl.BlockSpec((tm, 256), lambda n: (n, C_CQ // 256)),
            pl.BlockSpec((tm, 128), lambda n: (n, C_CKV // 128)),
            pl.BlockSpec((tm, 128), lambda n: (n, C_CKR // 128)),
            pl.BlockSpec((tm, 128), lambda n: (n, C_CKRS // 128)),
            pl.BlockSpec((1, 256), lambda n: (0, 0)),
            pl.BlockSpec((1, 128), lambda n: (0, 0)),
            pl.BlockSpec((tm, 128), lambda n: (n, 0)),
            pl.BlockSpec((tm, 128), lambda n: (n, 0))]


def _cprep_fwd(h32, gq, gkv, wuq2, wkv2, cosk, sin):
    T = h32.shape[0]
    tm = 512 if T % 512 == 0 else T

    def body(cq_ref, ckv_ref, ckr_ref, ckrs_ref, gq_ref, gkv_ref, cos_ref, sin_ref, wuq_ref, wkv_ref,
             q_ref, k_ref, v_ref):
        cosk_, sin_ = cos_ref[...], sin_ref[...]
        cosq = cosk_ + (_iota((tm, 128), 1) < 64).astype(F32)
        cqn, _ = _rms_fwd(cq_ref[...], gq_ref[...])
        q2 = _dot(cqn.astype(BF), wuq_ref[...])
        ckvn, _ = _rms_fwd(ckv_ref[...], gkv_ref[...])
        kv2 = _dot(ckvn.astype(BF), wkv_ref[...])
        kr = ckr_ref[...] * cosk_ + ckrs_ref[...] * sin_
        for h in range(4):
            hs = slice(h * 128, (h + 1) * 128)
            q_ref[:, hs] = ((q2[:, hs] * cosq + q2[:, 512 + h * 128:512 + (h + 1) * 128] * sin_) * MLA_SCALE).astype(BF)
            k_ref[:, hs] = (kv2[:, hs] + kr).astype(BF)
        ones = _iota((tm, 512), 1) % 128 == 64
        v_ref[...] = jnp.where(ones, 1.0, kv2[:, 512:]).astype(BF)

    return pl.pallas_call(
        body, name="cprep_fwd", grid=(T // tm,),
        in_specs=_cprep_specs(tm) + [pl.BlockSpec((256, 1024), lambda n: (0, 0)),
                                     pl.BlockSpec((128, 1024), lambda n: (0, 0))],
        out_specs=[pl.BlockSpec((tm, 512), lambda n: (n, 0))] * 3,
        out_shape=[jax.ShapeDtypeStruct((T, 512), BF)] * 3,
        compiler_params=_params(1))(h32, h32, h32, h32, gq, gkv, cosk, sin, wuq2, wkv2)


def _cprep_bwd(h32, gq, gkv, wuq2t, wkv2t, cosk, sin, dq, dk, dv):
    T = h32.shape[0]
    tm = 512 if T % 512 == 0 else T

    def body(cq_ref, ckv_ref, ckr_ref, ckrs_ref, gq_ref, gkv_ref, cos_ref, sin_ref, wuq_ref, wkv_ref,
             dq_ref, dk_ref, dv_ref,
             dcq_ref, dckv_ref, dckr_ref, dckrs_ref, dwuq_ref, dwkv_ref, dgq_ref, dgkv_ref):
        n = pl.program_id(0)

        @pl.when(n == 0)
        def _():
            dwuq_ref[...] = jnp.zeros_like(dwuq_ref)
            dwkv_ref[...] = jnp.zeros_like(dwkv_ref)
            dgq_ref[...] = jnp.zeros_like(dgq_ref)
            dgkv_ref[...] = jnp.zeros_like(dgkv_ref)

        cosk_, sin_ = cos_ref[...], sin_ref[...]
        cosq = cosk_ + (_iota((tm, 128), 1) < 64).astype(F32)
        dkr = jnp.zeros((tm, 128), F32)
        plain, swapped = [], []
        for h in range(4):
            hs = slice(h * 128, (h + 1) * 128)
            dqh = dq_ref[:, hs] * MLA_SCALE
            plain.append((dqh * cosq).astype(BF))
            swapped.append((dqh * sin_).astype(BF))
            dkr = dkr + dk_ref[:, hs]
        dq2 = jnp.concatenate(plain + swapped, axis=1)
        dkv2 = jnp.concatenate([dk_ref[...].astype(BF), dv_ref[...].astype(BF)], axis=1)
        dckr_ref[...] = (dkr * cosk_).astype(BF)
        dckrs_ref[...] = (dkr * sin_).astype(BF)

        cq, gq_ = cq_ref[...], gq_ref[...]
        cqn, rq = _rms_fwd(cq, gq_)
        dwuq_ref[...] += _dot_tn(cqn.astype(BF), dq2)
        dcq, dgt = _rms_bwd(cq, gq_, rq, _dot(dq2, wuq_ref[...]))
        dcq_ref[...] = dcq.astype(BF)
        dgq_ref[...] += jnp.sum(dgt, axis=0, keepdims=True)

        ckv, gkv_ = ckv_ref[...], gkv_ref[...]
        ckvn, rkv = _rms_fwd(ckv, gkv_)
        dwkv_ref[...] += _dot_tn(ckvn.astype(BF), dkv2)
        dckv, dgt2 = _rms_bwd(ckv, gkv_, rkv, _dot(dkv2, wkv_ref[...]))
        dckv_ref[...] = dckv.astype(BF)
        dgkv_ref[...] += jnp.sum(dgt2, axis=0, keepdims=True)

    row = lambda w: pl.BlockSpec((tm, w), lambda n: (n, 0))
    return pl.pallas_call(
        body, name="cprep_bwd", grid=(T // tm,),
        in_specs=_cprep_specs(tm) + [pl.BlockSpec((1024, 256), lambda n: (0, 0)),
                                     pl.BlockSpec((1024, 128), lambda n: (0, 0)),
                                     row(512), row(512), row(512)],
        out_specs=[row(256), row(128), row(128), row(128),
                   pl.BlockSpec((256, 1024), lambda n: (0, 0)), pl.BlockSpec((128, 1024), lambda n: (0, 0)),
                   pl.BlockSpec((1, 256), lambda n: (0, 0)), pl.BlockSpec((1, 128), lambda n: (0, 0))],
        out_shape=[jax.ShapeDtypeStruct((T, 256), BF), jax.ShapeDtypeStruct((T, 128), BF),
                   jax.ShapeDtypeStruct((T, 128), BF), jax.ShapeDtypeStruct((T, 128), BF),
                   jax.ShapeDtypeStruct((256, 1024), F32), jax.ShapeDtypeStruct((128, 1024), F32),
                   jax.ShapeDtypeStruct((1, 256), F32), jax.ShapeDtypeStruct((1, 128), F32)],
        compiler_params=_params(1))(h32, h32, h32, h32, gq, gkv, cosk, sin, wuq2t, wkv2t, dq, dk, dv)


MLA_TILE = 512
MLA_HEADS_PER_STEP = 4


def _causal_mask(t):
    return _iota((t, t), 1) <= _iota((t, t), 0)


def _mla_fwd(q, k, v):
    T = q.shape[0]
    tq = MLA_TILE

    def body(q_ref, k_ref, v_ref, o_ref, lse_ref):
        i = pl.program_id(1)
        mask = _causal_mask(tq)
        heads = [slice(128 * h, 128 * h + 128) for h in range(MLA_HEADS_PER_STEP)]
        qs = [q_ref[:, hs] for hs in heads]

        def step(j, carry, masked):
            rows = pl.ds(pl.multiple_of(j * tq, tq), tq)
            out = []
            for hh, hs in enumerate(heads):
                m, acc = carry[hh]
                s = _dot_nt(qs[hh], k_ref[rows, hs])
                if masked:
                    s = jnp.where(mask, s, NEG)
                m_new = jnp.maximum(m, jnp.max(s, axis=-1, keepdims=True))
                p = jnp.exp((s - m_new).astype(BF))
                acc = jnp.exp(m - m_new) * acc + _dot(p, v_ref[rows, hs])
                out.append((m_new, acc))
            return tuple(out)

        init = ((jnp.full((tq, 1), NEG, F32), jnp.zeros((tq, 128), F32)),) * len(heads)
        carry = lax.fori_loop(0, i // 2, lambda t, c: step(2 * t + 1, step(2 * t, c, False), False), init)
        carry = lax.cond(i % 2 == 1, lambda c: step(i - 1, c, False), lambda c: c, carry)
        carry = step(i, carry, True)
        for hh, hs in enumerate(heads):
            m, acc = carry[hh]
            l = acc[:, 64:65]
            o_ref[:, hs] = acc * (1.0 / l)
            lse_ref[:, hs] = jnp.broadcast_to(m + jnp.log(l), (tq, 128))

    width = 128 * MLA_HEADS_PER_STEP
    blk = pl.BlockSpec((tq, width), lambda h, i: (i, h))
    full = pl.BlockSpec((T, width), lambda h, i: (0, h))
    return pl.pallas_call(
        body, name="mla_fwd", grid=(4 // MLA_HEADS_PER_STEP, T // tq), in_specs=[blk, full, full],
        out_specs=[blk, blk],
        out_shape=[jax.ShapeDtypeStruct((T, 512), F32), jax.ShapeDtypeStruct((T, 512), F32)],
        compiler_params=_params(2))(q, k, v)


def _mla_bwd(q, k, v, o, lse, do):
    T = q.shape[0]
    tq = MLA_TILE

    def body(q_ref, k_ref, v_ref, o_ref, lse_ref, do_ref, dq_ref, dk_ref, dv_ref):
        i = pl.program_id(1)

        @pl.when(i == 0)
        def _():
            dk_ref[...] = jnp.zeros_like(dk_ref)
            dv_ref[...] = jnp.zeros_like(dv_ref)

        heads = [slice(0, 128), slice(128, 256)]
        mask = _causal_mask(tq)
        qs, dobs, dds, lses = [], [], [], []
        for hs in heads:
            do = do_ref[:, hs]
            qs.append(q_ref[:, hs])
            dobs.append(do.astype(BF))
            dds.append(jnp.sum(do * o_ref[:, hs], axis=-1, keepdims=True))
            lses.append(lse_ref[:, hs.start:hs.start + 1])

        def step(j, dqs, masked):
            rows = pl.ds(pl.multiple_of(j * tq, tq), tq)
            out = []
            for hh, hs in enumerate(heads):
                kj, vj = k_ref[rows, hs], v_ref[rows, hs]
                s = _dot_nt(qs[hh], kj)
                if masked:
                    s = jnp.where(mask, s, NEG)
                p = jnp.exp(s - lses[hh])
                ds = (p * (_dot_nt(dobs[hh], vj) - dds[hh])).astype(BF)
                dk_ref[rows, hs] += _dot_tn(ds, qs[hh])
                dv_ref[rows, hs] += _dot_tn(p.astype(BF), dobs[hh])
                out.append(dqs[hh] + _dot(ds, kj))
            return tuple(out)

        dqs = lax.fori_loop(0, i // 2, lambda t, c: step(2 * t + 1, step(2 * t, c, False), False),
                            (jnp.zeros((tq, 128), F32),) * 2)
        dqs = lax.cond(i % 2 == 1, lambda c: step(i - 1, c, False), lambda c: c, dqs)
        dqs = step(i, dqs, True)
        for hh, hs in enumerate(heads):
            dq_ref[:, hs] = dqs[hh]

    blk = pl.BlockSpec((tq, 256), lambda h, i: (i, h))
    full = pl.BlockSpec((T, 256), lambda h, i: (0, h), pipeline_mode=pl.Buffered(1))
    return pl.pallas_call(
        body, name="mla_bwd", grid=(2, T // tq), in_specs=[blk, full, full, blk, blk, blk],
        out_specs=[blk, full, full],
        out_shape=[jax.ShapeDtypeStruct((T, 512), F32)] * 3,
        compiler_params=_params(2))(q, k, v, o, lse, do)


def _sb_tile(qk, rr, strict, masked, upper):
    z2 = qk * (SB_SCALE * LOG2E)
    l1 = jnp.log2(1.0 + jnp.exp2(-jnp.abs(z2)))
    lk = -jnp.maximum(z2, 0.0) - l1
    if masked:
        lk = jnp.where(strict, lk, 0.0)
    after = rr + _dot(lk.astype(BF), upper)
    ll = jnp.minimum(z2, 0.0) - l1
    a = jnp.exp2(ll + after)
    if masked:
        a = jnp.where(strict, a, 0.0)
    return ll, a, jnp.sum(lk, axis=-1, keepdims=True)


SB_TQ, SB_TK = 256, 256
SB_DEAD = -160.0


def _sb_walk(trips, one_step, carry):
    def alive(c):
        t, cr = c
        top = functools.reduce(jnp.maximum, [jnp.max(h[0]) for h in cr])
        return jnp.logical_and(t < trips, top > SB_DEAD)

    def body(c):
        t, cr = c
        return t + 1, one_step(t, cr)

    return lax.while_loop(alive, body, (jnp.int32(0), carry))[1]


def _sb_consts(tq, tk):
    row, col = _iota((tq, tk), 0), _iota((tq, tk), 1)
    strict = [col + d * tk < row for d in range(tq // tk)]
    r2, c2 = _iota((tk, tk), 0), _iota((tk, tk), 1)
    return strict, (r2 > c2).astype(BF), (r2 < c2).astype(BF)


def _sb_fwd(h16):
    T = h16.shape[0]
    tq, tk = SB_TQ, SB_TK
    nd = tq // tk

    def body(q0_ref, q1_ref, k0_ref, k1_ref, v0_ref, v1_ref, o_ref):
        i = pl.program_id(0)
        strict, upper, _ = _sb_consts(tq, tk)
        lane = _iota((tq, 128), 1)
        pairs = [slice(0, 128), slice(128, 256)]
        k_refs, v_refs = (k0_ref, k1_ref), (v0_ref, v1_ref)
        qms = []
        for q_ref in (q0_ref, q1_ref):
            q2 = q_ref[...]
            qms += [jnp.where(lane < 64, q2, jnp.zeros_like(q2)), jnp.where(lane >= 64, q2, jnp.zeros_like(q2))]

        def step(j, carry, d):
            rows = pl.ds(pl.multiple_of(j * tk, tk), tk)
            out = []
            for h in range(4):
                rr, acc = carry[h]
                _, a, rs = _sb_tile(_dot_nt(qms[h], k_refs[h // 2][rows, :]), rr, None if d is None else strict[d],
                                    d is not None, upper)
                out.append((rr + rs, acc + _dot(a.astype(BF), v_refs[h // 2][rows, :])))
            return tuple(out)

        carry = ((jnp.zeros((tq, 1), F32), jnp.zeros((tq, 128), F32)),) * 4
        for d in reversed(range(nd)):
            carry = step(nd * i + d, carry, d)
        carry = _sb_walk(nd * i, lambda t, c: step(nd * i - 1 - t, c, None), carry)
        for p, ps in enumerate(pairs):
            o_ref[:, ps] = jnp.where(lane < 64, carry[2 * p][1], carry[2 * p + 1][1])

    return pl.pallas_call(
        body, name="sb_fwd", grid=(T // tq,),
        in_specs=[pl.BlockSpec((tq, 128), lambda i: (i, C_DQ // 128)),
                  pl.BlockSpec((tq, 128), lambda i: (i, C_DQ // 128 + 1)),
                  pl.BlockSpec((T, 128), lambda i: (0, C_DK // 128)),
                  pl.BlockSpec((T, 128), lambda i: (0, C_DK // 128 + 1)),
                  pl.BlockSpec((T, 128), lambda i: (0, C_DV // 128)),
                  pl.BlockSpec((T, 128), lambda i: (0, C_DV // 128 + 1))],
        out_specs=pl.BlockSpec((tq, 256), lambda i: (i, 0)),
        out_shape=jax.ShapeDtypeStruct((T, 256), F32),
        compiler_params=_params(1))(h16, h16, h16, h16, h16, h16)


def _sb_bwd(h16, yd, dyd):
    T = h16.shape[0]
    tq, tk = SB_TQ, SB_TK
    nd = tq // tk

    def body(q0_ref, q1_ref, k0_ref, k1_ref, v0_ref, v1_ref, o_ref, do_ref, dq_ref, dk_ref, dv_ref):
        i = pl.program_id(0)

        @pl.when(i == 0)
        def _():
            dk_ref[...] = jnp.zeros_like(dk_ref)
            dv_ref[...] = jnp.zeros_like(dv_ref)

        strict, upper, before = _sb_consts(tq, tk)
        lane = _iota((tq, 128), 1)
        lane_k = _iota((tk, 128), 1)
        pairs = [slice(0, 128), slice(128, 256)]
        k_refs, v_refs = (k0_ref, k1_ref), (v0_ref, v1_ref)
        q2s, dob2s, qms, doms, dds = [], [], [], [], []
        for p, q_ref in enumerate((q0_ref, q1_ref)):
            q2 = q_ref[...]
            dob2 = do_ref[:, pairs[p]].astype(BF)
            doo = dob2.astype(F32) * o_ref[:, pairs[p]]
            q2s.append(q2)
            dob2s.append(dob2)
            for mine in (lane < 64, lane >= 64):
                qms.append(jnp.where(mine, q2, jnp.zeros_like(q2)))
                doms.append(jnp.where(mine, dob2, jnp.zeros_like(dob2)))
                dds.append(jnp.sum(jnp.where(mine, doo, 0.0), axis=-1, keepdims=True))

        def step(j, carry, d):
            rows = pl.ds(pl.multiple_of(j * tk, tk), tk)
            out, dks, dvs = [], [], []
            for h in range(4):
                kj, vj = k_refs[h // 2][rows, :], v_refs[h // 2][rows, :]
                rr, sg, dq = carry[h]
                ll, a, rs = _sb_tile(_dot_nt(qms[h], kj), rr, None if d is None else strict[d], d is not None, upper)
                ab = a.astype(BF)
                g = _dot_nt(doms[h], vj) * ab.astype(F32)
                gs = jnp.sum(g, axis=-1, keepdims=True)
                pre = (dds[h] - sg - gs) + _dot(g.astype(BF), before)
                dz = g - jnp.exp2(ll) * (g + pre)
                if d is not None:
                    dz = jnp.where(strict[d], dz, 0.0)
                dzb = dz.astype(BF)
                dks.append(_dot_tn(dzb, q2s[h // 2]))
                dvs.append(_dot_tn(ab, dob2s[h // 2]))
                out.append((rr + rs, sg + gs, dq + _dot(dzb, kj)))
            for p, ps in enumerate(pairs):
                dk_ref[rows, ps] += jnp.where(lane_k < 64, dks[2 * p], dks[2 * p + 1]) * SB_SCALE
                dv_ref[rows, ps] += jnp.where(lane_k < 64, dvs[2 * p], dvs[2 * p + 1])
            return tuple(out)

        zero = jnp.zeros((tq, 1), F32)
        carry = ((zero, zero, jnp.zeros((tq, 128), F32)),) * 4
        for d in reversed(range(nd)):
            carry = step(nd * i + d, carry, d)
        carry = _sb_walk(nd * i, lambda t, c: step(nd * i - 1 - t, c, None), carry)
        for p, ps in enumerate(pairs):
            dq_ref[:, ps] = jnp.where(lane < 64, carry[2 * p][2], carry[2 * p + 1][2]) * SB_SCALE

    blk = lambda c: pl.BlockSpec((tq, 128), lambda i: (i, c // 128))
    full = lambda c: pl.BlockSpec((T, 128), lambda i: (0, c // 128))
    row = pl.BlockSpec((tq, 256), lambda i: (i, 0))
    acc = pl.BlockSpec((T, 256), lambda i: (0, 0))
    return pl.pallas_call(
        body, name="sb_bwd", grid=(T // tq,),
        in_specs=[blk(C_DQ), blk(C_DQ + 128), full(C_DK), full(C_DK + 128), full(C_DV), full(C_DV + 128), row, row],
        out_specs=[row, acc, acc],
        out_shape=[jax.ShapeDtypeStruct((T, 256), F32)] * 3,
        compiler_params=_params(1))(h16, h16, h16, h16, h16, h16, yd, dyd)


def _compact_c(ycp):
    return jnp.concatenate([ycp[:, h * 128:h * 128 + 64] for h in range(4)], axis=1)


def _post_fwd(ya, yb, ycp, yd, h32, ggrp, wout, gpost, x, tgt=None):
    T = x.shape[0]
    tm = 512 if T % 512 == 0 else 256
    last = tgt is not None

    def body(*refs):
        ya_ref, yb_ref, yc_ref, yd_ref, gate_ref, gg_ref, w_ref, gp_ref, x_ref = refs[:9]
        if last:
            t_ref, xn_ref, o_ref, sq_ref = refs[9:]
        else:
            xn_ref, o_ref = refs[9:]
        ys = [ya_ref[...], yb_ref[...], _compact_c(yc_ref[...]), yd_ref[...]]
        gate = gate_ref[...]
        sil = gate * (1.0 / (1.0 + jnp.exp(-gate)))
        parts = []
        for gi in range(4):
            ng, _ = _rms_fwd(ys[gi], gg_ref[:, gi * 256:(gi + 1) * 256])
            parts.append(ng * sil[:, gi * 256:(gi + 1) * 256])
        o = _dot(jnp.concatenate(parts, axis=1).astype(BF), w_ref[...])
        o_ref[...] = o
        on, _ = _rms_fwd(o, gp_ref[...])
        if last:
            @pl.when(pl.program_id(0) == 0)
            def _():
                sq_ref[...] = jnp.zeros_like(sq_ref)

            d = (x_ref[...] + on) - t_ref[...]
            sq_ref[...] += jnp.sum(d * d, axis=0, keepdims=True)
            xn_ref[...] = d * (1.0 / D_MODEL)
        else:
            xn_ref[...] = x_ref[...] + on

    row = lambda w: pl.BlockSpec((tm, w), lambda n: (n, 0))
    vec = pl.BlockSpec((1, 1024), lambda n: (0, 0))
    return pl.pallas_call(
        body, name="post_fwd", grid=(T // tm,),
        in_specs=[row(256), row(256), row(512), row(256), pl.BlockSpec((tm, 1024), lambda n: (n, C_GATE // 1024)),
                  vec, pl.BlockSpec((1024, 1024), lambda n: (0, 0)), vec, row(1024)] + ([row(1024)] if last else []),
        out_specs=[row(1024), row(1024)] + ([vec] if last else []),
        out_shape=[jax.ShapeDtypeStruct((T, 1024), F32), jax.ShapeDtypeStruct((T, 1024), F32)]
        + ([jax.ShapeDtypeStruct((1, 1024), F32)] if last else []),
        compiler_params=_params(1))(*([ya, yb, ycp, yd, h32, ggrp, wout, gpost, x] + ([tgt] if last else [])))


def _post_bwd(dx, o, gpost, woutt, ya, yb, ycp, yd, h32, ggrp):
    T = dx.shape[0]
    tm = 256

    def body(dx_ref, o_ref, gp_ref, w_ref, ya_ref, yb_ref, yc_ref, yd_ref, gate_ref, gg_ref,
             dw_ref, dya_ref, dyb_ref, dyc_ref, dyd_ref, dgate_ref, dgp_ref, dgg_ref):
        n = pl.program_id(0)

        @pl.when(n == 0)
        def _():
            dw_ref[...] = jnp.zeros_like(dw_ref)
            dgp_ref[...] = jnp.zeros_like(dgp_ref)
            dgg_ref[...] = jnp.zeros_like(dgg_ref)

        ov, gp = o_ref[...], gp_ref[...]
        _, ro = _rms_fwd(ov, gp)
        do, dgt = _rms_bwd(ov, gp, ro, dx_ref[...])
        dgp_ref[...] += jnp.sum(dgt, axis=0, keepdims=True)
        dob = do.astype(BF)
        gate = gate_ref[...]
        sg = 1.0 / (1.0 + jnp.exp(-gate))
        sil = gate * sg
        dsil = sg * (1.0 + gate * (1.0 - sg))
        ys = [ya_ref[...], yb_ref[...], _compact_c(yc_ref[...]), yd_ref[...]]
        normed = [_rms_fwd(ys[gi], gg_ref[:, gi * 256:(gi + 1) * 256]) for gi in range(4)]
        ym = jnp.concatenate([normed[gi][0] * sil[:, gi * 256:(gi + 1) * 256] for gi in range(4)], axis=1).astype(BF)
        dw_ref[...] += _dot_tn(ym, dob)
        dym = _dot(dob, w_ref[...])
        dys = []
        for gi in range(4):
            gs = slice(gi * 256, (gi + 1) * 256)
            gg = gg_ref[:, gs]
            ng, rg = normed[gi]
            dgate_ref[:, gs] = (dym[:, gs] * ng * dsil[:, gs]).astype(BF)
            dy, dgt2 = _rms_bwd(ys[gi], gg, rg, dym[:, gs] * sil[:, gs])
            dgg_ref[:, gs] += jnp.sum(dgt2, axis=0, keepdims=True)
            dys.append(dy)
        dya_ref[...] = dys[0]
        dyb_ref[...] = dys[1]
        dyd_ref[...] = dys[3]
        z64 = jnp.zeros((tm, 64), F32)
        dyc_ref[...] = jnp.concatenate(
            [piece for h in range(4) for piece in (dys[2][:, h * 64:(h + 1) * 64], z64)], axis=1)

    row = lambda w: pl.BlockSpec((tm, w), lambda n: (n, 0))
    vec = pl.BlockSpec((1, 1024), lambda n: (0, 0))
    return pl.pallas_call(
        body, name="post_bwd", grid=(T // tm,),
        in_specs=[row(1024), row(1024), vec, pl.BlockSpec((1024, 1024), lambda n: (0, 0)),
                  row(256), row(256), row(512), row(256),
                  pl.BlockSpec((tm, 1024), lambda n: (n, C_GATE // 1024)), vec],
        out_specs=[pl.BlockSpec((1024, 1024), lambda n: (0, 0)), row(256), row(256), row(512), row(256), row(1024),
                   vec, vec],
        out_shape=[jax.ShapeDtypeStruct((1024, 1024), F32), jax.ShapeDtypeStruct((T, 256), F32),
                   jax.ShapeDtypeStruct((T, 256), F32), jax.ShapeDtypeStruct((T, 512), F32),
                   jax.ShapeDtypeStruct((T, 256), F32), jax.ShapeDtypeStruct((T, 1024), BF),
                   jax.ShapeDtypeStruct((1, 1024), F32), jax.ShapeDtypeStruct((1, 1024), F32)],
        compiler_params=_params(1))(dx, o, gpost, woutt, ya, yb, ycp, yd, h32, ggrp)


def _swap_rows32(a):
    return jnp.concatenate([a[16:32], a[0:16]], axis=0)


def _pad_w_uq(w):
    z = lambda n: jnp.zeros((w.shape[0], n), w.dtype)
    a = [p for h in range(4) for p in (w[:, 96 * h:96 * h + 96], z(32))]
    b = [p for h in range(4) for p in (z(64), _swap32(w[:, 96 * h + 64:96 * h + 96]), z(32))]
    return jnp.concatenate(a + b, axis=1)


def _unpad_w_uq(d):
    out = []
    for h in range(4):
        out.append(d[:, 128 * h:128 * h + 64])
        out.append(d[:, 128 * h + 64:128 * h + 96] + _swap32(d[:, 512 + 128 * h + 64:512 + 128 * h + 96]))
    return jnp.concatenate(out, axis=1)


def _pad_w_ukv(w):
    z = jnp.zeros((w.shape[0], 64), w.dtype)
    a = [p for h in range(4) for p in (w[:, 128 * h:128 * h + 64], z)]
    b = [p for h in range(4) for p in (w[:, 128 * h + 64:128 * h + 128], z)]
    return jnp.concatenate(a + b, axis=1)


def _unpad_w_ukv(d):
    return jnp.concatenate([p for h in range(4) for p in (d[:, 128 * h:128 * h + 64],
                                                          d[:, 512 + 128 * h:512 + 128 * h + 64])], axis=1)


def _rope_tables(pos):
    freqs = 10000.0 ** (-jnp.arange(16, dtype=F32) / 16)
    ang = pos.astype(F32)[:, None] * freqs
    c, s = jnp.cos(ang), jnp.sin(ang)
    z = lambda n: jnp.zeros((pos.shape[0], n), F32)
    return (jnp.concatenate([z(64), c, c, z(32)], axis=1), jnp.concatenate([z(64), -s, s, z(32)], axis=1))


def _layer_weights(W, l):
    wuq2 = _pad_w_uq(W["mla_w_uq"][l])
    wkv2 = _pad_w_ukv(W["mla_w_ukv"][l])
    wout = W["w_out"][l]
    cw = jnp.concatenate([W["conv_w"][l].astype(F32), jnp.zeros((5, 256), F32)], axis=0)
    return dict(
        wpt=W["wpt"][l], wuq2=wuq2.astype(BF), wuq2t=wuq2.T.astype(BF),
        wkv2=wkv2.astype(BF), wkv2t=wkv2.T.astype(BF), wout=wout.astype(BF), woutt=wout.T.astype(BF),
        cw=cw, cb=W["conv_b"][l][None, :], sinks=W["attn_sinks"][l],
        gpre=W["norm_pre"][l][None, :], gq=W["mla_q_norm"][l][None, :], gkv=W["mla_kv_norm"][l][None, :],
        ggrp=W["group_norm"][l][None, :], gpost=W["norm_post"][l][None, :])


def _local_step(x, pos, W, tgt):
    cosk, sin = _rope_tables(pos)
    saved = []
    for l in range(DEPTH):
        lw = _layer_weights(W, l)
        h32, h16, xn = _inproj_fwd(x, lw["gpre"], lw["wpt"])
        ya = _swa_fwd(h16, lw["sinks"])
        yb = _conv_fwd(h32, lw["cw"], lw["cb"])
        qc, kc, vc = _cprep_fwd(h32, lw["gq"], lw["gkv"], lw["wuq2"], lw["wkv2"], cosk, sin)
        ycp, lse = _mla_fwd(qc, kc, vc)
        yd = _sb_fwd(h16)
        if l < DEPTH - 1:
            x_new, o = _post_fwd(ya, yb, ycp, yd, h32, lw["ggrp"], lw["wout"], lw["gpost"], x)
        else:
            dx, o, sq = _post_fwd(ya, yb, ycp, yd, h32, lw["ggrp"], lw["wout"], lw["gpost"], x, tgt)
        saved.append(dict(lw=lw, x=x, h32=h32, h16=h16, xn=xn, ya=ya, yb=yb, qc=qc, kc=kc, vc=vc, ycp=ycp,
                          lse=lse, yd=yd, o=o))
        if l < DEPTH - 1:
            x = x_new

    grads = {k: [None] * DEPTH for k in ("norm_pre", "w_in_pt", "attn_sinks", "conv_w", "conv_b", "mla_q_norm",
                                         "mla_w_uq", "mla_kv_norm", "mla_w_ukv", "group_norm", "w_out",
                                         "norm_post")}
    for l in reversed(range(DEPTH)):
        s = saved[l]
        lw = s["lw"]
        dwout, dya, dyb, dycp, dyd, dgate, dgpost, dggrp = _post_bwd(
            dx, s["o"], lw["gpost"], lw["woutt"], s["ya"], s["yb"], s["ycp"], s["yd"], s["h32"], lw["ggrp"])
        grads["norm_post"][l] = dgpost[0]
        grads["group_norm"][l] = dggrp[0]
        grads["w_out"][l] = dwout
        sdq, sdk, sdv = _sb_bwd(s["h16"], s["yd"], dyd)
        mdq, mdk, mdv = _mla_bwd(s["qc"], s["kc"], s["vc"], s["ycp"], s["lse"], dycp)
        dcq, dckv, dckr, dckrs, dwuq2, dwkv2, dgq, dgkv = _cprep_bwd(
            s["h32"], lw["gq"], lw["gkv"], lw["wuq2t"], lw["wkv2t"], cosk, sin, mdq, mdk, mdv)
        grads["mla_q_norm"][l] = dgq[0]
        grads["mla_kv_norm"][l] = dgkv[0]
        grads["mla_w_uq"][l] = _unpad_w_uq(dwuq2)
        grads["mla_w_ukv"][l] = _unpad_w_ukv(dwkv2)
        dbb, dbc, dbx, dcw = _conv_bwd(s["h32"], lw["cw"], lw["cb"], dyb)
        grads["conv_w"][l] = dcw[0:3]
        grads["conv_b"][l] = dcw[3]
        adq, adk, adv, dsk = _swa_bwd(s["h16"], lw["sinks"], dya)
        grads["attn_sinks"][l] = dsk[0, 0:4]
        parts = [dgate, dbb, dbc, dbx, dcq, dckv, dckr, dckrs, adq, adk, adv, sdq, sdk, sdv]
        dwp, dx, dgpre = _inproj_bwd(parts, lw["wpt"], s["x"], s["xn"], lw["gpre"], dx)
        grads["w_in_pt"][l] = dwp.T
        grads["norm_pre"][l] = dgpre[0]
    return sq, dx, grads


SMALL_SHARDED = ("conv_w", "mla_w_uq", "mla_w_ukv")
REPLICATED = ("norm_pre", "attn_sinks", "conv_b", "mla_q_norm", "mla_kv_norm", "group_norm", "norm_post")
ORDER = ("norm_pre", "w_in", "attn_sinks", "conv_w", "conv_b", "mla_q_norm", "mla_w_uq", "mla_kv_norm",
         "mla_w_ukv", "group_norm", "w_out", "norm_post")
W_IN_COLS = 436
W_IN_WIN = 440
SMALL_ROWS = 48


def _pack_small(arrs, dtype):
    flat = jnp.concatenate([a.reshape(-1).astype(dtype) for a in arrs])
    flat = jnp.concatenate([flat, jnp.zeros((SMALL_ROWS * D_MODEL - flat.shape[0],), dtype)])
    return flat.reshape(SMALL_ROWS, D_MODEL)


TAIL_ROW0 = DEPTH * W_IN_WIN


def _pack_state(ps):
    k = len(ps)
    wout = jnp.stack([p["w_out"] for p in ps]).reshape(k, DEPTH * 128, D_MODEL)
    flat = jnp.stack([jnp.concatenate([p[n].reshape(-1) for n in SMALL_SHARDED + REPLICATED]) for p in ps])
    small = jnp.pad(flat, ((0, 0), (0, SMALL_ROWS * D_MODEL - flat.shape[1]))).reshape(k, SMALL_ROWS, D_MODEL)
    return jnp.concatenate([wout, small], axis=1)


def _unpack_state(buf, p):
    k = buf.shape[0]
    out = {"w_out": buf[:, 0:DEPTH * 128].reshape(k, DEPTH, 128, D_MODEL)}
    flat = buf[:, DEPTH * 128:].reshape(k, SMALL_ROWS * D_MODEL)
    off = 0
    for n in SMALL_SHARDED + REPLICATED:
        size = int(np.prod(p[n].shape))
        out[n] = flat[:, off:off + size].reshape((k,) + p[n].shape)
        off += size
    return out


def _rows_of_w_in_t(lo, hi, padded, kr):
    segs = ((0, 512, padded, NF + C_AQ), (512, 1664, padded, C_BB), (1664, 1696, kr, 0),
            (1696, 2464, padded, NF + C_DQ), (2464, 3488, padded, C_GATE))
    out = []
    for s0, s1, src, base in segs:
        a, b = max(lo, s0), min(hi, s1)
        if a < b:
            out.append(src[base + a - s0:base + b - s0])
    return out


def _me():
    return lax.axis_index("x"), lax.axis_index("y"), lax.axis_index("c")


def _all_gather(block):
    R, C = block.shape

    def body(src_ref, out_ref, send_sems, recv_sems, local_sem):
        x, y, c = _me()
        me, sibling = (x, y, c), (x, y, 1 - c)
        chips = [(1 - x, y), (x, 1 - y), (1 - x, 1 - y)]

        def slot(px, py, pc):
            return out_ref.at[4 * px + 2 * py + pc]

        def copy(k, block, to, src=None):
            return pltpu.make_async_remote_copy(
                src_ref=slot(*block) if src is None else src, dst_ref=slot(*block), send_sem=send_sems.at[k],
                recv_sem=recv_sems.at[k], device_id=to, device_id_type=MESH)

        mine = pltpu.make_async_copy(src_ref, slot(*me), local_sem)
        mine.start()
        first = [copy(0, me, sibling, src=src_ref)]
        first += [copy(1 + j, me, (*chip, c), src=src_ref) for j, chip in enumerate(chips)]
        for cp in first:
            cp.start()
        passed = [copy(4 + j, (*chip, c), sibling) for j, chip in enumerate(chips)]
        for j, chip in enumerate(chips):
            copy(1 + j, (*chip, c), me).wait_recv()
            passed[j].start()
        copy(0, sibling, me).wait_recv()
        for j, chip in enumerate(chips):
            copy(4 + j, (*chip, 1 - c), me).wait_recv()
        for cp in first + passed:
            cp.wait_send()
        mine.wait()

    return pl.pallas_call(
        body, name="all_gather", out_shape=jax.ShapeDtypeStruct((N_DEV, R, C), block.dtype),
        in_specs=[pl.BlockSpec(memory_space=pl.ANY)], out_specs=pl.BlockSpec(memory_space=pl.ANY),
        scratch_shapes=[pltpu.SemaphoreType.DMA((N_DEV - 1,)), pltpu.SemaphoreType.DMA((N_DEV - 1,)),
                        pltpu.SemaphoreType.DMA])(block)


N_CHIP = 4


def _sibling_swap(blocks):
    _, R, C = blocks.shape

    def body(src_ref, out_ref, send_sems, recv_sems):
        x, y, c = _me()
        copies = [pltpu.make_async_remote_copy(
            src_ref=src_ref.at[2 * j + 1 - c], dst_ref=out_ref.at[j], send_sem=send_sems.at[j],
            recv_sem=recv_sems.at[j], device_id=(x, y, 1 - c), device_id_type=MESH) for j in range(N_CHIP)]
        for cp in copies:
            cp.start()
        for cp in copies:
            cp.wait()

    return pl.pallas_call(
        body, name="sibling_swap", out_shape=jax.ShapeDtypeStruct((N_CHIP, R, C), blocks.dtype),
        in_specs=[pl.BlockSpec(memory_space=pl.ANY)], out_specs=pl.BlockSpec(memory_space=pl.ANY),
        scratch_shapes=[pltpu.SemaphoreType.DMA((N_CHIP,)), pltpu.SemaphoreType.DMA((N_CHIP,))])(blocks)


def _pair_sum(a, b):
    n, R, C = a.shape
    tr = 592 if R % 592 == 0 else R

    def body(a_ref, b_ref, o_ref):
        o_ref[...] = (a_ref[...].astype(F32) + b_ref[...].astype(F32)).astype(BF)

    spec = pl.BlockSpec((1, tr, C), lambda j, r: (j, r, 0))
    return pl.pallas_call(body, name="pair_sum", grid=(n, R // tr), in_specs=[spec, spec], out_specs=spec,
                          out_shape=jax.ShapeDtypeStruct(a.shape, BF), compiler_params=_params(2))(a, b)


def _chip_exchange(sums):
    _, R, C = sums.shape

    def body(src_ref, out_ref, send_sems, recv_sems, local_sem):
        x, y, c = _me()
        here = 2 * x + y
        mine = pltpu.make_async_copy(src_ref.at[here], out_ref.at[here], local_sem)
        mine.start()
        copies = []
        for k in range(1, N_CHIP):
            px, py = x ^ (k >> 1), y ^ (k & 1)
            copies.append(pltpu.make_async_remote_copy(
                src_ref=src_ref.at[2 * px + py], dst_ref=out_ref.at[here], send_sem=send_sems.at[k - 1],
                recv_sem=recv_sems.at[k - 1], device_id=(px, py, c), device_id_type=MESH))
        for cp in copies:
            cp.start()
        for cp in copies:
            cp.wait()
        mine.wait()

    return pl.pallas_call(
        body, name="chip_exchange", out_shape=jax.ShapeDtypeStruct((N_CHIP, R, C), sums.dtype),
        in_specs=[pl.BlockSpec(memory_space=pl.ANY)], out_specs=pl.BlockSpec(memory_space=pl.ANY),
        scratch_shapes=[pltpu.SemaphoreType.DMA((N_CHIP - 1,)), pltpu.SemaphoreType.DMA((N_CHIP - 1,)),
                        pltpu.SemaphoreType.DMA])(sums)


def _adamw_update(g, w, m, v):
    m_ = ADAM_B1 * m + (1.0 - ADAM_B1) * g
    v_ = ADAM_B2 * v + (1.0 - ADAM_B2) * (g * g)
    m_hat = m_ / (1.0 - ADAM_B1 ** ADAM_STEP)
    v_hat = v_ / (1.0 - ADAM_B2 ** ADAM_STEP)
    return -ADAM_LR * (m_hat / (jnp.sqrt(v_hat) + ADAM_EPS) + ADAM_WD * w), m_, v_


def _adamw(parts, state):
    _, R, C = state.shape
    n_parts = parts.shape[0]
    tr = 16
    assert R % tr == 0 and TAIL_ROW0 % tr == 0

    def body(p_ref, s_ref, o_ref):
        g = p_ref[0].astype(F32)
        for k in range(1, n_parts):
            g = g + p_ref[k].astype(F32)
        o_ref[0] = g
        o_ref[1], o_ref[2], o_ref[3] = _adamw_update(g, s_ref[0], s_ref[1], s_ref[2])

    return pl.pallas_call(
        body, name="adamw", grid=(R // tr,),
        in_specs=[pl.BlockSpec((n_parts, tr, C), lambda n: (0, n + TAIL_ROW0 // tr, 0)),
                  pl.BlockSpec((3, tr, C), lambda n: (0, n, 0))],
        out_specs=pl.BlockSpec((4, tr, C), lambda n: (0, n, 0)), out_shape=jax.ShapeDtypeStruct((4, R, C), F32),
        compiler_params=_params(1))(parts, state)


def _adamw_w_in(parts, w, m, v, core):
    n_parts = parts.shape[0]
    tc = 256

    def body(core_ref, p_ref, w_ref, m_ref, v_ref, o_ref):
        g_t = p_ref[0].astype(F32)
        for k in range(1, n_parts):
            g_t = g_t + p_ref[k].astype(F32)
        g_t = jnp.concatenate([g_t, jnp.zeros((512 - W_IN_WIN, tc), F32)], axis=0).T
        g = jnp.where(core_ref[0] == 0, g_t[:, 0:W_IN_COLS], g_t[:, W_IN_WIN - W_IN_COLS:W_IN_WIN])
        o_ref[0, 0] = g
        o_ref[1, 0], o_ref[2, 0], o_ref[3, 0] = _adamw_update(g, w_ref[0], m_ref[0], v_ref[0])

    nat = pl.BlockSpec((1, tc, W_IN_COLS), lambda l, j: (l, j, 0))
    return pl.pallas_call(
        body, name="adamw_w_in", grid=(DEPTH, D_MODEL // tc),
        in_specs=[pl.BlockSpec(memory_space=pltpu.SMEM),
                  pl.BlockSpec((n_parts, W_IN_WIN, tc), lambda l, j: (0, l, j)), nat, nat, nat],
        out_specs=pl.BlockSpec((4, 1, tc, W_IN_COLS), lambda l, j: (0, l, j, 0)),
        out_shape=jax.ShapeDtypeStruct((4, DEPTH, D_MODEL, W_IN_COLS), F32),
        compiler_params=_params(2))(core, parts, w, m, v)


def kernel(x, positions, norm_pre, w_in, attn_sinks, conv_w, conv_b, mla_q_norm, mla_w_uq, mla_kv_norm, mla_w_ukv, group_norm, w_out, norm_post, loss_target, m_norm_pre, m_w_in, m_attn_sinks, m_conv_w, m_conv_b, m_mla_q_norm, m_mla_w_uq, m_mla_kv_norm, m_mla_w_ukv, m_group_norm, m_w_out, m_norm_post, v_norm_pre, v_w_in, v_attn_sinks, v_conv_w, v_conv_b, v_mla_q_norm, v_mla_w_uq, v_mla_kv_norm, v_mla_w_ukv, v_group_norm, v_w_out, v_norm_post):
    local = dict(norm_pre=norm_pre, w_in=w_in, attn_sinks=attn_sinks, conv_w=conv_w, conv_b=conv_b,
                 mla_q_norm=mla_q_norm, mla_w_uq=mla_w_uq, mla_kv_norm=mla_kv_norm, mla_w_ukv=mla_w_ukv,
                 group_norm=group_norm, w_out=w_out, norm_post=norm_post)
    mom = dict(norm_pre=m_norm_pre, w_in=m_w_in, attn_sinks=m_attn_sinks, conv_w=m_conv_w, conv_b=m_conv_b,
               mla_q_norm=m_mla_q_norm, mla_w_uq=m_mla_w_uq, mla_kv_norm=m_mla_kv_norm, mla_w_ukv=m_mla_w_ukv,
               group_norm=m_group_norm, w_out=m_w_out, norm_post=m_norm_post)
    vel = dict(norm_pre=v_norm_pre, w_in=v_w_in, attn_sinks=v_attn_sinks, conv_w=v_conv_w, conv_b=v_conv_b,
               mla_q_norm=v_mla_q_norm, mla_w_uq=v_mla_w_uq, mla_kv_norm=v_mla_kv_norm, mla_w_ukv=v_mla_w_ukv,
               group_norm=v_group_norm, w_out=v_w_out, norm_post=v_norm_post)

    c = lax.axis_index("c")

    tile = 16
    slot_rows = 464
    shift = 8 * lax.axis_index("y") + 4 * c
    wt = lax.dynamic_update_slice(jnp.zeros((DEPTH, slot_rows, D_MODEL), BF),
                                  jnp.transpose(w_in, (0, 2, 1)).astype(BF), (0, shift, 0))
    payload = jnp.concatenate([wt.reshape(DEPTH * slot_rows, D_MODEL),
                               w_out.astype(BF).reshape(DEPTH * 128, D_MODEL),
                               _pack_small([local[n] for n in SMALL_SHARDED], BF)], axis=0)
    gathered = _all_gather(payload)
    W = {n: local[n] for n in REPLICATED}

    def nat_rows(l, lo, hi):
        def piece(d, r0, r1):
            base = slot_rows * l - (W_IN_COLS * d) // tile * tile
            return gathered[d, base + r0:base + r1]

        out, run = [], None
        for r0 in range(lo, hi, tile):
            d0, d1 = r0 // W_IN_COLS, (r0 + tile - 1) // W_IN_COLS
            if d0 == d1 and run is not None and run[0] == d0:
                run = (d0, run[1], r0 + tile)
                continue
            if run is not None:
                out.append(piece(*run))
                run = None
            if d0 == d1:
                run = (d0, r0, r0 + tile)
            else:
                out.append(piece(d0, r0, r0 + tile) + piece(d1, r0, r0 + tile))
        if run is not None:
            out.append(piece(*run))
        return out

    z = lambda n: [jnp.zeros((n, D_MODEL), BF)]
    W["wpt"] = [jnp.concatenate(nat_rows(l, 2464, 3488) + nat_rows(l, 512, 1664) + z(64) + nat_rows(l, 1664, 1696)
                                + z(96) + nat_rows(l, 1680, 1696) + nat_rows(l, 1664, 1680) + z(32)
                                + nat_rows(l, 0, 512) + nat_rows(l, 1696, 2464), axis=0) for l in range(DEPTH)]
    wo0 = DEPTH * slot_rows
    W["w_out"] = gathered[:, wo0:wo0 + DEPTH * 128].reshape(N_DEV, DEPTH, 128, D_MODEL).transpose(1, 0, 2, 3).reshape(
        DEPTH, D_MODEL, D_MODEL)
    flat = gathered[:, wo0 + DEPTH * 128:].reshape(N_DEV, SMALL_ROWS * D_MODEL)
    off = 0
    for n in SMALL_SHARDED:
        depth, rows, width = local[n].shape
        size = depth * rows * width
        W[n] = flat[:, off:off + size].reshape(N_DEV, depth, rows, width).transpose(1, 2, 0, 3).reshape(
            depth, rows, N_DEV * width)
        off += size

    sq, grad_x, g = _local_step(x[0], positions[0], W, loss_target[0])
    loss = lax.psum(0.5 / D_MODEL * jnp.sum(sq), ("x", "y", "c"))

    cols = []
    for n in SMALL_SHARDED:
        depth, rows, width = local[n].shape
        cols.append(jnp.stack(g[n]).reshape(depth, rows, N_DEV, width).transpose(2, 0, 1, 3).reshape(N_DEV, -1))
    rep = jnp.concatenate([a.reshape(-1) for n in REPLICATED for a in g[n]])
    cols.append(jnp.broadcast_to(rep[None], (N_DEV, rep.shape[0])))
    small = jnp.concatenate(cols, axis=1)
    small = jnp.pad(small, ((0, 0), (0, SMALL_ROWS * D_MODEL - small.shape[1]))).reshape(N_DEV, SMALL_ROWS, D_MODEL)
    krs = [p[C_CKR + 64:C_CKR + 96] + _swap_rows32(p[C_CKRS + 64:C_CKRS + 96]) for p in g["w_in_pt"]]
    pieces = []
    for d in range(N_DEV):
        lo = W_IN_COLS * d // 8 * 8
        for l in range(DEPTH):
            pieces += _rows_of_w_in_t(lo, lo + W_IN_WIN, g["w_in_pt"][l], krs[l])
        pieces += [g["w_out"][l][128 * d:128 * (d + 1)] for l in range(DEPTH)]
        pieces.append(small[d])
    blocks = jnp.concatenate(pieces, axis=0).astype(BF).reshape(N_DEV, -1, D_MODEL)
    mine = lax.dynamic_index_in_dim(blocks.reshape(N_CHIP, 2, -1, D_MODEL), c, axis=1, keepdims=False)
    received = _chip_exchange(_pair_sum(mine, _sibling_swap(blocks)))

    out = _unpack_state(_adamw(received, _pack_state([local, mom, vel])), local)
    out["w_in"] = _adamw_w_in(received, w_in, m_w_in, v_w_in, c.astype(jnp.int32).reshape(1))
    return (loss, grad_x[None], *[out[n][t] for t in range(4) for n in ORDER])
```

```python
import functools

import jax
import jax.numpy as jnp
import numpy as np
from jax import lax
from jax.experimental import pallas as pl
from jax.experimental.pallas import tpu as pltpu

F32 = jnp.float32
BF = jnp.bfloat16
MESH = pl.DeviceIdType.MESH

D_MODEL = 1024
DEPTH = 2
EPS = 1e-6
N_DEV = 8
VMEM_LIMIT = 56 * 1024 * 1024
NEG = -1e30
MLA_SCALE = 96.0 ** -0.5
SB_SCALE = 0.125
LOG2E = 1.4426950408889634

NP = 3712
NF = 2432
NB = 1280
C_GATE = 0
C_BB = 1024
C_BC = 1280
C_BX = 1536
C_CQ = 1792
C_CKV = 2048
C_CKR = 2176
C_CKRS = 2304
C_AQ = 0
C_AK = 256
C_AV = 384
C_DQ = 512
C_DK = 768
C_DV = 1024
C_END = 3712

def _swap32(a):
    return jnp.concatenate([a[:, 16:32], a[:, 0:16]], axis=1)

ADAM_LR, ADAM_B1, ADAM_B2, ADAM_EPS, ADAM_WD, ADAM_STEP = 0.001, 0.9, 0.999, 1e-08, 0.01, 10


def _dot(a, b):
    return jnp.dot(a, b, preferred_element_type=F32)


def _dot_nt(a, b):
    return lax.dot_general(a, b, (((1,), (1,)), ((), ())), preferred_element_type=F32)


def _dot_tn(a, b):
    return lax.dot_general(a, b, (((0,), (0,)), ((), ())), preferred_element_type=F32)


def _params(n_grid):
    return pltpu.CompilerParams(dimension_semantics=("arbitrary",) * n_grid, vmem_limit_bytes=VMEM_LIMIT)


def _rms_fwd(x, g):
    r = lax.rsqrt(jnp.mean(x * x, axis=-1, keepdims=True) + EPS)
    return (x * r) * g, r


def _rms_bwd(x, g, r, dy, width=None):
    n = x.shape[-1] if width is None else width
    u = dy * g
    dx = r * u - x * (r * r * r) * (jnp.sum(x * u, axis=-1, keepdims=True) / n)
    return dx, dy * (x * r)


def _iota(shape, axis):
    return lax.broadcasted_iota(jnp.int32, shape, axis)


def _inproj_fwd(x, g, wpt):
    T = x.shape[0]
    tm = 256

    def body(x_ref, g_ref, w_ref, h32_ref, h16_ref, xn_ref):
        xn, _ = _rms_fwd(x_ref[...], g_ref[...])
        xn = xn.astype(BF)
        xn_ref[...] = xn
        h = _dot_nt(xn, w_ref[...])
        h32_ref[...] = h[:, :NF]
        h16_ref[...] = h[:, NF:].astype(BF)

    return pl.pallas_call(
        body, name="inproj_fwd", grid=(T // tm,),
        in_specs=[pl.BlockSpec((tm, D_MODEL), lambda n: (n, 0)),
                  pl.BlockSpec((1, D_MODEL), lambda n: (0, 0)),
                  pl.BlockSpec((NP, D_MODEL), lambda n: (0, 0))],
        out_specs=[pl.BlockSpec((tm, NF), lambda n: (n, 0)),
                   pl.BlockSpec((tm, NB), lambda n: (n, 0)),
                   pl.BlockSpec((tm, D_MODEL), lambda n: (n, 0))],
        out_shape=[jax.ShapeDtypeStruct((T, NF), F32), jax.ShapeDtypeStruct((T, NB), BF),
                   jax.ShapeDtypeStruct((T, D_MODEL), BF)],
        compiler_params=_params(1))(x, g, wpt)


def _inproj_bwd(parts, wpt, x, xn, g, dxo):
    T = x.shape[0]
    tm = 256
    np_ = len(parts)
    chunks = (0, 1280, 2560, NP)
    assert sum(p.shape[1] for p in parts) == C_END == NP

    def body(*refs):
        part_refs = refs[:np_]
        w_ref, x_ref, xn_ref, g_ref, dxo_ref, dw_ref, dx_ref, dg_ref = refs[np_:]
        n = pl.program_id(0)

        @pl.when(n == 0)
        def _():
            dw_ref[...] = jnp.zeros_like(dw_ref)
            dg_ref[...] = jnp.zeros_like(dg_ref)

        dh = jnp.concatenate([r[...].astype(BF) for r in part_refs], axis=1)
        xnv = xn_ref[...]
        for lo, hi in zip(chunks[:-1], chunks[1:]):
            dw_ref[:, lo:hi] += _dot_tn(xnv, dh[:, lo:hi])
        dxn = _dot(dh, w_ref[...])
        xv = x_ref[...]
        _, r = _rms_fwd(xv, g_ref[...])
        dx, dgt = _rms_bwd(xv, g_ref[...], r, dxn)
        dx_ref[...] = dxo_ref[...] + dx
        dg_ref[...] += jnp.sum(dgt, axis=0, keepdims=True)

    once = pl.Buffered(1)
    return pl.pallas_call(
        body, name="inproj_bwd", grid=(T // tm,),
        in_specs=[pl.BlockSpec((tm, p.shape[1]), lambda n: (n, 0)) for p in parts]
        + [pl.BlockSpec((NP, D_MODEL), lambda n: (0, 0), pipeline_mode=once),
           pl.BlockSpec((tm, D_MODEL), lambda n: (n, 0)),
           pl.BlockSpec((tm, D_MODEL), lambda n: (n, 0)),
           pl.BlockSpec((1, D_MODEL), lambda n: (0, 0)),
           pl.BlockSpec((tm, D_MODEL), lambda n: (n, 0))],
        out_specs=[pl.BlockSpec((D_MODEL, NP), lambda n: (0, 0), pipeline_mode=once),
                   pl.BlockSpec((tm, D_MODEL), lambda n: (n, 0)),
                   pl.BlockSpec((1, D_MODEL), lambda n: (0, 0))],
        out_shape=[jax.ShapeDtypeStruct((D_MODEL, NP), F32), jax.ShapeDtypeStruct((T, D_MODEL), F32),
                   jax.ShapeDtypeStruct((1, D_MODEL), F32)],
        compiler_params=_params(1))(*parts, wpt, x, xn, g, dxo)


SWA_BLK = 128
SWA_TQ = 2048


def _bdot_nt(a, b):
    return lax.dot_general(a, b, (((2,), (2,)), ((0,), (0,))), preferred_element_type=F32)


def _bdot(a, b):
    return lax.dot_general(a, b, (((2,), (1,)), ((0,), (0,))), preferred_element_type=F32)


def _bdot_tn(a, b):
    return lax.dot_general(a, b, (((1,), (1,)), ((0,), (0,))), preferred_element_type=F32)


def _swa_probs(q, kc, kp, sink, mask_c, mask_p):
    sc = jnp.where(mask_c, _bdot_nt(q, kc) * SB_SCALE, NEG)
    sp = jnp.where(mask_p, _bdot_nt(q, kp) * SB_SCALE, NEG)
    m = jnp.maximum(jnp.maximum(jnp.max(sc, axis=-1, keepdims=True), jnp.max(sp, axis=-1, keepdims=True)), sink)
    pc = jnp.exp(sc - m)
    pp = jnp.exp(sp - m)
    ps = jnp.exp(sink - m)
    inv = 1.0 / (jnp.sum(pc, axis=-1, keepdims=True) + jnp.sum(pp, axis=-1, keepdims=True) + ps)
    return pc * inv, pp * inv, ps * inv


def _swa_masks(n, nb):
    blk = _iota((nb, SWA_BLK, SWA_BLK), 0)
    row = _iota((nb, SWA_BLK, SWA_BLK), 1)
    col = _iota((nb, SWA_BLK, SWA_BLK), 2)
    return col <= row, jnp.logical_and(col > row, jnp.logical_or(blk > 0, n > 0))


def _swa_specs(tq):
    halo = tq // SWA_BLK
    return [pl.BlockSpec(memory_space=pltpu.SMEM),
            pl.BlockSpec((tq, 256), lambda n: (n, C_AQ // 256)),
            pl.BlockSpec((tq, 128), lambda n: (n, C_AK // 128)),
            pl.BlockSpec((SWA_BLK, 128), lambda n: (jnp.maximum(n * halo - 1, 0), C_AK // 128)),
            pl.BlockSpec((tq, 128), lambda n: (n, C_AV // 128)),
            pl.BlockSpec((SWA_BLK, 128), lambda n: (jnp.maximum(n * halo - 1, 0), C_AV // 128))]


def _swa_blocked(cur_ref, prev_ref, gs, nb):
    cur = cur_ref[:, gs].reshape(nb, SWA_BLK, 64)
    prev = jnp.concatenate([prev_ref[:, gs].reshape(1, SWA_BLK, 64), cur[:nb - 1]], axis=0) if nb > 1 \
        else prev_ref[:, gs].reshape(1, SWA_BLK, 64)
    return cur, prev


def _swa_masks2(n, nb):
    mask_c, mask_p = _swa_masks(n, nb)
    return jnp.concatenate([mask_c, mask_c], axis=1), jnp.concatenate([mask_p, mask_p], axis=1)


def _swa_stacked(ref, g, nb):
    a = ref[:, 128 * g:128 * g + 64].reshape(nb, SWA_BLK, 64)
    b = ref[:, 128 * g + 64:128 * g + 128].reshape(nb, SWA_BLK, 64)
    return jnp.concatenate([a, b], axis=1)


def _swa_fwd(h16, sinks):
    T = h16.shape[0]
    tq = SWA_TQ if T % SWA_TQ == 0 else SWA_BLK
    nb = tq // SWA_BLK

    def body(s_ref, q_ref, kc_ref, kp_ref, vc_ref, vp_ref, o_ref):
        n = pl.program_id(0)
        mask_c, mask_p = _swa_masks2(n, nb)
        first = _iota((1, 2 * SWA_BLK, 1), 1) < SWA_BLK
        for g in range(2):
            gs = slice(g * 64, (g + 1) * 64)
            kc, kp = _swa_blocked(kc_ref, kp_ref, gs, nb)
            vc, vp = _swa_blocked(vc_ref, vp_ref, gs, nb)
            q = _swa_stacked(q_ref, g, nb)
            sink = jnp.where(first, s_ref[2 * g], s_ref[2 * g + 1])
            pc, pp, _ = _swa_probs(q, kc, kp, sink, mask_c, mask_p)
            o = _bdot(pc.astype(BF), vc) + _bdot(pp.astype(BF), vp)
            o_ref[:, 128 * g:128 * g + 64] = o[:, :SWA_BLK].reshape(tq, 64)
            o_ref[:, 128 * g + 64:128 * g + 128] = o[:, SWA_BLK:].reshape(tq, 64)

    return pl.pallas_call(
        body, name="swa_fwd", grid=(T // tq,), in_specs=_swa_specs(tq),
        out_specs=pl.BlockSpec((tq, 256), lambda n: (n, 0)),
        out_shape=jax.ShapeDtypeStruct((T, 256), F32),
        compiler_params=_params(1))(sinks, h16, h16, h16, h16, h16)


def _swa_bwd(h16, sinks, dya):
    T = h16.shape[0]
    tq = SWA_TQ if T % SWA_TQ == 0 else SWA_BLK
    nb = tq // SWA_BLK

    def body(s_ref, q_ref, kc_ref, kp_ref, vc_ref, vp_ref, do_ref, dq_ref, dk_ref, dv_ref, ds_ref):
        n = pl.program_id(0)

        @pl.when(n == 0)
        def _():
            dk_ref[...] = jnp.zeros_like(dk_ref)
            dv_ref[...] = jnp.zeros_like(dv_ref)
            ds_ref[...] = jnp.zeros_like(ds_ref)

        mask_c, mask_p = _swa_masks2(n, nb)
        first = _iota((1, 2 * SWA_BLK, 1), 1) < SWA_BLK
        rows = pl.ds(pl.multiple_of(n * tq, tq), tq)
        before = pl.ds(pl.multiple_of(jnp.maximum(n * nb - 1, 0) * SWA_BLK, SWA_BLK), SWA_BLK)
        lane = _iota((8, 128), 1)
        row8 = _iota((8, 128), 0)

        def to_keys(own, prev):
            if nb == 1:
                return own
            return own + jnp.concatenate([prev[1:], jnp.zeros((1, SWA_BLK, 64), F32)], axis=0)

        for g in range(2):
            gs = slice(g * 64, (g + 1) * 64)
            q = _swa_stacked(q_ref, g, nb)
            kc, kp = _swa_blocked(kc_ref, kp_ref, gs, nb)
            vc, vp = _swa_blocked(vc_ref, vp_ref, gs, nb)
            sink = jnp.where(first, s_ref[2 * g], s_ref[2 * g + 1])
            pc, pp, ps = _swa_probs(q, kc, kp, sink, mask_c, mask_p)
            pcb, ppb = pc.astype(BF), pp.astype(BF)
            do = _swa_stacked(do_ref, g, nb)
            dob = do.astype(BF)
            o = _bdot(pcb, vc) + _bdot(ppb, vp)
            dd = jnp.sum(do * o, axis=-1, keepdims=True)
            dsc = (pc * (_bdot_nt(dob, vc) - dd) * SB_SCALE).astype(BF)
            dsp = (pp * (_bdot_nt(dob, vp) - dd) * SB_SCALE).astype(BF)
            dq = (_bdot(dsc, kc) + _bdot(dsp, kp)).astype(BF)
            dq_ref[:, 128 * g:128 * g + 64] = dq[:, :SWA_BLK].reshape(tq, 64)
            dq_ref[:, 128 * g + 64:128 * g + 128] = dq[:, SWA_BLK:].reshape(tq, 64)
            dkp, dvp = _bdot_tn(dsp, q), _bdot_tn(ppb, dob)
            dk_ref[rows, gs] += to_keys(_bdot_tn(dsc, q), dkp).reshape(tq, 64)
            dv_ref[rows, gs] += to_keys(_bdot_tn(pcb, dob), dvp).reshape(tq, 64)
            dk_ref[before, gs] += dkp[0]
            dv_ref[before, gs] += dvp[0]
            psd = ps * dd
            for hh in range(2):
                part = psd[:, hh * SWA_BLK:(hh + 1) * SWA_BLK]
                ds_ref[...] += jnp.where(jnp.logical_and(lane == 2 * g + hh, row8 == 0), -jnp.sum(part), 0.0)

    return pl.pallas_call(
        body, name="swa_bwd", grid=(T // tq,),
        in_specs=_swa_specs(tq) + [pl.BlockSpec((tq, 256), lambda n: (n, 0))],
        out_specs=[pl.BlockSpec((tq, 256), lambda n: (n, 0)),
                   pl.BlockSpec((T, 128), lambda n: (0, 0)),
                   pl.BlockSpec((T, 128), lambda n: (0, 0)),
                   pl.BlockSpec((8, 128), lambda n: (0, 0))],
        out_shape=[jax.ShapeDtypeStruct((T, 256), BF), jax.ShapeDtypeStruct((T, 128), F32),
                   jax.ShapeDtypeStruct((T, 128), F32), jax.ShapeDtypeStruct((8, 128), F32)],
        compiler_params=_params(1))(sinks, h16, h16, h16, h16, h16, dya)


def _conv_u(bc_ref, bx_ref, bch_ref, bxh_ref, n, tm):
    u = bc_ref[...] * bx_ref[...]
    uh = bch_ref[...] * bxh_ref[...] * (n > 0).astype(F32)
    rowi = _iota((tm, 256), 0)
    u1 = jnp.where(rowi == 0, uh[7:8, :], pltpu.roll(u, 1, axis=0))
    u2 = jnp.where(rowi == 0, uh[6:7, :], jnp.where(rowi == 1, uh[7:8, :], pltpu.roll(u, 2, axis=0)))
    return u, u1, u2


def _conv_fwd(h32, cw, cb):
    T = h32.shape[0]
    tm = 1024 if T % 1024 == 0 else T
    hb = tm // 8

    def body(bb_ref, bc_ref, bx_ref, bch_ref, bxh_ref, w_ref, b_ref, o_ref):
        n = pl.program_id(0)
        u, u1, u2 = _conv_u(bc_ref, bx_ref, bch_ref, bxh_ref, n, tm)
        y = w_ref[0:1, :] * u2 + w_ref[1:2, :] * u1 + w_ref[2:3, :] * u + b_ref[...]
        o_ref[...] = bb_ref[...] * y

    halo = lambda c: pl.BlockSpec((8, 256), lambda n: (jnp.maximum(n * hb - 1, 0), c // 256))
    return pl.pallas_call(
        body, name="conv_fwd", grid=(T // tm,),
        in_specs=[pl.BlockSpec((tm, 256), lambda n: (n, C_BB // 256)),
                  pl.BlockSpec((tm, 256), lambda n: (n, C_BC // 256)),
                  pl.BlockSpec((tm, 256), lambda n: (n, C_BX // 256)),
                  halo(C_BC), halo(C_BX),
                  pl.BlockSpec((8, 256), lambda n: (0, 0)),
                  pl.BlockSpec((1, 256), lambda n: (0, 0))],
        out_specs=pl.BlockSpec((tm, 256), lambda n: (n, 0)),
        out_shape=jax.ShapeDtypeStruct((T, 256), F32),
        compiler_params=_params(1))(h32, h32, h32, h32, h32, cw, cb)


def _conv_bwd(h32, cw, cb, dyb):
    T = h32.shape[0]
    tm = 1024 if T % 1024 == 0 else T
    hb = tm // 8
    nt = T // tm

    def body(bb_ref, bc_ref, bx_ref, bch_ref, bxh_ref, bbn_ref, dy_ref, dyn_ref, w_ref, b_ref,
             dbb_ref, dbc_ref, dbx_ref, dw_ref):
        n = pl.program_id(0)

        @pl.when(n == 0)
        def _():
            dw_ref[...] = jnp.zeros_like(dw_ref)

        u, u1, u2 = _conv_u(bc_ref, bx_ref, bch_ref, bxh_ref, n, tm)
        w0, w1, w2 = w_ref[0:1, :], w_ref[1:2, :], w_ref[2:3, :]
        y = w0 * u2 + w1 * u1 + w2 * u + b_ref[...]
        dyb_ = dy_ref[...]
        dbb_ref[...] = (dyb_ * y).astype(BF)
        dy = dyb_ * bb_ref[...]
        dyn = dyn_ref[...] * bbn_ref[...] * (n < nt - 1).astype(F32)
        rowi = _iota((tm, 256), 0)
        dy1 = jnp.where(rowi == tm - 1, dyn[0:1, :], pltpu.roll(dy, tm - 1, axis=0))
        dy2 = jnp.where(rowi == tm - 2, dyn[0:1, :],
                        jnp.where(rowi == tm - 1, dyn[1:2, :], pltpu.roll(dy, tm - 2, axis=0)))
        du = w2 * dy + w1 * dy1 + w0 * dy2
        dbc_ref[...] = (du * bx_ref[...]).astype(BF)
        dbx_ref[...] = (du * bc_ref[...]).astype(BF)
        dw_ref[0:1, :] += jnp.sum(dy * u2, axis=0, keepdims=True)
        dw_ref[1:2, :] += jnp.sum(dy * u1, axis=0, keepdims=True)
        dw_ref[2:3, :] += jnp.sum(dy * u, axis=0, keepdims=True)
        dw_ref[3:4, :] += jnp.sum(dy, axis=0, keepdims=True)

    halo = lambda c: pl.BlockSpec((8, 256), lambda n: (jnp.maximum(n * hb - 1, 0), c // 256))
    nxt = lambda c: pl.BlockSpec((8, 256), lambda n: (jnp.minimum((n + 1) * hb, T // 8 - 1), c // 256))
    cur = lambda c: pl.BlockSpec((tm, 256), lambda n: (n, c // 256))
    return pl.pallas_call(
        body, name="conv_bwd", grid=(nt,),
        in_specs=[cur(C_BB), cur(C_BC), cur(C_BX), halo(C_BC), halo(C_BX), nxt(C_BB),
                  cur(0), nxt(0),
                  pl.BlockSpec((8, 256), lambda n: (0, 0)),
                  pl.BlockSpec((1, 256), lambda n: (0, 0))],
        out_specs=[cur(0), cur(0), cur(0), pl.BlockSpec((8, 256), lambda n: (0, 0))],
        out_shape=[jax.ShapeDtypeStruct((T, 256), BF)] * 3 + [jax.ShapeDtypeStruct((8, 256), F32)],
        compiler_params=_params(1))(h32, h32, h32, h32, h32, h32, dyb, dyb, cw, cb)


def _cprep_specs(tm):
    return [pl.BlockSpec((tm, 256), lambda n: (n, C_CQ // 256)),
            pl.BlockSpec((tm, 128), lambda n: (n, C_CKV // 128)),
            pl.BlockSpec((tm, 128), lambda n: (n, C_CKR // 128)),
            pl.BlockSpec((tm, 128), lambda n: (n, C_CKRS // 128)),
            pl.BlockSpec((1, 256), lambda n: (0, 0)),
            pl.BlockSpec((1, 128), lambda n: (0, 0)),
            pl.BlockSpec((tm, 128), lambda n: (n, 0)),
            pl.BlockSpec((tm, 128), lambda n: (n, 0))]


def _cprep_fwd(h32, gq, gkv, wuq2, wkv2, cosk, sin):
    T = h32.shape[0]
    tm = 1024 if T % 1024 == 0 else T

    def body(cq_ref, ckv_ref, ckr_ref, ckrs_ref, gq_ref, gkv_ref, cos_ref, sin_ref, wuq_ref, wkv_ref,
             q_ref, k_ref, v_ref):
        cosk_, sin_ = cos_ref[...], sin_ref[...]
        cosq = cosk_ + (_iota((tm, 128), 1) < 64).astype(F32)
        cqn, _ = _rms_fwd(cq_ref[...], gq_ref[...])
        q2 = _dot(cqn.astype(BF), wuq_ref[...])
        ckvn, _ = _rms_fwd(ckv_ref[...], gkv_ref[...])
        kv2 = _dot(ckvn.astype(BF), wkv_ref[...])
        kr = ckr_ref[...] * cosk_ + ckrs_ref[...] * sin_
        for h in range(4):
            hs = slice(h * 128, (h + 1) * 128)
            q_ref[:, hs] = ((q2[:, hs] * cosq + q2[:, 512 + h * 128:512 + (h + 1) * 128] * sin_) * MLA_SCALE).astype(BF)
            k_ref[:, hs] = (kv2[:, hs] + kr).astype(BF)
        ones = _iota((tm, 512), 1) % 128 == 64
        v_ref[...] = jnp.where(ones, 1.0, kv2[:, 512:]).astype(BF)

    return pl.pallas_call(
        body, name="cprep_fwd", grid=(T // tm,),
        in_specs=_cprep_specs(tm) + [pl.BlockSpec((256, 1024), lambda n: (0, 0)),
                                     pl.BlockSpec((128, 1024), lambda n: (0, 0))],
        out_specs=[pl.BlockSpec((tm, 512), lambda n: (n, 0))] * 3,
        out_shape=[jax.ShapeDtypeStruct((T, 512), BF)] * 3,
        compiler_params=_params(1))(h32, h32, h32, h32, gq, gkv, cosk, sin, wuq2, wkv2)


def _cprep_bwd(h32, gq, gkv, wuq2t, wkv2t, cosk, sin, dq, dk, dv):
    T = h32.shape[0]
    tm = 1024 if T % 1024 == 0 else T

    def body(cq_ref, ckv_ref, ckr_ref, ckrs_ref, gq_ref, gkv_ref, cos_ref, sin_ref, wuq_ref, wkv_ref,
             dq_ref, dk_ref, dv_ref,
             dcq_ref, dckv_ref, dckr_ref, dckrs_ref, dwuq_ref, dwkv_ref, dgq_ref, dgkv_ref):
        n = pl.program_id(0)

        @pl.when(n == 0)
        def _():
            dwuq_ref[...] = jnp.zeros_like(dwuq_ref)
            dwkv_ref[...] = jnp.zeros_like(dwkv_ref)
            dgq_ref[...] = jnp.zeros_like(dgq_ref)
            dgkv_ref[...] = jnp.zeros_like(dgkv_ref)

        cosk_, sin_ = cos_ref[...], sin_ref[...]
        cosq = cosk_ + (_iota((tm, 128), 1) < 64).astype(F32)
        dkr = jnp.zeros((tm, 128), F32)
        plain, swapped = [], []
        for h in range(4):
            hs = slice(h * 128, (h + 1) * 128)
            dqh = dq_ref[:, hs] * MLA_SCALE
            plain.append((dqh * cosq).astype(BF))
            swapped.append((dqh * sin_).astype(BF))
            dkr = dkr + dk_ref[:, hs]
        dq2 = jnp.concatenate(plain + swapped, axis=1)
        dkv2 = jnp.concatenate([dk_ref[...].astype(BF), dv_ref[...].astype(BF)], axis=1)
        dckr_ref[...] = (dkr * cosk_).astype(BF)
        dckrs_ref[...] = (dkr * sin_).astype(BF)

        cq, gq_ = cq_ref[...], gq_ref[...]
        cqn, rq = _rms_fwd(cq, gq_)
        dwuq_ref[...] += _dot_tn(cqn.astype(BF), dq2)
        dcq, dgt = _rms_bwd(cq, gq_, rq, _dot(dq2, wuq_ref[...]))
        dcq_ref[...] = dcq.astype(BF)
        dgq_ref[...] += jnp.sum(dgt, axis=0, keepdims=True)

        ckv, gkv_ = ckv_ref[...], gkv_ref[...]
        ckvn, rkv = _rms_fwd(ckv, gkv_)
        dwkv_ref[...] += _dot_tn(ckvn.astype(BF), dkv2)
        dckv, dgt2 = _rms_bwd(ckv, gkv_, rkv, _dot(dkv2, wkv_ref[...]))
        dckv_ref[...] = dckv.astype(BF)
        dgkv_ref[...] += jnp.sum(dgt2, axis=0, keepdims=True)

    row = lambda w: pl.BlockSpec((tm, w), lambda n: (n, 0))
    return pl.pallas_call(
        body, name="cprep_bwd", grid=(T // tm,),
        in_specs=_cprep_specs(tm) + [pl.BlockSpec((1024, 256), lambda n: (0, 0)),
                                     pl.BlockSpec((1024, 128), lambda n: (0, 0)),
                                     row(512), row(512), row(512)],
        out_specs=[row(256), row(128), row(128), row(128),
                   pl.BlockSpec((256, 1024), lambda n: (0, 0)), pl.BlockSpec((128, 1024), lambda n: (0, 0)),
                   pl.BlockSpec((1, 256), lambda n: (0, 0)), pl.BlockSpec((1, 128), lambda n: (0, 0))],
        out_shape=[jax.ShapeDtypeStruct((T, 256), BF), jax.ShapeDtypeStruct((T, 128), BF),
                   jax.ShapeDtypeStruct((T, 128), BF), jax.ShapeDtypeStruct((T, 128), BF),
                   jax.ShapeDtypeStruct((256, 1024), F32), jax.ShapeDtypeStruct((128, 1024), F32),
                   jax.ShapeDtypeStruct((1, 256), F32), jax.ShapeDtypeStruct((1, 128), F32)],
        compiler_params=_params(1))(h32, h32, h32, h32, gq, gkv, cosk, sin, wuq2t, wkv2t, dq, dk, dv)


MLA_TILE = 512
MLA_HEADS_PER_STEP = 4


def _causal_mask(t):
    return _iota((t, t), 1) <= _iota((t, t), 0)


def _mla_fwd(q, k, v):
    T = q.shape[0]
    tq = MLA_TILE

    def body(q_ref, k_ref, v_ref, o_ref, lse_ref):
        i = pl.program_id(1)
        mask = _causal_mask(tq)
        heads = [slice(128 * h, 128 * h + 128) for h in range(MLA_HEADS_PER_STEP)]
        qs = [q_ref[:, hs] for hs in heads]

        def step(j, carry, masked):
            rows = pl.ds(pl.multiple_of(j * tq, tq), tq)
            out = []
            for hh, hs in enumerate(heads):
                m, acc = carry[hh]
                s = _dot_nt(qs[hh], k_ref[rows, hs])
                if masked:
                    s = jnp.where(mask, s, NEG)
                m_new = jnp.maximum(m, jnp.max(s, axis=-1, keepdims=True))
                p = jnp.exp((s - m_new).astype(BF))
                acc = jnp.exp(m - m_new) * acc + _dot(p, v_ref[rows, hs])
                out.append((m_new, acc))
            return tuple(out)

        init = ((jnp.full((tq, 1), NEG, F32), jnp.zeros((tq, 128), F32)),) * len(heads)
        carry = lax.fori_loop(0, i // 2, lambda t, c: step(2 * t + 1, step(2 * t, c, False), False), init)
        carry = lax.cond(i % 2 == 1, lambda c: step(i - 1, c, False), lambda c: c, carry)
        carry = step(i, carry, True)
        for hh, hs in enumerate(heads):
            m, acc = carry[hh]
            l = acc[:, 64:65]
            o_ref[:, hs] = acc * (1.0 / l)
            lse_ref[:, hs] = jnp.broadcast_to(m + jnp.log(l), (tq, 128))

    width = 128 * MLA_HEADS_PER_STEP
    blk = pl.BlockSpec((tq, width), lambda h, i: (i, h))
    full = pl.BlockSpec((T, width), lambda h, i: (0, h))
    return pl.pallas_call(
        body, name="mla_fwd", grid=(4 // MLA_HEADS_PER_STEP, T // tq), in_specs=[blk, full, full],
        out_specs=[blk, blk],
        out_shape=[jax.ShapeDtypeStruct((T, 512), F32), jax.ShapeDtypeStruct((T, 512), F32)],
        compiler_params=_params(2))(q, k, v)


def _mla_bwd(q, k, v, o, lse, do):
    T = q.shape[0]
    tq = MLA_TILE

    def body(q_ref, k_ref, v_ref, o_ref, lse_ref, do_ref, dq_ref, dk_ref, dv_ref):
        i = pl.program_id(1)

        @pl.when(i == 0)
        def _():
            dk_ref[...] = jnp.zeros_like(dk_ref)
            dv_ref[...] = jnp.zeros_like(dv_ref)

        heads = [slice(0, 128), slice(128, 256)]
        mask = _causal_mask(tq)
        qs, dobs, dds, lses = [], [], [], []
        for hs in heads:
            do = do_ref[:, hs]
            qs.append(q_ref[:, hs])
            dobs.append(do.astype(BF))
            dds.append(jnp.sum(do * o_ref[:, hs], axis=-1, keepdims=True))
            lses.append(lse_ref[:, hs.start:hs.start + 1])

        def step(j, dqs, masked):
            rows = pl.ds(pl.multiple_of(j * tq, tq), tq)
            out = []
            for hh, hs in enumerate(heads):
                kj, vj = k_ref[rows, hs], v_ref[rows, hs]
                s = _dot_nt(qs[hh], kj)
                if masked:
                    s = jnp.where(mask, s, NEG)
                p = jnp.exp(s - lses[hh])
                ds = (p * (_dot_nt(dobs[hh], vj) - dds[hh])).astype(BF)
                dk_ref[rows, hs] += _dot_tn(ds, qs[hh])
                dv_ref[rows, hs] += _dot_tn(p.astype(BF), dobs[hh])
                out.append(dqs[hh] + _dot(ds, kj))
            return tuple(out)

        dqs = lax.fori_loop(0, i // 2, lambda t, c: step(2 * t + 1, step(2 * t, c, False), False),
                            (jnp.zeros((tq, 128), F32),) * 2)
        dqs = lax.cond(i % 2 == 1, lambda c: step(i - 1, c, False), lambda c: c, dqs)
        dqs = step(i, dqs, True)
        for hh, hs in enumerate(heads):
            dq_ref[:, hs] = dqs[hh]

    blk = pl.BlockSpec((tq, 256), lambda h, i: (i, h))
    full = pl.BlockSpec((T, 256), lambda h, i: (0, h), pipeline_mode=pl.Buffered(1))
    return pl.pallas_call(
        body, name="mla_bwd", grid=(2, T // tq), in_specs=[blk, full, full, blk, blk, blk],
        out_specs=[blk, full, full],
        out_shape=[jax.ShapeDtypeStruct((T, 512), F32)] * 3,
        compiler_params=_params(2))(q, k, v, o, lse, do)


def _sb_tile(qk, rr, strict, masked, upper):
    z2 = qk * (SB_SCALE * LOG2E)
    l1 = jnp.log2(1.0 + jnp.exp2(-jnp.abs(z2)))
    lk = -jnp.maximum(z2, 0.0) - l1
    if masked:
        lk = jnp.where(strict, lk, 0.0)
    after = rr + _dot(lk.astype(BF), upper)
    ll = jnp.minimum(z2, 0.0) - l1
    a = jnp.exp2(ll + after)
    if masked:
        a = jnp.where(strict, a, 0.0)
    return ll, a, jnp.sum(lk, axis=-1, keepdims=True)


SB_TQ, SB_TK = 256, 256
SB_DEAD = -160.0


def _sb_walk(trips, one_step, carry):
    def alive(c):
        t, cr = c
        top = functools.reduce(jnp.maximum, [jnp.max(h[0]) for h in cr])
        return jnp.logical_and(t < trips, top > SB_DEAD)

    def body(c):
        t, cr = c
        return t + 1, one_step(t, cr)

    return lax.while_loop(alive, body, (jnp.int32(0), carry))[1]


def _sb_consts(tq, tk):
    row, col = _iota((tq, tk), 0), _iota((tq, tk), 1)
    strict = [col + d * tk < row for d in range(tq // tk)]
    r2, c2 = _iota((tk, tk), 0), _iota((tk, tk), 1)
    return strict, (r2 > c2).astype(BF), (r2 < c2).astype(BF)


def _sb_fwd(h16):
    T = h16.shape[0]
    tq, tk = SB_TQ, SB_TK
    nd = tq // tk

    def body(q0_ref, q1_ref, k0_ref, k1_ref, v0_ref, v1_ref, o_ref):
        i = pl.program_id(0)
        strict, upper, _ = _sb_consts(tq, tk)
        lane = _iota((tq, 128), 1)
        pairs = [slice(0, 128), slice(128, 256)]
        k_refs, v_refs = (k0_ref, k1_ref), (v0_ref, v1_ref)
        qms = []
        for q_ref in (q0_ref, q1_ref):
            q2 = q_ref[...]
            qms += [jnp.where(lane < 64, q2, jnp.zeros_like(q2)), jnp.where(lane >= 64, q2, jnp.zeros_like(q2))]

        def step(j, carry, d):
            rows = pl.ds(pl.multiple_of(j * tk, tk), tk)
            out = []
            for h in range(4):
                rr, acc = carry[h]
                _, a, rs = _sb_tile(_dot_nt(qms[h], k_refs[h // 2][rows, :]), rr, None if d is None else strict[d],
                                    d is not None, upper)
                out.append((rr + rs, acc + _dot(a.astype(BF), v_refs[h // 2][rows, :])))
            return tuple(out)

        carry = ((jnp.zeros((tq, 1), F32), jnp.zeros((tq, 128), F32)),) * 4
        for d in reversed(range(nd)):
            carry = step(nd * i + d, carry, d)
        carry = _sb_walk(nd * i, lambda t, c: step(nd * i - 1 - t, c, None), carry)
        for p, ps in enumerate(pairs):
            o_ref[:, ps] = jnp.where(lane < 64, carry[2 * p][1], carry[2 * p + 1][1])

    return pl.pallas_call(
        body, name="sb_fwd", grid=(T // tq,),
        in_specs=[pl.BlockSpec((tq, 128), lambda i: (i, C_DQ // 128)),
                  pl.BlockSpec((tq, 128), lambda i: (i, C_DQ // 128 + 1)),
                  pl.BlockSpec((T, 128), lambda i: (0, C_DK // 128)),
                  pl.BlockSpec((T, 128), lambda i: (0, C_DK // 128 + 1)),
                  pl.BlockSpec((T, 128), lambda i: (0, C_DV // 128)),
                  pl.BlockSpec((T, 128), lambda i: (0, C_DV // 128 + 1))],
        out_specs=pl.BlockSpec((tq, 256), lambda i: (i, 0)),
        out_shape=jax.ShapeDtypeStruct((T, 256), F32),
        compiler_params=_params(1))(h16, h16, h16, h16, h16, h16)


def _sb_bwd(h16, yd, dyd):
    T = h16.shape[0]
    tq, tk = SB_TQ, SB_TK
    nd = tq // tk

    def body(q0_ref, q1_ref, k0_ref, k1_ref, v0_ref, v1_ref, o_ref, do_ref, dq_ref, dk_ref, dv_ref):
        i = pl.program_id(0)

        @pl.when(i == 0)
        def _():
            dk_ref[...] = jnp.zeros_like(dk_ref)
            dv_ref[...] = jnp.zeros_like(dv_ref)

        strict, upper, before = _sb_consts(tq, tk)
        lane = _iota((tq, 128), 1)
        lane_k = _iota((tk, 128), 1)
        pairs = [slice(0, 128), slice(128, 256)]
        k_refs, v_refs = (k0_ref, k1_ref), (v0_ref, v1_ref)
        q2s, dob2s, qms, doms, dds = [], [], [], [], []
        for p, q_ref in enumerate((q0_ref, q1_ref)):
            q2 = q_ref[...]
            dob2 = do_ref[:, pairs[p]].astype(BF)
            doo = dob2.astype(F32) * o_ref[:, pairs[p]]
            q2s.append(q2)
            dob2s.append(dob2)
            for mine in (lane < 64, lane >= 64):
                qms.append(jnp.where(mine, q2, jnp.zeros_like(q2)))
                doms.append(jnp.where(mine, dob2, jnp.zeros_like(dob2)))
                dds.append(jnp.sum(jnp.where(mine, doo, 0.0), axis=-1, keepdims=True))

        def step(j, carry, d):
            rows = pl.ds(pl.multiple_of(j * tk, tk), tk)
            out, dks, dvs = [], [], []
            for h in range(4):
                kj, vj = k_refs[h // 2][rows, :], v_refs[h // 2][rows, :]
                rr, sg, dq = carry[h]
                ll, a, rs = _sb_tile(_dot_nt(qms[h], kj), rr, None if d is None else strict[d], d is not None, upper)
                ab = a.astype(BF)
                g = _dot_nt(doms[h], vj) * ab.astype(F32)
                gs = jnp.sum(g, axis=-1, keepdims=True)
                pre = (dds[h] - sg - gs) + _dot(g.astype(BF), before)
                dz = g - jnp.exp2(ll) * (g + pre)
                if d is not None:
                    dz = jnp.where(strict[d], dz, 0.0)
                dzb = dz.astype(BF)
                dks.append(_dot_tn(dzb, q2s[h // 2]))
                dvs.append(_dot_tn(ab, dob2s[h // 2]))
                out.append((rr + rs, sg + gs, dq + _dot(dzb, kj)))
            for p, ps in enumerate(pairs):
                dk_ref[rows, ps] += jnp.where(lane_k < 64, dks[2 * p], dks[2 * p + 1]) * SB_SCALE
                dv_ref[rows, ps] += jnp.where(lane_k < 64, dvs[2 * p], dvs[2 * p + 1])
            return tuple(out)

        zero = jnp.zeros((tq, 1), F32)
        carry = ((zero, zero, jnp.zeros((tq, 128), F32)),) * 4
        for d in reversed(range(nd)):
            carry = step(nd * i + d, carry, d)
        carry = _sb_walk(nd * i, lambda t, c: step(nd * i - 1 - t, c, None), carry)
        for p, ps in enumerate(pairs):
            dq_ref[:, ps] = jnp.where(lane < 64, carry[2 * p][2], carry[2 * p + 1][2]) * SB_SCALE

    blk = lambda c: pl.BlockSpec((tq, 128), lambda i: (i, c // 128))
    full = lambda c: pl.BlockSpec((T, 128), lambda i: (0, c // 128))
    row = pl.BlockSpec((tq, 256), lambda i: (i, 0))
    acc = pl.BlockSpec((T, 256), lambda i: (0, 0))
    return pl.pallas_call(
        body, name="sb_bwd", grid=(T // tq,),
        in_specs=[blk(C_DQ), blk(C_DQ + 128), full(C_DK), full(C_DK + 128), full(C_DV), full(C_DV + 128), row, row],
        out_specs=[row, acc, acc],
        out_shape=[jax.ShapeDtypeStruct((T, 256), F32)] * 3,
        compiler_params=_params(1))(h16, h16, h16, h16, h16, h16, yd, dyd)


def _compact_c(ycp):
    return jnp.concatenate([ycp[:, h * 128:h * 128 + 64] for h in range(4)], axis=1)


def _post_fwd(ya, yb, ycp, yd, h32, ggrp, wout, gpost, x, tgt=None):
    T = x.shape[0]
    tm = 512 if T % 512 == 0 else 256
    last = tgt is not None

    def body(*refs):
        ya_ref, yb_ref, yc_ref, yd_ref, gate_ref, gg_ref, w_ref, gp_ref, x_ref = refs[:9]
        if last:
            t_ref, xn_ref, o_ref, sq_ref = refs[9:]
        else:
            xn_ref, o_ref = refs[9:]
        ys = [ya_ref[...], yb_ref[...], _compact_c(yc_ref[...]), yd_ref[...]]
        gate = gate_ref[...]
        sil = gate * (1.0 / (1.0 + jnp.exp(-gate)))
        parts = []
        for gi in range(4):
            ng, _ = _rms_fwd(ys[gi], gg_ref[:, gi * 256:(gi + 1) * 256])
            parts.append(ng * sil[:, gi * 256:(gi + 1) * 256])
        o = _dot(jnp.concatenate(parts, axis=1).astype(BF), w_ref[...])
        o_ref[...] = o
        on, _ = _rms_fwd(o, gp_ref[...])
        if last:
            @pl.when(pl.program_id(0) == 0)
            def _():
                sq_ref[...] = jnp.zeros_like(sq_ref)

            d = (x_ref[...] + on) - t_ref[...]
            sq_ref[...] += jnp.sum(d * d, axis=0, keepdims=True)
            xn_ref[...] = d * (1.0 / D_MODEL)
        else:
            xn_ref[...] = x_ref[...] + on

    row = lambda w: pl.BlockSpec((tm, w), lambda n: (n, 0))
    vec = pl.BlockSpec((1, 1024), lambda n: (0, 0))
    return pl.pallas_call(
        body, name="post_fwd", grid=(T // tm,),
        in_specs=[row(256), row(256), row(512), row(256), pl.BlockSpec((tm, 1024), lambda n: (n, C_GATE // 1024)),
                  vec, pl.BlockSpec((1024, 1024), lambda n: (0, 0)), vec, row(1024)] + ([row(1024)] if last else []),
        out_specs=[row(1024), row(1024)] + ([vec] if last else []),
        out_shape=[jax.ShapeDtypeStruct((T, 1024), F32), jax.ShapeDtypeStruct((T, 1024), F32)]
        + ([jax.ShapeDtypeStruct((1, 1024), F32)] if last else []),
        compiler_params=_params(1))(*([ya, yb, ycp, yd, h32, ggrp, wout, gpost, x] + ([tgt] if last else [])))


def _post_bwd(dx, o, gpost, woutt, ya, yb, ycp, yd, h32, ggrp):
    T = dx.shape[0]
    tm = 256

    def body(dx_ref, o_ref, gp_ref, w_ref, ya_ref, yb_ref, yc_ref, yd_ref, gate_ref, gg_ref,
             dw_ref, dya_ref, dyb_ref, dyc_ref, dyd_ref, dgate_ref, dgp_ref, dgg_ref):
        n = pl.program_id(0)

        @pl.when(n == 0)
        def _():
            dw_ref[...] = jnp.zeros_like(dw_ref)
            dgp_ref[...] = jnp.zeros_like(dgp_ref)
            dgg_ref[...] = jnp.zeros_like(dgg_ref)

        ov, gp = o_ref[...], gp_ref[...]
        _, ro = _rms_fwd(ov, gp)
        do, dgt = _rms_bwd(ov, gp, ro, dx_ref[...])
        dgp_ref[...] += jnp.sum(dgt, axis=0, keepdims=True)
        dob = do.astype(BF)
        gate = gate_ref[...]
        sg = 1.0 / (1.0 + jnp.exp(-gate))
        sil = gate * sg
        dsil = sg * (1.0 + gate * (1.0 - sg))
        ys = [ya_ref[...], yb_ref[...], _compact_c(yc_ref[...]), yd_ref[...]]
        normed = [_rms_fwd(ys[gi], gg_ref[:, gi * 256:(gi + 1) * 256]) for gi in range(4)]
        ym = jnp.concatenate([normed[gi][0] * sil[:, gi * 256:(gi + 1) * 256] for gi in range(4)], axis=1).astype(BF)
        dw_ref[...] += _dot_tn(ym, dob)
        dym = _dot(dob, w_ref[...])
        dys = []
        for gi in range(4):
            gs = slice(gi * 256, (gi + 1) * 256)
            gg = gg_ref[:, gs]
            ng, rg = normed[gi]
            dgate_ref[:, gs] = (dym[:, gs] * ng * dsil[:, gs]).astype(BF)
            dy, dgt2 = _rms_bwd(ys[gi], gg, rg, dym[:, gs] * sil[:, gs])
            dgg_ref[:, gs] += jnp.sum(dgt2, axis=0, keepdims=True)
            dys.append(dy)
        dya_ref[...] = dys[0]
        dyb_ref[...] = dys[1]
        dyd_ref[...] = dys[3]
        z64 = jnp.zeros((tm, 64), F32)
        dyc_ref[...] = jnp.concatenate(
            [piece for h in range(4) for piece in (dys[2][:, h * 64:(h + 1) * 64], z64)], axis=1)

    row = lambda w: pl.BlockSpec((tm, w), lambda n: (n, 0))
    vec = pl.BlockSpec((1, 1024), lambda n: (0, 0))
    return pl.pallas_call(
        body, name="post_bwd", grid=(T // tm,),
        in_specs=[row(1024), row(1024), vec, pl.BlockSpec((1024, 1024), lambda n: (0, 0)),
                  row(256), row(256), row(512), row(256),
                  pl.BlockSpec((tm, 1024), lambda n: (n, C_GATE // 1024)), vec],
        out_specs=[pl.BlockSpec((1024, 1024), lambda n: (0, 0)), row(256), row(256), row(512), row(256), row(1024),
                   vec, vec],
        out_shape=[jax.ShapeDtypeStruct((1024, 1024), F32), jax.ShapeDtypeStruct((T, 256), F32),
                   jax.ShapeDtypeStruct((T, 256), F32), jax.ShapeDtypeStruct((T, 512), F32),
                   jax.ShapeDtypeStruct((T, 256), F32), jax.ShapeDtypeStruct((T, 1024), BF),
                   jax.ShapeDtypeStruct((1, 1024), F32), jax.ShapeDtypeStruct((1, 1024), F32)],
        compiler_params=_params(1))(dx, o, gpost, woutt, ya, yb, ycp, yd, h32, ggrp)


def _swap_rows32(a):
    return jnp.concatenate([a[16:32], a[0:16]], axis=0)


def _pad_w_uq(w):
    z = lambda n: jnp.zeros((w.shape[0], n), w.dtype)
    a = [p for h in range(4) for p in (w[:, 96 * h:96 * h + 96], z(32))]
    b = [p for h in range(4) for p in (z(64), _swap32(w[:, 96 * h + 64:96 * h + 96]), z(32))]
    return jnp.concatenate(a + b, axis=1)


def _unpad_w_uq(d):
    out = []
    for h in range(4):
        out.append(d[:, 128 * h:128 * h + 64])
        out.append(d[:, 128 * h + 64:128 * h + 96] + _swap32(d[:, 512 + 128 * h + 64:512 + 128 * h + 96]))
    return jnp.concatenate(out, axis=1)


def _pad_w_ukv(w):
    z = jnp.zeros((w.shape[0], 64), w.dtype)
    a = [p for h in range(4) for p in (w[:, 128 * h:128 * h + 64], z)]
    b = [p for h in range(4) for p in (w[:, 128 * h + 64:128 * h + 128], z)]
    return jnp.concatenate(a + b, axis=1)


def _unpad_w_ukv(d):
    return jnp.concatenate([p for h in range(4) for p in (d[:, 128 * h:128 * h + 64],
                                                          d[:, 512 + 128 * h:512 + 128 * h + 64])], axis=1)


def _rope_tables(pos):
    freqs = 10000.0 ** (-jnp.arange(16, dtype=F32) / 16)
    ang = pos.astype(F32)[:, None] * freqs
    c, s = jnp.cos(ang), jnp.sin(ang)
    z = lambda n: jnp.zeros((pos.shape[0], n), F32)
    return (jnp.concatenate([z(64), c, c, z(32)], axis=1), jnp.concatenate([z(64), -s, s, z(32)], axis=1))


def _layer_weights(W, l):
    wuq2 = _pad_w_uq(W["mla_w_uq"][l])
    wkv2 = _pad_w_ukv(W["mla_w_ukv"][l])
    wout = W["w_out"][l]
    cw = jnp.concatenate([W["conv_w"][l].astype(F32), jnp.zeros((5, 256), F32)], axis=0)
    return dict(
        wpt=W["wpt"][l], wuq2=wuq2.astype(BF), wuq2t=wuq2.T.astype(BF),
        wkv2=wkv2.astype(BF), wkv2t=wkv2.T.astype(BF), wout=wout.astype(BF), woutt=wout.T.astype(BF),
        cw=cw, cb=W["conv_b"][l][None, :], sinks=W["attn_sinks"][l],
        gpre=W["norm_pre"][l][None, :], gq=W["mla_q_norm"][l][None, :], gkv=W["mla_kv_norm"][l][None, :],
        ggrp=W["group_norm"][l][None, :], gpost=W["norm_post"][l][None, :])


def _local_step(x, pos, W, tgt):
    cosk, sin = _rope_tables(pos)
    saved = []
    for l in range(DEPTH):
        lw = _layer_weights(W, l)
        h32, h16, xn = _inproj_fwd(x, lw["gpre"], lw["wpt"])
        ya = _swa_fwd(h16, lw["sinks"])
        yb = _conv_fwd(h32, lw["cw"], lw["cb"])
        qc, kc, vc = _cprep_fwd(h32, lw["gq"], lw["gkv"], lw["wuq2"], lw["wkv2"], cosk, sin)
        ycp, lse = _mla_fwd(qc, kc, vc)
        yd = _sb_fwd(h16)
        if l < DEPTH - 1:
            x_new, o = _post_fwd(ya, yb, ycp, yd, h32, lw["ggrp"], lw["wout"], lw["gpost"], x)
        else:
            dx, o, sq = _post_fwd(ya, yb, ycp, yd, h32, lw["ggrp"], lw["wout"], lw["gpost"], x, tgt)
        saved.append(dict(lw=lw, x=x, h32=h32, h16=h16, xn=xn, ya=ya, yb=yb, qc=qc, kc=kc, vc=vc, ycp=ycp,
                          lse=lse, yd=yd, o=o))
        if l < DEPTH - 1:
            x = x_new

    grads = {k: [None] * DEPTH for k in ("norm_pre", "w_in_pt", "attn_sinks", "conv_w", "conv_b", "mla_q_norm",
                                         "mla_w_uq", "mla_kv_norm", "mla_w_ukv", "group_norm", "w_out",
                                         "norm_post")}
    for l in reversed(range(DEPTH)):
        s = saved[l]
        lw = s["lw"]
        dwout, dya, dyb, dycp, dyd, dgate, dgpost, dggrp = _post_bwd(
            dx, s["o"], lw["gpost"], lw["woutt"], s["ya"], s["yb"], s["ycp"], s["yd"], s["h32"], lw["ggrp"])
        grads["norm_post"][l] = dgpost[0]
        grads["group_norm"][l] = dggrp[0]
        grads["w_out"][l] = dwout
        sdq, sdk, sdv = _sb_bwd(s["h16"], s["yd"], dyd)
        mdq, mdk, mdv = _mla_bwd(s["qc"], s["kc"], s["vc"], s["ycp"], s["lse"], dycp)
        dcq, dckv, dckr, dckrs, dwuq2, dwkv2, dgq, dgkv = _cprep_bwd(
            s["h32"], lw["gq"], lw["gkv"], lw["wuq2t"], lw["wkv2t"], cosk, sin, mdq, mdk, mdv)
        grads["mla_q_norm"][l] = dgq[0]
        grads["mla_kv_norm"][l] = dgkv[0]
        grads["mla_w_uq"][l] = _unpad_w_uq(dwuq2)
        grads["mla_w_ukv"][l] = _unpad_w_ukv(dwkv2)
        dbb, dbc, dbx, dcw = _conv_bwd(s["h32"], lw["cw"], lw["cb"], dyb)
        grads["conv_w"][l] = dcw[0:3]
        grads["conv_b"][l] = dcw[3]
        adq, adk, adv, dsk = _swa_bwd(s["h16"], lw["sinks"], dya)
        grads["attn_sinks"][l] = dsk[0, 0:4]
        parts = [dgate, dbb, dbc, dbx, dcq, dckv, dckr, dckrs, adq, adk, adv, sdq, sdk, sdv]
        dwp, dx, dgpre = _inproj_bwd(parts, lw["wpt"], s["x"], s["xn"], lw["gpre"], dx)
        grads["w_in_pt"][l] = dwp.T
        grads["norm_pre"][l] = dgpre[0]
    return sq, dx, grads


SMALL_SHARDED = ("conv_w", "mla_w_uq", "mla_w_ukv")
REPLICATED = ("norm_pre", "attn_sinks", "conv_b", "mla_q_norm", "mla_kv_norm", "group_norm", "norm_post")
ORDER = ("norm_pre", "w_in", "attn_sinks", "conv_w", "conv_b", "mla_q_norm", "mla_w_uq", "mla_kv_norm",
         "mla_w_ukv", "group_norm", "w_out", "norm_post")
W_IN_COLS = 436
W_IN_WIN = 440
SMALL_ROWS = 48


def _pack_small(arrs, dtype):
    flat = jnp.concatenate([a.reshape(-1).astype(dtype) for a in arrs])
    flat = jnp.concatenate([flat, jnp.zeros((SMALL_ROWS * D_MODEL - flat.shape[0],), dtype)])
    return flat.reshape(SMALL_ROWS, D_MODEL)


TAIL_ROW0 = DEPTH * W_IN_WIN


def _pack_state(ps):
    k = len(ps)
    wout = jnp.stack([p["w_out"] for p in ps]).reshape(k, DEPTH * 128, D_MODEL)
    flat = jnp.stack([jnp.concatenate([p[n].reshape(-1) for n in SMALL_SHARDED + REPLICATED]) for p in ps])
    small = jnp.pad(flat, ((0, 0), (0, SMALL_ROWS * D_MODEL - flat.shape[1]))).reshape(k, SMALL_ROWS, D_MODEL)
    return jnp.concatenate([wout, small], axis=1)


def _unpack_state(buf, p):
    k = buf.shape[0]
    out = {"w_out": buf[:, 0:DEPTH * 128].reshape(k, DEPTH, 128, D_MODEL)}
    flat = buf[:, DEPTH * 128:].reshape(k, SMALL_ROWS * D_MODEL)
    off = 0
    for n in SMALL_SHARDED + REPLICATED:
        size = int(np.prod(p[n].shape))
        out[n] = flat[:, off:off + size].reshape((k,) + p[n].shape)
        off += size
    return out


def _rows_of_w_in_t(lo, hi, padded, kr):
    segs = ((0, 512, padded, NF + C_AQ), (512, 1664, padded, C_BB), (1664, 1696, kr, 0),
            (1696, 2464, padded, NF + C_DQ), (2464, 3488, padded, C_GATE))
    out = []
    for s0, s1, src, base in segs:
        a, b = max(lo, s0), min(hi, s1)
        if a < b:
            out.append(src[base + a - s0:base + b - s0])
    return out


def _me():
    return lax.axis_index("x"), lax.axis_index("y"), lax.axis_index("c")


def _all_gather(block):
    R, C = block.shape

    def body(src_ref, out_ref, send_sems, recv_sems, local_sem):
        x, y, c = _me()
        me, sibling = (x, y, c), (x, y, 1 - c)
        chips = [(1 - x, y), (x, 1 - y), (1 - x, 1 - y)]

        def slot(px, py, pc):
            return out_ref.at[4 * px + 2 * py + pc]

        def copy(k, block, to, src=None):
            return pltpu.make_async_remote_copy(
                src_ref=slot(*block) if src is None else src, dst_ref=slot(*block), send_sem=send_sems.at[k],
                recv_sem=recv_sems.at[k], device_id=to, device_id_type=MESH)

        mine = pltpu.make_async_copy(src_ref, slot(*me), local_sem)
        mine.start()
        first = [copy(0, me, sibling, src=src_ref)]
        first += [copy(1 + j, me, (*chip, c), src=src_ref) for j, chip in enumerate(chips)]
        for cp in first:
            cp.start()
        passed = [copy(4 + j, (*chip, c), sibling) for j, chip in enumerate(chips)]
        for j, chip in enumerate(chips):
            copy(1 + j, (*chip, c), me).wait_recv()
            passed[j].start()
        copy(0, sibling, me).wait_recv()
        for j, chip in enumerate(chips):
            copy(4 + j, (*chip, 1 - c), me).wait_recv()
        for cp in first + passed:
            cp.wait_send()
        mine.wait()

    return pl.pallas_call(
        body, name="all_gather", out_shape=jax.ShapeDtypeStruct((N_DEV, R, C), block.dtype),
        in_specs=[pl.BlockSpec(memory_space=pl.ANY)], out_specs=pl.BlockSpec(memory_space=pl.ANY),
        scratch_shapes=[pltpu.SemaphoreType.DMA((N_DEV - 1,)), pltpu.SemaphoreType.DMA((N_DEV - 1,)),
                        pltpu.SemaphoreType.DMA])(block)


N_CHIP = 4


def _sibling_swap(blocks):
    _, R, C = blocks.shape

    def body(src_ref, out_ref, send_sems, recv_sems):
        x, y, c = _me()
        copies = [pltpu.make_async_remote_copy(
            src_ref=src_ref.at[2 * j + 1 - c], dst_ref=out_ref.at[j], send_sem=send_sems.at[j],
            recv_sem=recv_sems.at[j], device_id=(x, y, 1 - c), device_id_type=MESH) for j in range(N_CHIP)]
        for cp in copies:
            cp.start()
        for cp in copies:
            cp.wait()

    return pl.pallas_call(
        body, name="sibling_swap", out_shape=jax.ShapeDtypeStruct((N_CHIP, R, C), blocks.dtype),
        in_specs=[pl.BlockSpec(memory_space=pl.ANY)], out_specs=pl.BlockSpec(memory_space=pl.ANY),
        scratch_shapes=[pltpu.SemaphoreType.DMA((N_CHIP,)), pltpu.SemaphoreType.DMA((N_CHIP,))])(blocks)


def _pair_sum(a, b):
    n, R, C = a.shape
    tr = 592 if R % 592 == 0 else R

    def body(a_ref, b_ref, o_ref):
        o_ref[...] = (a_ref[...].astype(F32) + b_ref[...].astype(F32)).astype(BF)

    spec = pl.BlockSpec((1, tr, C), lambda j, r: (j, r, 0))
    return pl.pallas_call(body, name="pair_sum", grid=(n, R // tr), in_specs=[spec, spec], out_specs=spec,
                          out_shape=jax.ShapeDtypeStruct(a.shape, BF), compiler_params=_params(2))(a, b)


def _chip_exchange(sums):
    _, R, C = sums.shape

    def body(src_ref, out_ref, send_sems, recv_sems, local_sem):
        x, y, c = _me()
        here = 2 * x + y
        mine = pltpu.make_async_copy(src_ref.at[here], out_ref.at[here], local_sem)
        mine.start()
        copies = []
        for k in range(1, N_CHIP):
            px, py = x ^ (k >> 1), y ^ (k & 1)
            copies.append(pltpu.make_async_remote_copy(
                src_ref=src_ref.at[2 * px + py], dst_ref=out_ref.at[here], send_sem=send_sems.at[k - 1],
                recv_sem=recv_sems.at[k - 1], device_id=(px, py, c), device_id_type=MESH))
        for cp in copies:
            cp.start()
        for cp in copies:
            cp.wait()
        mine.wait()

    return pl.pallas_call(
        body, name="chip_exchange", out_shape=jax.ShapeDtypeStruct((N_CHIP, R, C), sums.dtype),
        in_specs=[pl.BlockSpec(memory_space=pl.ANY)], out_specs=pl.BlockSpec(memory_space=pl.ANY),
        scratch_shapes=[pltpu.SemaphoreType.DMA((N_CHIP - 1,)), pltpu.SemaphoreType.DMA((N_CHIP - 1,)),
                        pltpu.SemaphoreType.DMA])(sums)


def _adamw_update(g, w, m, v):
    m_ = ADAM_B1 * m + (1.0 - ADAM_B1) * g
    v_ = ADAM_B2 * v + (1.0 - ADAM_B2) * (g * g)
    m_hat = m_ / (1.0 - ADAM_B1 ** ADAM_STEP)
    v_hat = v_ / (1.0 - ADAM_B2 ** ADAM_STEP)
    return -ADAM_LR * (m_hat / (jnp.sqrt(v_hat) + ADAM_EPS) + ADAM_WD * w), m_, v_


def _adamw(parts, state):
    _, R, C = state.shape
    n_parts = parts.shape[0]
    tr = 16
    assert R % tr == 0 and TAIL_ROW0 % tr == 0

    def body(p_ref, s_ref, o_ref):
        g = p_ref[0].astype(F32)
        for k in range(1, n_parts):
            g = g + p_ref[k].astype(F32)
        o_ref[0] = g
        o_ref[1], o_ref[2], o_ref[3] = _adamw_update(g, s_ref[0], s_ref[1], s_ref[2])

    return pl.pallas_call(
        body, name="adamw", grid=(R // tr,),
        in_specs=[pl.BlockSpec((n_parts, tr, C), lambda n: (0, n + TAIL_ROW0 // tr, 0)),
                  pl.BlockSpec((3, tr, C), lambda n: (0, n, 0))],
        out_specs=pl.BlockSpec((4, tr, C), lambda n: (0, n, 0)), out_shape=jax.ShapeDtypeStruct((4, R, C), F32),
        compiler_params=_params(1))(parts, state)


def _adamw_w_in(parts, w, m, v, core):
    n_parts = parts.shape[0]
    tc = 256

    def body(core_ref, p_ref, w_ref, m_ref, v_ref, o_ref):
        g_t = p_ref[0].astype(F32)
        for k in range(1, n_parts):
            g_t = g_t + p_ref[k].astype(F32)
        g_t = jnp.concatenate([g_t, jnp.zeros((512 - W_IN_WIN, tc), F32)], axis=0).T
        g = jnp.where(core_ref[0] == 0, g_t[:, 0:W_IN_COLS], g_t[:, W_IN_WIN - W_IN_COLS:W_IN_WIN])
        o_ref[0, 0] = g
        o_ref[1, 0], o_ref[2, 0], o_ref[3, 0] = _adamw_update(g, w_ref[0], m_ref[0], v_ref[0])

    nat = pl.BlockSpec((1, tc, W_IN_COLS), lambda l, j: (l, j, 0))
    return pl.pallas_call(
        body, name="adamw_w_in", grid=(DEPTH, D_MODEL // tc),
        in_specs=[pl.BlockSpec(memory_space=pltpu.SMEM),
                  pl.BlockSpec((n_parts, W_IN_WIN, tc), lambda l, j: (0, l, j)), nat, nat, nat],
        out_specs=pl.BlockSpec((4, 1, tc, W_IN_COLS), lambda l, j: (0, l, j, 0)),
        out_shape=jax.ShapeDtypeStruct((4, DEPTH, D_MODEL, W_IN_COLS), F32),
        compiler_params=_params(2))(core, parts, w, m, v)


def kernel(x, positions, norm_pre, w_in, attn_sinks, conv_w, conv_b, mla_q_norm, mla_w_uq, mla_kv_norm, mla_w_ukv, group_norm, w_out, norm_post, loss_target, m_norm_pre, m_w_in, m_attn_sinks, m_conv_w, m_conv_b, m_mla_q_norm, m_mla_w_uq, m_mla_kv_norm, m_mla_w_ukv, m_group_norm, m_w_out, m_norm_post, v_norm_pre, v_w_in, v_attn_sinks, v_conv_w, v_conv_b, v_mla_q_norm, v_mla_w_uq, v_mla_kv_norm, v_mla_w_ukv, v_group_norm, v_w_out, v_norm_post):
    local = dict(norm_pre=norm_pre, w_in=w_in, attn_sinks=attn_sinks, conv_w=conv_w, conv_b=conv_b,
                 mla_q_norm=mla_q_norm, mla_w_uq=mla_w_uq, mla_kv_norm=mla_kv_norm, mla_w_ukv=mla_w_ukv,
                 group_norm=group_norm, w_out=w_out, norm_post=norm_post)
    mom = dict(norm_pre=m_norm_pre, w_in=m_w_in, attn_sinks=m_attn_sinks, conv_w=m_conv_w, conv_b=m_conv_b,
               mla_q_norm=m_mla_q_norm, mla_w_uq=m_mla_w_uq, mla_kv_norm=m_mla_kv_norm, mla_w_ukv=m_mla_w_ukv,
               group_norm=m_group_norm, w_out=m_w_out, norm_post=m_norm_post)
    vel = dict(norm_pre=v_norm_pre, w_in=v_w_in, attn_sinks=v_attn_sinks, conv_w=v_conv_w, conv_b=v_conv_b,
               mla_q_norm=v_mla_q_norm, mla_w_uq=v_mla_w_uq, mla_kv_norm=v_mla_kv_norm, mla_w_ukv=v_mla_w_ukv,
               group_norm=v_group_norm, w_out=v_w_out, norm_post=v_norm_post)

    c = lax.axis_index("c")

    tile = 16
    slot_rows = 464
    shift = 8 * lax.axis_index("y") + 4 * c
    wt = lax.dynamic_update_slice(jnp.zeros((DEPTH, slot_rows, D_MODEL), BF),
                                  jnp.transpose(w_in, (0, 2, 1)).astype(BF), (0, shift, 0))
    payload = jnp.concatenate([wt.reshape(DEPTH * slot_rows, D_MODEL),
                               w_out.astype(BF).reshape(DEPTH * 128, D_MODEL),
                               _pack_small([local[n] for n in SMALL_SHARDED], BF)], axis=0)
    gathered = _all_gather(payload)
    W = {n: local[n] for n in REPLICATED}

    def nat_rows(l, lo, hi):
        def piece(d, r0, r1):
            base = slot_rows * l - (W_IN_COLS * d) // tile * tile
            return gathered[d, base + r0:base + r1]

        out, run = [], None
        for r0 in range(lo, hi, tile):
            d0, d1 = r0 // W_IN_COLS, (r0 + tile - 1) // W_IN_COLS
            if d0 == d1 and run is not None and run[0] == d0:
                run = (d0, run[1], r0 + tile)
                continue
            if run is not None:
                out.append(piece(*run))
                run = None
            if d0 == d1:
                run = (d0, r0, r0 + tile)
            else:
                out.append(piece(d0, r0, r0 + tile) + piece(d1, r0, r0 + tile))
        if run is not None:
            out.append(piece(*run))
        return out

    z = lambda n: [jnp.zeros((n, D_MODEL), BF)]
    W["wpt"] = [jnp.concatenate(nat_rows(l, 2464, 3488) + nat_rows(l, 512, 1664) + z(64) + nat_rows(l, 1664, 1696)
                                + z(96) + nat_rows(l, 1680, 1696) + nat_rows(l, 1664, 1680) + z(32)
                                + nat_rows(l, 0, 512) + nat_rows(l, 1696, 2464), axis=0) for l in range(DEPTH)]
    wo0 = DEPTH * slot_rows
    W["w_out"] = gathered[:, wo0:wo0 + DEPTH * 128].reshape(N_DEV, DEPTH, 128, D_MODEL).transpose(1, 0, 2, 3).reshape(
        DEPTH, D_MODEL, D_MODEL)
    flat = gathered[:, wo0 + DEPTH * 128:].reshape(N_DEV, SMALL_ROWS * D_MODEL)
    off = 0
    for n in SMALL_SHARDED:
        depth, rows, width = local[n].shape
        size = depth * rows * width
        W[n] = flat[:, off:off + size].reshape(N_DEV, depth, rows, width).transpose(1, 2, 0, 3).reshape(
            depth, rows, N_DEV * width)
        off += size

    sq, grad_x, g = _local_step(x[0], positions[0], W, loss_target[0])
    loss = lax.psum(0.5 / D_MODEL * jnp.sum(sq), ("x", "y", "c"))

    cols = []
    for n in SMALL_SHARDED:
        depth, rows, width = local[n].shape
        cols.append(jnp.stack(g[n]).reshape(depth, rows, N_DEV, width).transpose(2, 0, 1, 3).reshape(N_DEV, -1))
    rep = jnp.concatenate([a.reshape(-1) for n in REPLICATED for a in g[n]])
    cols.append(jnp.broadcast_to(rep[None], (N_DEV, rep.shape[0])))
    small = jnp.concatenate(cols, axis=1)
    small = jnp.pad(small, ((0, 0), (0, SMALL_ROWS * D_MODEL - small.shape[1]))).reshape(N_DEV, SMALL_ROWS, D_MODEL)
    krs = [p[C_CKR + 64:C_CKR + 96] + _swap_rows32(p[C_CKRS + 64:C_CKRS + 96]) for p in g["w_in_pt"]]
    pieces = []
    for d in range(N_DEV):
        lo = W_IN_COLS * d // 8 * 8
        for l in range(DEPTH):
            pieces += _rows_of_w_in_t(lo, lo + W_IN_WIN, g["w_in_pt"][l], krs[l])
        pieces += [g["w_out"][l][128 * d:128 * (d + 1)] for l in range(DEPTH)]
        pieces.append(small[d])
    blocks = jnp.concatenate(pieces, axis=0).astype(BF).reshape(N_DEV, -1, D_MODEL)
    mine = lax.dynamic_index_in_dim(blocks.reshape(N_CHIP, 2, -1, D_MODEL), c, axis=1, keepdims=False)
    received = _chip_exchange(_pair_sum(mine, _sibling_swap(blocks)))

    out = _unpack_state(_adamw(received, _pack_state([local, mom, vel])), local)
    out["w_in"] = _adamw_w_in(received, w_in, m_w_in, v_w_in, c.astype(jnp.int32).reshape(1))
    return (loss, grad_x[None], *[out[n][t] for t in range(4) for n in ORDER])
```

```python
import functools

import jax
import jax.numpy as jnp
import numpy as np
from jax import lax
from jax.experimental import pallas as pl
from jax.experimental.pallas import tpu as pltpu

F32 = jnp.float32
BF = jnp.bfloat16
MESH = pl.DeviceIdType.MESH

D_MODEL = 1024
DEPTH = 2
EPS = 1e-6
N_DEV = 8
VMEM_LIMIT = 56 * 1024 * 1024
NEG = -1e30
MLA_SCALE = 96.0 ** -0.5
SB_SCALE = 0.125
LOG2E = 1.4426950408889634

NP = 3712
NF = 2432
NB = 1280
C_GATE = 0
C_BB = 1024
C_BC = 1280
C_BX = 1536
C_CQ = 1792
C_CKV = 2048
C_CKR = 2176
C_CKRS = 2304
C_AQ = 0
C_AK = 256
C_AV = 384
C_DQ = 512
C_DK = 768
C_DV = 1024
C_END = 3712

def _swap32(a):
    return jnp.concatenate([a[:, 16:32], a[:, 0:16]], axis=1)

ADAM_LR, ADAM_B1, ADAM_B2, ADAM_EPS, ADAM_WD, ADAM_STEP = 0.001, 0.9, 0.999, 1e-08, 0.01, 10


def _dot(a, b):
    return jnp.dot(a, b, preferred_element_type=F32)


def _dot_nt(a, b):
    return lax.dot_general(a, b, (((1,), (1,)), ((), ())), preferred_element_type=F32)


def _dot_tn(a, b):
    return lax.dot_general(a, b, (((0,), (0,)), ((), ())), preferred_element_type=F32)


def _params(n_grid):
    return pltpu.CompilerParams(dimension_semantics=("arbitrary",) * n_grid, vmem_limit_bytes=VMEM_LIMIT)


def _rms_fwd(x, g):
    r = lax.rsqrt(jnp.mean(x * x, axis=-1, keepdims=True) + EPS)
    return (x * r) * g, r


def _rms_bwd(x, g, r, dy, width=None):
    n = x.shape[-1] if width is None else width
    u = dy * g
    dx = r * u - x * (r * r * r) * (jnp.sum(x * u, axis=-1, keepdims=True) / n)
    return dx, dy * (x * r)


def _iota(shape, axis):
    return lax.broadcasted_iota(jnp.int32, shape, axis)


def _inproj_fwd(x, g, wpt):
    T = x.shape[0]
    tm = 256

    def body(x_ref, g_ref, w_ref, h32_ref, h16_ref, xn_ref):
        xn, _ = _rms_fwd(x_ref[...], g_ref[...])
        xn = xn.astype(BF)
        xn_ref[...] = xn
        h = _dot_nt(xn, w_ref[...])
        h32_ref[...] = h[:, :NF]
        h16_ref[...] = h[:, NF:].astype(BF)

    return pl.pallas_call(
        body, name="inproj_fwd", grid=(T // tm,),
        in_specs=[pl.BlockSpec((tm, D_MODEL), lambda n: (n, 0)),
                  pl.BlockSpec((1, D_MODEL), lambda n: (0, 0)),
                  pl.BlockSpec((NP, D_MODEL), lambda n: (0, 0))],
        out_specs=[pl.BlockSpec((tm, NF), lambda n: (n, 0)),
                   pl.BlockSpec((tm, NB), lambda n: (n, 0)),
                   pl.BlockSpec((tm, D_MODEL), lambda n: (n, 0))],
        out_shape=[jax.ShapeDtypeStruct((T, NF), F32), jax.ShapeDtypeStruct((T, NB), BF),
                   jax.ShapeDtypeStruct((T, D_MODEL), BF)],
        compiler_params=_params(1))(x, g, wpt)


def _inproj_bwd(parts, wpt, x, xn, g, dxo):
    T = x.shape[0]
    tm = 256
    np_ = len(parts)
    chunks = (0, 1280, 2560, NP)
    assert sum(p.shape[1] for p in parts) == C_END == NP

    def body(*refs):
        part_refs = refs[:np_]
        w_ref, x_ref, xn_ref, g_ref, dxo_ref, dw_ref, dx_ref, dg_ref = refs[np_:]
        n = pl.program_id(0)

        @pl.when(n == 0)
        def _():
            dw_ref[...] = jnp.zeros_like(dw_ref)
            dg_ref[...] = jnp.zeros_like(dg_ref)

        dh = jnp.concatenate([r[...].astype(BF) for r in part_refs], axis=1)
        xnv = xn_ref[...]
        for lo, hi in zip(chunks[:-1], chunks[1:]):
            dw_ref[:, lo:hi] += _dot_tn(xnv, dh[:, lo:hi])
        dxn = _dot(dh, w_ref[...])
        xv = x_ref[...]
        _, r = _rms_fwd(xv, g_ref[...])
        dx, dgt = _rms_bwd(xv, g_ref[...], r, dxn)
        dx_ref[...] = dxo_ref[...] + dx
        dg_ref[...] += jnp.sum(dgt, axis=0, keepdims=True)

    once = pl.Buffered(1)
    return pl.pallas_call(
        body, name="inproj_bwd", grid=(T // tm,),
        in_specs=[pl.BlockSpec((tm, p.shape[1]), lambda n: (n, 0)) for p in parts]
        + [pl.BlockSpec((NP, D_MODEL), lambda n: (0, 0), pipeline_mode=once),
           pl.BlockSpec((tm, D_MODEL), lambda n: (n, 0)),
           pl.BlockSpec((tm, D_MODEL), lambda n: (n, 0)),
           pl.BlockSpec((1, D_MODEL), lambda n: (0, 0)),
           pl.BlockSpec((tm, D_MODEL), lambda n: (n, 0))],
        out_specs=[pl.BlockSpec((D_MODEL, NP), lambda n: (0, 0), pipeline_mode=once),
                   pl.BlockSpec((tm, D_MODEL), lambda n: (n, 0)),
                   pl.BlockSpec((1, D_MODEL), lambda n: (0, 0))],
        out_shape=[jax.ShapeDtypeStruct((D_MODEL, NP), F32), jax.ShapeDtypeStruct((T, D_MODEL), F32),
                   jax.ShapeDtypeStruct((1, D_MODEL), F32)],
        compiler_params=_params(1))(*parts, wpt, x, xn, g, dxo)


SWA_BLK = 128
SWA_TQ = 2048


def _bdot_nt(a, b):
    return lax.dot_general(a, b, (((2,), (2,)), ((0,), (0,))), preferred_element_type=F32)


def _bdot(a, b):
    return lax.dot_general(a, b, (((2,), (1,)), ((0,), (0,))), preferred_element_type=F32)


def _bdot_tn(a, b):
    return lax.dot_general(a, b, (((1,), (1,)), ((0,), (0,))), preferred_element_type=F32)


def _swa_probs(q, kc, kp, sink, mask_c, mask_p):
    sc = jnp.where(mask_c, _bdot_nt(q, kc) * SB_SCALE, NEG)
    sp = jnp.where(mask_p, _bdot_nt(q, kp) * SB_SCALE, NEG)
    m = jnp.maximum(jnp.maximum(jnp.max(sc, axis=-1, keepdims=True), jnp.max(sp, axis=-1, keepdims=True)), sink)
    pc = jnp.exp(sc - m)
    pp = jnp.exp(sp - m)
    ps = jnp.exp(sink - m)
    inv = 1.0 / (jnp.sum(pc, axis=-1, keepdims=True) + jnp.sum(pp, axis=-1, keepdims=True) + ps)
    return pc * inv, pp * inv, ps * inv


def _swa_masks(n, nb):
    blk = _iota((nb, SWA_BLK, SWA_BLK), 0)
    row = _iota((nb, SWA_BLK, SWA_BLK), 1)
    col = _iota((nb, SWA_BLK, SWA_BLK), 2)
    return col <= row, jnp.logical_and(col > row, jnp.logical_or(blk > 0, n > 0))


def _swa_specs(tq):
    halo = tq // SWA_BLK
    return [pl.BlockSpec(memory_space=pltpu.SMEM),
            pl.BlockSpec((tq, 256), lambda n: (n, C_AQ // 256)),
            pl.BlockSpec((tq, 128), lambda n: (n, C_AK // 128)),
            pl.BlockSpec((SWA_BLK, 128), lambda n: (jnp.maximum(n * halo - 1, 0), C_AK // 128)),
            pl.BlockSpec((tq, 128), lambda n: (n, C_AV // 128)),
            pl.BlockSpec((SWA_BLK, 128), lambda n: (jnp.maximum(n * halo - 1, 0), C_AV // 128))]


def _swa_blocked(cur_ref, prev_ref, gs, nb):
    cur = cur_ref[:, gs].reshape(nb, SWA_BLK, 64)
    prev = jnp.concatenate([prev_ref[:, gs].reshape(1, SWA_BLK, 64), cur[:nb - 1]], axis=0) if nb > 1 \
        else prev_ref[:, gs].reshape(1, SWA_BLK, 64)
    return cur, prev


def _swa_masks2(n, nb):
    mask_c, mask_p = _swa_masks(n, nb)
    return jnp.concatenate([mask_c, mask_c], axis=1), jnp.concatenate([mask_p, mask_p], axis=1)


def _swa_stacked(ref, g, nb):
    a = ref[:, 128 * g:128 * g + 64].reshape(nb, SWA_BLK, 64)
    b = ref[:, 128 * g + 64:128 * g + 128].reshape(nb, SWA_BLK, 64)
    return jnp.concatenate([a, b], axis=1)


def _swa_fwd(h16, sinks):
    T = h16.shape[0]
    tq = SWA_TQ if T % SWA_TQ == 0 else SWA_BLK
    nb = tq // SWA_BLK

    def body(s_ref, q_ref, kc_ref, kp_ref, vc_ref, vp_ref, o_ref):
        n = pl.program_id(0)
        mask_c, mask_p = _swa_masks2(n, nb)
        first = _iota((1, 2 * SWA_BLK, 1), 1) < SWA_BLK
        for g in range(2):
            gs = slice(g * 64, (g + 1) * 64)
            kc, kp = _swa_blocked(kc_ref, kp_ref, gs, nb)
            vc, vp = _swa_blocked(vc_ref, vp_ref, gs, nb)
            q = _swa_stacked(q_ref, g, nb)
            sink = jnp.where(first, s_ref[2 * g], s_ref[2 * g + 1])
            pc, pp, _ = _swa_probs(q, kc, kp, sink, mask_c, mask_p)
            o = _bdot(pc.astype(BF), vc) + _bdot(pp.astype(BF), vp)
            o_ref[:, 128 * g:128 * g + 64] = o[:, :SWA_BLK].reshape(tq, 64)
            o_ref[:, 128 * g + 64:128 * g + 128] = o[:, SWA_BLK:].reshape(tq, 64)

    return pl.pallas_call(
        body, name="swa_fwd", grid=(T // tq,), in_specs=_swa_specs(tq),
        out_specs=pl.BlockSpec((tq, 256), lambda n: (n, 0)),
        out_shape=jax.ShapeDtypeStruct((T, 256), F32),
        compiler_params=_params(1))(sinks, h16, h16, h16, h16, h16)


def _swa_bwd(h16, sinks, dya):
    T = h16.shape[0]
    tq = SWA_TQ if T % SWA_TQ == 0 else SWA_BLK
    nb = tq // SWA_BLK

    def body(s_ref, q_ref, kc_ref, kp_ref, vc_ref, vp_ref, do_ref, dq_ref, dk_ref, dv_ref, ds_ref):
        n = pl.program_id(0)

        @pl.when(n == 0)
        def _():
            dk_ref[...] = jnp.zeros_like(dk_ref)
            dv_ref[...] = jnp.zeros_like(dv_ref)
            ds_ref[...] = jnp.zeros_like(ds_ref)

        mask_c, mask_p = _swa_masks2(n, nb)
        first = _iota((1, 2 * SWA_BLK, 1), 1) < SWA_BLK
        rows = pl.ds(pl.multiple_of(n * tq, tq), tq)
        before = pl.ds(pl.multiple_of(jnp.maximum(n * nb - 1, 0) * SWA_BLK, SWA_BLK), SWA_BLK)
        lane = _iota((8, 128), 1)
        row8 = _iota((8, 128), 0)

        def to_keys(own, prev):
            if nb == 1:
                return own
            return own + jnp.concatenate([prev[1:], jnp.zeros((1, SWA_BLK, 64), F32)], axis=0)

        for g in range(2):
            gs = slice(g * 64, (g + 1) * 64)
            q = _swa_stacked(q_ref, g, nb)
            kc, kp = _swa_blocked(kc_ref, kp_ref, gs, nb)
            vc, vp = _swa_blocked(vc_ref, vp_ref, gs, nb)
            sink = jnp.where(first, s_ref[2 * g], s_ref[2 * g + 1])
            pc, pp, ps = _swa_probs(q, kc, kp, sink, mask_c, mask_p)
            pcb, ppb = pc.astype(BF), pp.astype(BF)
            do = _swa_stacked(do_ref, g, nb)
            dob = do.astype(BF)
            o = _bdot(pcb, vc) + _bdot(ppb, vp)
            dd = jnp.sum(do * o, axis=-1, keepdims=True)
            dsc = (pc * (_bdot_nt(dob, vc) - dd) * SB_SCALE).astype(BF)
            dsp = (pp * (_bdot_nt(dob, vp) - dd) * SB_SCALE).astype(BF)
            dq = (_bdot(dsc, kc) + _bdot(dsp, kp)).astype(BF)
            dq_ref[:, 128 * g:128 * g + 64] = dq[:, :SWA_BLK].reshape(tq, 64)
            dq_ref[:, 128 * g + 64:128 * g + 128] = dq[:, SWA_BLK:].reshape(tq, 64)
            dkp, dvp = _bdot_tn(dsp, q), _bdot_tn(ppb, dob)
            dk_ref[rows, gs] += to_keys(_bdot_tn(dsc, q), dkp).reshape(tq, 64)
            dv_ref[rows, gs] += to_keys(_bdot_tn(pcb, dob), dvp).reshape(tq, 64)
            dk_ref[before, gs] += dkp[0]
            dv_ref[before, gs] += dvp[0]
            psd = ps * dd
            for hh in range(2):
                part = psd[:, hh * SWA_BLK:(hh + 1) * SWA_BLK]
                ds_ref[...] += jnp.where(jnp.logical_and(lane == 2 * g + hh, row8 == 0), -jnp.sum(part), 0.0)

    return pl.pallas_call(
        body, name="swa_bwd", grid=(T // tq,),
        in_specs=_swa_specs(tq) + [pl.BlockSpec((tq, 256), lambda n: (n, 0))],
        out_specs=[pl.BlockSpec((tq, 256), lambda n: (n, 0)),
                   pl.BlockSpec((T, 128), lambda n: (0, 0)),
                   pl.BlockSpec((T, 128), lambda n: (0, 0)),
                   pl.BlockSpec((8, 128), lambda n: (0, 0))],
        out_shape=[jax.ShapeDtypeStruct((T, 256), BF), jax.ShapeDtypeStruct((T, 128), F32),
                   jax.ShapeDtypeStruct((T, 128), F32), jax.ShapeDtypeStruct((8, 128), F32)],
        compiler_params=_params(1))(sinks, h16, h16, h16, h16, h16, dya)


def _conv_u(bc_ref, bx_ref, bch_ref, bxh_ref, n, tm):
    u = bc_ref[...] * bx_ref[...]
    uh = bch_ref[...] * bxh_ref[...] * (n > 0).astype(F32)
    rowi = _iota((tm, 256), 0)
    u1 = jnp.where(rowi == 0, uh[7:8, :], pltpu.roll(u, 1, axis=0))
    u2 = jnp.where(rowi == 0, uh[6:7, :], jnp.where(rowi == 1, uh[7:8, :], pltpu.roll(u, 2, axis=0)))
    return u, u1, u2


def _conv_fwd(h32, cw, cb):
    T = h32.shape[0]
    tm = 1024 if T % 1024 == 0 else T
    hb = tm // 8

    def body(bb_ref, bc_ref, bx_ref, bch_ref, bxh_ref, w_ref, b_ref, o_ref):
        n = pl.program_id(0)
        u, u1, u2 = _conv_u(bc_ref, bx_ref, bch_ref, bxh_ref, n, tm)
        y = w_ref[0:1, :] * u2 + w_ref[1:2, :] * u1 + w_ref[2:3, :] * u + b_ref[...]
        o_ref[...] = bb_ref[...] * y

    halo = lambda c: pl.BlockSpec((8, 256), lambda n: (jnp.maximum(n * hb - 1, 0), c // 256))
    return pl.pallas_call(
        body, name="conv_fwd", grid=(T // tm,),
        in_specs=[pl.BlockSpec((tm, 256), lambda n: (n, C_BB // 256)),
                  pl.BlockSpec((tm, 256), lambda n: (n, C_BC // 256)),
                  pl.BlockSpec((tm, 256), lambda n: (n, C_BX // 256)),
                  halo(C_BC), halo(C_BX),
                  pl.BlockSpec((8, 256), lambda n: (0, 0)),
                  pl.BlockSpec((1, 256), lambda n: (0, 0))],
        out_specs=pl.BlockSpec((tm, 256), lambda n: (n, 0)),
        out_shape=jax.ShapeDtypeStruct((T, 256), F32),
        compiler_params=_params(1))(h32, h32, h32, h32, h32, cw, cb)


def _conv_bwd(h32, cw, cb, dyb):
    T = h32.shape[0]
    tm = 1024 if T % 1024 == 0 else T
    hb = tm // 8
    nt = T // tm

    def body(bb_ref, bc_ref, bx_ref, bch_ref, bxh_ref, bbn_ref, dy_ref, dyn_ref, w_ref, b_ref,
             dbb_ref, dbc_ref, dbx_ref, dw_ref):
        n = pl.program_id(0)

        @pl.when(n == 0)
        def _():
            dw_ref[...] = jnp.zeros_like(dw_ref)

        u, u1, u2 = _conv_u(bc_ref, bx_ref, bch_ref, bxh_ref, n, tm)
        w0, w1, w2 = w_ref[0:1, :], w_ref[1:2, :], w_ref[2:3, :]
        y = w0 * u2 + w1 * u1 + w2 * u + b_ref[...]
        dyb_ = dy_ref[...]
        dbb_ref[...] = (dyb_ * y).astype(BF)
        dy = dyb_ * bb_ref[...]
        dyn = dyn_ref[...] * bbn_ref[...] * (n < nt - 1).astype(F32)
        rowi = _iota((tm, 256), 0)
        dy1 = jnp.where(rowi == tm - 1, dyn[0:1, :], pltpu.roll(dy, tm - 1, axis=0))
        dy2 = jnp.where(rowi == tm - 2, dyn[0:1, :],
                        jnp.where(rowi == tm - 1, dyn[1:2, :], pltpu.roll(dy, tm - 2, axis=0)))
        du = w2 * dy + w1 * dy1 + w0 * dy2
        dbc_ref[...] = (du * bx_ref[...]).astype(BF)
        dbx_ref[...] = (du * bc_ref[...]).astype(BF)
        dw_ref[0:1, :] += jnp.sum(dy * u2, axis=0, keepdims=True)
        dw_ref[1:2, :] += jnp.sum(dy * u1, axis=0, keepdims=True)
        dw_ref[2:3, :] += jnp.sum(dy * u, axis=0, keepdims=True)
        dw_ref[3:4, :] += jnp.sum(dy, axis=0, keepdims=True)

    halo = lambda c: pl.BlockSpec((8, 256), lambda n: (jnp.maximum(n * hb - 1, 0), c // 256))
    nxt = lambda c: pl.BlockSpec((8, 256), lambda n: (jnp.minimum((n + 1) * hb, T // 8 - 1), c // 256))
    cur = lambda c: pl.BlockSpec((tm, 256), lambda n: (n, c // 256))
    return pl.pallas_call(
        body, name="conv_bwd", grid=(nt,),
        in_specs=[cur(C_BB), cur(C_BC), cur(C_BX), halo(C_BC), halo(C_BX), nxt(C_BB),
                  cur(0), nxt(0),
                  pl.BlockSpec((8, 256), lambda n: (0, 0)),
                  pl.BlockSpec((1, 256), lambda n: (0, 0))],
        out_specs=[cur(0), cur(0), cur(0), pl.BlockSpec((8, 256), lambda n: (0, 0))],
        out_shape=[jax.ShapeDtypeStruct((T, 256), BF)] * 3 + [jax.ShapeDtypeStruct((8, 256), F32)],
        compiler_params=_params(1))(h32, h32, h32, h32, h32, h32, dyb, dyb, cw, cb)


def _cprep_specs(tm):
    return [pl.BlockSpec((tm, 256), lambda n: (n, C_CQ // 256)),
            pl.BlockSpec((tm, 128), lambda n: (n, C_CKV // 128)),
            pl.BlockSpec((tm, 128), lambda n: (n, C_CKR // 128)),
            pl.BlockSpec((tm, 128), lambda n: (n, C_CKRS // 128)),
            pl.BlockSpec((1, 256), lambda n: (0, 0)),
            pl.BlockSpec((1, 128), lambda n: (0, 0)),
            pl.BlockSpec((tm, 128), lambda n: (n, 0)),
            pl.BlockSpec((tm, 128), lambda n: (n, 0))]


def _cprep_fwd(h32, gq, gkv, wuq2, wkv2, cosk, sin):
    T = h32.shape[0]
    tm = 1024 if T % 1024 == 0 else T

    def body(cq_ref, ckv_ref, ckr_ref, ckrs_ref, gq_ref, gkv_ref, cos_ref, sin_ref, wuq_ref, wkv_ref,
             q_ref, k_ref, v_ref):
        cosk_, sin_ = cos_ref[...], sin_ref[...]
        cosq = cosk_ + (_iota((tm, 128), 1) < 64).astype(F32)
        cqn, _ = _rms_fwd(cq_ref[...], gq_ref[...])
        q2 = _dot(cqn.astype(BF), wuq_ref[...])
        ckvn, _ = _rms_fwd(ckv_ref[...], gkv_ref[...])
        kv2 = _dot(ckvn.astype(BF), wkv_ref[...])
        kr = ckr_ref[...] * cosk_ + ckrs_ref[...] * sin_
        for h in range(4):
            hs = slice(h * 128, (h + 1) * 128)
            q_ref[:, hs] = ((q2[:, hs] * cosq + q2[:, 512 + h * 128:512 + (h + 1) * 128] * sin_) * MLA_SCALE).astype(BF)
            k_ref[:, hs] = (kv2[:, hs] + kr).astype(BF)
        ones = _iota((tm, 512), 1) % 128 == 64
        v_ref[...] = jnp.where(ones, 1.0, kv2[:, 512:]).astype(BF)

    return pl.pallas_call(
        body, name="cprep_fwd", grid=(T // tm,),
        in_specs=_cprep_specs(tm) + [pl.BlockSpec((256, 1024), lambda n: (0, 0)),
                                     pl.BlockSpec((128, 1024), lambda n: (0, 0))],
        out_specs=[pl.BlockSpec((tm, 512), lambda n: (n, 0))] * 3,
        out_shape=[jax.ShapeDtypeStruct((T, 512), BF)] * 3,
        compiler_params=_params(1))(h32, h32, h32, h32, gq, gkv, cosk, sin, wuq2, wkv2)


def _cprep_bwd(h32, gq, gkv, wuq2t, wkv2t, cosk, sin, dq, dk, dv):
    T = h32.shape[0]
    tm = 1024 if T % 1024 == 0 else T

    def body(cq_ref, ckv_ref, ckr_ref, ckrs_ref, gq_ref, gkv_ref, cos_ref, sin_ref, wuq_ref, wkv_ref,
             dq_ref, dk_ref, dv_ref,
             dcq_ref, dckv_ref, dckr_ref, dckrs_ref, dwuq_ref, dwkv_ref, dgq_ref, dgkv_ref):
        n = pl.program_id(0)

        @pl.when(n == 0)
        def _():
            dwuq_ref[...] = jnp.zeros_like(dwuq_ref)
            dwkv_ref[...] = jnp.zeros_like(dwkv_ref)
            dgq_ref[...] = jnp.zeros_like(dgq_ref)
            dgkv_ref[...] = jnp.zeros_like(dgkv_ref)

        cosk_, sin_ = cos_ref[...], sin_ref[...]
        cosq = cosk_ + (_iota((tm, 128), 1) < 64).astype(F32)
        dkr = jnp.zeros((tm, 128), F32)
        plain, swapped = [], []
        for h in range(4):
            hs = slice(h * 128, (h + 1) * 128)
            dqh = dq_ref[:, hs] * MLA_SCALE
            plain.append((dqh * cosq).astype(BF))
            swapped.append((dqh * sin_).astype(BF))
            dkr = dkr + dk_ref[:, hs]
        dq2 = jnp.concatenate(plain + swapped, axis=1)
        dkv2 = jnp.concatenate([dk_ref[...].astype(BF), dv_ref[...].astype(BF)], axis=1)
        dckr_ref[...] = (dkr * cosk_).astype(BF)
        dckrs_ref[...] = (dkr * sin_).astype(BF)

        cq, gq_ = cq_ref[...], gq_ref[...]
        cqn, rq = _rms_fwd(cq, gq_)
        dwuq_ref[...] += _dot_tn(cqn.astype(BF), dq2)
        dcq, dgt = _rms_bwd(cq, gq_, rq, _dot(dq2, wuq_ref[...]))
        dcq_ref[...] = dcq.astype(BF)
        dgq_ref[...] += jnp.sum(dgt, axis=0, keepdims=True)

        ckv, gkv_ = ckv_ref[...], gkv_ref[...]
        ckvn, rkv = _rms_fwd(ckv, gkv_)
        dwkv_ref[...] += _dot_tn(ckvn.astype(BF), dkv2)
        dckv, dgt2 = _rms_bwd(ckv, gkv_, rkv, _dot(dkv2, wkv_ref[...]))
        dckv_ref[...] = dckv.astype(BF)
        dgkv_ref[...] += jnp.sum(dgt2, axis=0, keepdims=True)

    row = lambda w: pl.BlockSpec((tm, w), lambda n: (n, 0))
    return pl.pallas_call(
        body, name="cprep_bwd", grid=(T // tm,),
        in_specs=_cprep_specs(tm) + [pl.BlockSpec((1024, 256), lambda n: (0, 0)),
                                     pl.BlockSpec((1024, 128), lambda n: (0, 0)),
                                     row(512), row(512), row(512)],
        out_specs=[row(256), row(128), row(128), row(128),
                   pl.BlockSpec((256, 1024), lambda n: (0, 0)), pl.BlockSpec((128, 1024), lambda n: (0, 0)),
                   pl.BlockSpec((1, 256), lambda n: (0, 0)), pl.BlockSpec((1, 128), lambda n: (0, 0))],
        out_shape=[jax.ShapeDtypeStruct((T, 256), BF), jax.ShapeDtypeStruct((T, 128), BF),
                   jax.ShapeDtypeStruct((T, 128), BF), jax.ShapeDtypeStruct((T, 128), BF),
                   jax.ShapeDtypeStruct((256, 1024), F32), jax.ShapeDtypeStruct((128, 1024), F32),
                   jax.ShapeDtypeStruct((1, 256), F32), jax.ShapeDtypeStruct((1, 128), F32)],
        compiler_params=_params(1))(h32, h32, h32, h32, gq, gkv, cosk, sin, wuq2t, wkv2t, dq, dk, dv)


MLA_TILE = 512
MLA_HEADS_PER_STEP = 4


def _causal_mask(t):
    return _iota((t, t), 1) <= _iota((t, t), 0)


def _mla_fwd(q, k, v):
    T = q.shape[0]
    tq = MLA_TILE

    def body(q_ref, k_ref, v_ref, o_ref, lse_ref):
        i = pl.program_id(1)
        mask = _causal_mask(tq)
        heads = [slice(128 * h, 128 * h + 128) for h in range(MLA_HEADS_PER_STEP)]
        qs = [q_ref[:, hs] for hs in heads]

        def step(j, carry, masked):
            rows = pl.ds(pl.multiple_of(j * tq, tq), tq)
            out = []
            for hh, hs in enumerate(heads):
                m, acc = carry[hh]
                s = _dot_nt(qs[hh], k_ref[rows, hs])
                if masked:
                    s = jnp.where(mask, s, NEG)
                m_new = jnp.maximum(m, jnp.max(s, axis=-1, keepdims=True))
                p = jnp.exp((s - m_new).astype(BF))
                acc = jnp.exp(m - m_new) * acc + _dot(p, v_ref[rows, hs])
                out.append((m_new, acc))
            return tuple(out)

        init = ((jnp.full((tq, 1), NEG, F32), jnp.zeros((tq, 128), F32)),) * len(heads)
        carry = lax.fori_loop(0, i // 2, lambda t, c: step(2 * t + 1, step(2 * t, c, False), False), init)
        carry = lax.cond(i % 2 == 1, lambda c: step(i - 1, c, False), lambda c: c, carry)
        carry = step(i, carry, True)
        for hh, hs in enumerate(heads):
            m, acc = carry[hh]
            l = acc[:, 64:65]
            o_ref[:, hs] = acc * (1.0 / l)
            lse_ref[:, hs] = jnp.broadcast_to(m + jnp.log(l), (tq, 128))

    width = 128 * MLA_HEADS_PER_STEP
    blk = pl.BlockSpec((tq, width), lambda h, i: (i, h))
    full = pl.BlockSpec((T, width), lambda h, i: (0, h))
    return pl.pallas_call(
        body, name="mla_fwd", grid=(4 // MLA_HEADS_PER_STEP, T // tq), in_specs=[blk, full, full],
        out_specs=[blk, blk],
        out_shape=[jax.ShapeDtypeStruct((T, 512), F32), jax.ShapeDtypeStruct((T, 512), F32)],
        compiler_params=_params(2))(q, k, v)


def _mla_bwd(q, k, v, o, lse, do):
    T = q.shape[0]
    tq = MLA_TILE

    def body(q_ref, k_ref, v_ref, o_ref, lse_ref, do_ref, dq_ref, dk_ref, dv_ref):
        i = pl.program_id(1)

        @pl.when(i == 0)
        def _():
            dk_ref[...] = jnp.zeros_like(dk_ref)
            dv_ref[...] = jnp.zeros_like(dv_ref)

        heads = [slice(0, 128), slice(128, 256)]
        mask = _causal_mask(tq)
        qs, dobs, dds, lses = [], [], [], []
        for hs in heads:
            do = do_ref[:, hs]
            qs.append(q_ref[:, hs])
            dobs.append(do.astype(BF))
            dds.append(jnp.sum(do * o_ref[:, hs], axis=-1, keepdims=True))
            lses.append(lse_ref[:, hs.start:hs.start + 1])

        def step(j, dqs, masked):
            rows = pl.ds(pl.multiple_of(j * tq, tq), tq)
            out = []
            for hh, hs in enumerate(heads):
                kj, vj = k_ref[rows, hs], v_ref[rows, hs]
                s = _dot_nt(qs[hh], kj)
                if masked:
                    s = jnp.where(mask, s, NEG)
                p = jnp.exp(s - lses[hh])
                ds = (p * (_dot_nt(dobs[hh], vj) - dds[hh])).astype(BF)
                dk_ref[rows, hs] += _dot_tn(ds, qs[hh])
                dv_ref[rows, hs] += _dot_tn(p.astype(BF), dobs[hh])
                out.append(dqs[hh] + _dot(ds, kj))
            return tuple(out)

        dqs = lax.fori_loop(0, i // 2, lambda t, c: step(2 * t + 1, step(2 * t, c, False), False),
                            (jnp.zeros((tq, 128), F32),) * 2)
        dqs = lax.cond(i % 2 == 1, lambda c: step(i - 1, c, False), lambda c: c, dqs)
        dqs = step(i, dqs, True)
        for hh, hs in enumerate(heads):
            dq_ref[:, hs] = dqs[hh]

    blk = pl.BlockSpec((tq, 256), lambda h, i: (i, h))
    full = pl.BlockSpec((T, 256), lambda h, i: (0, h), pipeline_mode=pl.Buffered(1))
    return pl.pallas_call(
        body, name="mla_bwd", grid=(2, T // tq), in_specs=[blk, full, full, blk, blk, blk],
        out_specs=[blk, full, full],
        out_shape=[jax.ShapeDtypeStruct((T, 512), F32)] * 3,
        compiler_params=_params(2))(q, k, v, o, lse, do)


def _sb_tile(qk, rr, strict, masked, upper):
    z2 = qk * (SB_SCALE * LOG2E)
    l1 = jnp.log2(1.0 + jnp.exp2(-jnp.abs(z2)))
    lk = -jnp.maximum(z2, 0.0) - l1
    if masked:
        lk = jnp.where(strict, lk, 0.0)
    after = rr + _dot(lk.astype(BF), upper)
    ll = jnp.minimum(z2, 0.0) - l1
    a = jnp.exp2(ll + after)
    if masked:
        a = jnp.where(strict, a, 0.0)
    return ll, a, jnp.sum(lk, axis=-1, keepdims=True)


SB_TQ, SB_TK = 256, 256
SB_DEAD = -160.0


def _sb_walk(trips, one_step, carry):
    def alive(c):
        t, cr = c
        top = functools.reduce(jnp.maximum, [jnp.max(h[0]) for h in cr])
        return jnp.logical_and(t < trips, top > SB_DEAD)

    def body(c):
        t, cr = c
        return t + 1, one_step(t, cr)

    return lax.while_loop(alive, body, (jnp.int32(0), carry))[1]


def _sb_consts(tq, tk):
    row, col = _iota((tq, tk), 0), _iota((tq, tk), 1)
    strict = [col + d * tk < row for d in range(tq // tk)]
    r2, c2 = _iota((tk, tk), 0), _iota((tk, tk), 1)
    return strict, (r2 > c2).astype(BF), (r2 < c2).astype(BF)


def _sb_fwd(h16):
    T = h16.shape[0]
    tq, tk = SB_TQ, SB_TK
    nd = tq // tk

    def body(q0_ref, q1_ref, k0_ref, k1_ref, v0_ref, v1_ref, o_ref):
        i = pl.program_id(0)
        strict, upper, _ = _sb_consts(tq, tk)
        lane = _iota((tq, 128), 1)
        pairs = [slice(0, 128), slice(128, 256)]
        k_refs, v_refs = (k0_ref, k1_ref), (v0_ref, v1_ref)
        qms = []
        for q_ref in (q0_ref, q1_ref):
            q2 = q_ref[...]
            qms += [jnp.where(lane < 64, q2, jnp.zeros_like(q2)), jnp.where(lane >= 64, q2, jnp.zeros_like(q2))]

        def step(j, carry, d):
            rows = pl.ds(pl.multiple_of(j * tk, tk), tk)
            out = []
            for h in range(4):
                rr, acc = carry[h]
                _, a, rs = _sb_tile(_dot_nt(qms[h], k_refs[h // 2][rows, :]), rr, None if d is None else strict[d],
                                    d is not None, upper)
                out.append((rr + rs, acc + _dot(a.astype(BF), v_refs[h // 2][rows, :])))
            return tuple(out)

        carry = ((jnp.zeros((tq, 1), F32), jnp.zeros((tq, 128), F32)),) * 4
        for d in reversed(range(nd)):
            carry = step(nd * i + d, carry, d)
        carry = _sb_walk(nd * i, lambda t, c: step(nd * i - 1 - t, c, None), carry)
        for p, ps in enumerate(pairs):
            o_ref[:, ps] = jnp.where(lane < 64, carry[2 * p][1], carry[2 * p + 1][1])

    return pl.pallas_call(
        body, name="sb_fwd", grid=(T // tq,),
        in_specs=[pl.BlockSpec((tq, 128), lambda i: (i, C_DQ // 128)),
                  pl.BlockSpec((tq, 128), lambda i: (i, C_DQ // 128 + 1)),
                  pl.BlockSpec((T, 128), lambda i: (0, C_DK // 128)),
                  pl.BlockSpec((T, 128), lambda i: (0, C_DK // 128 + 1)),
                  pl.BlockSpec((T, 128), lambda i: (0, C_DV // 128)),
                  pl.BlockSpec((T, 128), lambda i: (0, C_DV // 128 + 1))],
        out_specs=pl.BlockSpec((tq, 256), lambda i: (i, 0)),
        out_shape=jax.ShapeDtypeStruct((T, 256), F32),
        compiler_params=_params(1))(h16, h16, h16, h16, h16, h16)


def _sb_bwd(h16, yd, dyd):
    T = h16.shape[0]
    tq, tk = SB_TQ, SB_TK
    nd = tq // tk

    def body(q0_ref, q1_ref, k0_ref, k1_ref, v0_ref, v1_ref, o_ref, do_ref, dq_ref, dk_ref, dv_ref):
        i = pl.program_id(0)

        @pl.when(i == 0)
        def _():
            dk_ref[...] = jnp.zeros_like(dk_ref)
            dv_ref[...] = jnp.zeros_like(dv_ref)

        strict, upper, before = _sb_consts(tq, tk)
        lane = _iota((tq, 128), 1)
        lane_k = _iota((tk, 128), 1)
        pairs = [slice(0, 128), slice(128, 256)]
        k_refs, v_refs = (k0_ref, k1_ref), (v0_ref, v1_ref)
        q2s, dob2s, qms, doms, dds = [], [], [], [], []
        for p, q_ref in enumerate((q0_ref, q1_ref)):
            q2 = q_ref[...]
            dob2 = do_ref[:, pairs[p]].astype(BF)
            doo = dob2.astype(F32) * o_ref[:, pairs[p]]
            q2s.append(q2)
            dob2s.append(dob2)
            for mine in (lane < 64, lane >= 64):
                qms.append(jnp.where(mine, q2, jnp.zeros_like(q2)))
                doms.append(jnp.where(mine, dob2, jnp.zeros_like(dob2)))
                dds.append(jnp.sum(jnp.where(mine, doo, 0.0), axis=-1, keepdims=True))

        def step(j, carry, d):
            rows = pl.ds(pl.multiple_of(j * tk, tk), tk)
            out, dks, dvs = [], [], []
            for h in range(4):
                kj, vj = k_refs[h // 2][rows, :], v_refs[h // 2][rows, :]
                rr, sg, dq = carry[h]
                ll, a, rs = _sb_tile(_dot_nt(qms[h], kj), rr, None if d is None else strict[d], d is not None, upper)
                ab = a.astype(BF)
                g = _dot_nt(doms[h], vj) * ab.astype(F32)
                gs = jnp.sum(g, axis=-1, keepdims=True)
                pre = (dds[h] - sg - gs) + _dot(g.astype(BF), before)
                dz = g - jnp.exp2(ll) * (g + pre)
                if d is not None:
                    dz = jnp.where(strict[d], dz, 0.0)
                dzb = dz.astype(BF)
                dks.append(_dot_tn(dzb, q2s[h // 2]))
                dvs.append(_dot_tn(ab, dob2s[h // 2]))
                out.append((rr + rs, sg + gs, dq + _dot(dzb, kj)))
            for p, ps in enumerate(pairs):
                dk_ref[rows, ps] += jnp.where(lane_k < 64, dks[2 * p], dks[2 * p + 1]) * SB_SCALE
                dv_ref[rows, ps] += jnp.where(lane_k < 64, dvs[2 * p], dvs[2 * p + 1])
            return tuple(out)

        zero = jnp.zeros((tq, 1), F32)
        carry = ((zero, zero, jnp.zeros((tq, 128), F32)),) * 4
        for d in reversed(range(nd)):
            carry = step(nd * i + d, carry, d)
        carry = _sb_walk(nd * i, lambda t, c: step(nd * i - 1 - t, c, None), carry)
        for p, ps in enumerate(pairs):
            dq_ref[:, ps] = jnp.where(lane < 64, carry[2 * p][2], carry[2 * p + 1][2]) * SB_SCALE

    blk = lambda c: pl.BlockSpec((tq, 128), lambda i: (i, c // 128))
    full = lambda c: pl.BlockSpec((T, 128), lambda i: (0, c // 128))
    row = pl.BlockSpec((tq, 256), lambda i: (i, 0))
    acc = pl.BlockSpec((T, 256), lambda i: (0, 0))
    return pl.pallas_call(
        body, name="sb_bwd", grid=(T // tq,),
        in_specs=[blk(C_DQ), blk(C_DQ + 128), full(C_DK), full(C_DK + 128), full(C_DV), full(C_DV + 128), row, row],
        out_specs=[row, acc, acc],
        out_shape=[jax.ShapeDtypeStruct((T, 256), F32)] * 3,
        compiler_params=_params(1))(h16, h16, h16, h16, h16, h16, yd, dyd)


def _compact_c(ycp):
    return jnp.concatenate([ycp[:, h * 128:h * 128 + 64] for h in range(4)], axis=1)


def _post_fwd(ya, yb, ycp, yd, h32, ggrp, wout, gpost, x, tgt=None):
    T = x.shape[0]
    tm = 512 if T % 512 == 0 else 256
    last = tgt is not None

    def body(*refs):
        ya_ref, yb_ref, yc_ref, yd_ref, gate_ref, gg_ref, w_ref, gp_ref, x_ref = refs[:9]
        if last:
            t_ref, xn_ref, o_ref, sq_ref = refs[9:]
        else:
            xn_ref, o_ref = refs[9:]
        ys = [ya_ref[...], yb_ref[...], _compact_c(yc_ref[...]), yd_ref[...]]
        gate = gate_ref[...]
        sil = gate * (1.0 / (1.0 + jnp.exp(-gate)))
        parts = []
        for gi in range(4):
            ng, _ = _rms_fwd(ys[gi], gg_ref[:, gi * 256:(gi + 1) * 256])
            parts.append(ng * sil[:, gi * 256:(gi + 1) * 256])
        o = _dot(jnp.concatenate(parts, axis=1).astype(BF), w_ref[...])
        o_ref[...] = o
        on, _ = _rms_fwd(o, gp_ref[...])
        if last:
            @pl.when(pl.program_id(0) == 0)
            def _():
                sq_ref[...] = jnp.zeros_like(sq_ref)

            d = (x_ref[...] + on) - t_ref[...]
            sq_ref[...] += jnp.sum(d * d, axis=0, keepdims=True)
            xn_ref[...] = d * (1.0 / D_MODEL)
        else:
            xn_ref[...] = x_ref[...] + on

    row = lambda w: pl.BlockSpec((tm, w), lambda n: (n, 0))
    vec = pl.BlockSpec((1, 1024), lambda n: (0, 0))
    if not last:
        rin = lambda w: pl.BlockSpec((tm, w), lambda n: (n, 0), pipeline_mode=pl.Buffered(3))

        def outer(ya_h, yb_h, yc_h, yd_h, h_h, gg_ref, w_ref, gp_ref, x_h, xn_h, o_h):
            def inner(ya_ref, yb_ref, yc_ref, yd_ref, gate_ref, x_ref, xn_ref, o_ref):
                body(ya_ref, yb_ref, yc_ref, yd_ref, gate_ref, gg_ref, w_ref, gp_ref, x_ref, xn_ref, o_ref)

            pltpu.emit_pipeline(inner, grid=(T // tm,),
                                in_specs=[rin(256), rin(256), rin(512), rin(256), rin(1024), rin(1024)],
                                out_specs=[row(1024), row(1024)])(ya_h, yb_h, yc_h, yd_h, h_h, x_h, xn_h, o_h)

        hbm = pl.BlockSpec(memory_space=pl.ANY)
        vmem = pl.BlockSpec(memory_space=pltpu.VMEM)
        return pl.pallas_call(
            outer, name="post_fwd_piped", in_specs=[hbm, hbm, hbm, hbm, hbm, vmem, vmem, vmem, hbm],
            out_specs=[hbm, hbm],
            out_shape=[jax.ShapeDtypeStruct((T, 1024), F32), jax.ShapeDtypeStruct((T, 1024), F32)],
            compiler_params=pltpu.CompilerParams(vmem_limit_bytes=VMEM_LIMIT))(
                ya, yb, ycp, yd, h32, ggrp, wout, gpost, x)
    return pl.pallas_call(
        body, name="post_fwd", grid=(T // tm,),
        in_specs=[row(256), row(256), row(512), row(256), pl.BlockSpec((tm, 1024), lambda n: (n, C_GATE // 1024)),
                  vec, pl.BlockSpec((1024, 1024), lambda n: (0, 0)), vec, row(1024)] + ([row(1024)] if last else []),
        out_specs=[row(1024), row(1024)] + ([vec] if last else []),
        out_shape=[jax.ShapeDtypeStruct((T, 1024), F32), jax.ShapeDtypeStruct((T, 1024), F32)]
        + ([jax.ShapeDtypeStruct((1, 1024), F32)] if last else []),
        compiler_params=_params(1))(*([ya, yb, ycp, yd, h32, ggrp, wout, gpost, x] + ([tgt] if last else [])))


def _post_bwd(dx, o, gpost, woutt, ya, yb, ycp, yd, h32, ggrp):
    T = dx.shape[0]
    tm = 256

    def body(dx_ref, o_ref, gp_ref, w_ref, ya_ref, yb_ref, yc_ref, yd_ref, gate_ref, gg_ref,
             dw_ref, dya_ref, dyb_ref, dyc_ref, dyd_ref, dgate_ref, dgp_ref, dgg_ref):
        n = pl.program_id(0)

        @pl.when(n == 0)
        def _():
            dw_ref[...] = jnp.zeros_like(dw_ref)
            dgp_ref[...] = jnp.zeros_like(dgp_ref)
            dgg_ref[...] = jnp.zeros_like(dgg_ref)

        ov, gp = o_ref[...], gp_ref[...]
        _, ro = _rms_fwd(ov, gp)
        do, dgt = _rms_bwd(ov, gp, ro, dx_ref[...])
        dgp_ref[...] += jnp.sum(dgt, axis=0, keepdims=True)
        dob = do.astype(BF)
        gate = gate_ref[...]
        sg = 1.0 / (1.0 + jnp.exp(-gate))
        sil = gate * sg
        dsil = sg * (1.0 + gate * (1.0 - sg))
        ys = [ya_ref[...], yb_ref[...], _compact_c(yc_ref[...]), yd_ref[...]]
        normed = [_rms_fwd(ys[gi], gg_ref[:, gi * 256:(gi + 1) * 256]) for gi in range(4)]
        ym = jnp.concatenate([normed[gi][0] * sil[:, gi * 256:(gi + 1) * 256] for gi in range(4)], axis=1).astype(BF)
        dw_ref[...] += _dot_tn(ym, dob)
        dym = _dot(dob, w_ref[...])
        dys = []
        for gi in range(4):
            gs = slice(gi * 256, (gi + 1) * 256)
            gg = gg_ref[:, gs]
            ng, rg = normed[gi]
            dgate_ref[:, gs] = (dym[:, gs] * ng * dsil[:, gs]).astype(BF)
            dy, dgt2 = _rms_bwd(ys[gi], gg, rg, dym[:, gs] * sil[:, gs])
            dgg_ref[:, gs] += jnp.sum(dgt2, axis=0, keepdims=True)
            dys.append(dy)
        dya_ref[...] = dys[0]
        dyb_ref[...] = dys[1]
        dyd_ref[...] = dys[3]
        z64 = jnp.zeros((tm, 64), F32)
        dyc_ref[...] = jnp.concatenate(
            [piece for h in range(4) for piece in (dys[2][:, h * 64:(h + 1) * 64], z64)], axis=1)

    row = lambda w: pl.BlockSpec((tm, w), lambda n: (n, 0))
    vec = pl.BlockSpec((1, 1024), lambda n: (0, 0))
    return pl.pallas_call(
        body, name="post_bwd", grid=(T // tm,),
        in_specs=[row(1024), row(1024), vec, pl.BlockSpec((1024, 1024), lambda n: (0, 0)),
                  row(256), row(256), row(512), row(256),
                  pl.BlockSpec((tm, 1024), lambda n: (n, C_GATE // 1024)), vec],
        out_specs=[pl.BlockSpec((1024, 1024), lambda n: (0, 0)), row(256), row(256), row(512), row(256), row(1024),
                   vec, vec],
        out_shape=[jax.ShapeDtypeStruct((1024, 1024), F32), jax.ShapeDtypeStruct((T, 256), F32),
                   jax.ShapeDtypeStruct((T, 256), F32), jax.ShapeDtypeStruct((T, 512), F32),
                   jax.ShapeDtypeStruct((T, 256), F32), jax.ShapeDtypeStruct((T, 1024), BF),
                   jax.ShapeDtypeStruct((1, 1024), F32), jax.ShapeDtypeStruct((1, 1024), F32)],
        compiler_params=_params(1))(dx, o, gpost, woutt, ya, yb, ycp, yd, h32, ggrp)


def _swap_rows32(a):
    return jnp.concatenate([a[16:32], a[0:16]], axis=0)


def _pad_w_uq(w):
    z = lambda n: jnp.zeros((w.shape[0], n), w.dtype)
    a = [p for h in range(4) for p in (w[:, 96 * h:96 * h + 96], z(32))]
    b = [p for h in range(4) for p in (z(64), _swap32(w[:, 96 * h + 64:96 * h + 96]), z(32))]
    return jnp.concatenate(a + b, axis=1)


def _unpad_w_uq(d):
    out = []
    for h in range(4):
        out.append(d[:, 128 * h:128 * h + 64])
        out.append(d[:, 128 * h + 64:128 * h + 96] + _swap32(d[:, 512 + 128 * h + 64:512 + 128 * h + 96]))
    return jnp.concatenate(out, axis=1)


def _pad_w_ukv(w):
    z = jnp.zeros((w.shape[0], 64), w.dtype)
    a = [p for h in range(4) for p in (w[:, 128 * h:128 * h + 64], z)]
    b = [p for h in range(4) for p in (w[:, 128 * h + 64:128 * h + 128], z)]
    return jnp.concatenate(a + b, axis=1)


def _unpad_w_ukv(d):
    return jnp.concatenate([p for h in range(4) for p in (d[:, 128 * h:128 * h + 64],
                                                          d[:, 512 + 128 * h:512 + 128 * h + 64])], axis=1)


def _rope_tables(pos):
    freqs = 10000.0 ** (-jnp.arange(16, dtype=F32) / 16)
    ang = pos.astype(F32)[:, None] * freqs
    c, s = jnp.cos(ang), jnp.sin(ang)
    z = lambda n: jnp.zeros((pos.shape[0], n), F32)
    return (jnp.concatenate([z(64), c, c, z(32)], axis=1), jnp.concatenate([z(64), -s, s, z(32)], axis=1))


def _layer_weights(W, l):
    wuq2 = _pad_w_uq(W["mla_w_uq"][l])
    wkv2 = _pad_w_ukv(W["mla_w_ukv"][l])
    wout = W["w_out"][l]
    cw = jnp.concatenate([W["conv_w"][l].astype(F32), jnp.zeros((5, 256), F32)], axis=0)
    return dict(
        wpt=W["wpt"][l], wuq2=wuq2.astype(BF), wuq2t=wuq2.T.astype(BF),
        wkv2=wkv2.astype(BF), wkv2t=wkv2.T.astype(BF), wout=wout.astype(BF), woutt=wout.T.astype(BF),
        cw=cw, cb=W["conv_b"][l][None, :], sinks=W["attn_sinks"][l],
        gpre=W["norm_pre"][l][None, :], gq=W["mla_q_norm"][l][None, :], gkv=W["mla_kv_norm"][l][None, :],
        ggrp=W["group_norm"][l][None, :], gpost=W["norm_post"][l][None, :])


def _local_step(x, pos, W, tgt):
    cosk, sin = _rope_tables(pos)
    saved = []
    for l in range(DEPTH):
        lw = _layer_weights(W, l)
        h32, h16, xn = _inproj_fwd(x, lw["gpre"], lw["wpt"])
        ya = _swa_fwd(h16, lw["sinks"])
        yb = _conv_fwd(h32, lw["cw"], lw["cb"])
        qc, kc, vc = _cprep_fwd(h32, lw["gq"], lw["gkv"], lw["wuq2"], lw["wkv2"], cosk, sin)
        ycp, lse = _mla_fwd(qc, kc, vc)
        yd = _sb_fwd(h16)
        if l < DEPTH - 1:
            x_new, o = _post_fwd(ya, yb, ycp, yd, h32, lw["ggrp"], lw["wout"], lw["gpost"], x)
        else:
            dx, o, sq = _post_fwd(ya, yb, ycp, yd, h32, lw["ggrp"], lw["wout"], lw["gpost"], x, tgt)
        saved.append(dict(lw=lw, x=x, h32=h32, h16=h16, xn=xn, ya=ya, yb=yb, qc=qc, kc=kc, vc=vc, ycp=ycp,
                          lse=lse, yd=yd, o=o))
        if l < DEPTH - 1:
            x = x_new

    grads = {k: [None] * DEPTH for k in ("norm_pre", "w_in_pt", "attn_sinks", "conv_w", "conv_b", "mla_q_norm",
                                         "mla_w_uq", "mla_kv_norm", "mla_w_ukv", "group_norm", "w_out",
                                         "norm_post")}
    for l in reversed(range(DEPTH)):
        s = saved[l]
        lw = s["lw"]
        dwout, dya, dyb, dycp, dyd, dgate, dgpost, dggrp = _post_bwd(
            dx, s["o"], lw["gpost"], lw["woutt"], s["ya"], s["yb"], s["ycp"], s["yd"], s["h32"], lw["ggrp"])
        grads["norm_post"][l] = dgpost[0]
        grads["group_norm"][l] = dggrp[0]
        grads["w_out"][l] = dwout
        sdq, sdk, sdv = _sb_bwd(s["h16"], s["yd"], dyd)
        mdq, mdk, mdv = _mla_bwd(s["qc"], s["kc"], s["vc"], s["ycp"], s["lse"], dycp)
        dcq, dckv, dckr, dckrs, dwuq2, dwkv2, dgq, dgkv = _cprep_bwd(
            s["h32"], lw["gq"], lw["gkv"], lw["wuq2t"], lw["wkv2t"], cosk, sin, mdq, mdk, mdv)
        grads["mla_q_norm"][l] = dgq[0]
        grads["mla_kv_norm"][l] = dgkv[0]
        grads["mla_w_uq"][l] = _unpad_w_uq(dwuq2)
        grads["mla_w_ukv"][l] = _unpad_w_ukv(dwkv2)
        dbb, dbc, dbx, dcw = _conv_bwd(s["h32"], lw["cw"], lw["cb"], dyb)
        grads["conv_w"][l] = dcw[0:3]
        grads["conv_b"][l] = dcw[3]
        adq, adk, adv, dsk = _swa_bwd(s["h16"], lw["sinks"], dya)
        grads["attn_sinks"][l] = dsk[0, 0:4]
        parts = [dgate, dbb, dbc, dbx, dcq, dckv, dckr, dckrs, adq, adk, adv, sdq, sdk, sdv]
        dwp, dx, dgpre = _inproj_bwd(parts, lw["wpt"], s["x"], s["xn"], lw["gpre"], dx)
        grads["w_in_pt"][l] = dwp.T
        grads["norm_pre"][l] = dgpre[0]
    return sq, dx, grads


SMALL_SHARDED = ("conv_w", "mla_w_uq", "mla_w_ukv")
REPLICATED = ("norm_pre", "attn_sinks", "conv_b", "mla_q_norm", "mla_kv_norm", "group_norm", "norm_post")
ORDER = ("norm_pre", "w_in", "attn_sinks", "conv_w", "conv_b", "mla_q_norm", "mla_w_uq", "mla_kv_norm",
         "mla_w_ukv", "group_norm", "w_out", "norm_post")
W_IN_COLS = 436
W_IN_WIN = 440
SMALL_ROWS = 48


def _pack_small(arrs, dtype):
    flat = jnp.concatenate([a.reshape(-1).astype(dtype) for a in arrs])
    flat = jnp.concatenate([flat, jnp.zeros((SMALL_ROWS * D_MODEL - flat.shape[0],), dtype)])
    return flat.reshape(SMALL_ROWS, D_MODEL)


TAIL_ROW0 = DEPTH * W_IN_WIN


def _pack_state(ps):
    k = len(ps)
    wout = jnp.stack([p["w_out"] for p in ps]).reshape(k, DEPTH * 128, D_MODEL)
    flat = jnp.stack([jnp.concatenate([p[n].reshape(-1) for n in SMALL_SHARDED + REPLICATED]) for p in ps])
    small = jnp.pad(flat, ((0, 0), (0, SMALL_ROWS * D_MODEL - flat.shape[1]))).reshape(k, SMALL_ROWS, D_MODEL)
    return jnp.concatenate([wout, small], axis=1)


def _unpack_state(buf, p):
    k = buf.shape[0]
    out = {"w_out": buf[:, 0:DEPTH * 128].reshape(k, DEPTH, 128, D_MODEL)}
    flat = buf[:, DEPTH * 128:].reshape(k, SMALL_ROWS * D_MODEL)
    off = 0
    for n in SMALL_SHARDED + REPLICATED:
        size = int(np.prod(p[n].shape))
        out[n] = flat[:, off:off + size].reshape((k,) + p[n].shape)
        off += size
    return out


def _rows_of_w_in_t(lo, hi, padded, kr):
    segs = ((0, 512, padded, NF + C_AQ), (512, 1664, padded, C_BB), (1664, 1696, kr, 0),
            (1696, 2464, padded, NF + C_DQ), (2464, 3488, padded, C_GATE))
    out = []
    for s0, s1, src, base in segs:
        a, b = max(lo, s0), min(hi, s1)
        if a < b:
            out.append(src[base + a - s0:base + b - s0])
    return out


def _me():
    return lax.axis_index("x"), lax.axis_index("y"), lax.axis_index("c")


def _all_gather(block):
    R, C = block.shape

    def body(src_ref, out_ref, send_sems, recv_sems, local_sem):
        x, y, c = _me()
        me, sibling = (x, y, c), (x, y, 1 - c)
        chips = [(1 - x, y), (x, 1 - y), (1 - x, 1 - y)]

        def slot(px, py, pc):
            return out_ref.at[4 * px + 2 * py + pc]

        def copy(k, block, to, src=None):
            return pltpu.make_async_remote_copy(
                src_ref=slot(*block) if src is None else src, dst_ref=slot(*block), send_sem=send_sems.at[k],
                recv_sem=recv_sems.at[k], device_id=to, device_id_type=MESH)

        mine = pltpu.make_async_copy(src_ref, slot(*me), local_sem)
        mine.start()
        first = [copy(0, me, sibling, src=src_ref)]
        first += [copy(1 + j, me, (*chip, c), src=src_ref) for j, chip in enumerate(chips)]
        for cp in first:
            cp.start()
        passed = [copy(4 + j, (*chip, c), sibling) for j, chip in enumerate(chips)]
        for j, chip in enumerate(chips):
            copy(1 + j, (*chip, c), me).wait_recv()
            passed[j].start()
        copy(0, sibling, me).wait_recv()
        for j, chip in enumerate(chips):
            copy(4 + j, (*chip, 1 - c), me).wait_recv()
        for cp in first + passed:
            cp.wait_send()
        mine.wait()

    return pl.pallas_call(
        body, name="all_gather", out_shape=jax.ShapeDtypeStruct((N_DEV, R, C), block.dtype),
        in_specs=[pl.BlockSpec(memory_space=pl.ANY)], out_specs=pl.BlockSpec(memory_space=pl.ANY),
        scratch_shapes=[pltpu.SemaphoreType.DMA((N_DEV - 1,)), pltpu.SemaphoreType.DMA((N_DEV - 1,)),
                        pltpu.SemaphoreType.DMA])(block)


N_CHIP = 4


def _sibling_swap(blocks):
    _, R, C = blocks.shape

    def body(src_ref, out_ref, send_sems, recv_sems):
        x, y, c = _me()
        copies = [pltpu.make_async_remote_copy(
            src_ref=src_ref.at[2 * j + 1 - c], dst_ref=out_ref.at[j], send_sem=send_sems.at[j],
            recv_sem=recv_sems.at[j], device_id=(x, y, 1 - c), device_id_type=MESH) for j in range(N_CHIP)]
        for cp in copies:
            cp.start()
        for cp in copies:
            cp.wait()

    return pl.pallas_call(
        body, name="sibling_swap", out_shape=jax.ShapeDtypeStruct((N_CHIP, R, C), blocks.dtype),
        in_specs=[pl.BlockSpec(memory_space=pl.ANY)], out_specs=pl.BlockSpec(memory_space=pl.ANY),
        scratch_shapes=[pltpu.SemaphoreType.DMA((N_CHIP,)), pltpu.SemaphoreType.DMA((N_CHIP,))])(blocks)


def _pair_sum(a, b):
    n, R, C = a.shape
    tr = 592 if R % 592 == 0 else R

    def body(a_ref, b_ref, o_ref):
        o_ref[...] = (a_ref[...].astype(F32) + b_ref[...].astype(F32)).astype(BF)

    spec = pl.BlockSpec((1, tr, C), lambda j, r: (j, r, 0))
    return pl.pallas_call(body, name="pair_sum", grid=(n, R // tr), in_specs=[spec, spec], out_specs=spec,
                          out_shape=jax.ShapeDtypeStruct(a.shape, BF), compiler_params=_params(2))(a, b)


def _chip_exchange(sums):
    _, R, C = sums.shape

    def body(src_ref, out_ref, send_sems, recv_sems, local_sem):
        x, y, c = _me()
        here = 2 * x + y
        mine = pltpu.make_async_copy(src_ref.at[here], out_ref.at[here], local_sem)
        mine.start()
        copies = []
        for k in range(1, N_CHIP):
            px, py = x ^ (k >> 1), y ^ (k & 1)
            copies.append(pltpu.make_async_remote_copy(
                src_ref=src_ref.at[2 * px + py], dst_ref=out_ref.at[here], send_sem=send_sems.at[k - 1],
                recv_sem=recv_sems.at[k - 1], device_id=(px, py, c), device_id_type=MESH))
        for cp in copies:
            cp.start()
        for cp in copies:
            cp.wait()
        mine.wait()

    return pl.pallas_call(
        body, name="chip_exchange", out_shape=jax.ShapeDtypeStruct((N_CHIP, R, C), sums.dtype),
        in_specs=[pl.BlockSpec(memory_space=pl.ANY)], out_specs=pl.BlockSpec(memory_space=pl.ANY),
        scratch_shapes=[pltpu.SemaphoreType.DMA((N_CHIP - 1,)), pltpu.SemaphoreType.DMA((N_CHIP - 1,)),
                        pltpu.SemaphoreType.DMA])(sums)


def _adamw_update(g, w, m, v):
    m_ = ADAM_B1 * m + (1.0 - ADAM_B1) * g
    v_ = ADAM_B2 * v + (1.0 - ADAM_B2) * (g * g)
    m_hat = m_ / (1.0 - ADAM_B1 ** ADAM_STEP)
    v_hat = v_ / (1.0 - ADAM_B2 ** ADAM_STEP)
    return -ADAM_LR * (m_hat / (jnp.sqrt(v_hat) + ADAM_EPS) + ADAM_WD * w), m_, v_


def _adamw(parts, state):
    _, R, C = state.shape
    n_parts = parts.shape[0]
    tr = 16
    assert R % tr == 0 and TAIL_ROW0 % tr == 0

    def body(p_ref, s_ref, o_ref):
        g = p_ref[0].astype(F32)
        for k in range(1, n_parts):
            g = g + p_ref[k].astype(F32)
        o_ref[0] = g
        o_ref[1], o_ref[2], o_ref[3] = _adamw_update(g, s_ref[0], s_ref[1], s_ref[2])

    return pl.pallas_call(
        body, name="adamw", grid=(R // tr,),
        in_specs=[pl.BlockSpec((n_parts, tr, C), lambda n: (0, n + TAIL_ROW0 // tr, 0)),
                  pl.BlockSpec((3, tr, C), lambda n: (0, n, 0))],
        out_specs=pl.BlockSpec((4, tr, C), lambda n: (0, n, 0)), out_shape=jax.ShapeDtypeStruct((4, R, C), F32),
        compiler_params=_params(1))(parts, state)


def _adamw_w_in(parts, w, m, v, core):
    n_parts = parts.shape[0]
    tc = 256

    def body(core_ref, p_ref, w_ref, m_ref, v_ref, o_ref):
        g_t = p_ref[0].astype(F32)
        for k in range(1, n_parts):
            g_t = g_t + p_ref[k].astype(F32)
        g_t = jnp.concatenate([g_t, jnp.zeros((512 - W_IN_WIN, tc), F32)], axis=0).T
        g = jnp.where(core_ref[0] == 0, g_t[:, 0:W_IN_COLS], g_t[:, W_IN_WIN - W_IN_COLS:W_IN_WIN])
        o_ref[0, 0] = g
        o_ref[1, 0], o_ref[2, 0], o_ref[3, 0] = _adamw_update(g, w_ref[0], m_ref[0], v_ref[0])

    nat = pl.BlockSpec((1, tc, W_IN_COLS), lambda l, j: (l, j, 0))
    return pl.pallas_call(
        body, name="adamw_w_in", grid=(DEPTH, D_MODEL // tc),
        in_specs=[pl.BlockSpec(memory_space=pltpu.SMEM),
                  pl.BlockSpec((n_parts, W_IN_WIN, tc), lambda l, j: (0, l, j)), nat, nat, nat],
        out_specs=pl.BlockSpec((4, 1, tc, W_IN_COLS), lambda l, j: (0, l, j, 0)),
        out_shape=jax.ShapeDtypeStruct((4, DEPTH, D_MODEL, W_IN_COLS), F32),
        compiler_params=_params(2))(core, parts, w, m, v)


def kernel(x, positions, norm_pre, w_in, attn_sinks, conv_w, conv_b, mla_q_norm, mla_w_uq, mla_kv_norm, mla_w_ukv, group_norm, w_out, norm_post, loss_target, m_norm_pre, m_w_in, m_attn_sinks, m_conv_w, m_conv_b, m_mla_q_norm, m_mla_w_uq, m_mla_kv_norm, m_mla_w_ukv, m_group_norm, m_w_out, m_norm_post, v_norm_pre, v_w_in, v_attn_sinks, v_conv_w, v_conv_b, v_mla_q_norm, v_mla_w_uq, v_mla_kv_norm, v_mla_w_ukv, v_group_norm, v_w_out, v_norm_post):
    local = dict(norm_pre=norm_pre, w_in=w_in, attn_sinks=attn_sinks, conv_w=conv_w, conv_b=conv_b,
                 mla_q_norm=mla_q_norm, mla_w_uq=mla_w_uq, mla_kv_norm=mla_kv_norm, mla_w_ukv=mla_w_ukv,
                 group_norm=group_norm, w_out=w_out, norm_post=norm_post)
    mom = dict(norm_pre=m_norm_pre, w_in=m_w_in, attn_sinks=m_attn_sinks, conv_w=m_conv_w, conv_b=m_conv_b,
               mla_q_norm=m_mla_q_norm, mla_w_uq=m_mla_w_uq, mla_kv_norm=m_mla_kv_norm, mla_w_ukv=m_mla_w_ukv,
               group_norm=m_group_norm, w_out=m_w_out, norm_post=m_norm_post)
    vel = dict(norm_pre=v_norm_pre, w_in=v_w_in, attn_sinks=v_attn_sinks, conv_w=v_conv_w, conv_b=v_conv_b,
               mla_q_norm=v_mla_q_norm, mla_w_uq=v_mla_w_uq, mla_kv_norm=v_mla_kv_norm, mla_w_ukv=v_mla_w_ukv,
               group_norm=v_group_norm, w_out=v_w_out, norm_post=v_norm_post)

    c = lax.axis_index("c")

    tile = 16
    slot_rows = 464
    shift = 8 * lax.axis_index("y") + 4 * c
    wt = lax.dynamic_update_slice(jnp.zeros((DEPTH, slot_rows, D_MODEL), BF),
                                  jnp.transpose(w_in, (0, 2, 1)).astype(BF), (0, shift, 0))
    payload = jnp.concatenate([wt.reshape(DEPTH * slot_rows, D_MODEL),
                               w_out.astype(BF).reshape(DEPTH * 128, D_MODEL),
                               _pack_small([local[n] for n in SMALL_SHARDED], BF)], axis=0)
    gathered = _all_gather(payload)
    W = {n: local[n] for n in REPLICATED}

    def nat_rows(l, lo, hi):
        def piece(d, r0, r1):
            base = slot_rows * l - (W_IN_COLS * d) // tile * tile
            return gathered[d, base + r0:base + r1]

        out, run = [], None
        for r0 in range(lo, hi, tile):
            d0, d1 = r0 // W_IN_COLS, (r0 + tile - 1) // W_IN_COLS
            if d0 == d1 and run is not None and run[0] == d0:
                run = (d0, run[1], r0 + tile)
                continue
            if run is not None:
                out.append(piece(*run))
                run = None
            if d0 == d1:
                run = (d0, r0, r0 + tile)
            else:
                out.append(piece(d0, r0, r0 + tile) + piece(d1, r0, r0 + tile))
        if run is not None:
            out.append(piece(*run))
        return out

    z = lambda n: [jnp.zeros((n, D_MODEL), BF)]
    W["wpt"] = [jnp.concatenate(nat_rows(l, 2464, 3488) + nat_rows(l, 512, 1664) + z(64) + nat_rows(l, 1664, 1696)
                                + z(96) + nat_rows(l, 1680, 1696) + nat_rows(l, 1664, 1680) + z(32)
                                + nat_rows(l, 0, 512) + nat_rows(l, 1696, 2464), axis=0) for l in range(DEPTH)]
    wo0 = DEPTH * slot_rows
    W["w_out"] = gathered[:, wo0:wo0 + DEPTH * 128].reshape(N_DEV, DEPTH, 128, D_MODEL).transpose(1, 0, 2, 3).reshape(
        DEPTH, D_MODEL, D_MODEL)
    flat = gathered[:, wo0 + DEPTH * 128:].reshape(N_DEV, SMALL_ROWS * D_MODEL)
    off = 0
    for n in SMALL_SHARDED:
        depth, rows, width = local[n].shape
        size = depth * rows * width
        W[n] = flat[:, off:off + size].reshape(N_DEV, depth, rows, width).transpose(1, 2, 0, 3).reshape(
            depth, rows, N_DEV * width)
        off += size

    sq, grad_x, g = _local_step(x[0], positions[0], W, loss_target[0])
    loss = lax.psum(0.5 / D_MODEL * jnp.sum(sq), ("x", "y", "c"))

    cols = []
    for n in SMALL_SHARDED:
        depth, rows, width = local[n].shape
        cols.append(jnp.stack(g[n]).reshape(depth, rows, N_DEV, width).transpose(2, 0, 1, 3).reshape(N_DEV, -1))
    rep = jnp.concatenate([a.reshape(-1) for n in REPLICATED for a in g[n]])
    cols.append(jnp.broadcast_to(rep[None], (N_DEV, rep.shape[0])))
    small = jnp.concatenate(cols, axis=1)
    small = jnp.pad(small, ((0, 0), (0, SMALL_ROWS * D_MODEL - small.shape[1]))).reshape(N_DEV, SMALL_ROWS, D_MODEL)
    krs = [p[C_CKR + 64:C_CKR + 96] + _swap_rows32(p[C_CKRS + 64:C_CKRS + 96]) for p in g["w_in_pt"]]
    pieces = []
    for d in range(N_DEV):
        lo = W_IN_COLS * d // 8 * 8
        for l in range(DEPTH):
            pieces += _rows_of_w_in_t(lo, lo + W_IN_WIN, g["w_in_pt"][l], krs[l])
        pieces += [g["w_out"][l][128 * d:128 * (d + 1)] for l in range(DEPTH)]
        pieces.append(small[d])
    blocks = jnp.concatenate(pieces, axis=0).astype(BF).reshape(N_DEV, -1, D_MODEL)
    mine = lax.dynamic_index_in_dim(blocks.reshape(N_CHIP, 2, -1, D_MODEL), c, axis=1, keepdims=False)
    received = _chip_exchange(_pair_sum(mine, _sibling_swap(blocks)))

    out = _unpack_state(_adamw(received, _pack_state([local, mom, vel])), local)
    out["w_in"] = _adamw_w_in(received, w_in, m_w_in, v_w_in, c.astype(jnp.int32).reshape(1))
    return (loss, grad_x[None], *[out[n][t] for t in range(4) for n in ORDER])
```

```python
import functools

import jax
import jax.numpy as jnp
import numpy as np
from jax import lax
from jax.experimental import pallas as pl
from jax.experimental.pallas import tpu as pltpu

F32 = jnp.float32
BF = jnp.bfloat16
MESH = pl.DeviceIdType.MESH

D_MODEL = 1024
DEPTH = 2
EPS = 1e-6
N_DEV = 8
VMEM_LIMIT = 56 * 1024 * 1024
NEG = -1e30
MLA_SCALE = 96.0 ** -0.5
SB_SCALE = 0.125
LOG2E = 1.4426950408889634

NP = 3712
NF = 2432
NB = 1280
C_GATE = 0
C_BB = 1024
C_BC = 1280
C_BX = 1536
C_CQ = 1792
C_CKV = 2048
C_CKR = 2176
C_CKRS = 2304
C_AQ = 0
C_AK = 256
C_AV = 384
C_DQ = 512
C_DK = 768
C_DV = 1024
C_END = 3712

def _swap32(a):
    return jnp.concatenate([a[:, 16:32], a[:, 0:16]], axis=1)

ADAM_LR, ADAM_B1, ADAM_B2, ADAM_EPS, ADAM_WD, ADAM_STEP = 0.001, 0.9, 0.999, 1e-08, 0.01, 10


def _dot(a, b):
    return jnp.dot(a, b, preferred_element_type=F32)


def _dot_nt(a, b):
    return lax.dot_general(a, b, (((1,), (1,)), ((), ())), preferred_element_type=F32)


def _dot_tn(a, b):
    return lax.dot_general(a, b, (((0,), (0,)), ((), ())), preferred_element_type=F32)


def _params(n_grid):
    return pltpu.CompilerParams(dimension_semantics=("arbitrary",) * n_grid, vmem_limit_bytes=VMEM_LIMIT)


def _rms_fwd(x, g):
    r = lax.rsqrt(jnp.mean(x * x, axis=-1, keepdims=True) + EPS)
    return (x * r) * g, r


def _rms_bwd(x, g, r, dy, width=None):
    n = x.shape[-1] if width is None else width
    u = dy * g
    dx = r * u - x * (r * r * r) * (jnp.sum(x * u, axis=-1, keepdims=True) / n)
    return dx, dy * (x * r)


def _iota(shape, axis):
    return lax.broadcasted_iota(jnp.int32, shape, axis)


def _inproj_fwd(x, g, wpt):
    T = x.shape[0]
    tm = 256

    def body(x_ref, g_ref, w_ref, h32_ref, h16_ref, xn_ref):
        xn, _ = _rms_fwd(x_ref[...], g_ref[...])
        xn = xn.astype(BF)
        xn_ref[...] = xn
        h = _dot_nt(xn, w_ref[...])
        h32_ref[...] = h[:, :NF]
        h16_ref[...] = h[:, NF:].astype(BF)

    return pl.pallas_call(
        body, name="inproj_fwd", grid=(T // tm,),
        in_specs=[pl.BlockSpec((tm, D_MODEL), lambda n: (n, 0)),
                  pl.BlockSpec((1, D_MODEL), lambda n: (0, 0)),
                  pl.BlockSpec((NP, D_MODEL), lambda n: (0, 0))],
        out_specs=[pl.BlockSpec((tm, NF), lambda n: (n, 0)),
                   pl.BlockSpec((tm, NB), lambda n: (n, 0)),
                   pl.BlockSpec((tm, D_MODEL), lambda n: (n, 0))],
        out_shape=[jax.ShapeDtypeStruct((T, NF), F32), jax.ShapeDtypeStruct((T, NB), BF),
                   jax.ShapeDtypeStruct((T, D_MODEL), BF)],
        compiler_params=_params(1))(x, g, wpt)


def _inproj_bwd(parts, wpt, x, xn, g, dxo):
    T = x.shape[0]
    tm = 256
    np_ = len(parts)
    chunks = (0, 1280, 2560, NP)
    assert sum(p.shape[1] for p in parts) == C_END == NP

    def body(*refs):
        part_refs = refs[:np_]
        w_ref, x_ref, xn_ref, g_ref, dxo_ref, dw_ref, dx_ref, dg_ref = refs[np_:]
        n = pl.program_id(0)

        @pl.when(n == 0)
        def _():
            dw_ref[...] = jnp.zeros_like(dw_ref)
            dg_ref[...] = jnp.zeros_like(dg_ref)

        dh = jnp.concatenate([r[...].astype(BF) for r in part_refs], axis=1)
        xnv = xn_ref[...]
        for lo, hi in zip(chunks[:-1], chunks[1:]):
            dw_ref[:, lo:hi] += _dot_tn(xnv, dh[:, lo:hi])
        dxn = _dot(dh, w_ref[...])
        xv = x_ref[...]
        _, r = _rms_fwd(xv, g_ref[...])
        dx, dgt = _rms_bwd(xv, g_ref[...], r, dxn)
        dx_ref[...] = dxo_ref[...] + dx
        dg_ref[...] += jnp.sum(dgt, axis=0, keepdims=True)

    once = pl.Buffered(1)
    return pl.pallas_call(
        body, name="inproj_bwd", grid=(T // tm,),
        in_specs=[pl.BlockSpec((tm, p.shape[1]), lambda n: (n, 0)) for p in parts]
        + [pl.BlockSpec((NP, D_MODEL), lambda n: (0, 0), pipeline_mode=once),
           pl.BlockSpec((tm, D_MODEL), lambda n: (n, 0)),
           pl.BlockSpec((tm, D_MODEL), lambda n: (n, 0)),
           pl.BlockSpec((1, D_MODEL), lambda n: (0, 0)),
           pl.BlockSpec((tm, D_MODEL), lambda n: (n, 0))],
        out_specs=[pl.BlockSpec((D_MODEL, NP), lambda n: (0, 0), pipeline_mode=once),
                   pl.BlockSpec((tm, D_MODEL), lambda n: (n, 0)),
                   pl.BlockSpec((1, D_MODEL), lambda n: (0, 0))],
        out_shape=[jax.ShapeDtypeStruct((D_MODEL, NP), F32), jax.ShapeDtypeStruct((T, D_MODEL), F32),
                   jax.ShapeDtypeStruct((1, D_MODEL), F32)],
        compiler_params=_params(1))(*parts, wpt, x, xn, g, dxo)


SWA_BLK = 128
SWA_TQ = 2048


def _bdot_nt(a, b):
    return lax.dot_general(a, b, (((2,), (2,)), ((0,), (0,))), preferred_element_type=F32)


def _bdot(a, b):
    return lax.dot_general(a, b, (((2,), (1,)), ((0,), (0,))), preferred_element_type=F32)


def _bdot_tn(a, b):
    return lax.dot_general(a, b, (((1,), (1,)), ((0,), (0,))), preferred_element_type=F32)


def _swa_probs(q, kc, kp, sink, mask_c, mask_p):
    sc = jnp.where(mask_c, _bdot_nt(q, kc) * SB_SCALE, NEG)
    sp = jnp.where(mask_p, _bdot_nt(q, kp) * SB_SCALE, NEG)
    m = jnp.maximum(jnp.maximum(jnp.max(sc, axis=-1, keepdims=True), jnp.max(sp, axis=-1, keepdims=True)), sink)
    pc = jnp.exp(sc - m)
    pp = jnp.exp(sp - m)
    ps = jnp.exp(sink - m)
    inv = 1.0 / (jnp.sum(pc, axis=-1, keepdims=True) + jnp.sum(pp, axis=-1, keepdims=True) + ps)
    return pc * inv, pp * inv, ps * inv


def _swa_masks(n, nb):
    blk = _iota((nb, SWA_BLK, SWA_BLK), 0)
    row = _iota((nb, SWA_BLK, SWA_BLK), 1)
    col = _iota((nb, SWA_BLK, SWA_BLK), 2)
    return col <= row, jnp.logical_and(col > row, jnp.logical_or(blk > 0, n > 0))


def _swa_specs(tq):
    halo = tq // SWA_BLK
    return [pl.BlockSpec(memory_space=pltpu.SMEM),
            pl.BlockSpec((tq, 256), lambda n: (n, C_AQ // 256)),
            pl.BlockSpec((tq, 128), lambda n: (n, C_AK // 128)),
            pl.BlockSpec((SWA_BLK, 128), lambda n: (jnp.maximum(n * halo - 1, 0), C_AK // 128)),
            pl.BlockSpec((tq, 128), lambda n: (n, C_AV // 128)),
            pl.BlockSpec((SWA_BLK, 128), lambda n: (jnp.maximum(n * halo - 1, 0), C_AV // 128))]


def _swa_blocked(cur_ref, prev_ref, gs, nb):
    cur = cur_ref[:, gs].reshape(nb, SWA_BLK, 64)
    prev = jnp.concatenate([prev_ref[:, gs].reshape(1, SWA_BLK, 64), cur[:nb - 1]], axis=0) if nb > 1 \
        else prev_ref[:, gs].reshape(1, SWA_BLK, 64)
    return cur, prev


def _swa_masks2(n, nb):
    mask_c, mask_p = _swa_masks(n, nb)
    return jnp.concatenate([mask_c, mask_c], axis=1), jnp.concatenate([mask_p, mask_p], axis=1)


def _swa_stacked(ref, g, nb):
    a = ref[:, 128 * g:128 * g + 64].reshape(nb, SWA_BLK, 64)
    b = ref[:, 128 * g + 64:128 * g + 128].reshape(nb, SWA_BLK, 64)
    return jnp.concatenate([a, b], axis=1)


def _swa_fwd(h16, sinks):
    T = h16.shape[0]
    tq = SWA_TQ if T % SWA_TQ == 0 else SWA_BLK
    nb = tq // SWA_BLK

    def body(s_ref, q_ref, kc_ref, kp_ref, vc_ref, vp_ref, o_ref):
        n = pl.program_id(0)
        mask_c, mask_p = _swa_masks2(n, nb)
        first = _iota((1, 2 * SWA_BLK, 1), 1) < SWA_BLK
        for g in range(2):
            gs = slice(g * 64, (g + 1) * 64)
            kc, kp = _swa_blocked(kc_ref, kp_ref, gs, nb)
            vc, vp = _swa_blocked(vc_ref, vp_ref, gs, nb)
            q = _swa_stacked(q_ref, g, nb)
            sink = jnp.where(first, s_ref[2 * g], s_ref[2 * g + 1])
            pc, pp, _ = _swa_probs(q, kc, kp, sink, mask_c, mask_p)
            o = _bdot(pc.astype(BF), vc) + _bdot(pp.astype(BF), vp)
            o_ref[:, 128 * g:128 * g + 64] = o[:, :SWA_BLK].reshape(tq, 64)
            o_ref[:, 128 * g + 64:128 * g + 128] = o[:, SWA_BLK:].reshape(tq, 64)

    return pl.pallas_call(
        body, name="swa_fwd", grid=(T // tq,), in_specs=_swa_specs(tq),
        out_specs=pl.BlockSpec((tq, 256), lambda n: (n, 0)),
        out_shape=jax.ShapeDtypeStruct((T, 256), F32),
        compiler_params=_params(1))(sinks, h16, h16, h16, h16, h16)


def _swa_bwd(h16, sinks, dya):
    T = h16.shape[0]
    tq = SWA_TQ if T % SWA_TQ == 0 else SWA_BLK
    nb = tq // SWA_BLK

    def body(s_ref, q_ref, kc_ref, kp_ref, vc_ref, vp_ref, do_ref, dq_ref, dk_ref, dv_ref, ds_ref):
        n = pl.program_id(0)

        @pl.when(n == 0)
        def _():
            dk_ref[...] = jnp.zeros_like(dk_ref)
            dv_ref[...] = jnp.zeros_like(dv_ref)
            ds_ref[...] = jnp.zeros_like(ds_ref)

        mask_c, mask_p = _swa_masks2(n, nb)
        first = _iota((1, 2 * SWA_BLK, 1), 1) < SWA_BLK
        rows = pl.ds(pl.multiple_of(n * tq, tq), tq)
        before = pl.ds(pl.multiple_of(jnp.maximum(n * nb - 1, 0) * SWA_BLK, SWA_BLK), SWA_BLK)
        lane = _iota((8, 128), 1)
        row8 = _iota((8, 128), 0)

        def to_keys(own, prev):
            if nb == 1:
                return own
            return own + jnp.concatenate([prev[1:], jnp.zeros((1, SWA_BLK, 64), F32)], axis=0)

        for g in range(2):
            gs = slice(g * 64, (g + 1) * 64)
            q = _swa_stacked(q_ref, g, nb)
            kc, kp = _swa_blocked(kc_ref, kp_ref, gs, nb)
            vc, vp = _swa_blocked(vc_ref, vp_ref, gs, nb)
            sink = jnp.where(first, s_ref[2 * g], s_ref[2 * g + 1])
            pc, pp, ps = _swa_probs(q, kc, kp, sink, mask_c, mask_p)
            pcb, ppb = pc.astype(BF), pp.astype(BF)
            do = _swa_stacked(do_ref, g, nb)
            dob = do.astype(BF)
            o = _bdot(pcb, vc) + _bdot(ppb, vp)
            dd = jnp.sum(do * o, axis=-1, keepdims=True)
            dsc = (pc * (_bdot_nt(dob, vc) - dd) * SB_SCALE).astype(BF)
            dsp = (pp * (_bdot_nt(dob, vp) - dd) * SB_SCALE).astype(BF)
            dq = (_bdot(dsc, kc) + _bdot(dsp, kp)).astype(BF)
            dq_ref[:, 128 * g:128 * g + 64] = dq[:, :SWA_BLK].reshape(tq, 64)
            dq_ref[:, 128 * g + 64:128 * g + 128] = dq[:, SWA_BLK:].reshape(tq, 64)
            dkp, dvp = _bdot_tn(dsp, q), _bdot_tn(ppb, dob)
            dk_ref[rows, gs] += to_keys(_bdot_tn(dsc, q), dkp).reshape(tq, 64)
            dv_ref[rows, gs] += to_keys(_bdot_tn(pcb, dob), dvp).reshape(tq, 64)
            dk_ref[before, gs] += dkp[0]
            dv_ref[before, gs] += dvp[0]
            psd = ps * dd
            for hh in range(2):
                part = psd[:, hh * SWA_BLK:(hh + 1) * SWA_BLK]
                ds_ref[...] += jnp.where(jnp.logical_and(lane == 2 * g + hh, row8 == 0), -jnp.sum(part), 0.0)

    return pl.pallas_call(
        body, name="swa_bwd", grid=(T // tq,),
        in_specs=_swa_specs(tq) + [pl.BlockSpec((tq, 256), lambda n: (n, 0))],
        out_specs=[pl.BlockSpec((tq, 256), lambda n: (n, 0)),
                   pl.BlockSpec((T, 128), lambda n: (0, 0)),
                   pl.BlockSpec((T, 128), lambda n: (0, 0)),
                   pl.BlockSpec((8, 128), lambda n: (0, 0))],
        out_shape=[jax.ShapeDtypeStruct((T, 256), BF), jax.ShapeDtypeStruct((T, 128), F32),
                   jax.ShapeDtypeStruct((T, 128), F32), jax.ShapeDtypeStruct((8, 128), F32)],
        compiler_params=_params(1))(sinks, h16, h16, h16, h16, h16, dya)


def _conv_u(bc_ref, bx_ref, bch_ref, bxh_ref, n, tm):
    u = bc_ref[...] * bx_ref[...]
    uh = bch_ref[...] * bxh_ref[...] * (n > 0).astype(F32)
    rowi = _iota((tm, 256), 0)
    u1 = jnp.where(rowi == 0, uh[7:8, :], pltpu.roll(u, 1, axis=0))
    u2 = jnp.where(rowi == 0, uh[6:7, :], jnp.where(rowi == 1, uh[7:8, :], pltpu.roll(u, 2, axis=0)))
    return u, u1, u2


def _conv_fwd(h32, cw, cb):
    T = h32.shape[0]
    tm = 1024 if T % 1024 == 0 else T
    hb = tm // 8

    def body(bb_ref, bc_ref, bx_ref, bch_ref, bxh_ref, w_ref, b_ref, o_ref):
        n = pl.program_id(0)
        u, u1, u2 = _conv_u(bc_ref, bx_ref, bch_ref, bxh_ref, n, tm)
        y = w_ref[0:1, :] * u2 + w_ref[1:2, :] * u1 + w_ref[2:3, :] * u + b_ref[...]
        o_ref[...] = bb_ref[...] * y

    halo = lambda c: pl.BlockSpec((8, 256), lambda n: (jnp.maximum(n * hb - 1, 0), c // 256))
    return pl.pallas_call(
        body, name="conv_fwd", grid=(T // tm,),
        in_specs=[pl.BlockSpec((tm, 256), lambda n: (n, C_BB // 256)),
                  pl.BlockSpec((tm, 256), lambda n: (n, C_BC // 256)),
                  pl.BlockSpec((tm, 256), lambda n: (n, C_BX // 256)),
                  halo(C_BC), halo(C_BX),
                  pl.BlockSpec((8, 256), lambda n: (0, 0)),
                  pl.BlockSpec((1, 256), lambda n: (0, 0))],
        out_specs=pl.BlockSpec((tm, 256), lambda n: (n, 0)),
        out_shape=jax.ShapeDtypeStruct((T, 256), F32),
        compiler_params=_params(1))(h32, h32, h32, h32, h32, cw, cb)


def _conv_bwd(h32, cw, cb, dyb):
    T = h32.shape[0]
    tm = 1024 if T % 1024 == 0 else T
    hb = tm // 8
    nt = T // tm

    def body(bb_ref, bc_ref, bx_ref, bch_ref, bxh_ref, bbn_ref, dy_ref, dyn_ref, w_ref, b_ref,
             dbb_ref, dbc_ref, dbx_ref, dw_ref):
        n = pl.program_id(0)

        @pl.when(n == 0)
        def _():
            dw_ref[...] = jnp.zeros_like(dw_ref)

        u, u1, u2 = _conv_u(bc_ref, bx_ref, bch_ref, bxh_ref, n, tm)
        w0, w1, w2 = w_ref[0:1, :], w_ref[1:2, :], w_ref[2:3, :]
        y = w0 * u2 + w1 * u1 + w2 * u + b_ref[...]
        dyb_ = dy_ref[...]
        dbb_ref[...] = (dyb_ * y).astype(BF)
        dy = dyb_ * bb_ref[...]
        dyn = dyn_ref[...] * bbn_ref[...] * (n < nt - 1).astype(F32)
        rowi = _iota((tm, 256), 0)
        dy1 = jnp.where(rowi == tm - 1, dyn[0:1, :], pltpu.roll(dy, tm - 1, axis=0))
        dy2 = jnp.where(rowi == tm - 2, dyn[0:1, :],
                        jnp.where(rowi == tm - 1, dyn[1:2, :], pltpu.roll(dy, tm - 2, axis=0)))
        du = w2 * dy + w1 * dy1 + w0 * dy2
        dbc_ref[...] = (du * bx_ref[...]).astype(BF)
        dbx_ref[...] = (du * bc_ref[...]).astype(BF)
        dw_ref[0:1, :] += jnp.sum(dy * u2, axis=0, keepdims=True)
        dw_ref[1:2, :] += jnp.sum(dy * u1, axis=0, keepdims=True)
        dw_ref[2:3, :] += jnp.sum(dy * u, axis=0, keepdims=True)
        dw_ref[3:4, :] += jnp.sum(dy, axis=0, keepdims=True)

    halo = lambda c: pl.BlockSpec((8, 256), lambda n: (jnp.maximum(n * hb - 1, 0), c // 256))
    nxt = lambda c: pl.BlockSpec((8, 256), lambda n: (jnp.minimum((n + 1) * hb, T // 8 - 1), c // 256))
    cur = lambda c: pl.BlockSpec((tm, 256), lambda n: (n, c // 256))
    return pl.pallas_call(
        body, name="conv_bwd", grid=(nt,),
        in_specs=[cur(C_BB), cur(C_BC), cur(C_BX), halo(C_BC), halo(C_BX), nxt(C_BB),
                  cur(0), nxt(0),
                  pl.BlockSpec((8, 256), lambda n: (0, 0)),
                  pl.BlockSpec((1, 256), lambda n: (0, 0))],
        out_specs=[cur(0), cur(0), cur(0), pl.BlockSpec((8, 256), lambda n: (0, 0))],
        out_shape=[jax.ShapeDtypeStruct((T, 256), BF)] * 3 + [jax.ShapeDtypeStruct((8, 256), F32)],
        compiler_params=_params(1))(h32, h32, h32, h32, h32, h32, dyb, dyb, cw, cb)


def _cprep_specs(tm):
    return [pl.BlockSpec((tm, 256), lambda n: (n, C_CQ // 256)),
            pl.BlockSpec((tm, 128), lambda n: (n, C_CKV // 128)),
            pl.BlockSpec((tm, 128), lambda n: (n, C_CKR // 128)),
            pl.BlockSpec((tm, 128), lambda n: (n, C_CKRS // 128)),
            pl.BlockSpec((1, 256), lambda n: (0, 0)),
            pl.BlockSpec((1, 128), lambda n: (0, 0)),
            pl.BlockSpec((tm, 128), lambda n: (n, 0)),
            pl.BlockSpec((tm, 128), lambda n: (n, 0))]


def _cprep_fwd(h32, gq, gkv, wuq2, wkv2, cosk, sin):
    T = h32.shape[0]
    tm = 1024 if T % 1024 == 0 else T

    def body(cq_ref, ckv_ref, ckr_ref, ckrs_ref, gq_ref, gkv_ref, cos_ref, sin_ref, wuq_ref, wkv_ref,
             q_ref, k_ref, v_ref):
        cosk_, sin_ = cos_ref[...], sin_ref[...]
        cosq = cosk_ + (_iota((tm, 128), 1) < 64).astype(F32)
        cqn, _ = _rms_fwd(cq_ref[...], gq_ref[...])
        q2 = _dot(cqn.astype(BF), wuq_ref[...])
        ckvn, _ = _rms_fwd(ckv_ref[...], gkv_ref[...])
        kv2 = _dot(ckvn.astype(BF), wkv_ref[...])
        kr = ckr_ref[...] * cosk_ + ckrs_ref[...] * sin_
        for h in range(4):
            hs = slice(h * 128, (h + 1) * 128)
            q_ref[:, hs] = ((q2[:, hs] * cosq + q2[:, 512 + h * 128:512 + (h + 1) * 128] * sin_) * MLA_SCALE).astype(BF)
            k_ref[:, hs] = (kv2[:, hs] + kr).astype(BF)
        ones = _iota((tm, 512), 1) % 128 == 64
        v_ref[...] = jnp.where(ones, 1.0, kv2[:, 512:]).astype(BF)

    def rin(w, c):
        return pl.BlockSpec((tm, w), lambda n: (n, c), pipeline_mode=pl.Buffered(3))

    def outer(h_h, gq_ref, gkv_ref, cos_h, sin_h, wuq_ref, wkv_ref, q_h, k_h, v_h):
        def inner(cq_ref, ckv_ref, ckr_ref, ckrs_ref, cos_ref, sin_ref, q_ref, k_ref, v_ref):
            body(cq_ref, ckv_ref, ckr_ref, ckrs_ref, gq_ref, gkv_ref, cos_ref, sin_ref, wuq_ref, wkv_ref,
                 q_ref, k_ref, v_ref)

        pltpu.emit_pipeline(
            inner, grid=(T // tm,),
            in_specs=[rin(256, C_CQ // 256), rin(128, C_CKV // 128), rin(128, C_CKR // 128), rin(128, C_CKRS // 128),
                      rin(128, 0), rin(128, 0)],
            out_specs=[pl.BlockSpec((tm, 512), lambda n: (n, 0))] * 3)(h_h, h_h, h_h, h_h, cos_h, sin_h, q_h, k_h, v_h)

    hbm = pl.BlockSpec(memory_space=pl.ANY)
    vmem = pl.BlockSpec(memory_space=pltpu.VMEM)
    return pl.pallas_call(
        outer, name="cprep_fwd", in_specs=[hbm, vmem, vmem, hbm, hbm, vmem, vmem], out_specs=[hbm] * 3,
        out_shape=[jax.ShapeDtypeStruct((T, 512), BF)] * 3,
        compiler_params=pltpu.CompilerParams(vmem_limit_bytes=VMEM_LIMIT))(h32, gq, gkv, cosk, sin, wuq2, wkv2)


def _cprep_bwd(h32, gq, gkv, wuq2t, wkv2t, cosk, sin, dq, dk, dv):
    T = h32.shape[0]
    tm = 1024 if T % 1024 == 0 else T

    def body(cq_ref, ckv_ref, ckr_ref, ckrs_ref, gq_ref, gkv_ref, cos_ref, sin_ref, wuq_ref, wkv_ref,
             dq_ref, dk_ref, dv_ref,
             dcq_ref, dckv_ref, dckr_ref, dckrs_ref, dwuq_ref, dwkv_ref, dgq_ref, dgkv_ref):
        n = pl.program_id(0)

        @pl.when(n == 0)
        def _():
            dwuq_ref[...] = jnp.zeros_like(dwuq_ref)
            dwkv_ref[...] = jnp.zeros_like(dwkv_ref)
            dgq_ref[...] = jnp.zeros_like(dgq_ref)
            dgkv_ref[...] = jnp.zeros_like(dgkv_ref)

        cosk_, sin_ = cos_ref[...], sin_ref[...]
        cosq = cosk_ + (_iota((tm, 128), 1) < 64).astype(F32)
        dkr = jnp.zeros((tm, 128), F32)
        plain, swapped = [], []
        for h in range(4):
            hs = slice(h * 128, (h + 1) * 128)
            dqh = dq_ref[:, hs] * MLA_SCALE
            plain.append((dqh * cosq).astype(BF))
            swapped.append((dqh * sin_).astype(BF))
            dkr = dkr + dk_ref[:, hs]
        dq2 = jnp.concatenate(plain + swapped, axis=1)
        dkv2 = jnp.concatenate([dk_ref[...].astype(BF), dv_ref[...].astype(BF)], axis=1)
        dckr_ref[...] = (dkr * cosk_).astype(BF)
        dckrs_ref[...] = (dkr * sin_).astype(BF)

        cq, gq_ = cq_ref[...], gq_ref[...]
        cqn, rq = _rms_fwd(cq, gq_)
        dwuq_ref[...] += _dot_tn(cqn.astype(BF), dq2)
        dcq, dgt = _rms_bwd(cq, gq_, rq, _dot(dq2, wuq_ref[...]))
        dcq_ref[...] = dcq.astype(BF)
        dgq_ref[...] += jnp.sum(dgt, axis=0, keepdims=True)

        ckv, gkv_ = ckv_ref[...], gkv_ref[...]
        ckvn, rkv = _rms_fwd(ckv, gkv_)
        dwkv_ref[...] += _dot_tn(ckvn.astype(BF), dkv2)
        dckv, dgt2 = _rms_bwd(ckv, gkv_, rkv, _dot(dkv2, wkv_ref[...]))
        dckv_ref[...] = dckv.astype(BF)
        dgkv_ref[...] += jnp.sum(dgt2, axis=0, keepdims=True)

    row = lambda w: pl.BlockSpec((tm, w), lambda n: (n, 0))
    return pl.pallas_call(
        body, name="cprep_bwd", grid=(T // tm,),
        in_specs=_cprep_specs(tm) + [pl.BlockSpec((1024, 256), lambda n: (0, 0)),
                                     pl.BlockSpec((1024, 128), lambda n: (0, 0)),
                                     row(512), row(512), row(512)],
        out_specs=[row(256), row(128), row(128), row(128),
                   pl.BlockSpec((256, 1024), lambda n: (0, 0)), pl.BlockSpec((128, 1024), lambda n: (0, 0)),
                   pl.BlockSpec((1, 256), lambda n: (0, 0)), pl.BlockSpec((1, 128), lambda n: (0, 0))],
        out_shape=[jax.ShapeDtypeStruct((T, 256), BF), jax.ShapeDtypeStruct((T, 128), BF),
                   jax.ShapeDtypeStruct((T, 128), BF), jax.ShapeDtypeStruct((T, 128), BF),
                   jax.ShapeDtypeStruct((256, 1024), F32), jax.ShapeDtypeStruct((128, 1024), F32),
                   jax.ShapeDtypeStruct((1, 256), F32), jax.ShapeDtypeStruct((1, 128), F32)],
        compiler_params=_params(1))(h32, h32, h32, h32, gq, gkv, cosk, sin, wuq2t, wkv2t, dq, dk, dv)


MLA_TILE = 512
MLA_HEADS_PER_STEP = 4


def _causal_mask(t):
    return _iota((t, t), 1) <= _iota((t, t), 0)


def _mla_fwd(q, k, v):
    T = q.shape[0]
    tq = MLA_TILE

    def body(q_ref, k_ref, v_ref, o_ref, lse_ref):
        i = pl.program_id(1)
        mask = _causal_mask(tq)
        heads = [slice(128 * h, 128 * h + 128) for h in range(MLA_HEADS_PER_STEP)]
        qs = [q_ref[:, hs] for hs in heads]

        def step(j, carry, masked):
            rows = pl.ds(pl.multiple_of(j * tq, tq), tq)
            out = []
            for hh, hs in enumerate(heads):
                m, acc = carry[hh]
                s = _dot_nt(qs[hh], k_ref[rows, hs])
                if masked:
                    s = jnp.where(mask, s, NEG)
                m_new = jnp.maximum(m, jnp.max(s, axis=-1, keepdims=True))
                p = jnp.exp((s - m_new).astype(BF))
                acc = jnp.exp(m - m_new) * acc + _dot(p, v_ref[rows, hs])
                out.append((m_new, acc))
            return tuple(out)

        init = ((jnp.full((tq, 1), NEG, F32), jnp.zeros((tq, 128), F32)),) * len(heads)
        carry = lax.fori_loop(0, i // 2, lambda t, c: step(2 * t + 1, step(2 * t, c, False), False), init)
        carry = lax.cond(i % 2 == 1, lambda c: step(i - 1, c, False), lambda c: c, carry)
        carry = step(i, carry, True)
        for hh, hs in enumerate(heads):
            m, acc = carry[hh]
            l = acc[:, 64:65]
            o_ref[:, hs] = acc * (1.0 / l)
            lse_ref[:, hs] = jnp.broadcast_to(m + jnp.log(l), (tq, 128))

    width = 128 * MLA_HEADS_PER_STEP
    blk = pl.BlockSpec((tq, width), lambda h, i: (i, h))
    full = pl.BlockSpec((T, width), lambda h, i: (0, h))
    return pl.pallas_call(
        body, name="mla_fwd", grid=(4 // MLA_HEADS_PER_STEP, T // tq), in_specs=[blk, full, full],
        out_specs=[blk, blk],
        out_shape=[jax.ShapeDtypeStruct((T, 512), F32), jax.ShapeDtypeStruct((T, 512), F32)],
        compiler_params=_params(2))(q, k, v)


def _mla_bwd(q, k, v, o, lse, do):
    T = q.shape[0]
    tq = MLA_TILE

    def body(q_ref, k_ref, v_ref, o_ref, lse_ref, do_ref, dq_ref, dk_ref, dv_ref):
        i = pl.program_id(1)

        @pl.when(i == 0)
        def _():
            dk_ref[...] = jnp.zeros_like(dk_ref)
            dv_ref[...] = jnp.zeros_like(dv_ref)

        heads = [slice(0, 128), slice(128, 256)]
        mask = _causal_mask(tq)
        qs, dobs, dds, lses = [], [], [], []
        for hs in heads:
            do = do_ref[:, hs]
            qs.append(q_ref[:, hs])
            dobs.append(do.astype(BF))
            dds.append(jnp.sum(do * o_ref[:, hs], axis=-1, keepdims=True))
            lses.append(lse_ref[:, hs.start:hs.start + 1])

        def step(j, dqs, masked):
            rows = pl.ds(pl.multiple_of(j * tq, tq), tq)
            out = []
            for hh, hs in enumerate(heads):
                kj, vj = k_ref[rows, hs], v_ref[rows, hs]
                s = _dot_nt(qs[hh], kj)
                if masked:
                    s = jnp.where(mask, s, NEG)
                p = jnp.exp(s - lses[hh])
                ds = (p * (_dot_nt(dobs[hh], vj) - dds[hh])).astype(BF)
                dk_ref[rows, hs] += _dot_tn(ds, qs[hh])
                dv_ref[rows, hs] += _dot_tn(p.astype(BF), dobs[hh])
                out.append(dqs[hh] + _dot(ds, kj))
            return tuple(out)

        dqs = lax.fori_loop(0, i // 2, lambda t, c: step(2 * t + 1, step(2 * t, c, False), False),
                            (jnp.zeros((tq, 128), F32),) * 2)
        dqs = lax.cond(i % 2 == 1, lambda c: step(i - 1, c, False), lambda c: c, dqs)
        dqs = step(i, dqs, True)
        for hh, hs in enumerate(heads):
            dq_ref[:, hs] = dqs[hh]

    blk = pl.BlockSpec((tq, 256), lambda h, i: (i, h))
    full = pl.BlockSpec((T, 256), lambda h, i: (0, h), pipeline_mode=pl.Buffered(1))
    return pl.pallas_call(
        body, name="mla_bwd", grid=(2, T // tq), in_specs=[blk, full, full, blk, blk, blk],
        out_specs=[blk, full, full],
        out_shape=[jax.ShapeDtypeStruct((T, 512), F32)] * 3,
        compiler_params=_params(2))(q, k, v, o, lse, do)


def _sb_tile(qk, rr, strict, masked, upper):
    z2 = qk * (SB_SCALE * LOG2E)
    l1 = jnp.log2(1.0 + jnp.exp2(-jnp.abs(z2)))
    lk = -jnp.maximum(z2, 0.0) - l1
    if masked:
        lk = jnp.where(strict, lk, 0.0)
    after = rr + _dot(lk.astype(BF), upper)
    ll = jnp.minimum(z2, 0.0) - l1
    a = jnp.exp2(ll + after)
    if masked:
        a = jnp.where(strict, a, 0.0)
    return ll, a, jnp.sum(lk, axis=-1, keepdims=True)


SB_TQ, SB_TK = 256, 256
SB_DEAD = -160.0


def _sb_walk(trips, one_step, carry):
    def alive(c):
        t, cr = c
        top = functools.reduce(jnp.maximum, [jnp.max(h[0]) for h in cr])
        return jnp.logical_and(t < trips, top > SB_DEAD)

    def body(c):
        t, cr = c
        return t + 1, one_step(t, cr)

    return lax.while_loop(alive, body, (jnp.int32(0), carry))[1]


def _sb_consts(tq, tk):
    row, col = _iota((tq, tk), 0), _iota((tq, tk), 1)
    strict = [col + d * tk < row for d in range(tq // tk)]
    r2, c2 = _iota((tk, tk), 0), _iota((tk, tk), 1)
    return strict, (r2 > c2).astype(BF), (r2 < c2).astype(BF)


def _sb_fwd(h16):
    T = h16.shape[0]
    tq, tk = SB_TQ, SB_TK
    nd = tq // tk

    def body(q0_ref, q1_ref, k0_ref, k1_ref, v0_ref, v1_ref, o_ref):
        i = pl.program_id(0)
        strict, upper, _ = _sb_consts(tq, tk)
        lane = _iota((tq, 128), 1)
        pairs = [slice(0, 128), slice(128, 256)]
        k_refs, v_refs = (k0_ref, k1_ref), (v0_ref, v1_ref)
        qms = []
        for q_ref in (q0_ref, q1_ref):
            q2 = q_ref[...]
            qms += [jnp.where(lane < 64, q2, jnp.zeros_like(q2)), jnp.where(lane >= 64, q2, jnp.zeros_like(q2))]

        def step(j, carry, d):
            rows = pl.ds(pl.multiple_of(j * tk, tk), tk)
            out = []
            for h in range(4):
                rr, acc = carry[h]
                _, a, rs = _sb_tile(_dot_nt(qms[h], k_refs[h // 2][rows, :]), rr, None if d is None else strict[d],
                                    d is not None, upper)
                out.append((rr + rs, acc + _dot(a.astype(BF), v_refs[h // 2][rows, :])))
            return tuple(out)

        carry = ((jnp.zeros((tq, 1), F32), jnp.zeros((tq, 128), F32)),) * 4
        for d in reversed(range(nd)):
            carry = step(nd * i + d, carry, d)
        carry = _sb_walk(nd * i, lambda t, c: step(nd * i - 1 - t, c, None), carry)
        for p, ps in enumerate(pairs):
            o_ref[:, ps] = jnp.where(lane < 64, carry[2 * p][1], carry[2 * p + 1][1])

    return pl.pallas_call(
        body, name="sb_fwd", grid=(T // tq,),
        in_specs=[pl.BlockSpec((tq, 128), lambda i: (i, C_DQ // 128)),
                  pl.BlockSpec((tq, 128), lambda i: (i, C_DQ // 128 + 1)),
                  pl.BlockSpec((T, 128), lambda i: (0, C_DK // 128)),
                  pl.BlockSpec((T, 128), lambda i: (0, C_DK // 128 + 1)),
                  pl.BlockSpec((T, 128), lambda i: (0, C_DV // 128)),
                  pl.BlockSpec((T, 128), lambda i: (0, C_DV // 128 + 1))],
        out_specs=pl.BlockSpec((tq, 256), lambda i: (i, 0)),
        out_shape=jax.ShapeDtypeStruct((T, 256), F32),
        compiler_params=_params(1))(h16, h16, h16, h16, h16, h16)


def _sb_bwd(h16, yd, dyd):
    T = h16.shape[0]
    tq, tk = SB_TQ, SB_TK
    nd = tq // tk

    def body(q0_ref, q1_ref, k0_ref, k1_ref, v0_ref, v1_ref, o_ref, do_ref, dq_ref, dk_ref, dv_ref):
        i = pl.program_id(0)

        @pl.when(i == 0)
        def _():
            dk_ref[...] = jnp.zeros_like(dk_ref)
            dv_ref[...] = jnp.zeros_like(dv_ref)

        strict, upper, before = _sb_consts(tq, tk)
        lane = _iota((tq, 128), 1)
        lane_k = _iota((tk, 128), 1)
        pairs = [slice(0, 128), slice(128, 256)]
        k_refs, v_refs = (k0_ref, k1_ref), (v0_ref, v1_ref)
        q2s, dob2s, qms, doms, dds = [], [], [], [], []
        for p, q_ref in enumerate((q0_ref, q1_ref)):
            q2 = q_ref[...]
            dob2 = do_ref[:, pairs[p]].astype(BF)
            doo = dob2.astype(F32) * o_ref[:, pairs[p]]
            q2s.append(q2)
            dob2s.append(dob2)
            for mine in (lane < 64, lane >= 64):
                qms.append(jnp.where(mine, q2, jnp.zeros_like(q2)))
                doms.append(jnp.where(mine, dob2, jnp.zeros_like(dob2)))
                dds.append(jnp.sum(jnp.where(mine, doo, 0.0), axis=-1, keepdims=True))

        def step(j, carry, d):
            rows = pl.ds(pl.multiple_of(j * tk, tk), tk)
            out, dks, dvs = [], [], []
            for h in range(4):
                kj, vj = k_refs[h // 2][rows, :], v_refs[h // 2][rows, :]
                rr, sg, dq = carry[h]
                ll, a, rs = _sb_tile(_dot_nt(qms[h], kj), rr, None if d is None else strict[d], d is not None, upper)
                ab = a.astype(BF)
                g = _dot_nt(doms[h], vj) * ab.astype(F32)
                gs = jnp.sum(g, axis=-1, keepdims=True)
                pre = (dds[h] - sg - gs) + _dot(g.astype(BF), before)
                dz = g - jnp.exp2(ll) * (g + pre)
                if d is not None:
                    dz = jnp.where(strict[d], dz, 0.0)
                dzb = dz.astype(BF)
                dks.append(_dot_tn(dzb, q2s[h // 2]))
                dvs.append(_dot_tn(ab, dob2s[h // 2]))
                out.append((rr + rs, sg + gs, dq + _dot(dzb, kj)))
            for p, ps in enumerate(pairs):
                dk_ref[rows, ps] += jnp.where(lane_k < 64, dks[2 * p], dks[2 * p + 1]) * SB_SCALE
                dv_ref[rows, ps] += jnp.where(lane_k < 64, dvs[2 * p], dvs[2 * p + 1])
            return tuple(out)

        zero = jnp.zeros((tq, 1), F32)
        carry = ((zero, zero, jnp.zeros((tq, 128), F32)),) * 4
        for d in reversed(range(nd)):
            carry = step(nd * i + d, carry, d)
        carry = _sb_walk(nd * i, lambda t, c: step(nd * i - 1 - t, c, None), carry)
        for p, ps in enumerate(pairs):
            dq_ref[:, ps] = jnp.where(lane < 64, carry[2 * p][2], carry[2 * p + 1][2]) * SB_SCALE

    blk = lambda c: pl.BlockSpec((tq, 128), lambda i: (i, c // 128))
    full = lambda c: pl.BlockSpec((T, 128), lambda i: (0, c // 128))
    row = pl.BlockSpec((tq, 256), lambda i: (i, 0))
    acc = pl.BlockSpec((T, 256), lambda i: (0, 0))
    return pl.pallas_call(
        body, name="sb_bwd", grid=(T // tq,),
        in_specs=[blk(C_DQ), blk(C_DQ + 128), full(C_DK), full(C_DK + 128), full(C_DV), full(C_DV + 128), row, row],
        out_specs=[row, acc, acc],
        out_shape=[jax.ShapeDtypeStruct((T, 256), F32)] * 3,
        compiler_params=_params(1))(h16, h16, h16, h16, h16, h16, yd, dyd)


def _compact_c(ycp):
    return jnp.concatenate([ycp[:, h * 128:h * 128 + 64] for h in range(4)], axis=1)


def _post_fwd(ya, yb, ycp, yd, h32, ggrp, wout, gpost, x, tgt=None):
    T = x.shape[0]
    tm = 512 if T % 512 == 0 else 256
    last = tgt is not None

    def body(*refs):
        ya_ref, yb_ref, yc_ref, yd_ref, gate_ref, gg_ref, w_ref, gp_ref, x_ref = refs[:9]
        if last:
            t_ref, xn_ref, o_ref, sq_ref = refs[9:]
        else:
            xn_ref, o_ref = refs[9:]
        ys = [ya_ref[...], yb_ref[...], _compact_c(yc_ref[...]), yd_ref[...]]
        gate = gate_ref[...]
        sil = gate * (1.0 / (1.0 + jnp.exp(-gate)))
        parts = []
        for gi in range(4):
            ng, _ = _rms_fwd(ys[gi], gg_ref[:, gi * 256:(gi + 1) * 256])
            parts.append(ng * sil[:, gi * 256:(gi + 1) * 256])
        o = _dot(jnp.concatenate(parts, axis=1).astype(BF), w_ref[...])
        o_ref[...] = o
        on, _ = _rms_fwd(o, gp_ref[...])
        if last:
            @pl.when(pl.program_id(0) == 0)
            def _():
                sq_ref[...] = jnp.zeros_like(sq_ref)

            d = (x_ref[...] + on) - t_ref[...]
            sq_ref[...] += jnp.sum(d * d, axis=0, keepdims=True)
            xn_ref[...] = d * (1.0 / D_MODEL)
        else:
            xn_ref[...] = x_ref[...] + on

    row = lambda w: pl.BlockSpec((tm, w), lambda n: (n, 0))
    vec = pl.BlockSpec((1, 1024), lambda n: (0, 0))
    return pl.pallas_call(
        body, name="post_fwd", grid=(T // tm,),
        in_specs=[row(256), row(256), row(512), row(256), pl.BlockSpec((tm, 1024), lambda n: (n, C_GATE // 1024)),
                  vec, pl.BlockSpec((1024, 1024), lambda n: (0, 0)), vec, row(1024)] + ([row(1024)] if last else []),
        out_specs=[row(1024), row(1024)] + ([vec] if last else []),
        out_shape=[jax.ShapeDtypeStruct((T, 1024), F32), jax.ShapeDtypeStruct((T, 1024), F32)]
        + ([jax.ShapeDtypeStruct((1, 1024), F32)] if last else []),
        compiler_params=_params(1))(*([ya, yb, ycp, yd, h32, ggrp, wout, gpost, x] + ([tgt] if last else [])))


def _post_bwd(dx, o, gpost, woutt, ya, yb, ycp, yd, h32, ggrp):
    T = dx.shape[0]
    tm = 256

    def body(dx_ref, o_ref, gp_ref, w_ref, ya_ref, yb_ref, yc_ref, yd_ref, gate_ref, gg_ref,
             dw_ref, dya_ref, dyb_ref, dyc_ref, dyd_ref, dgate_ref, dgp_ref, dgg_ref):
        n = pl.program_id(0)

        @pl.when(n == 0)
        def _():
            dw_ref[...] = jnp.zeros_like(dw_ref)
            dgp_ref[...] = jnp.zeros_like(dgp_ref)
            dgg_ref[...] = jnp.zeros_like(dgg_ref)

        ov, gp = o_ref[...], gp_ref[...]
        _, ro = _rms_fwd(ov, gp)
        do, dgt = _rms_bwd(ov, gp, ro, dx_ref[...])
        dgp_ref[...] += jnp.sum(dgt, axis=0, keepdims=True)
        dob = do.astype(BF)
        gate = gate_ref[...]
        sg = 1.0 / (1.0 + jnp.exp(-gate))
        sil = gate * sg
        dsil = sg * (1.0 + gate * (1.0 - sg))
        ys = [ya_ref[...], yb_ref[...], _compact_c(yc_ref[...]), yd_ref[...]]
        normed = [_rms_fwd(ys[gi], gg_ref[:, gi * 256:(gi + 1) * 256]) for gi in range(4)]
        ym = jnp.concatenate([normed[gi][0] * sil[:, gi * 256:(gi + 1) * 256] for gi in range(4)], axis=1).astype(BF)
        dw_ref[...] += _dot_tn(ym, dob)
        dym = _dot(dob, w_ref[...])
        dys = []
        for gi in range(4):
            gs = slice(gi * 256, (gi + 1) * 256)
            gg = gg_ref[:, gs]
            ng, rg = normed[gi]
            dgate_ref[:, gs] = (dym[:, gs] * ng * dsil[:, gs]).astype(BF)
            dy, dgt2 = _rms_bwd(ys[gi], gg, rg, dym[:, gs] * sil[:, gs])
            dgg_ref[:, gs] += jnp.sum(dgt2, axis=0, keepdims=True)
            dys.append(dy)
        dya_ref[...] = dys[0]
        dyb_ref[...] = dys[1]
        dyd_ref[...] = dys[3]
        z64 = jnp.zeros((tm, 64), F32)
        dyc_ref[...] = jnp.concatenate(
            [piece for h in range(4) for piece in (dys[2][:, h * 64:(h + 1) * 64], z64)], axis=1)

    row = lambda w: pl.BlockSpec((tm, w), lambda n: (n, 0))
    vec = pl.BlockSpec((1, 1024), lambda n: (0, 0))
    return pl.pallas_call(
        body, name="post_bwd", grid=(T // tm,),
        in_specs=[row(1024), row(1024), vec, pl.BlockSpec((1024, 1024), lambda n: (0, 0)),
                  row(256), row(256), row(512), row(256),
                  pl.BlockSpec((tm, 1024), lambda n: (n, C_GATE // 1024)), vec],
        out_specs=[pl.BlockSpec((1024, 1024), lambda n: (0, 0)), row(256), row(256), row(512), row(256), row(1024),
                   vec, vec],
        out_shape=[jax.ShapeDtypeStruct((1024, 1024), F32), jax.ShapeDtypeStruct((T, 256), F32),
                   jax.ShapeDtypeStruct((T, 256), F32), jax.ShapeDtypeStruct((T, 512), F32),
                   jax.ShapeDtypeStruct((T, 256), F32), jax.ShapeDtypeStruct((T, 1024), BF),
                   jax.ShapeDtypeStruct((1, 1024), F32), jax.ShapeDtypeStruct((1, 1024), F32)],
        compiler_params=_params(1))(dx, o, gpost, woutt, ya, yb, ycp, yd, h32, ggrp)


def _swap_rows32(a):
    return jnp.concatenate([a[16:32], a[0:16]], axis=0)


def _pad_w_uq(w):
    z = lambda n: jnp.zeros((w.shape[0], n), w.dtype)
    a = [p for h in range(4) for p in (w[:, 96 * h:96 * h + 96], z(32))]
    b = [p for h in range(4) for p in (z(64), _swap32(w[:, 96 * h + 64:96 * h + 96]), z(32))]
    return jnp.concatenate(a + b, axis=1)


def _unpad_w_uq(d):
    out = []
    for h in range(4):
        out.append(d[:, 128 * h:128 * h + 64])
        out.append(d[:, 128 * h + 64:128 * h + 96] + _swap32(d[:, 512 + 128 * h + 64:512 + 128 * h + 96]))
    return jnp.concatenate(out, axis=1)


def _pad_w_ukv(w):
    z = jnp.zeros((w.shape[0], 64), w.dtype)
    a = [p for h in range(4) for p in (w[:, 128 * h:128 * h + 64], z)]
    b = [p for h in range(4) for p in (w[:, 128 * h + 64:128 * h + 128], z)]
    return jnp.concatenate(a + b, axis=1)


def _unpad_w_ukv(d):
    return jnp.concatenate([p for h in range(4) for p in (d[:, 128 * h:128 * h + 64],
                                                          d[:, 512 + 128 * h:512 + 128 * h + 64])], axis=1)


def _rope_tables(pos):
    freqs = 10000.0 ** (-jnp.arange(16, dtype=F32) / 16)
    ang = pos.astype(F32)[:, None] * freqs
    c, s = jnp.cos(ang), jnp.sin(ang)
    z = lambda n: jnp.zeros((pos.shape[0], n), F32)
    return (jnp.concatenate([z(64), c, c, z(32)], axis=1), jnp.concatenate([z(64), -s, s, z(32)], axis=1))


def _layer_weights(W, l):
    wuq2 = _pad_w_uq(W["mla_w_uq"][l])
    wkv2 = _pad_w_ukv(W["mla_w_ukv"][l])
    wout = W["w_out"][l]
    cw = jnp.concatenate([W["conv_w"][l].astype(F32), jnp.zeros((5, 256), F32)], axis=0)
    return dict(
        wpt=W["wpt"][l], wuq2=wuq2.astype(BF), wuq2t=wuq2.T.astype(BF),
        wkv2=wkv2.astype(BF), wkv2t=wkv2.T.astype(BF), wout=wout.astype(BF), woutt=wout.T.astype(BF),
        cw=cw, cb=W["conv_b"][l][None, :], sinks=W["attn_sinks"][l],
        gpre=W["norm_pre"][l][None, :], gq=W["mla_q_norm"][l][None, :], gkv=W["mla_kv_norm"][l][None, :],
        ggrp=W["group_norm"][l][None, :], gpost=W["norm_post"][l][None, :])


def _local_step(x, pos, W, tgt):
    cosk, sin = _rope_tables(pos)
    saved = []
    for l in range(DEPTH):
        lw = _layer_weights(W, l)
        h32, h16, xn = _inproj_fwd(x, lw["gpre"], lw["wpt"])
        ya = _swa_fwd(h16, lw["sinks"])
        yb = _conv_fwd(h32, lw["cw"], lw["cb"])
        qc, kc, vc = _cprep_fwd(h32, lw["gq"], lw["gkv"], lw["wuq2"], lw["wkv2"], cosk, sin)
        ycp, lse = _mla_fwd(qc, kc, vc)
        yd = _sb_fwd(h16)
        if l < DEPTH - 1:
            x_new, o = _post_fwd(ya, yb, ycp, yd, h32, lw["ggrp"], lw["wout"], lw["gpost"], x)
        else:
            dx, o, sq = _post_fwd(ya, yb, ycp, yd, h32, lw["ggrp"], lw["wout"], lw["gpost"], x, tgt)
        saved.append(dict(lw=lw, x=x, h32=h32, h16=h16, xn=xn, ya=ya, yb=yb, qc=qc, kc=kc, vc=vc, ycp=ycp,
                          lse=lse, yd=yd, o=o))
        if l < DEPTH - 1:
            x = x_new

    grads = {k: [None] * DEPTH for k in ("norm_pre", "w_in_pt", "attn_sinks", "conv_w", "conv_b", "mla_q_norm",
                                         "mla_w_uq", "mla_kv_norm", "mla_w_ukv", "group_norm", "w_out",
                                         "norm_post")}
    for l in reversed(range(DEPTH)):
        s = saved[l]
        lw = s["lw"]
        dwout, dya, dyb, dycp, dyd, dgate, dgpost, dggrp = _post_bwd(
            dx, s["o"], lw["gpost"], lw["woutt"], s["ya"], s["yb"], s["ycp"], s["yd"], s["h32"], lw["ggrp"])
        grads["norm_post"][l] = dgpost[0]
        grads["group_norm"][l] = dggrp[0]
        grads["w_out"][l] = dwout
        sdq, sdk, sdv = _sb_bwd(s["h16"], s["yd"], dyd)
        mdq, mdk, mdv = _mla_bwd(s["qc"], s["kc"], s["vc"], s["ycp"], s["lse"], dycp)
        dcq, dckv, dckr, dckrs, dwuq2, dwkv2, dgq, dgkv = _cprep_bwd(
            s["h32"], lw["gq"], lw["gkv"], lw["wuq2t"], lw["wkv2t"], cosk, sin, mdq, mdk, mdv)
        grads["mla_q_norm"][l] = dgq[0]
        grads["mla_kv_norm"][l] = dgkv[0]
        grads["mla_w_uq"][l] = _unpad_w_uq(dwuq2)
        grads["mla_w_ukv"][l] = _unpad_w_ukv(dwkv2)
        dbb, dbc, dbx, dcw = _conv_bwd(s["h32"], lw["cw"], lw["cb"], dyb)
        grads["conv_w"][l] = dcw[0:3]
        grads["conv_b"][l] = dcw[3]
        adq, adk, adv, dsk = _swa_bwd(s["h16"], lw["sinks"], dya)
        grads["attn_sinks"][l] = dsk[0, 0:4]
        parts = [dgate, dbb, dbc, dbx, dcq, dckv, dckr, dckrs, adq, adk, adv, sdq, sdk, sdv]
        dwp, dx, dgpre = _inproj_bwd(parts, lw["wpt"], s["x"], s["xn"], lw["gpre"], dx)
        grads["w_in_pt"][l] = dwp.T
        grads["norm_pre"][l] = dgpre[0]
    return sq, dx, grads


SMALL_SHARDED = ("conv_w", "mla_w_uq", "mla_w_ukv")
REPLICATED = ("norm_pre", "attn_sinks", "conv_b", "mla_q_norm", "mla_kv_norm", "group_norm", "norm_post")
ORDER = ("norm_pre", "w_in", "attn_sinks", "conv_w", "conv_b", "mla_q_norm", "mla_w_uq", "mla_kv_norm",
         "mla_w_ukv", "group_norm", "w_out", "norm_post")
W_IN_COLS = 436
W_IN_WIN = 440
SMALL_ROWS = 48


def _pack_small(arrs, dtype):
    flat = jnp.concatenate([a.reshape(-1).astype(dtype) for a in arrs])
    flat = jnp.concatenate([flat, jnp.zeros((SMALL_ROWS * D_MODEL - flat.shape[0],), dtype)])
    return flat.reshape(SMALL_ROWS, D_MODEL)


TAIL_ROW0 = DEPTH * W_IN_WIN


def _pack_state(ps):
    k = len(ps)
    wout = jnp.stack([p["w_out"] for p in ps]).reshape(k, DEPTH * 128, D_MODEL)
    flat = jnp.stack([jnp.concatenate([p[n].reshape(-1) for n in SMALL_SHARDED + REPLICATED]) for p in ps])
    small = jnp.pad(flat, ((0, 0), (0, SMALL_ROWS * D_MODEL - flat.shape[1]))).reshape(k, SMALL_ROWS, D_MODEL)
    return jnp.concatenate([wout, small], axis=1)


def _unpack_state(buf, p):
    k = buf.shape[0]
    out = {"w_out": buf[:, 0:DEPTH * 128].reshape(k, DEPTH, 128, D_MODEL)}
    flat = buf[:, DEPTH * 128:].reshape(k, SMALL_ROWS * D_MODEL)
    off = 0
    for n in SMALL_SHARDED + REPLICATED:
        size = int(np.prod(p[n].shape))
        out[n] = flat[:, off:off + size].reshape((k,) + p[n].shape)
        off += size
    return out


def _rows_of_w_in_t(lo, hi, padded, kr):
    segs = ((0, 512, padded, NF + C_AQ), (512, 1664, padded, C_BB), (1664, 1696, kr, 0),
            (1696, 2464, padded, NF + C_DQ), (2464, 3488, padded, C_GATE))
    out = []
    for s0, s1, src, base in segs:
        a, b = max(lo, s0), min(hi, s1)
        if a < b:
            out.append(src[base + a - s0:base + b - s0])
    return out


def _me():
    return lax.axis_index("x"), lax.axis_index("y"), lax.axis_index("c")


def _all_gather(block):
    R, C = block.shape

    def body(src_ref, out_ref, send_sems, recv_sems, local_sem):
        x, y, c = _me()
        me, sibling = (x, y, c), (x, y, 1 - c)
        chips = [(1 - x, y), (x, 1 - y), (1 - x, 1 - y)]

        def slot(px, py, pc):
            return out_ref.at[4 * px + 2 * py + pc]

        def copy(k, block, to, src=None):
            return pltpu.make_async_remote_copy(
                src_ref=slot(*block) if src is None else src, dst_ref=slot(*block), send_sem=send_sems.at[k],
                recv_sem=recv_sems.at[k], device_id=to, device_id_type=MESH)

        mine = pltpu.make_async_copy(src_ref, slot(*me), local_sem)
        mine.start()
        first = [copy(0, me, sibling, src=src_ref)]
        first += [copy(1 + j, me, (*chip, c), src=src_ref) for j, chip in enumerate(chips)]
        for cp in first:
            cp.start()
        passed = [copy(4 + j, (*chip, c), sibling) for j, chip in enumerate(chips)]
        for j, chip in enumerate(chips):
            copy(1 + j, (*chip, c), me).wait_recv()
            passed[j].start()
        copy(0, sibling, me).wait_recv()
        for j, chip in enumerate(chips):
            copy(4 + j, (*chip, 1 - c), me).wait_recv()
        for cp in first + passed:
            cp.wait_send()
        mine.wait()

    return pl.pallas_call(
        body, name="all_gather", out_shape=jax.ShapeDtypeStruct((N_DEV, R, C), block.dtype),
        in_specs=[pl.BlockSpec(memory_space=pl.ANY)], out_specs=pl.BlockSpec(memory_space=pl.ANY),
        scratch_shapes=[pltpu.SemaphoreType.DMA((N_DEV - 1,)), pltpu.SemaphoreType.DMA((N_DEV - 1,)),
                        pltpu.SemaphoreType.DMA])(block)


N_CHIP = 4


def _sibling_swap(blocks):
    _, R, C = blocks.shape

    def body(src_ref, out_ref, send_sems, recv_sems):
        x, y, c = _me()
        copies = [pltpu.make_async_remote_copy(
            src_ref=src_ref.at[2 * j + 1 - c], dst_ref=out_ref.at[j], send_sem=send_sems.at[j],
            recv_sem=recv_sems.at[j], device_id=(x, y, 1 - c), device_id_type=MESH) for j in range(N_CHIP)]
        for cp in copies:
            cp.start()
        for cp in copies:
            cp.wait()

    return pl.pallas_call(
        body, name="sibling_swap", out_shape=jax.ShapeDtypeStruct((N_CHIP, R, C), blocks.dtype),
        in_specs=[pl.BlockSpec(memory_space=pl.ANY)], out_specs=pl.BlockSpec(memory_space=pl.ANY),
        scratch_shapes=[pltpu.SemaphoreType.DMA((N_CHIP,)), pltpu.SemaphoreType.DMA((N_CHIP,))])(blocks)


def _pair_sum(a, b):
    n, R, C = a.shape
    tr = 592 if R % 592 == 0 else R

    def body(a_ref, b_ref, o_ref):
        o_ref[...] = (a_ref[...].astype(F32) + b_ref[...].astype(F32)).astype(BF)

    spec = pl.BlockSpec((1, tr, C), lambda j, r: (j, r, 0))
    return pl.pallas_call(body, name="pair_sum", grid=(n, R // tr), in_specs=[spec, spec], out_specs=spec,
                          out_shape=jax.ShapeDtypeStruct(a.shape, BF), compiler_params=_params(2))(a, b)


def _chip_exchange(sums):
    _, R, C = sums.shape

    def body(src_ref, out_ref, send_sems, recv_sems, local_sem):
        x, y, c = _me()
        here = 2 * x + y
        mine = pltpu.make_async_copy(src_ref.at[here], out_ref.at[here], local_sem)
        mine.start()
        copies = []
        for k in range(1, N_CHIP):
            px, py = x ^ (k >> 1), y ^ (k & 1)
            copies.append(pltpu.make_async_remote_copy(
                src_ref=src_ref.at[2 * px + py], dst_ref=out_ref.at[here], send_sem=send_sems.at[k - 1],
                recv_sem=recv_sems.at[k - 1], device_id=(px, py, c), device_id_type=MESH))
        for cp in copies:
            cp.start()
        for cp in copies:
            cp.wait()
        mine.wait()

    return pl.pallas_call(
        body, name="chip_exchange", out_shape=jax.ShapeDtypeStruct((N_CHIP, R, C), sums.dtype),
        in_specs=[pl.BlockSpec(memory_space=pl.ANY)], out_specs=pl.BlockSpec(memory_space=pl.ANY),
        scratch_shapes=[pltpu.SemaphoreType.DMA((N_CHIP - 1,)), pltpu.SemaphoreType.DMA((N_CHIP - 1,)),
                        pltpu.SemaphoreType.DMA])(sums)


def _adamw_update(g, w, m, v):
    m_ = ADAM_B1 * m + (1.0 - ADAM_B1) * g
    v_ = ADAM_B2 * v + (1.0 - ADAM_B2) * (g * g)
    m_hat = m_ / (1.0 - ADAM_B1 ** ADAM_STEP)
    v_hat = v_ / (1.0 - ADAM_B2 ** ADAM_STEP)
    return -ADAM_LR * (m_hat / (jnp.sqrt(v_hat) + ADAM_EPS) + ADAM_WD * w), m_, v_


def _adamw(parts, state):
    _, R, C = state.shape
    n_parts = parts.shape[0]
    tr = 16
    assert R % tr == 0 and TAIL_ROW0 % tr == 0

    def body(p_ref, s_ref, o_ref):
        g = p_ref[0].astype(F32)
        for k in range(1, n_parts):
            g = g + p_ref[k].astype(F32)
        o_ref[0] = g
        o_ref[1], o_ref[2], o_ref[3] = _adamw_update(g, s_ref[0], s_ref[1], s_ref[2])

    return pl.pallas_call(
        body, name="adamw", grid=(R // tr,),
        in_specs=[pl.BlockSpec((n_parts, tr, C), lambda n: (0, n + TAIL_ROW0 // tr, 0)),
                  pl.BlockSpec((3, tr, C), lambda n: (0, n, 0))],
        out_specs=pl.BlockSpec((4, tr, C), lambda n: (0, n, 0)), out_shape=jax.ShapeDtypeStruct((4, R, C), F32),
        compiler_params=_params(1))(parts, state)


def _adamw_w_in(parts, w, m, v, core):
    n_parts = parts.shape[0]
    tc = 256

    def body(core_ref, p_ref, w_ref, m_ref, v_ref, o_ref):
        g_t = p_ref[0].astype(F32)
        for k in range(1, n_parts):
            g_t = g_t + p_ref[k].astype(F32)
        g_t = jnp.concatenate([g_t, jnp.zeros((512 - W_IN_WIN, tc), F32)], axis=0).T
        g = jnp.where(core_ref[0] == 0, g_t[:, 0:W_IN_COLS], g_t[:, W_IN_WIN - W_IN_COLS:W_IN_WIN])
        o_ref[0, 0] = g
        o_ref[1, 0], o_ref[2, 0], o_ref[3, 0] = _adamw_update(g, w_ref[0], m_ref[0], v_ref[0])

    nat = pl.BlockSpec((1, tc, W_IN_COLS), lambda l, j: (l, j, 0))
    return pl.pallas_call(
        body, name="adamw_w_in", grid=(DEPTH, D_MODEL // tc),
        in_specs=[pl.BlockSpec(memory_space=pltpu.SMEM),
                  pl.BlockSpec((n_parts, W_IN_WIN, tc), lambda l, j: (0, l, j)), nat, nat, nat],
        out_specs=pl.BlockSpec((4, 1, tc, W_IN_COLS), lambda l, j: (0, l, j, 0)),
        out_shape=jax.ShapeDtypeStruct((4, DEPTH, D_MODEL, W_IN_COLS), F32),
        compiler_params=_params(2))(core, parts, w, m, v)


def kernel(x, positions, norm_pre, w_in, attn_sinks, conv_w, conv_b, mla_q_norm, mla_w_uq, mla_kv_norm, mla_w_ukv, group_norm, w_out, norm_post, loss_target, m_norm_pre, m_w_in, m_attn_sinks, m_conv_w, m_conv_b, m_mla_q_norm, m_mla_w_uq, m_mla_kv_norm, m_mla_w_ukv, m_group_norm, m_w_out, m_norm_post, v_norm_pre, v_w_in, v_attn_sinks, v_conv_w, v_conv_b, v_mla_q_norm, v_mla_w_uq, v_mla_kv_norm, v_mla_w_ukv, v_group_norm, v_w_out, v_norm_post):
    local = dict(norm_pre=norm_pre, w_in=w_in, attn_sinks=attn_sinks, conv_w=conv_w, conv_b=conv_b,
                 mla_q_norm=mla_q_norm, mla_w_uq=mla_w_uq, mla_kv_norm=mla_kv_norm, mla_w_ukv=mla_w_ukv,
                 group_norm=group_norm, w_out=w_out, norm_post=norm_post)
    mom = dict(norm_pre=m_norm_pre, w_in=m_w_in, attn_sinks=m_attn_sinks, conv_w=m_conv_w, conv_b=m_conv_b,
               mla_q_norm=m_mla_q_norm, mla_w_uq=m_mla_w_uq, mla_kv_norm=m_mla_kv_norm, mla_w_ukv=m_mla_w_ukv,
               group_norm=m_group_norm, w_out=m_w_out, norm_post=m_norm_post)
    vel = dict(norm_pre=v_norm_pre, w_in=v_w_in, attn_sinks=v_attn_sinks, conv_w=v_conv_w, conv_b=v_conv_b,
               mla_q_norm=v_mla_q_norm, mla_w_uq=v_mla_w_uq, mla_kv_norm=v_mla_kv_norm, mla_w_ukv=v_mla_w_ukv,
               group_norm=v_group_norm, w_out=v_w_out, norm_post=v_norm_post)

    c = lax.axis_index("c")

    tile = 16
    slot_rows = 464
    shift = 8 * lax.axis_index("y") + 4 * c
    wt = lax.dynamic_update_slice(jnp.zeros((DEPTH, slot_rows, D_MODEL), BF),
                                  jnp.transpose(w_in, (0, 2, 1)).astype(BF), (0, shift, 0))
    payload = jnp.concatenate([wt.reshape(DEPTH * slot_rows, D_MODEL),
                               w_out.astype(BF).reshape(DEPTH * 128, D_MODEL),
                               _pack_small([local[n] for n in SMALL_SHARDED], BF)], axis=0)
    gathered = _all_gather(payload)
    W = {n: local[n] for n in REPLICATED}

    def nat_rows(l, lo, hi):
        def piece(d, r0, r1):
            base = slot_rows * l - (W_IN_COLS * d) // tile * tile
            return gathered[d, base + r0:base + r1]

        out, run = [], None
        for r0 in range(lo, hi, tile):
            d0, d1 = r0 // W_IN_COLS, (r0 + tile - 1) // W_IN_COLS
            if d0 == d1 and run is not None and run[0] == d0:
                run = (d0, run[1], r0 + tile)
                continue
            if run is not None:
                out.append(piece(*run))
                run = None
            if d0 == d1:
                run = (d0, r0, r0 + tile)
            else:
                out.append(piece(d0, r0, r0 + tile) + piece(d1, r0, r0 + tile))
        if run is not None:
            out.append(piece(*run))
        return out

    z = lambda n: [jnp.zeros((n, D_MODEL), BF)]
    W["wpt"] = [jnp.concatenate(nat_rows(l, 2464, 3488) + nat_rows(l, 512, 1664) + z(64) + nat_rows(l, 1664, 1696)
                                + z(96) + nat_rows(l, 1680, 1696) + nat_rows(l, 1664, 1680) + z(32)
                                + nat_rows(l, 0, 512) + nat_rows(l, 1696, 2464), axis=0) for l in range(DEPTH)]
    wo0 = DEPTH * slot_rows
    W["w_out"] = gathered[:, wo0:wo0 + DEPTH * 128].reshape(N_DEV, DEPTH, 128, D_MODEL).transpose(1, 0, 2, 3).reshape(
        DEPTH, D_MODEL, D_MODEL)
    flat = gathered[:, wo0 + DEPTH * 128:].reshape(N_DEV, SMALL_ROWS * D_MODEL)
    off = 0
    for n in SMALL_SHARDED:
        depth, rows, width = local[n].shape
        size = depth * rows * width
        W[n] = flat[:, off:off + size].reshape(N_DEV, depth, rows, width).transpose(1, 2, 0, 3).reshape(
            depth, rows, N_DEV * width)
        off += size

    sq, grad_x, g = _local_step(x[0], positions[0], W, loss_target[0])
    loss = lax.psum(0.5 / D_MODEL * jnp.sum(sq), ("x", "y", "c"))

    cols = []
    for n in SMALL_SHARDED:
        depth, rows, width = local[n].shape
        cols.append(jnp.stack(g[n]).reshape(depth, rows, N_DEV, width).transpose(2, 0, 1, 3).reshape(N_DEV, -1))
    rep = jnp.concatenate([a.reshape(-1) for n in REPLICATED for a in g[n]])
    cols.append(jnp.broadcast_to(rep[None], (N_DEV, rep.shape[0])))
    small = jnp.concatenate(cols, axis=1)
    small = jnp.pad(small, ((0, 0), (0, SMALL_ROWS * D_MODEL - small.shape[1]))).reshape(N_DEV, SMALL_ROWS, D_MODEL)
    krs = [p[C_CKR + 64:C_CKR + 96] + _swap_rows32(p[C_CKRS + 64:C_CKRS + 96]) for p in g["w_in_pt"]]
    pieces = []
    for d in range(N_DEV):
        lo = W_IN_COLS * d // 8 * 8
        for l in range(DEPTH):
            pieces += _rows_of_w_in_t(lo, lo + W_IN_WIN, g["w_in_pt"][l], krs[l])
        pieces += [g["w_out"][l][128 * d:128 * (d + 1)] for l in range(DEPTH)]
        pieces.append(small[d])
    blocks = jnp.concatenate(pieces, axis=0).astype(BF).reshape(N_DEV, -1, D_MODEL)
    mine = lax.dynamic_index_in_dim(blocks.reshape(N_CHIP, 2, -1, D_MODEL), c, axis=1, keepdims=False)
    received = _chip_exchange(_pair_sum(mine, _sibling_swap(blocks)))

    out = _unpack_state(_adamw(received, _pack_state([local, mom, vel])), local)
    out["w_in"] = _adamw_w_in(received, w_in, m_w_in, v_w_in, c.astype(jnp.int32).reshape(1))
    return (loss, grad_x[None], *[out[n][t] for t in range(4) for n in ORDER])
```
